```python
import math
import jax, jax.numpy as jnp
from jax import lax
import numpy as np

D_MODEL = 1024
BATCH = 2
SEQ = 8192
DEPTH = 1

RET_HEADS = 4
RET_QK_DIM = 64
RET_V_DIM = 128
RET_CHUNK = 128
RET_WIDTH = RET_HEADS * RET_V_DIM
ROPE_BASE = 10000.0
SWA_HEADS = 8
SWA_KV_HEADS = 2
SWA_HEAD_DIM = 64
SWA_WINDOW = 128
SWA_WIDTH = SWA_HEADS * SWA_HEAD_DIM
MIX_WIDTH = RET_WIDTH + SWA_WIDTH
IN_SIZES = (RET_HEADS * RET_QK_DIM, RET_HEADS * RET_QK_DIM, RET_WIDTH, RET_WIDTH,
            SWA_WIDTH, SWA_KV_HEADS * SWA_HEAD_DIM, SWA_KV_HEADS * SWA_HEAD_DIM)
IN_WIDTH = sum(IN_SIZES)
REL_BUCKETS = 32
REL_MAX_DIST = 128
N_EXPERTS = 256
TOP_K = 8
N_GROUPS = 8
TOPK_GROUPS = 4
EXPERT_DIM = 256
SHARED_DIM = 256
ROUTED_SCALE = 2.5
MOE_BLOCK = 128
LN_EPS = 1e-5
GN_EPS = 1e-6
DEEPNORM_ALPHA = (2 * DEPTH) ** 0.25
DEEPNORM_BETA = (8 * DEPTH) ** -0.25

kernel_name = "hybrid_retention_swa_moe_deepnorm"


def layer_norm(x, g, b):
    x32 = x.astype(jnp.float32)
    mu = jnp.mean(x32, axis=-1, keepdims=True)
    var = jnp.mean(jnp.square(x32 - mu), axis=-1, keepdims=True)
    y = (x32 - mu) * lax.rsqrt(var + LN_EPS) * g.astype(jnp.float32) + b.astype(jnp.float32)
    return y.astype(x.dtype)


def rotary(x, pos):
    half = x.shape[-1] // 2
    inv = ROPE_BASE ** (-jnp.arange(half, dtype=jnp.float32) / half)
    ang = pos.astype(jnp.float32)[:, None] * inv[None, :]
    cos = jnp.cos(ang)[None, :, None, :]
    sin = jnp.sin(ang)[None, :, None, :]
    x32 = x.astype(jnp.float32)
    x1, x2 = x32[..., :half], x32[..., half:]
    return jnp.concatenate([x1 * cos - x2 * sin, x1 * sin + x2 * cos], axis=-1).astype(x.dtype)


def retention_chunkwise(q, k, v):
    B, S, H, dk = q.shape
    dv = v.shape[-1]
    C = RET_CHUNK
    NC = S // C
    log_gamma = jnp.log(1.0 - 2.0 ** (-5.0 - jnp.arange(H, dtype=jnp.float32)))
    qc = q.astype(jnp.float32).reshape(B, NC, C, H, dk)
    kc = k.astype(jnp.float32).reshape(B, NC, C, H, dk)
    vc = v.astype(jnp.float32).reshape(B, NC, C, H, dv)
    idx = jnp.arange(C, dtype=jnp.float32)
    diff = idx[:, None] - idx[None, :]
    intra_decay = jnp.where(diff[None] >= 0,
                            jnp.exp(jnp.maximum(diff, 0.0)[None] * log_gamma[:, None, None]), 0.0)
    zeta = jnp.exp((C - 1.0 - idx)[None, :] * log_gamma[:, None])
    xi = jnp.exp((idx + 1.0)[None, :] * log_gamma[:, None])
    chunk_decay = jnp.exp(C * log_gamma)
    scores = jnp.einsum('bnihd,bnjhd->bnhij', qc, kc) * intra_decay
    intra = jnp.einsum('bnhij,bnjhe->bnihe', scores, vc)
    kv = jnp.einsum('bnjhd,bnjhe->bnhde', kc * zeta.T[None, None, :, :, None], vc)

    def step(state, kv_n):
        return state * chunk_decay[None, :, None, None] + kv_n, state

    _, states = lax.scan(step, jnp.zeros((B, H, dk, dv), jnp.float32), jnp.moveaxis(kv, 1, 0))
    states = jnp.moveaxis(states, 0, 1)
    inter = jnp.einsum('bnihd,bnhde->bnihe', qc, states) * xi.T[None, None, :, :, None]
    return (intra + inter).reshape(B, S, H, dv)


def t5_causal_bucket(dist):
    n = jnp.maximum(dist, 0)
    max_exact = REL_BUCKETS // 2
    ratio = jnp.log(jnp.maximum(n, 1).astype(jnp.float32) / max_exact) / math.log(REL_MAX_DIST / max_exact)
    large = max_exact + (ratio * (REL_BUCKETS - max_exact)).astype(jnp.int32)
    large = jnp.minimum(large, REL_BUCKETS - 1)
    return jnp.where(n < max_exact, n, large)


def sliding_window_gqa(q, k, v, rel_bias, sinks):
    B, S, H, d = q.shape
    Hk = k.shape[2]
    G = H // Hk
    W = SWA_WINDOW
    NB = S // W
    qb = q.reshape(B, NB, W, Hk, G, d)
    kb = k.reshape(B, NB, W, Hk, d)
    vb = v.reshape(B, NB, W, Hk, d)
    pad = ((0, 0), (1, 0), (0, 0), (0, 0), (0, 0))
    kcat = jnp.concatenate([jnp.pad(kb, pad)[:, :-1], kb], axis=2)
    vcat = jnp.concatenate([jnp.pad(vb, pad)[:, :-1], vb], axis=2)
    logits = jnp.einsum('bnikgd,bnjkd->bnkgij', qb, kcat).astype(jnp.float32) * (d ** -0.5)
    i = jnp.arange(W)
    j = jnp.arange(2 * W)
    dist = i[:, None] + W - j[None, :]
    band = (dist >= 0) & (dist < W)
    first_ok = (jnp.arange(NB)[:, None] > 0) | (j[None, :] >= W)
    mask = band[None, :, :] & first_ok[:, None, :]
    bias = rel_bias.astype(jnp.float32)[t5_causal_bucket(dist)]
    bias = jnp.transpose(bias, (2, 0, 1)).reshape(Hk, G, W, 2 * W)
    logits = jnp.where(mask[None, :, None, None], logits + bias[None, None], -jnp.inf)
    sink = jnp.broadcast_to(sinks.astype(jnp.float32).reshape(Hk, G)[None, None, :, :, None, None],
                            logits.shape[:-1] + (1,))
    probs = jax.nn.softmax(jnp.concatenate([logits, sink], axis=-1), axis=-1)[..., :-1]
    out = jnp.einsum('bnkgij,bnjkd->bnikgd', probs.astype(v.dtype), vcat)
    return out.reshape(B, S, H * d)


def hybrid_mixer(h, w_in, w_out, rel_bias, sinks):
    B, S, _ = h.shape
    pos = jnp.arange(S)
    split_pts = np.cumsum(IN_SIZES)[:-1].tolist()
    q_r, k_r, v_r, g_r, q_s, k_s, v_s = jnp.split(h @ w_in, split_pts, axis=-1)
    q_r = rotary(q_r.reshape(B, S, RET_HEADS, RET_QK_DIM), pos)
    k_r = rotary(k_r.reshape(B, S, RET_HEADS, RET_QK_DIM), pos) * (RET_QK_DIM ** -0.5)
    ret = retention_chunkwise(q_r, k_r, v_r.reshape(B, S, RET_HEADS, RET_V_DIM))
    mu = jnp.mean(ret, axis=-1, keepdims=True)
    var = jnp.mean(jnp.square(ret - mu), axis=-1, keepdims=True)
    ret = ((ret - mu) * lax.rsqrt(var + GN_EPS)).reshape(B, S, RET_WIDTH).astype(h.dtype)
    ret = jax.nn.silu(g_r) * ret
    swa = sliding_window_gqa(q_s.reshape(B, S, SWA_HEADS, SWA_HEAD_DIM),
                             k_s.reshape(B, S, SWA_KV_HEADS, SWA_HEAD_DIM),
                             v_s.reshape(B, S, SWA_KV_HEADS, SWA_HEAD_DIM), rel_bias, sinks)
    return jnp.concatenate([ret, swa], axis=-1) @ w_out


def moe_ffn(h, w_router, router_bias, w_gate, w_up, w_down, ws_gate, ws_up, ws_down):
    B, S, D = h.shape
    T = B * S
    E = N_EXPERTS
    xf = h.reshape(T, D)
    scores = jax.nn.sigmoid((xf @ w_router).astype(jnp.float32))
    choice = scores + router_bias.astype(jnp.float32)
    grp_score = lax.top_k(choice.reshape(T, N_GROUPS, E // N_GROUPS), 2)[0].sum(-1)
    _, top_grp = lax.top_k(grp_score, TOPK_GROUPS)
    grp_mask = jnp.any(top_grp[:, :, None] == jnp.arange(N_GROUPS)[None, None, :], axis=1)
    masked = jnp.where(jnp.repeat(grp_mask, E // N_GROUPS, axis=1), choice, -jnp.inf)
    _, top_idx = lax.top_k(masked, TOP_K)
    top_w = jnp.take_along_axis(scores, top_idx, axis=1)
    top_w = top_w / jnp.sum(top_w, axis=-1, keepdims=True) * ROUTED_SCALE
    flat_e = top_idx.reshape(-1)
    flat_tok = jnp.repeat(jnp.arange(T, dtype=jnp.int32), TOP_K)
    flat_w = top_w.reshape(-1)
    counts = jnp.bincount(flat_e, length=E)
    padded = (counts + MOE_BLOCK - 1) // MOE_BLOCK * MOE_BLOCK
    pad_end = jnp.cumsum(padded)
    pad_start = pad_end - padded
    start = jnp.cumsum(counts) - counts
    order = jnp.argsort(flat_e)
    se = flat_e[order]
    dest = pad_start[se] + jnp.arange(T * TOP_K) - start[se]
    n_rows = T * TOP_K + E * MOE_BLOCK
    n_blocks = n_rows // MOE_BLOCK
    row_tok = jnp.full((n_rows,), T, jnp.int32).at[dest].set(flat_tok[order])
    row_w = jnp.zeros((n_rows,), jnp.float32).at[dest].set(flat_w[order])
    blk_e = jnp.minimum(jnp.searchsorted(pad_end, jnp.arange(n_blocks) * MOE_BLOCK, side='right'), E - 1)

    def expert_block(args):
        e, tok, wt = args
        xb = xf[jnp.minimum(tok, T - 1)]
        y = (jax.nn.silu(xb @ w_gate[e]) * (xb @ w_up[e])) @ w_down[e]
        return (y * wt[:, None]).astype(xf.dtype)

    y_rows = lax.map(expert_block, (blk_e, row_tok.reshape(n_blocks, MOE_BLOCK),
                                    row_w.reshape(n_blocks, MOE_BLOCK)))
    routed = jax.ops.segment_sum(y_rows.reshape(n_rows, D), row_tok, num_segments=T)
    shared = (jax.nn.silu(xf @ ws_gate) * (xf @ ws_up)) @ ws_down
    return (routed + shared).reshape(B, S, D)


def setup_inputs(seed: int = 0) -> dict:
    key = jax.random.key(seed)
    ks = jax.random.split(key, 24)
    f32 = jnp.float32

    def nrm(k, shape, scale):
        return jax.random.normal(k, shape, f32) * scale

    beta = DEEPNORM_BETA
    col_scale = jnp.concatenate([jnp.full((n,), beta if c in (2, 6) else 1.0, f32)
                                 for c, n in enumerate(IN_SIZES)])
    return {
        "x": nrm(ks[0], (BATCH, SEQ, D_MODEL), 1.0),
        "ln_in_g": 1.0 + nrm(ks[1], (D_MODEL,), 0.02),
        "ln_in_b": nrm(ks[2], (D_MODEL,), 0.02),
        "w_in": nrm(ks[3], (DEPTH, D_MODEL, IN_WIDTH), D_MODEL ** -0.5) * col_scale,
        "w_out": nrm(ks[4], (DEPTH, MIX_WIDTH, D_MODEL), MIX_WIDTH ** -0.5 * beta),
        "rel_bias": nrm(ks[5], (REL_BUCKETS, SWA_HEADS), 0.5),
        "attn_sinks": nrm(ks[6], (DEPTH, SWA_HEADS), 1.0),
        "ln_mix_g": 1.0 + nrm(ks[7], (DEPTH, D_MODEL), 0.02),
        "ln_mix_b": nrm(ks[8], (DEPTH, D_MODEL), 0.02),
        "w_router": nrm(ks[9], (DEPTH, D_MODEL, N_EXPERTS), D_MODEL ** -0.5),
        "router_bias": nrm(ks[10], (DEPTH, N_EXPERTS), 0.01),
        "w_gate": nrm(ks[11], (DEPTH, N_EXPERTS, D_MODEL, EXPERT_DIM), D_MODEL ** -0.5 * beta),
        "w_up": nrm(ks[12], (DEPTH, N_EXPERTS, D_MODEL, EXPERT_DIM), D_MODEL ** -0.5 * beta),
        "w_down": nrm(ks[13], (DEPTH, N_EXPERTS, EXPERT_DIM, D_MODEL), EXPERT_DIM ** -0.5 * beta),
        "ws_gate": nrm(ks[14], (DEPTH, D_MODEL, SHARED_DIM), D_MODEL ** -0.5 * beta),
        "ws_up": nrm(ks[15], (DEPTH, D_MODEL, SHARED_DIM), D_MODEL ** -0.5 * beta),
        "ws_down": nrm(ks[16], (DEPTH, SHARED_DIM, D_MODEL), SHARED_DIM ** -0.5 * beta),
        "ln_ffn_g": 1.0 + nrm(ks[17], (DEPTH, D_MODEL), 0.02),
        "ln_ffn_b": nrm(ks[18], (DEPTH, D_MODEL), 0.02),
    }


def reference(x, ln_in_g, ln_in_b, w_in, w_out, rel_bias, attn_sinks, ln_mix_g, ln_mix_b,
              w_router, router_bias, w_gate, w_up, w_down, ws_gate, ws_up, ws_down,
              ln_ffn_g, ln_ffn_b):
    h = layer_norm(x, ln_in_g, ln_in_b)
    for l in range(DEPTH):
        mix = hybrid_mixer(h, w_in[l], w_out[l], rel_bias, attn_sinks[l])
        h = layer_norm(DEEPNORM_ALPHA * h + mix, ln_mix_g[l], ln_mix_b[l])
        ffn = moe_ffn(h, w_router[l], router_bias[l], w_gate[l], w_up[l], w_down[l],
                      ws_gate[l], ws_up[l], ws_down[l])
        h = layer_norm(DEEPNORM_ALPHA * h + ffn, ln_ffn_g[l], ln_ffn_b[l])
    return h
```

```python
import functools
import math

import jax
import jax.numpy as jnp
from jax import lax
from jax.experimental import pallas as pl
from jax.experimental.pallas import tpu as pltpu

D_MODEL = 1024
DEPTH = 1
RET_HEADS = 4
RET_QK_DIM = 64
RET_V_DIM = 128
RET_CHUNK = 128
RET_WIDTH = RET_HEADS * RET_V_DIM
ROPE_BASE = 10000.0
SWA_HEADS = 8
SWA_KV_HEADS = 2
SWA_GROUP = SWA_HEADS // SWA_KV_HEADS
SWA_HEAD_DIM = 64
SWA_WINDOW = 128
SWA_WIDTH = SWA_HEADS * SWA_HEAD_DIM
MIX_WIDTH = RET_WIDTH + SWA_WIDTH
RQK = RET_HEADS * RET_QK_DIM
SKV = SWA_KV_HEADS * SWA_HEAD_DIM
IN_SIZES = (RQK, RQK, RET_WIDTH, RET_WIDTH, SWA_WIDTH, SKV, SKV)
IN_OFFS = tuple(sum(IN_SIZES[:i]) for i in range(len(IN_SIZES)))
IN_WIDTH = sum(IN_SIZES)
REL_BUCKETS = 32
REL_MAX_DIST = 128
N_EXPERTS = 256
TOP_K = 8
N_GROUPS = 8
GROUP_SIZE = N_EXPERTS // N_GROUPS
TOPK_GROUPS = 4
EXPERT_DIM = 256
SHARED_DIM = 256
ROUTED_SCALE = 2.5
LN_EPS = 1e-5
GN_EPS = 1e-6
DEEPNORM_ALPHA = (2 * DEPTH) ** 0.25
MASK_VALUE = -1e30

VMEM_LIMIT_BYTES = 56 * 1024 * 1024

MIX_ROWS = 256
ROUTE_ROWS = 256
DISPATCH_ROWS = 256
EXPERT_ROWS = 256
COMBINE_ROWS = 128


def _layer_norm(x, g, b):
    mu = jnp.mean(x, axis=-1, keepdims=True)
    xc = x - mu
    var = jnp.mean(xc * xc, axis=-1, keepdims=True)
    return xc * lax.rsqrt(var + LN_EPS) * g + b


def _dot(a, b):
    return jnp.dot(a, b, preferred_element_type=jnp.float32)


def _dot_nt(a, b):
    return lax.dot_general(a, b, (((1,), (1,)), ((), ())), preferred_element_type=jnp.float32)


def _dot_tn(a, b):
    return lax.dot_general(a, b, (((0,), (0,)), ((), ())), preferred_element_type=jnp.float32)


def _silu(x):
    return x * (1.0 / (1.0 + jnp.exp(-x)))


def _swap_halves(x):
    n = x.shape[-1]
    half = RET_QK_DIM // 2
    lane = lax.broadcasted_iota(jnp.int32, x.shape, 1)
    from_right = pltpu.roll(x, n - half, axis=1)
    from_left = pltpu.roll(x, half, axis=1)
    return jnp.where((lane % RET_QK_DIM) < half, from_right, from_left)


def _mixer_kernel(rel_bias_ref, x_ref, g_in_ref, b_in_ref, w_in_ref, w_out_ref, rot_ref, decay_ref,
                  zeta_ref, xi_ref, cdecay_ref, bucket_ref, sink_ref, g_mix_ref, b_mix_ref,
                  h2_ref, state_ref, kprev_ref, vprev_ref, bias_ref, cat_ref):
    b_id = pl.program_id(0)
    c_id = pl.program_id(1)
    W = SWA_WINDOW

    @pl.when((b_id == 0) & (c_id == 0))
    def _build_bias():
        bucket = bucket_ref[...]
        for h in range(SWA_HEADS):
            acc = jnp.full((W, 2 * W), MASK_VALUE, jnp.float32)
            for b in range(REL_BUCKETS):
                acc = jnp.where(bucket == b, rel_bias_ref[b, h], acc)
            kh, g = divmod(h, SWA_GROUP)
            bias_ref[kh, g * W:(g + 1) * W, :] = acc

    @pl.when(c_id == 0)
    def _reset():
        state_ref[...] = jnp.zeros_like(state_ref)
        kprev_ref[...] = jnp.zeros_like(kprev_ref)
        vprev_ref[...] = jnp.zeros_like(vprev_ref)

    h = _layer_norm(x_ref[...], g_in_ref[...], b_in_ref[...])
    proj = _dot(h.astype(jnp.bfloat16), w_in_ref[...])

    o_q, o_k, o_v, o_g, o_sq, o_sk, o_sv = IN_OFFS
    cos_t = rot_ref[:, :RQK]
    sin_t = rot_ref[:, RQK:]
    q_all = proj[:, o_q:o_q + RQK]
    k_all = proj[:, o_k:o_k + RQK]
    q_rot = q_all * cos_t + _swap_halves(q_all) * sin_t
    k_rot = (k_all * cos_t + _swap_halves(k_all) * sin_t) * (RET_QK_DIM ** -0.5)

    n_sub = x_ref.shape[0] // RET_CHUNK
    for s in range(n_sub):
        r0 = s * RET_CHUNK
        rows = slice(r0, r0 + RET_CHUNK)
        for hh in range(RET_HEADS):
            qk = slice(hh * RET_QK_DIM, (hh + 1) * RET_QK_DIM)
            vv = slice(o_v + hh * RET_V_DIM, o_v + (hh + 1) * RET_V_DIM)
            gg = slice(o_g + hh * RET_V_DIM, o_g + (hh + 1) * RET_V_DIM)
            q = q_rot[rows, qk].astype(jnp.bfloat16)
            k32 = k_rot[rows, qk]
            v = proj[rows, vv].astype(jnp.bfloat16)
            scores = _dot_nt(q, k32.astype(jnp.bfloat16)) * decay_ref[hh]
            intra = _dot(scores.astype(jnp.bfloat16), v)
            state = state_ref[hh]
            inter = _dot(q, state.astype(jnp.bfloat16)) * xi_ref[hh]
            ret = intra + inter
            kz = (k32 * zeta_ref[hh]).astype(jnp.bfloat16)
            state_ref[hh] = state * cdecay_ref[hh] + _dot_tn(kz, v)
            mu = jnp.mean(ret, axis=-1, keepdims=True)
            rc = ret - mu
            var = jnp.mean(rc * rc, axis=-1, keepdims=True)
            normed = rc * lax.rsqrt(var + GN_EPS)
            gated = _silu(proj[rows, gg]) * normed
            cat_ref[rows, hh * RET_V_DIM:(hh + 1) * RET_V_DIM] = gated.astype(jnp.bfloat16)
        left_mask = jnp.where(c_id == 0, MASK_VALUE, 0.0) if s == 0 else 0.0
        k_cur = proj[rows, o_sk:o_sk + SKV].astype(jnp.bfloat16)
        v_cur = proj[rows, o_sv:o_sv + SKV].astype(jnp.bfloat16)
        k_prev = kprev_ref[...]
        v_prev = vprev_ref[...]
        for kh in range(SWA_KV_HEADS):
            kv = slice(kh * SWA_HEAD_DIM, (kh + 1) * SWA_HEAD_DIM)
            q4 = jnp.concatenate(
                [proj[rows, o_sq + (kh * SWA_GROUP + g) * SWA_HEAD_DIM:
                      o_sq + (kh * SWA_GROUP + g + 1) * SWA_HEAD_DIM] for g in range(SWA_GROUP)],
                axis=0) * (SWA_HEAD_DIM ** -0.5)
            kcat = jnp.concatenate([k_prev[:, kv], k_cur[:, kv]], axis=0)
            vcat = jnp.concatenate([v_prev[:, kv], v_cur[:, kv]], axis=0)
            logits = _dot_nt(q4.astype(jnp.bfloat16), kcat) + bias_ref[kh]
            col = lax.broadcasted_iota(jnp.int32, logits.shape, 1)
            logits = logits + jnp.where(col < W, left_mask, 0.0)
            sink = sink_ref[kh]
            m = jnp.maximum(jnp.max(logits, axis=-1, keepdims=True), sink)
            p = jnp.exp(logits - m)
            den = jnp.sum(p, axis=-1, keepdims=True) + jnp.exp(sink - m)
            probs = (p / den).astype(jnp.bfloat16)
            o4 = _dot(probs, vcat)
            for g in range(SWA_GROUP):
                c0 = RET_WIDTH + (kh * SWA_GROUP + g) * SWA_HEAD_DIM
                cat_ref[rows, c0:c0 + SWA_HEAD_DIM] = o4[g * W:(g + 1) * W].astype(jnp.bfloat16)
        kprev_ref[...] = k_cur
        vprev_ref[...] = v_cur

    mix = _dot(cat_ref[...], w_out_ref[...])
    h2_ref[...] = _layer_norm(DEEPNORM_ALPHA * h + mix, g_mix_ref[...], b_mix_ref[...])


def _t5_bucket(dist):
    n = jnp.maximum(dist, 0)
    max_exact = REL_BUCKETS // 2
    ratio = jnp.log(jnp.maximum(n, 1).astype(jnp.float32) / max_exact) / math.log(REL_MAX_DIST / max_exact)
    large = jnp.minimum(max_exact + (ratio * (REL_BUCKETS - max_exact)).astype(jnp.int32), REL_BUCKETS - 1)
    return jnp.where(n < max_exact, n, large)


def _mixer(x, ln_in_g, ln_in_b, w_in, w_out, rel_bias, sinks, ln_mix_g, ln_mix_b):
    B, S, D = x.shape
    R = MIX_ROWS
    C = RET_CHUNK
    W = SWA_WINDOW
    f32 = jnp.float32
    half = RET_QK_DIM // 2
    inv = ROPE_BASE ** (-jnp.arange(half, dtype=f32) / half)
    ang = jnp.arange(S, dtype=f32)[:, None] * inv[None, :]
    cos, sin = jnp.cos(ang), jnp.sin(ang)
    cos_t = jnp.tile(jnp.concatenate([cos, cos], axis=-1), (1, RET_HEADS))
    sin_t = jnp.tile(jnp.concatenate([-sin, sin], axis=-1), (1, RET_HEADS))
    rot = jnp.concatenate([cos_t, sin_t], axis=-1)
    log_gamma = jnp.log(1.0 - 2.0 ** (-5.0 - jnp.arange(RET_HEADS, dtype=f32)))
    idx = jnp.arange(C, dtype=f32)
    diff = idx[:, None] - idx[None, :]
    decay = jnp.where(diff[None] >= 0, jnp.exp(jnp.maximum(diff, 0.0)[None] * log_gamma[:, None, None]), 0.0)
    zeta = jnp.exp((C - 1.0 - idx)[None, :] * log_gamma[:, None])
    xi = jnp.exp((idx + 1.0)[None, :] * log_gamma[:, None])
    zeta_b = jnp.broadcast_to(zeta[:, :, None], (RET_HEADS, C, RET_QK_DIM))
    xi_b = jnp.broadcast_to(xi[:, :, None], (RET_HEADS, C, RET_V_DIM))
    cdecay = jnp.broadcast_to(jnp.exp(C * log_gamma)[:, None, None], (RET_HEADS, RET_QK_DIM, RET_V_DIM))
    i = jnp.arange(W)
    j = jnp.arange(2 * W)
    dist = i[:, None] + W - j[None, :]
    bucket = jnp.where((dist >= 0) & (dist < W), _t5_bucket(dist), -1).astype(jnp.int32)
    sink_col = jnp.repeat(sinks.astype(f32), W).reshape(SWA_KV_HEADS, SWA_GROUP * W, 1)

    const = lambda shape: pl.BlockSpec(shape, lambda b, c, *_: (0,) * len(shape))
    grid_spec = pltpu.PrefetchScalarGridSpec(
        num_scalar_prefetch=1,
        grid=(B, S // R),
        in_specs=[
            pl.BlockSpec((None, R, D), lambda b, c, *_: (b, c, 0)),
            const((1, D)), const((1, D)),
            const((D, IN_WIDTH)), const((MIX_WIDTH, D)),
            pl.BlockSpec((R, 2 * RQK), lambda b, c, *_: (c, 0)),
            const((RET_HEADS, C, C)), const((RET_HEADS, C, RET_QK_DIM)), const((RET_HEADS, C, RET_V_DIM)),
            const((RET_HEADS, RET_QK_DIM, RET_V_DIM)),
            const((W, 2 * W)), const((SWA_KV_HEADS, SWA_GROUP * W, 1)),
            const((1, D)), const((1, D)),
        ],
        out_specs=pl.BlockSpec((None, R, D), lambda b, c, *_: (b, c, 0)),
        scratch_shapes=[
            pltpu.VMEM((RET_HEADS, RET_QK_DIM, RET_V_DIM), f32),
            pltpu.VMEM((W, SKV), jnp.bfloat16),
            pltpu.VMEM((W, SKV), jnp.bfloat16),
            pltpu.VMEM((SWA_KV_HEADS, SWA_GROUP * W, 2 * W), f32),
            pltpu.VMEM((R, MIX_WIDTH), jnp.bfloat16),
        ],
    )
    return pl.pallas_call(
        _mixer_kernel,
        grid_spec=grid_spec,
        out_shape=jax.ShapeDtypeStruct((B, S, D), f32),
        compiler_params=pltpu.CompilerParams(
            dimension_semantics=("arbitrary", "arbitrary"), vmem_limit_bytes=VMEM_LIMIT_BYTES),
    )(rel_bias.astype(f32), x, ln_in_g.reshape(1, D), ln_in_b.reshape(1, D),
      w_in.astype(jnp.bfloat16), w_out.astype(jnp.bfloat16), rot, decay, zeta_b, xi_b, cdecay,
      bucket, sink_col, ln_mix_g.reshape(1, D), ln_mix_b.reshape(1, D))


def _router_kernel(h_ref, wr_ref, rb_ref, e_ref, w_ref, rk_ref, cnt_ref, run_ref):
    f32 = jnp.float32
    R = h_ref.shape[0]
    E = N_EXPERTS
    neg = -jnp.inf

    @pl.when(pl.program_id(0) == 0)
    def _init():
        run_ref[...] = jnp.zeros_like(run_ref)

    logits = jnp.dot(h_ref[...], wr_ref[...], precision=lax.Precision.HIGHEST, preferred_element_type=f32)
    scores = 1.0 / (1.0 + jnp.exp(-logits))
    choice = scores + rb_ref[...]
    lane = lax.broadcasted_iota(jnp.int32, (R, E), 1)
    grp = lane // GROUP_SIZE

    def first_argmax(vals):
        m = jnp.max(vals, axis=-1, keepdims=True)
        idx = jnp.min(jnp.where(vals == m, lane, E), axis=-1, keepdims=True)
        return m, idx

    gscore = []
    for g in range(N_GROUPS):
        vals = jnp.where(grp == g, choice, neg)
        m1, i1 = first_argmax(vals)
        m2 = jnp.max(jnp.where(lane == i1, neg, vals), axis=-1, keepdims=True)
        gscore.append(m1 + m2)
    keep = jnp.zeros((R, E), f32)
    for g in range(N_GROUPS):
        beaten = jnp.zeros((R, 1), f32)
        for g2 in range(N_GROUPS):
            if g2 == g:
                continue
            ahead = (gscore[g2] > gscore[g]) | (gscore[g2] == gscore[g]) if g2 < g else gscore[g2] > gscore[g]
            beaten = beaten + jnp.where(ahead, 1.0, 0.0)
        keep = jnp.where(grp == g, jnp.where(beaten < TOPK_GROUPS, 1.0, 0.0), keep)
    masked = jnp.where(keep > 0.0, choice, neg)

    idxs, wts = [], []
    for _ in range(TOP_K):
        _, idx = first_argmax(masked)
        hit = lane == idx
        idxs.append(idx)
        wts.append(jnp.sum(jnp.where(hit, scores, 0.0), axis=-1, keepdims=True))
        masked = jnp.where(hit, neg, masked)
    wsum = wts[0]
    for k in range(1, TOP_K):
        wsum = wsum + wts[k]

    picked = jnp.zeros((R, E), f32)
    for k in range(TOP_K):
        picked = jnp.where(lane == idxs[k], 1.0, picked)
    row = lax.broadcasted_iota(jnp.int32, (R, R), 0)
    colr = lax.broadcasted_iota(jnp.int32, (R, R), 1)
    tri = jnp.where(colr < row, 1.0, 0.0).astype(jnp.bfloat16)
    before = _dot(tri, picked.astype(jnp.bfloat16)) + run_ref[...]
    lane_k = lax.broadcasted_iota(jnp.int32, (R, TOP_K), 1)
    e_out = jnp.zeros((R, TOP_K), jnp.int32)
    w_out = jnp.zeros((R, TOP_K), f32)
    rk_out = jnp.zeros((R, TOP_K), jnp.int32)
    for k in range(TOP_K):
        rank_k = jnp.sum(jnp.where(lane == idxs[k], before, 0.0), axis=-1, keepdims=True)
        e_out = jnp.where(lane_k == k, idxs[k], e_out)
        w_out = jnp.where(lane_k == k, wts[k] / wsum * ROUTED_SCALE, w_out)
        rk_out = jnp.where(lane_k == k, rank_k.astype(jnp.int32), rk_out)
    e_ref[...] = e_out
    w_ref[...] = w_out
    rk_ref[...] = rk_out
    run_ref[...] = run_ref[...] + jnp.sum(picked, axis=0, keepdims=True)
    cnt_ref[...] = run_ref[...]


def _router(h2, w_router, router_bias):
    T, D = h2.shape
    R = ROUTE_ROWS
    E = N_EXPERTS
    return pl.pallas_call(
        _router_kernel,
        grid=(T // R,),
        in_specs=[
            pl.BlockSpec((R, D), lambda i: (i, 0)),
            pl.BlockSpec((D, E), lambda i: (0, 0)),
            pl.BlockSpec((1, E), lambda i: (0, 0)),
        ],
        out_specs=[
            pl.BlockSpec((R, TOP_K), lambda i: (i, 0)),
            pl.BlockSpec((R, TOP_K), lambda i: (i, 0)),
            pl.BlockSpec((R, TOP_K), lambda i: (i, 0)),
            pl.BlockSpec((1, E), lambda i: (0, 0)),
        ],
        out_shape=[
            jax.ShapeDtypeStruct((T, TOP_K), jnp.int32),
            jax.ShapeDtypeStruct((T, TOP_K), jnp.float32),
            jax.ShapeDtypeStruct((T, TOP_K), jnp.int32),
            jax.ShapeDtypeStruct((1, E), jnp.float32),
        ],
        scratch_shapes=[pltpu.VMEM((1, E), jnp.float32)],
        compiler_params=pltpu.CompilerParams(
            dimension_semantics=("arbitrary",), vmem_limit_bytes=VMEM_LIMIT_BYTES),
    )(h2, w_router, router_bias.reshape(1, E).astype(jnp.float32))


def _dispatch_kernel(row_start_ref, e_ref, rk_ref, h_ref, xs_in_ref, xs_ref, sem):
    del xs_in_ref
    R = h_ref.shape[0]

    def issue(t, carry):
        for k in range(TOP_K):
            p = t * TOP_K + k
            dest = row_start_ref[e_ref[p]] + rk_ref[p]
            pltpu.make_async_copy(h_ref.at[pl.ds(t, 1)], xs_ref.at[pl.ds(dest, 1)], sem).start()
        return carry

    lax.fori_loop(0, R, issue, 0)
    n = R * TOP_K
    pltpu.make_async_copy(xs_ref.at[pl.ds(0, n)], xs_ref.at[pl.ds(0, n)], sem).wait()


def _dispatch(h2, e_flat, rk_flat, row_start, n_rows):
    T, D = h2.shape
    R = DISPATCH_ROWS
    grid_spec = pltpu.PrefetchScalarGridSpec(
        num_scalar_prefetch=1,
        grid=(T // R,),
        in_specs=[
            pl.BlockSpec((R * TOP_K,), lambda i, *_: (i,), memory_space=pltpu.SMEM),
            pl.BlockSpec((R * TOP_K,), lambda i, *_: (i,), memory_space=pltpu.SMEM),
            pl.BlockSpec((R, D), lambda i, *_: (i, 0)),
            pl.BlockSpec(memory_space=pl.ANY),
        ],
        out_specs=pl.BlockSpec(memory_space=pl.ANY),
        scratch_shapes=[pltpu.SemaphoreType.DMA(())],
    )
    return pl.pallas_call(
        _dispatch_kernel,
        grid_spec=grid_spec,
        out_shape=jax.ShapeDtypeStruct((n_rows, D), jnp.float32),
        input_output_aliases={4: 0},
        compiler_params=pltpu.CompilerParams(
            dimension_semantics=("arbitrary",), vmem_limit_bytes=VMEM_LIMIT_BYTES),
    )(row_start, e_flat, rk_flat, h2, jnp.zeros((n_rows, D), jnp.float32))


def _experts_kernel(blk_e_ref, n_act_ref, x_ref, wg_ref, wu_ref, wd_ref, y_ref):
    del blk_e_ref

    @pl.when(pl.program_id(0) < n_act_ref[0])
    def _compute():
        bf16 = jnp.bfloat16
        x = x_ref[...].astype(bf16)
        g = _dot(x, wg_ref[...].astype(bf16))
        u = _dot(x, wu_ref[...].astype(bf16))
        a = (_silu(g) * u).astype(bf16)
        y_ref[...] = _dot(a, wd_ref[...].astype(bf16))


def _experts(xs, blk_e, n_act, w_gate, w_up, w_down):
    n_rows, D = xs.shape
    BM = EXPERT_ROWS
    F = EXPERT_DIM
    row_map = lambda i, blk_e, n_act: (jnp.minimum(i, n_act[0] - 1), 0)
    grid_spec = pltpu.PrefetchScalarGridSpec(
        num_scalar_prefetch=2,
        grid=(n_rows // BM,),
        in_specs=[
            pl.BlockSpec((BM, D), row_map),
            pl.BlockSpec((None, D, F), lambda i, blk_e, n_act: (blk_e[i], 0, 0)),
            pl.BlockSpec((None, D, F), lambda i, blk_e, n_act: (blk_e[i], 0, 0)),
            pl.BlockSpec((None, F, D), lambda i, blk_e, n_act: (blk_e[i], 0, 0)),
        ],
        out_specs=pl.BlockSpec((BM, D), row_map),
    )
    return pl.pallas_call(
        _experts_kernel,
        grid_spec=grid_spec,
        out_shape=jax.ShapeDtypeStruct((n_rows, D), jnp.float32),
        input_output_aliases={2: 0},
        compiler_params=pltpu.CompilerParams(
            dimension_semantics=("arbitrary",), vmem_limit_bytes=VMEM_LIMIT_BYTES),
    )(blk_e, n_act, xs, w_gate, w_up, w_down)


def _combine_kernel(row_start_ref, e_ref, rk_ref, h_ref, w_ref, ys_ref, wsg_ref, wsu_ref, wsd_ref,
                    g_ref, b_ref, out_ref, buf_ref, sem):
    R = h_ref.shape[0]

    def issue(t, carry):
        for k in range(TOP_K):
            p = t * TOP_K + k
            src = row_start_ref[e_ref[p]] + rk_ref[p]
            pltpu.make_async_copy(ys_ref.at[pl.ds(src, 1)], buf_ref.at[k, pl.ds(t, 1)], sem).start()
        return carry

    lax.fori_loop(0, R, issue, 0)
    h = h_ref[...]
    hb = h.astype(jnp.bfloat16)
    act = (_silu(_dot(hb, wsg_ref[...])) * _dot(hb, wsu_ref[...])).astype(jnp.bfloat16)
    ffn = _dot(act, wsd_ref[...])
    pltpu.make_async_copy(buf_ref, buf_ref, sem).wait()
    w = w_ref[...]
    for k in range(TOP_K):
        ffn = ffn + buf_ref[k] * w[:, k:k + 1]
    out_ref[...] = _layer_norm(DEEPNORM_ALPHA * h + ffn, g_ref[...], b_ref[...])


def _combine(h2, e_flat, rk_flat, top_w, row_start, ys, ws_gate, ws_up, ws_down, ln_g, ln_b):
    T, D = h2.shape
    R = COMBINE_ROWS
    F = SHARED_DIM
    bf16 = jnp.bfloat16
    const = lambda shape: pl.BlockSpec(shape, lambda i, *_: (0,) * len(shape))
    grid_spec = pltpu.PrefetchScalarGridSpec(
        num_scalar_prefetch=1,
        grid=(T // R,),
        in_specs=[
            pl.BlockSpec((R * TOP_K,), lambda i, *_: (i,), memory_space=pltpu.SMEM),
            pl.BlockSpec((R * TOP_K,), lambda i, *_: (i,), memory_space=pltpu.SMEM),
            pl.BlockSpec((R, D), lambda i, *_: (i, 0)),
            pl.BlockSpec((R, TOP_K), lambda i, *_: (i, 0)),
            pl.BlockSpec(memory_space=pl.ANY),
            const((D, F)), const((D, F)), const((F, D)), const((1, D)), const((1, D)),
        ],
        out_specs=pl.BlockSpec((R, D), lambda i, *_: (i, 0)),
        scratch_shapes=[pltpu.VMEM((TOP_K, R, D), jnp.float32), pltpu.SemaphoreType.DMA(())],
    )
    return pl.pallas_call(
        _combine_kernel,
        grid_spec=grid_spec,
        out_shape=jax.ShapeDtypeStruct((T, D), jnp.float32),
        compiler_params=pltpu.CompilerParams(
            dimension_semantics=("arbitrary",), vmem_limit_bytes=VMEM_LIMIT_BYTES),
    )(row_start, e_flat, rk_flat, h2, top_w, ys, ws_gate.astype(bf16), ws_up.astype(bf16),
      ws_down.astype(bf16), ln_g.reshape(1, D), ln_b.reshape(1, D))


def _moe(h2, w_router, router_bias, w_gate, w_up, w_down, ws_gate, ws_up, ws_down, ln_g, ln_b):
    T, D = h2.shape
    E = N_EXPERTS
    BM = EXPERT_ROWS
    e_idx, top_w, rank, counts = _router(h2, w_router, router_bias)
    cnt = counts.reshape(E).astype(jnp.int32)
    nblk = (cnt + BM - 1) // BM
    blk_end = jnp.cumsum(nblk)
    row_start = ((blk_end - nblk) * BM).astype(jnp.int32)
    n_blocks = T * TOP_K // BM + E
    n_act = blk_end[-1:].astype(jnp.int32)
    blk_ids = jnp.minimum(jnp.arange(n_blocks, dtype=jnp.int32), n_act[0] - 1)
    blk_e = jnp.minimum(jnp.searchsorted(blk_end, blk_ids, side="right"), E - 1).astype(jnp.int32)
    e_flat = e_idx.reshape(T * TOP_K)
    rk_flat = rank.reshape(T * TOP_K)
    xs = _dispatch(h2, e_flat, rk_flat, row_start, n_blocks * BM)
    ys = _experts(xs, blk_e, n_act, w_gate, w_up, w_down)
    return _combine(h2, e_flat, rk_flat, top_w, row_start, ys, ws_gate, ws_up, ws_down, ln_g, ln_b)


def kernel(x, ln_in_g, ln_in_b, w_in, w_out, rel_bias, attn_sinks, ln_mix_g, ln_mix_b, w_router,
           router_bias, w_gate, w_up, w_down, ws_gate, ws_up, ws_down, ln_ffn_g, ln_ffn_b):
    B, S, D = x.shape
    h = _mixer(x, ln_in_g, ln_in_b, w_in[0], w_out[0], rel_bias, attn_sinks[0], ln_mix_g[0], ln_mix_b[0])
    out = _moe(h.reshape(B * S, D), w_router[0], router_bias[0], w_gate[0], w_up[0], w_down[0],
               ws_gate[0], ws_up[0], ws_down[0], ln_ffn_g[0], ln_ffn_b[0])
    return out.reshape(B, S, D)
```

```python
import functools
import math

import jax
import jax.numpy as jnp
from jax import lax
from jax.experimental import pallas as pl
from jax.experimental.pallas import tpu as pltpu

D_MODEL = 1024
DEPTH = 1
RET_HEADS = 4
RET_QK_DIM = 64
RET_V_DIM = 128
RET_CHUNK = 128
RET_WIDTH = RET_HEADS * RET_V_DIM
ROPE_BASE = 10000.0
SWA_HEADS = 8
SWA_KV_HEADS = 2
SWA_GROUP = SWA_HEADS // SWA_KV_HEADS
SWA_HEAD_DIM = 64
SWA_WINDOW = 128
SWA_WIDTH = SWA_HEADS * SWA_HEAD_DIM
MIX_WIDTH = RET_WIDTH + SWA_WIDTH
RQK = RET_HEADS * RET_QK_DIM
SKV = SWA_KV_HEADS * SWA_HEAD_DIM
IN_SIZES = (RQK, RQK, RET_WIDTH, RET_WIDTH, SWA_WIDTH, SKV, SKV)
IN_OFFS = tuple(sum(IN_SIZES[:i]) for i in range(len(IN_SIZES)))
IN_WIDTH = sum(IN_SIZES)
REL_BUCKETS = 32
REL_MAX_DIST = 128
N_EXPERTS = 256
TOP_K = 8
N_GROUPS = 8
GROUP_SIZE = N_EXPERTS // N_GROUPS
TOPK_GROUPS = 4
EXPERT_DIM = 256
SHARED_DIM = 256
ROUTED_SCALE = 2.5
LN_EPS = 1e-5
GN_EPS = 1e-6
DEEPNORM_ALPHA = (2 * DEPTH) ** 0.25
MASK_VALUE = -1e30

VMEM_LIMIT_BYTES = 56 * 1024 * 1024

MIX_ROWS = 256
ROUTE_ROWS = 256
DISPATCH_ROWS = 256
EXPERT_ROWS = 256
COMBINE_ROWS = 128


def _layer_norm(x, g, b):
    mu = jnp.mean(x, axis=-1, keepdims=True)
    xc = x - mu
    var = jnp.mean(xc * xc, axis=-1, keepdims=True)
    return xc * lax.rsqrt(var + LN_EPS) * g + b


def _dot(a, b):
    return jnp.dot(a, b, preferred_element_type=jnp.float32)


def _dot_nt(a, b):
    return lax.dot_general(a, b, (((1,), (1,)), ((), ())), preferred_element_type=jnp.float32)


def _dot_tn(a, b):
    return lax.dot_general(a, b, (((0,), (0,)), ((), ())), preferred_element_type=jnp.float32)


def _silu(x):
    return x * (1.0 / (1.0 + jnp.exp(-x)))


def _swap_halves(x):
    n = x.shape[-1]
    half = RET_QK_DIM // 2
    lane = lax.broadcasted_iota(jnp.int32, x.shape, 1)
    from_right = pltpu.roll(x, n - half, axis=1)
    from_left = pltpu.roll(x, half, axis=1)
    return jnp.where((lane % RET_QK_DIM) < half, from_right, from_left)


def _mixer_kernel(rel_bias_ref, x_ref, g_in_ref, b_in_ref, w_in_ref, w_out_ref, rot_ref, decay_ref,
                  zeta_ref, xi_ref, cdecay_ref, bucket_ref, sink_ref, g_mix_ref, b_mix_ref,
                  h2_ref, state_ref, kprev_ref, vprev_ref, bias_ref, cat_ref):
    b_id = pl.program_id(0)
    c_id = pl.program_id(1)
    W = SWA_WINDOW

    @pl.when((b_id == 0) & (c_id == 0))
    def _build_bias():
        bucket = bucket_ref[...]
        for h in range(SWA_HEADS):
            acc = jnp.full((W, 2 * W), MASK_VALUE, jnp.float32)
            for b in range(REL_BUCKETS):
                acc = jnp.where(bucket == b, rel_bias_ref[b, h], acc)
            kh, g = divmod(h, SWA_GROUP)
            bias_ref[kh, g * W:(g + 1) * W, :] = acc

    @pl.when(c_id == 0)
    def _reset():
        state_ref[...] = jnp.zeros_like(state_ref)
        kprev_ref[...] = jnp.zeros_like(kprev_ref)
        vprev_ref[...] = jnp.zeros_like(vprev_ref)

    h = _layer_norm(x_ref[...], g_in_ref[...], b_in_ref[...])
    proj = _dot(h.astype(jnp.bfloat16), w_in_ref[...])

    o_q, o_k, o_v, o_g, o_sq, o_sk, o_sv = IN_OFFS
    cos_t = rot_ref[:, :RQK]
    sin_t = rot_ref[:, RQK:]
    q_all = proj[:, o_q:o_q + RQK]
    k_all = proj[:, o_k:o_k + RQK]
    q_rot = q_all * cos_t + _swap_halves(q_all) * sin_t
    k_rot = (k_all * cos_t + _swap_halves(k_all) * sin_t) * (RET_QK_DIM ** -0.5)

    n_sub = x_ref.shape[0] // RET_CHUNK
    for s in range(n_sub):
        r0 = s * RET_CHUNK
        rows = slice(r0, r0 + RET_CHUNK)
        for hh in range(RET_HEADS):
            qk = slice(hh * RET_QK_DIM, (hh + 1) * RET_QK_DIM)
            vv = slice(o_v + hh * RET_V_DIM, o_v + (hh + 1) * RET_V_DIM)
            gg = slice(o_g + hh * RET_V_DIM, o_g + (hh + 1) * RET_V_DIM)
            q = q_rot[rows, qk].astype(jnp.bfloat16)
            k32 = k_rot[rows, qk]
            v = proj[rows, vv].astype(jnp.bfloat16)
            scores = _dot_nt(q, k32.astype(jnp.bfloat16)) * decay_ref[hh]
            intra = _dot(scores.astype(jnp.bfloat16), v)
            state = state_ref[hh]
            inter = _dot(q, state.astype(jnp.bfloat16)) * xi_ref[hh]
            ret = intra + inter
            kz = (k32 * zeta_ref[hh]).astype(jnp.bfloat16)
            state_ref[hh] = state * cdecay_ref[hh] + _dot_tn(kz, v)
            mu = jnp.mean(ret, axis=-1, keepdims=True)
            rc = ret - mu
            var = jnp.mean(rc * rc, axis=-1, keepdims=True)
            normed = rc * lax.rsqrt(var + GN_EPS)
            gated = _silu(proj[rows, gg]) * normed
            cat_ref[rows, hh * RET_V_DIM:(hh + 1) * RET_V_DIM] = gated.astype(jnp.bfloat16)
        left_mask = jnp.where(c_id == 0, MASK_VALUE, 0.0) if s == 0 else 0.0
        k_cur = proj[rows, o_sk:o_sk + SKV].astype(jnp.bfloat16)
        v_cur = proj[rows, o_sv:o_sv + SKV].astype(jnp.bfloat16)
        k_prev = kprev_ref[...]
        v_prev = vprev_ref[...]
        for kh in range(SWA_KV_HEADS):
            kv = slice(kh * SWA_HEAD_DIM, (kh + 1) * SWA_HEAD_DIM)
            q4 = jnp.concatenate(
                [proj[rows, o_sq + (kh * SWA_GROUP + g) * SWA_HEAD_DIM:
                      o_sq + (kh * SWA_GROUP + g + 1) * SWA_HEAD_DIM] for g in range(SWA_GROUP)],
                axis=0) * (SWA_HEAD_DIM ** -0.5)
            kcat = jnp.concatenate([k_prev[:, kv], k_cur[:, kv]], axis=0)
            vcat = jnp.concatenate([v_prev[:, kv], v_cur[:, kv]], axis=0)
            logits = _dot_nt(q4.astype(jnp.bfloat16), kcat) + bias_ref[kh]
            col = lax.broadcasted_iota(jnp.int32, logits.shape, 1)
            logits = logits + jnp.where(col < W, left_mask, 0.0)
            sink = sink_ref[kh]
            m = jnp.maximum(jnp.max(logits, axis=-1, keepdims=True), sink)
            p = jnp.exp(logits - m)
            den = jnp.sum(p, axis=-1, keepdims=True) + jnp.exp(sink - m)
            probs = (p / den).astype(jnp.bfloat16)
            o4 = _dot(probs, vcat)
            for g in range(SWA_GROUP):
                c0 = RET_WIDTH + (kh * SWA_GROUP + g) * SWA_HEAD_DIM
                cat_ref[rows, c0:c0 + SWA_HEAD_DIM] = o4[g * W:(g + 1) * W].astype(jnp.bfloat16)
        kprev_ref[...] = k_cur
        vprev_ref[...] = v_cur

    mix = _dot(cat_ref[...], w_out_ref[...])
    h2_ref[...] = _layer_norm(DEEPNORM_ALPHA * h + mix, g_mix_ref[...], b_mix_ref[...])


def _t5_bucket(dist):
    n = jnp.maximum(dist, 0)
    max_exact = REL_BUCKETS // 2
    ratio = jnp.log(jnp.maximum(n, 1).astype(jnp.float32) / max_exact) / math.log(REL_MAX_DIST / max_exact)
    large = jnp.minimum(max_exact + (ratio * (REL_BUCKETS - max_exact)).astype(jnp.int32), REL_BUCKETS - 1)
    return jnp.where(n < max_exact, n, large)


def _mixer(x, ln_in_g, ln_in_b, w_in, w_out, rel_bias, sinks, ln_mix_g, ln_mix_b):
    B, S, D = x.shape
    R = MIX_ROWS
    C = RET_CHUNK
    W = SWA_WINDOW
    f32 = jnp.float32
    half = RET_QK_DIM // 2
    inv = ROPE_BASE ** (-jnp.arange(half, dtype=f32) / half)
    ang = jnp.arange(S, dtype=f32)[:, None] * inv[None, :]
    cos, sin = jnp.cos(ang), jnp.sin(ang)
    cos_t = jnp.tile(jnp.concatenate([cos, cos], axis=-1), (1, RET_HEADS))
    sin_t = jnp.tile(jnp.concatenate([-sin, sin], axis=-1), (1, RET_HEADS))
    rot = jnp.concatenate([cos_t, sin_t], axis=-1)
    log_gamma = jnp.log(1.0 - 2.0 ** (-5.0 - jnp.arange(RET_HEADS, dtype=f32)))
    idx = jnp.arange(C, dtype=f32)
    diff = idx[:, None] - idx[None, :]
    decay = jnp.where(diff[None] >= 0, jnp.exp(jnp.maximum(diff, 0.0)[None] * log_gamma[:, None, None]), 0.0)
    zeta = jnp.exp((C - 1.0 - idx)[None, :] * log_gamma[:, None])
    xi = jnp.exp((idx + 1.0)[None, :] * log_gamma[:, None])
    zeta_b = jnp.broadcast_to(zeta[:, :, None], (RET_HEADS, C, RET_QK_DIM))
    xi_b = jnp.broadcast_to(xi[:, :, None], (RET_HEADS, C, RET_V_DIM))
    cdecay = jnp.broadcast_to(jnp.exp(C * log_gamma)[:, None, None], (RET_HEADS, RET_QK_DIM, RET_V_DIM))
    i = jnp.arange(W)
    j = jnp.arange(2 * W)
    dist = i[:, None] + W - j[None, :]
    bucket = jnp.where((dist >= 0) & (dist < W), _t5_bucket(dist), -1).astype(jnp.int32)
    sink_col = jnp.repeat(sinks.astype(f32), W).reshape(SWA_KV_HEADS, SWA_GROUP * W, 1)

    const = lambda shape: pl.BlockSpec(shape, lambda b, c, *_: (0,) * len(shape))
    grid_spec = pltpu.PrefetchScalarGridSpec(
        num_scalar_prefetch=1,
        grid=(B, S // R),
        in_specs=[
            pl.BlockSpec((None, R, D), lambda b, c, *_: (b, c, 0)),
            const((1, D)), const((1, D)),
            const((D, IN_WIDTH)), const((MIX_WIDTH, D)),
            pl.BlockSpec((R, 2 * RQK), lambda b, c, *_: (c, 0)),
            const((RET_HEADS, C, C)), const((RET_HEADS, C, RET_QK_DIM)), const((RET_HEADS, C, RET_V_DIM)),
            const((RET_HEADS, RET_QK_DIM, RET_V_DIM)),
            const((W, 2 * W)), const((SWA_KV_HEADS, SWA_GROUP * W, 1)),
            const((1, D)), const((1, D)),
        ],
        out_specs=pl.BlockSpec((None, R, D), lambda b, c, *_: (b, c, 0)),
        scratch_shapes=[
            pltpu.VMEM((RET_HEADS, RET_QK_DIM, RET_V_DIM), f32),
            pltpu.VMEM((W, SKV), jnp.bfloat16),
            pltpu.VMEM((W, SKV), jnp.bfloat16),
            pltpu.VMEM((SWA_KV_HEADS, SWA_GROUP * W, 2 * W), f32),
            pltpu.VMEM((R, MIX_WIDTH), jnp.bfloat16),
        ],
    )
    return pl.pallas_call(
        _mixer_kernel,
        grid_spec=grid_spec,
        out_shape=jax.ShapeDtypeStruct((B, S, D), f32),
        compiler_params=pltpu.CompilerParams(
            dimension_semantics=("arbitrary", "arbitrary"), vmem_limit_bytes=VMEM_LIMIT_BYTES),
    )(rel_bias.astype(f32), x, ln_in_g.reshape(1, D), ln_in_b.reshape(1, D),
      w_in.astype(jnp.bfloat16), w_out.astype(jnp.bfloat16), rot, decay, zeta_b, xi_b, cdecay,
      bucket, sink_col, ln_mix_g.reshape(1, D), ln_mix_b.reshape(1, D))


def _router_kernel(h_ref, wr_ref, rb_ref, e_ref, w_ref, rk_ref, cnt_ref, run_ref):
    f32 = jnp.float32
    R = h_ref.shape[0]
    E = N_EXPERTS
    neg = -jnp.inf

    @pl.when(pl.program_id(0) == 0)
    def _init():
        run_ref[...] = jnp.zeros_like(run_ref)

    logits = jnp.dot(h_ref[...], wr_ref[...], precision=lax.Precision.HIGHEST, preferred_element_type=f32)
    scores = 1.0 / (1.0 + jnp.exp(-logits))
    choice = scores + rb_ref[...]
    lane = lax.broadcasted_iota(jnp.int32, (R, E), 1)
    grp = lane // GROUP_SIZE

    def first_argmax(vals):
        m = jnp.max(vals, axis=-1, keepdims=True)
        idx = jnp.min(jnp.where(vals == m, lane, E), axis=-1, keepdims=True)
        return m, idx

    gscore = []
    for g in range(N_GROUPS):
        vals = jnp.where(grp == g, choice, neg)
        m1, i1 = first_argmax(vals)
        m2 = jnp.max(jnp.where(lane == i1, neg, vals), axis=-1, keepdims=True)
        gscore.append(m1 + m2)
    keep = jnp.zeros((R, E), f32)
    for g in range(N_GROUPS):
        beaten = jnp.zeros((R, 1), f32)
        for g2 in range(N_GROUPS):
            if g2 == g:
                continue
            ahead = (gscore[g2] > gscore[g]) | (gscore[g2] == gscore[g]) if g2 < g else gscore[g2] > gscore[g]
            beaten = beaten + jnp.where(ahead, 1.0, 0.0)
        keep = jnp.where(grp == g, jnp.where(beaten < TOPK_GROUPS, 1.0, 0.0), keep)
    masked = jnp.where(keep > 0.0, choice, neg)

    idxs, wts = [], []
    for _ in range(TOP_K):
        _, idx = first_argmax(masked)
        hit = lane == idx
        idxs.append(idx)
        wts.append(jnp.sum(jnp.where(hit, scores, 0.0), axis=-1, keepdims=True))
        masked = jnp.where(hit, neg, masked)
    wsum = wts[0]
    for k in range(1, TOP_K):
        wsum = wsum + wts[k]

    picked = jnp.zeros((R, E), f32)
    for k in range(TOP_K):
        picked = jnp.where(lane == idxs[k], 1.0, picked)
    row = lax.broadcasted_iota(jnp.int32, (R, R), 0)
    colr = lax.broadcasted_iota(jnp.int32, (R, R), 1)
    tri = jnp.where(colr < row, 1.0, 0.0).astype(jnp.bfloat16)
    before = _dot(tri, picked.astype(jnp.bfloat16)) + run_ref[...]
    lane_k = lax.broadcasted_iota(jnp.int32, (R, TOP_K), 1)
    e_out = jnp.zeros((R, TOP_K), jnp.int32)
    w_out = jnp.zeros((R, TOP_K), f32)
    rk_out = jnp.zeros((R, TOP_K), jnp.int32)
    for k in range(TOP_K):
        rank_k = jnp.sum(jnp.where(lane == idxs[k], before, 0.0), axis=-1, keepdims=True)
        e_out = jnp.where(lane_k == k, idxs[k], e_out)
        w_out = jnp.where(lane_k == k, wts[k] / wsum * ROUTED_SCALE, w_out)
        rk_out = jnp.where(lane_k == k, rank_k.astype(jnp.int32), rk_out)
    e_ref[...] = e_out
    w_ref[...] = w_out
    rk_ref[...] = rk_out
    run_ref[...] = run_ref[...] + jnp.sum(picked, axis=0, keepdims=True)
    cnt_ref[...] = run_ref[...]


def _router(h2, w_router, router_bias):
    T, D = h2.shape
    R = ROUTE_ROWS
    E = N_EXPERTS
    return pl.pallas_call(
        _router_kernel,
        grid=(T // R,),
        in_specs=[
            pl.BlockSpec((R, D), lambda i: (i, 0)),
            pl.BlockSpec((D, E), lambda i: (0, 0)),
            pl.BlockSpec((1, E), lambda i: (0, 0)),
        ],
        out_specs=[
            pl.BlockSpec((R, TOP_K), lambda i: (i, 0)),
            pl.BlockSpec((R, TOP_K), lambda i: (i, 0)),
            pl.BlockSpec((R, TOP_K), lambda i: (i, 0)),
            pl.BlockSpec((1, E), lambda i: (0, 0)),
        ],
        out_shape=[
            jax.ShapeDtypeStruct((T, TOP_K), jnp.int32),
            jax.ShapeDtypeStruct((T, TOP_K), jnp.float32),
            jax.ShapeDtypeStruct((T, TOP_K), jnp.int32),
            jax.ShapeDtypeStruct((1, E), jnp.float32),
        ],
        scratch_shapes=[pltpu.VMEM((1, E), jnp.float32)],
        compiler_params=pltpu.CompilerParams(
            dimension_semantics=("arbitrary",), vmem_limit_bytes=VMEM_LIMIT_BYTES),
    )(h2, w_router, router_bias.reshape(1, E).astype(jnp.float32))


def _dispatch_kernel(row_start_ref, cnt_ref, n_act_ref, e_ref, rk_ref, h_ref, xs_ref, zero_ref, sem, zsem,
                     *, n_tokens):
    R = h_ref.shape[0]
    BM = EXPERT_ROWS
    n_blocks = xs_ref.shape[0] // BM
    n_pad_units = n_blocks - n_tokens * TOP_K // BM

    @pl.when(pl.program_id(0) == 0)
    def _zero_padding():
        zero_ref[...] = jnp.zeros_like(zero_ref)

        def expert_tail(e, carry):
            begin = row_start_ref[e] + cnt_ref[e]
            n_tail = (BM - cnt_ref[e] % BM) % BM
            n_head = jnp.minimum((8 - begin % 8) % 8, n_tail)

            def head(j, c):
                pltpu.make_async_copy(zero_ref.at[pl.ds(0, 1)], xs_ref.at[pl.ds(begin + j, 1)], zsem).start()
                return c

            lax.fori_loop(0, n_head, head, 0)
            n_rest = pl.multiple_of(n_tail - n_head, 8)

            @pl.when(n_rest > 0)
            def _():
                dst = pl.multiple_of(begin + n_head, 8)
                pltpu.make_async_copy(zero_ref.at[pl.ds(0, n_rest)], xs_ref.at[pl.ds(dst, n_rest)], zsem).start()
            return carry

        lax.fori_loop(0, N_EXPERTS, expert_tail, 0)

        def idle_block(i, carry):
            dst = pl.multiple_of(i * BM, BM)
            pltpu.make_async_copy(zero_ref, xs_ref.at[pl.ds(dst, BM)], zsem).start()
            return carry

        lax.fori_loop(n_act_ref[0], n_blocks, idle_block, 0)

        def drain(i, carry):
            pltpu.make_async_copy(zero_ref, xs_ref.at[pl.ds(0, BM)], zsem).wait()
            return carry

        lax.fori_loop(0, n_pad_units, drain, 0)

    def issue(t, carry):
        for k in range(TOP_K):
            p = t * TOP_K + k
            dest = row_start_ref[e_ref[p]] + rk_ref[p]
            pltpu.make_async_copy(h_ref.at[pl.ds(t, 1)], xs_ref.at[pl.ds(dest, 1)], sem).start(priority=k % 2)
        return carry

    lax.fori_loop(0, R, issue, 0)
    n = R * TOP_K
    pltpu.make_async_copy(xs_ref.at[pl.ds(0, n)], xs_ref.at[pl.ds(0, n)], sem).wait()


def _dispatch(h2, e_flat, rk_flat, row_start, cnt, n_act, n_rows):
    T, D = h2.shape
    R = DISPATCH_ROWS
    grid_spec = pltpu.PrefetchScalarGridSpec(
        num_scalar_prefetch=3,
        grid=(T // R,),
        in_specs=[
            pl.BlockSpec((R * TOP_K,), lambda i, *_: (i,), memory_space=pltpu.SMEM),
            pl.BlockSpec((R * TOP_K,), lambda i, *_: (i,), memory_space=pltpu.SMEM),
            pl.BlockSpec((R, D), lambda i, *_: (i, 0)),
        ],
        out_specs=pl.BlockSpec(memory_space=pl.ANY),
        scratch_shapes=[pltpu.VMEM((EXPERT_ROWS, D), jnp.float32),
                        pltpu.SemaphoreType.DMA(()), pltpu.SemaphoreType.DMA(())],
    )
    return pl.pallas_call(
        functools.partial(_dispatch_kernel, n_tokens=T),
        grid_spec=grid_spec,
        out_shape=jax.ShapeDtypeStruct((n_rows, D), jnp.float32),
        compiler_params=pltpu.CompilerParams(
            dimension_semantics=("arbitrary",), vmem_limit_bytes=VMEM_LIMIT_BYTES),
    )(row_start, cnt, n_act, e_flat, rk_flat, h2)


def _experts_kernel(blk_e_ref, n_act_ref, x_ref, wg_ref, wu_ref, wd_ref, y_ref):
    del blk_e_ref

    @pl.when(pl.program_id(0) < n_act_ref[0])
    def _compute():
        bf16 = jnp.bfloat16
        x = x_ref[...].astype(bf16)
        g = _dot(x, wg_ref[...].astype(bf16))
        u = _dot(x, wu_ref[...].astype(bf16))
        a = (_silu(g) * u).astype(bf16)
        y_ref[...] = _dot(a, wd_ref[...].astype(bf16))


def _experts(xs, blk_e, n_act, w_gate, w_up, w_down):
    n_rows, D = xs.shape
    BM = EXPERT_ROWS
    F = EXPERT_DIM
    row_map = lambda i, blk_e, n_act: (jnp.minimum(i, n_act[0] - 1), 0)
    grid_spec = pltpu.PrefetchScalarGridSpec(
        num_scalar_prefetch=2,
        grid=(n_rows // BM,),
        in_specs=[
            pl.BlockSpec((BM, D), row_map),
            pl.BlockSpec((None, D, F), lambda i, blk_e, n_act: (blk_e[i], 0, 0)),
            pl.BlockSpec((None, D, F), lambda i, blk_e, n_act: (blk_e[i], 0, 0)),
            pl.BlockSpec((None, F, D), lambda i, blk_e, n_act: (blk_e[i], 0, 0)),
        ],
        out_specs=pl.BlockSpec((BM, D), row_map),
    )
    return pl.pallas_call(
        _experts_kernel,
        grid_spec=grid_spec,
        out_shape=jax.ShapeDtypeStruct((n_rows, D), jnp.float32),
        input_output_aliases={2: 0},
        compiler_params=pltpu.CompilerParams(
            dimension_semantics=("arbitrary",), vmem_limit_bytes=VMEM_LIMIT_BYTES),
    )(blk_e, n_act, xs, w_gate, w_up, w_down)


def _combine_kernel(row_start_ref, e_ref, rk_ref, h_ref, w_ref, ys_ref, wsg_ref, wsu_ref, wsd_ref,
                    g_ref, b_ref, out_ref, buf_ref, sem):
    R = h_ref.shape[0]

    def issue(t, carry):
        for k in range(TOP_K):
            p = t * TOP_K + k
            src = row_start_ref[e_ref[p]] + rk_ref[p]
            pltpu.make_async_copy(ys_ref.at[pl.ds(src, 1)], buf_ref.at[k, pl.ds(t, 1)], sem).start(priority=k % 2)
        return carry

    lax.fori_loop(0, R, issue, 0)
    h = h_ref[...]
    hb = h.astype(jnp.bfloat16)
    act = (_silu(_dot(hb, wsg_ref[...])) * _dot(hb, wsu_ref[...])).astype(jnp.bfloat16)
    ffn = _dot(act, wsd_ref[...])
    pltpu.make_async_copy(buf_ref, buf_ref, sem).wait()
    w = w_ref[...]
    for k in range(TOP_K):
        ffn = ffn + buf_ref[k] * w[:, k:k + 1]
    out_ref[...] = _layer_norm(DEEPNORM_ALPHA * h + ffn, g_ref[...], b_ref[...])


def _combine(h2, e_flat, rk_flat, top_w, row_start, ys, ws_gate, ws_up, ws_down, ln_g, ln_b):
    T, D = h2.shape
    R = COMBINE_ROWS
    F = SHARED_DIM
    bf16 = jnp.bfloat16
    const = lambda shape: pl.BlockSpec(shape, lambda i, *_: (0,) * len(shape))
    grid_spec = pltpu.PrefetchScalarGridSpec(
        num_scalar_prefetch=1,
        grid=(T // R,),
        in_specs=[
            pl.BlockSpec((R * TOP_K,), lambda i, *_: (i,), memory_space=pltpu.SMEM),
            pl.BlockSpec((R * TOP_K,), lambda i, *_: (i,), memory_space=pltpu.SMEM),
            pl.BlockSpec((R, D), lambda i, *_: (i, 0)),
            pl.BlockSpec((R, TOP_K), lambda i, *_: (i, 0)),
            pl.BlockSpec(memory_space=pl.ANY),
            const((D, F)), const((D, F)), const((F, D)), const((1, D)), const((1, D)),
        ],
        out_specs=pl.BlockSpec((R, D), lambda i, *_: (i, 0)),
        scratch_shapes=[pltpu.VMEM((TOP_K, R, D), jnp.float32), pltpu.SemaphoreType.DMA(())],
    )
    return pl.pallas_call(
        _combine_kernel,
        grid_spec=grid_spec,
        out_shape=jax.ShapeDtypeStruct((T, D), jnp.float32),
        compiler_params=pltpu.CompilerParams(
            dimension_semantics=("arbitrary",), vmem_limit_bytes=VMEM_LIMIT_BYTES),
    )(row_start, e_flat, rk_flat, h2, top_w, ys, ws_gate.astype(bf16), ws_up.astype(bf16),
      ws_down.astype(bf16), ln_g.reshape(1, D), ln_b.reshape(1, D))


def _moe(h2, w_router, router_bias, w_gate, w_up, w_down, ws_gate, ws_up, ws_down, ln_g, ln_b):
    T, D = h2.shape
    E = N_EXPERTS
    BM = EXPERT_ROWS
    e_idx, top_w, rank, counts = _router(h2, w_router, router_bias)
    cnt = counts.reshape(E).astype(jnp.int32)
    nblk = (cnt + BM - 1) // BM
    blk_end = jnp.cumsum(nblk)
    row_start = ((blk_end - nblk) * BM).astype(jnp.int32)
    n_blocks = T * TOP_K // BM + E
    n_act = blk_end[-1:].astype(jnp.int32)
    blk_ids = jnp.minimum(jnp.arange(n_blocks, dtype=jnp.int32), n_act[0] - 1)
    blk_e = jnp.minimum(jnp.searchsorted(blk_end, blk_ids, side="right"), E - 1).astype(jnp.int32)
    e_flat = e_idx.reshape(T * TOP_K)
    rk_flat = rank.reshape(T * TOP_K)
    xs = _dispatch(h2, e_flat, rk_flat, row_start, cnt, n_act, n_blocks * BM)
    ys = _experts(xs, blk_e, n_act, w_gate, w_up, w_down)
    return _combine(h2, e_flat, rk_flat, top_w, row_start, ys, ws_gate, ws_up, ws_down, ln_g, ln_b)


def kernel(x, ln_in_g, ln_in_b, w_in, w_out, rel_bias, attn_sinks, ln_mix_g, ln_mix_b, w_router,
           router_bias, w_gate, w_up, w_down, ws_gate, ws_up, ws_down, ln_ffn_g, ln_ffn_b):
    B, S, D = x.shape
    h = _mixer(x, ln_in_g, ln_in_b, w_in[0], w_out[0], rel_bias, attn_sinks[0], ln_mix_g[0], ln_mix_b[0])
    out = _moe(h.reshape(B * S, D), w_router[0], router_bias[0], w_gate[0], w_up[0], w_down[0],
               ws_gate[0], ws_up[0], ws_down[0], ln_ffn_g[0], ln_ffn_b[0])
    return out.reshape(B, S, D)
```

```python
import functools
import math

import jax
import jax.numpy as jnp
from jax import lax
from jax.experimental import pallas as pl
from jax.experimental.pallas import tpu as pltpu

D_MODEL = 1024
DEPTH = 1
RET_HEADS = 4
RET_QK_DIM = 64
RET_V_DIM = 128
RET_CHUNK = 128
RET_WIDTH = RET_HEADS * RET_V_DIM
ROPE_BASE = 10000.0
SWA_HEADS = 8
SWA_KV_HEADS = 2
SWA_GROUP = SWA_HEADS // SWA_KV_HEADS
SWA_HEAD_DIM = 64
SWA_WINDOW = 128
SWA_WIDTH = SWA_HEADS * SWA_HEAD_DIM
MIX_WIDTH = RET_WIDTH + SWA_WIDTH
RQK = RET_HEADS * RET_QK_DIM
SKV = SWA_KV_HEADS * SWA_HEAD_DIM
IN_SIZES = (RQK, RQK, RET_WIDTH, RET_WIDTH, SWA_WIDTH, SKV, SKV)
IN_OFFS = tuple(sum(IN_SIZES[:i]) for i in range(len(IN_SIZES)))
IN_WIDTH = sum(IN_SIZES)
REL_BUCKETS = 32
REL_MAX_DIST = 128
N_EXPERTS = 256
TOP_K = 8
N_GROUPS = 8
GROUP_SIZE = N_EXPERTS // N_GROUPS
TOPK_GROUPS = 4
EXPERT_DIM = 256
SHARED_DIM = 256
ROUTED_SCALE = 2.5
LN_EPS = 1e-5
GN_EPS = 1e-6
DEEPNORM_ALPHA = (2 * DEPTH) ** 0.25
MASK_VALUE = -1e30

VMEM_LIMIT_BYTES = 56 * 1024 * 1024

MIX_ROWS = 256
ROUTE_ROWS = 256
DISPATCH_ROWS = 256
EXPERT_ROWS = 256
COMBINE_ROWS = 128


def _layer_norm(x, g, b):
    mu = jnp.mean(x, axis=-1, keepdims=True)
    xc = x - mu
    var = jnp.mean(xc * xc, axis=-1, keepdims=True)
    return xc * lax.rsqrt(var + LN_EPS) * g + b


def _dot(a, b):
    return jnp.dot(a, b, preferred_element_type=jnp.float32)


def _dot_nt(a, b):
    return lax.dot_general(a, b, (((1,), (1,)), ((), ())), preferred_element_type=jnp.float32)


def _dot_tn(a, b):
    return lax.dot_general(a, b, (((0,), (0,)), ((), ())), preferred_element_type=jnp.float32)


def _silu(x):
    return x * (1.0 / (1.0 + jnp.exp(-x)))


LANES = 128
ROW_TILE = D_MODEL // LANES


def _load_rows(ref, n_rows, lead=()):
    return jnp.concatenate([ref[lead + (pl.ds(s, n_rows, stride=ROW_TILE), slice(None))]
                            for s in range(ROW_TILE)], axis=1)


def _store_rows(ref, val):
    n_rows = val.shape[0]
    for s in range(ROW_TILE):
        ref[pl.ds(s, n_rows, stride=ROW_TILE), :] = val[:, s * LANES:(s + 1) * LANES]


def _row_tile(r):
    return pl.ds(pl.multiple_of(r * ROW_TILE, ROW_TILE), ROW_TILE)


def _swap_halves(x):
    n = x.shape[-1]
    half = RET_QK_DIM // 2
    lane = lax.broadcasted_iota(jnp.int32, x.shape, 1)
    from_right = pltpu.roll(x, n - half, axis=1)
    from_left = pltpu.roll(x, half, axis=1)
    return jnp.where((lane % RET_QK_DIM) < half, from_right, from_left)


def _mixer_kernel(rel_bias_ref, x_ref, g_in_ref, b_in_ref, w_in_ref, w_out_ref, rot_ref, decay_ref,
                  zeta_ref, xi_ref, cdecay_ref, bucket_ref, sink_ref, g_mix_ref, b_mix_ref,
                  h2_ref, state_ref, kprev_ref, vprev_ref, bias_ref, cat_ref):
    b_id = pl.program_id(0)
    c_id = pl.program_id(1)
    W = SWA_WINDOW

    @pl.when((b_id == 0) & (c_id == 0))
    def _build_bias():
        bucket = bucket_ref[...]
        for h in range(SWA_HEADS):
            acc = jnp.full((W, 2 * W), MASK_VALUE, jnp.float32)
            for b in range(REL_BUCKETS):
                acc = jnp.where(bucket == b, rel_bias_ref[b, h], acc)
            kh, g = divmod(h, SWA_GROUP)
            bias_ref[kh, g * W:(g + 1) * W, :] = acc

    @pl.when(c_id == 0)
    def _reset():
        state_ref[...] = jnp.zeros_like(state_ref)
        kprev_ref[...] = jnp.zeros_like(kprev_ref)
        vprev_ref[...] = jnp.zeros_like(vprev_ref)

    h = _layer_norm(x_ref[...], g_in_ref[...], b_in_ref[...])
    proj = _dot(h.astype(jnp.bfloat16), w_in_ref[...])

    o_q, o_k, o_v, o_g, o_sq, o_sk, o_sv = IN_OFFS
    cos_t = rot_ref[:, :RQK]
    sin_t = rot_ref[:, RQK:]
    q_all = proj[:, o_q:o_q + RQK]
    k_all = proj[:, o_k:o_k + RQK]
    q_rot = q_all * cos_t + _swap_halves(q_all) * sin_t
    k_rot = (k_all * cos_t + _swap_halves(k_all) * sin_t) * (RET_QK_DIM ** -0.5)

    n_sub = x_ref.shape[0] // RET_CHUNK
    for s in range(n_sub):
        r0 = s * RET_CHUNK
        rows = slice(r0, r0 + RET_CHUNK)
        for hh in range(RET_HEADS):
            qk = slice(hh * RET_QK_DIM, (hh + 1) * RET_QK_DIM)
            vv = slice(o_v + hh * RET_V_DIM, o_v + (hh + 1) * RET_V_DIM)
            gg = slice(o_g + hh * RET_V_DIM, o_g + (hh + 1) * RET_V_DIM)
            q = q_rot[rows, qk].astype(jnp.bfloat16)
            k32 = k_rot[rows, qk]
            v = proj[rows, vv].astype(jnp.bfloat16)
            scores = _dot_nt(q, k32.astype(jnp.bfloat16)) * decay_ref[hh]
            intra = _dot(scores.astype(jnp.bfloat16), v)
            state = state_ref[hh]
            inter = _dot(q, state.astype(jnp.bfloat16)) * xi_ref[hh]
            ret = intra + inter
            kz = (k32 * zeta_ref[hh]).astype(jnp.bfloat16)
            state_ref[hh] = state * cdecay_ref[hh] + _dot_tn(kz, v)
            mu = jnp.mean(ret, axis=-1, keepdims=True)
            rc = ret - mu
            var = jnp.mean(rc * rc, axis=-1, keepdims=True)
            normed = rc * lax.rsqrt(var + GN_EPS)
            gated = _silu(proj[rows, gg]) * normed
            cat_ref[rows, hh * RET_V_DIM:(hh + 1) * RET_V_DIM] = gated.astype(jnp.bfloat16)
        left_mask = jnp.where(c_id == 0, MASK_VALUE, 0.0) if s == 0 else 0.0
        k_cur = proj[rows, o_sk:o_sk + SKV].astype(jnp.bfloat16)
        v_cur = proj[rows, o_sv:o_sv + SKV].astype(jnp.bfloat16)
        k_prev = kprev_ref[...]
        v_prev = vprev_ref[...]
        for kh in range(SWA_KV_HEADS):
            kv = slice(kh * SWA_HEAD_DIM, (kh + 1) * SWA_HEAD_DIM)
            q4 = jnp.concatenate(
                [proj[rows, o_sq + (kh * SWA_GROUP + g) * SWA_HEAD_DIM:
                      o_sq + (kh * SWA_GROUP + g + 1) * SWA_HEAD_DIM] for g in range(SWA_GROUP)],
                axis=0) * (SWA_HEAD_DIM ** -0.5)
            kcat = jnp.concatenate([k_prev[:, kv], k_cur[:, kv]], axis=0)
            vcat = jnp.concatenate([v_prev[:, kv], v_cur[:, kv]], axis=0)
            logits = _dot_nt(q4.astype(jnp.bfloat16), kcat) + bias_ref[kh]
            col = lax.broadcasted_iota(jnp.int32, logits.shape, 1)
            logits = logits + jnp.where(col < W, left_mask, 0.0)
            sink = sink_ref[kh]
            m = jnp.maximum(jnp.max(logits, axis=-1, keepdims=True), sink)
            p = jnp.exp(logits - m)
            den = jnp.sum(p, axis=-1, keepdims=True) + jnp.exp(sink - m)
            probs = (p / den).astype(jnp.bfloat16)
            o4 = _dot(probs, vcat)
            for g in range(SWA_GROUP):
                c0 = RET_WIDTH + (kh * SWA_GROUP + g) * SWA_HEAD_DIM
                cat_ref[rows, c0:c0 + SWA_HEAD_DIM] = o4[g * W:(g + 1) * W].astype(jnp.bfloat16)
        kprev_ref[...] = k_cur
        vprev_ref[...] = v_cur

    mix = _dot(cat_ref[...], w_out_ref[...])
    _store_rows(h2_ref, _layer_norm(DEEPNORM_ALPHA * h + mix, g_mix_ref[...], b_mix_ref[...]))


def _t5_bucket(dist):
    n = jnp.maximum(dist, 0)
    max_exact = REL_BUCKETS // 2
    ratio = jnp.log(jnp.maximum(n, 1).astype(jnp.float32) / max_exact) / math.log(REL_MAX_DIST / max_exact)
    large = jnp.minimum(max_exact + (ratio * (REL_BUCKETS - max_exact)).astype(jnp.int32), REL_BUCKETS - 1)
    return jnp.where(n < max_exact, n, large)


def _mixer(x, ln_in_g, ln_in_b, w_in, w_out, rel_bias, sinks, ln_mix_g, ln_mix_b):
    B, S, D = x.shape
    R = MIX_ROWS
    C = RET_CHUNK
    W = SWA_WINDOW
    f32 = jnp.float32
    half = RET_QK_DIM // 2
    inv = ROPE_BASE ** (-jnp.arange(half, dtype=f32) / half)
    ang = jnp.arange(S, dtype=f32)[:, None] * inv[None, :]
    cos, sin = jnp.cos(ang), jnp.sin(ang)
    cos_t = jnp.tile(jnp.concatenate([cos, cos], axis=-1), (1, RET_HEADS))
    sin_t = jnp.tile(jnp.concatenate([-sin, sin], axis=-1), (1, RET_HEADS))
    rot = jnp.concatenate([cos_t, sin_t], axis=-1)
    log_gamma = jnp.log(1.0 - 2.0 ** (-5.0 - jnp.arange(RET_HEADS, dtype=f32)))
    idx = jnp.arange(C, dtype=f32)
    diff = idx[:, None] - idx[None, :]
    decay = jnp.where(diff[None] >= 0, jnp.exp(jnp.maximum(diff, 0.0)[None] * log_gamma[:, None, None]), 0.0)
    zeta = jnp.exp((C - 1.0 - idx)[None, :] * log_gamma[:, None])
    xi = jnp.exp((idx + 1.0)[None, :] * log_gamma[:, None])
    zeta_b = jnp.broadcast_to(zeta[:, :, None], (RET_HEADS, C, RET_QK_DIM))
    xi_b = jnp.broadcast_to(xi[:, :, None], (RET_HEADS, C, RET_V_DIM))
    cdecay = jnp.broadcast_to(jnp.exp(C * log_gamma)[:, None, None], (RET_HEADS, RET_QK_DIM, RET_V_DIM))
    i = jnp.arange(W)
    j = jnp.arange(2 * W)
    dist = i[:, None] + W - j[None, :]
    bucket = jnp.where((dist >= 0) & (dist < W), _t5_bucket(dist), -1).astype(jnp.int32)
    sink_col = jnp.repeat(sinks.astype(f32), W).reshape(SWA_KV_HEADS, SWA_GROUP * W, 1)

    const = lambda shape: pl.BlockSpec(shape, lambda b, c, *_: (0,) * len(shape))
    grid_spec = pltpu.PrefetchScalarGridSpec(
        num_scalar_prefetch=1,
        grid=(B, S // R),
        in_specs=[
            pl.BlockSpec((None, R, D), lambda b, c, *_: (b, c, 0)),
            const((1, D)), const((1, D)),
            const((D, IN_WIDTH)), const((MIX_WIDTH, D)),
            pl.BlockSpec((R, 2 * RQK), lambda b, c, *_: (c, 0)),
            const((RET_HEADS, C, C)), const((RET_HEADS, C, RET_QK_DIM)), const((RET_HEADS, C, RET_V_DIM)),
            const((RET_HEADS, RET_QK_DIM, RET_V_DIM)),
            const((W, 2 * W)), const((SWA_KV_HEADS, SWA_GROUP * W, 1)),
            const((1, D)), const((1, D)),
        ],
        out_specs=pl.BlockSpec((R * ROW_TILE, LANES), lambda b, c, *_: (b * (S // R) + c, 0)),
        scratch_shapes=[
            pltpu.VMEM((RET_HEADS, RET_QK_DIM, RET_V_DIM), f32),
            pltpu.VMEM((W, SKV), jnp.bfloat16),
            pltpu.VMEM((W, SKV), jnp.bfloat16),
            pltpu.VMEM((SWA_KV_HEADS, SWA_GROUP * W, 2 * W), f32),
            pltpu.VMEM((R, MIX_WIDTH), jnp.bfloat16),
        ],
    )
    return pl.pallas_call(
        _mixer_kernel,
        grid_spec=grid_spec,
        out_shape=jax.ShapeDtypeStruct((B * S * ROW_TILE, LANES), f32),
        compiler_params=pltpu.CompilerParams(
            dimension_semantics=("arbitrary", "arbitrary"), vmem_limit_bytes=VMEM_LIMIT_BYTES),
    )(rel_bias.astype(f32), x, ln_in_g.reshape(1, D), ln_in_b.reshape(1, D),
      w_in.astype(jnp.bfloat16), w_out.astype(jnp.bfloat16), rot, decay, zeta_b, xi_b, cdecay,
      bucket, sink_col, ln_mix_g.reshape(1, D), ln_mix_b.reshape(1, D))


def _router_kernel(h_ref, wr_ref, rb_ref, e_ref, w_ref, rk_ref, cnt_ref, run_ref):
    f32 = jnp.float32
    R = h_ref.shape[0] // ROW_TILE
    E = N_EXPERTS
    neg = -jnp.inf

    @pl.when(pl.program_id(0) == 0)
    def _init():
        run_ref[...] = jnp.zeros_like(run_ref)

    logits = jnp.dot(_load_rows(h_ref, R), wr_ref[...], precision=lax.Precision.HIGHEST, preferred_element_type=f32)
    scores = 1.0 / (1.0 + jnp.exp(-logits))
    choice = scores + rb_ref[...]
    lane = lax.broadcasted_iota(jnp.int32, (R, E), 1)
    grp = lane // GROUP_SIZE

    def first_argmax(vals):
        m = jnp.max(vals, axis=-1, keepdims=True)
        idx = jnp.min(jnp.where(vals == m, lane, E), axis=-1, keepdims=True)
        return m, idx

    gscore = []
    for g in range(N_GROUPS):
        vals = jnp.where(grp == g, choice, neg)
        m1, i1 = first_argmax(vals)
        m2 = jnp.max(jnp.where(lane == i1, neg, vals), axis=-1, keepdims=True)
        gscore.append(m1 + m2)
    keep = jnp.zeros((R, E), f32)
    for g in range(N_GROUPS):
        beaten = jnp.zeros((R, 1), f32)
        for g2 in range(N_GROUPS):
            if g2 == g:
                continue
            ahead = (gscore[g2] > gscore[g]) | (gscore[g2] == gscore[g]) if g2 < g else gscore[g2] > gscore[g]
            beaten = beaten + jnp.where(ahead, 1.0, 0.0)
        keep = jnp.where(grp == g, jnp.where(beaten < TOPK_GROUPS, 1.0, 0.0), keep)
    masked = jnp.where(keep > 0.0, choice, neg)

    idxs, wts = [], []
    for _ in range(TOP_K):
        _, idx = first_argmax(masked)
        hit = lane == idx
        idxs.append(idx)
        wts.append(jnp.sum(jnp.where(hit, scores, 0.0), axis=-1, keepdims=True))
        masked = jnp.where(hit, neg, masked)
    wsum = wts[0]
    for k in range(1, TOP_K):
        wsum = wsum + wts[k]

    picked = jnp.zeros((R, E), f32)
    for k in range(TOP_K):
        picked = jnp.where(lane == idxs[k], 1.0, picked)
    row = lax.broadcasted_iota(jnp.int32, (R, R), 0)
    colr = lax.broadcasted_iota(jnp.int32, (R, R), 1)
    tri = jnp.where(colr < row, 1.0, 0.0).astype(jnp.bfloat16)
    before = _dot(tri, picked.astype(jnp.bfloat16)) + run_ref[...]
    lane_k = lax.broadcasted_iota(jnp.int32, (R, TOP_K), 1)
    e_out = jnp.zeros((R, TOP_K), jnp.int32)
    w_out = jnp.zeros((R, TOP_K), f32)
    rk_out = jnp.zeros((R, TOP_K), jnp.int32)
    for k in range(TOP_K):
        rank_k = jnp.sum(jnp.where(lane == idxs[k], before, 0.0), axis=-1, keepdims=True)
        e_out = jnp.where(lane_k == k, idxs[k], e_out)
        w_out = jnp.where(lane_k == k, wts[k] / wsum * ROUTED_SCALE, w_out)
        rk_out = jnp.where(lane_k == k, rank_k.astype(jnp.int32), rk_out)
    e_ref[...] = e_out
    w_ref[...] = w_out
    rk_ref[...] = rk_out
    run_ref[...] = run_ref[...] + jnp.sum(picked, axis=0, keepdims=True)
    cnt_ref[...] = run_ref[...]


def _router(h2, w_router, router_bias):
    T, D = h2.shape[0] // ROW_TILE, D_MODEL
    R = ROUTE_ROWS
    E = N_EXPERTS
    return pl.pallas_call(
        _router_kernel,
        grid=(T // R,),
        in_specs=[
            pl.BlockSpec((R * ROW_TILE, LANES), lambda i: (i, 0)),
            pl.BlockSpec((D, E), lambda i: (0, 0)),
            pl.BlockSpec((1, E), lambda i: (0, 0)),
        ],
        out_specs=[
            pl.BlockSpec((R, TOP_K), lambda i: (i, 0)),
            pl.BlockSpec((R, TOP_K), lambda i: (i, 0)),
            pl.BlockSpec((R, TOP_K), lambda i: (i, 0)),
            pl.BlockSpec((1, E), lambda i: (0, 0)),
        ],
        out_shape=[
            jax.ShapeDtypeStruct((T, TOP_K), jnp.int32),
            jax.ShapeDtypeStruct((T, TOP_K), jnp.float32),
            jax.ShapeDtypeStruct((T, TOP_K), jnp.int32),
            jax.ShapeDtypeStruct((1, E), jnp.float32),
        ],
        scratch_shapes=[pltpu.VMEM((1, E), jnp.float32)],
        compiler_params=pltpu.CompilerParams(
            dimension_semantics=("arbitrary",), vmem_limit_bytes=VMEM_LIMIT_BYTES),
    )(h2, w_router, router_bias.reshape(1, E).astype(jnp.float32))


def _positions_kernel(row_start_ref, e_ref, rk_ref, pos_ref):
    e = e_ref[...]
    rk = rk_ref[...]

    def per_expert(i, pos):
        return jnp.where(e == i, rk + row_start_ref[i], pos)

    pos_ref[...] = lax.fori_loop(0, N_EXPERTS, per_expert, jnp.zeros_like(rk))


def _positions(e_idx, rank, row_start):
    n = e_idx.size
    shape = (n // LANES, LANES)
    grid_spec = pltpu.PrefetchScalarGridSpec(
        num_scalar_prefetch=1,
        grid=(1,),
        in_specs=[pl.BlockSpec(shape, lambda i, *_: (0, 0)), pl.BlockSpec(shape, lambda i, *_: (0, 0))],
        out_specs=pl.BlockSpec(shape, lambda i, *_: (0, 0)),
    )
    pos = pl.pallas_call(
        _positions_kernel,
        grid_spec=grid_spec,
        out_shape=jax.ShapeDtypeStruct(shape, jnp.int32),
    )(row_start, e_idx.reshape(shape), rank.reshape(shape))
    return pos.reshape(n)


def _dispatch_kernel(row_start_ref, cnt_ref, n_act_ref, pos_ref, h_ref, xs_ref, zero_ref, sem, zsem,
                     *, n_tokens):
    R = h_ref.shape[0] // ROW_TILE
    BM = EXPERT_ROWS
    n_blocks = xs_ref.shape[0] // (BM * ROW_TILE)
    n_pad_units = n_blocks - n_tokens * TOP_K // BM

    @pl.when(pl.program_id(0) == 0)
    def _zero_padding():
        zero_ref[...] = jnp.zeros_like(zero_ref)

        def expert_tail(e, carry):
            n_tail = pl.multiple_of(((BM - cnt_ref[e] % BM) % BM) * ROW_TILE, ROW_TILE)

            @pl.when(n_tail > 0)
            def _():
                dst = pl.multiple_of((row_start_ref[e] + cnt_ref[e]) * ROW_TILE, ROW_TILE)
                pltpu.make_async_copy(zero_ref.at[pl.ds(0, n_tail)], xs_ref.at[pl.ds(dst, n_tail)], zsem).start()
            return carry

        lax.fori_loop(0, N_EXPERTS, expert_tail, 0)

        def idle_block(i, carry):
            dst = pl.multiple_of(i * (BM * ROW_TILE), BM * ROW_TILE)
            pltpu.make_async_copy(zero_ref, xs_ref.at[pl.ds(dst, BM * ROW_TILE)], zsem).start()
            return carry

        lax.fori_loop(n_act_ref[0], n_blocks, idle_block, 0)

        def drain(i, carry):
            pltpu.make_async_copy(zero_ref, xs_ref.at[pl.ds(0, BM * ROW_TILE)], zsem).wait()
            return carry

        lax.fori_loop(0, n_pad_units, drain, 0)

    def issue(t, carry):
        src = h_ref.at[_row_tile(t)]
        for k in range(TOP_K):
            dest = pos_ref[t * TOP_K + k]
            pltpu.make_async_copy(src, xs_ref.at[_row_tile(dest)], sem).start(priority=k % 2)
        return carry

    lax.fori_loop(0, R, issue, 0)
    n = R * TOP_K * ROW_TILE
    pltpu.make_async_copy(xs_ref.at[pl.ds(0, n)], xs_ref.at[pl.ds(0, n)], sem).wait()


def _dispatch(h2, pos, row_start, cnt, n_act, n_rows):
    T = h2.shape[0] // ROW_TILE
    R = DISPATCH_ROWS
    grid_spec = pltpu.PrefetchScalarGridSpec(
        num_scalar_prefetch=3,
        grid=(T // R,),
        in_specs=[
            pl.BlockSpec((R * TOP_K,), lambda i, *_: (i,), memory_space=pltpu.SMEM),
            pl.BlockSpec((R * ROW_TILE, LANES), lambda i, *_: (i, 0)),
        ],
        out_specs=pl.BlockSpec(memory_space=pl.ANY),
        scratch_shapes=[pltpu.VMEM((EXPERT_ROWS * ROW_TILE, LANES), jnp.float32),
                        pltpu.SemaphoreType.DMA(()), pltpu.SemaphoreType.DMA(())],
    )
    return pl.pallas_call(
        functools.partial(_dispatch_kernel, n_tokens=T),
        grid_spec=grid_spec,
        out_shape=jax.ShapeDtypeStruct((n_rows * ROW_TILE, LANES), jnp.float32),
        compiler_params=pltpu.CompilerParams(
            dimension_semantics=("arbitrary",), vmem_limit_bytes=VMEM_LIMIT_BYTES),
    )(row_start, cnt, n_act, pos, h2)


def _experts_kernel(blk_e_ref, n_act_ref, x_ref, wg_ref, wu_ref, wd_ref, y_ref):
    del blk_e_ref

    @pl.when(pl.program_id(0) < n_act_ref[0])
    def _compute():
        bf16 = jnp.bfloat16
        x = _load_rows(x_ref, EXPERT_ROWS).astype(bf16)
        g = _dot(x, wg_ref[...].astype(bf16))
        u = _dot(x, wu_ref[...].astype(bf16))
        a = (_silu(g) * u).astype(bf16)
        _store_rows(y_ref, _dot(a, wd_ref[...].astype(bf16)))


def _experts(xs, blk_e, n_act, w_gate, w_up, w_down):
    D = D_MODEL
    BM = EXPERT_ROWS
    F = EXPERT_DIM
    row_map = lambda i, blk_e, n_act: (jnp.minimum(i, n_act[0] - 1), 0)
    grid_spec = pltpu.PrefetchScalarGridSpec(
        num_scalar_prefetch=2,
        grid=(xs.shape[0] // (BM * ROW_TILE),),
        in_specs=[
            pl.BlockSpec((BM * ROW_TILE, LANES), row_map),
            pl.BlockSpec((None, D, F), lambda i, blk_e, n_act: (blk_e[i], 0, 0)),
            pl.BlockSpec((None, D, F), lambda i, blk_e, n_act: (blk_e[i], 0, 0)),
            pl.BlockSpec((None, F, D), lambda i, blk_e, n_act: (blk_e[i], 0, 0)),
        ],
        out_specs=pl.BlockSpec((BM * ROW_TILE, LANES), row_map),
    )
    return pl.pallas_call(
        _experts_kernel,
        grid_spec=grid_spec,
        out_shape=jax.ShapeDtypeStruct(xs.shape, jnp.float32),
        input_output_aliases={2: 0},
        compiler_params=pltpu.CompilerParams(
            dimension_semantics=("arbitrary",), vmem_limit_bytes=VMEM_LIMIT_BYTES),
    )(blk_e, n_act, xs, w_gate, w_up, w_down)


def _combine_kernel(pos_ref, h_ref, w_ref, ys_ref, wsg_ref, wsu_ref, wsd_ref,
                    g_ref, b_ref, out_ref, buf_ref, sem):
    R = h_ref.shape[0] // ROW_TILE

    def issue(t, carry):
        for k in range(TOP_K):
            src = pos_ref[t * TOP_K + k]
            pltpu.make_async_copy(ys_ref.at[_row_tile(src)], buf_ref.at[k, _row_tile(t)], sem).start(priority=k % 2)
        return carry

    lax.fori_loop(0, R, issue, 0)
    h = _load_rows(h_ref, R)
    hb = h.astype(jnp.bfloat16)
    act = (_silu(_dot(hb, wsg_ref[...])) * _dot(hb, wsu_ref[...])).astype(jnp.bfloat16)
    ffn = _dot(act, wsd_ref[...])
    pltpu.make_async_copy(buf_ref, buf_ref, sem).wait()
    w = w_ref[...]
    for k in range(TOP_K):
        ffn = ffn + _load_rows(buf_ref, R, lead=(k,)) * w[:, k:k + 1]
    out_ref[...] = _layer_norm(DEEPNORM_ALPHA * h + ffn, g_ref[...], b_ref[...])


def _combine(h2, pos, top_w, ys, ws_gate, ws_up, ws_down, ln_g, ln_b):
    T, D = h2.shape[0] // ROW_TILE, D_MODEL
    R = COMBINE_ROWS
    F = SHARED_DIM
    bf16 = jnp.bfloat16
    const = lambda shape: pl.BlockSpec(shape, lambda i: (0,) * len(shape))
    return pl.pallas_call(
        _combine_kernel,
        grid=(T // R,),
        in_specs=[
            pl.BlockSpec((R * TOP_K,), lambda i: (i,), memory_space=pltpu.SMEM),
            pl.BlockSpec((R * ROW_TILE, LANES), lambda i: (i, 0)),
            pl.BlockSpec((R, TOP_K), lambda i: (i, 0)),
            pl.BlockSpec(memory_space=pl.ANY),
            const((D, F)), const((D, F)), const((F, D)), const((1, D)), const((1, D)),
        ],
        out_specs=pl.BlockSpec((R, D), lambda i: (i, 0)),
        scratch_shapes=[pltpu.VMEM((TOP_K, R * ROW_TILE, LANES), jnp.float32), pltpu.SemaphoreType.DMA(())],
        out_shape=jax.ShapeDtypeStruct((T, D), jnp.float32),
        compiler_params=pltpu.CompilerParams(
            dimension_semantics=("arbitrary",), vmem_limit_bytes=VMEM_LIMIT_BYTES),
    )(pos, h2, top_w, ys, ws_gate.astype(bf16), ws_up.astype(bf16),
      ws_down.astype(bf16), ln_g.reshape(1, D), ln_b.reshape(1, D))


def _moe(h2, w_router, router_bias, w_gate, w_up, w_down, ws_gate, ws_up, ws_down, ln_g, ln_b):
    T = h2.shape[0] // ROW_TILE
    E = N_EXPERTS
    BM = EXPERT_ROWS
    e_idx, top_w, rank, counts = _router(h2, w_router, router_bias)
    cnt = counts.reshape(E).astype(jnp.int32)
    nblk = (cnt + BM - 1) // BM
    blk_end = jnp.cumsum(nblk)
    row_start = ((blk_end - nblk) * BM).astype(jnp.int32)
    n_blocks = T * TOP_K // BM + E
    n_act = blk_end[-1:].astype(jnp.int32)
    blk_ids = jnp.minimum(jnp.arange(n_blocks, dtype=jnp.int32), n_act[0] - 1)
    blk_e = jnp.minimum(jnp.searchsorted(blk_end, blk_ids, side="right"), E - 1).astype(jnp.int32)
    pos = _positions(e_idx, rank, row_start)
    xs = _dispatch(h2, pos, row_start, cnt, n_act, n_blocks * BM)
    ys = _experts(xs, blk_e, n_act, w_gate, w_up, w_down)
    return _combine(h2, pos, top_w, ys, ws_gate, ws_up, ws_down, ln_g, ln_b)


def kernel(x, ln_in_g, ln_in_b, w_in, w_out, rel_bias, attn_sinks, ln_mix_g, ln_mix_b, w_router,
           router_bias, w_gate, w_up, w_down, ws_gate, ws_up, ws_down, ln_ffn_g, ln_ffn_b):
    B, S, D = x.shape
    h = _mixer(x, ln_in_g, ln_in_b, w_in[0], w_out[0], rel_bias, attn_sinks[0], ln_mix_g[0], ln_mix_b[0])
    out = _moe(h, w_router[0], router_bias[0], w_gate[0], w_up[0], w_down[0],
               ws_gate[0], ws_up[0], ws_down[0], ln_ffn_g[0], ln_ffn_b[0])
    return out.reshape(B, S, D)
```

```python
import functools
import math

import jax
import jax.numpy as jnp
from jax import lax
from jax.experimental import pallas as pl
from jax.experimental.pallas import tpu as pltpu

D_MODEL = 1024
DEPTH = 1
RET_HEADS = 4
RET_QK_DIM = 64
RET_V_DIM = 128
RET_CHUNK = 128
RET_WIDTH = RET_HEADS * RET_V_DIM
ROPE_BASE = 10000.0
SWA_HEADS = 8
SWA_KV_HEADS = 2
SWA_GROUP = SWA_HEADS // SWA_KV_HEADS
SWA_HEAD_DIM = 64
SWA_WINDOW = 128
SWA_WIDTH = SWA_HEADS * SWA_HEAD_DIM
MIX_WIDTH = RET_WIDTH + SWA_WIDTH
RQK = RET_HEADS * RET_QK_DIM
SKV = SWA_KV_HEADS * SWA_HEAD_DIM
IN_SIZES = (RQK, RQK, RET_WIDTH, RET_WIDTH, SWA_WIDTH, SKV, SKV)
IN_OFFS = tuple(sum(IN_SIZES[:i]) for i in range(len(IN_SIZES)))
IN_WIDTH = sum(IN_SIZES)
REL_BUCKETS = 32
REL_MAX_DIST = 128
N_EXPERTS = 256
TOP_K = 8
N_GROUPS = 8
GROUP_SIZE = N_EXPERTS // N_GROUPS
TOPK_GROUPS = 4
EXPERT_DIM = 256
SHARED_DIM = 256
ROUTED_SCALE = 2.5
LN_EPS = 1e-5
GN_EPS = 1e-6
DEEPNORM_ALPHA = (2 * DEPTH) ** 0.25
MASK_VALUE = -1e30

VMEM_LIMIT_BYTES = 56 * 1024 * 1024

MIX_ROWS = 256
ROUTE_ROWS = 256
DISPATCH_ROWS = 256
EXPERT_ROWS = 256
COMBINE_ROWS = 128


def _layer_norm(x, g, b):
    mu = jnp.mean(x, axis=-1, keepdims=True)
    xc = x - mu
    var = jnp.mean(xc * xc, axis=-1, keepdims=True)
    return xc * lax.rsqrt(var + LN_EPS) * g + b


def _dot(a, b):
    return jnp.dot(a, b, preferred_element_type=jnp.float32)


def _dot_nt(a, b):
    return lax.dot_general(a, b, (((1,), (1,)), ((), ())), preferred_element_type=jnp.float32)


def _dot_tn(a, b):
    return lax.dot_general(a, b, (((0,), (0,)), ((), ())), preferred_element_type=jnp.float32)


def _silu(x):
    return x * (1.0 / (1.0 + jnp.exp(-x)))


LANES = 128
ROW_TILE = D_MODEL // LANES


def _load_rows(ref, n_rows, lead=()):
    return jnp.concatenate([ref[lead + (pl.ds(s, n_rows, stride=ROW_TILE), slice(None))]
                            for s in range(ROW_TILE)], axis=1)


def _store_rows(ref, val):
    n_rows = val.shape[0]
    for s in range(ROW_TILE):
        ref[pl.ds(s, n_rows, stride=ROW_TILE), :] = val[:, s * LANES:(s + 1) * LANES]


def _row_tile(r):
    return pl.ds(pl.multiple_of(r * ROW_TILE, ROW_TILE), ROW_TILE)


def _swap_halves(x):
    n = x.shape[-1]
    half = RET_QK_DIM // 2
    lane = lax.broadcasted_iota(jnp.int32, x.shape, 1)
    from_right = pltpu.roll(x, n - half, axis=1)
    from_left = pltpu.roll(x, half, axis=1)
    return jnp.where((lane % RET_QK_DIM) < half, from_right, from_left)


def _mixer_kernel(rel_bias_ref, x_ref, g_in_ref, b_in_ref, w_in_ref, w_out_ref, rot_ref, decay_ref,
                  zeta_ref, xi_ref, cdecay_ref, bucket_ref, sink_ref, g_mix_ref, b_mix_ref,
                  h2_ref, state_ref, kprev_ref, vprev_ref, bias_ref, cat_ref):
    b_id = pl.program_id(0)
    c_id = pl.program_id(1)
    W = SWA_WINDOW

    @pl.when((b_id == 0) & (c_id == 0))
    def _build_bias():
        bucket = bucket_ref[...]
        for h in range(SWA_HEADS):
            acc = jnp.full((W, 2 * W), MASK_VALUE, jnp.float32)
            for b in range(REL_BUCKETS):
                acc = jnp.where(bucket == b, rel_bias_ref[b, h], acc)
            kh, g = divmod(h, SWA_GROUP)
            bias_ref[kh, g * W:(g + 1) * W, :] = acc

    @pl.when(c_id == 0)
    def _reset():
        state_ref[...] = jnp.zeros_like(state_ref)
        kprev_ref[...] = jnp.zeros_like(kprev_ref)
        vprev_ref[...] = jnp.zeros_like(vprev_ref)

    h = _layer_norm(x_ref[...], g_in_ref[...], b_in_ref[...])
    proj = _dot(h.astype(jnp.bfloat16), w_in_ref[...])

    o_q, o_k, o_v, o_g, o_sq, o_sk, o_sv = IN_OFFS
    cos_t = rot_ref[:, :RQK]
    sin_t = rot_ref[:, RQK:]
    q_all = proj[:, o_q:o_q + RQK]
    k_all = proj[:, o_k:o_k + RQK]
    q_rot = q_all * cos_t + _swap_halves(q_all) * sin_t
    k_rot = (k_all * cos_t + _swap_halves(k_all) * sin_t) * (RET_QK_DIM ** -0.5)

    n_sub = x_ref.shape[0] // RET_CHUNK
    for s in range(n_sub):
        r0 = s * RET_CHUNK
        rows = slice(r0, r0 + RET_CHUNK)
        for hh in range(RET_HEADS):
            qk = slice(hh * RET_QK_DIM, (hh + 1) * RET_QK_DIM)
            vv = slice(o_v + hh * RET_V_DIM, o_v + (hh + 1) * RET_V_DIM)
            gg = slice(o_g + hh * RET_V_DIM, o_g + (hh + 1) * RET_V_DIM)
            q = q_rot[rows, qk].astype(jnp.bfloat16)
            k32 = k_rot[rows, qk]
            v = proj[rows, vv].astype(jnp.bfloat16)
            scores = _dot_nt(q, k32.astype(jnp.bfloat16)) * decay_ref[hh]
            intra = _dot(scores.astype(jnp.bfloat16), v)
            state = state_ref[hh]
            inter = _dot(q, state.astype(jnp.bfloat16)) * xi_ref[hh]
            ret = intra + inter
            kz = (k32 * zeta_ref[hh]).astype(jnp.bfloat16)
            state_ref[hh] = state * cdecay_ref[hh] + _dot_tn(kz, v)
            mu = jnp.mean(ret, axis=-1, keepdims=True)
            rc = ret - mu
            var = jnp.mean(rc * rc, axis=-1, keepdims=True)
            normed = rc * lax.rsqrt(var + GN_EPS)
            gated = _silu(proj[rows, gg]) * normed
            cat_ref[rows, hh * RET_V_DIM:(hh + 1) * RET_V_DIM] = gated.astype(jnp.bfloat16)
        left_mask = jnp.where(c_id == 0, MASK_VALUE, 0.0) if s == 0 else 0.0
        k_cur = proj[rows, o_sk:o_sk + SKV].astype(jnp.bfloat16)
        v_cur = proj[rows, o_sv:o_sv + SKV].astype(jnp.bfloat16)
        k_prev = kprev_ref[...]
        v_prev = vprev_ref[...]
        for kh in range(SWA_KV_HEADS):
            kv = slice(kh * SWA_HEAD_DIM, (kh + 1) * SWA_HEAD_DIM)
            q4 = jnp.concatenate(
                [proj[rows, o_sq + (kh * SWA_GROUP + g) * SWA_HEAD_DIM:
                      o_sq + (kh * SWA_GROUP + g + 1) * SWA_HEAD_DIM] for g in range(SWA_GROUP)],
                axis=0) * (SWA_HEAD_DIM ** -0.5)
            kcat = jnp.concatenate([k_prev[:, kv], k_cur[:, kv]], axis=0)
            vcat = jnp.concatenate([v_prev[:, kv], v_cur[:, kv]], axis=0)
            logits = _dot_nt(q4.astype(jnp.bfloat16), kcat) + bias_ref[kh]
            col = lax.broadcasted_iota(jnp.int32, logits.shape, 1)
            logits = logits + jnp.where(col < W, left_mask, 0.0)
            sink = sink_ref[kh]
            m = jnp.maximum(jnp.max(logits, axis=-1, keepdims=True), sink)
            p = jnp.exp(logits - m)
            den = jnp.sum(p, axis=-1, keepdims=True) + jnp.exp(sink - m)
            probs = (p / den).astype(jnp.bfloat16)
            o4 = _dot(probs, vcat)
            for g in range(SWA_GROUP):
                c0 = RET_WIDTH + (kh * SWA_GROUP + g) * SWA_HEAD_DIM
                cat_ref[rows, c0:c0 + SWA_HEAD_DIM] = o4[g * W:(g + 1) * W].astype(jnp.bfloat16)
        kprev_ref[...] = k_cur
        vprev_ref[...] = v_cur

    mix = _dot(cat_ref[...], w_out_ref[...])
    _store_rows(h2_ref, _layer_norm(DEEPNORM_ALPHA * h + mix, g_mix_ref[...], b_mix_ref[...]))


def _t5_bucket(dist):
    n = jnp.maximum(dist, 0)
    max_exact = REL_BUCKETS // 2
    ratio = jnp.log(jnp.maximum(n, 1).astype(jnp.float32) / max_exact) / math.log(REL_MAX_DIST / max_exact)
    large = jnp.minimum(max_exact + (ratio * (REL_BUCKETS - max_exact)).astype(jnp.int32), REL_BUCKETS - 1)
    return jnp.where(n < max_exact, n, large)


def _mixer(x, ln_in_g, ln_in_b, w_in, w_out, rel_bias, sinks, ln_mix_g, ln_mix_b):
    B, S, D = x.shape
    R = MIX_ROWS
    C = RET_CHUNK
    W = SWA_WINDOW
    f32 = jnp.float32
    half = RET_QK_DIM // 2
    inv = ROPE_BASE ** (-jnp.arange(half, dtype=f32) / half)
    ang = jnp.arange(S, dtype=f32)[:, None] * inv[None, :]
    cos, sin = jnp.cos(ang), jnp.sin(ang)
    cos_t = jnp.tile(jnp.concatenate([cos, cos], axis=-1), (1, RET_HEADS))
    sin_t = jnp.tile(jnp.concatenate([-sin, sin], axis=-1), (1, RET_HEADS))
    rot = jnp.concatenate([cos_t, sin_t], axis=-1)
    log_gamma = jnp.log(1.0 - 2.0 ** (-5.0 - jnp.arange(RET_HEADS, dtype=f32)))
    idx = jnp.arange(C, dtype=f32)
    diff = idx[:, None] - idx[None, :]
    decay = jnp.where(diff[None] >= 0, jnp.exp(jnp.maximum(diff, 0.0)[None] * log_gamma[:, None, None]), 0.0)
    zeta = jnp.exp((C - 1.0 - idx)[None, :] * log_gamma[:, None])
    xi = jnp.exp((idx + 1.0)[None, :] * log_gamma[:, None])
    zeta_b = jnp.broadcast_to(zeta[:, :, None], (RET_HEADS, C, RET_QK_DIM))
    xi_b = jnp.broadcast_to(xi[:, :, None], (RET_HEADS, C, RET_V_DIM))
    cdecay = jnp.broadcast_to(jnp.exp(C * log_gamma)[:, None, None], (RET_HEADS, RET_QK_DIM, RET_V_DIM))
    i = jnp.arange(W)
    j = jnp.arange(2 * W)
    dist = i[:, None] + W - j[None, :]
    bucket = jnp.where((dist >= 0) & (dist < W), _t5_bucket(dist), -1).astype(jnp.int32)
    sink_col = jnp.repeat(sinks.astype(f32), W).reshape(SWA_KV_HEADS, SWA_GROUP * W, 1)

    const = lambda shape: pl.BlockSpec(shape, lambda b, c, *_: (0,) * len(shape))
    grid_spec = pltpu.PrefetchScalarGridSpec(
        num_scalar_prefetch=1,
        grid=(B, S // R),
        in_specs=[
            pl.BlockSpec((None, R, D), lambda b, c, *_: (b, c, 0)),
            const((1, D)), const((1, D)),
            const((D, IN_WIDTH)), const((MIX_WIDTH, D)),
            pl.BlockSpec((R, 2 * RQK), lambda b, c, *_: (c, 0)),
            const((RET_HEADS, C, C)), const((RET_HEADS, C, RET_QK_DIM)), const((RET_HEADS, C, RET_V_DIM)),
            const((RET_HEADS, RET_QK_DIM, RET_V_DIM)),
            const((W, 2 * W)), const((SWA_KV_HEADS, SWA_GROUP * W, 1)),
            const((1, D)), const((1, D)),
        ],
        out_specs=pl.BlockSpec((R * ROW_TILE, LANES), lambda b, c, *_: (b * (S // R) + c, 0)),
        scratch_shapes=[
            pltpu.VMEM((RET_HEADS, RET_QK_DIM, RET_V_DIM), f32),
            pltpu.VMEM((W, SKV), jnp.bfloat16),
            pltpu.VMEM((W, SKV), jnp.bfloat16),
            pltpu.VMEM((SWA_KV_HEADS, SWA_GROUP * W, 2 * W), f32),
            pltpu.VMEM((R, MIX_WIDTH), jnp.bfloat16),
        ],
    )
    return pl.pallas_call(
        _mixer_kernel,
        grid_spec=grid_spec,
        out_shape=jax.ShapeDtypeStruct((B * S * ROW_TILE, LANES), f32),
        compiler_params=pltpu.CompilerParams(
            dimension_semantics=("arbitrary", "arbitrary"), vmem_limit_bytes=VMEM_LIMIT_BYTES),
    )(rel_bias.astype(f32), x, ln_in_g.reshape(1, D), ln_in_b.reshape(1, D),
      w_in.astype(jnp.bfloat16), w_out.astype(jnp.bfloat16), rot, decay, zeta_b, xi_b, cdecay,
      bucket, sink_col, ln_mix_g.reshape(1, D), ln_mix_b.reshape(1, D))


def _router_kernel(h_ref, wr_ref, rb_ref, e_ref, w_ref, rk_ref, cnt_ref, run_ref):
    f32 = jnp.float32
    R = h_ref.shape[0] // ROW_TILE
    E = N_EXPERTS
    neg = -jnp.inf

    @pl.when(pl.program_id(0) == 0)
    def _init():
        run_ref[...] = jnp.zeros_like(run_ref)

    logits = jnp.dot(_load_rows(h_ref, R), wr_ref[...], precision=lax.Precision.HIGHEST, preferred_element_type=f32)
    scores = 1.0 / (1.0 + jnp.exp(-logits))
    choice = scores + rb_ref[...]
    lane = lax.broadcasted_iota(jnp.int32, (R, E), 1)
    grp = lane // GROUP_SIZE

    def first_argmax(vals):
        m = jnp.max(vals, axis=-1, keepdims=True)
        idx = jnp.min(jnp.where(vals == m, lane, E), axis=-1, keepdims=True)
        return m, idx

    gscore = []
    for g in range(N_GROUPS):
        vals = jnp.where(grp == g, choice, neg)
        m1, i1 = first_argmax(vals)
        m2 = jnp.max(jnp.where(lane == i1, neg, vals), axis=-1, keepdims=True)
        gscore.append(m1 + m2)
    keep = jnp.zeros((R, E), f32)
    for g in range(N_GROUPS):
        beaten = jnp.zeros((R, 1), f32)
        for g2 in range(N_GROUPS):
            if g2 == g:
                continue
            ahead = (gscore[g2] > gscore[g]) | (gscore[g2] == gscore[g]) if g2 < g else gscore[g2] > gscore[g]
            beaten = beaten + jnp.where(ahead, 1.0, 0.0)
        keep = jnp.where(grp == g, jnp.where(beaten < TOPK_GROUPS, 1.0, 0.0), keep)
    masked = jnp.where(keep > 0.0, choice, neg)

    idxs, wts = [], []
    for _ in range(TOP_K):
        _, idx = first_argmax(masked)
        hit = lane == idx
        idxs.append(idx)
        wts.append(jnp.sum(jnp.where(hit, scores, 0.0), axis=-1, keepdims=True))
        masked = jnp.where(hit, neg, masked)
    wsum = wts[0]
    for k in range(1, TOP_K):
        wsum = wsum + wts[k]

    picked = jnp.zeros((R, E), f32)
    for k in range(TOP_K):
        picked = jnp.where(lane == idxs[k], 1.0, picked)
    row = lax.broadcasted_iota(jnp.int32, (R, R), 0)
    colr = lax.broadcasted_iota(jnp.int32, (R, R), 1)
    tri = jnp.where(colr < row, 1.0, 0.0).astype(jnp.bfloat16)
    before = _dot(tri, picked.astype(jnp.bfloat16)) + run_ref[...]
    lane_k = lax.broadcasted_iota(jnp.int32, (R, TOP_K), 1)
    e_out = jnp.zeros((R, TOP_K), jnp.int32)
    w_out = jnp.zeros((R, TOP_K), f32)
    rk_out = jnp.zeros((R, TOP_K), jnp.int32)
    for k in range(TOP_K):
        rank_k = jnp.sum(jnp.where(lane == idxs[k], before, 0.0), axis=-1, keepdims=True)
        e_out = jnp.where(lane_k == k, idxs[k], e_out)
        w_out = jnp.where(lane_k == k, wts[k] / wsum * ROUTED_SCALE, w_out)
        rk_out = jnp.where(lane_k == k, rank_k.astype(jnp.int32), rk_out)
    e_ref[...] = e_out
    w_ref[...] = w_out
    rk_ref[...] = rk_out
    run_ref[...] = run_ref[...] + jnp.sum(picked, axis=0, keepdims=True)
    cnt_ref[...] = run_ref[...]


def _router(h2, w_router, router_bias):
    T, D = h2.shape[0] // ROW_TILE, D_MODEL
    R = ROUTE_ROWS
    E = N_EXPERTS
    return pl.pallas_call(
        _router_kernel,
        grid=(T // R,),
        in_specs=[
            pl.BlockSpec((R * ROW_TILE, LANES), lambda i: (i, 0)),
            pl.BlockSpec((D, E), lambda i: (0, 0)),
            pl.BlockSpec((1, E), lambda i: (0, 0)),
        ],
        out_specs=[
            pl.BlockSpec((R, TOP_K), lambda i: (i, 0)),
            pl.BlockSpec((R, TOP_K), lambda i: (i, 0)),
            pl.BlockSpec((R, TOP_K), lambda i: (i, 0)),
            pl.BlockSpec((1, E), lambda i: (0, 0)),
        ],
        out_shape=[
            jax.ShapeDtypeStruct((T, TOP_K), jnp.int32),
            jax.ShapeDtypeStruct((T, TOP_K), jnp.float32),
            jax.ShapeDtypeStruct((T, TOP_K), jnp.int32),
            jax.ShapeDtypeStruct((1, E), jnp.float32),
        ],
        scratch_shapes=[pltpu.VMEM((1, E), jnp.float32)],
        compiler_params=pltpu.CompilerParams(
            dimension_semantics=("arbitrary",), vmem_limit_bytes=VMEM_LIMIT_BYTES),
    )(h2, w_router, router_bias.reshape(1, E).astype(jnp.float32))


def _positions_kernel(row_start_ref, e_ref, rk_ref, pos_ref):
    e = e_ref[...]
    rk = rk_ref[...]

    def per_expert(i, pos):
        return jnp.where(e == i, rk + row_start_ref[i], pos)

    pos_ref[...] = lax.fori_loop(0, N_EXPERTS, per_expert, jnp.zeros_like(rk))


def _positions(e_idx, rank, row_start):
    n = e_idx.size
    shape = (n // LANES, LANES)
    grid_spec = pltpu.PrefetchScalarGridSpec(
        num_scalar_prefetch=1,
        grid=(1,),
        in_specs=[pl.BlockSpec(shape, lambda i, *_: (0, 0)), pl.BlockSpec(shape, lambda i, *_: (0, 0))],
        out_specs=pl.BlockSpec(shape, lambda i, *_: (0, 0)),
    )
    pos = pl.pallas_call(
        _positions_kernel,
        grid_spec=grid_spec,
        out_shape=jax.ShapeDtypeStruct(shape, jnp.int32),
    )(row_start, e_idx.reshape(shape), rank.reshape(shape))
    return pos.reshape(n)


def _dispatch_kernel(row_start_ref, cnt_ref, n_act_ref, pos_ref, h_ref, xs_ref, zero_ref, sem, zsem,
                     *, n_tokens):
    R = h_ref.shape[0] // ROW_TILE
    BM = EXPERT_ROWS
    n_blocks = xs_ref.shape[0] // (BM * ROW_TILE)
    n_pad_units = n_blocks - n_tokens * TOP_K // BM

    @pl.when(pl.program_id(0) == 0)
    def _zero_padding():
        zero_ref[...] = jnp.zeros_like(zero_ref)

        def expert_tail(e, carry):
            n_tail = pl.multiple_of(((BM - cnt_ref[e] % BM) % BM) * ROW_TILE, ROW_TILE)

            @pl.when(n_tail > 0)
            def _():
                dst = pl.multiple_of((row_start_ref[e] + cnt_ref[e]) * ROW_TILE, ROW_TILE)
                pltpu.make_async_copy(zero_ref.at[pl.ds(0, n_tail)], xs_ref.at[pl.ds(dst, n_tail)], zsem).start()
            return carry

        lax.fori_loop(0, N_EXPERTS, expert_tail, 0)

        def idle_block(i, carry):
            dst = pl.multiple_of(i * (BM * ROW_TILE), BM * ROW_TILE)
            pltpu.make_async_copy(zero_ref, xs_ref.at[pl.ds(dst, BM * ROW_TILE)], zsem).start()
            return carry

        lax.fori_loop(n_act_ref[0], n_blocks, idle_block, 0)

        def drain(i, carry):
            pltpu.make_async_copy(zero_ref, xs_ref.at[pl.ds(0, BM * ROW_TILE)], zsem).wait()
            return carry

        lax.fori_loop(0, n_pad_units, drain, 0)

    def issue(t, carry):
        src = h_ref.at[_row_tile(t)]
        for k in range(TOP_K):
            dest = pos_ref[t * TOP_K + k]
            pltpu.make_async_copy(src, xs_ref.at[_row_tile(dest)], sem).start(priority=k % 2)
        return carry

    lax.fori_loop(0, R, issue, 0)
    n = R * TOP_K * ROW_TILE
    pltpu.make_async_copy(xs_ref.at[pl.ds(0, n)], xs_ref.at[pl.ds(0, n)], sem).wait()


def _dispatch(h2, pos, row_start, cnt, n_act, n_rows):
    T = h2.shape[0] // ROW_TILE
    R = DISPATCH_ROWS
    grid_spec = pltpu.PrefetchScalarGridSpec(
        num_scalar_prefetch=3,
        grid=(T // R,),
        in_specs=[
            pl.BlockSpec((R * TOP_K,), lambda i, *_: (i,), memory_space=pltpu.SMEM),
            pl.BlockSpec((R * ROW_TILE, LANES), lambda i, *_: (i, 0)),
        ],
        out_specs=pl.BlockSpec(memory_space=pl.ANY),
        scratch_shapes=[pltpu.VMEM((EXPERT_ROWS * ROW_TILE, LANES), jnp.float32),
                        pltpu.SemaphoreType.DMA(()), pltpu.SemaphoreType.DMA(())],
    )
    return pl.pallas_call(
        functools.partial(_dispatch_kernel, n_tokens=T),
        grid_spec=grid_spec,
        out_shape=jax.ShapeDtypeStruct((n_rows * ROW_TILE, LANES), jnp.float32),
        compiler_params=pltpu.CompilerParams(
            dimension_semantics=("arbitrary",), vmem_limit_bytes=VMEM_LIMIT_BYTES),
    )(row_start, cnt, n_act, pos, h2)


def _experts_kernel(blk_e_ref, first_ref, slot_ref, next_e_ref, n_act_ref, x_ref, wg_hbm, wu_hbm, wd_hbm,
                    y_ref, wg_buf, wu_buf, wd_buf, wg_bf, wu_bf, wd_bf, sems):
    i = pl.program_id(0)
    bf16 = jnp.bfloat16

    def weight_copies(e, slot):
        return (pltpu.make_async_copy(wg_hbm.at[e], wg_buf.at[slot], sems.at[slot]),
                pltpu.make_async_copy(wu_hbm.at[e], wu_buf.at[slot], sems.at[slot]),
                pltpu.make_async_copy(wd_hbm.at[e], wd_buf.at[slot], sems.at[slot]))

    @pl.when((i < n_act_ref[0]) & (first_ref[i] == 1))
    def _new_expert():
        slot = slot_ref[i]

        @pl.when(i == 0)
        def _():
            for c in weight_copies(blk_e_ref[0], 0):
                c.start()

        for c in weight_copies(blk_e_ref[i], slot):
            c.wait()

        @pl.when(next_e_ref[i] >= 0)
        def _():
            for c in weight_copies(next_e_ref[i], 1 - slot):
                c.start()

        wg_bf[...] = wg_buf[slot].astype(bf16)
        wu_bf[...] = wu_buf[slot].astype(bf16)
        wd_bf[...] = wd_buf[slot].astype(bf16)

    @pl.when(i < n_act_ref[0])
    def _compute():
        x = _load_rows(x_ref, EXPERT_ROWS).astype(bf16)
        g = _dot(x, wg_bf[...])
        u = _dot(x, wu_bf[...])
        a = (_silu(g) * u).astype(bf16)
        _store_rows(y_ref, _dot(a, wd_bf[...]))


def _experts(xs, blk_e, n_act, w_gate, w_up, w_down):
    D = D_MODEL
    BM = EXPERT_ROWS
    F = EXPERT_DIM
    n_blocks = xs.shape[0] // (BM * ROW_TILE)
    ids = jnp.arange(n_blocks, dtype=jnp.int32)
    active = ids < n_act[0]
    first = active & ((ids == 0) | (blk_e != jnp.roll(blk_e, 1)))
    slot = ((jnp.cumsum(first.astype(jnp.int32)) - 1) % 2).astype(jnp.int32)
    first_pos = jnp.where(first, ids, n_blocks)
    later_first = lax.cummin(jnp.concatenate([first_pos[1:], jnp.full((1,), n_blocks, jnp.int32)]), reverse=True)
    next_e = jnp.where(later_first < n_blocks, blk_e[jnp.minimum(later_first, n_blocks - 1)], -1).astype(jnp.int32)

    row_map = lambda i, blk_e, first, slot, next_e, n_act: (jnp.minimum(i, n_act[0] - 1), 0)
    grid_spec = pltpu.PrefetchScalarGridSpec(
        num_scalar_prefetch=5,
        grid=(n_blocks,),
        in_specs=[
            pl.BlockSpec((BM * ROW_TILE, LANES), row_map),
            pl.BlockSpec(memory_space=pl.ANY),
            pl.BlockSpec(memory_space=pl.ANY),
            pl.BlockSpec(memory_space=pl.ANY),
        ],
        out_specs=pl.BlockSpec((BM * ROW_TILE, LANES), row_map),
        scratch_shapes=[
            pltpu.VMEM((2, D, F), jnp.float32), pltpu.VMEM((2, D, F), jnp.float32),
            pltpu.VMEM((2, F, D), jnp.float32),
            pltpu.VMEM((D, F), jnp.bfloat16), pltpu.VMEM((D, F), jnp.bfloat16),
            pltpu.VMEM((F, D), jnp.bfloat16),
            pltpu.SemaphoreType.DMA((2,)),
        ],
    )
    return pl.pallas_call(
        _experts_kernel,
        grid_spec=grid_spec,
        out_shape=jax.ShapeDtypeStruct(xs.shape, jnp.float32),
        input_output_aliases={5: 0},
        compiler_params=pltpu.CompilerParams(
            dimension_semantics=("arbitrary",), vmem_limit_bytes=VMEM_LIMIT_BYTES),
    )(blk_e, first.astype(jnp.int32), slot, next_e, n_act, xs, w_gate, w_up, w_down)


def _combine_kernel(pos_ref, h_ref, w_ref, ys_ref, wsg_ref, wsu_ref, wsd_ref,
                    g_ref, b_ref, out_ref, buf_ref, sem):
    R = h_ref.shape[0] // ROW_TILE

    def issue(t, carry):
        for k in range(TOP_K):
            src = pos_ref[t * TOP_K + k]
            pltpu.make_async_copy(ys_ref.at[_row_tile(src)], buf_ref.at[k, _row_tile(t)], sem).start(priority=k % 2)
        return carry

    lax.fori_loop(0, R, issue, 0)
    h = _load_rows(h_ref, R)
    hb = h.astype(jnp.bfloat16)
    act = (_silu(_dot(hb, wsg_ref[...])) * _dot(hb, wsu_ref[...])).astype(jnp.bfloat16)
    ffn = _dot(act, wsd_ref[...])
    pltpu.make_async_copy(buf_ref, buf_ref, sem).wait()
    w = w_ref[...]
    for k in range(TOP_K):
        ffn = ffn + _load_rows(buf_ref, R, lead=(k,)) * w[:, k:k + 1]
    out_ref[...] = _layer_norm(DEEPNORM_ALPHA * h + ffn, g_ref[...], b_ref[...])


def _combine(h2, pos, top_w, ys, ws_gate, ws_up, ws_down, ln_g, ln_b):
    T, D = h2.shape[0] // ROW_TILE, D_MODEL
    R = COMBINE_ROWS
    F = SHARED_DIM
    bf16 = jnp.bfloat16
    const = lambda shape: pl.BlockSpec(shape, lambda i: (0,) * len(shape))
    return pl.pallas_call(
        _combine_kernel,
        grid=(T // R,),
        in_specs=[
            pl.BlockSpec((R * TOP_K,), lambda i: (i,), memory_space=pltpu.SMEM),
            pl.BlockSpec((R * ROW_TILE, LANES), lambda i: (i, 0)),
            pl.BlockSpec((R, TOP_K), lambda i: (i, 0)),
            pl.BlockSpec(memory_space=pl.ANY),
            const((D, F)), const((D, F)), const((F, D)), const((1, D)), const((1, D)),
        ],
        out_specs=pl.BlockSpec((R, D), lambda i: (i, 0)),
        scratch_shapes=[pltpu.VMEM((TOP_K, R * ROW_TILE, LANES), jnp.float32), pltpu.SemaphoreType.DMA(())],
        out_shape=jax.ShapeDtypeStruct((T, D), jnp.float32),
        compiler_params=pltpu.CompilerParams(
            dimension_semantics=("arbitrary",), vmem_limit_bytes=VMEM_LIMIT_BYTES),
    )(pos, h2, top_w, ys, ws_gate.astype(bf16), ws_up.astype(bf16),
      ws_down.astype(bf16), ln_g.reshape(1, D), ln_b.reshape(1, D))


def _moe(h2, w_router, router_bias, w_gate, w_up, w_down, ws_gate, ws_up, ws_down, ln_g, ln_b):
    T = h2.shape[0] // ROW_TILE
    E = N_EXPERTS
    BM = EXPERT_ROWS
    e_idx, top_w, rank, counts = _router(h2, w_router, router_bias)
    cnt = counts.reshape(E).astype(jnp.int32)
    nblk = (cnt + BM - 1) // BM
    blk_end = jnp.cumsum(nblk)
    row_start = ((blk_end - nblk) * BM).astype(jnp.int32)
    n_blocks = T * TOP_K // BM + E
    n_act = blk_end[-1:].astype(jnp.int32)
    blk_ids = jnp.minimum(jnp.arange(n_blocks, dtype=jnp.int32), n_act[0] - 1)
    blk_e = jnp.minimum(jnp.searchsorted(blk_end, blk_ids, side="right"), E - 1).astype(jnp.int32)
    pos = _positions(e_idx, rank, row_start)
    xs = _dispatch(h2, pos, row_start, cnt, n_act, n_blocks * BM)
    ys = _experts(xs, blk_e, n_act, w_gate, w_up, w_down)
    return _combine(h2, pos, top_w, ys, ws_gate, ws_up, ws_down, ln_g, ln_b)


def kernel(x, ln_in_g, ln_in_b, w_in, w_out, rel_bias, attn_sinks, ln_mix_g, ln_mix_b, w_router,
           router_bias, w_gate, w_up, w_down, ws_gate, ws_up, ws_down, ln_ffn_g, ln_ffn_b):
    B, S, D = x.shape
    h = _mixer(x, ln_in_g, ln_in_b, w_in[0], w_out[0], rel_bias, attn_sinks[0], ln_mix_g[0], ln_mix_b[0])
    out = _moe(h, w_router[0], router_bias[0], w_gate[0], w_up[0], w_down[0],
               ws_gate[0], ws_up[0], ws_down[0], ln_ffn_g[0], ln_ffn_b[0])
    return out.reshape(B, S, D)
```

```python
import functools
import math

import jax
import jax.numpy as jnp
from jax import lax
from jax.experimental import pallas as pl
from jax.experimental.pallas import tpu as pltpu

D_MODEL = 1024
DEPTH = 1
RET_HEADS = 4
RET_QK_DIM = 64
RET_V_DIM = 128
RET_CHUNK = 128
RET_WIDTH = RET_HEADS * RET_V_DIM
ROPE_BASE = 10000.0
SWA_HEADS = 8
SWA_KV_HEADS = 2
SWA_GROUP = SWA_HEADS // SWA_KV_HEADS
SWA_HEAD_DIM = 64
SWA_WINDOW = 128
SWA_WIDTH = SWA_HEADS * SWA_HEAD_DIM
MIX_WIDTH = RET_WIDTH + SWA_WIDTH
RQK = RET_HEADS * RET_QK_DIM
SKV = SWA_KV_HEADS * SWA_HEAD_DIM
IN_SIZES = (RQK, RQK, RET_WIDTH, RET_WIDTH, SWA_WIDTH, SKV, SKV)
IN_OFFS = tuple(sum(IN_SIZES[:i]) for i in range(len(IN_SIZES)))
IN_WIDTH = sum(IN_SIZES)
REL_BUCKETS = 32
REL_MAX_DIST = 128
N_EXPERTS = 256
TOP_K = 8
N_GROUPS = 8
GROUP_SIZE = N_EXPERTS // N_GROUPS
TOPK_GROUPS = 4
EXPERT_DIM = 256
SHARED_DIM = 256
ROUTED_SCALE = 2.5
LN_EPS = 1e-5
GN_EPS = 1e-6
DEEPNORM_ALPHA = (2 * DEPTH) ** 0.25
MASK_VALUE = -1e30

VMEM_LIMIT_BYTES = 56 * 1024 * 1024

MIX_ROWS = 256
ROUTE_ROWS = 256
DISPATCH_ROWS = 256
EXPERT_ROWS = 256
COMBINE_ROWS = 128


def _layer_norm(x, g, b):
    mu = jnp.mean(x, axis=-1, keepdims=True)
    xc = x - mu
    var = jnp.mean(xc * xc, axis=-1, keepdims=True)
    return xc * lax.rsqrt(var + LN_EPS) * g + b


def _dot(a, b):
    return jnp.dot(a, b, preferred_element_type=jnp.float32)


def _dot_nt(a, b):
    return lax.dot_general(a, b, (((1,), (1,)), ((), ())), preferred_element_type=jnp.float32)


def _dot_tn(a, b):
    return lax.dot_general(a, b, (((0,), (0,)), ((), ())), preferred_element_type=jnp.float32)


def _silu(x):
    return x * (1.0 / (1.0 + jnp.exp(-x)))


LANES = 128
ROW_TILE = D_MODEL // LANES


def _load_rows(ref, n_rows, lead=()):
    return jnp.concatenate([ref[lead + (pl.ds(s, n_rows, stride=ROW_TILE), slice(None))]
                            for s in range(ROW_TILE)], axis=1)


def _store_rows(ref, val, lead=()):
    n_rows = val.shape[0]
    for s in range(ROW_TILE):
        ref[lead + (pl.ds(s, n_rows, stride=ROW_TILE), slice(None))] = val[:, s * LANES:(s + 1) * LANES]


def _row_tile(r):
    return pl.ds(pl.multiple_of(r * ROW_TILE, ROW_TILE), ROW_TILE)


def _swap_halves(x):
    n = x.shape[-1]
    half = RET_QK_DIM // 2
    lane = lax.broadcasted_iota(jnp.int32, x.shape, 1)
    from_right = pltpu.roll(x, n - half, axis=1)
    from_left = pltpu.roll(x, half, axis=1)
    return jnp.where((lane % RET_QK_DIM) < half, from_right, from_left)


def _mixer_kernel(rel_bias_ref, x_ref, g_in_ref, b_in_ref, w_in_ref, w_out_ref, rot_ref, decay_ref,
                  zeta_ref, xi_ref, cdecay_ref, bucket_ref, sink_ref, g_mix_ref, b_mix_ref,
                  h2_ref, state_ref, kprev_ref, vprev_ref, bias_ref, cat_ref):
    b_id = pl.program_id(0)
    c_id = pl.program_id(1)
    W = SWA_WINDOW

    @pl.when((b_id == 0) & (c_id == 0))
    def _build_bias():
        bucket = bucket_ref[...]
        for h in range(SWA_HEADS):
            acc = jnp.full((W, 2 * W), MASK_VALUE, jnp.float32)
            for b in range(REL_BUCKETS):
                acc = jnp.where(bucket == b, rel_bias_ref[b, h], acc)
            kh, g = divmod(h, SWA_GROUP)
            bias_ref[kh, g * W:(g + 1) * W, :] = acc

    @pl.when(c_id == 0)
    def _reset():
        state_ref[...] = jnp.zeros_like(state_ref)
        kprev_ref[...] = jnp.zeros_like(kprev_ref)
        vprev_ref[...] = jnp.zeros_like(vprev_ref)

    h = _layer_norm(x_ref[...], g_in_ref[...], b_in_ref[...])
    proj = _dot(h.astype(jnp.bfloat16), w_in_ref[...])

    o_q, o_k, o_v, o_g, o_sq, o_sk, o_sv = IN_OFFS
    cos_t = rot_ref[:, :RQK]
    sin_t = rot_ref[:, RQK:]
    q_all = proj[:, o_q:o_q + RQK]
    k_all = proj[:, o_k:o_k + RQK]
    q_rot = q_all * cos_t + _swap_halves(q_all) * sin_t
    k_rot = (k_all * cos_t + _swap_halves(k_all) * sin_t) * (RET_QK_DIM ** -0.5)

    n_sub = x_ref.shape[0] // RET_CHUNK
    for s in range(n_sub):
        r0 = s * RET_CHUNK
        rows = slice(r0, r0 + RET_CHUNK)
        for hh in range(RET_HEADS):
            qk = slice(hh * RET_QK_DIM, (hh + 1) * RET_QK_DIM)
            vv = slice(o_v + hh * RET_V_DIM, o_v + (hh + 1) * RET_V_DIM)
            gg = slice(o_g + hh * RET_V_DIM, o_g + (hh + 1) * RET_V_DIM)
            q = q_rot[rows, qk].astype(jnp.bfloat16)
            k32 = k_rot[rows, qk]
            v = proj[rows, vv].astype(jnp.bfloat16)
            scores = _dot_nt(q, k32.astype(jnp.bfloat16)) * decay_ref[hh]
            intra = _dot(scores.astype(jnp.bfloat16), v)
            state = state_ref[hh]
            inter = _dot(q, state.astype(jnp.bfloat16)) * xi_ref[hh]
            ret = intra + inter
            kz = (k32 * zeta_ref[hh]).astype(jnp.bfloat16)
            state_ref[hh] = state * cdecay_ref[hh] + _dot_tn(kz, v)
            mu = jnp.mean(ret, axis=-1, keepdims=True)
            rc = ret - mu
            var = jnp.mean(rc * rc, axis=-1, keepdims=True)
            normed = rc * lax.rsqrt(var + GN_EPS)
            gated = _silu(proj[rows, gg]) * normed
            cat_ref[rows, hh * RET_V_DIM:(hh + 1) * RET_V_DIM] = gated.astype(jnp.bfloat16)
        left_mask = jnp.where(c_id == 0, MASK_VALUE, 0.0) if s == 0 else 0.0
        k_cur = proj[rows, o_sk:o_sk + SKV].astype(jnp.bfloat16)
        v_cur = proj[rows, o_sv:o_sv + SKV].astype(jnp.bfloat16)
        k_prev = kprev_ref[...]
        v_prev = vprev_ref[...]
        for kh in range(SWA_KV_HEADS):
            kv = slice(kh * SWA_HEAD_DIM, (kh + 1) * SWA_HEAD_DIM)
            q4 = jnp.concatenate(
                [proj[rows, o_sq + (kh * SWA_GROUP + g) * SWA_HEAD_DIM:
                      o_sq + (kh * SWA_GROUP + g + 1) * SWA_HEAD_DIM] for g in range(SWA_GROUP)],
                axis=0) * (SWA_HEAD_DIM ** -0.5)
            kcat = jnp.concatenate([k_prev[:, kv], k_cur[:, kv]], axis=0)
            vcat = jnp.concatenate([v_prev[:, kv], v_cur[:, kv]], axis=0)
            logits = _dot_nt(q4.astype(jnp.bfloat16), kcat) + bias_ref[kh]
            col = lax.broadcasted_iota(jnp.int32, logits.shape, 1)
            logits = logits + jnp.where(col < W, left_mask, 0.0)
            sink = sink_ref[kh]
            m = jnp.maximum(jnp.max(logits, axis=-1, keepdims=True), sink)
            p = jnp.exp(logits - m)
            den = jnp.sum(p, axis=-1, keepdims=True) + jnp.exp(sink - m)
            probs = (p / den).astype(jnp.bfloat16)
            o4 = _dot(probs, vcat)
            for g in range(SWA_GROUP):
                c0 = RET_WIDTH + (kh * SWA_GROUP + g) * SWA_HEAD_DIM
                cat_ref[rows, c0:c0 + SWA_HEAD_DIM] = o4[g * W:(g + 1) * W].astype(jnp.bfloat16)
        kprev_ref[...] = k_cur
        vprev_ref[...] = v_cur

    mix = _dot(cat_ref[...], w_out_ref[...])
    _store_rows(h2_ref, _layer_norm(DEEPNORM_ALPHA * h + mix, g_mix_ref[...], b_mix_ref[...]))


def _t5_bucket(dist):
    n = jnp.maximum(dist, 0)
    max_exact = REL_BUCKETS // 2
    ratio = jnp.log(jnp.maximum(n, 1).astype(jnp.float32) / max_exact) / math.log(REL_MAX_DIST / max_exact)
    large = jnp.minimum(max_exact + (ratio * (REL_BUCKETS - max_exact)).astype(jnp.int32), REL_BUCKETS - 1)
    return jnp.where(n < max_exact, n, large)


def _mixer(x, ln_in_g, ln_in_b, w_in, w_out, rel_bias, sinks, ln_mix_g, ln_mix_b):
    B, S, D = x.shape
    R = MIX_ROWS
    C = RET_CHUNK
    W = SWA_WINDOW
    f32 = jnp.float32
    half = RET_QK_DIM // 2
    inv = ROPE_BASE ** (-jnp.arange(half, dtype=f32) / half)
    ang = jnp.arange(S, dtype=f32)[:, None] * inv[None, :]
    cos, sin = jnp.cos(ang), jnp.sin(ang)
    cos_t = jnp.tile(jnp.concatenate([cos, cos], axis=-1), (1, RET_HEADS))
    sin_t = jnp.tile(jnp.concatenate([-sin, sin], axis=-1), (1, RET_HEADS))
    rot = jnp.concatenate([cos_t, sin_t], axis=-1)
    log_gamma = jnp.log(1.0 - 2.0 ** (-5.0 - jnp.arange(RET_HEADS, dtype=f32)))
    idx = jnp.arange(C, dtype=f32)
    diff = idx[:, None] - idx[None, :]
    decay = jnp.where(diff[None] >= 0, jnp.exp(jnp.maximum(diff, 0.0)[None] * log_gamma[:, None, None]), 0.0)
    zeta = jnp.exp((C - 1.0 - idx)[None, :] * log_gamma[:, None])
    xi = jnp.exp((idx + 1.0)[None, :] * log_gamma[:, None])
    zeta_b = jnp.broadcast_to(zeta[:, :, None], (RET_HEADS, C, RET_QK_DIM))
    xi_b = jnp.broadcast_to(xi[:, :, None], (RET_HEADS, C, RET_V_DIM))
    cdecay = jnp.broadcast_to(jnp.exp(C * log_gamma)[:, None, None], (RET_HEADS, RET_QK_DIM, RET_V_DIM))
    i = jnp.arange(W)
    j = jnp.arange(2 * W)
    dist = i[:, None] + W - j[None, :]
    bucket = jnp.where((dist >= 0) & (dist < W), _t5_bucket(dist), -1).astype(jnp.int32)
    sink_col = jnp.repeat(sinks.astype(f32), W).reshape(SWA_KV_HEADS, SWA_GROUP * W, 1)

    const = lambda shape: pl.BlockSpec(shape, lambda b, c, *_: (0,) * len(shape))
    grid_spec = pltpu.PrefetchScalarGridSpec(
        num_scalar_prefetch=1,
        grid=(B, S // R),
        in_specs=[
            pl.BlockSpec((None, R, D), lambda b, c, *_: (b, c, 0)),
            const((1, D)), const((1, D)),
            const((D, IN_WIDTH)), const((MIX_WIDTH, D)),
            pl.BlockSpec((R, 2 * RQK), lambda b, c, *_: (c, 0)),
            const((RET_HEADS, C, C)), const((RET_HEADS, C, RET_QK_DIM)), const((RET_HEADS, C, RET_V_DIM)),
            const((RET_HEADS, RET_QK_DIM, RET_V_DIM)),
            const((W, 2 * W)), const((SWA_KV_HEADS, SWA_GROUP * W, 1)),
            const((1, D)), const((1, D)),
        ],
        out_specs=pl.BlockSpec((R * ROW_TILE, LANES), lambda b, c, *_: (b * (S // R) + c, 0)),
        scratch_shapes=[
            pltpu.VMEM((RET_HEADS, RET_QK_DIM, RET_V_DIM), f32),
            pltpu.VMEM((W, SKV), jnp.bfloat16),
            pltpu.VMEM((W, SKV), jnp.bfloat16),
            pltpu.VMEM((SWA_KV_HEADS, SWA_GROUP * W, 2 * W), f32),
            pltpu.VMEM((R, MIX_WIDTH), jnp.bfloat16),
        ],
    )
    return pl.pallas_call(
        _mixer_kernel,
        grid_spec=grid_spec,
        out_shape=jax.ShapeDtypeStruct((B * S * ROW_TILE, LANES), f32),
        compiler_params=pltpu.CompilerParams(
            dimension_semantics=("arbitrary", "arbitrary"), vmem_limit_bytes=VMEM_LIMIT_BYTES),
    )(rel_bias.astype(f32), x, ln_in_g.reshape(1, D), ln_in_b.reshape(1, D),
      w_in.astype(jnp.bfloat16), w_out.astype(jnp.bfloat16), rot, decay, zeta_b, xi_b, cdecay,
      bucket, sink_col, ln_mix_g.reshape(1, D), ln_mix_b.reshape(1, D))


def _router_kernel(h_ref, wr_ref, rb_ref, e_ref, w_ref, rk_ref, cnt_ref, run_ref):
    f32 = jnp.float32
    R = h_ref.shape[0] // ROW_TILE
    E = N_EXPERTS
    neg = -jnp.inf

    @pl.when(pl.program_id(0) == 0)
    def _init():
        run_ref[...] = jnp.zeros_like(run_ref)

    logits = jnp.dot(_load_rows(h_ref, R), wr_ref[...], precision=lax.Precision.HIGHEST, preferred_element_type=f32)
    scores = 1.0 / (1.0 + jnp.exp(-logits))
    choice = scores + rb_ref[...]
    lane = lax.broadcasted_iota(jnp.int32, (R, E), 1)
    grp = lane // GROUP_SIZE

    def first_argmax(vals):
        m = jnp.max(vals, axis=-1, keepdims=True)
        idx = jnp.min(jnp.where(vals == m, lane, E), axis=-1, keepdims=True)
        return m, idx

    gscore = []
    for g in range(N_GROUPS):
        vals = jnp.where(grp == g, choice, neg)
        m1, i1 = first_argmax(vals)
        m2 = jnp.max(jnp.where(lane == i1, neg, vals), axis=-1, keepdims=True)
        gscore.append(m1 + m2)
    keep = jnp.zeros((R, E), f32)
    for g in range(N_GROUPS):
        beaten = jnp.zeros((R, 1), f32)
        for g2 in range(N_GROUPS):
            if g2 == g:
                continue
            ahead = (gscore[g2] > gscore[g]) | (gscore[g2] == gscore[g]) if g2 < g else gscore[g2] > gscore[g]
            beaten = beaten + jnp.where(ahead, 1.0, 0.0)
        keep = jnp.where(grp == g, jnp.where(beaten < TOPK_GROUPS, 1.0, 0.0), keep)
    masked = jnp.where(keep > 0.0, choice, neg)

    idxs, wts = [], []
    for _ in range(TOP_K):
        _, idx = first_argmax(masked)
        hit = lane == idx
        idxs.append(idx)
        wts.append(jnp.sum(jnp.where(hit, scores, 0.0), axis=-1, keepdims=True))
        masked = jnp.where(hit, neg, masked)
    wsum = wts[0]
    for k in range(1, TOP_K):
        wsum = wsum + wts[k]

    picked = jnp.zeros((R, E), f32)
    for k in range(TOP_K):
        picked = jnp.where(lane == idxs[k], 1.0, picked)
    row = lax.broadcasted_iota(jnp.int32, (R, R), 0)
    colr = lax.broadcasted_iota(jnp.int32, (R, R), 1)
    tri = jnp.where(colr < row, 1.0, 0.0).astype(jnp.bfloat16)
    before = _dot(tri, picked.astype(jnp.bfloat16)) + run_ref[...]
    lane_k = lax.broadcasted_iota(jnp.int32, (R, TOP_K), 1)
    e_out = jnp.zeros((R, TOP_K), jnp.int32)
    w_out = jnp.zeros((R, TOP_K), f32)
    rk_out = jnp.zeros((R, TOP_K), jnp.int32)
    for k in range(TOP_K):
        rank_k = jnp.sum(jnp.where(lane == idxs[k], before, 0.0), axis=-1, keepdims=True)
        e_out = jnp.where(lane_k == k, idxs[k], e_out)
        w_out = jnp.where(lane_k == k, wts[k] / wsum * ROUTED_SCALE, w_out)
        rk_out = jnp.where(lane_k == k, rank_k.astype(jnp.int32), rk_out)
    e_ref[...] = e_out
    w_ref[...] = w_out
    rk_ref[...] = rk_out
    run_ref[...] = run_ref[...] + jnp.sum(picked, axis=0, keepdims=True)
    cnt_ref[...] = run_ref[...]


def _router(h2, w_router, router_bias):
    T, D = h2.shape[0] // ROW_TILE, D_MODEL
    R = ROUTE_ROWS
    E = N_EXPERTS
    return pl.pallas_call(
        _router_kernel,
        grid=(T // R,),
        in_specs=[
            pl.BlockSpec((R * ROW_TILE, LANES), lambda i: (i, 0)),
            pl.BlockSpec((D, E), lambda i: (0, 0)),
            pl.BlockSpec((1, E), lambda i: (0, 0)),
        ],
        out_specs=[
            pl.BlockSpec((R, TOP_K), lambda i: (i, 0)),
            pl.BlockSpec((R, TOP_K), lambda i: (i, 0)),
            pl.BlockSpec((R, TOP_K), lambda i: (i, 0)),
            pl.BlockSpec((1, E), lambda i: (0, 0)),
        ],
        out_shape=[
            jax.ShapeDtypeStruct((T, TOP_K), jnp.int32),
            jax.ShapeDtypeStruct((T, TOP_K), jnp.float32),
            jax.ShapeDtypeStruct((T, TOP_K), jnp.int32),
            jax.ShapeDtypeStruct((1, E), jnp.float32),
        ],
        scratch_shapes=[pltpu.VMEM((1, E), jnp.float32)],
        compiler_params=pltpu.CompilerParams(
            dimension_semantics=("arbitrary",), vmem_limit_bytes=VMEM_LIMIT_BYTES),
    )(h2, w_router, router_bias.reshape(1, E).astype(jnp.float32))


def _positions_kernel(row_start_ref, e_ref, rk_ref, pos_ref):
    e = e_ref[...]
    rk = rk_ref[...]

    def per_expert(i, pos):
        return jnp.where(e == i, rk + row_start_ref[i], pos)

    pos_ref[...] = lax.fori_loop(0, N_EXPERTS, per_expert, jnp.zeros_like(rk))


def _positions(e_idx, rank, row_start):
    n = e_idx.size
    shape = (n // LANES, LANES)
    grid_spec = pltpu.PrefetchScalarGridSpec(
        num_scalar_prefetch=1,
        grid=(1,),
        in_specs=[pl.BlockSpec(shape, lambda i, *_: (0, 0)), pl.BlockSpec(shape, lambda i, *_: (0, 0))],
        out_specs=pl.BlockSpec(shape, lambda i, *_: (0, 0)),
    )
    pos = pl.pallas_call(
        _positions_kernel,
        grid_spec=grid_spec,
        out_shape=jax.ShapeDtypeStruct(shape, jnp.int32),
    )(row_start, e_idx.reshape(shape), rank.reshape(shape))
    return pos.reshape(n)


def _dispatch_kernel(row_start_ref, cnt_ref, n_act_ref, pos_ref, h_ref, xs_ref, zero_ref, sem, zsem,
                     *, n_tokens):
    R = h_ref.shape[0] // ROW_TILE
    BM = EXPERT_ROWS
    n_blocks = xs_ref.shape[0] // (BM * ROW_TILE)
    n_pad_units = n_blocks - n_tokens * TOP_K // BM

    @pl.when(pl.program_id(0) == 0)
    def _zero_padding():
        zero_ref[...] = jnp.zeros_like(zero_ref)

        def expert_tail(e, carry):
            n_tail = pl.multiple_of(((BM - cnt_ref[e] % BM) % BM) * ROW_TILE, ROW_TILE)

            @pl.when(n_tail > 0)
            def _():
                dst = pl.multiple_of((row_start_ref[e] + cnt_ref[e]) * ROW_TILE, ROW_TILE)
                pltpu.make_async_copy(zero_ref.at[pl.ds(0, n_tail)], xs_ref.at[pl.ds(dst, n_tail)], zsem).start()
            return carry

        lax.fori_loop(0, N_EXPERTS, expert_tail, 0)

        def idle_block(i, carry):
            dst = pl.multiple_of(i * (BM * ROW_TILE), BM * ROW_TILE)
            pltpu.make_async_copy(zero_ref, xs_ref.at[pl.ds(dst, BM * ROW_TILE)], zsem).start()
            return carry

        lax.fori_loop(n_act_ref[0], n_blocks, idle_block, 0)

        def drain(i, carry):
            pltpu.make_async_copy(zero_ref, xs_ref.at[pl.ds(0, BM * ROW_TILE)], zsem).wait()
            return carry

        lax.fori_loop(0, n_pad_units, drain, 0)

    def issue(t, carry):
        src = h_ref.at[_row_tile(t)]
        for k in range(TOP_K):
            dest = pos_ref[t * TOP_K + k]
            pltpu.make_async_copy(src, xs_ref.at[_row_tile(dest)], sem).start(priority=k % 2)
        return carry

    lax.fori_loop(0, R, issue, 0)
    n = R * TOP_K * ROW_TILE
    pltpu.make_async_copy(xs_ref.at[pl.ds(0, n)], xs_ref.at[pl.ds(0, n)], sem).wait()


def _dispatch(h2, pos, row_start, cnt, n_act, n_rows):
    T = h2.shape[0] // ROW_TILE
    R = DISPATCH_ROWS
    grid_spec = pltpu.PrefetchScalarGridSpec(
        num_scalar_prefetch=3,
        grid=(T // R,),
        in_specs=[
            pl.BlockSpec((R * TOP_K,), lambda i, *_: (i,), memory_space=pltpu.SMEM),
            pl.BlockSpec((R * ROW_TILE, LANES), lambda i, *_: (i, 0)),
        ],
        out_specs=pl.BlockSpec(memory_space=pl.ANY),
        scratch_shapes=[pltpu.VMEM((EXPERT_ROWS * ROW_TILE, LANES), jnp.float32),
                        pltpu.SemaphoreType.DMA(()), pltpu.SemaphoreType.DMA(())],
    )
    return pl.pallas_call(
        functools.partial(_dispatch_kernel, n_tokens=T),
        grid_spec=grid_spec,
        out_shape=jax.ShapeDtypeStruct((n_rows * ROW_TILE, LANES), jnp.float32),
        compiler_params=pltpu.CompilerParams(
            dimension_semantics=("arbitrary",), vmem_limit_bytes=VMEM_LIMIT_BYTES),
    )(row_start, cnt, n_act, pos, h2)


X_SLOTS = 4
Y_SLOTS = 3


def _experts_kernel(blk_e_ref, first_ref, slot_ref, next_e_ref, n_act_ref, xs_hbm, wg_hbm, wu_hbm, wd_hbm,
                    ys_hbm, x_buf, y_buf, wg_buf, wu_buf, wd_buf, wg_bf, wu_bf, wd_bf, sems, x_sems, y_sems):
    i = pl.program_id(0)
    n_act = n_act_ref[0]
    bf16 = jnp.bfloat16
    blk = EXPERT_ROWS * ROW_TILE

    def block_rows(j):
        return pl.ds(pl.multiple_of(j * blk, blk), blk)

    def x_copy(j):
        return pltpu.make_async_copy(xs_hbm.at[block_rows(j)], x_buf.at[j % X_SLOTS], x_sems.at[j % X_SLOTS])

    def y_copy(j):
        return pltpu.make_async_copy(y_buf.at[j % Y_SLOTS], ys_hbm.at[block_rows(j)], y_sems.at[j % Y_SLOTS])

    @pl.when(i == 0)
    def _prime():
        for j in range(X_SLOTS - 1):
            @pl.when(j < n_act)
            def _():
                x_copy(j).start()

    @pl.when(i + (X_SLOTS - 1) < n_act)
    def _prefetch():
        x_copy(i + (X_SLOTS - 1)).start()

    def weight_copies(e, slot):
        return (pltpu.make_async_copy(wg_hbm.at[e], wg_buf.at[slot], sems.at[slot]),
                pltpu.make_async_copy(wu_hbm.at[e], wu_buf.at[slot], sems.at[slot]),
                pltpu.make_async_copy(wd_hbm.at[e], wd_buf.at[slot], sems.at[slot]))

    @pl.when((i < n_act_ref[0]) & (first_ref[i] == 1))
    def _new_expert():
        slot = slot_ref[i]

        @pl.when(i == 0)
        def _():
            for c in weight_copies(blk_e_ref[0], 0):
                c.start()

        for c in weight_copies(blk_e_ref[i], slot):
            c.wait()

        @pl.when(next_e_ref[i] >= 0)
        def _():
            for c in weight_copies(next_e_ref[i], 1 - slot):
                c.start()

        wg_bf[...] = wg_buf[slot].astype(bf16)
        wu_bf[...] = wu_buf[slot].astype(bf16)
        wd_bf[...] = wd_buf[slot].astype(bf16)

    @pl.when(i < n_act)
    def _compute():
        x_copy(i).wait()
        x = _load_rows(x_buf, EXPERT_ROWS, lead=(i % X_SLOTS,)).astype(bf16)
        g = _dot(x, wg_bf[...])
        u = _dot(x, wu_bf[...])
        a = (_silu(g) * u).astype(bf16)
        y = _dot(a, wd_bf[...])

        @pl.when(i >= Y_SLOTS)
        def _():
            y_copy(i - Y_SLOTS).wait()

        _store_rows(y_buf, y, lead=(i % Y_SLOTS,))
        y_copy(i).start()

    @pl.when(i == n_act - 1)
    def _drain():
        for d in range(Y_SLOTS):
            @pl.when(i - d >= 0)
            def _():
                y_copy(i - d).wait()


def _experts(xs, blk_e, n_act, w_gate, w_up, w_down):
    D = D_MODEL
    BM = EXPERT_ROWS
    F = EXPERT_DIM
    n_blocks = xs.shape[0] // (BM * ROW_TILE)
    ids = jnp.arange(n_blocks, dtype=jnp.int32)
    active = ids < n_act[0]
    first = active & ((ids == 0) | (blk_e != jnp.roll(blk_e, 1)))
    slot = ((jnp.cumsum(first.astype(jnp.int32)) - 1) % 2).astype(jnp.int32)
    first_pos = jnp.where(first, ids, n_blocks)
    later_first = lax.cummin(jnp.concatenate([first_pos[1:], jnp.full((1,), n_blocks, jnp.int32)]), reverse=True)
    next_e = jnp.where(later_first < n_blocks, blk_e[jnp.minimum(later_first, n_blocks - 1)], -1).astype(jnp.int32)

    grid_spec = pltpu.PrefetchScalarGridSpec(
        num_scalar_prefetch=5,
        grid=(n_blocks,),
        in_specs=[pl.BlockSpec(memory_space=pl.ANY)] * 4,
        out_specs=pl.BlockSpec(memory_space=pl.ANY),
        scratch_shapes=[
            pltpu.VMEM((X_SLOTS, BM * ROW_TILE, LANES), jnp.float32),
            pltpu.VMEM((Y_SLOTS, BM * ROW_TILE, LANES), jnp.float32),
            pltpu.VMEM((2, D, F), jnp.float32), pltpu.VMEM((2, D, F), jnp.float32),
            pltpu.VMEM((2, F, D), jnp.float32),
            pltpu.VMEM((D, F), jnp.bfloat16), pltpu.VMEM((D, F), jnp.bfloat16),
            pltpu.VMEM((F, D), jnp.bfloat16),
            pltpu.SemaphoreType.DMA((2,)), pltpu.SemaphoreType.DMA((X_SLOTS,)),
            pltpu.SemaphoreType.DMA((Y_SLOTS,)),
        ],
    )
    return pl.pallas_call(
        _experts_kernel,
        grid_spec=grid_spec,
        out_shape=jax.ShapeDtypeStruct(xs.shape, jnp.float32),
        input_output_aliases={5: 0},
        compiler_params=pltpu.CompilerParams(
            dimension_semantics=("arbitrary",), vmem_limit_bytes=VMEM_LIMIT_BYTES),
    )(blk_e, first.astype(jnp.int32), slot, next_e, n_act, xs, w_gate, w_up, w_down)


def _combine_kernel(pos_ref, h_ref, w_ref, ys_ref, wsg_ref, wsu_ref, wsd_ref,
                    g_ref, b_ref, out_ref, buf_ref, sem):
    R = h_ref.shape[0] // ROW_TILE

    def issue(t, carry):
        for k in range(TOP_K):
            src = pos_ref[t * TOP_K + k]
            pltpu.make_async_copy(ys_ref.at[_row_tile(src)], buf_ref.at[k, _row_tile(t)], sem).start(priority=k % 2)
        return carry

    lax.fori_loop(0, R, issue, 0)
    h = _load_rows(h_ref, R)
    hb = h.astype(jnp.bfloat16)
    act = (_silu(_dot(hb, wsg_ref[...])) * _dot(hb, wsu_ref[...])).astype(jnp.bfloat16)
    ffn = _dot(act, wsd_ref[...])
    pltpu.make_async_copy(buf_ref, buf_ref, sem).wait()
    w = w_ref[...]
    for k in range(TOP_K):
        ffn = ffn + _load_rows(buf_ref, R, lead=(k,)) * w[:, k:k + 1]
    out_ref[...] = _layer_norm(DEEPNORM_ALPHA * h + ffn, g_ref[...], b_ref[...])


def _combine(h2, pos, top_w, ys, ws_gate, ws_up, ws_down, ln_g, ln_b):
    T, D = h2.shape[0] // ROW_TILE, D_MODEL
    R = COMBINE_ROWS
    F = SHARED_DIM
    bf16 = jnp.bfloat16
    const = lambda shape: pl.BlockSpec(shape, lambda i: (0,) * len(shape))
    return pl.pallas_call(
        _combine_kernel,
        grid=(T // R,),
        in_specs=[
            pl.BlockSpec((R * TOP_K,), lambda i: (i,), memory_space=pltpu.SMEM),
            pl.BlockSpec((R * ROW_TILE, LANES), lambda i: (i, 0)),
            pl.BlockSpec((R, TOP_K), lambda i: (i, 0)),
            pl.BlockSpec(memory_space=pl.ANY),
            const((D, F)), const((D, F)), const((F, D)), const((1, D)), const((1, D)),
        ],
        out_specs=pl.BlockSpec((R, D), lambda i: (i, 0)),
        scratch_shapes=[pltpu.VMEM((TOP_K, R * ROW_TILE, LANES), jnp.float32), pltpu.SemaphoreType.DMA(())],
        out_shape=jax.ShapeDtypeStruct((T, D), jnp.float32),
        compiler_params=pltpu.CompilerParams(
            dimension_semantics=("arbitrary",), vmem_limit_bytes=VMEM_LIMIT_BYTES),
    )(pos, h2, top_w, ys, ws_gate.astype(bf16), ws_up.astype(bf16),
      ws_down.astype(bf16), ln_g.reshape(1, D), ln_b.reshape(1, D))


def _moe(h2, w_router, router_bias, w_gate, w_up, w_down, ws_gate, ws_up, ws_down, ln_g, ln_b):
    T = h2.shape[0] // ROW_TILE
    E = N_EXPERTS
    BM = EXPERT_ROWS
    e_idx, top_w, rank, counts = _router(h2, w_router, router_bias)
    cnt = counts.reshape(E).astype(jnp.int32)
    nblk = (cnt + BM - 1) // BM
    blk_end = jnp.cumsum(nblk)
    row_start = ((blk_end - nblk) * BM).astype(jnp.int32)
    n_blocks = T * TOP_K // BM + E
    n_act = blk_end[-1:].astype(jnp.int32)
    blk_ids = jnp.minimum(jnp.arange(n_blocks, dtype=jnp.int32), n_act[0] - 1)
    blk_e = jnp.minimum(jnp.searchsorted(blk_end, blk_ids, side="right"), E - 1).astype(jnp.int32)
    pos = _positions(e_idx, rank, row_start)
    xs = _dispatch(h2, pos, row_start, cnt, n_act, n_blocks * BM)
    ys = _experts(xs, blk_e, n_act, w_gate, w_up, w_down)
    return _combine(h2, pos, top_w, ys, ws_gate, ws_up, ws_down, ln_g, ln_b)


def kernel(x, ln_in_g, ln_in_b, w_in, w_out, rel_bias, attn_sinks, ln_mix_g, ln_mix_b, w_router,
           router_bias, w_gate, w_up, w_down, ws_gate, ws_up, ws_down, ln_ffn_g, ln_ffn_b):
    B, S, D = x.shape
    h = _mixer(x, ln_in_g, ln_in_b, w_in[0], w_out[0], rel_bias, attn_sinks[0], ln_mix_g[0], ln_mix_b[0])
    out = _moe(h, w_router[0], router_bias[0], w_gate[0], w_up[0], w_down[0],
               ws_gate[0], ws_up[0], ws_down[0], ln_ffn_g[0], ln_ffn_b[0])
    return out.reshape(B, S, D)
```

```python
import functools
import math

import jax
import jax.numpy as jnp
from jax import lax
from jax.experimental import pallas as pl
from jax.experimental.pallas import tpu as pltpu

D_MODEL = 1024
DEPTH = 1
RET_HEADS = 4
RET_QK_DIM = 64
RET_V_DIM = 128
RET_CHUNK = 128
RET_WIDTH = RET_HEADS * RET_V_DIM
ROPE_BASE = 10000.0
SWA_HEADS = 8
SWA_KV_HEADS = 2
SWA_GROUP = SWA_HEADS // SWA_KV_HEADS
SWA_HEAD_DIM = 64
SWA_WINDOW = 128
SWA_WIDTH = SWA_HEADS * SWA_HEAD_DIM
MIX_WIDTH = RET_WIDTH + SWA_WIDTH
RQK = RET_HEADS * RET_QK_DIM
SKV = SWA_KV_HEADS * SWA_HEAD_DIM
IN_SIZES = (RQK, RQK, RET_WIDTH, RET_WIDTH, SWA_WIDTH, SKV, SKV)
IN_OFFS = tuple(sum(IN_SIZES[:i]) for i in range(len(IN_SIZES)))
IN_WIDTH = sum(IN_SIZES)
REL_BUCKETS = 32
REL_MAX_DIST = 128
N_EXPERTS = 256
TOP_K = 8
N_GROUPS = 8
GROUP_SIZE = N_EXPERTS // N_GROUPS
TOPK_GROUPS = 4
EXPERT_DIM = 256
SHARED_DIM = 256
ROUTED_SCALE = 2.5
LN_EPS = 1e-5
GN_EPS = 1e-6
DEEPNORM_ALPHA = (2 * DEPTH) ** 0.25
MASK_VALUE = -1e30

VMEM_LIMIT_BYTES = 56 * 1024 * 1024

MIX_ROWS = 256
ROUTE_ROWS = 256
DISPATCH_ROWS = 256
EXPERT_ROWS = 256
COMBINE_ROWS = 128


def _layer_norm(x, g, b):
    mu = jnp.mean(x, axis=-1, keepdims=True)
    xc = x - mu
    var = jnp.mean(xc * xc, axis=-1, keepdims=True)
    return xc * lax.rsqrt(var + LN_EPS) * g + b


def _dot(a, b):
    return jnp.dot(a, b, preferred_element_type=jnp.float32)


def _dot_nt(a, b):
    return lax.dot_general(a, b, (((1,), (1,)), ((), ())), preferred_element_type=jnp.float32)


def _dot_tn(a, b):
    return lax.dot_general(a, b, (((0,), (0,)), ((), ())), preferred_element_type=jnp.float32)


def _silu(x):
    return x * (1.0 / (1.0 + jnp.exp(-x)))


LANES = 128
ROW_TILE = D_MODEL // LANES


def _load_rows(ref, n_rows, lead=()):
    return jnp.concatenate([ref[lead + (pl.ds(s, n_rows, stride=ROW_TILE), slice(None))]
                            for s in range(ROW_TILE)], axis=1)


def _store_rows(ref, val, lead=()):
    n_rows = val.shape[0]
    for s in range(ROW_TILE):
        ref[lead + (pl.ds(s, n_rows, stride=ROW_TILE), slice(None))] = val[:, s * LANES:(s + 1) * LANES]


def _row_tile(r):
    return pl.ds(pl.multiple_of(r * ROW_TILE, ROW_TILE), ROW_TILE)


def _swap_halves(x):
    n = x.shape[-1]
    half = RET_QK_DIM // 2
    lane = lax.broadcasted_iota(jnp.int32, x.shape, 1)
    from_right = pltpu.roll(x, n - half, axis=1)
    from_left = pltpu.roll(x, half, axis=1)
    return jnp.where((lane % RET_QK_DIM) < half, from_right, from_left)


def _mixer_kernel(rel_bias_ref, x_ref, g_in_ref, b_in_ref, w_in_ref, w_out_ref, rot_ref, decay_ref,
                  zeta_ref, xi_ref, cdecay_ref, bucket_ref, sink_ref, g_mix_ref, b_mix_ref,
                  h2_ref, state_ref, kprev_ref, vprev_ref, bias_ref):
    b_id = pl.program_id(0)
    c_id = pl.program_id(1)
    W = SWA_WINDOW

    @pl.when((b_id == 0) & (c_id == 0))
    def _build_bias():
        bucket = bucket_ref[...]
        for h in range(SWA_HEADS):
            acc = jnp.full((2 * W, W), MASK_VALUE, jnp.float32)
            for b in range(REL_BUCKETS):
                acc = jnp.where(bucket == b, rel_bias_ref[b, h], acc)
            kh, g = divmod(h, SWA_GROUP)
            bias_ref[kh, :, g * W:(g + 1) * W] = acc

    @pl.when(c_id == 0)
    def _reset():
        state_ref[...] = jnp.zeros_like(state_ref)
        kprev_ref[...] = jnp.zeros_like(kprev_ref)
        vprev_ref[...] = jnp.zeros_like(vprev_ref)

    h = _layer_norm(x_ref[...], g_in_ref[...], b_in_ref[...])
    proj = _dot(h.astype(jnp.bfloat16), w_in_ref[...])

    o_q, o_k, o_v, o_g, o_sq, o_sk, o_sv = IN_OFFS
    cos_t = rot_ref[:, :RQK]
    sin_t = rot_ref[:, RQK:]
    q_all = proj[:, o_q:o_q + RQK]
    k_all = proj[:, o_k:o_k + RQK]
    q_rot = q_all * cos_t + _swap_halves(q_all) * sin_t
    k_rot = (k_all * cos_t + _swap_halves(k_all) * sin_t) * (RET_QK_DIM ** -0.5)

    n_sub = x_ref.shape[0] // RET_CHUNK
    states = [state_ref[hh] for hh in range(RET_HEADS)]
    k_prev = kprev_ref[...]
    v_prev = vprev_ref[...]
    cat_rows = []
    for s in range(n_sub):
        r0 = s * RET_CHUNK
        rows = slice(r0, r0 + RET_CHUNK)
        pieces = []
        for hh in range(RET_HEADS):
            qk = slice(hh * RET_QK_DIM, (hh + 1) * RET_QK_DIM)
            vv = slice(o_v + hh * RET_V_DIM, o_v + (hh + 1) * RET_V_DIM)
            gg = slice(o_g + hh * RET_V_DIM, o_g + (hh + 1) * RET_V_DIM)
            q = q_rot[rows, qk].astype(jnp.bfloat16)
            k32 = k_rot[rows, qk]
            v = proj[rows, vv].astype(jnp.bfloat16)
            scores = _dot_nt(q, k32.astype(jnp.bfloat16)) * decay_ref[hh]
            intra = _dot(scores.astype(jnp.bfloat16), v)
            inter = _dot(q, states[hh].astype(jnp.bfloat16)) * xi_ref[hh]
            ret = intra + inter
            kz = (k32 * zeta_ref[hh]).astype(jnp.bfloat16)
            states[hh] = states[hh] * cdecay_ref[hh] + _dot_tn(kz, v)
            mu = jnp.mean(ret, axis=-1, keepdims=True)
            rc = ret - mu
            var = jnp.mean(rc * rc, axis=-1, keepdims=True)
            normed = rc * lax.rsqrt(var + GN_EPS)
            pieces.append((_silu(proj[rows, gg]) * normed).astype(jnp.bfloat16))
        k_cur = proj[rows, o_sk:o_sk + SKV].astype(jnp.bfloat16)
        v_cur = proj[rows, o_sv:o_sv + SKV].astype(jnp.bfloat16)
        for kh in range(SWA_KV_HEADS):
            kv = slice(kh * SWA_HEAD_DIM, (kh + 1) * SWA_HEAD_DIM)
            q4 = jnp.concatenate(
                [proj[rows, o_sq + (kh * SWA_GROUP + g) * SWA_HEAD_DIM:
                      o_sq + (kh * SWA_GROUP + g + 1) * SWA_HEAD_DIM] for g in range(SWA_GROUP)],
                axis=0) * (SWA_HEAD_DIM ** -0.5)
            kcat = jnp.concatenate([k_prev[:, kv], k_cur[:, kv]], axis=0)
            vcat = jnp.concatenate([v_prev[:, kv], v_cur[:, kv]], axis=0)
            logits = _dot_nt(kcat, q4.astype(jnp.bfloat16)) + bias_ref[kh]
            if s == 0:
                key = lax.broadcasted_iota(jnp.int32, logits.shape, 0)
                logits = logits + jnp.where((key < W) & (c_id == 0), MASK_VALUE, 0.0)
            sink = sink_ref[kh]
            m = jnp.maximum(jnp.max(logits, axis=0, keepdims=True), sink)
            p = jnp.exp(logits - m)
            den = jnp.sum(p, axis=0, keepdims=True) + jnp.exp(sink - m)
            probs = (p / den).astype(jnp.bfloat16)
            o4 = _dot_tn(vcat, probs)
            pieces.extend(o4[:, g * W:(g + 1) * W].T.astype(jnp.bfloat16) for g in range(SWA_GROUP))
        k_prev, v_prev = k_cur, v_cur
        cat_rows.append(jnp.concatenate(pieces, axis=1))
    for hh in range(RET_HEADS):
        state_ref[hh] = states[hh]
    kprev_ref[...] = k_prev
    vprev_ref[...] = v_prev

    mix = _dot(jnp.concatenate(cat_rows, axis=0), w_out_ref[...])
    _store_rows(h2_ref, _layer_norm(DEEPNORM_ALPHA * h + mix, g_mix_ref[...], b_mix_ref[...]))


def _t5_bucket(dist):
    n = jnp.maximum(dist, 0)
    max_exact = REL_BUCKETS // 2
    ratio = jnp.log(jnp.maximum(n, 1).astype(jnp.float32) / max_exact) / math.log(REL_MAX_DIST / max_exact)
    large = jnp.minimum(max_exact + (ratio * (REL_BUCKETS - max_exact)).astype(jnp.int32), REL_BUCKETS - 1)
    return jnp.where(n < max_exact, n, large)


def _mixer(x, ln_in_g, ln_in_b, w_in, w_out, rel_bias, sinks, ln_mix_g, ln_mix_b):
    B, S, D = x.shape
    R = MIX_ROWS
    C = RET_CHUNK
    W = SWA_WINDOW
    f32 = jnp.float32
    half = RET_QK_DIM // 2
    inv = ROPE_BASE ** (-jnp.arange(half, dtype=f32) / half)
    ang = jnp.arange(S, dtype=f32)[:, None] * inv[None, :]
    cos, sin = jnp.cos(ang), jnp.sin(ang)
    cos_t = jnp.tile(jnp.concatenate([cos, cos], axis=-1), (1, RET_HEADS))
    sin_t = jnp.tile(jnp.concatenate([-sin, sin], axis=-1), (1, RET_HEADS))
    rot = jnp.concatenate([cos_t, sin_t], axis=-1)
    log_gamma = jnp.log(1.0 - 2.0 ** (-5.0 - jnp.arange(RET_HEADS, dtype=f32)))
    idx = jnp.arange(C, dtype=f32)
    diff = idx[:, None] - idx[None, :]
    decay = jnp.where(diff[None] >= 0, jnp.exp(jnp.maximum(diff, 0.0)[None] * log_gamma[:, None, None]), 0.0)
    zeta = jnp.exp((C - 1.0 - idx)[None, :] * log_gamma[:, None])
    xi = jnp.exp((idx + 1.0)[None, :] * log_gamma[:, None])
    zeta_b = jnp.broadcast_to(zeta[:, :, None], (RET_HEADS, C, RET_QK_DIM))
    xi_b = jnp.broadcast_to(xi[:, :, None], (RET_HEADS, C, RET_V_DIM))
    cdecay = jnp.broadcast_to(jnp.exp(C * log_gamma)[:, None, None], (RET_HEADS, RET_QK_DIM, RET_V_DIM))
    i = jnp.arange(W)
    j = jnp.arange(2 * W)
    dist = i[:, None] + W - j[None, :]
    bucket = jnp.where((dist >= 0) & (dist < W), _t5_bucket(dist), -1).astype(jnp.int32).T
    sink_row = jnp.repeat(sinks.astype(f32), W).reshape(SWA_KV_HEADS, 1, SWA_GROUP * W)

    const = lambda shape: pl.BlockSpec(shape, lambda b, c, *_: (0,) * len(shape))
    grid_spec = pltpu.PrefetchScalarGridSpec(
        num_scalar_prefetch=1,
        grid=(B, S // R),
        in_specs=[
            pl.BlockSpec((None, R, D), lambda b, c, *_: (b, c, 0)),
            const((1, D)), const((1, D)),
            const((D, IN_WIDTH)), const((MIX_WIDTH, D)),
            pl.BlockSpec((R, 2 * RQK), lambda b, c, *_: (c, 0)),
            const((RET_HEADS, C, C)), const((RET_HEADS, C, RET_QK_DIM)), const((RET_HEADS, C, RET_V_DIM)),
            const((RET_HEADS, RET_QK_DIM, RET_V_DIM)),
            const((2 * W, W)), const((SWA_KV_HEADS, 1, SWA_GROUP * W)),
            const((1, D)), const((1, D)),
        ],
        out_specs=pl.BlockSpec((R * ROW_TILE, LANES), lambda b, c, *_: (b * (S // R) + c, 0)),
        scratch_shapes=[
            pltpu.VMEM((RET_HEADS, RET_QK_DIM, RET_V_DIM), f32),
            pltpu.VMEM((W, SKV), jnp.bfloat16),
            pltpu.VMEM((W, SKV), jnp.bfloat16),
            pltpu.VMEM((SWA_KV_HEADS, 2 * W, SWA_GROUP * W), f32),
        ],
    )
    return pl.pallas_call(
        _mixer_kernel,
        grid_spec=grid_spec,
        out_shape=jax.ShapeDtypeStruct((B * S * ROW_TILE, LANES), f32),
        compiler_params=pltpu.CompilerParams(
            dimension_semantics=("arbitrary", "arbitrary"), vmem_limit_bytes=VMEM_LIMIT_BYTES),
    )(rel_bias.astype(f32), x, ln_in_g.reshape(1, D), ln_in_b.reshape(1, D),
      w_in.astype(jnp.bfloat16), w_out.astype(jnp.bfloat16), rot, decay, zeta_b, xi_b, cdecay,
      bucket, sink_row, ln_mix_g.reshape(1, D), ln_mix_b.reshape(1, D))


def _router_kernel(h_ref, wr_ref, rb_ref, e_ref, w_ref, rk_ref, cnt_ref, run_ref):
    f32 = jnp.float32
    R = h_ref.shape[0] // ROW_TILE
    E = N_EXPERTS
    neg = -jnp.inf

    @pl.when(pl.program_id(0) == 0)
    def _init():
        run_ref[...] = jnp.zeros_like(run_ref)

    logits = jnp.dot(_load_rows(h_ref, R), wr_ref[...], precision=lax.Precision.HIGHEST, preferred_element_type=f32)
    scores = 1.0 / (1.0 + jnp.exp(-logits))
    choice = scores + rb_ref[...]
    lane = lax.broadcasted_iota(jnp.int32, (R, E), 1)
    grp = lane // GROUP_SIZE

    def first_argmax(vals):
        m = jnp.max(vals, axis=-1, keepdims=True)
        idx = jnp.min(jnp.where(vals == m, lane, E), axis=-1, keepdims=True)
        return m, idx

    gscore = []
    for g in range(N_GROUPS):
        vals = jnp.where(grp == g, choice, neg)
        m1, i1 = first_argmax(vals)
        m2 = jnp.max(jnp.where(lane == i1, neg, vals), axis=-1, keepdims=True)
        gscore.append(m1 + m2)
    keep = jnp.zeros((R, E), f32)
    for g in range(N_GROUPS):
        beaten = jnp.zeros((R, 1), f32)
        for g2 in range(N_GROUPS):
            if g2 == g:
                continue
            ahead = (gscore[g2] > gscore[g]) | (gscore[g2] == gscore[g]) if g2 < g else gscore[g2] > gscore[g]
            beaten = beaten + jnp.where(ahead, 1.0, 0.0)
        keep = jnp.where(grp == g, jnp.where(beaten < TOPK_GROUPS, 1.0, 0.0), keep)
    masked = jnp.where(keep > 0.0, choice, neg)

    idxs, wts = [], []
    for _ in range(TOP_K):
        _, idx = first_argmax(masked)
        hit = lane == idx
        idxs.append(idx)
        wts.append(jnp.sum(jnp.where(hit, scores, 0.0), axis=-1, keepdims=True))
        masked = jnp.where(hit, neg, masked)
    wsum = wts[0]
    for k in range(1, TOP_K):
        wsum = wsum + wts[k]

    picked = jnp.zeros((R, E), f32)
    for k in range(TOP_K):
        picked = jnp.where(lane == idxs[k], 1.0, picked)
    row = lax.broadcasted_iota(jnp.int32, (R, R), 0)
    colr = lax.broadcasted_iota(jnp.int32, (R, R), 1)
    tri = jnp.where(colr < row, 1.0, 0.0).astype(jnp.bfloat16)
    before = _dot(tri, picked.astype(jnp.bfloat16)) + run_ref[...]
    lane_k = lax.broadcasted_iota(jnp.int32, (R, TOP_K), 1)
    e_out = jnp.zeros((R, TOP_K), jnp.int32)
    w_out = jnp.zeros((R, TOP_K), f32)
    rk_out = jnp.zeros((R, TOP_K), jnp.int32)
    for k in range(TOP_K):
        rank_k = jnp.sum(jnp.where(lane == idxs[k], before, 0.0), axis=-1, keepdims=True)
        e_out = jnp.where(lane_k == k, idxs[k], e_out)
        w_out = jnp.where(lane_k == k, wts[k] / wsum * ROUTED_SCALE, w_out)
        rk_out = jnp.where(lane_k == k, rank_k.astype(jnp.int32), rk_out)
    e_ref[...] = e_out
    w_ref[...] = w_out
    rk_ref[...] = rk_out
    run_ref[...] = run_ref[...] + jnp.sum(picked, axis=0, keepdims=True)
    cnt_ref[...] = run_ref[...]


def _router(h2, w_router, router_bias):
    T, D = h2.shape[0] // ROW_TILE, D_MODEL
    R = ROUTE_ROWS
    E = N_EXPERTS
    return pl.pallas_call(
        _router_kernel,
        grid=(T // R,),
        in_specs=[
            pl.BlockSpec((R * ROW_TILE, LANES), lambda i: (i, 0)),
            pl.BlockSpec((D, E), lambda i: (0, 0)),
            pl.BlockSpec((1, E), lambda i: (0, 0)),
        ],
        out_specs=[
            pl.BlockSpec((R, TOP_K), lambda i: (i, 0)),
            pl.BlockSpec((R, TOP_K), lambda i: (i, 0)),
            pl.BlockSpec((R, TOP_K), lambda i: (i, 0)),
            pl.BlockSpec((1, E), lambda i: (0, 0)),
        ],
        out_shape=[
            jax.ShapeDtypeStruct((T, TOP_K), jnp.int32),
            jax.ShapeDtypeStruct((T, TOP_K), jnp.float32),
            jax.ShapeDtypeStruct((T, TOP_K), jnp.int32),
            jax.ShapeDtypeStruct((1, E), jnp.float32),
        ],
        scratch_shapes=[pltpu.VMEM((1, E), jnp.float32)],
        compiler_params=pltpu.CompilerParams(
            dimension_semantics=("arbitrary",), vmem_limit_bytes=VMEM_LIMIT_BYTES),
    )(h2, w_router, router_bias.reshape(1, E).astype(jnp.float32))


def _positions_kernel(row_start_ref, e_ref, rk_ref, pos_ref):
    e = e_ref[...]
    rk = rk_ref[...]

    def per_expert(i, pos):
        return jnp.where(e == i, rk + row_start_ref[i], pos)

    pos_ref[...] = lax.fori_loop(0, N_EXPERTS, per_expert, jnp.zeros_like(rk))


def _positions(e_idx, rank, row_start):
    n = e_idx.size
    shape = (n // LANES, LANES)
    grid_spec = pltpu.PrefetchScalarGridSpec(
        num_scalar_prefetch=1,
        grid=(1,),
        in_specs=[pl.BlockSpec(shape, lambda i, *_: (0, 0)), pl.BlockSpec(shape, lambda i, *_: (0, 0))],
        out_specs=pl.BlockSpec(shape, lambda i, *_: (0, 0)),
    )
    pos = pl.pallas_call(
        _positions_kernel,
        grid_spec=grid_spec,
        out_shape=jax.ShapeDtypeStruct(shape, jnp.int32),
    )(row_start, e_idx.reshape(shape), rank.reshape(shape))
    return pos.reshape(n)


def _dispatch_kernel(row_start_ref, cnt_ref, n_act_ref, pos_ref, h_ref, xs_ref, zero_ref, sem, zsem,
                     *, n_tokens):
    R = h_ref.shape[0] // ROW_TILE
    BM = EXPERT_ROWS
    n_blocks = xs_ref.shape[0] // (BM * ROW_TILE)
    n_pad_units = n_blocks - n_tokens * TOP_K // BM

    @pl.when(pl.program_id(0) == 0)
    def _zero_padding():
        zero_ref[...] = jnp.zeros_like(zero_ref)

        def expert_tail(e, carry):
            n_tail = pl.multiple_of(((BM - cnt_ref[e] % BM) % BM) * ROW_TILE, ROW_TILE)

            @pl.when(n_tail > 0)
            def _():
                dst = pl.multiple_of((row_start_ref[e] + cnt_ref[e]) * ROW_TILE, ROW_TILE)
                pltpu.make_async_copy(zero_ref.at[pl.ds(0, n_tail)], xs_ref.at[pl.ds(dst, n_tail)], zsem).start()
            return carry

        lax.fori_loop(0, N_EXPERTS, expert_tail, 0)

        def idle_block(i, carry):
            dst = pl.multiple_of(i * (BM * ROW_TILE), BM * ROW_TILE)
            pltpu.make_async_copy(zero_ref, xs_ref.at[pl.ds(dst, BM * ROW_TILE)], zsem).start()
            return carry

        lax.fori_loop(n_act_ref[0], n_blocks, idle_block, 0)

        def drain(i, carry):
            pltpu.make_async_copy(zero_ref, xs_ref.at[pl.ds(0, BM * ROW_TILE)], zsem).wait()
            return carry

        lax.fori_loop(0, n_pad_units, drain, 0)

    def issue(t, carry):
        src = h_ref.at[_row_tile(t)]
        for k in range(TOP_K):
            dest = pos_ref[t * TOP_K + k]
            pltpu.make_async_copy(src, xs_ref.at[_row_tile(dest)], sem).start(priority=k % 2)
        return carry

    lax.fori_loop(0, R, issue, 0)
    n = R * TOP_K * ROW_TILE
    pltpu.make_async_copy(xs_ref.at[pl.ds(0, n)], xs_ref.at[pl.ds(0, n)], sem).wait()


def _dispatch(h2, pos, row_start, cnt, n_act, n_rows):
    T = h2.shape[0] // ROW_TILE
    R = DISPATCH_ROWS
    grid_spec = pltpu.PrefetchScalarGridSpec(
        num_scalar_prefetch=3,
        grid=(T // R,),
        in_specs=[
            pl.BlockSpec((R * TOP_K,), lambda i, *_: (i,), memory_space=pltpu.SMEM),
            pl.BlockSpec((R * ROW_TILE, LANES), lambda i, *_: (i, 0)),
        ],
        out_specs=pl.BlockSpec(memory_space=pl.ANY),
        scratch_shapes=[pltpu.VMEM((EXPERT_ROWS * ROW_TILE, LANES), jnp.float32),
                        pltpu.SemaphoreType.DMA(()), pltpu.SemaphoreType.DMA(())],
    )
    return pl.pallas_call(
        functools.partial(_dispatch_kernel, n_tokens=T),
        grid_spec=grid_spec,
        out_shape=jax.ShapeDtypeStruct((n_rows * ROW_TILE, LANES), jnp.float32),
        compiler_params=pltpu.CompilerParams(
            dimension_semantics=("arbitrary",), vmem_limit_bytes=VMEM_LIMIT_BYTES),
    )(row_start, cnt, n_act, pos, h2)


X_SLOTS = 4
Y_SLOTS = 3


def _experts_kernel(blk_e_ref, first_ref, slot_ref, next_e_ref, n_act_ref, xs_hbm, wg_hbm, wu_hbm, wd_hbm,
                    ys_hbm, x_buf, y_buf, wg_buf, wu_buf, wd_buf, wg_bf, wu_bf, wd_bf, sems, x_sems, y_sems):
    i = pl.program_id(0)
    n_act = n_act_ref[0]
    bf16 = jnp.bfloat16
    blk = EXPERT_ROWS * ROW_TILE

    def block_rows(j):
        return pl.ds(pl.multiple_of(j * blk, blk), blk)

    def x_copy(j):
        return pltpu.make_async_copy(xs_hbm.at[block_rows(j)], x_buf.at[j % X_SLOTS], x_sems.at[j % X_SLOTS])

    def y_copy(j):
        return pltpu.make_async_copy(y_buf.at[j % Y_SLOTS], ys_hbm.at[block_rows(j)], y_sems.at[j % Y_SLOTS])

    @pl.when(i == 0)
    def _prime():
        for j in range(X_SLOTS - 1):
            @pl.when(j < n_act)
            def _():
                x_copy(j).start()

    @pl.when(i + (X_SLOTS - 1) < n_act)
    def _prefetch():
        x_copy(i + (X_SLOTS - 1)).start()

    def weight_copies(e, slot):
        return (pltpu.make_async_copy(wg_hbm.at[e], wg_buf.at[slot], sems.at[slot]),
                pltpu.make_async_copy(wu_hbm.at[e], wu_buf.at[slot], sems.at[slot]),
                pltpu.make_async_copy(wd_hbm.at[e], wd_buf.at[slot], sems.at[slot]))

    @pl.when((i < n_act_ref[0]) & (first_ref[i] == 1))
    def _new_expert():
        slot = slot_ref[i]

        @pl.when(i == 0)
        def _():
            for c in weight_copies(blk_e_ref[0], 0):
                c.start()

        for c in weight_copies(blk_e_ref[i], slot):
            c.wait()

        @pl.when(next_e_ref[i] >= 0)
        def _():
            for c in weight_copies(next_e_ref[i], 1 - slot):
                c.start()

        wg_bf[...] = wg_buf[slot].astype(bf16)
        wu_bf[...] = wu_buf[slot].astype(bf16)
        wd_bf[...] = wd_buf[slot].astype(bf16)

    @pl.when(i < n_act)
    def _compute():
        x_copy(i).wait()
        x = _load_rows(x_buf, EXPERT_ROWS, lead=(i % X_SLOTS,)).astype(bf16)
        g = _dot(x, wg_bf[...])
        u = _dot(x, wu_bf[...])
        a = (_silu(g) * u).astype(bf16)
        y = _dot(a, wd_bf[...])

        @pl.when(i >= Y_SLOTS)
        def _():
            y_copy(i - Y_SLOTS).wait()

        _store_rows(y_buf, y, lead=(i % Y_SLOTS,))
        y_copy(i).start()

    @pl.when(i == n_act - 1)
    def _drain():
        for d in range(Y_SLOTS):
            @pl.when(i - d >= 0)
            def _():
                y_copy(i - d).wait()


def _experts(xs, blk_e, n_act, w_gate, w_up, w_down):
    D = D_MODEL
    BM = EXPERT_ROWS
    F = EXPERT_DIM
    n_blocks = xs.shape[0] // (BM * ROW_TILE)
    ids = jnp.arange(n_blocks, dtype=jnp.int32)
    active = ids < n_act[0]
    first = active & ((ids == 0) | (blk_e != jnp.roll(blk_e, 1)))
    slot = ((jnp.cumsum(first.astype(jnp.int32)) - 1) % 2).astype(jnp.int32)
    first_pos = jnp.where(first, ids, n_blocks)
    later_first = lax.cummin(jnp.concatenate([first_pos[1:], jnp.full((1,), n_blocks, jnp.int32)]), reverse=True)
    next_e = jnp.where(later_first < n_blocks, blk_e[jnp.minimum(later_first, n_blocks - 1)], -1).astype(jnp.int32)

    grid_spec = pltpu.PrefetchScalarGridSpec(
        num_scalar_prefetch=5,
        grid=(n_blocks,),
        in_specs=[pl.BlockSpec(memory_space=pl.ANY)] * 4,
        out_specs=pl.BlockSpec(memory_space=pl.ANY),
        scratch_shapes=[
            pltpu.VMEM((X_SLOTS, BM * ROW_TILE, LANES), jnp.float32),
            pltpu.VMEM((Y_SLOTS, BM * ROW_TILE, LANES), jnp.float32),
            pltpu.VMEM((2, D, F), jnp.float32), pltpu.VMEM((2, D, F), jnp.float32),
            pltpu.VMEM((2, F, D), jnp.float32),
            pltpu.VMEM((D, F), jnp.bfloat16), pltpu.VMEM((D, F), jnp.bfloat16),
            pltpu.VMEM((F, D), jnp.bfloat16),
            pltpu.SemaphoreType.DMA((2,)), pltpu.SemaphoreType.DMA((X_SLOTS,)),
            pltpu.SemaphoreType.DMA((Y_SLOTS,)),
        ],
    )
    return pl.pallas_call(
        _experts_kernel,
        grid_spec=grid_spec,
        out_shape=jax.ShapeDtypeStruct(xs.shape, jnp.float32),
        input_output_aliases={5: 0},
        compiler_params=pltpu.CompilerParams(
            dimension_semantics=("arbitrary",), vmem_limit_bytes=VMEM_LIMIT_BYTES),
    )(blk_e, first.astype(jnp.int32), slot, next_e, n_act, xs, w_gate, w_up, w_down)


def _combine_kernel(pos_ref, h_ref, w_ref, ys_ref, wsg_ref, wsu_ref, wsd_ref,
                    g_ref, b_ref, out_ref, buf_ref, sem):
    R = h_ref.shape[0] // ROW_TILE

    def issue(t, carry):
        for k in range(TOP_K):
            src = pos_ref[t * TOP_K + k]
            pltpu.make_async_copy(ys_ref.at[_row_tile(src)], buf_ref.at[k, _row_tile(t)], sem).start(priority=k % 2)
        return carry

    lax.fori_loop(0, R, issue, 0)
    h = _load_rows(h_ref, R)
    hb = h.astype(jnp.bfloat16)
    act = (_silu(_dot(hb, wsg_ref[...])) * _dot(hb, wsu_ref[...])).astype(jnp.bfloat16)
    ffn = _dot(act, wsd_ref[...])
    pltpu.make_async_copy(buf_ref, buf_ref, sem).wait()
    w = w_ref[...]
    for k in range(TOP_K):
        ffn = ffn + _load_rows(buf_ref, R, lead=(k,)) * w[:, k:k + 1]
    out_ref[...] = _layer_norm(DEEPNORM_ALPHA * h + ffn, g_ref[...], b_ref[...])


def _combine(h2, pos, top_w, ys, ws_gate, ws_up, ws_down, ln_g, ln_b):
    T, D = h2.shape[0] // ROW_TILE, D_MODEL
    R = COMBINE_ROWS
    F = SHARED_DIM
    bf16 = jnp.bfloat16
    const = lambda shape: pl.BlockSpec(shape, lambda i: (0,) * len(shape))
    return pl.pallas_call(
        _combine_kernel,
        grid=(T // R,),
        in_specs=[
            pl.BlockSpec((R * TOP_K,), lambda i: (i,), memory_space=pltpu.SMEM),
            pl.BlockSpec((R * ROW_TILE, LANES), lambda i: (i, 0)),
            pl.BlockSpec((R, TOP_K), lambda i: (i, 0)),
            pl.BlockSpec(memory_space=pl.ANY),
            const((D, F)), const((D, F)), const((F, D)), const((1, D)), const((1, D)),
        ],
        out_specs=pl.BlockSpec((R, D), lambda i: (i, 0)),
        scratch_shapes=[pltpu.VMEM((TOP_K, R * ROW_TILE, LANES), jnp.float32), pltpu.SemaphoreType.DMA(())],
        out_shape=jax.ShapeDtypeStruct((T, D), jnp.float32),
        compiler_params=pltpu.CompilerParams(
            dimension_semantics=("arbitrary",), vmem_limit_bytes=VMEM_LIMIT_BYTES),
    )(pos, h2, top_w, ys, ws_gate.astype(bf16), ws_up.astype(bf16),
      ws_down.astype(bf16), ln_g.reshape(1, D), ln_b.reshape(1, D))


def _moe(h2, w_router, router_bias, w_gate, w_up, w_down, ws_gate, ws_up, ws_down, ln_g, ln_b):
    T = h2.shape[0] // ROW_TILE
    E = N_EXPERTS
    BM = EXPERT_ROWS
    e_idx, top_w, rank, counts = _router(h2, w_router, router_bias)
    cnt = counts.reshape(E).astype(jnp.int32)
    nblk = (cnt + BM - 1) // BM
    blk_end = jnp.cumsum(nblk)
    row_start = ((blk_end - nblk) * BM).astype(jnp.int32)
    n_blocks = T * TOP_K // BM + E
    n_act = blk_end[-1:].astype(jnp.int32)
    blk_ids = jnp.minimum(jnp.arange(n_blocks, dtype=jnp.int32), n_act[0] - 1)
    blk_e = jnp.minimum(jnp.sum(blk_end[None, :] <= blk_ids[:, None], axis=1), E - 1).astype(jnp.int32)
    pos = _positions(e_idx, rank, row_start)
    xs = _dispatch(h2, pos, row_start, cnt, n_act, n_blocks * BM)
    ys = _experts(xs, blk_e, n_act, w_gate, w_up, w_down)
    return _combine(h2, pos, top_w, ys, ws_gate, ws_up, ws_down, ln_g, ln_b)


def kernel(x, ln_in_g, ln_in_b, w_in, w_out, rel_bias, attn_sinks, ln_mix_g, ln_mix_b, w_router,
           router_bias, w_gate, w_up, w_down, ws_gate, ws_up, ws_down, ln_ffn_g, ln_ffn_b):
    B, S, D = x.shape
    h = _mixer(x, ln_in_g, ln_in_b, w_in[0], w_out[0], rel_bias, attn_sinks[0], ln_mix_g[0], ln_mix_b[0])
    out = _moe(h, w_router[0], router_bias[0], w_gate[0], w_up[0], w_down[0],
               ws_gate[0], ws_up[0], ws_down[0], ln_ffn_g[0], ln_ffn_b[0])
    return out.reshape(B, S, D)
```

```python
import functools
import math

import jax
import jax.numpy as jnp
from jax import lax
from jax.experimental import pallas as pl
from jax.experimental.pallas import tpu as pltpu

D_MODEL = 1024
DEPTH = 1
RET_HEADS = 4
RET_QK_DIM = 64
RET_V_DIM = 128
RET_CHUNK = 128
RET_WIDTH = RET_HEADS * RET_V_DIM
ROPE_BASE = 10000.0
SWA_HEADS = 8
SWA_KV_HEADS = 2
SWA_GROUP = SWA_HEADS // SWA_KV_HEADS
SWA_HEAD_DIM = 64
SWA_WINDOW = 128
SWA_WIDTH = SWA_HEADS * SWA_HEAD_DIM
MIX_WIDTH = RET_WIDTH + SWA_WIDTH
RQK = RET_HEADS * RET_QK_DIM
SKV = SWA_KV_HEADS * SWA_HEAD_DIM
IN_SIZES = (RQK, RQK, RET_WIDTH, RET_WIDTH, SWA_WIDTH, SKV, SKV)
IN_OFFS = tuple(sum(IN_SIZES[:i]) for i in range(len(IN_SIZES)))
IN_WIDTH = sum(IN_SIZES)
REL_BUCKETS = 32
REL_MAX_DIST = 128
N_EXPERTS = 256
TOP_K = 8
N_GROUPS = 8
GROUP_SIZE = N_EXPERTS // N_GROUPS
TOPK_GROUPS = 4
EXPERT_DIM = 256
SHARED_DIM = 256
ROUTED_SCALE = 2.5
LN_EPS = 1e-5
GN_EPS = 1e-6
DEEPNORM_ALPHA = (2 * DEPTH) ** 0.25
MASK_VALUE = -1e30

VMEM_LIMIT_BYTES = 56 * 1024 * 1024

MIX_ROWS = 256
ROUTE_ROWS = 256
DISPATCH_ROWS = 256
EXPERT_ROWS = 256
COMBINE_ROWS = 128


def _layer_norm(x, g, b):
    mu = jnp.mean(x, axis=-1, keepdims=True)
    xc = x - mu
    var = jnp.mean(xc * xc, axis=-1, keepdims=True)
    return xc * lax.rsqrt(var + LN_EPS) * g + b


def _dot(a, b):
    return jnp.dot(a, b, preferred_element_type=jnp.float32)


def _dot_nt(a, b):
    return lax.dot_general(a, b, (((1,), (1,)), ((), ())), preferred_element_type=jnp.float32)


def _dot_tn(a, b):
    return lax.dot_general(a, b, (((0,), (0,)), ((), ())), preferred_element_type=jnp.float32)


def _silu(x):
    return x * (1.0 / (1.0 + jnp.exp(-x)))


LANES = 128
ROW_TILE = D_MODEL // LANES


def _load_rows(ref, n_rows, lead=()):
    return jnp.concatenate([ref[lead + (pl.ds(s, n_rows, stride=ROW_TILE), slice(None))]
                            for s in range(ROW_TILE)], axis=1)


def _store_rows(ref, val, lead=()):
    n_rows = val.shape[0]
    for s in range(ROW_TILE):
        ref[lead + (pl.ds(s, n_rows, stride=ROW_TILE), slice(None))] = val[:, s * LANES:(s + 1) * LANES]


def _row_tile(r):
    return pl.ds(pl.multiple_of(r * ROW_TILE, ROW_TILE), ROW_TILE)


def _swap_halves(x):
    n = x.shape[-1]
    half = RET_QK_DIM // 2
    lane = lax.broadcasted_iota(jnp.int32, x.shape, 1)
    from_right = pltpu.roll(x, n - half, axis=1)
    from_left = pltpu.roll(x, half, axis=1)
    return jnp.where((lane % RET_QK_DIM) < half, from_right, from_left)


def _mixer_kernel(rel_bias_ref, x_ref, g_in_ref, b_in_ref, w_in_ref, w_out_ref, rot_ref, decay_ref,
                  zeta_ref, xi_ref, cdecay_ref, bucket_ref, sink_ref, g_mix_ref, b_mix_ref,
                  h2_ref, state_ref, kprev_ref, vprev_ref, bias_ref):
    b_id = pl.program_id(0)
    c_id = pl.program_id(1)
    W = SWA_WINDOW

    @pl.when((b_id == 0) & (c_id == 0))
    def _build_bias():
        bucket = bucket_ref[...]
        for h in range(SWA_HEADS):
            acc = jnp.full((2 * W, W), MASK_VALUE, jnp.float32)
            for b in range(REL_BUCKETS):
                acc = jnp.where(bucket == b, rel_bias_ref[b, h], acc)
            kh, g = divmod(h, SWA_GROUP)
            bias_ref[kh, :, g * W:(g + 1) * W] = acc

    @pl.when(c_id == 0)
    def _reset():
        state_ref[...] = jnp.zeros_like(state_ref)
        kprev_ref[...] = jnp.zeros_like(kprev_ref)
        vprev_ref[...] = jnp.zeros_like(vprev_ref)

    h = _layer_norm(x_ref[...], g_in_ref[...], b_in_ref[...])
    proj = _dot(h.astype(jnp.bfloat16), w_in_ref[...])

    o_q, o_k, o_v, o_g, o_sq, o_sk, o_sv = IN_OFFS
    cos_t = rot_ref[:, :RQK]
    sin_t = rot_ref[:, RQK:]
    q_all = proj[:, o_q:o_q + RQK]
    k_all = proj[:, o_k:o_k + RQK]
    q_rot = q_all * cos_t + _swap_halves(q_all) * sin_t
    k_rot = (k_all * cos_t + _swap_halves(k_all) * sin_t) * (RET_QK_DIM ** -0.5)

    n_sub = x_ref.shape[0] // RET_CHUNK
    states = [state_ref[hh] for hh in range(RET_HEADS)]
    k_prev = kprev_ref[...]
    v_prev = vprev_ref[...]
    cat_rows = []
    for s in range(n_sub):
        r0 = s * RET_CHUNK
        rows = slice(r0, r0 + RET_CHUNK)
        pieces = []
        for hh in range(RET_HEADS):
            qk = slice(hh * RET_QK_DIM, (hh + 1) * RET_QK_DIM)
            vv = slice(o_v + hh * RET_V_DIM, o_v + (hh + 1) * RET_V_DIM)
            gg = slice(o_g + hh * RET_V_DIM, o_g + (hh + 1) * RET_V_DIM)
            q = q_rot[rows, qk].astype(jnp.bfloat16)
            k32 = k_rot[rows, qk]
            v = proj[rows, vv].astype(jnp.bfloat16)
            scores = _dot_nt(q, k32.astype(jnp.bfloat16)) * decay_ref[hh]
            intra = _dot(scores.astype(jnp.bfloat16), v)
            inter = _dot(q, states[hh].astype(jnp.bfloat16)) * xi_ref[hh]
            ret = intra + inter
            kz = (k32 * zeta_ref[hh]).astype(jnp.bfloat16)
            states[hh] = states[hh] * cdecay_ref[hh] + _dot_tn(kz, v)
            mu = jnp.mean(ret, axis=-1, keepdims=True)
            rc = ret - mu
            var = jnp.mean(rc * rc, axis=-1, keepdims=True)
            normed = rc * lax.rsqrt(var + GN_EPS)
            pieces.append((_silu(proj[rows, gg]) * normed).astype(jnp.bfloat16))
        k_cur = proj[rows, o_sk:o_sk + SKV].astype(jnp.bfloat16)
        v_cur = proj[rows, o_sv:o_sv + SKV].astype(jnp.bfloat16)
        for kh in range(SWA_KV_HEADS):
            kv = slice(kh * SWA_HEAD_DIM, (kh + 1) * SWA_HEAD_DIM)
            q4 = jnp.concatenate(
                [proj[rows, o_sq + (kh * SWA_GROUP + g) * SWA_HEAD_DIM:
                      o_sq + (kh * SWA_GROUP + g + 1) * SWA_HEAD_DIM] for g in range(SWA_GROUP)],
                axis=0) * (SWA_HEAD_DIM ** -0.5)
            kcat = jnp.concatenate([k_prev[:, kv], k_cur[:, kv]], axis=0)
            vcat = jnp.concatenate([v_prev[:, kv], v_cur[:, kv]], axis=0)
            logits = _dot_nt(kcat, q4.astype(jnp.bfloat16)) + bias_ref[kh]
            if s == 0:
                key = lax.broadcasted_iota(jnp.int32, logits.shape, 0)
                logits = logits + jnp.where((key < W) & (c_id == 0), MASK_VALUE, 0.0)
            sink = sink_ref[kh]
            m = jnp.maximum(jnp.max(logits, axis=0, keepdims=True), sink)
            p = jnp.exp(logits - m)
            den = jnp.sum(p, axis=0, keepdims=True) + jnp.exp(sink - m)
            probs = (p / den).astype(jnp.bfloat16)
            o4 = _dot_tn(vcat, probs)
            pieces.extend(o4[:, g * W:(g + 1) * W].T.astype(jnp.bfloat16) for g in range(SWA_GROUP))
        k_prev, v_prev = k_cur, v_cur
        cat_rows.append(jnp.concatenate(pieces, axis=1))
    for hh in range(RET_HEADS):
        state_ref[hh] = states[hh]
    kprev_ref[...] = k_prev
    vprev_ref[...] = v_prev

    mix = _dot(jnp.concatenate(cat_rows, axis=0), w_out_ref[...])
    _store_rows(h2_ref, _layer_norm(DEEPNORM_ALPHA * h + mix, g_mix_ref[...], b_mix_ref[...]))


def _t5_bucket(dist):
    n = jnp.maximum(dist, 0)
    max_exact = REL_BUCKETS // 2
    ratio = jnp.log(jnp.maximum(n, 1).astype(jnp.float32) / max_exact) / math.log(REL_MAX_DIST / max_exact)
    large = jnp.minimum(max_exact + (ratio * (REL_BUCKETS - max_exact)).astype(jnp.int32), REL_BUCKETS - 1)
    return jnp.where(n < max_exact, n, large)


def _mixer(x, ln_in_g, ln_in_b, w_in, w_out, rel_bias, sinks, ln_mix_g, ln_mix_b):
    B, S, D = x.shape
    R = MIX_ROWS
    C = RET_CHUNK
    W = SWA_WINDOW
    f32 = jnp.float32
    half = RET_QK_DIM // 2
    inv = ROPE_BASE ** (-jnp.arange(half, dtype=f32) / half)
    ang = jnp.arange(S, dtype=f32)[:, None] * inv[None, :]
    cos, sin = jnp.cos(ang), jnp.sin(ang)
    cos_t = jnp.tile(jnp.concatenate([cos, cos], axis=-1), (1, RET_HEADS))
    sin_t = jnp.tile(jnp.concatenate([-sin, sin], axis=-1), (1, RET_HEADS))
    rot = jnp.concatenate([cos_t, sin_t], axis=-1)
    log_gamma = jnp.log(1.0 - 2.0 ** (-5.0 - jnp.arange(RET_HEADS, dtype=f32)))
    idx = jnp.arange(C, dtype=f32)
    diff = idx[:, None] - idx[None, :]
    decay = jnp.where(diff[None] >= 0, jnp.exp(jnp.maximum(diff, 0.0)[None] * log_gamma[:, None, None]), 0.0)
    zeta = jnp.exp((C - 1.0 - idx)[None, :] * log_gamma[:, None])
    xi = jnp.exp((idx + 1.0)[None, :] * log_gamma[:, None])
    zeta_b = jnp.broadcast_to(zeta[:, :, None], (RET_HEADS, C, RET_QK_DIM))
    xi_b = jnp.broadcast_to(xi[:, :, None], (RET_HEADS, C, RET_V_DIM))
    cdecay = jnp.broadcast_to(jnp.exp(C * log_gamma)[:, None, None], (RET_HEADS, RET_QK_DIM, RET_V_DIM))
    i = jnp.arange(W)
    j = jnp.arange(2 * W)
    dist = i[:, None] + W - j[None, :]
    bucket = jnp.where((dist >= 0) & (dist < W), _t5_bucket(dist), -1).astype(jnp.int32).T
    sink_row = jnp.repeat(sinks.astype(f32), W).reshape(SWA_KV_HEADS, 1, SWA_GROUP * W)

    const = lambda shape: pl.BlockSpec(shape, lambda b, c, *_: (0,) * len(shape))
    grid_spec = pltpu.PrefetchScalarGridSpec(
        num_scalar_prefetch=1,
        grid=(B, S // R),
        in_specs=[
            pl.BlockSpec((None, R, D), lambda b, c, *_: (b, c, 0)),
            const((1, D)), const((1, D)),
            const((D, IN_WIDTH)), const((MIX_WIDTH, D)),
            pl.BlockSpec((R, 2 * RQK), lambda b, c, *_: (c, 0)),
            const((RET_HEADS, C, C)), const((RET_HEADS, C, RET_QK_DIM)), const((RET_HEADS, C, RET_V_DIM)),
            const((RET_HEADS, RET_QK_DIM, RET_V_DIM)),
            const((2 * W, W)), const((SWA_KV_HEADS, 1, SWA_GROUP * W)),
            const((1, D)), const((1, D)),
        ],
        out_specs=pl.BlockSpec((R * ROW_TILE, LANES), lambda b, c, *_: (b * (S // R) + c, 0)),
        scratch_shapes=[
            pltpu.VMEM((RET_HEADS, RET_QK_DIM, RET_V_DIM), f32),
            pltpu.VMEM((W, SKV), jnp.bfloat16),
            pltpu.VMEM((W, SKV), jnp.bfloat16),
            pltpu.VMEM((SWA_KV_HEADS, 2 * W, SWA_GROUP * W), f32),
        ],
    )
    return pl.pallas_call(
        _mixer_kernel,
        grid_spec=grid_spec,
        out_shape=jax.ShapeDtypeStruct((B * S * ROW_TILE, LANES), f32),
        compiler_params=pltpu.CompilerParams(
            dimension_semantics=("arbitrary", "arbitrary"), vmem_limit_bytes=VMEM_LIMIT_BYTES),
    )(rel_bias.astype(f32), x, ln_in_g.reshape(1, D), ln_in_b.reshape(1, D),
      w_in.astype(jnp.bfloat16), w_out.astype(jnp.bfloat16), rot, decay, zeta_b, xi_b, cdecay,
      bucket, sink_row, ln_mix_g.reshape(1, D), ln_mix_b.reshape(1, D))


def _router_kernel(h_ref, wr_ref, rb_ref, e_ref, w_ref, rk_ref, cnt_ref, run_ref):
    f32 = jnp.float32
    R = h_ref.shape[0] // ROW_TILE
    E = N_EXPERTS
    neg = -jnp.inf

    @pl.when(pl.program_id(0) == 0)
    def _init():
        run_ref[...] = jnp.zeros_like(run_ref)

    logits = lax.dot_general(wr_ref[...], _load_rows(h_ref, R), (((1,), (1,)), ((), ())),
                             precision=lax.Precision.HIGHEST, preferred_element_type=f32)
    scores = 1.0 / (1.0 + jnp.exp(-logits))
    choice = scores + rb_ref[...]
    eid = lax.broadcasted_iota(jnp.int32, (E, R), 0)

    def first_argmax(vals, ids, none):
        m = jnp.max(vals, axis=0, keepdims=True)
        idx = jnp.min(jnp.where(vals == m, ids, none), axis=0, keepdims=True)
        return m, idx

    gid = lax.broadcasted_iota(jnp.int32, (GROUP_SIZE, R), 0)
    groups, gscore = [], []
    for g in range(N_GROUPS):
        vals = choice[g * GROUP_SIZE:(g + 1) * GROUP_SIZE]
        m1, i1 = first_argmax(vals, gid, GROUP_SIZE)
        m2 = jnp.max(jnp.where(gid == i1, neg, vals), axis=0, keepdims=True)
        groups.append(vals)
        gscore.append(m1 + m2)
    kept = []
    for g in range(N_GROUPS):
        beaten = jnp.zeros((1, R), f32)
        for g2 in range(N_GROUPS):
            if g2 == g:
                continue
            ahead = (gscore[g2] > gscore[g]) | (gscore[g2] == gscore[g]) if g2 < g else gscore[g2] > gscore[g]
            beaten = beaten + jnp.where(ahead, 1.0, 0.0)
        kept.append(jnp.where(beaten < TOPK_GROUPS, groups[g], neg))
    masked = jnp.concatenate(kept, axis=0)

    idxs, wts = [], []
    picked = jnp.zeros((E, R), f32)
    for _ in range(TOP_K):
        _, idx = first_argmax(masked, eid, E)
        hit = eid == idx
        idxs.append(idx)
        wts.append(jnp.sum(jnp.where(hit, scores, 0.0), axis=0, keepdims=True))
        masked = jnp.where(hit, neg, masked)
        picked = jnp.where(hit, 1.0, picked)
    wsum = wts[0]
    for k in range(1, TOP_K):
        wsum = wsum + wts[k]

    row = lax.broadcasted_iota(jnp.int32, (R, R), 0)
    col = lax.broadcasted_iota(jnp.int32, (R, R), 1)
    earlier = jnp.where(row < col, 1.0, 0.0).astype(jnp.bfloat16)
    picked_bf = picked.astype(jnp.bfloat16)
    run = run_ref[...]
    before = _dot(picked_bf, earlier) + jnp.concatenate([run] * (R // LANES), axis=1)
    sub_k = lax.broadcasted_iota(jnp.int32, (TOP_K, R), 0)
    e_out = jnp.zeros((TOP_K, R), jnp.int32)
    w_out = jnp.zeros((TOP_K, R), f32)
    rk_out = jnp.zeros((TOP_K, R), jnp.int32)
    for k in range(TOP_K):
        rank_k = jnp.sum(jnp.where(eid == idxs[k], before, 0.0), axis=0, keepdims=True)
        e_out = jnp.where(sub_k == k, idxs[k], e_out)
        w_out = jnp.where(sub_k == k, wts[k] / wsum * ROUTED_SCALE, w_out)
        rk_out = jnp.where(sub_k == k, rank_k.astype(jnp.int32), rk_out)
    e_ref[...] = e_out
    w_ref[...] = w_out
    rk_ref[...] = rk_out
    run_ref[...] = run + _dot(picked_bf, jnp.ones((R, LANES), jnp.bfloat16))
    cnt_ref[...] = run_ref[...]


def _router(h2, w_router, router_bias):
    T, D = h2.shape[0] // ROW_TILE, D_MODEL
    R = ROUTE_ROWS
    E = N_EXPERTS
    return pl.pallas_call(
        _router_kernel,
        grid=(T // R,),
        in_specs=[
            pl.BlockSpec((R * ROW_TILE, LANES), lambda i: (i, 0)),
            pl.BlockSpec((E, D), lambda i: (0, 0)),
            pl.BlockSpec((E, R), lambda i: (0, 0)),
        ],
        out_specs=[
            pl.BlockSpec((TOP_K, R), lambda i: (0, i)),
            pl.BlockSpec((TOP_K, R), lambda i: (0, i)),
            pl.BlockSpec((TOP_K, R), lambda i: (0, i)),
            pl.BlockSpec((E, LANES), lambda i: (0, 0)),
        ],
        out_shape=[
            jax.ShapeDtypeStruct((TOP_K, T), jnp.int32),
            jax.ShapeDtypeStruct((TOP_K, T), jnp.float32),
            jax.ShapeDtypeStruct((TOP_K, T), jnp.int32),
            jax.ShapeDtypeStruct((E, LANES), jnp.float32),
        ],
        scratch_shapes=[pltpu.VMEM((E, LANES), jnp.float32)],
        compiler_params=pltpu.CompilerParams(
            dimension_semantics=("arbitrary",), vmem_limit_bytes=VMEM_LIMIT_BYTES),
    )(h2, w_router.T, jnp.broadcast_to(router_bias.astype(jnp.float32)[:, None], (E, R)))


def _positions_kernel(row_start_ref, e_ref, rk_ref, pos_ref):
    e = e_ref[...]
    rk = rk_ref[...]

    def per_expert(i, pos):
        return jnp.where(e == i, rk + row_start_ref[i], pos)

    pos_ref[...] = lax.fori_loop(0, N_EXPERTS, per_expert, jnp.zeros_like(rk))


def _positions(e_idx, rank, row_start):
    n = e_idx.size
    shape = (n // LANES, LANES)
    grid_spec = pltpu.PrefetchScalarGridSpec(
        num_scalar_prefetch=1,
        grid=(1,),
        in_specs=[pl.BlockSpec(shape, lambda i, *_: (0, 0)), pl.BlockSpec(shape, lambda i, *_: (0, 0))],
        out_specs=pl.BlockSpec(shape, lambda i, *_: (0, 0)),
    )
    pos = pl.pallas_call(
        _positions_kernel,
        grid_spec=grid_spec,
        out_shape=jax.ShapeDtypeStruct(shape, jnp.int32),
    )(row_start, e_idx.reshape(shape), rank.reshape(shape))
    return pos.reshape(n)


def _dispatch_kernel(row_start_ref, cnt_ref, n_act_ref, pos_ref, h_ref, xs_ref, zero_ref, sem, zsem,
                     *, n_tokens):
    R = h_ref.shape[0] // ROW_TILE
    BM = EXPERT_ROWS
    n_blocks = xs_ref.shape[0] // (BM * ROW_TILE)
    n_pad_units = n_blocks - n_tokens * TOP_K // BM

    @pl.when(pl.program_id(0) == 0)
    def _zero_padding():
        zero_ref[...] = jnp.zeros_like(zero_ref)

        def expert_tail(e, carry):
            n_tail = pl.multiple_of(((BM - cnt_ref[e] % BM) % BM) * ROW_TILE, ROW_TILE)

            @pl.when(n_tail > 0)
            def _():
                dst = pl.multiple_of((row_start_ref[e] + cnt_ref[e]) * ROW_TILE, ROW_TILE)
                pltpu.make_async_copy(zero_ref.at[pl.ds(0, n_tail)], xs_ref.at[pl.ds(dst, n_tail)], zsem).start()
            return carry

        lax.fori_loop(0, N_EXPERTS, expert_tail, 0)

        def idle_block(i, carry):
            dst = pl.multiple_of(i * (BM * ROW_TILE), BM * ROW_TILE)
            pltpu.make_async_copy(zero_ref, xs_ref.at[pl.ds(dst, BM * ROW_TILE)], zsem).start()
            return carry

        lax.fori_loop(n_act_ref[0], n_blocks, idle_block, 0)

        def drain(i, carry):
            pltpu.make_async_copy(zero_ref, xs_ref.at[pl.ds(0, BM * ROW_TILE)], zsem).wait()
            return carry

        lax.fori_loop(0, n_pad_units, drain, 0)

    def issue(t, carry):
        src = h_ref.at[_row_tile(t)]
        for k in range(TOP_K):
            dest = pos_ref[t * TOP_K + k]
            pltpu.make_async_copy(src, xs_ref.at[_row_tile(dest)], sem).start(priority=k % 2)
        return carry

    lax.fori_loop(0, R, issue, 0)
    n = R * TOP_K * ROW_TILE
    pltpu.make_async_copy(xs_ref.at[pl.ds(0, n)], xs_ref.at[pl.ds(0, n)], sem).wait()


def _dispatch(h2, pos, row_start, cnt, n_act, n_rows):
    T = h2.shape[0] // ROW_TILE
    R = DISPATCH_ROWS
    grid_spec = pltpu.PrefetchScalarGridSpec(
        num_scalar_prefetch=3,
        grid=(T // R,),
        in_specs=[
            pl.BlockSpec((R * TOP_K,), lambda i, *_: (i,), memory_space=pltpu.SMEM),
            pl.BlockSpec((R * ROW_TILE, LANES), lambda i, *_: (i, 0)),
        ],
        out_specs=pl.BlockSpec(memory_space=pl.ANY),
        scratch_shapes=[pltpu.VMEM((EXPERT_ROWS * ROW_TILE, LANES), jnp.float32),
                        pltpu.SemaphoreType.DMA(()), pltpu.SemaphoreType.DMA(())],
    )
    return pl.pallas_call(
        functools.partial(_dispatch_kernel, n_tokens=T),
        grid_spec=grid_spec,
        out_shape=jax.ShapeDtypeStruct((n_rows * ROW_TILE, LANES), jnp.float32),
        compiler_params=pltpu.CompilerParams(
            dimension_semantics=("arbitrary",), vmem_limit_bytes=VMEM_LIMIT_BYTES),
    )(row_start, cnt, n_act, pos, h2)


X_SLOTS = 4
Y_SLOTS = 3


def _experts_kernel(blk_e_ref, first_ref, slot_ref, next_e_ref, n_act_ref, xs_hbm, wg_hbm, wu_hbm, wd_hbm,
                    ys_hbm, x_buf, y_buf, wg_buf, wu_buf, wd_buf, wg_bf, wu_bf, wd_bf, sems, x_sems, y_sems):
    i = pl.program_id(0)
    n_act = n_act_ref[0]
    bf16 = jnp.bfloat16
    blk = EXPERT_ROWS * ROW_TILE

    def block_rows(j):
        return pl.ds(pl.multiple_of(j * blk, blk), blk)

    def x_copy(j):
        return pltpu.make_async_copy(xs_hbm.at[block_rows(j)], x_buf.at[j % X_SLOTS], x_sems.at[j % X_SLOTS])

    def y_copy(j):
        return pltpu.make_async_copy(y_buf.at[j % Y_SLOTS], ys_hbm.at[block_rows(j)], y_sems.at[j % Y_SLOTS])

    @pl.when(i == 0)
    def _prime():
        for j in range(X_SLOTS - 1):
            @pl.when(j < n_act)
            def _():
                x_copy(j).start()

    @pl.when(i + (X_SLOTS - 1) < n_act)
    def _prefetch():
        x_copy(i + (X_SLOTS - 1)).start()

    def weight_copies(e, slot):
        return (pltpu.make_async_copy(wg_hbm.at[e], wg_buf.at[slot], sems.at[slot]),
                pltpu.make_async_copy(wu_hbm.at[e], wu_buf.at[slot], sems.at[slot]),
                pltpu.make_async_copy(wd_hbm.at[e], wd_buf.at[slot], sems.at[slot]))

    @pl.when((i < n_act_ref[0]) & (first_ref[i] == 1))
    def _new_expert():
        slot = slot_ref[i]

        @pl.when(i == 0)
        def _():
            for c in weight_copies(blk_e_ref[0], 0):
                c.start()

        for c in weight_copies(blk_e_ref[i], slot):
            c.wait()

        @pl.when(next_e_ref[i] >= 0)
        def _():
            for c in weight_copies(next_e_ref[i], 1 - slot):
                c.start()

        wg_bf[...] = wg_buf[slot].astype(bf16)
        wu_bf[...] = wu_buf[slot].astype(bf16)
        wd_bf[...] = wd_buf[slot].astype(bf16)

    @pl.when(i < n_act)
    def _compute():
        x_copy(i).wait()
        x = _load_rows(x_buf, EXPERT_ROWS, lead=(i % X_SLOTS,)).astype(bf16)
        g = _dot(x, wg_bf[...])
        u = _dot(x, wu_bf[...])
        a = (_silu(g) * u).astype(bf16)
        y = _dot(a, wd_bf[...])

        @pl.when(i >= Y_SLOTS)
        def _():
            y_copy(i - Y_SLOTS).wait()

        _store_rows(y_buf, y, lead=(i % Y_SLOTS,))
        y_copy(i).start()

    @pl.when(i == n_act - 1)
    def _drain():
        for d in range(Y_SLOTS):
            @pl.when(i - d >= 0)
            def _():
                y_copy(i - d).wait()


def _experts(xs, blk_e, n_act, w_gate, w_up, w_down):
    D = D_MODEL
    BM = EXPERT_ROWS
    F = EXPERT_DIM
    n_blocks = xs.shape[0] // (BM * ROW_TILE)
    ids = jnp.arange(n_blocks, dtype=jnp.int32)
    active = ids < n_act[0]
    first = active & ((ids == 0) | (blk_e != jnp.roll(blk_e, 1)))
    slot = ((jnp.cumsum(first.astype(jnp.int32)) - 1) % 2).astype(jnp.int32)
    first_pos = jnp.where(first, ids, n_blocks)
    later_first = lax.cummin(jnp.concatenate([first_pos[1:], jnp.full((1,), n_blocks, jnp.int32)]), reverse=True)
    next_e = jnp.where(later_first < n_blocks, blk_e[jnp.minimum(later_first, n_blocks - 1)], -1).astype(jnp.int32)

    grid_spec = pltpu.PrefetchScalarGridSpec(
        num_scalar_prefetch=5,
        grid=(n_blocks,),
        in_specs=[pl.BlockSpec(memory_space=pl.ANY)] * 4,
        out_specs=pl.BlockSpec(memory_space=pl.ANY),
        scratch_shapes=[
            pltpu.VMEM((X_SLOTS, BM * ROW_TILE, LANES), jnp.float32),
            pltpu.VMEM((Y_SLOTS, BM * ROW_TILE, LANES), jnp.float32),
            pltpu.VMEM((2, D, F), jnp.float32), pltpu.VMEM((2, D, F), jnp.float32),
            pltpu.VMEM((2, F, D), jnp.float32),
            pltpu.VMEM((D, F), jnp.bfloat16), pltpu.VMEM((D, F), jnp.bfloat16),
            pltpu.VMEM((F, D), jnp.bfloat16),
            pltpu.SemaphoreType.DMA((2,)), pltpu.SemaphoreType.DMA((X_SLOTS,)),
            pltpu.SemaphoreType.DMA((Y_SLOTS,)),
        ],
    )
    return pl.pallas_call(
        _experts_kernel,
        grid_spec=grid_spec,
        out_shape=jax.ShapeDtypeStruct(xs.shape, jnp.float32),
        input_output_aliases={5: 0},
        compiler_params=pltpu.CompilerParams(
            dimension_semantics=("arbitrary",), vmem_limit_bytes=VMEM_LIMIT_BYTES),
    )(blk_e, first.astype(jnp.int32), slot, next_e, n_act, xs, w_gate, w_up, w_down)


def _combine_kernel(pos_ref, h_ref, w_ref, ys_ref, wsg_ref, wsu_ref, wsd_ref,
                    g_ref, b_ref, out_ref, buf_ref, sem):
    R = h_ref.shape[0] // ROW_TILE

    def issue(t, carry):
        for k in range(TOP_K):
            src = pos_ref[t * TOP_K + k]
            pltpu.make_async_copy(ys_ref.at[_row_tile(src)], buf_ref.at[k, _row_tile(t)], sem).start(priority=k % 2)
        return carry

    lax.fori_loop(0, R, issue, 0)
    h = _load_rows(h_ref, R)
    hb = h.astype(jnp.bfloat16)
    act = (_silu(_dot(hb, wsg_ref[...])) * _dot(hb, wsu_ref[...])).astype(jnp.bfloat16)
    ffn = _dot(act, wsd_ref[...])
    pltpu.make_async_copy(buf_ref, buf_ref, sem).wait()
    w = w_ref[...]
    for k in range(TOP_K):
        ffn = ffn + _load_rows(buf_ref, R, lead=(k,)) * w[:, k:k + 1]
    out_ref[...] = _layer_norm(DEEPNORM_ALPHA * h + ffn, g_ref[...], b_ref[...])


def _combine(h2, pos, top_w, ys, ws_gate, ws_up, ws_down, ln_g, ln_b):
    T, D = h2.shape[0] // ROW_TILE, D_MODEL
    R = COMBINE_ROWS
    F = SHARED_DIM
    bf16 = jnp.bfloat16
    const = lambda shape: pl.BlockSpec(shape, lambda i: (0,) * len(shape))
    return pl.pallas_call(
        _combine_kernel,
        grid=(T // R,),
        in_specs=[
            pl.BlockSpec((R * TOP_K,), lambda i: (i,), memory_space=pltpu.SMEM),
            pl.BlockSpec((R * ROW_TILE, LANES), lambda i: (i, 0)),
            pl.BlockSpec((R, TOP_K), lambda i: (i, 0)),
            pl.BlockSpec(memory_space=pl.ANY),
            const((D, F)), const((D, F)), const((F, D)), const((1, D)), const((1, D)),
        ],
        out_specs=pl.BlockSpec((R, D), lambda i: (i, 0)),
        scratch_shapes=[pltpu.VMEM((TOP_K, R * ROW_TILE, LANES), jnp.float32), pltpu.SemaphoreType.DMA(())],
        out_shape=jax.ShapeDtypeStruct((T, D), jnp.float32),
        compiler_params=pltpu.CompilerParams(
            dimension_semantics=("arbitrary",), vmem_limit_bytes=VMEM_LIMIT_BYTES),
    )(pos, h2, top_w, ys, ws_gate.astype(bf16), ws_up.astype(bf16),
      ws_down.astype(bf16), ln_g.reshape(1, D), ln_b.reshape(1, D))


def _moe(h2, w_router, router_bias, w_gate, w_up, w_down, ws_gate, ws_up, ws_down, ln_g, ln_b):
    T = h2.shape[0] // ROW_TILE
    E = N_EXPERTS
    BM = EXPERT_ROWS
    e_idx, top_w, rank, counts = _router(h2, w_router, router_bias)
    cnt = counts[:, 0].astype(jnp.int32)
    nblk = (cnt + BM - 1) // BM
    blk_end = jnp.cumsum(nblk)
    row_start = ((blk_end - nblk) * BM).astype(jnp.int32)
    n_blocks = T * TOP_K // BM + E
    n_act = blk_end[-1:].astype(jnp.int32)
    blk_ids = jnp.minimum(jnp.arange(n_blocks, dtype=jnp.int32), n_act[0] - 1)
    blk_e = jnp.minimum(jnp.sum(blk_end[None, :] <= blk_ids[:, None], axis=1), E - 1).astype(jnp.int32)
    pos = _positions(e_idx, rank, row_start).reshape(TOP_K, T).T.reshape(T * TOP_K)
    xs = _dispatch(h2, pos, row_start, cnt, n_act, n_blocks * BM)
    ys = _experts(xs, blk_e, n_act, w_gate, w_up, w_down)
    return _combine(h2, pos, top_w.T, ys, ws_gate, ws_up, ws_down, ln_g, ln_b)


def kernel(x, ln_in_g, ln_in_b, w_in, w_out, rel_bias, attn_sinks, ln_mix_g, ln_mix_b, w_router,
           router_bias, w_gate, w_up, w_down, ws_gate, ws_up, ws_down, ln_ffn_g, ln_ffn_b):
    B, S, D = x.shape
    h = _mixer(x, ln_in_g, ln_in_b, w_in[0], w_out[0], rel_bias, attn_sinks[0], ln_mix_g[0], ln_mix_b[0])
    out = _moe(h, w_router[0], router_bias[0], w_gate[0], w_up[0], w_down[0],
               ws_gate[0], ws_up[0], ws_down[0], ln_ffn_g[0], ln_ffn_b[0])
    return out.reshape(B, S, D)
```

```python
import functools
import math

import jax
import jax.numpy as jnp
from jax import lax
from jax.experimental import pallas as pl
from jax.experimental.pallas import tpu as pltpu

D_MODEL = 1024
DEPTH = 1
RET_HEADS = 4
RET_QK_DIM = 64
RET_V_DIM = 128
RET_CHUNK = 128
RET_WIDTH = RET_HEADS * RET_V_DIM
ROPE_BASE = 10000.0
SWA_HEADS = 8
SWA_KV_HEADS = 2
SWA_GROUP = SWA_HEADS // SWA_KV_HEADS
SWA_HEAD_DIM = 64
SWA_WINDOW = 128
SWA_WIDTH = SWA_HEADS * SWA_HEAD_DIM
MIX_WIDTH = RET_WIDTH + SWA_WIDTH
RQK = RET_HEADS * RET_QK_DIM
SKV = SWA_KV_HEADS * SWA_HEAD_DIM
IN_SIZES = (RQK, RQK, RET_WIDTH, RET_WIDTH, SWA_WIDTH, SKV, SKV)
IN_OFFS = tuple(sum(IN_SIZES[:i]) for i in range(len(IN_SIZES)))
IN_WIDTH = sum(IN_SIZES)
REL_BUCKETS = 32
REL_MAX_DIST = 128
N_EXPERTS = 256
TOP_K = 8
N_GROUPS = 8
GROUP_SIZE = N_EXPERTS // N_GROUPS
TOPK_GROUPS = 4
EXPERT_DIM = 256
SHARED_DIM = 256
ROUTED_SCALE = 2.5
LN_EPS = 1e-5
GN_EPS = 1e-6
DEEPNORM_ALPHA = (2 * DEPTH) ** 0.25
MASK_VALUE = -1e30

VMEM_LIMIT_BYTES = 56 * 1024 * 1024

MIX_ROWS = 256
ROUTE_ROWS = 256
DISPATCH_ROWS = 256
EXPERT_ROWS = 256
COMBINE_ROWS = 128


def _layer_norm(x, g, b):
    mu = jnp.mean(x, axis=-1, keepdims=True)
    xc = x - mu
    var = jnp.mean(xc * xc, axis=-1, keepdims=True)
    return xc * lax.rsqrt(var + LN_EPS) * g + b


def _dot(a, b):
    return jnp.dot(a, b, preferred_element_type=jnp.float32)


def _dot_nt(a, b):
    return lax.dot_general(a, b, (((1,), (1,)), ((), ())), preferred_element_type=jnp.float32)


def _dot_tn(a, b):
    return lax.dot_general(a, b, (((0,), (0,)), ((), ())), preferred_element_type=jnp.float32)


def _silu(x):
    return x * (1.0 / (1.0 + jnp.exp(-x)))


LANES = 128
ROW_TILE = D_MODEL // LANES


def _load_rows(ref, n_rows, lead=()):
    return jnp.concatenate([ref[lead + (pl.ds(s, n_rows, stride=ROW_TILE), slice(None))]
                            for s in range(ROW_TILE)], axis=1)


def _store_rows(ref, val, lead=()):
    n_rows = val.shape[0]
    for s in range(ROW_TILE):
        ref[lead + (pl.ds(s, n_rows, stride=ROW_TILE), slice(None))] = val[:, s * LANES:(s + 1) * LANES]


def _row_tile(r):
    return pl.ds(pl.multiple_of(r * ROW_TILE, ROW_TILE), ROW_TILE)


def _swap_halves(x):
    n = x.shape[-1]
    half = RET_QK_DIM // 2
    lane = lax.broadcasted_iota(jnp.int32, x.shape, 1)
    from_right = pltpu.roll(x, n - half, axis=1)
    from_left = pltpu.roll(x, half, axis=1)
    return jnp.where((lane % RET_QK_DIM) < half, from_right, from_left)


def _mixer_kernel(rel_bias_ref, x_ref, g_in_ref, b_in_ref, w_in_ref, w_out_ref, rot_ref, decay_ref,
                  zeta_ref, xi_ref, cdecay_ref, bucket_ref, sink_ref, g_mix_ref, b_mix_ref,
                  h2_ref, state_ref, kprev_ref, vprev_ref, bias_ref):
    b_id = pl.program_id(0)
    c_id = pl.program_id(1)
    W = SWA_WINDOW

    @pl.when((b_id == 0) & (c_id == 0))
    def _build_bias():
        bucket = bucket_ref[...]
        for h in range(SWA_HEADS):
            acc = jnp.full((2 * W, W), MASK_VALUE, jnp.float32)
            for b in range(REL_BUCKETS):
                acc = jnp.where(bucket == b, rel_bias_ref[b, h], acc)
            kh, g = divmod(h, SWA_GROUP)
            bias_ref[kh, :, g * W:(g + 1) * W] = acc

    @pl.when(c_id == 0)
    def _reset():
        state_ref[...] = jnp.zeros_like(state_ref)
        kprev_ref[...] = jnp.zeros_like(kprev_ref)
        vprev_ref[...] = jnp.zeros_like(vprev_ref)

    h = _layer_norm(x_ref[...], g_in_ref[...], b_in_ref[...])
    proj = _dot(h.astype(jnp.bfloat16), w_in_ref[...])

    o_q, o_k, o_v, o_g, o_sq, o_sk, o_sv = IN_OFFS
    cos_t = rot_ref[:, :RQK]
    sin_t = rot_ref[:, RQK:]
    q_all = proj[:, o_q:o_q + RQK]
    k_all = proj[:, o_k:o_k + RQK]
    q_rot = q_all * cos_t + _swap_halves(q_all) * sin_t
    k_rot = (k_all * cos_t + _swap_halves(k_all) * sin_t) * (RET_QK_DIM ** -0.5)

    n_sub = x_ref.shape[0] // RET_CHUNK
    states = [state_ref[hh] for hh in range(RET_HEADS)]
    k_prev = kprev_ref[...]
    v_prev = vprev_ref[...]
    cat_rows = []
    for s in range(n_sub):
        r0 = s * RET_CHUNK
        rows = slice(r0, r0 + RET_CHUNK)
        pieces = []
        for hh in range(RET_HEADS):
            qk = slice(hh * RET_QK_DIM, (hh + 1) * RET_QK_DIM)
            vv = slice(o_v + hh * RET_V_DIM, o_v + (hh + 1) * RET_V_DIM)
            gg = slice(o_g + hh * RET_V_DIM, o_g + (hh + 1) * RET_V_DIM)
            q = q_rot[rows, qk].astype(jnp.bfloat16)
            k32 = k_rot[rows, qk]
            v = proj[rows, vv].astype(jnp.bfloat16)
            scores = _dot_nt(q, k32.astype(jnp.bfloat16)) * decay_ref[hh]
            intra = _dot(scores.astype(jnp.bfloat16), v)
            inter = _dot(q, states[hh].astype(jnp.bfloat16)) * xi_ref[hh]
            ret = intra + inter
            kz = (k32 * zeta_ref[hh]).astype(jnp.bfloat16)
            states[hh] = states[hh] * cdecay_ref[hh] + _dot_tn(kz, v)
            mu = jnp.mean(ret, axis=-1, keepdims=True)
            rc = ret - mu
            var = jnp.mean(rc * rc, axis=-1, keepdims=True)
            normed = rc * lax.rsqrt(var + GN_EPS)
            pieces.append((_silu(proj[rows, gg]) * normed).astype(jnp.bfloat16))
        k_cur = proj[rows, o_sk:o_sk + SKV].astype(jnp.bfloat16)
        v_cur = proj[rows, o_sv:o_sv + SKV].astype(jnp.bfloat16)
        for kh in range(SWA_KV_HEADS):
            kv = slice(kh * SWA_HEAD_DIM, (kh + 1) * SWA_HEAD_DIM)
            q4 = jnp.concatenate(
                [proj[rows, o_sq + (kh * SWA_GROUP + g) * SWA_HEAD_DIM:
                      o_sq + (kh * SWA_GROUP + g + 1) * SWA_HEAD_DIM] for g in range(SWA_GROUP)],
                axis=0) * (SWA_HEAD_DIM ** -0.5)
            kcat = jnp.concatenate([k_prev[:, kv], k_cur[:, kv]], axis=0)
            vcat = jnp.concatenate([v_prev[:, kv], v_cur[:, kv]], axis=0)
            logits = _dot_nt(kcat, q4.astype(jnp.bfloat16)) + bias_ref[kh]
            if s == 0:
                key = lax.broadcasted_iota(jnp.int32, logits.shape, 0)
                logits = logits + jnp.where((key < W) & (c_id == 0), MASK_VALUE, 0.0)
            sink = sink_ref[kh]
            m = jnp.maximum(jnp.max(logits, axis=0, keepdims=True), sink)
            p = jnp.exp(logits - m)
            den = jnp.sum(p, axis=0, keepdims=True) + jnp.exp(sink - m)
            probs = (p / den).astype(jnp.bfloat16)
            o4 = _dot_tn(vcat, probs)
            pieces.extend(o4[:, g * W:(g + 1) * W].T.astype(jnp.bfloat16) for g in range(SWA_GROUP))
        k_prev, v_prev = k_cur, v_cur
        cat_rows.append(jnp.concatenate(pieces, axis=1))
    for hh in range(RET_HEADS):
        state_ref[hh] = states[hh]
    kprev_ref[...] = k_prev
    vprev_ref[...] = v_prev

    mix = _dot(jnp.concatenate(cat_rows, axis=0), w_out_ref[...])
    _store_rows(h2_ref, _layer_norm(DEEPNORM_ALPHA * h + mix, g_mix_ref[...], b_mix_ref[...]))


def _t5_bucket(dist):
    n = jnp.maximum(dist, 0)
    max_exact = REL_BUCKETS // 2
    ratio = jnp.log(jnp.maximum(n, 1).astype(jnp.float32) / max_exact) / math.log(REL_MAX_DIST / max_exact)
    large = jnp.minimum(max_exact + (ratio * (REL_BUCKETS - max_exact)).astype(jnp.int32), REL_BUCKETS - 1)
    return jnp.where(n < max_exact, n, large)


def _mixer(x, ln_in_g, ln_in_b, w_in, w_out, rel_bias, sinks, ln_mix_g, ln_mix_b):
    B, S, D = x.shape
    R = MIX_ROWS
    C = RET_CHUNK
    W = SWA_WINDOW
    f32 = jnp.float32
    half = RET_QK_DIM // 2
    inv = ROPE_BASE ** (-jnp.arange(half, dtype=f32) / half)
    ang = jnp.arange(S, dtype=f32)[:, None] * inv[None, :]
    cos, sin = jnp.cos(ang), jnp.sin(ang)
    cos_t = jnp.tile(jnp.concatenate([cos, cos], axis=-1), (1, RET_HEADS))
    sin_t = jnp.tile(jnp.concatenate([-sin, sin], axis=-1), (1, RET_HEADS))
    rot = jnp.concatenate([cos_t, sin_t], axis=-1)
    log_gamma = jnp.log(1.0 - 2.0 ** (-5.0 - jnp.arange(RET_HEADS, dtype=f32)))
    idx = jnp.arange(C, dtype=f32)
    diff = idx[:, None] - idx[None, :]
    decay = jnp.where(diff[None] >= 0, jnp.exp(jnp.maximum(diff, 0.0)[None] * log_gamma[:, None, None]), 0.0)
    zeta = jnp.exp((C - 1.0 - idx)[None, :] * log_gamma[:, None])
    xi = jnp.exp((idx + 1.0)[None, :] * log_gamma[:, None])
    zeta_b = jnp.broadcast_to(zeta[:, :, None], (RET_HEADS, C, RET_QK_DIM))
    xi_b = jnp.broadcast_to(xi[:, :, None], (RET_HEADS, C, RET_V_DIM))
    cdecay = jnp.broadcast_to(jnp.exp(C * log_gamma)[:, None, None], (RET_HEADS, RET_QK_DIM, RET_V_DIM))
    i = jnp.arange(W)
    j = jnp.arange(2 * W)
    dist = i[:, None] + W - j[None, :]
    bucket = jnp.where((dist >= 0) & (dist < W), _t5_bucket(dist), -1).astype(jnp.int32).T
    sink_row = jnp.repeat(sinks.astype(f32), W).reshape(SWA_KV_HEADS, 1, SWA_GROUP * W)

    const = lambda shape: pl.BlockSpec(shape, lambda b, c, *_: (0,) * len(shape))
    grid_spec = pltpu.PrefetchScalarGridSpec(
        num_scalar_prefetch=1,
        grid=(B, S // R),
        in_specs=[
            pl.BlockSpec((None, R, D), lambda b, c, *_: (b, c, 0)),
            const((1, D)), const((1, D)),
            const((D, IN_WIDTH)), const((MIX_WIDTH, D)),
            pl.BlockSpec((R, 2 * RQK), lambda b, c, *_: (c, 0)),
            const((RET_HEADS, C, C)), const((RET_HEADS, C, RET_QK_DIM)), const((RET_HEADS, C, RET_V_DIM)),
            const((RET_HEADS, RET_QK_DIM, RET_V_DIM)),
            const((2 * W, W)), const((SWA_KV_HEADS, 1, SWA_GROUP * W)),
            const((1, D)), const((1, D)),
        ],
        out_specs=pl.BlockSpec((R * ROW_TILE, LANES), lambda b, c, *_: (b * (S // R) + c, 0)),
        scratch_shapes=[
            pltpu.VMEM((RET_HEADS, RET_QK_DIM, RET_V_DIM), f32),
            pltpu.VMEM((W, SKV), jnp.bfloat16),
            pltpu.VMEM((W, SKV), jnp.bfloat16),
            pltpu.VMEM((SWA_KV_HEADS, 2 * W, SWA_GROUP * W), f32),
        ],
    )
    return pl.pallas_call(
        _mixer_kernel,
        grid_spec=grid_spec,
        out_shape=jax.ShapeDtypeStruct((B * S * ROW_TILE, LANES), f32),
        compiler_params=pltpu.CompilerParams(
            dimension_semantics=("arbitrary", "arbitrary"), vmem_limit_bytes=VMEM_LIMIT_BYTES),
    )(rel_bias.astype(f32), x, ln_in_g.reshape(1, D), ln_in_b.reshape(1, D),
      w_in.astype(jnp.bfloat16), w_out.astype(jnp.bfloat16), rot, decay, zeta_b, xi_b, cdecay,
      bucket, sink_row, ln_mix_g.reshape(1, D), ln_mix_b.reshape(1, D))


def _router_kernel(h_ref, wr_ref, rb_ref, e_ref, w_ref, rk_ref, cnt_ref, run_ref):
    f32 = jnp.float32
    R = h_ref.shape[0] // ROW_TILE
    E = N_EXPERTS
    neg = -jnp.inf

    @pl.when(pl.program_id(0) == 0)
    def _init():
        run_ref[...] = jnp.zeros_like(run_ref)

    logits = lax.dot_general(wr_ref[...], _load_rows(h_ref, R), (((1,), (1,)), ((), ())),
                             precision=lax.Precision.HIGHEST, preferred_element_type=f32)
    scores = 1.0 / (1.0 + jnp.exp(-logits))
    choice = scores + rb_ref[...]
    eid = lax.broadcasted_iota(jnp.int32, (E, R), 0)

    def first_argmax(vals, ids, none):
        m = jnp.max(vals, axis=0, keepdims=True)
        idx = jnp.min(jnp.where(vals == m, ids, none), axis=0, keepdims=True)
        return m, idx

    gid = lax.broadcasted_iota(jnp.int32, (GROUP_SIZE, R), 0)
    groups, gscore = [], []
    for g in range(N_GROUPS):
        vals = choice[g * GROUP_SIZE:(g + 1) * GROUP_SIZE]
        m1, i1 = first_argmax(vals, gid, GROUP_SIZE)
        m2 = jnp.max(jnp.where(gid == i1, neg, vals), axis=0, keepdims=True)
        groups.append(vals)
        gscore.append(m1 + m2)
    kept = []
    for g in range(N_GROUPS):
        beaten = jnp.zeros((1, R), f32)
        for g2 in range(N_GROUPS):
            if g2 == g:
                continue
            ahead = (gscore[g2] > gscore[g]) | (gscore[g2] == gscore[g]) if g2 < g else gscore[g2] > gscore[g]
            beaten = beaten + jnp.where(ahead, 1.0, 0.0)
        kept.append(jnp.where(beaten < TOPK_GROUPS, groups[g], neg))
    masked = jnp.concatenate(kept, axis=0)

    idxs, wts = [], []
    picked = jnp.zeros((E, R), f32)
    for _ in range(TOP_K):
        _, idx = first_argmax(masked, eid, E)
        hit = eid == idx
        idxs.append(idx)
        wts.append(jnp.sum(jnp.where(hit, scores, 0.0), axis=0, keepdims=True))
        masked = jnp.where(hit, neg, masked)
        picked = jnp.where(hit, 1.0, picked)
    wsum = wts[0]
    for k in range(1, TOP_K):
        wsum = wsum + wts[k]

    row = lax.broadcasted_iota(jnp.int32, (R, R), 0)
    col = lax.broadcasted_iota(jnp.int32, (R, R), 1)
    earlier = jnp.where(row < col, 1.0, 0.0).astype(jnp.bfloat16)
    picked_bf = picked.astype(jnp.bfloat16)
    run = run_ref[...]
    before = _dot(picked_bf, earlier) + jnp.concatenate([run] * (R // LANES), axis=1)
    sub_k = lax.broadcasted_iota(jnp.int32, (TOP_K, R), 0)
    e_out = jnp.zeros((TOP_K, R), jnp.int32)
    w_out = jnp.zeros((TOP_K, R), f32)
    rk_out = jnp.zeros((TOP_K, R), jnp.int32)
    for k in range(TOP_K):
        rank_k = jnp.sum(jnp.where(eid == idxs[k], before, 0.0), axis=0, keepdims=True)
        e_out = jnp.where(sub_k == k, idxs[k], e_out)
        w_out = jnp.where(sub_k == k, wts[k] / wsum * ROUTED_SCALE, w_out)
        rk_out = jnp.where(sub_k == k, rank_k.astype(jnp.int32), rk_out)
    e_ref[...] = e_out
    w_ref[...] = w_out
    rk_ref[...] = rk_out
    run_ref[...] = run + _dot(picked_bf, jnp.ones((R, LANES), jnp.bfloat16))
    cnt_ref[...] = run_ref[...]


def _router(h2, w_router, router_bias):
    T, D = h2.shape[0] // ROW_TILE, D_MODEL
    R = ROUTE_ROWS
    E = N_EXPERTS
    return pl.pallas_call(
        _router_kernel,
        grid=(T // R,),
        in_specs=[
            pl.BlockSpec((R * ROW_TILE, LANES), lambda i: (i, 0)),
            pl.BlockSpec((E, D), lambda i: (0, 0)),
            pl.BlockSpec((E, R), lambda i: (0, 0)),
        ],
        out_specs=[
            pl.BlockSpec((TOP_K, R), lambda i: (0, i)),
            pl.BlockSpec((TOP_K, R), lambda i: (0, i)),
            pl.BlockSpec((TOP_K, R), lambda i: (0, i)),
            pl.BlockSpec((E, LANES), lambda i: (0, 0)),
        ],
        out_shape=[
            jax.ShapeDtypeStruct((TOP_K, T), jnp.int32),
            jax.ShapeDtypeStruct((TOP_K, T), jnp.float32),
            jax.ShapeDtypeStruct((TOP_K, T), jnp.int32),
            jax.ShapeDtypeStruct((E, LANES), jnp.float32),
        ],
        scratch_shapes=[pltpu.VMEM((E, LANES), jnp.float32)],
        compiler_params=pltpu.CompilerParams(
            dimension_semantics=("arbitrary",), vmem_limit_bytes=VMEM_LIMIT_BYTES),
    )(h2, w_router.T, jnp.broadcast_to(router_bias.astype(jnp.float32)[:, None], (E, R)))


def _positions_kernel(row_start_ref, e_ref, rk_ref, pos_ref):
    e = e_ref[...]
    rk = rk_ref[...]

    def per_expert(i, pos):
        return jnp.where(e == i, rk + row_start_ref[i], pos)

    pos_ref[...] = lax.fori_loop(0, N_EXPERTS, per_expert, jnp.zeros_like(rk))


def _positions(e_idx, rank, row_start):
    n = e_idx.size
    shape = (n // LANES, LANES)
    grid_spec = pltpu.PrefetchScalarGridSpec(
        num_scalar_prefetch=1,
        grid=(1,),
        in_specs=[pl.BlockSpec(shape, lambda i, *_: (0, 0)), pl.BlockSpec(shape, lambda i, *_: (0, 0))],
        out_specs=pl.BlockSpec(shape, lambda i, *_: (0, 0)),
    )
    pos = pl.pallas_call(
        _positions_kernel,
        grid_spec=grid_spec,
        out_shape=jax.ShapeDtypeStruct(shape, jnp.int32),
    )(row_start, e_idx.reshape(shape), rank.reshape(shape))
    return pos.reshape(n)


def _dispatch_kernel(row_start_ref, cnt_ref, n_act_ref, pos_ref, h_ref, xs_ref, zero_ref, sem, zsem,
                     *, n_tokens):
    R = h_ref.shape[0] // ROW_TILE
    BM = EXPERT_ROWS
    n_blocks = xs_ref.shape[0] // (BM * ROW_TILE)
    n_pad_units = n_blocks - n_tokens * TOP_K // BM

    @pl.when(pl.program_id(0) == 0)
    def _zero_padding():
        zero_ref[...] = jnp.zeros_like(zero_ref)

        def expert_tail(e, carry):
            n_tail = pl.multiple_of(((BM - cnt_ref[e] % BM) % BM) * ROW_TILE, ROW_TILE)

            @pl.when(n_tail > 0)
            def _():
                dst = pl.multiple_of((row_start_ref[e] + cnt_ref[e]) * ROW_TILE, ROW_TILE)
                pltpu.make_async_copy(zero_ref.at[pl.ds(0, n_tail)], xs_ref.at[pl.ds(dst, n_tail)], zsem).start()
            return carry

        lax.fori_loop(0, N_EXPERTS, expert_tail, 0)

        def idle_block(i, carry):
            dst = pl.multiple_of(i * (BM * ROW_TILE), BM * ROW_TILE)
            pltpu.make_async_copy(zero_ref, xs_ref.at[pl.ds(dst, BM * ROW_TILE)], zsem).start()
            return carry

        lax.fori_loop(n_act_ref[0], n_blocks, idle_block, 0)

        def drain(i, carry):
            pltpu.make_async_copy(zero_ref, xs_ref.at[pl.ds(0, BM * ROW_TILE)], zsem).wait()
            return carry

        lax.fori_loop(0, n_pad_units, drain, 0)

    def issue(t, carry):
        src = h_ref.at[_row_tile(t)]
        for k in range(TOP_K):
            dest = pos_ref[t * TOP_K + k]
            pltpu.make_async_copy(src, xs_ref.at[_row_tile(dest)], sem).start(priority=k % 2)
        return carry

    lax.fori_loop(0, R, issue, 0)
    n = R * TOP_K * ROW_TILE
    pltpu.make_async_copy(xs_ref.at[pl.ds(0, n)], xs_ref.at[pl.ds(0, n)], sem).wait()


def _dispatch(h2, pos, row_start, cnt, n_act, n_rows):
    T = h2.shape[0] // ROW_TILE
    R = DISPATCH_ROWS
    grid_spec = pltpu.PrefetchScalarGridSpec(
        num_scalar_prefetch=3,
        grid=(T // R,),
        in_specs=[
            pl.BlockSpec((R * TOP_K,), lambda i, *_: (i,), memory_space=pltpu.SMEM),
            pl.BlockSpec((R * ROW_TILE, LANES), lambda i, *_: (i, 0)),
        ],
        out_specs=pl.BlockSpec(memory_space=pl.ANY),
        scratch_shapes=[pltpu.VMEM((EXPERT_ROWS * ROW_TILE, LANES), jnp.float32),
                        pltpu.SemaphoreType.DMA(()), pltpu.SemaphoreType.DMA(())],
    )
    return pl.pallas_call(
        functools.partial(_dispatch_kernel, n_tokens=T),
        grid_spec=grid_spec,
        out_shape=jax.ShapeDtypeStruct((n_rows * ROW_TILE, LANES), jnp.float32),
        compiler_params=pltpu.CompilerParams(
            dimension_semantics=("arbitrary",), vmem_limit_bytes=VMEM_LIMIT_BYTES),
    )(row_start, cnt, n_act, pos, h2)


X_SLOTS = 4
Y_SLOTS = 3


def _experts_kernel(blk_e_ref, first_ref, slot_ref, next_e_ref, n_act_ref, xs_hbm, wg_hbm, wu_hbm, wd_hbm,
                    ys_hbm, x_buf, y_buf, wg_buf, wu_buf, wd_buf, wg_bf, wu_bf, wd_bf, sems, x_sems, y_sems):
    i = pl.program_id(0)
    n_act = n_act_ref[0]
    bf16 = jnp.bfloat16
    blk = EXPERT_ROWS * ROW_TILE

    def block_rows(j):
        return pl.ds(pl.multiple_of(j * blk, blk), blk)

    def x_copy(j):
        return pltpu.make_async_copy(xs_hbm.at[block_rows(j)], x_buf.at[j % X_SLOTS], x_sems.at[j % X_SLOTS])

    def y_copy(j):
        return pltpu.make_async_copy(y_buf.at[j % Y_SLOTS], ys_hbm.at[block_rows(j)], y_sems.at[j % Y_SLOTS])

    @pl.when(i == 0)
    def _prime():
        for j in range(X_SLOTS - 1):
            @pl.when(j < n_act)
            def _():
                x_copy(j).start()

    @pl.when(i + (X_SLOTS - 1) < n_act)
    def _prefetch():
        x_copy(i + (X_SLOTS - 1)).start()

    def weight_copies(e, slot):
        return (pltpu.make_async_copy(wg_hbm.at[e], wg_buf.at[slot], sems.at[slot]),
                pltpu.make_async_copy(wu_hbm.at[e], wu_buf.at[slot], sems.at[slot]),
                pltpu.make_async_copy(wd_hbm.at[e], wd_buf.at[slot], sems.at[slot]))

    @pl.when((i < n_act_ref[0]) & (first_ref[i] == 1))
    def _new_expert():
        slot = slot_ref[i]

        @pl.when(i == 0)
        def _():
            for c in weight_copies(blk_e_ref[0], 0):
                c.start()

        for c in weight_copies(blk_e_ref[i], slot):
            c.wait()

        @pl.when(next_e_ref[i] >= 0)
        def _():
            for c in weight_copies(next_e_ref[i], 1 - slot):
                c.start()

        wg_bf[...] = wg_buf[slot].astype(bf16)
        wu_bf[...] = wu_buf[slot].astype(bf16)
        wd_bf[...] = wd_buf[slot].astype(bf16)

    @pl.when(i < n_act)
    def _compute():
        x_copy(i).wait()
        x = _load_rows(x_buf, EXPERT_ROWS, lead=(i % X_SLOTS,)).astype(bf16)
        g = _dot(x, wg_bf[...])
        u = _dot(x, wu_bf[...])
        a = (_silu(g) * u).astype(bf16)
        y = _dot(a, wd_bf[...])

        @pl.when(i >= Y_SLOTS)
        def _():
            y_copy(i - Y_SLOTS).wait()

        _store_rows(y_buf, y, lead=(i % Y_SLOTS,))
        y_copy(i).start()

    @pl.when(i == n_act - 1)
    def _drain():
        for d in range(Y_SLOTS):
            @pl.when(i - d >= 0)
            def _():
                y_copy(i - d).wait()


def _experts(xs, blk_e, n_act, w_gate, w_up, w_down):
    D = D_MODEL
    BM = EXPERT_ROWS
    F = EXPERT_DIM
    n_blocks = xs.shape[0] // (BM * ROW_TILE)
    ids = jnp.arange(n_blocks, dtype=jnp.int32)
    active = ids < n_act[0]
    first = active & ((ids == 0) | (blk_e != jnp.roll(blk_e, 1)))
    slot = ((jnp.cumsum(first.astype(jnp.int32)) - 1) % 2).astype(jnp.int32)
    first_pos = jnp.where(first, ids, n_blocks)
    later_first = lax.cummin(jnp.concatenate([first_pos[1:], jnp.full((1,), n_blocks, jnp.int32)]), reverse=True)
    next_e = jnp.where(later_first < n_blocks, blk_e[jnp.minimum(later_first, n_blocks - 1)], -1).astype(jnp.int32)

    grid_spec = pltpu.PrefetchScalarGridSpec(
        num_scalar_prefetch=5,
        grid=(n_blocks,),
        in_specs=[pl.BlockSpec(memory_space=pl.ANY)] * 4,
        out_specs=pl.BlockSpec(memory_space=pl.ANY),
        scratch_shapes=[
            pltpu.VMEM((X_SLOTS, BM * ROW_TILE, LANES), jnp.float32),
            pltpu.VMEM((Y_SLOTS, BM * ROW_TILE, LANES), jnp.float32),
            pltpu.VMEM((2, D, F), jnp.float32), pltpu.VMEM((2, D, F), jnp.float32),
            pltpu.VMEM((2, F, D), jnp.float32),
            pltpu.VMEM((D, F), jnp.bfloat16), pltpu.VMEM((D, F), jnp.bfloat16),
            pltpu.VMEM((F, D), jnp.bfloat16),
            pltpu.SemaphoreType.DMA((2,)), pltpu.SemaphoreType.DMA((X_SLOTS,)),
            pltpu.SemaphoreType.DMA((Y_SLOTS,)),
        ],
    )
    return pl.pallas_call(
        _experts_kernel,
        grid_spec=grid_spec,
        out_shape=jax.ShapeDtypeStruct(xs.shape, jnp.float32),
        input_output_aliases={5: 0},
        compiler_params=pltpu.CompilerParams(
            dimension_semantics=("arbitrary",), vmem_limit_bytes=VMEM_LIMIT_BYTES),
    )(blk_e, first.astype(jnp.int32), slot, next_e, n_act, xs, w_gate, w_up, w_down)


def _combine_kernel(pos_ref, pos_next_ref, h_ref, w_ref, ys_ref, wsg_ref, wsu_ref, wsd_ref,
                    g_ref, b_ref, out_ref, buf_ref, sems):
    R = h_ref.shape[0] // ROW_TILE
    i = pl.program_id(0)
    slot = i % 2

    def gather(p_ref, s):
        def issue(t, carry):
            for k in range(TOP_K):
                src = p_ref[t * TOP_K + k]
                pltpu.make_async_copy(ys_ref.at[_row_tile(src)], buf_ref.at[s, k, _row_tile(t)],
                                      sems.at[s]).start(priority=k % 2)
            return carry

        lax.fori_loop(0, R, issue, 0)

    @pl.when(i == 0)
    def _():
        gather(pos_ref, 0)

    @pl.when(i + 1 < pl.num_programs(0))
    def _():
        gather(pos_next_ref, 1 - slot)

    h = _load_rows(h_ref, R)
    hb = h.astype(jnp.bfloat16)
    act = (_silu(_dot(hb, wsg_ref[...])) * _dot(hb, wsu_ref[...])).astype(jnp.bfloat16)
    ffn = _dot(act, wsd_ref[...])
    pltpu.make_async_copy(buf_ref.at[slot], buf_ref.at[slot], sems.at[slot]).wait()
    w = w_ref[...]
    for k in range(TOP_K):
        ffn = ffn + _load_rows(buf_ref, R, lead=(slot, k)) * w[:, k:k + 1]
    out_ref[...] = _layer_norm(DEEPNORM_ALPHA * h + ffn, g_ref[...], b_ref[...])


def _combine(h2, pos, top_w, ys, ws_gate, ws_up, ws_down, ln_g, ln_b):
    T, D = h2.shape[0] // ROW_TILE, D_MODEL
    R = COMBINE_ROWS
    F = SHARED_DIM
    bf16 = jnp.bfloat16
    const = lambda shape: pl.BlockSpec(shape, lambda i: (0,) * len(shape))
    return pl.pallas_call(
        _combine_kernel,
        grid=(T // R,),
        in_specs=[
            pl.BlockSpec((R * TOP_K,), lambda i: (i,), memory_space=pltpu.SMEM),
            pl.BlockSpec((R * TOP_K,), lambda i: (jnp.minimum(i + 1, T // R - 1),), memory_space=pltpu.SMEM),
            pl.BlockSpec((R * ROW_TILE, LANES), lambda i: (i, 0)),
            pl.BlockSpec((R, TOP_K), lambda i: (i, 0)),
            pl.BlockSpec(memory_space=pl.ANY),
            const((D, F)), const((D, F)), const((F, D)), const((1, D)), const((1, D)),
        ],
        out_specs=pl.BlockSpec((R, D), lambda i: (i, 0)),
        scratch_shapes=[pltpu.VMEM((2, TOP_K, R * ROW_TILE, LANES), jnp.float32), pltpu.SemaphoreType.DMA((2,))],
        out_shape=jax.ShapeDtypeStruct((T, D), jnp.float32),
        compiler_params=pltpu.CompilerParams(
            dimension_semantics=("arbitrary",), vmem_limit_bytes=VMEM_LIMIT_BYTES),
    )(pos, pos, h2, top_w, ys, ws_gate.astype(bf16), ws_up.astype(bf16),
      ws_down.astype(bf16), ln_g.reshape(1, D), ln_b.reshape(1, D))


def _moe(h2, w_router, router_bias, w_gate, w_up, w_down, ws_gate, ws_up, ws_down, ln_g, ln_b):
    T = h2.shape[0] // ROW_TILE
    E = N_EXPERTS
    BM = EXPERT_ROWS
    e_idx, top_w, rank, counts = _router(h2, w_router, router_bias)
    cnt = counts[:, 0].astype(jnp.int32)
    nblk = (cnt + BM - 1) // BM
    blk_end = jnp.cumsum(nblk)
    row_start = ((blk_end - nblk) * BM).astype(jnp.int32)
    n_blocks = T * TOP_K // BM + E
    n_act = blk_end[-1:].astype(jnp.int32)
    blk_ids = jnp.minimum(jnp.arange(n_blocks, dtype=jnp.int32), n_act[0] - 1)
    blk_e = jnp.minimum(jnp.sum(blk_end[None, :] <= blk_ids[:, None], axis=1), E - 1).astype(jnp.int32)
    pos = _positions(e_idx, rank, row_start).reshape(TOP_K, T).T.reshape(T * TOP_K)
    xs = _dispatch(h2, pos, row_start, cnt, n_act, n_blocks * BM)
    ys = _experts(xs, blk_e, n_act, w_gate, w_up, w_down)
    return _combine(h2, pos, top_w.T, ys, ws_gate, ws_up, ws_down, ln_g, ln_b)


def kernel(x, ln_in_g, ln_in_b, w_in, w_out, rel_bias, attn_sinks, ln_mix_g, ln_mix_b, w_router,
           router_bias, w_gate, w_up, w_down, ws_gate, ws_up, ws_down, ln_ffn_g, ln_ffn_b):
    B, S, D = x.shape
    h = _mixer(x, ln_in_g, ln_in_b, w_in[0], w_out[0], rel_bias, attn_sinks[0], ln_mix_g[0], ln_mix_b[0])
    out = _moe(h, w_router[0], router_bias[0], w_gate[0], w_up[0], w_down[0],
               ws_gate[0], ws_up[0], ws_down[0], ln_ffn_g[0], ln_ffn_b[0])
    return out.reshape(B, S, D)
```

```python
import functools
import math

import jax
import jax.numpy as jnp
from jax import lax
from jax.experimental import pallas as pl
from jax.experimental.pallas import tpu as pltpu

D_MODEL = 1024
DEPTH = 1
RET_HEADS = 4
RET_QK_DIM = 64
RET_V_DIM = 128
RET_CHUNK = 128
RET_WIDTH = RET_HEADS * RET_V_DIM
ROPE_BASE = 10000.0
SWA_HEADS = 8
SWA_KV_HEADS = 2
SWA_GROUP = SWA_HEADS // SWA_KV_HEADS
SWA_HEAD_DIM = 64
SWA_WINDOW = 128
SWA_WIDTH = SWA_HEADS * SWA_HEAD_DIM
MIX_WIDTH = RET_WIDTH + SWA_WIDTH
RQK = RET_HEADS * RET_QK_DIM
SKV = SWA_KV_HEADS * SWA_HEAD_DIM
IN_SIZES = (RQK, RQK, RET_WIDTH, RET_WIDTH, SWA_WIDTH, SKV, SKV)
IN_OFFS = tuple(sum(IN_SIZES[:i]) for i in range(len(IN_SIZES)))
IN_WIDTH = sum(IN_SIZES)
REL_BUCKETS = 32
REL_MAX_DIST = 128
N_EXPERTS = 256
TOP_K = 8
N_GROUPS = 8
GROUP_SIZE = N_EXPERTS // N_GROUPS
TOPK_GROUPS = 4
EXPERT_DIM = 256
SHARED_DIM = 256
ROUTED_SCALE = 2.5
LN_EPS = 1e-5
GN_EPS = 1e-6
DEEPNORM_ALPHA = (2 * DEPTH) ** 0.25
MASK_VALUE = -1e30

VMEM_LIMIT_BYTES = 56 * 1024 * 1024

MIX_ROWS = 256
ROUTE_ROWS = 256
DISPATCH_ROWS = 256
EXPERT_ROWS = 256
COMBINE_ROWS = 128


def _layer_norm(x, g, b):
    mu = jnp.mean(x, axis=-1, keepdims=True)
    xc = x - mu
    var = jnp.mean(xc * xc, axis=-1, keepdims=True)
    return xc * lax.rsqrt(var + LN_EPS) * g + b


def _dot(a, b):
    return jnp.dot(a, b, preferred_element_type=jnp.float32)


def _dot_nt(a, b):
    return lax.dot_general(a, b, (((1,), (1,)), ((), ())), preferred_element_type=jnp.float32)


def _dot_tn(a, b):
    return lax.dot_general(a, b, (((0,), (0,)), ((), ())), preferred_element_type=jnp.float32)


def _silu(x):
    return x * (1.0 / (1.0 + jnp.exp(-x)))


LANES = 128
ROW_TILE = D_MODEL // LANES


def _load_rows(ref, n_rows, lead=()):
    return jnp.concatenate([ref[lead + (pl.ds(s, n_rows, stride=ROW_TILE), slice(None))]
                            for s in range(ROW_TILE)], axis=1)


def _store_rows(ref, val, lead=()):
    n_rows = val.shape[0]
    for s in range(ROW_TILE):
        ref[lead + (pl.ds(s, n_rows, stride=ROW_TILE), slice(None))] = val[:, s * LANES:(s + 1) * LANES]


def _row_tile(r):
    return pl.ds(pl.multiple_of(r * ROW_TILE, ROW_TILE), ROW_TILE)


def _swap_halves(x):
    n = x.shape[-1]
    half = RET_QK_DIM // 2
    lane = lax.broadcasted_iota(jnp.int32, x.shape, 1)
    from_right = pltpu.roll(x, n - half, axis=1)
    from_left = pltpu.roll(x, half, axis=1)
    return jnp.where((lane % RET_QK_DIM) < half, from_right, from_left)


def _mixer_kernel(rel_bias_ref, x_ref, g_in_ref, b_in_ref, w_in_ref, w_out_ref, rot_ref, decay_ref,
                  zeta_ref, xi_ref, cdecay_ref, bucket_ref, sink_ref, g_mix_ref, b_mix_ref,
                  h2_ref, state_ref, kprev_ref, vprev_ref, bias_ref):
    b_id = pl.program_id(0)
    c_id = pl.program_id(1)
    W = SWA_WINDOW

    @pl.when((b_id == 0) & (c_id == 0))
    def _build_bias():
        bucket = bucket_ref[...]
        for h in range(SWA_HEADS):
            acc = jnp.full((2 * W, W), MASK_VALUE, jnp.float32)
            for b in range(REL_BUCKETS):
                acc = jnp.where(bucket == b, rel_bias_ref[b, h], acc)
            kh, g = divmod(h, SWA_GROUP)
            bias_ref[kh, :, g * W:(g + 1) * W] = acc

    @pl.when(c_id == 0)
    def _reset():
        state_ref[...] = jnp.zeros_like(state_ref)
        kprev_ref[...] = jnp.zeros_like(kprev_ref)
        vprev_ref[...] = jnp.zeros_like(vprev_ref)

    h = _layer_norm(x_ref[...], g_in_ref[...], b_in_ref[...])
    proj = _dot(h.astype(jnp.bfloat16), w_in_ref[...])

    o_q, o_k, o_v, o_g, o_sq, o_sk, o_sv = IN_OFFS
    cos_t = rot_ref[:, :RQK]
    sin_t = rot_ref[:, RQK:]
    q_all = proj[:, o_q:o_q + RQK]
    k_all = proj[:, o_k:o_k + RQK]
    q_rot = q_all * cos_t + _swap_halves(q_all) * sin_t
    k_rot = (k_all * cos_t + _swap_halves(k_all) * sin_t) * (RET_QK_DIM ** -0.5)

    n_sub = x_ref.shape[0] // RET_CHUNK
    states = [state_ref[hh] for hh in range(RET_HEADS)]
    k_prev = kprev_ref[...]
    v_prev = vprev_ref[...]
    cat_rows = []
    for s in range(n_sub):
        r0 = s * RET_CHUNK
        rows = slice(r0, r0 + RET_CHUNK)
        pieces = []
        for hh in range(RET_HEADS):
            qk = slice(hh * RET_QK_DIM, (hh + 1) * RET_QK_DIM)
            vv = slice(o_v + hh * RET_V_DIM, o_v + (hh + 1) * RET_V_DIM)
            gg = slice(o_g + hh * RET_V_DIM, o_g + (hh + 1) * RET_V_DIM)
            q = q_rot[rows, qk].astype(jnp.bfloat16)
            k32 = k_rot[rows, qk]
            v = proj[rows, vv].astype(jnp.bfloat16)
            scores = _dot_nt(q, k32.astype(jnp.bfloat16)) * decay_ref[hh]
            intra = _dot(scores.astype(jnp.bfloat16), v)
            inter = _dot(q, states[hh].astype(jnp.bfloat16)) * xi_ref[hh]
            ret = intra + inter
            kz = (k32 * zeta_ref[hh]).astype(jnp.bfloat16)
            states[hh] = states[hh] * cdecay_ref[hh] + _dot_tn(kz, v)
            mu = jnp.mean(ret, axis=-1, keepdims=True)
            rc = ret - mu
            var = jnp.mean(rc * rc, axis=-1, keepdims=True)
            normed = rc * lax.rsqrt(var + GN_EPS)
            pieces.append((_silu(proj[rows, gg]) * normed).astype(jnp.bfloat16))
        k_cur = proj[rows, o_sk:o_sk + SKV].astype(jnp.bfloat16)
        v_cur = proj[rows, o_sv:o_sv + SKV].astype(jnp.bfloat16)
        for kh in range(SWA_KV_HEADS):
            kv = slice(kh * SWA_HEAD_DIM, (kh + 1) * SWA_HEAD_DIM)
            q4 = jnp.concatenate(
                [proj[rows, o_sq + (kh * SWA_GROUP + g) * SWA_HEAD_DIM:
                      o_sq + (kh * SWA_GROUP + g + 1) * SWA_HEAD_DIM] for g in range(SWA_GROUP)],
                axis=0) * (SWA_HEAD_DIM ** -0.5)
            kcat = jnp.concatenate([k_prev[:, kv], k_cur[:, kv]], axis=0)
            vcat = jnp.concatenate([v_prev[:, kv], v_cur[:, kv]], axis=0)
            logits = _dot_nt(kcat, q4.astype(jnp.bfloat16)) + bias_ref[kh]
            if s == 0:
                key = lax.broadcasted_iota(jnp.int32, logits.shape, 0)
                logits = logits + jnp.where((key < W) & (c_id == 0), MASK_VALUE, 0.0)
            sink = sink_ref[kh]
            m = jnp.maximum(jnp.max(logits, axis=0, keepdims=True), sink)
            p = jnp.exp(logits - m)
            den = jnp.sum(p, axis=0, keepdims=True) + jnp.exp(sink - m)
            probs = (p / den).astype(jnp.bfloat16)
            o4 = _dot_tn(vcat, probs)
            pieces.extend(o4[:, g * W:(g + 1) * W].T.astype(jnp.bfloat16) for g in range(SWA_GROUP))
        k_prev, v_prev = k_cur, v_cur
        cat_rows.append(jnp.concatenate(pieces, axis=1))
    for hh in range(RET_HEADS):
        state_ref[hh] = states[hh]
    kprev_ref[...] = k_prev
    vprev_ref[...] = v_prev

    mix = _dot(jnp.concatenate(cat_rows, axis=0), w_out_ref[...])
    _store_rows(h2_ref, _layer_norm(DEEPNORM_ALPHA * h + mix, g_mix_ref[...], b_mix_ref[...]))


def _t5_bucket(dist):
    n = jnp.maximum(dist, 0)
    max_exact = REL_BUCKETS // 2
    ratio = jnp.log(jnp.maximum(n, 1).astype(jnp.float32) / max_exact) / math.log(REL_MAX_DIST / max_exact)
    large = jnp.minimum(max_exact + (ratio * (REL_BUCKETS - max_exact)).astype(jnp.int32), REL_BUCKETS - 1)
    return jnp.where(n < max_exact, n, large)


def _mixer(x, ln_in_g, ln_in_b, w_in, w_out, rel_bias, sinks, ln_mix_g, ln_mix_b):
    B, S, D = x.shape
    R = MIX_ROWS
    C = RET_CHUNK
    W = SWA_WINDOW
    f32 = jnp.float32
    half = RET_QK_DIM // 2
    inv = ROPE_BASE ** (-jnp.arange(half, dtype=f32) / half)
    ang = jnp.arange(S, dtype=f32)[:, None] * inv[None, :]
    cos, sin = jnp.cos(ang), jnp.sin(ang)
    cos_t = jnp.tile(jnp.concatenate([cos, cos], axis=-1), (1, RET_HEADS))
    sin_t = jnp.tile(jnp.concatenate([-sin, sin], axis=-1), (1, RET_HEADS))
    rot = jnp.concatenate([cos_t, sin_t], axis=-1)
    log_gamma = jnp.log(1.0 - 2.0 ** (-5.0 - jnp.arange(RET_HEADS, dtype=f32)))
    idx = jnp.arange(C, dtype=f32)
    diff = idx[:, None] - idx[None, :]
    decay = jnp.where(diff[None] >= 0, jnp.exp(jnp.maximum(diff, 0.0)[None] * log_gamma[:, None, None]), 0.0)
    zeta = jnp.exp((C - 1.0 - idx)[None, :] * log_gamma[:, None])
    xi = jnp.exp((idx + 1.0)[None, :] * log_gamma[:, None])
    zeta_b = jnp.broadcast_to(zeta[:, :, None], (RET_HEADS, C, RET_QK_DIM))
    xi_b = jnp.broadcast_to(xi[:, :, None], (RET_HEADS, C, RET_V_DIM))
    cdecay = jnp.broadcast_to(jnp.exp(C * log_gamma)[:, None, None], (RET_HEADS, RET_QK_DIM, RET_V_DIM))
    i = jnp.arange(W)
    j = jnp.arange(2 * W)
    dist = i[:, None] + W - j[None, :]
    bucket = jnp.where((dist >= 0) & (dist < W), _t5_bucket(dist), -1).astype(jnp.int32).T
    sink_row = jnp.repeat(sinks.astype(f32), W).reshape(SWA_KV_HEADS, 1, SWA_GROUP * W)

    const = lambda shape: pl.BlockSpec(shape, lambda b, c, *_: (0,) * len(shape))
    grid_spec = pltpu.PrefetchScalarGridSpec(
        num_scalar_prefetch=1,
        grid=(B, S // R),
        in_specs=[
            pl.BlockSpec((None, R, D), lambda b, c, *_: (b, c, 0)),
            const((1, D)), const((1, D)),
            const((D, IN_WIDTH)), const((MIX_WIDTH, D)),
            pl.BlockSpec((R, 2 * RQK), lambda b, c, *_: (c, 0)),
            const((RET_HEADS, C, C)), const((RET_HEADS, C, RET_QK_DIM)), const((RET_HEADS, C, RET_V_DIM)),
            const((RET_HEADS, RET_QK_DIM, RET_V_DIM)),
            const((2 * W, W)), const((SWA_KV_HEADS, 1, SWA_GROUP * W)),
            const((1, D)), const((1, D)),
        ],
        out_specs=pl.BlockSpec((R * ROW_TILE, LANES), lambda b, c, *_: (b * (S // R) + c, 0)),
        scratch_shapes=[
            pltpu.VMEM((RET_HEADS, RET_QK_DIM, RET_V_DIM), f32),
            pltpu.VMEM((W, SKV), jnp.bfloat16),
            pltpu.VMEM((W, SKV), jnp.bfloat16),
            pltpu.VMEM((SWA_KV_HEADS, 2 * W, SWA_GROUP * W), f32),
        ],
    )
    return pl.pallas_call(
        _mixer_kernel,
        grid_spec=grid_spec,
        out_shape=jax.ShapeDtypeStruct((B * S * ROW_TILE, LANES), f32),
        compiler_params=pltpu.CompilerParams(
            dimension_semantics=("arbitrary", "arbitrary"), vmem_limit_bytes=VMEM_LIMIT_BYTES),
    )(rel_bias.astype(f32), x, ln_in_g.reshape(1, D), ln_in_b.reshape(1, D),
      w_in.astype(jnp.bfloat16), w_out.astype(jnp.bfloat16), rot, decay, zeta_b, xi_b, cdecay,
      bucket, sink_row, ln_mix_g.reshape(1, D), ln_mix_b.reshape(1, D))


def _router_kernel(h_ref, wr_ref, rb_ref, e_ref, w_ref, rk_ref, cnt_ref, run_ref):
    f32 = jnp.float32
    R = h_ref.shape[0] // ROW_TILE
    E = N_EXPERTS
    neg = -jnp.inf

    @pl.when(pl.program_id(0) == 0)
    def _init():
        run_ref[...] = jnp.zeros_like(run_ref)

    logits = lax.dot_general(wr_ref[...], _load_rows(h_ref, R), (((1,), (1,)), ((), ())),
                             precision=lax.Precision.HIGHEST, preferred_element_type=f32)
    scores = 1.0 / (1.0 + jnp.exp(-logits))
    choice = scores + rb_ref[...]
    eid = lax.broadcasted_iota(jnp.int32, (E, R), 0)

    def first_argmax(vals, ids, none):
        m = jnp.max(vals, axis=0, keepdims=True)
        idx = jnp.min(jnp.where(vals == m, ids, none), axis=0, keepdims=True)
        return m, idx

    gid = lax.broadcasted_iota(jnp.int32, (GROUP_SIZE, R), 0)
    groups, gscore = [], []
    for g in range(N_GROUPS):
        vals = choice[g * GROUP_SIZE:(g + 1) * GROUP_SIZE]
        m1, i1 = first_argmax(vals, gid, GROUP_SIZE)
        m2 = jnp.max(jnp.where(gid == i1, neg, vals), axis=0, keepdims=True)
        groups.append(vals)
        gscore.append(m1 + m2)
    kept = []
    for g in range(N_GROUPS):
        beaten = jnp.zeros((1, R), f32)
        for g2 in range(N_GROUPS):
            if g2 == g:
                continue
            ahead = (gscore[g2] > gscore[g]) | (gscore[g2] == gscore[g]) if g2 < g else gscore[g2] > gscore[g]
            beaten = beaten + jnp.where(ahead, 1.0, 0.0)
        kept.append(jnp.where(beaten < TOPK_GROUPS, groups[g], neg))
    masked = jnp.concatenate(kept, axis=0)

    idxs, wts = [], []
    picked = jnp.zeros((E, R), f32)
    for _ in range(TOP_K):
        _, idx = first_argmax(masked, eid, E)
        hit = eid == idx
        idxs.append(idx)
        wts.append(jnp.sum(jnp.where(hit, scores, 0.0), axis=0, keepdims=True))
        masked = jnp.where(hit, neg, masked)
        picked = jnp.where(hit, 1.0, picked)
    wsum = wts[0]
    for k in range(1, TOP_K):
        wsum = wsum + wts[k]

    row = lax.broadcasted_iota(jnp.int32, (R, R), 0)
    col = lax.broadcasted_iota(jnp.int32, (R, R), 1)
    earlier = jnp.where(row < col, 1.0, 0.0).astype(jnp.bfloat16)
    picked_bf = picked.astype(jnp.bfloat16)
    run = run_ref[...]
    before = _dot(picked_bf, earlier) + jnp.concatenate([run] * (R // LANES), axis=1)
    sub_k = lax.broadcasted_iota(jnp.int32, (TOP_K, R), 0)
    e_out = jnp.zeros((TOP_K, R), jnp.int32)
    w_out = jnp.zeros((TOP_K, R), f32)
    rk_out = jnp.zeros((TOP_K, R), jnp.int32)
    for k in range(TOP_K):
        rank_k = jnp.sum(jnp.where(eid == idxs[k], before, 0.0), axis=0, keepdims=True)
        e_out = jnp.where(sub_k == k, idxs[k], e_out)
        w_out = jnp.where(sub_k == k, wts[k] / wsum * ROUTED_SCALE, w_out)
        rk_out = jnp.where(sub_k == k, rank_k.astype(jnp.int32), rk_out)
    e_ref[...] = e_out
    w_ref[...] = w_out
    rk_ref[...] = rk_out
    run_ref[...] = run + _dot(picked_bf, jnp.ones((R, LANES), jnp.bfloat16))
    cnt_ref[...] = run_ref[...]


def _router(h2, w_router, router_bias):
    T, D = h2.shape[0] // ROW_TILE, D_MODEL
    R = ROUTE_ROWS
    E = N_EXPERTS
    return pl.pallas_call(
        _router_kernel,
        grid=(T // R,),
        in_specs=[
            pl.BlockSpec((R * ROW_TILE, LANES), lambda i: (i, 0)),
            pl.BlockSpec((E, D), lambda i: (0, 0)),
            pl.BlockSpec((E, R), lambda i: (0, 0)),
        ],
        out_specs=[
            pl.BlockSpec((TOP_K, R), lambda i: (0, i)),
            pl.BlockSpec((TOP_K, R), lambda i: (0, i)),
            pl.BlockSpec((TOP_K, R), lambda i: (0, i)),
            pl.BlockSpec((E, LANES), lambda i: (0, 0)),
        ],
        out_shape=[
            jax.ShapeDtypeStruct((TOP_K, T), jnp.int32),
            jax.ShapeDtypeStruct((TOP_K, T), jnp.float32),
            jax.ShapeDtypeStruct((TOP_K, T), jnp.int32),
            jax.ShapeDtypeStruct((E, LANES), jnp.float32),
        ],
        scratch_shapes=[pltpu.VMEM((E, LANES), jnp.float32)],
        compiler_params=pltpu.CompilerParams(
            dimension_semantics=("arbitrary",), vmem_limit_bytes=VMEM_LIMIT_BYTES),
    )(h2, w_router.T, jnp.broadcast_to(router_bias.astype(jnp.float32)[:, None], (E, R)))


def _positions_kernel(row_start_ref, e_ref, rk_ref, pos_ref):
    e = e_ref[...]
    rk = rk_ref[...]

    def per_expert(i, pos):
        return jnp.where(e == i, rk + row_start_ref[i], pos)

    pos_ref[...] = lax.fori_loop(0, N_EXPERTS, per_expert, jnp.zeros_like(rk))


def _positions(e_idx, rank, row_start):
    n = e_idx.size
    shape = (n // LANES, LANES)
    grid_spec = pltpu.PrefetchScalarGridSpec(
        num_scalar_prefetch=1,
        grid=(1,),
        in_specs=[pl.BlockSpec(shape, lambda i, *_: (0, 0)), pl.BlockSpec(shape, lambda i, *_: (0, 0))],
        out_specs=pl.BlockSpec(shape, lambda i, *_: (0, 0)),
    )
    pos = pl.pallas_call(
        _positions_kernel,
        grid_spec=grid_spec,
        out_shape=jax.ShapeDtypeStruct(shape, jnp.int32),
    )(row_start, e_idx.reshape(shape), rank.reshape(shape))
    return pos.reshape(n)


def _dispatch_kernel(row_start_ref, cnt_ref, n_act_ref, pos_ref, h_ref, xs_ref, zero_ref, sem, zsem,
                     *, n_tokens):
    R = h_ref.shape[0] // ROW_TILE
    BM = EXPERT_ROWS
    n_blocks = xs_ref.shape[0] // (BM * ROW_TILE)
    n_pad_units = n_blocks - n_tokens * TOP_K // BM

    @pl.when(pl.program_id(0) == 0)
    def _zero_padding():
        zero_ref[...] = jnp.zeros_like(zero_ref)

        def expert_tail(e, carry):
            n_tail = pl.multiple_of(((BM - cnt_ref[e] % BM) % BM) * ROW_TILE, ROW_TILE)

            @pl.when(n_tail > 0)
            def _():
                dst = pl.multiple_of((row_start_ref[e] + cnt_ref[e]) * ROW_TILE, ROW_TILE)
                pltpu.make_async_copy(zero_ref.at[pl.ds(0, n_tail)], xs_ref.at[pl.ds(dst, n_tail)], zsem).start()
            return carry

        lax.fori_loop(0, N_EXPERTS, expert_tail, 0)

        def idle_block(i, carry):
            dst = pl.multiple_of(i * (BM * ROW_TILE), BM * ROW_TILE)
            pltpu.make_async_copy(zero_ref, xs_ref.at[pl.ds(dst, BM * ROW_TILE)], zsem).start()
            return carry

        lax.fori_loop(n_act_ref[0], n_blocks, idle_block, 0)

        def drain(i, carry):
            pltpu.make_async_copy(zero_ref, xs_ref.at[pl.ds(0, BM * ROW_TILE)], zsem).wait()
            return carry

        lax.fori_loop(0, n_pad_units, drain, 0)

    def issue(t, carry):
        src = h_ref.at[_row_tile(t)]
        for k in range(TOP_K):
            dest = pos_ref[t * TOP_K + k]
            pltpu.make_async_copy(src, xs_ref.at[_row_tile(dest)], sem).start(priority=k % 2)
        return carry

    lax.fori_loop(0, R, issue, 0)
    n = R * TOP_K * ROW_TILE
    pltpu.make_async_copy(xs_ref.at[pl.ds(0, n)], xs_ref.at[pl.ds(0, n)], sem).wait()


def _dispatch(h2, pos, row_start, cnt, n_act, n_rows):
    T = h2.shape[0] // ROW_TILE
    R = DISPATCH_ROWS
    grid_spec = pltpu.PrefetchScalarGridSpec(
        num_scalar_prefetch=3,
        grid=(T // R,),
        in_specs=[
            pl.BlockSpec((R * TOP_K,), lambda i, *_: (i,), memory_space=pltpu.SMEM),
            pl.BlockSpec((R * ROW_TILE, LANES), lambda i, *_: (i, 0)),
        ],
        out_specs=pl.BlockSpec(memory_space=pl.ANY),
        scratch_shapes=[pltpu.VMEM((EXPERT_ROWS * ROW_TILE, LANES), jnp.float32),
                        pltpu.SemaphoreType.DMA(()), pltpu.SemaphoreType.DMA(())],
    )
    return pl.pallas_call(
        functools.partial(_dispatch_kernel, n_tokens=T),
        grid_spec=grid_spec,
        out_shape=jax.ShapeDtypeStruct((n_rows * ROW_TILE, LANES), jnp.float32),
        compiler_params=pltpu.CompilerParams(
            dimension_semantics=("arbitrary",), vmem_limit_bytes=VMEM_LIMIT_BYTES),
    )(row_start, cnt, n_act, pos, h2)


X_SLOTS = 4
Y_SLOTS = 3


def _experts_kernel(blk_e_ref, first_ref, slot_ref, next_e_ref, valid_ref, n_act_ref, xs_hbm, wg_hbm, wu_hbm,
                    wd_hbm, ys_hbm, x_buf, y_buf, wg_buf, wu_buf, wd_buf, wg_bf, wu_bf, wd_bf, sems, x_sems,
                    y_sems):
    i = pl.program_id(0)
    n_act = n_act_ref[0]
    bf16 = jnp.bfloat16
    blk = EXPERT_ROWS * ROW_TILE

    def hbm_rows(j):
        return pl.ds(pl.multiple_of(j * blk, blk), pl.multiple_of(valid_ref[j] * ROW_TILE, ROW_TILE))

    def vmem_rows(j):
        return pl.ds(0, pl.multiple_of(valid_ref[j] * ROW_TILE, ROW_TILE))

    def x_copy(j):
        return pltpu.make_async_copy(xs_hbm.at[hbm_rows(j)], x_buf.at[j % X_SLOTS, vmem_rows(j)],
                                     x_sems.at[j % X_SLOTS])

    def y_copy(j):
        return pltpu.make_async_copy(y_buf.at[j % Y_SLOTS, vmem_rows(j)], ys_hbm.at[hbm_rows(j)],
                                     y_sems.at[j % Y_SLOTS])

    @pl.when(i == 0)
    def _prime():
        x_buf[...] = jnp.zeros_like(x_buf)
        for j in range(X_SLOTS - 1):
            @pl.when(j < n_act)
            def _():
                x_copy(j).start()

    @pl.when(i + (X_SLOTS - 1) < n_act)
    def _prefetch():
        x_copy(i + (X_SLOTS - 1)).start()

    def weight_copies(e, slot):
        return (pltpu.make_async_copy(wg_hbm.at[e], wg_buf.at[slot], sems.at[slot]),
                pltpu.make_async_copy(wu_hbm.at[e], wu_buf.at[slot], sems.at[slot]),
                pltpu.make_async_copy(wd_hbm.at[e], wd_buf.at[slot], sems.at[slot]))

    @pl.when((i < n_act_ref[0]) & (first_ref[i] == 1))
    def _new_expert():
        slot = slot_ref[i]

        @pl.when(i == 0)
        def _():
            for c in weight_copies(blk_e_ref[0], 0):
                c.start()

        for c in weight_copies(blk_e_ref[i], slot):
            c.wait()

        @pl.when(next_e_ref[i] >= 0)
        def _():
            for c in weight_copies(next_e_ref[i], 1 - slot):
                c.start()

        wg_bf[...] = wg_buf[slot].astype(bf16)
        wu_bf[...] = wu_buf[slot].astype(bf16)
        wd_bf[...] = wd_buf[slot].astype(bf16)

    @pl.when(i < n_act)
    def _compute():
        x_copy(i).wait()
        x = _load_rows(x_buf, EXPERT_ROWS, lead=(i % X_SLOTS,)).astype(bf16)
        g = _dot(x, wg_bf[...])
        u = _dot(x, wu_bf[...])
        a = (_silu(g) * u).astype(bf16)
        y = _dot(a, wd_bf[...])

        @pl.when(i >= Y_SLOTS)
        def _():
            y_copy(i - Y_SLOTS).wait()

        _store_rows(y_buf, y, lead=(i % Y_SLOTS,))
        y_copy(i).start()

    @pl.when(i == n_act - 1)
    def _drain():
        for d in range(Y_SLOTS):
            @pl.when(i - d >= 0)
            def _():
                y_copy(i - d).wait()


def _experts(xs, blk_e, n_act, row_start, cnt, w_gate, w_up, w_down):
    D = D_MODEL
    BM = EXPERT_ROWS
    F = EXPERT_DIM
    n_blocks = xs.shape[0] // (BM * ROW_TILE)
    blk_in_expert = jnp.arange(n_blocks, dtype=jnp.int32) - row_start[blk_e] // BM
    valid = jnp.clip(cnt[blk_e] - blk_in_expert * BM, 0, BM).astype(jnp.int32)
    ids = jnp.arange(n_blocks, dtype=jnp.int32)
    active = ids < n_act[0]
    first = active & ((ids == 0) | (blk_e != jnp.roll(blk_e, 1)))
    slot = ((jnp.cumsum(first.astype(jnp.int32)) - 1) % 2).astype(jnp.int32)
    first_pos = jnp.where(first, ids, n_blocks)
    later_first = lax.cummin(jnp.concatenate([first_pos[1:], jnp.full((1,), n_blocks, jnp.int32)]), reverse=True)
    next_e = jnp.where(later_first < n_blocks, blk_e[jnp.minimum(later_first, n_blocks - 1)], -1).astype(jnp.int32)

    grid_spec = pltpu.PrefetchScalarGridSpec(
        num_scalar_prefetch=6,
        grid=(n_blocks,),
        in_specs=[pl.BlockSpec(memory_space=pl.ANY)] * 4,
        out_specs=pl.BlockSpec(memory_space=pl.ANY),
        scratch_shapes=[
            pltpu.VMEM((X_SLOTS, BM * ROW_TILE, LANES), jnp.float32),
            pltpu.VMEM((Y_SLOTS, BM * ROW_TILE, LANES), jnp.float32),
            pltpu.VMEM((2, D, F), jnp.float32), pltpu.VMEM((2, D, F), jnp.float32),
            pltpu.VMEM((2, F, D), jnp.float32),
            pltpu.VMEM((D, F), jnp.bfloat16), pltpu.VMEM((D, F), jnp.bfloat16),
            pltpu.VMEM((F, D), jnp.bfloat16),
            pltpu.SemaphoreType.DMA((2,)), pltpu.SemaphoreType.DMA((X_SLOTS,)),
            pltpu.SemaphoreType.DMA((Y_SLOTS,)),
        ],
    )
    return pl.pallas_call(
        _experts_kernel,
        grid_spec=grid_spec,
        out_shape=jax.ShapeDtypeStruct(xs.shape, jnp.float32),
        input_output_aliases={6: 0},
        compiler_params=pltpu.CompilerParams(
            dimension_semantics=("arbitrary",), vmem_limit_bytes=VMEM_LIMIT_BYTES),
    )(blk_e, first.astype(jnp.int32), slot, next_e, valid, n_act, xs, w_gate, w_up, w_down)


def _combine_kernel(pos_ref, pos_next_ref, h_ref, w_ref, ys_ref, wsg_ref, wsu_ref, wsd_ref,
                    g_ref, b_ref, out_ref, buf_ref, sems):
    R = h_ref.shape[0] // ROW_TILE
    i = pl.program_id(0)
    slot = i % 2

    def gather(p_ref, s):
        def issue(t, carry):
            for k in range(TOP_K):
                src = p_ref[t * TOP_K + k]
                pltpu.make_async_copy(ys_ref.at[_row_tile(src)], buf_ref.at[s, k, _row_tile(t)],
                                      sems.at[s]).start(priority=k % 2)
            return carry

        lax.fori_loop(0, R, issue, 0)

    @pl.when(i == 0)
    def _():
        gather(pos_ref, 0)

    @pl.when(i + 1 < pl.num_programs(0))
    def _():
        gather(pos_next_ref, 1 - slot)

    h = _load_rows(h_ref, R)
    hb = h.astype(jnp.bfloat16)
    act = (_silu(_dot(hb, wsg_ref[...])) * _dot(hb, wsu_ref[...])).astype(jnp.bfloat16)
    ffn = _dot(act, wsd_ref[...])
    pltpu.make_async_copy(buf_ref.at[slot], buf_ref.at[slot], sems.at[slot]).wait()
    w = w_ref[...]
    for k in range(TOP_K):
        ffn = ffn + _load_rows(buf_ref, R, lead=(slot, k)) * w[:, k:k + 1]
    out_ref[...] = _layer_norm(DEEPNORM_ALPHA * h + ffn, g_ref[...], b_ref[...])


def _combine(h2, pos, top_w, ys, ws_gate, ws_up, ws_down, ln_g, ln_b):
    T, D = h2.shape[0] // ROW_TILE, D_MODEL
    R = COMBINE_ROWS
    F = SHARED_DIM
    bf16 = jnp.bfloat16
    const = lambda shape: pl.BlockSpec(shape, lambda i: (0,) * len(shape))
    return pl.pallas_call(
        _combine_kernel,
        grid=(T // R,),
        in_specs=[
            pl.BlockSpec((R * TOP_K,), lambda i: (i,), memory_space=pltpu.SMEM),
            pl.BlockSpec((R * TOP_K,), lambda i: (jnp.minimum(i + 1, T // R - 1),), memory_space=pltpu.SMEM),
            pl.BlockSpec((R * ROW_TILE, LANES), lambda i: (i, 0)),
            pl.BlockSpec((R, TOP_K), lambda i: (i, 0)),
            pl.BlockSpec(memory_space=pl.ANY),
            const((D, F)), const((D, F)), const((F, D)), const((1, D)), const((1, D)),
        ],
        out_specs=pl.BlockSpec((R, D), lambda i: (i, 0)),
        scratch_shapes=[pltpu.VMEM((2, TOP_K, R * ROW_TILE, LANES), jnp.float32), pltpu.SemaphoreType.DMA((2,))],
        out_shape=jax.ShapeDtypeStruct((T, D), jnp.float32),
        compiler_params=pltpu.CompilerParams(
            dimension_semantics=("arbitrary",), vmem_limit_bytes=VMEM_LIMIT_BYTES),
    )(pos, pos, h2, top_w, ys, ws_gate.astype(bf16), ws_up.astype(bf16),
      ws_down.astype(bf16), ln_g.reshape(1, D), ln_b.reshape(1, D))


def _moe(h2, w_router, router_bias, w_gate, w_up, w_down, ws_gate, ws_up, ws_down, ln_g, ln_b):
    T = h2.shape[0] // ROW_TILE
    E = N_EXPERTS
    BM = EXPERT_ROWS
    e_idx, top_w, rank, counts = _router(h2, w_router, router_bias)
    cnt = counts[:, 0].astype(jnp.int32)
    nblk = (cnt + BM - 1) // BM
    blk_end = jnp.cumsum(nblk)
    row_start = ((blk_end - nblk) * BM).astype(jnp.int32)
    n_blocks = T * TOP_K // BM + E
    n_act = blk_end[-1:].astype(jnp.int32)
    blk_ids = jnp.minimum(jnp.arange(n_blocks, dtype=jnp.int32), n_act[0] - 1)
    blk_e = jnp.minimum(jnp.sum(blk_end[None, :] <= blk_ids[:, None], axis=1), E - 1).astype(jnp.int32)
    pos = _positions(e_idx, rank, row_start).reshape(TOP_K, T).T.reshape(T * TOP_K)
    xs = _dispatch(h2, pos, row_start, cnt, n_act, n_blocks * BM)
    ys = _experts(xs, blk_e, n_act, row_start, cnt, w_gate, w_up, w_down)
    return _combine(h2, pos, top_w.T, ys, ws_gate, ws_up, ws_down, ln_g, ln_b)


def kernel(x, ln_in_g, ln_in_b, w_in, w_out, rel_bias, attn_sinks, ln_mix_g, ln_mix_b, w_router,
           router_bias, w_gate, w_up, w_down, ws_gate, ws_up, ws_down, ln_ffn_g, ln_ffn_b):
    B, S, D = x.shape
    h = _mixer(x, ln_in_g, ln_in_b, w_in[0], w_out[0], rel_bias, attn_sinks[0], ln_mix_g[0], ln_mix_b[0])
    out = _moe(h, w_router[0], router_bias[0], w_gate[0], w_up[0], w_down[0],
               ws_gate[0], ws_up[0], ws_down[0], ln_ffn_g[0], ln_ffn_b[0])
    return out.reshape(B, S, D)
```

```python
import functools
import math

import jax
import jax.numpy as jnp
from jax import lax
from jax.experimental import pallas as pl
from jax.experimental.pallas import tpu as pltpu

D_MODEL = 1024
DEPTH = 1
RET_HEADS = 4
RET_QK_DIM = 64
RET_V_DIM = 128
RET_CHUNK = 128
RET_WIDTH = RET_HEADS * RET_V_DIM
ROPE_BASE = 10000.0
SWA_HEADS = 8
SWA_KV_HEADS = 2
SWA_GROUP = SWA_HEADS // SWA_KV_HEADS
SWA_HEAD_DIM = 64
SWA_WINDOW = 128
SWA_WIDTH = SWA_HEADS * SWA_HEAD_DIM
MIX_WIDTH = RET_WIDTH + SWA_WIDTH
RQK = RET_HEADS * RET_QK_DIM
SKV = SWA_KV_HEADS * SWA_HEAD_DIM
IN_SIZES = (RQK, RQK, RET_WIDTH, RET_WIDTH, SWA_WIDTH, SKV, SKV)
IN_OFFS = tuple(sum(IN_SIZES[:i]) for i in range(len(IN_SIZES)))
IN_WIDTH = sum(IN_SIZES)
REL_BUCKETS = 32
REL_MAX_DIST = 128
N_EXPERTS = 256
TOP_K = 8
N_GROUPS = 8
GROUP_SIZE = N_EXPERTS // N_GROUPS
TOPK_GROUPS = 4
EXPERT_DIM = 256
SHARED_DIM = 256
ROUTED_SCALE = 2.5
LN_EPS = 1e-5
GN_EPS = 1e-6
DEEPNORM_ALPHA = (2 * DEPTH) ** 0.25
MASK_VALUE = -1e30

VMEM_LIMIT_BYTES = 56 * 1024 * 1024

MIX_ROWS = 256
ROUTE_ROWS = 256
DISPATCH_ROWS = 256
EXPERT_ROWS = 512
COMBINE_ROWS = 128


def _layer_norm(x, g, b):
    mu = jnp.mean(x, axis=-1, keepdims=True)
    xc = x - mu
    var = jnp.mean(xc * xc, axis=-1, keepdims=True)
    return xc * lax.rsqrt(var + LN_EPS) * g + b


def _dot(a, b):
    return jnp.dot(a, b, preferred_element_type=jnp.float32)


def _dot_nt(a, b):
    return lax.dot_general(a, b, (((1,), (1,)), ((), ())), preferred_element_type=jnp.float32)


def _dot_tn(a, b):
    return lax.dot_general(a, b, (((0,), (0,)), ((), ())), preferred_element_type=jnp.float32)


def _silu(x):
    return x * (1.0 / (1.0 + jnp.exp(-x)))


LANES = 128
ROW_TILE = D_MODEL // LANES


def _load_rows(ref, n_rows, lead=()):
    return jnp.concatenate([ref[lead + (pl.ds(s, n_rows, stride=ROW_TILE), slice(None))]
                            for s in range(ROW_TILE)], axis=1)


def _store_rows(ref, val, lead=()):
    n_rows = val.shape[0]
    for s in range(ROW_TILE):
        ref[lead + (pl.ds(s, n_rows, stride=ROW_TILE), slice(None))] = val[:, s * LANES:(s + 1) * LANES]


def _row_tile(r):
    return pl.ds(pl.multiple_of(r * ROW_TILE, ROW_TILE), ROW_TILE)


def _swap_halves(x):
    n = x.shape[-1]
    half = RET_QK_DIM // 2
    lane = lax.broadcasted_iota(jnp.int32, x.shape, 1)
    from_right = pltpu.roll(x, n - half, axis=1)
    from_left = pltpu.roll(x, half, axis=1)
    return jnp.where((lane % RET_QK_DIM) < half, from_right, from_left)


def _mixer_kernel(rel_bias_ref, x_ref, g_in_ref, b_in_ref, w_in_ref, w_out_ref, rot_ref, decay_ref,
                  zeta_ref, xi_ref, cdecay_ref, bucket_ref, sink_ref, g_mix_ref, b_mix_ref,
                  h2_ref, state_ref, kprev_ref, vprev_ref, bias_ref):
    b_id = pl.program_id(0)
    c_id = pl.program_id(1)
    W = SWA_WINDOW

    @pl.when((b_id == 0) & (c_id == 0))
    def _build_bias():
        bucket = bucket_ref[...]
        for h in range(SWA_HEADS):
            acc = jnp.full((2 * W, W), MASK_VALUE, jnp.float32)
            for b in range(REL_BUCKETS):
                acc = jnp.where(bucket == b, rel_bias_ref[b, h], acc)
            kh, g = divmod(h, SWA_GROUP)
            bias_ref[kh, :, g * W:(g + 1) * W] = acc

    @pl.when(c_id == 0)
    def _reset():
        state_ref[...] = jnp.zeros_like(state_ref)
        kprev_ref[...] = jnp.zeros_like(kprev_ref)
        vprev_ref[...] = jnp.zeros_like(vprev_ref)

    h = _layer_norm(x_ref[...], g_in_ref[...], b_in_ref[...])
    proj = _dot(h.astype(jnp.bfloat16), w_in_ref[...])

    o_q, o_k, o_v, o_g, o_sq, o_sk, o_sv = IN_OFFS
    cos_t = rot_ref[:, :RQK]
    sin_t = rot_ref[:, RQK:]
    q_all = proj[:, o_q:o_q + RQK]
    k_all = proj[:, o_k:o_k + RQK]
    q_rot = q_all * cos_t + _swap_halves(q_all) * sin_t
    k_rot = (k_all * cos_t + _swap_halves(k_all) * sin_t) * (RET_QK_DIM ** -0.5)

    n_sub = x_ref.shape[0] // RET_CHUNK
    states = [state_ref[hh] for hh in range(RET_HEADS)]
    k_prev = kprev_ref[...]
    v_prev = vprev_ref[...]
    cat_rows = []
    for s in range(n_sub):
        r0 = s * RET_CHUNK
        rows = slice(r0, r0 + RET_CHUNK)
        pieces = []
        for hh in range(RET_HEADS):
            qk = slice(hh * RET_QK_DIM, (hh + 1) * RET_QK_DIM)
            vv = slice(o_v + hh * RET_V_DIM, o_v + (hh + 1) * RET_V_DIM)
            gg = slice(o_g + hh * RET_V_DIM, o_g + (hh + 1) * RET_V_DIM)
            q = q_rot[rows, qk].astype(jnp.bfloat16)
            k32 = k_rot[rows, qk]
            v = proj[rows, vv].astype(jnp.bfloat16)
            scores = _dot_nt(q, k32.astype(jnp.bfloat16)) * decay_ref[hh]
            intra = _dot(scores.astype(jnp.bfloat16), v)
            inter = _dot(q, states[hh].astype(jnp.bfloat16)) * xi_ref[hh]
            ret = intra + inter
            kz = (k32 * zeta_ref[hh]).astype(jnp.bfloat16)
            states[hh] = states[hh] * cdecay_ref[hh] + _dot_tn(kz, v)
            mu = jnp.mean(ret, axis=-1, keepdims=True)
            rc = ret - mu
            var = jnp.mean(rc * rc, axis=-1, keepdims=True)
            normed = rc * lax.rsqrt(var + GN_EPS)
            pieces.append((_silu(proj[rows, gg]) * normed).astype(jnp.bfloat16))
        k_cur = proj[rows, o_sk:o_sk + SKV].astype(jnp.bfloat16)
        v_cur = proj[rows, o_sv:o_sv + SKV].astype(jnp.bfloat16)
        for kh in range(SWA_KV_HEADS):
            kv = slice(kh * SWA_HEAD_DIM, (kh + 1) * SWA_HEAD_DIM)
            q4 = jnp.concatenate(
                [proj[rows, o_sq + (kh * SWA_GROUP + g) * SWA_HEAD_DIM:
                      o_sq + (kh * SWA_GROUP + g + 1) * SWA_HEAD_DIM] for g in range(SWA_GROUP)],
                axis=0) * (SWA_HEAD_DIM ** -0.5)
            kcat = jnp.concatenate([k_prev[:, kv], k_cur[:, kv]], axis=0)
            vcat = jnp.concatenate([v_prev[:, kv], v_cur[:, kv]], axis=0)
            logits = _dot_nt(kcat, q4.astype(jnp.bfloat16)) + bias_ref[kh]
            if s == 0:
                key = lax.broadcasted_iota(jnp.int32, logits.shape, 0)
                logits = logits + jnp.where((key < W) & (c_id == 0), MASK_VALUE, 0.0)
            sink = sink_ref[kh]
            m = jnp.maximum(jnp.max(logits, axis=0, keepdims=True), sink)
            p = jnp.exp(logits - m)
            den = jnp.sum(p, axis=0, keepdims=True) + jnp.exp(sink - m)
            probs = (p / den).astype(jnp.bfloat16)
            o4 = _dot_tn(vcat, probs)
            pieces.extend(o4[:, g * W:(g + 1) * W].T.astype(jnp.bfloat16) for g in range(SWA_GROUP))
        k_prev, v_prev = k_cur, v_cur
        cat_rows.append(jnp.concatenate(pieces, axis=1))
    for hh in range(RET_HEADS):
        state_ref[hh] = states[hh]
    kprev_ref[...] = k_prev
    vprev_ref[...] = v_prev

    mix = _dot(jnp.concatenate(cat_rows, axis=0), w_out_ref[...])
    _store_rows(h2_ref, _layer_norm(DEEPNORM_ALPHA * h + mix, g_mix_ref[...], b_mix_ref[...]))


def _t5_bucket(dist):
    n = jnp.maximum(dist, 0)
    max_exact = REL_BUCKETS // 2
    ratio = jnp.log(jnp.maximum(n, 1).astype(jnp.float32) / max_exact) / math.log(REL_MAX_DIST / max_exact)
    large = jnp.minimum(max_exact + (ratio * (REL_BUCKETS - max_exact)).astype(jnp.int32), REL_BUCKETS - 1)
    return jnp.where(n < max_exact, n, large)


def _mixer(x, ln_in_g, ln_in_b, w_in, w_out, rel_bias, sinks, ln_mix_g, ln_mix_b):
    B, S, D = x.shape
    R = MIX_ROWS
    C = RET_CHUNK
    W = SWA_WINDOW
    f32 = jnp.float32
    half = RET_QK_DIM // 2
    inv = ROPE_BASE ** (-jnp.arange(half, dtype=f32) / half)
    ang = jnp.arange(S, dtype=f32)[:, None] * inv[None, :]
    cos, sin = jnp.cos(ang), jnp.sin(ang)
    cos_t = jnp.tile(jnp.concatenate([cos, cos], axis=-1), (1, RET_HEADS))
    sin_t = jnp.tile(jnp.concatenate([-sin, sin], axis=-1), (1, RET_HEADS))
    rot = jnp.concatenate([cos_t, sin_t], axis=-1)
    log_gamma = jnp.log(1.0 - 2.0 ** (-5.0 - jnp.arange(RET_HEADS, dtype=f32)))
    idx = jnp.arange(C, dtype=f32)
    diff = idx[:, None] - idx[None, :]
    decay = jnp.where(diff[None] >= 0, jnp.exp(jnp.maximum(diff, 0.0)[None] * log_gamma[:, None, None]), 0.0)
    zeta = jnp.exp((C - 1.0 - idx)[None, :] * log_gamma[:, None])
    xi = jnp.exp((idx + 1.0)[None, :] * log_gamma[:, None])
    zeta_b = jnp.broadcast_to(zeta[:, :, None], (RET_HEADS, C, RET_QK_DIM))
    xi_b = jnp.broadcast_to(xi[:, :, None], (RET_HEADS, C, RET_V_DIM))
    cdecay = jnp.broadcast_to(jnp.exp(C * log_gamma)[:, None, None], (RET_HEADS, RET_QK_DIM, RET_V_DIM))
    i = jnp.arange(W)
    j = jnp.arange(2 * W)
    dist = i[:, None] + W - j[None, :]
    bucket = jnp.where((dist >= 0) & (dist < W), _t5_bucket(dist), -1).astype(jnp.int32).T
    sink_row = jnp.repeat(sinks.astype(f32), W).reshape(SWA_KV_HEADS, 1, SWA_GROUP * W)

    const = lambda shape: pl.BlockSpec(shape, lambda b, c, *_: (0,) * len(shape))
    grid_spec = pltpu.PrefetchScalarGridSpec(
        num_scalar_prefetch=1,
        grid=(B, S // R),
        in_specs=[
            pl.BlockSpec((None, R, D), lambda b, c, *_: (b, c, 0)),
            const((1, D)), const((1, D)),
            const((D, IN_WIDTH)), const((MIX_WIDTH, D)),
            pl.BlockSpec((R, 2 * RQK), lambda b, c, *_: (c, 0)),
            const((RET_HEADS, C, C)), const((RET_HEADS, C, RET_QK_DIM)), const((RET_HEADS, C, RET_V_DIM)),
            const((RET_HEADS, RET_QK_DIM, RET_V_DIM)),
            const((2 * W, W)), const((SWA_KV_HEADS, 1, SWA_GROUP * W)),
            const((1, D)), const((1, D)),
        ],
        out_specs=pl.BlockSpec((R * ROW_TILE, LANES), lambda b, c, *_: (b * (S // R) + c, 0)),
        scratch_shapes=[
            pltpu.VMEM((RET_HEADS, RET_QK_DIM, RET_V_DIM), f32),
            pltpu.VMEM((W, SKV), jnp.bfloat16),
            pltpu.VMEM((W, SKV), jnp.bfloat16),
            pltpu.VMEM((SWA_KV_HEADS, 2 * W, SWA_GROUP * W), f32),
        ],
    )
    return pl.pallas_call(
        _mixer_kernel,
        grid_spec=grid_spec,
        out_shape=jax.ShapeDtypeStruct((B * S * ROW_TILE, LANES), f32),
        compiler_params=pltpu.CompilerParams(
            dimension_semantics=("arbitrary", "arbitrary"), vmem_limit_bytes=VMEM_LIMIT_BYTES),
    )(rel_bias.astype(f32), x, ln_in_g.reshape(1, D), ln_in_b.reshape(1, D),
      w_in.astype(jnp.bfloat16), w_out.astype(jnp.bfloat16), rot, decay, zeta_b, xi_b, cdecay,
      bucket, sink_row, ln_mix_g.reshape(1, D), ln_mix_b.reshape(1, D))


def _router_kernel(h_ref, wr_ref, rb_ref, e_ref, w_ref, rk_ref, cnt_ref, run_ref):
    f32 = jnp.float32
    R = h_ref.shape[0] // ROW_TILE
    E = N_EXPERTS
    neg = -jnp.inf

    @pl.when(pl.program_id(0) == 0)
    def _init():
        run_ref[...] = jnp.zeros_like(run_ref)

    logits = lax.dot_general(wr_ref[...], _load_rows(h_ref, R), (((1,), (1,)), ((), ())),
                             precision=lax.Precision.HIGHEST, preferred_element_type=f32)
    scores = 1.0 / (1.0 + jnp.exp(-logits))
    choice = scores + rb_ref[...]
    eid = lax.broadcasted_iota(jnp.int32, (E, R), 0)

    def first_argmax(vals, ids, none):
        m = jnp.max(vals, axis=0, keepdims=True)
        idx = jnp.min(jnp.where(vals == m, ids, none), axis=0, keepdims=True)
        return m, idx

    gid = lax.broadcasted_iota(jnp.int32, (GROUP_SIZE, R), 0)
    groups, gscore = [], []
    for g in range(N_GROUPS):
        vals = choice[g * GROUP_SIZE:(g + 1) * GROUP_SIZE]
        m1, i1 = first_argmax(vals, gid, GROUP_SIZE)
        m2 = jnp.max(jnp.where(gid == i1, neg, vals), axis=0, keepdims=True)
        groups.append(vals)
        gscore.append(m1 + m2)
    kept = []
    for g in range(N_GROUPS):
        beaten = jnp.zeros((1, R), f32)
        for g2 in range(N_GROUPS):
            if g2 == g:
                continue
            ahead = (gscore[g2] > gscore[g]) | (gscore[g2] == gscore[g]) if g2 < g else gscore[g2] > gscore[g]
            beaten = beaten + jnp.where(ahead, 1.0, 0.0)
        kept.append(jnp.where(beaten < TOPK_GROUPS, groups[g], neg))
    masked = jnp.concatenate(kept, axis=0)

    idxs, wts = [], []
    picked = jnp.zeros((E, R), f32)
    for _ in range(TOP_K):
        _, idx = first_argmax(masked, eid, E)
        hit = eid == idx
        idxs.append(idx)
        wts.append(jnp.sum(jnp.where(hit, scores, 0.0), axis=0, keepdims=True))
        masked = jnp.where(hit, neg, masked)
        picked = jnp.where(hit, 1.0, picked)
    wsum = wts[0]
    for k in range(1, TOP_K):
        wsum = wsum + wts[k]

    row = lax.broadcasted_iota(jnp.int32, (R, R), 0)
    col = lax.broadcasted_iota(jnp.int32, (R, R), 1)
    earlier = jnp.where(row < col, 1.0, 0.0).astype(jnp.bfloat16)
    picked_bf = picked.astype(jnp.bfloat16)
    run = run_ref[...]
    before = _dot(picked_bf, earlier) + jnp.concatenate([run] * (R // LANES), axis=1)
    sub_k = lax.broadcasted_iota(jnp.int32, (TOP_K, R), 0)
    e_out = jnp.zeros((TOP_K, R), jnp.int32)
    w_out = jnp.zeros((TOP_K, R), f32)
    rk_out = jnp.zeros((TOP_K, R), jnp.int32)
    for k in range(TOP_K):
        rank_k = jnp.sum(jnp.where(eid == idxs[k], before, 0.0), axis=0, keepdims=True)
        e_out = jnp.where(sub_k == k, idxs[k], e_out)
        w_out = jnp.where(sub_k == k, wts[k] / wsum * ROUTED_SCALE, w_out)
        rk_out = jnp.where(sub_k == k, rank_k.astype(jnp.int32), rk_out)
    e_ref[...] = e_out
    w_ref[...] = w_out
    rk_ref[...] = rk_out
    run_ref[...] = run + _dot(picked_bf, jnp.ones((R, LANES), jnp.bfloat16))
    cnt_ref[...] = run_ref[...]


def _router(h2, w_router, router_bias):
    T, D = h2.shape[0] // ROW_TILE, D_MODEL
    R = ROUTE_ROWS
    E = N_EXPERTS
    return pl.pallas_call(
        _router_kernel,
        grid=(T // R,),
        in_specs=[
            pl.BlockSpec((R * ROW_TILE, LANES), lambda i: (i, 0)),
            pl.BlockSpec((E, D), lambda i: (0, 0)),
            pl.BlockSpec((E, R), lambda i: (0, 0)),
        ],
        out_specs=[
            pl.BlockSpec((TOP_K, R), lambda i: (0, i)),
            pl.BlockSpec((TOP_K, R), lambda i: (0, i)),
            pl.BlockSpec((TOP_K, R), lambda i: (0, i)),
            pl.BlockSpec((E, LANES), lambda i: (0, 0)),
        ],
        out_shape=[
            jax.ShapeDtypeStruct((TOP_K, T), jnp.int32),
            jax.ShapeDtypeStruct((TOP_K, T), jnp.float32),
            jax.ShapeDtypeStruct((TOP_K, T), jnp.int32),
            jax.ShapeDtypeStruct((E, LANES), jnp.float32),
        ],
        scratch_shapes=[pltpu.VMEM((E, LANES), jnp.float32)],
        compiler_params=pltpu.CompilerParams(
            dimension_semantics=("arbitrary",), vmem_limit_bytes=VMEM_LIMIT_BYTES),
    )(h2, w_router.T, jnp.broadcast_to(router_bias.astype(jnp.float32)[:, None], (E, R)))


def _positions_kernel(row_start_ref, e_ref, rk_ref, pos_ref):
    e = e_ref[...]
    rk = rk_ref[...]

    def per_expert(i, pos):
        return jnp.where(e == i, rk + row_start_ref[i], pos)

    pos_ref[...] = lax.fori_loop(0, N_EXPERTS, per_expert, jnp.zeros_like(rk))


def _positions(e_idx, rank, row_start):
    n = e_idx.size
    shape = (n // LANES, LANES)
    grid_spec = pltpu.PrefetchScalarGridSpec(
        num_scalar_prefetch=1,
        grid=(1,),
        in_specs=[pl.BlockSpec(shape, lambda i, *_: (0, 0)), pl.BlockSpec(shape, lambda i, *_: (0, 0))],
        out_specs=pl.BlockSpec(shape, lambda i, *_: (0, 0)),
    )
    pos = pl.pallas_call(
        _positions_kernel,
        grid_spec=grid_spec,
        out_shape=jax.ShapeDtypeStruct(shape, jnp.int32),
    )(row_start, e_idx.reshape(shape), rank.reshape(shape))
    return pos.reshape(n)


def _dispatch_kernel(row_start_ref, cnt_ref, n_act_ref, pos_ref, h_ref, xs_ref, zero_ref, sem, zsem,
                     *, n_tokens):
    R = h_ref.shape[0] // ROW_TILE
    BM = EXPERT_ROWS
    n_blocks = xs_ref.shape[0] // (BM * ROW_TILE)
    n_pad_units = n_blocks - n_tokens * TOP_K // BM

    @pl.when(pl.program_id(0) == 0)
    def _zero_padding():
        zero_ref[...] = jnp.zeros_like(zero_ref)

        def expert_tail(e, carry):
            n_tail = pl.multiple_of(((BM - cnt_ref[e] % BM) % BM) * ROW_TILE, ROW_TILE)

            @pl.when(n_tail > 0)
            def _():
                dst = pl.multiple_of((row_start_ref[e] + cnt_ref[e]) * ROW_TILE, ROW_TILE)
                pltpu.make_async_copy(zero_ref.at[pl.ds(0, n_tail)], xs_ref.at[pl.ds(dst, n_tail)], zsem).start()
            return carry

        lax.fori_loop(0, N_EXPERTS, expert_tail, 0)

        def idle_block(i, carry):
            dst = pl.multiple_of(i * (BM * ROW_TILE), BM * ROW_TILE)
            pltpu.make_async_copy(zero_ref, xs_ref.at[pl.ds(dst, BM * ROW_TILE)], zsem).start()
            return carry

        lax.fori_loop(n_act_ref[0], n_blocks, idle_block, 0)

        def drain(i, carry):
            pltpu.make_async_copy(zero_ref, xs_ref.at[pl.ds(0, BM * ROW_TILE)], zsem).wait()
            return carry

        lax.fori_loop(0, n_pad_units, drain, 0)

    def issue(t, carry):
        src = h_ref.at[_row_tile(t)]
        for k in range(TOP_K):
            dest = pos_ref[t * TOP_K + k]
            pltpu.make_async_copy(src, xs_ref.at[_row_tile(dest)], sem).start(priority=k % 2)
        return carry

    lax.fori_loop(0, R, issue, 0)
    n = R * TOP_K * ROW_TILE
    pltpu.make_async_copy(xs_ref.at[pl.ds(0, n)], xs_ref.at[pl.ds(0, n)], sem).wait()


def _dispatch(h2, pos, row_start, cnt, n_act, n_rows):
    T = h2.shape[0] // ROW_TILE
    R = DISPATCH_ROWS
    grid_spec = pltpu.PrefetchScalarGridSpec(
        num_scalar_prefetch=3,
        grid=(T // R,),
        in_specs=[
            pl.BlockSpec((R * TOP_K,), lambda i, *_: (i,), memory_space=pltpu.SMEM),
            pl.BlockSpec((R * ROW_TILE, LANES), lambda i, *_: (i, 0)),
        ],
        out_specs=pl.BlockSpec(memory_space=pl.ANY),
        scratch_shapes=[pltpu.VMEM((EXPERT_ROWS * ROW_TILE, LANES), jnp.float32),
                        pltpu.SemaphoreType.DMA(()), pltpu.SemaphoreType.DMA(())],
    )
    return pl.pallas_call(
        functools.partial(_dispatch_kernel, n_tokens=T),
        grid_spec=grid_spec,
        out_shape=jax.ShapeDtypeStruct((n_rows * ROW_TILE, LANES), jnp.float32),
        compiler_params=pltpu.CompilerParams(
            dimension_semantics=("arbitrary",), vmem_limit_bytes=VMEM_LIMIT_BYTES),
    )(row_start, cnt, n_act, pos, h2)


X_SLOTS = 4
Y_SLOTS = 3


def _experts_kernel(blk_e_ref, first_ref, slot_ref, next_e_ref, valid_ref, n_act_ref, xs_hbm, wg_hbm, wu_hbm,
                    wd_hbm, ys_hbm, x_buf, y_buf, wg_buf, wu_buf, wd_buf, wg_bf, wu_bf, wd_bf, sems, x_sems,
                    y_sems):
    i = pl.program_id(0)
    n_act = n_act_ref[0]
    bf16 = jnp.bfloat16
    blk = EXPERT_ROWS * ROW_TILE

    def hbm_rows(j):
        return pl.ds(pl.multiple_of(j * blk, blk), pl.multiple_of(valid_ref[j] * ROW_TILE, ROW_TILE))

    def vmem_rows(j):
        return pl.ds(0, pl.multiple_of(valid_ref[j] * ROW_TILE, ROW_TILE))

    def x_copy(j):
        return pltpu.make_async_copy(xs_hbm.at[hbm_rows(j)], x_buf.at[j % X_SLOTS, vmem_rows(j)],
                                     x_sems.at[j % X_SLOTS])

    def y_copy(j):
        return pltpu.make_async_copy(y_buf.at[j % Y_SLOTS, vmem_rows(j)], ys_hbm.at[hbm_rows(j)],
                                     y_sems.at[j % Y_SLOTS])

    @pl.when(i == 0)
    def _prime():
        x_buf[...] = jnp.zeros_like(x_buf)
        for j in range(X_SLOTS - 1):
            @pl.when(j < n_act)
            def _():
                x_copy(j).start()

    @pl.when(i + (X_SLOTS - 1) < n_act)
    def _prefetch():
        x_copy(i + (X_SLOTS - 1)).start()

    def weight_copies(e, slot):
        return (pltpu.make_async_copy(wg_hbm.at[e], wg_buf.at[slot], sems.at[slot]),
                pltpu.make_async_copy(wu_hbm.at[e], wu_buf.at[slot], sems.at[slot]),
                pltpu.make_async_copy(wd_hbm.at[e], wd_buf.at[slot], sems.at[slot]))

    @pl.when((i < n_act_ref[0]) & (first_ref[i] == 1))
    def _new_expert():
        slot = slot_ref[i]

        @pl.when(i == 0)
        def _():
            for c in weight_copies(blk_e_ref[0], 0):
                c.start()

        for c in weight_copies(blk_e_ref[i], slot):
            c.wait()

        @pl.when(next_e_ref[i] >= 0)
        def _():
            for c in weight_copies(next_e_ref[i], 1 - slot):
                c.start()

        wg_bf[...] = wg_buf[slot].astype(bf16)
        wu_bf[...] = wu_buf[slot].astype(bf16)
        wd_bf[...] = wd_buf[slot].astype(bf16)

    @pl.when(i < n_act)
    def _compute():
        x_copy(i).wait()
        x = _load_rows(x_buf, EXPERT_ROWS, lead=(i % X_SLOTS,)).astype(bf16)
        g = _dot(x, wg_bf[...])
        u = _dot(x, wu_bf[...])
        a = (_silu(g) * u).astype(bf16)
        y = _dot(a, wd_bf[...])

        @pl.when(i >= Y_SLOTS)
        def _():
            y_copy(i - Y_SLOTS).wait()

        _store_rows(y_buf, y, lead=(i % Y_SLOTS,))
        y_copy(i).start()

    @pl.when(i == n_act - 1)
    def _drain():
        for d in range(Y_SLOTS):
            @pl.when(i - d >= 0)
            def _():
                y_copy(i - d).wait()


def _experts(xs, blk_e, n_act, row_start, cnt, w_gate, w_up, w_down):
    D = D_MODEL
    BM = EXPERT_ROWS
    F = EXPERT_DIM
    n_blocks = xs.shape[0] // (BM * ROW_TILE)
    blk_in_expert = jnp.arange(n_blocks, dtype=jnp.int32) - row_start[blk_e] // BM
    valid = jnp.clip(cnt[blk_e] - blk_in_expert * BM, 0, BM).astype(jnp.int32)
    ids = jnp.arange(n_blocks, dtype=jnp.int32)
    active = ids < n_act[0]
    first = active & ((ids == 0) | (blk_e != jnp.roll(blk_e, 1)))
    slot = ((jnp.cumsum(first.astype(jnp.int32)) - 1) % 2).astype(jnp.int32)
    first_pos = jnp.where(first, ids, n_blocks)
    later_first = lax.cummin(jnp.concatenate([first_pos[1:], jnp.full((1,), n_blocks, jnp.int32)]), reverse=True)
    next_e = jnp.where(later_first < n_blocks, blk_e[jnp.minimum(later_first, n_blocks - 1)], -1).astype(jnp.int32)

    grid_spec = pltpu.PrefetchScalarGridSpec(
        num_scalar_prefetch=6,
        grid=(n_blocks,),
        in_specs=[pl.BlockSpec(memory_space=pl.ANY)] * 4,
        out_specs=pl.BlockSpec(memory_space=pl.ANY),
        scratch_shapes=[
            pltpu.VMEM((X_SLOTS, BM * ROW_TILE, LANES), jnp.float32),
            pltpu.VMEM((Y_SLOTS, BM * ROW_TILE, LANES), jnp.float32),
            pltpu.VMEM((2, D, F), jnp.float32), pltpu.VMEM((2, D, F), jnp.float32),
            pltpu.VMEM((2, F, D), jnp.float32),
            pltpu.VMEM((D, F), jnp.bfloat16), pltpu.VMEM((D, F), jnp.bfloat16),
            pltpu.VMEM((F, D), jnp.bfloat16),
            pltpu.SemaphoreType.DMA((2,)), pltpu.SemaphoreType.DMA((X_SLOTS,)),
            pltpu.SemaphoreType.DMA((Y_SLOTS,)),
        ],
    )
    return pl.pallas_call(
        _experts_kernel,
        grid_spec=grid_spec,
        out_shape=jax.ShapeDtypeStruct(xs.shape, jnp.float32),
        input_output_aliases={6: 0},
        compiler_params=pltpu.CompilerParams(
            dimension_semantics=("arbitrary",), vmem_limit_bytes=VMEM_LIMIT_BYTES),
    )(blk_e, first.astype(jnp.int32), slot, next_e, valid, n_act, xs, w_gate, w_up, w_down)


def _combine_kernel(pos_ref, pos_next_ref, h_ref, w_ref, ys_ref, wsg_ref, wsu_ref, wsd_ref,
                    g_ref, b_ref, out_ref, buf_ref, sems):
    R = h_ref.shape[0] // ROW_TILE
    i = pl.program_id(0)
    slot = i % 2

    def gather(p_ref, s):
        def issue(t, carry):
            for k in range(TOP_K):
                src = p_ref[t * TOP_K + k]
                pltpu.make_async_copy(ys_ref.at[_row_tile(src)], buf_ref.at[s, k, _row_tile(t)],
                                      sems.at[s]).start(priority=k % 2)
            return carry

        lax.fori_loop(0, R, issue, 0)

    @pl.when(i == 0)
    def _():
        gather(pos_ref, 0)

    @pl.when(i + 1 < pl.num_programs(0))
    def _():
        gather(pos_next_ref, 1 - slot)

    h = _load_rows(h_ref, R)
    hb = h.astype(jnp.bfloat16)
    act = (_silu(_dot(hb, wsg_ref[...])) * _dot(hb, wsu_ref[...])).astype(jnp.bfloat16)
    ffn = _dot(act, wsd_ref[...])
    pltpu.make_async_copy(buf_ref.at[slot], buf_ref.at[slot], sems.at[slot]).wait()
    w = w_ref[...]
    for k in range(TOP_K):
        ffn = ffn + _load_rows(buf_ref, R, lead=(slot, k)) * w[:, k:k + 1]
    out_ref[...] = _layer_norm(DEEPNORM_ALPHA * h + ffn, g_ref[...], b_ref[...])


def _combine(h2, pos, top_w, ys, ws_gate, ws_up, ws_down, ln_g, ln_b):
    T, D = h2.shape[0] // ROW_TILE, D_MODEL
    R = COMBINE_ROWS
    F = SHARED_DIM
    bf16 = jnp.bfloat16
    const = lambda shape: pl.BlockSpec(shape, lambda i: (0,) * len(shape))
    return pl.pallas_call(
        _combine_kernel,
        grid=(T // R,),
        in_specs=[
            pl.BlockSpec((R * TOP_K,), lambda i: (i,), memory_space=pltpu.SMEM),
            pl.BlockSpec((R * TOP_K,), lambda i: (jnp.minimum(i + 1, T // R - 1),), memory_space=pltpu.SMEM),
            pl.BlockSpec((R * ROW_TILE, LANES), lambda i: (i, 0)),
            pl.BlockSpec((R, TOP_K), lambda i: (i, 0)),
            pl.BlockSpec(memory_space=pl.ANY),
            const((D, F)), const((D, F)), const((F, D)), const((1, D)), const((1, D)),
        ],
        out_specs=pl.BlockSpec((R, D), lambda i: (i, 0)),
        scratch_shapes=[pltpu.VMEM((2, TOP_K, R * ROW_TILE, LANES), jnp.float32), pltpu.SemaphoreType.DMA((2,))],
        out_shape=jax.ShapeDtypeStruct((T, D), jnp.float32),
        compiler_params=pltpu.CompilerParams(
            dimension_semantics=("arbitrary",), vmem_limit_bytes=VMEM_LIMIT_BYTES),
    )(pos, pos, h2, top_w, ys, ws_gate.astype(bf16), ws_up.astype(bf16),
      ws_down.astype(bf16), ln_g.reshape(1, D), ln_b.reshape(1, D))


def _moe(h2, w_router, router_bias, w_gate, w_up, w_down, ws_gate, ws_up, ws_down, ln_g, ln_b):
    T = h2.shape[0] // ROW_TILE
    E = N_EXPERTS
    BM = EXPERT_ROWS
    e_idx, top_w, rank, counts = _router(h2, w_router, router_bias)
    cnt = counts[:, 0].astype(jnp.int32)
    nblk = (cnt + BM - 1) // BM
    blk_end = jnp.cumsum(nblk)
    row_start = ((blk_end - nblk) * BM).astype(jnp.int32)
    n_blocks = T * TOP_K // BM + E
    n_act = blk_end[-1:].astype(jnp.int32)
    blk_ids = jnp.minimum(jnp.arange(n_blocks, dtype=jnp.int32), n_act[0] - 1)
    blk_e = jnp.minimum(jnp.sum(blk_end[None, :] <= blk_ids[:, None], axis=1), E - 1).astype(jnp.int32)
    pos = _positions(e_idx, rank, row_start).reshape(TOP_K, T).T.reshape(T * TOP_K)
    xs = _dispatch(h2, pos, row_start, cnt, n_act, n_blocks * BM)
    ys = _experts(xs, blk_e, n_act, row_start, cnt, w_gate, w_up, w_down)
    return _combine(h2, pos, top_w.T, ys, ws_gate, ws_up, ws_down, ln_g, ln_b)


def kernel(x, ln_in_g, ln_in_b, w_in, w_out, rel_bias, attn_sinks, ln_mix_g, ln_mix_b, w_router,
           router_bias, w_gate, w_up, w_down, ws_gate, ws_up, ws_down, ln_ffn_g, ln_ffn_b):
    B, S, D = x.shape
    h = _mixer(x, ln_in_g, ln_in_b, w_in[0], w_out[0], rel_bias, attn_sinks[0], ln_mix_g[0], ln_mix_b[0])
    out = _moe(h, w_router[0], router_bias[0], w_gate[0], w_up[0], w_down[0],
               ws_gate[0], ws_up[0], ws_down[0], ln_ffn_g[0], ln_ffn_b[0])
    return out.reshape(B, S, D)
```

```python
import functools
import math

import jax
import jax.numpy as jnp
from jax import lax
from jax.experimental import pallas as pl
from jax.experimental.pallas import tpu as pltpu
from jax.experimental.pallas import tpu_sc as plsc

D_MODEL = 1024
DEPTH = 1
RET_HEADS = 4
RET_QK_DIM = 64
RET_V_DIM = 128
RET_CHUNK = 128
RET_WIDTH = RET_HEADS * RET_V_DIM
ROPE_BASE = 10000.0
SWA_HEADS = 8
SWA_KV_HEADS = 2
SWA_GROUP = SWA_HEADS // SWA_KV_HEADS
SWA_HEAD_DIM = 64
SWA_WINDOW = 128
SWA_WIDTH = SWA_HEADS * SWA_HEAD_DIM
MIX_WIDTH = RET_WIDTH + SWA_WIDTH
RQK = RET_HEADS * RET_QK_DIM
SKV = SWA_KV_HEADS * SWA_HEAD_DIM
IN_SIZES = (RQK, RQK, RET_WIDTH, RET_WIDTH, SWA_WIDTH, SKV, SKV)
IN_OFFS = tuple(sum(IN_SIZES[:i]) for i in range(len(IN_SIZES)))
IN_WIDTH = sum(IN_SIZES)
REL_BUCKETS = 32
REL_MAX_DIST = 128
N_EXPERTS = 256
TOP_K = 8
N_GROUPS = 8
GROUP_SIZE = N_EXPERTS // N_GROUPS
TOPK_GROUPS = 4
EXPERT_DIM = 256
SHARED_DIM = 256
ROUTED_SCALE = 2.5
LN_EPS = 1e-5
GN_EPS = 1e-6
DEEPNORM_ALPHA = (2 * DEPTH) ** 0.25
MASK_VALUE = -1e30

VMEM_LIMIT_BYTES = 56 * 1024 * 1024

MIX_ROWS = 256
ROUTE_ROWS = 256
DISPATCH_ROWS = 256
EXPERT_ROWS = 256
COMBINE_ROWS = 128


def _layer_norm(x, g, b):
    mu = jnp.mean(x, axis=-1, keepdims=True)
    xc = x - mu
    var = jnp.mean(xc * xc, axis=-1, keepdims=True)
    return xc * lax.rsqrt(var + LN_EPS) * g + b


def _dot(a, b):
    return jnp.dot(a, b, preferred_element_type=jnp.float32)


def _dot_nt(a, b):
    return lax.dot_general(a, b, (((1,), (1,)), ((), ())), preferred_element_type=jnp.float32)


def _dot_tn(a, b):
    return lax.dot_general(a, b, (((0,), (0,)), ((), ())), preferred_element_type=jnp.float32)


def _silu(x):
    return x * (1.0 / (1.0 + jnp.exp(-x)))


LANES = 128
ROW_TILE = D_MODEL // LANES


def _load_rows(ref, n_rows, lead=()):
    return jnp.concatenate([ref[lead + (pl.ds(s, n_rows, stride=ROW_TILE), slice(None))]
                            for s in range(ROW_TILE)], axis=1)


def _store_rows(ref, val, lead=()):
    n_rows = val.shape[0]
    for s in range(ROW_TILE):
        ref[lead + (pl.ds(s, n_rows, stride=ROW_TILE), slice(None))] = val[:, s * LANES:(s + 1) * LANES]


def _row_tile(r):
    return pl.ds(pl.multiple_of(r * ROW_TILE, ROW_TILE), ROW_TILE)


def _swap_halves(x):
    n = x.shape[-1]
    half = RET_QK_DIM // 2
    lane = lax.broadcasted_iota(jnp.int32, x.shape, 1)
    from_right = pltpu.roll(x, n - half, axis=1)
    from_left = pltpu.roll(x, half, axis=1)
    return jnp.where((lane % RET_QK_DIM) < half, from_right, from_left)


def _mixer_kernel(rel_bias_ref, x_ref, g_in_ref, b_in_ref, w_in_ref, w_out_ref, rot_ref, decay_ref,
                  zeta_ref, xi_ref, cdecay_ref, bucket_ref, sink_ref, g_mix_ref, b_mix_ref,
                  h2_ref, state_ref, kprev_ref, vprev_ref, bias_ref):
    b_id = pl.program_id(0)
    c_id = pl.program_id(1)
    W = SWA_WINDOW

    @pl.when((b_id == 0) & (c_id == 0))
    def _build_bias():
        bucket = bucket_ref[...]
        for h in range(SWA_HEADS):
            acc = jnp.full((2 * W, W), MASK_VALUE, jnp.float32)
            for b in range(REL_BUCKETS):
                acc = jnp.where(bucket == b, rel_bias_ref[b, h], acc)
            kh, g = divmod(h, SWA_GROUP)
            bias_ref[kh, :, g * W:(g + 1) * W] = acc

    @pl.when(c_id == 0)
    def _reset():
        state_ref[...] = jnp.zeros_like(state_ref)
        kprev_ref[...] = jnp.zeros_like(kprev_ref)
        vprev_ref[...] = jnp.zeros_like(vprev_ref)

    h = _layer_norm(x_ref[...], g_in_ref[...], b_in_ref[...])
    proj = _dot(h.astype(jnp.bfloat16), w_in_ref[...])

    o_q, o_k, o_v, o_g, o_sq, o_sk, o_sv = IN_OFFS
    cos_t = rot_ref[:, :RQK]
    sin_t = rot_ref[:, RQK:]
    q_all = proj[:, o_q:o_q + RQK]
    k_all = proj[:, o_k:o_k + RQK]
    q_rot = q_all * cos_t + _swap_halves(q_all) * sin_t
    k_rot = (k_all * cos_t + _swap_halves(k_all) * sin_t) * (RET_QK_DIM ** -0.5)

    n_sub = x_ref.shape[0] // RET_CHUNK
    states = [state_ref[hh] for hh in range(RET_HEADS)]
    k_prev = kprev_ref[...]
    v_prev = vprev_ref[...]
    cat_rows = []
    for s in range(n_sub):
        r0 = s * RET_CHUNK
        rows = slice(r0, r0 + RET_CHUNK)
        pieces = []
        for hh in range(RET_HEADS):
            qk = slice(hh * RET_QK_DIM, (hh + 1) * RET_QK_DIM)
            vv = slice(o_v + hh * RET_V_DIM, o_v + (hh + 1) * RET_V_DIM)
            gg = slice(o_g + hh * RET_V_DIM, o_g + (hh + 1) * RET_V_DIM)
            q = q_rot[rows, qk].astype(jnp.bfloat16)
            k32 = k_rot[rows, qk]
            v = proj[rows, vv].astype(jnp.bfloat16)
            scores = _dot_nt(q, k32.astype(jnp.bfloat16)) * decay_ref[hh]
            intra = _dot(scores.astype(jnp.bfloat16), v)
            inter = _dot(q, states[hh].astype(jnp.bfloat16)) * xi_ref[hh]
            ret = intra + inter
            kz = (k32 * zeta_ref[hh]).astype(jnp.bfloat16)
            states[hh] = states[hh] * cdecay_ref[hh] + _dot_tn(kz, v)
            mu = jnp.mean(ret, axis=-1, keepdims=True)
            rc = ret - mu
            var = jnp.mean(rc * rc, axis=-1, keepdims=True)
            normed = rc * lax.rsqrt(var + GN_EPS)
            pieces.append((_silu(proj[rows, gg]) * normed).astype(jnp.bfloat16))
        k_cur = proj[rows, o_sk:o_sk + SKV].astype(jnp.bfloat16)
        v_cur = proj[rows, o_sv:o_sv + SKV].astype(jnp.bfloat16)
        for kh in range(SWA_KV_HEADS):
            kv = slice(kh * SWA_HEAD_DIM, (kh + 1) * SWA_HEAD_DIM)
            q4 = jnp.concatenate(
                [proj[rows, o_sq + (kh * SWA_GROUP + g) * SWA_HEAD_DIM:
                      o_sq + (kh * SWA_GROUP + g + 1) * SWA_HEAD_DIM] for g in range(SWA_GROUP)],
                axis=0) * (SWA_HEAD_DIM ** -0.5)
            kcat = jnp.concatenate([k_prev[:, kv], k_cur[:, kv]], axis=0)
            vcat = jnp.concatenate([v_prev[:, kv], v_cur[:, kv]], axis=0)
            logits = _dot_nt(kcat, q4.astype(jnp.bfloat16)) + bias_ref[kh]
            if s == 0:
                key = lax.broadcasted_iota(jnp.int32, logits.shape, 0)
                logits = logits + jnp.where((key < W) & (c_id == 0), MASK_VALUE, 0.0)
            sink = sink_ref[kh]
            m = jnp.maximum(jnp.max(logits, axis=0, keepdims=True), sink)
            p = jnp.exp(logits - m)
            den = jnp.sum(p, axis=0, keepdims=True) + jnp.exp(sink - m)
            probs = (p / den).astype(jnp.bfloat16)
            o4 = _dot_tn(vcat, probs)
            pieces.extend(o4[:, g * W:(g + 1) * W].T.astype(jnp.bfloat16) for g in range(SWA_GROUP))
        k_prev, v_prev = k_cur, v_cur
        cat_rows.append(jnp.concatenate(pieces, axis=1))
    for hh in range(RET_HEADS):
        state_ref[hh] = states[hh]
    kprev_ref[...] = k_prev
    vprev_ref[...] = v_prev

    mix = _dot(jnp.concatenate(cat_rows, axis=0), w_out_ref[...])
    h2_ref[...] = _layer_norm(DEEPNORM_ALPHA * h + mix, g_mix_ref[...], b_mix_ref[...])


def _t5_bucket(dist):
    n = jnp.maximum(dist, 0)
    max_exact = REL_BUCKETS // 2
    ratio = jnp.log(jnp.maximum(n, 1).astype(jnp.float32) / max_exact) / math.log(REL_MAX_DIST / max_exact)
    large = jnp.minimum(max_exact + (ratio * (REL_BUCKETS - max_exact)).astype(jnp.int32), REL_BUCKETS - 1)
    return jnp.where(n < max_exact, n, large)


def _mixer(x, ln_in_g, ln_in_b, w_in, w_out, rel_bias, sinks, ln_mix_g, ln_mix_b):
    B, S, D = x.shape
    R = MIX_ROWS
    C = RET_CHUNK
    W = SWA_WINDOW
    f32 = jnp.float32
    half = RET_QK_DIM // 2
    inv = ROPE_BASE ** (-jnp.arange(half, dtype=f32) / half)
    ang = jnp.arange(S, dtype=f32)[:, None] * inv[None, :]
    cos, sin = jnp.cos(ang), jnp.sin(ang)
    cos_t = jnp.tile(jnp.concatenate([cos, cos], axis=-1), (1, RET_HEADS))
    sin_t = jnp.tile(jnp.concatenate([-sin, sin], axis=-1), (1, RET_HEADS))
    rot = jnp.concatenate([cos_t, sin_t], axis=-1)
    log_gamma = jnp.log(1.0 - 2.0 ** (-5.0 - jnp.arange(RET_HEADS, dtype=f32)))
    idx = jnp.arange(C, dtype=f32)
    diff = idx[:, None] - idx[None, :]
    decay = jnp.where(diff[None] >= 0, jnp.exp(jnp.maximum(diff, 0.0)[None] * log_gamma[:, None, None]), 0.0)
    zeta = jnp.exp((C - 1.0 - idx)[None, :] * log_gamma[:, None])
    xi = jnp.exp((idx + 1.0)[None, :] * log_gamma[:, None])
    zeta_b = jnp.broadcast_to(zeta[:, :, None], (RET_HEADS, C, RET_QK_DIM))
    xi_b = jnp.broadcast_to(xi[:, :, None], (RET_HEADS, C, RET_V_DIM))
    cdecay = jnp.broadcast_to(jnp.exp(C * log_gamma)[:, None, None], (RET_HEADS, RET_QK_DIM, RET_V_DIM))
    i = jnp.arange(W)
    j = jnp.arange(2 * W)
    dist = i[:, None] + W - j[None, :]
    bucket = jnp.where((dist >= 0) & (dist < W), _t5_bucket(dist), -1).astype(jnp.int32).T
    sink_row = jnp.repeat(sinks.astype(f32), W).reshape(SWA_KV_HEADS, 1, SWA_GROUP * W)

    const = lambda shape: pl.BlockSpec(shape, lambda b, c, *_: (0,) * len(shape))
    grid_spec = pltpu.PrefetchScalarGridSpec(
        num_scalar_prefetch=1,
        grid=(B, S // R),
        in_specs=[
            pl.BlockSpec((None, R, D), lambda b, c, *_: (b, c, 0)),
            const((1, D)), const((1, D)),
            const((D, IN_WIDTH)), const((MIX_WIDTH, D)),
            pl.BlockSpec((R, 2 * RQK), lambda b, c, *_: (c, 0)),
            const((RET_HEADS, C, C)), const((RET_HEADS, C, RET_QK_DIM)), const((RET_HEADS, C, RET_V_DIM)),
            const((RET_HEADS, RET_QK_DIM, RET_V_DIM)),
            const((2 * W, W)), const((SWA_KV_HEADS, 1, SWA_GROUP * W)),
            const((1, D)), const((1, D)),
        ],
        out_specs=pl.BlockSpec((R, D), lambda b, c, *_: (b * (S // R) + c, 0)),
        scratch_shapes=[
            pltpu.VMEM((RET_HEADS, RET_QK_DIM, RET_V_DIM), f32),
            pltpu.VMEM((W, SKV), jnp.bfloat16),
            pltpu.VMEM((W, SKV), jnp.bfloat16),
            pltpu.VMEM((SWA_KV_HEADS, 2 * W, SWA_GROUP * W), f32),
        ],
    )
    return pl.pallas_call(
        _mixer_kernel,
        grid_spec=grid_spec,
        out_shape=jax.ShapeDtypeStruct((B * S, D), f32),
        compiler_params=pltpu.CompilerParams(
            dimension_semantics=("arbitrary", "arbitrary"), vmem_limit_bytes=VMEM_LIMIT_BYTES),
    )(rel_bias.astype(f32), x, ln_in_g.reshape(1, D), ln_in_b.reshape(1, D),
      w_in.astype(jnp.bfloat16), w_out.astype(jnp.bfloat16), rot, decay, zeta_b, xi_b, cdecay,
      bucket, sink_row, ln_mix_g.reshape(1, D), ln_mix_b.reshape(1, D))


def _router_kernel(h_ref, wr_ref, rb_ref, e_ref, w_ref, rk_ref, cnt_ref, run_ref):
    f32 = jnp.float32
    R = h_ref.shape[0]
    E = N_EXPERTS
    neg = -jnp.inf

    @pl.when(pl.program_id(0) == 0)
    def _init():
        run_ref[...] = jnp.zeros_like(run_ref)

    logits = lax.dot_general(wr_ref[...], h_ref[...], (((1,), (1,)), ((), ())),
                             precision=lax.Precision.HIGHEST, preferred_element_type=f32)
    scores = 1.0 / (1.0 + jnp.exp(-logits))
    choice = scores + rb_ref[...]
    eid = lax.broadcasted_iota(jnp.int32, (E, R), 0)

    def first_argmax(vals, ids, none):
        m = jnp.max(vals, axis=0, keepdims=True)
        idx = jnp.min(jnp.where(vals == m, ids, none), axis=0, keepdims=True)
        return m, idx

    gid = lax.broadcasted_iota(jnp.int32, (GROUP_SIZE, R), 0)
    groups, gscore = [], []
    for g in range(N_GROUPS):
        vals = choice[g * GROUP_SIZE:(g + 1) * GROUP_SIZE]
        m1, i1 = first_argmax(vals, gid, GROUP_SIZE)
        m2 = jnp.max(jnp.where(gid == i1, neg, vals), axis=0, keepdims=True)
        groups.append(vals)
        gscore.append(m1 + m2)
    kept = []
    for g in range(N_GROUPS):
        beaten = jnp.zeros((1, R), f32)
        for g2 in range(N_GROUPS):
            if g2 == g:
                continue
            ahead = (gscore[g2] > gscore[g]) | (gscore[g2] == gscore[g]) if g2 < g else gscore[g2] > gscore[g]
            beaten = beaten + jnp.where(ahead, 1.0, 0.0)
        kept.append(jnp.where(beaten < TOPK_GROUPS, groups[g], neg))
    masked = jnp.concatenate(kept, axis=0)

    idxs, wts = [], []
    picked = jnp.zeros((E, R), f32)
    for _ in range(TOP_K):
        _, idx = first_argmax(masked, eid, E)
        hit = eid == idx
        idxs.append(idx)
        wts.append(jnp.sum(jnp.where(hit, scores, 0.0), axis=0, keepdims=True))
        masked = jnp.where(hit, neg, masked)
        picked = jnp.where(hit, 1.0, picked)
    wsum = wts[0]
    for k in range(1, TOP_K):
        wsum = wsum + wts[k]

    row = lax.broadcasted_iota(jnp.int32, (R, R), 0)
    col = lax.broadcasted_iota(jnp.int32, (R, R), 1)
    earlier = jnp.where(row < col, 1.0, 0.0).astype(jnp.bfloat16)
    picked_bf = picked.astype(jnp.bfloat16)
    run = run_ref[...]
    before = _dot(picked_bf, earlier) + jnp.concatenate([run] * (R // LANES), axis=1)
    sub_k = lax.broadcasted_iota(jnp.int32, (TOP_K, R), 0)
    e_out = jnp.zeros((TOP_K, R), jnp.int32)
    w_out = jnp.zeros((TOP_K, R), f32)
    rk_out = jnp.zeros((TOP_K, R), jnp.int32)
    for k in range(TOP_K):
        rank_k = jnp.sum(jnp.where(eid == idxs[k], before, 0.0), axis=0, keepdims=True)
        e_out = jnp.where(sub_k == k, idxs[k], e_out)
        w_out = jnp.where(sub_k == k, wts[k] / wsum * ROUTED_SCALE, w_out)
        rk_out = jnp.where(sub_k == k, rank_k.astype(jnp.int32), rk_out)
    e_ref[...] = e_out
    w_ref[...] = w_out
    rk_ref[...] = rk_out
    run_ref[...] = run + _dot(picked_bf, jnp.ones((R, LANES), jnp.bfloat16))
    cnt_ref[...] = run_ref[...]


def _router(h2, w_router, router_bias):
    T, D = h2.shape
    R = ROUTE_ROWS
    E = N_EXPERTS
    return pl.pallas_call(
        _router_kernel,
        grid=(T // R,),
        in_specs=[
            pl.BlockSpec((R, D), lambda i: (i, 0)),
            pl.BlockSpec((E, D), lambda i: (0, 0)),
            pl.BlockSpec((E, R), lambda i: (0, 0)),
        ],
        out_specs=[
            pl.BlockSpec((TOP_K, R), lambda i: (0, i)),
            pl.BlockSpec((TOP_K, R), lambda i: (0, i)),
            pl.BlockSpec((TOP_K, R), lambda i: (0, i)),
            pl.BlockSpec((E, LANES), lambda i: (0, 0)),
        ],
        out_shape=[
            jax.ShapeDtypeStruct((TOP_K, T), jnp.int32),
            jax.ShapeDtypeStruct((TOP_K, T), jnp.float32),
            jax.ShapeDtypeStruct((TOP_K, T), jnp.int32),
            jax.ShapeDtypeStruct((E, LANES), jnp.float32),
        ],
        scratch_shapes=[pltpu.VMEM((E, LANES), jnp.float32)],
        compiler_params=pltpu.CompilerParams(
            dimension_semantics=("arbitrary",), vmem_limit_bytes=VMEM_LIMIT_BYTES),
    )(h2, w_router.T, jnp.broadcast_to(router_bias.astype(jnp.float32)[:, None], (E, R)))


def _positions_kernel(row_start_ref, e_ref, rk_ref, pos_ref):
    e = e_ref[...]
    rk = rk_ref[...]

    def per_expert(i, pos):
        return jnp.where(e == i, rk + row_start_ref[i], pos)

    pos_ref[...] = lax.fori_loop(0, N_EXPERTS, per_expert, jnp.zeros_like(rk))


def _positions(e_idx, rank, row_start):
    n = e_idx.size
    shape = (n // LANES, LANES)
    grid_spec = pltpu.PrefetchScalarGridSpec(
        num_scalar_prefetch=1,
        grid=(1,),
        in_specs=[pl.BlockSpec(shape, lambda i, *_: (0, 0)), pl.BlockSpec(shape, lambda i, *_: (0, 0))],
        out_specs=pl.BlockSpec(shape, lambda i, *_: (0, 0)),
    )
    pos = pl.pallas_call(
        _positions_kernel,
        grid_spec=grid_spec,
        out_shape=jax.ShapeDtypeStruct(shape, jnp.int32),
    )(row_start, e_idx.reshape(shape), rank.reshape(shape))
    return pos.reshape(n)


def _dispatch_kernel(row_start_ref, cnt_ref, n_act_ref, pos_ref, h_ref, xs_ref, zero_ref, sem, zsem,
                     *, n_tokens):
    R = h_ref.shape[0] // ROW_TILE
    BM = EXPERT_ROWS
    n_blocks = xs_ref.shape[0] // (BM * ROW_TILE)
    n_pad_units = n_blocks - n_tokens * TOP_K // BM

    @pl.when(pl.program_id(0) == 0)
    def _zero_padding():
        zero_ref[...] = jnp.zeros_like(zero_ref)

        def expert_tail(e, carry):
            n_tail = pl.multiple_of(((BM - cnt_ref[e] % BM) % BM) * ROW_TILE, ROW_TILE)

            @pl.when(n_tail > 0)
            def _():
                dst = pl.multiple_of((row_start_ref[e] + cnt_ref[e]) * ROW_TILE, ROW_TILE)
                pltpu.make_async_copy(zero_ref.at[pl.ds(0, n_tail)], xs_ref.at[pl.ds(dst, n_tail)], zsem).start()
            return carry

        lax.fori_loop(0, N_EXPERTS, expert_tail, 0)

        def idle_block(i, carry):
            dst = pl.multiple_of(i * (BM * ROW_TILE), BM * ROW_TILE)
            pltpu.make_async_copy(zero_ref, xs_ref.at[pl.ds(dst, BM * ROW_TILE)], zsem).start()
            return carry

        lax.fori_loop(n_act_ref[0], n_blocks, idle_block, 0)

        def drain(i, carry):
            pltpu.make_async_copy(zero_ref, xs_ref.at[pl.ds(0, BM * ROW_TILE)], zsem).wait()
            return carry

        lax.fori_loop(0, n_pad_units, drain, 0)

    def issue(t, carry):
        src = h_ref.at[_row_tile(t)]
        for k in range(TOP_K):
            dest = pos_ref[t * TOP_K + k]
            pltpu.make_async_copy(src, xs_ref.at[_row_tile(dest)], sem).start(priority=k % 2)
        return carry

    lax.fori_loop(0, R, issue, 0)
    n = R * TOP_K * ROW_TILE
    pltpu.make_async_copy(xs_ref.at[pl.ds(0, n)], xs_ref.at[pl.ds(0, n)], sem).wait()


def _dispatch(h2, pos, row_start, cnt, n_act, n_rows):
    T = h2.shape[0] // ROW_TILE
    R = DISPATCH_ROWS
    grid_spec = pltpu.PrefetchScalarGridSpec(
        num_scalar_prefetch=3,
        grid=(T // R,),
        in_specs=[
            pl.BlockSpec((R * TOP_K,), lambda i, *_: (i,), memory_space=pltpu.SMEM),
            pl.BlockSpec((R * ROW_TILE, LANES), lambda i, *_: (i, 0)),
        ],
        out_specs=pl.BlockSpec(memory_space=pl.ANY),
        scratch_shapes=[pltpu.VMEM((EXPERT_ROWS * ROW_TILE, LANES), jnp.float32),
                        pltpu.SemaphoreType.DMA(()), pltpu.SemaphoreType.DMA(())],
    )
    return pl.pallas_call(
        functools.partial(_dispatch_kernel, n_tokens=T),
        grid_spec=grid_spec,
        out_shape=jax.ShapeDtypeStruct((n_rows * ROW_TILE, LANES), jnp.float32),
        compiler_params=pltpu.CompilerParams(
            dimension_semantics=("arbitrary",), vmem_limit_bytes=VMEM_LIMIT_BYTES),
    )(row_start, cnt, n_act, pos, h2)


SC_WINDOW = 32


def _dispatch_sc(h2, pos_kt, n_rows):
    T, D = h2.shape
    W = SC_WINDOW
    idx = pos_kt.reshape(TOP_K, T // W, W).transpose(1, 0, 2).reshape(T * TOP_K // LANES, LANES)
    idx_rows = TOP_K * W // LANES
    mesh = plsc.VectorSubcoreMesh(core_axis_name="core", subcore_axis_name="subcore")

    @pl.kernel(out_type=jax.ShapeDtypeStruct((n_rows, D), h2.dtype), mesh=mesh, scratch_types=[])
    def scatter_rows(x_hbm, i_hbm, o_hbm):
        def body(x_vmem, i_vmem):
            for k in range(TOP_K):
                r, q = divmod(k * W, LANES)
                pltpu.sync_copy(x_vmem, o_hbm.at[i_vmem.at[r, pl.ds(q, W)]])

        pltpu.emit_pipeline(
            body,
            grid=(T // W,),
            in_specs=[pl.BlockSpec((W, D), lambda i: (i, 0)),
                      pl.BlockSpec((idx_rows, LANES), lambda i: (i, 0))],
            out_specs=[],
            core_axis_name=("core", "subcore"),
            dimension_semantics=(pltpu.PARALLEL,),
        )(x_hbm, i_hbm)

    return scatter_rows(h2, idx)


X_SLOTS = 4
Y_SLOTS = 3


def _experts_kernel(blk_e_ref, first_ref, slot_ref, next_e_ref, valid_ref, n_act_ref, xs_hbm, wg_hbm, wu_hbm,
                    wd_hbm, ys_hbm, x_buf, y_buf, wg_buf, wu_buf, wd_buf, wg_bf, wu_bf, wd_bf, sems, x_sems,
                    y_sems):
    i = pl.program_id(0)
    n_act = n_act_ref[0]
    bf16 = jnp.bfloat16
    blk = EXPERT_ROWS * ROW_TILE

    def hbm_rows(j):
        return pl.ds(pl.multiple_of(j * blk, blk), pl.multiple_of(valid_ref[j] * ROW_TILE, ROW_TILE))

    def vmem_rows(j):
        return pl.ds(0, pl.multiple_of(valid_ref[j] * ROW_TILE, ROW_TILE))

    def x_copy(j):
        n = pl.multiple_of((valid_ref[j] + 7) // 8 * 8, 8)
        src = xs_hbm.at[pl.ds(pl.multiple_of(j * EXPERT_ROWS, EXPERT_ROWS), n)]
        return pltpu.make_async_copy(src, x_buf.at[j % X_SLOTS, pl.ds(0, n)], x_sems.at[j % X_SLOTS])

    def y_copy(j):
        return pltpu.make_async_copy(y_buf.at[j % Y_SLOTS, vmem_rows(j)], ys_hbm.at[hbm_rows(j)],
                                     y_sems.at[j % Y_SLOTS])

    @pl.when(i == 0)
    def _prime():
        x_buf[...] = jnp.zeros_like(x_buf)
        for j in range(X_SLOTS - 1):
            @pl.when(j < n_act)
            def _():
                x_copy(j).start()

    @pl.when(i + (X_SLOTS - 1) < n_act)
    def _prefetch():
        x_copy(i + (X_SLOTS - 1)).start()

    def weight_copies(e, slot):
        return (pltpu.make_async_copy(wg_hbm.at[e], wg_buf.at[slot], sems.at[slot]),
                pltpu.make_async_copy(wu_hbm.at[e], wu_buf.at[slot], sems.at[slot]),
                pltpu.make_async_copy(wd_hbm.at[e], wd_buf.at[slot], sems.at[slot]))

    @pl.when((i < n_act_ref[0]) & (first_ref[i] == 1))
    def _new_expert():
        slot = slot_ref[i]

        @pl.when(i == 0)
        def _():
            for c in weight_copies(blk_e_ref[0], 0):
                c.start()

        for c in weight_copies(blk_e_ref[i], slot):
            c.wait()

        @pl.when(next_e_ref[i] >= 0)
        def _():
            for c in weight_copies(next_e_ref[i], 1 - slot):
                c.start()

        wg_bf[...] = wg_buf[slot].astype(bf16)
        wu_bf[...] = wu_buf[slot].astype(bf16)
        wd_bf[...] = wd_buf[slot].astype(bf16)

    @pl.when(i < n_act)
    def _compute():
        x_copy(i).wait()
        x = x_buf[i % X_SLOTS].astype(bf16)
        g = _dot(x, wg_bf[...])
        u = _dot(x, wu_bf[...])
        a = (_silu(g) * u).astype(bf16)
        y = _dot(a, wd_bf[...])

        @pl.when(i >= Y_SLOTS)
        def _():
            y_copy(i - Y_SLOTS).wait()

        _store_rows(y_buf, y, lead=(i % Y_SLOTS,))
        y_copy(i).start()

    @pl.when(i == n_act - 1)
    def _drain():
        for d in range(Y_SLOTS):
            @pl.when(i - d >= 0)
            def _():
                y_copy(i - d).wait()


def _experts(xs, blk_e, n_act, row_start, cnt, w_gate, w_up, w_down):
    D = D_MODEL
    BM = EXPERT_ROWS
    F = EXPERT_DIM
    n_blocks = xs.shape[0] // BM
    blk_in_expert = jnp.arange(n_blocks, dtype=jnp.int32) - row_start[blk_e] // BM
    valid = jnp.clip(cnt[blk_e] - blk_in_expert * BM, 0, BM).astype(jnp.int32)
    ids = jnp.arange(n_blocks, dtype=jnp.int32)
    active = ids < n_act[0]
    first = active & ((ids == 0) | (blk_e != jnp.roll(blk_e, 1)))
    slot = ((jnp.cumsum(first.astype(jnp.int32)) - 1) % 2).astype(jnp.int32)
    first_pos = jnp.where(first, ids, n_blocks)
    later_first = lax.cummin(jnp.concatenate([first_pos[1:], jnp.full((1,), n_blocks, jnp.int32)]), reverse=True)
    next_e = jnp.where(later_first < n_blocks, blk_e[jnp.minimum(later_first, n_blocks - 1)], -1).astype(jnp.int32)

    grid_spec = pltpu.PrefetchScalarGridSpec(
        num_scalar_prefetch=6,
        grid=(n_blocks,),
        in_specs=[pl.BlockSpec(memory_space=pl.ANY)] * 4,
        out_specs=pl.BlockSpec(memory_space=pl.ANY),
        scratch_shapes=[
            pltpu.VMEM((X_SLOTS, BM, D), jnp.float32),
            pltpu.VMEM((Y_SLOTS, BM * ROW_TILE, LANES), jnp.float32),
            pltpu.VMEM((2, D, F), jnp.float32), pltpu.VMEM((2, D, F), jnp.float32),
            pltpu.VMEM((2, F, D), jnp.float32),
            pltpu.VMEM((D, F), jnp.bfloat16), pltpu.VMEM((D, F), jnp.bfloat16),
            pltpu.VMEM((F, D), jnp.bfloat16),
            pltpu.SemaphoreType.DMA((2,)), pltpu.SemaphoreType.DMA((X_SLOTS,)),
            pltpu.SemaphoreType.DMA((Y_SLOTS,)),
        ],
    )
    return pl.pallas_call(
        _experts_kernel,
        grid_spec=grid_spec,
        out_shape=jax.ShapeDtypeStruct((xs.shape[0] * ROW_TILE, LANES), jnp.float32),
        compiler_params=pltpu.CompilerParams(
            dimension_semantics=("arbitrary",), vmem_limit_bytes=VMEM_LIMIT_BYTES),
    )(blk_e, first.astype(jnp.int32), slot, next_e, valid, n_act, xs, w_gate, w_up, w_down)


def _combine_kernel(pos_ref, pos_next_ref, h_ref, w_ref, ys_ref, wsg_ref, wsu_ref, wsd_ref,
                    g_ref, b_ref, out_ref, buf_ref, sems):
    R = h_ref.shape[0]
    i = pl.program_id(0)
    slot = i % 2

    def gather(p_ref, s):
        def issue(t, carry):
            for k in range(TOP_K):
                src = p_ref[t * TOP_K + k]
                pltpu.make_async_copy(ys_ref.at[_row_tile(src)], buf_ref.at[s, k, _row_tile(t)],
                                      sems.at[s]).start(priority=k % 2)
            return carry

        lax.fori_loop(0, R, issue, 0)

    @pl.when(i == 0)
    def _():
        gather(pos_ref, 0)

    @pl.when(i + 1 < pl.num_programs(0))
    def _():
        gather(pos_next_ref, 1 - slot)

    h = h_ref[...]
    hb = h.astype(jnp.bfloat16)
    act = (_silu(_dot(hb, wsg_ref[...])) * _dot(hb, wsu_ref[...])).astype(jnp.bfloat16)
    ffn = _dot(act, wsd_ref[...])
    pltpu.make_async_copy(buf_ref.at[slot], buf_ref.at[slot], sems.at[slot]).wait()
    w = w_ref[...]
    for k in range(TOP_K):
        ffn = ffn + _load_rows(buf_ref, R, lead=(slot, k)) * w[:, k:k + 1]
    out_ref[...] = _layer_norm(DEEPNORM_ALPHA * h + ffn, g_ref[...], b_ref[...])


def _combine(h2, pos, top_w, ys, ws_gate, ws_up, ws_down, ln_g, ln_b):
    T, D = h2.shape
    R = COMBINE_ROWS
    F = SHARED_DIM
    bf16 = jnp.bfloat16
    const = lambda shape: pl.BlockSpec(shape, lambda i: (0,) * len(shape))
    return pl.pallas_call(
        _combine_kernel,
        grid=(T // R,),
        in_specs=[
            pl.BlockSpec((R * TOP_K,), lambda i: (i,), memory_space=pltpu.SMEM),
            pl.BlockSpec((R * TOP_K,), lambda i: (jnp.minimum(i + 1, T // R - 1),), memory_space=pltpu.SMEM),
            pl.BlockSpec((R, D), lambda i: (i, 0)),
            pl.BlockSpec((R, TOP_K), lambda i: (i, 0)),
            pl.BlockSpec(memory_space=pl.ANY),
            const((D, F)), const((D, F)), const((F, D)), const((1, D)), const((1, D)),
        ],
        out_specs=pl.BlockSpec((R, D), lambda i: (i, 0)),
        scratch_shapes=[pltpu.VMEM((2, TOP_K, R * ROW_TILE, LANES), jnp.float32), pltpu.SemaphoreType.DMA((2,))],
        out_shape=jax.ShapeDtypeStruct((T, D), jnp.float32),
        compiler_params=pltpu.CompilerParams(
            dimension_semantics=("arbitrary",), vmem_limit_bytes=VMEM_LIMIT_BYTES),
    )(pos, pos, h2, top_w, ys, ws_gate.astype(bf16), ws_up.astype(bf16),
      ws_down.astype(bf16), ln_g.reshape(1, D), ln_b.reshape(1, D))


def _moe(h2, w_router, router_bias, w_gate, w_up, w_down, ws_gate, ws_up, ws_down, ln_g, ln_b):
    T = h2.shape[0]
    E = N_EXPERTS
    BM = EXPERT_ROWS
    e_idx, top_w, rank, counts = _router(h2, w_router, router_bias)
    cnt = counts[:, 0].astype(jnp.int32)
    nblk = (cnt + BM - 1) // BM
    blk_end = jnp.cumsum(nblk)
    row_start = ((blk_end - nblk) * BM).astype(jnp.int32)
    n_blocks = T * TOP_K // BM + E
    n_act = blk_end[-1:].astype(jnp.int32)
    blk_ids = jnp.minimum(jnp.arange(n_blocks, dtype=jnp.int32), n_act[0] - 1)
    blk_e = jnp.minimum(jnp.sum(blk_end[None, :] <= blk_ids[:, None], axis=1), E - 1).astype(jnp.int32)
    pos_kt = _positions(e_idx, rank, row_start).reshape(TOP_K, T)
    pos = pos_kt.T.reshape(T * TOP_K)
    xs = _dispatch_sc(h2, pos_kt, n_blocks * BM)
    ys = _experts(xs, blk_e, n_act, row_start, cnt, w_gate, w_up, w_down)
    return _combine(h2, pos, top_w.T, ys, ws_gate, ws_up, ws_down, ln_g, ln_b)


def kernel(x, ln_in_g, ln_in_b, w_in, w_out, rel_bias, attn_sinks, ln_mix_g, ln_mix_b, w_router,
           router_bias, w_gate, w_up, w_down, ws_gate, ws_up, ws_down, ln_ffn_g, ln_ffn_b):
    B, S, D = x.shape
    h = _mixer(x, ln_in_g, ln_in_b, w_in[0], w_out[0], rel_bias, attn_sinks[0], ln_mix_g[0], ln_mix_b[0])
    out = _moe(h, w_router[0], router_bias[0], w_gate[0], w_up[0], w_down[0],
               ws_gate[0], ws_up[0], ws_down[0], ln_ffn_g[0], ln_ffn_b[0])
    return out.reshape(B, S, D)
```

```python
import functools
import math

import jax
import jax.numpy as jnp
from jax import lax
from jax.experimental import pallas as pl
from jax.experimental.pallas import tpu as pltpu
from jax.experimental.pallas import tpu_sc as plsc

D_MODEL = 1024
DEPTH = 1
RET_HEADS = 4
RET_QK_DIM = 64
RET_V_DIM = 128
RET_CHUNK = 128
RET_WIDTH = RET_HEADS * RET_V_DIM
ROPE_BASE = 10000.0
SWA_HEADS = 8
SWA_KV_HEADS = 2
SWA_GROUP = SWA_HEADS // SWA_KV_HEADS
SWA_HEAD_DIM = 64
SWA_WINDOW = 128
SWA_WIDTH = SWA_HEADS * SWA_HEAD_DIM
MIX_WIDTH = RET_WIDTH + SWA_WIDTH
RQK = RET_HEADS * RET_QK_DIM
SKV = SWA_KV_HEADS * SWA_HEAD_DIM
IN_SIZES = (RQK, RQK, RET_WIDTH, RET_WIDTH, SWA_WIDTH, SKV, SKV)
IN_OFFS = tuple(sum(IN_SIZES[:i]) for i in range(len(IN_SIZES)))
IN_WIDTH = sum(IN_SIZES)
REL_BUCKETS = 32
REL_MAX_DIST = 128
N_EXPERTS = 256
TOP_K = 8
N_GROUPS = 8
GROUP_SIZE = N_EXPERTS // N_GROUPS
TOPK_GROUPS = 4
EXPERT_DIM = 256
SHARED_DIM = 256
ROUTED_SCALE = 2.5
LN_EPS = 1e-5
GN_EPS = 1e-6
DEEPNORM_ALPHA = (2 * DEPTH) ** 0.25
MASK_VALUE = -1e30

VMEM_LIMIT_BYTES = 56 * 1024 * 1024

MIX_ROWS = 256
ROUTE_ROWS = 256
DISPATCH_ROWS = 256
EXPERT_ROWS = 256
COMBINE_ROWS = 128
FINISH_ROWS = 512


def _layer_norm(x, g, b):
    mu = jnp.mean(x, axis=-1, keepdims=True)
    xc = x - mu
    var = jnp.mean(xc * xc, axis=-1, keepdims=True)
    return xc * lax.rsqrt(var + LN_EPS) * g + b


def _dot(a, b):
    return jnp.dot(a, b, preferred_element_type=jnp.float32)


def _dot_nt(a, b):
    return lax.dot_general(a, b, (((1,), (1,)), ((), ())), preferred_element_type=jnp.float32)


def _dot_tn(a, b):
    return lax.dot_general(a, b, (((0,), (0,)), ((), ())), preferred_element_type=jnp.float32)


def _silu(x):
    return x * (1.0 / (1.0 + jnp.exp(-x)))


LANES = 128
ROW_TILE = D_MODEL // LANES


def _load_rows(ref, n_rows, lead=()):
    return jnp.concatenate([ref[lead + (pl.ds(s, n_rows, stride=ROW_TILE), slice(None))]
                            for s in range(ROW_TILE)], axis=1)


def _store_rows(ref, val, lead=()):
    n_rows = val.shape[0]
    for s in range(ROW_TILE):
        ref[lead + (pl.ds(s, n_rows, stride=ROW_TILE), slice(None))] = val[:, s * LANES:(s + 1) * LANES]


def _row_tile(r):
    return pl.ds(pl.multiple_of(r * ROW_TILE, ROW_TILE), ROW_TILE)


def _swap_halves(x):
    n = x.shape[-1]
    half = RET_QK_DIM // 2
    lane = lax.broadcasted_iota(jnp.int32, x.shape, 1)
    from_right = pltpu.roll(x, n - half, axis=1)
    from_left = pltpu.roll(x, half, axis=1)
    return jnp.where((lane % RET_QK_DIM) < half, from_right, from_left)


def _mixer_kernel(rel_bias_ref, x_ref, g_in_ref, b_in_ref, w_in_ref, w_out_ref, rot_ref, decay_ref,
                  zeta_ref, xi_ref, cdecay_ref, bucket_ref, sink_ref, g_mix_ref, b_mix_ref,
                  h2_ref, state_ref, kprev_ref, vprev_ref, bias_ref):
    b_id = pl.program_id(0)
    c_id = pl.program_id(1)
    W = SWA_WINDOW

    @pl.when((b_id == 0) & (c_id == 0))
    def _build_bias():
        bucket = bucket_ref[...]
        for h in range(SWA_HEADS):
            acc = jnp.full((2 * W, W), MASK_VALUE, jnp.float32)
            for b in range(REL_BUCKETS):
                acc = jnp.where(bucket == b, rel_bias_ref[b, h], acc)
            kh, g = divmod(h, SWA_GROUP)
            bias_ref[kh, :, g * W:(g + 1) * W] = acc

    @pl.when(c_id == 0)
    def _reset():
        state_ref[...] = jnp.zeros_like(state_ref)
        kprev_ref[...] = jnp.zeros_like(kprev_ref)
        vprev_ref[...] = jnp.zeros_like(vprev_ref)

    h = _layer_norm(x_ref[...], g_in_ref[...], b_in_ref[...])
    proj = _dot(h.astype(jnp.bfloat16), w_in_ref[...])

    o_q, o_k, o_v, o_g, o_sq, o_sk, o_sv = IN_OFFS
    cos_t = rot_ref[:, :RQK]
    sin_t = rot_ref[:, RQK:]
    q_all = proj[:, o_q:o_q + RQK]
    k_all = proj[:, o_k:o_k + RQK]
    q_rot = q_all * cos_t + _swap_halves(q_all) * sin_t
    k_rot = (k_all * cos_t + _swap_halves(k_all) * sin_t) * (RET_QK_DIM ** -0.5)

    n_sub = x_ref.shape[0] // RET_CHUNK
    states = [state_ref[hh] for hh in range(RET_HEADS)]
    k_prev = kprev_ref[...]
    v_prev = vprev_ref[...]
    cat_rows = []
    for s in range(n_sub):
        r0 = s * RET_CHUNK
        rows = slice(r0, r0 + RET_CHUNK)
        pieces = []
        for hh in range(RET_HEADS):
            qk = slice(hh * RET_QK_DIM, (hh + 1) * RET_QK_DIM)
            vv = slice(o_v + hh * RET_V_DIM, o_v + (hh + 1) * RET_V_DIM)
            gg = slice(o_g + hh * RET_V_DIM, o_g + (hh + 1) * RET_V_DIM)
            q = q_rot[rows, qk].astype(jnp.bfloat16)
            k32 = k_rot[rows, qk]
            v = proj[rows, vv].astype(jnp.bfloat16)
            scores = _dot_nt(q, k32.astype(jnp.bfloat16)) * decay_ref[hh]
            intra = _dot(scores.astype(jnp.bfloat16), v)
            inter = _dot(q, states[hh].astype(jnp.bfloat16)) * xi_ref[hh]
            ret = intra + inter
            kz = (k32 * zeta_ref[hh]).astype(jnp.bfloat16)
            states[hh] = states[hh] * cdecay_ref[hh] + _dot_tn(kz, v)
            mu = jnp.mean(ret, axis=-1, keepdims=True)
            rc = ret - mu
            var = jnp.mean(rc * rc, axis=-1, keepdims=True)
            normed = rc * lax.rsqrt(var + GN_EPS)
            pieces.append((_silu(proj[rows, gg]) * normed).astype(jnp.bfloat16))
        k_cur = proj[rows, o_sk:o_sk + SKV].astype(jnp.bfloat16)
        v_cur = proj[rows, o_sv:o_sv + SKV].astype(jnp.bfloat16)
        for kh in range(SWA_KV_HEADS):
            kv = slice(kh * SWA_HEAD_DIM, (kh + 1) * SWA_HEAD_DIM)
            q4 = jnp.concatenate(
                [proj[rows, o_sq + (kh * SWA_GROUP + g) * SWA_HEAD_DIM:
                      o_sq + (kh * SWA_GROUP + g + 1) * SWA_HEAD_DIM] for g in range(SWA_GROUP)],
                axis=0) * (SWA_HEAD_DIM ** -0.5)
            kcat = jnp.concatenate([k_prev[:, kv], k_cur[:, kv]], axis=0)
            vcat = jnp.concatenate([v_prev[:, kv], v_cur[:, kv]], axis=0)
            logits = _dot_nt(kcat, q4.astype(jnp.bfloat16)) + bias_ref[kh]
            if s == 0:
                key = lax.broadcasted_iota(jnp.int32, logits.shape, 0)
                logits = logits + jnp.where((key < W) & (c_id == 0), MASK_VALUE, 0.0)
            sink = sink_ref[kh]
            m = jnp.maximum(jnp.max(logits, axis=0, keepdims=True), sink)
            p = jnp.exp(logits - m)
            den = jnp.sum(p, axis=0, keepdims=True) + jnp.exp(sink - m)
            probs = (p / den).astype(jnp.bfloat16)
            o4 = _dot_tn(vcat, probs)
            pieces.extend(o4[:, g * W:(g + 1) * W].T.astype(jnp.bfloat16) for g in range(SWA_GROUP))
        k_prev, v_prev = k_cur, v_cur
        cat_rows.append(jnp.concatenate(pieces, axis=1))
    for hh in range(RET_HEADS):
        state_ref[hh] = states[hh]
    kprev_ref[...] = k_prev
    vprev_ref[...] = v_prev

    mix = _dot(jnp.concatenate(cat_rows, axis=0), w_out_ref[...])
    h2_ref[...] = _layer_norm(DEEPNORM_ALPHA * h + mix, g_mix_ref[...], b_mix_ref[...])


def _t5_bucket(dist):
    n = jnp.maximum(dist, 0)
    max_exact = REL_BUCKETS // 2
    ratio = jnp.log(jnp.maximum(n, 1).astype(jnp.float32) / max_exact) / math.log(REL_MAX_DIST / max_exact)
    large = jnp.minimum(max_exact + (ratio * (REL_BUCKETS - max_exact)).astype(jnp.int32), REL_BUCKETS - 1)
    return jnp.where(n < max_exact, n, large)


def _mixer(x, ln_in_g, ln_in_b, w_in, w_out, rel_bias, sinks, ln_mix_g, ln_mix_b):
    B, S, D = x.shape
    R = MIX_ROWS
    C = RET_CHUNK
    W = SWA_WINDOW
    f32 = jnp.float32
    half = RET_QK_DIM // 2
    inv = ROPE_BASE ** (-jnp.arange(half, dtype=f32) / half)
    ang = jnp.arange(S, dtype=f32)[:, None] * inv[None, :]
    cos, sin = jnp.cos(ang), jnp.sin(ang)
    cos_t = jnp.tile(jnp.concatenate([cos, cos], axis=-1), (1, RET_HEADS))
    sin_t = jnp.tile(jnp.concatenate([-sin, sin], axis=-1), (1, RET_HEADS))
    rot = jnp.concatenate([cos_t, sin_t], axis=-1)
    log_gamma = jnp.log(1.0 - 2.0 ** (-5.0 - jnp.arange(RET_HEADS, dtype=f32)))
    idx = jnp.arange(C, dtype=f32)
    diff = idx[:, None] - idx[None, :]
    decay = jnp.where(diff[None] >= 0, jnp.exp(jnp.maximum(diff, 0.0)[None] * log_gamma[:, None, None]), 0.0)
    zeta = jnp.exp((C - 1.0 - idx)[None, :] * log_gamma[:, None])
    xi = jnp.exp((idx + 1.0)[None, :] * log_gamma[:, None])
    zeta_b = jnp.broadcast_to(zeta[:, :, None], (RET_HEADS, C, RET_QK_DIM))
    xi_b = jnp.broadcast_to(xi[:, :, None], (RET_HEADS, C, RET_V_DIM))
    cdecay = jnp.broadcast_to(jnp.exp(C * log_gamma)[:, None, None], (RET_HEADS, RET_QK_DIM, RET_V_DIM))
    i = jnp.arange(W)
    j = jnp.arange(2 * W)
    dist = i[:, None] + W - j[None, :]
    bucket = jnp.where((dist >= 0) & (dist < W), _t5_bucket(dist), -1).astype(jnp.int32).T
    sink_row = jnp.repeat(sinks.astype(f32), W).reshape(SWA_KV_HEADS, 1, SWA_GROUP * W)

    const = lambda shape: pl.BlockSpec(shape, lambda b, c, *_: (0,) * len(shape))
    grid_spec = pltpu.PrefetchScalarGridSpec(
        num_scalar_prefetch=1,
        grid=(B, S // R),
        in_specs=[
            pl.BlockSpec((None, R, D), lambda b, c, *_: (b, c, 0)),
            const((1, D)), const((1, D)),
            const((D, IN_WIDTH)), const((MIX_WIDTH, D)),
            pl.BlockSpec((R, 2 * RQK), lambda b, c, *_: (c, 0)),
            const((RET_HEADS, C, C)), const((RET_HEADS, C, RET_QK_DIM)), const((RET_HEADS, C, RET_V_DIM)),
            const((RET_HEADS, RET_QK_DIM, RET_V_DIM)),
            const((2 * W, W)), const((SWA_KV_HEADS, 1, SWA_GROUP * W)),
            const((1, D)), const((1, D)),
        ],
        out_specs=pl.BlockSpec((R, D), lambda b, c, *_: (b * (S // R) + c, 0)),
        scratch_shapes=[
            pltpu.VMEM((RET_HEADS, RET_QK_DIM, RET_V_DIM), f32),
            pltpu.VMEM((W, SKV), jnp.bfloat16),
            pltpu.VMEM((W, SKV), jnp.bfloat16),
            pltpu.VMEM((SWA_KV_HEADS, 2 * W, SWA_GROUP * W), f32),
        ],
    )
    return pl.pallas_call(
        _mixer_kernel,
        grid_spec=grid_spec,
        out_shape=jax.ShapeDtypeStruct((B * S, D), f32),
        compiler_params=pltpu.CompilerParams(
            dimension_semantics=("arbitrary", "arbitrary"), vmem_limit_bytes=VMEM_LIMIT_BYTES),
    )(rel_bias.astype(f32), x, ln_in_g.reshape(1, D), ln_in_b.reshape(1, D),
      w_in.astype(jnp.bfloat16), w_out.astype(jnp.bfloat16), rot, decay, zeta_b, xi_b, cdecay,
      bucket, sink_row, ln_mix_g.reshape(1, D), ln_mix_b.reshape(1, D))


def _router_kernel(h_ref, wr_ref, rb_ref, e_ref, w_ref, rk_ref, cnt_ref, run_ref):
    f32 = jnp.float32
    R = h_ref.shape[0]
    E = N_EXPERTS
    neg = -jnp.inf

    @pl.when(pl.program_id(0) == 0)
    def _init():
        run_ref[...] = jnp.zeros_like(run_ref)

    logits = lax.dot_general(wr_ref[...], h_ref[...], (((1,), (1,)), ((), ())),
                             precision=lax.Precision.HIGHEST, preferred_element_type=f32)
    scores = 1.0 / (1.0 + jnp.exp(-logits))
    choice = scores + rb_ref[...]
    eid = lax.broadcasted_iota(jnp.int32, (E, R), 0)

    def first_argmax(vals, ids, none):
        m = jnp.max(vals, axis=0, keepdims=True)
        idx = jnp.min(jnp.where(vals == m, ids, none), axis=0, keepdims=True)
        return m, idx

    gid = lax.broadcasted_iota(jnp.int32, (GROUP_SIZE, R), 0)
    groups, gscore = [], []
    for g in range(N_GROUPS):
        vals = choice[g * GROUP_SIZE:(g + 1) * GROUP_SIZE]
        m1, i1 = first_argmax(vals, gid, GROUP_SIZE)
        m2 = jnp.max(jnp.where(gid == i1, neg, vals), axis=0, keepdims=True)
        groups.append(vals)
        gscore.append(m1 + m2)
    kept = []
    for g in range(N_GROUPS):
        beaten = jnp.zeros((1, R), f32)
        for g2 in range(N_GROUPS):
            if g2 == g:
                continue
            ahead = (gscore[g2] > gscore[g]) | (gscore[g2] == gscore[g]) if g2 < g else gscore[g2] > gscore[g]
            beaten = beaten + jnp.where(ahead, 1.0, 0.0)
        kept.append(jnp.where(beaten < TOPK_GROUPS, groups[g], neg))
    masked = jnp.concatenate(kept, axis=0)

    idxs, wts = [], []
    picked = jnp.zeros((E, R), f32)
    for _ in range(TOP_K):
        _, idx = first_argmax(masked, eid, E)
        hit = eid == idx
        idxs.append(idx)
        wts.append(jnp.sum(jnp.where(hit, scores, 0.0), axis=0, keepdims=True))
        masked = jnp.where(hit, neg, masked)
        picked = jnp.where(hit, 1.0, picked)
    wsum = wts[0]
    for k in range(1, TOP_K):
        wsum = wsum + wts[k]

    row = lax.broadcasted_iota(jnp.int32, (R, R), 0)
    col = lax.broadcasted_iota(jnp.int32, (R, R), 1)
    earlier = jnp.where(row < col, 1.0, 0.0).astype(jnp.bfloat16)
    picked_bf = picked.astype(jnp.bfloat16)
    run = run_ref[...]
    before = _dot(picked_bf, earlier) + jnp.concatenate([run] * (R // LANES), axis=1)
    sub_k = lax.broadcasted_iota(jnp.int32, (TOP_K, R), 0)
    e_out = jnp.zeros((TOP_K, R), jnp.int32)
    w_out = jnp.zeros((TOP_K, R), f32)
    rk_out = jnp.zeros((TOP_K, R), jnp.int32)
    for k in range(TOP_K):
        rank_k = jnp.sum(jnp.where(eid == idxs[k], before, 0.0), axis=0, keepdims=True)
        e_out = jnp.where(sub_k == k, idxs[k], e_out)
        w_out = jnp.where(sub_k == k, wts[k] / wsum * ROUTED_SCALE, w_out)
        rk_out = jnp.where(sub_k == k, rank_k.astype(jnp.int32), rk_out)
    e_ref[...] = e_out
    w_ref[...] = w_out
    rk_ref[...] = rk_out
    run_ref[...] = run + _dot(picked_bf, jnp.ones((R, LANES), jnp.bfloat16))
    cnt_ref[...] = run_ref[...]


def _router(h2, w_router, router_bias):
    T, D = h2.shape
    R = ROUTE_ROWS
    E = N_EXPERTS
    return pl.pallas_call(
        _router_kernel,
        grid=(T // R,),
        in_specs=[
            pl.BlockSpec((R, D), lambda i: (i, 0)),
            pl.BlockSpec((E, D), lambda i: (0, 0)),
            pl.BlockSpec((E, R), lambda i: (0, 0)),
        ],
        out_specs=[
            pl.BlockSpec((TOP_K, R), lambda i: (0, i)),
            pl.BlockSpec((TOP_K, R), lambda i: (0, i)),
            pl.BlockSpec((TOP_K, R), lambda i: (0, i)),
            pl.BlockSpec((E, LANES), lambda i: (0, 0)),
        ],
        out_shape=[
            jax.ShapeDtypeStruct((TOP_K, T), jnp.int32),
            jax.ShapeDtypeStruct((TOP_K, T), jnp.float32),
            jax.ShapeDtypeStruct((TOP_K, T), jnp.int32),
            jax.ShapeDtypeStruct((E, LANES), jnp.float32),
        ],
        scratch_shapes=[pltpu.VMEM((E, LANES), jnp.float32)],
        compiler_params=pltpu.CompilerParams(
            dimension_semantics=("arbitrary",), vmem_limit_bytes=VMEM_LIMIT_BYTES),
    )(h2, w_router.T, jnp.broadcast_to(router_bias.astype(jnp.float32)[:, None], (E, R)))


def _positions_kernel(row_start_ref, e_ref, rk_ref, pos_ref):
    e = e_ref[...]
    rk = rk_ref[...]

    def per_expert(i, pos):
        return jnp.where(e == i, rk + row_start_ref[i], pos)

    pos_ref[...] = lax.fori_loop(0, N_EXPERTS, per_expert, jnp.zeros_like(rk))


def _positions(e_idx, rank, row_start):
    n = e_idx.size
    shape = (n // LANES, LANES)
    grid_spec = pltpu.PrefetchScalarGridSpec(
        num_scalar_prefetch=1,
        grid=(1,),
        in_specs=[pl.BlockSpec(shape, lambda i, *_: (0, 0)), pl.BlockSpec(shape, lambda i, *_: (0, 0))],
        out_specs=pl.BlockSpec(shape, lambda i, *_: (0, 0)),
    )
    pos = pl.pallas_call(
        _positions_kernel,
        grid_spec=grid_spec,
        out_shape=jax.ShapeDtypeStruct(shape, jnp.int32),
    )(row_start, e_idx.reshape(shape), rank.reshape(shape))
    return pos.reshape(n)


def _dispatch_kernel(row_start_ref, cnt_ref, n_act_ref, pos_ref, h_ref, xs_ref, zero_ref, sem, zsem,
                     *, n_tokens):
    R = h_ref.shape[0] // ROW_TILE
    BM = EXPERT_ROWS
    n_blocks = xs_ref.shape[0] // (BM * ROW_TILE)
    n_pad_units = n_blocks - n_tokens * TOP_K // BM

    @pl.when(pl.program_id(0) == 0)
    def _zero_padding():
        zero_ref[...] = jnp.zeros_like(zero_ref)

        def expert_tail(e, carry):
            n_tail = pl.multiple_of(((BM - cnt_ref[e] % BM) % BM) * ROW_TILE, ROW_TILE)

            @pl.when(n_tail > 0)
            def _():
                dst = pl.multiple_of((row_start_ref[e] + cnt_ref[e]) * ROW_TILE, ROW_TILE)
                pltpu.make_async_copy(zero_ref.at[pl.ds(0, n_tail)], xs_ref.at[pl.ds(dst, n_tail)], zsem).start()
            return carry

        lax.fori_loop(0, N_EXPERTS, expert_tail, 0)

        def idle_block(i, carry):
            dst = pl.multiple_of(i * (BM * ROW_TILE), BM * ROW_TILE)
            pltpu.make_async_copy(zero_ref, xs_ref.at[pl.ds(dst, BM * ROW_TILE)], zsem).start()
            return carry

        lax.fori_loop(n_act_ref[0], n_blocks, idle_block, 0)

        def drain(i, carry):
            pltpu.make_async_copy(zero_ref, xs_ref.at[pl.ds(0, BM * ROW_TILE)], zsem).wait()
            return carry

        lax.fori_loop(0, n_pad_units, drain, 0)

    def issue(t, carry):
        src = h_ref.at[_row_tile(t)]
        for k in range(TOP_K):
            dest = pos_ref[t * TOP_K + k]
            pltpu.make_async_copy(src, xs_ref.at[_row_tile(dest)], sem).start(priority=k % 2)
        return carry

    lax.fori_loop(0, R, issue, 0)
    n = R * TOP_K * ROW_TILE
    pltpu.make_async_copy(xs_ref.at[pl.ds(0, n)], xs_ref.at[pl.ds(0, n)], sem).wait()


def _dispatch(h2, pos, row_start, cnt, n_act, n_rows):
    T = h2.shape[0] // ROW_TILE
    R = DISPATCH_ROWS
    grid_spec = pltpu.PrefetchScalarGridSpec(
        num_scalar_prefetch=3,
        grid=(T // R,),
        in_specs=[
            pl.BlockSpec((R * TOP_K,), lambda i, *_: (i,), memory_space=pltpu.SMEM),
            pl.BlockSpec((R * ROW_TILE, LANES), lambda i, *_: (i, 0)),
        ],
        out_specs=pl.BlockSpec(memory_space=pl.ANY),
        scratch_shapes=[pltpu.VMEM((EXPERT_ROWS * ROW_TILE, LANES), jnp.float32),
                        pltpu.SemaphoreType.DMA(()), pltpu.SemaphoreType.DMA(())],
    )
    return pl.pallas_call(
        functools.partial(_dispatch_kernel, n_tokens=T),
        grid_spec=grid_spec,
        out_shape=jax.ShapeDtypeStruct((n_rows * ROW_TILE, LANES), jnp.float32),
        compiler_params=pltpu.CompilerParams(
            dimension_semantics=("arbitrary",), vmem_limit_bytes=VMEM_LIMIT_BYTES),
    )(row_start, cnt, n_act, pos, h2)


SC_WINDOW = 32


def _dispatch_sc(h2, pos_kt, n_rows):
    T, D = h2.shape
    W = SC_WINDOW
    idx = _window_indices(pos_kt, W)
    idx_rows = TOP_K * W // LANES
    mesh = plsc.VectorSubcoreMesh(core_axis_name="core", subcore_axis_name="subcore")

    @pl.kernel(out_type=jax.ShapeDtypeStruct((n_rows, D), h2.dtype), mesh=mesh, scratch_types=[])
    def scatter_rows(x_hbm, i_hbm, o_hbm):
        def body(x_vmem, i_vmem):
            for k in range(TOP_K):
                r, q = divmod(k * W, LANES)
                pltpu.sync_copy(x_vmem, o_hbm.at[i_vmem.at[r, pl.ds(q, W)]])

        pltpu.emit_pipeline(
            body,
            grid=(T // W,),
            in_specs=[pl.BlockSpec((W, D), lambda i: (i, 0)),
                      pl.BlockSpec((idx_rows, LANES), lambda i: (i, 0))],
            out_specs=[],
            core_axis_name=("core", "subcore"),
            dimension_semantics=(pltpu.PARALLEL,),
        )(x_hbm, i_hbm)

    return scatter_rows(h2, idx)


def _window_indices(pos_kt, window):
    K, T = pos_kt.shape
    return pos_kt.reshape(K, T // window, window).transpose(1, 0, 2).reshape(T * K // LANES, LANES)


def _gather_rows_sc(ys, pos_kt):
    n = pos_kt.size
    W = SC_WINDOW
    ys3 = ys.reshape(ys.shape[0] // ROW_TILE, ROW_TILE, LANES)
    idx = jnp.pad(pos_kt.reshape(n // W, W), ((0, 0), (0, LANES - W)))
    mesh = plsc.VectorSubcoreMesh(core_axis_name="core", subcore_axis_name="subcore")

    @pl.kernel(out_type=jax.ShapeDtypeStruct((n, ROW_TILE, LANES), ys.dtype), mesh=mesh, scratch_types=[])
    def gather_rows(y_hbm, i_hbm, o_hbm):
        def body(i_vmem, o_vmem):
            pltpu.sync_copy(y_hbm.at[i_vmem.at[0, pl.ds(0, W)]], o_vmem)

        pltpu.emit_pipeline(
            body,
            grid=(n // W,),
            in_specs=[pl.BlockSpec((1, LANES), lambda i: (i, 0))],
            out_specs=[pl.BlockSpec((W, ROW_TILE, LANES), lambda i: (i, 0, 0))],
            core_axis_name=("core", "subcore"),
            dimension_semantics=(pltpu.PARALLEL,),
        )(i_hbm, o_hbm)

    return gather_rows(ys3, idx).reshape(n * ROW_TILE, LANES)


X_SLOTS = 4
Y_SLOTS = 3


def _experts_kernel(blk_e_ref, first_ref, slot_ref, next_e_ref, valid_ref, n_act_ref, xs_hbm, wg_hbm,
                    wu_hbm, wd_hbm, ys_hbm, x_buf, y_buf, wg_buf, wu_buf, wd_buf, wg_bf, wu_bf, wd_bf,
                    sems, x_sems, y_sems):
    i = pl.program_id(0)
    n_act = n_act_ref[0]
    bf16 = jnp.bfloat16
    blk = EXPERT_ROWS * ROW_TILE

    def x_copy(j):
        n = pl.multiple_of((valid_ref[j] + 7) // 8 * 8, 8)
        src = xs_hbm.at[pl.ds(pl.multiple_of(j * EXPERT_ROWS, EXPERT_ROWS), n)]
        return pltpu.make_async_copy(src, x_buf.at[j % X_SLOTS, pl.ds(0, n)], x_sems.at[j % X_SLOTS])

    def y_copy(j):
        n = pl.multiple_of(valid_ref[j] * ROW_TILE, ROW_TILE)
        dst = ys_hbm.at[pl.ds(pl.multiple_of(j * blk, blk), n)]
        return pltpu.make_async_copy(y_buf.at[j % Y_SLOTS, pl.ds(0, n)], dst, y_sems.at[j % Y_SLOTS])

    @pl.when(i == 0)
    def _prime():
        x_buf[...] = jnp.zeros_like(x_buf)
        for j in range(X_SLOTS - 1):
            @pl.when(j < n_act)
            def _():
                x_copy(j).start()

    @pl.when(i + (X_SLOTS - 1) < n_act)
    def _prefetch():
        x_copy(i + (X_SLOTS - 1)).start()

    def weight_copies(e, slot):
        return (pltpu.make_async_copy(wg_hbm.at[e], wg_buf.at[slot], sems.at[slot]),
                pltpu.make_async_copy(wu_hbm.at[e], wu_buf.at[slot], sems.at[slot]),
                pltpu.make_async_copy(wd_hbm.at[e], wd_buf.at[slot], sems.at[slot]))

    @pl.when((i < n_act_ref[0]) & (first_ref[i] == 1))
    def _new_expert():
        slot = slot_ref[i]

        @pl.when(i == 0)
        def _():
            for c in weight_copies(blk_e_ref[0], 0):
                c.start()

        for c in weight_copies(blk_e_ref[i], slot):
            c.wait()

        @pl.when(next_e_ref[i] >= 0)
        def _():
            for c in weight_copies(next_e_ref[i], 1 - slot):
                c.start()

        wg_bf[...] = wg_buf[slot].astype(bf16)
        wu_bf[...] = wu_buf[slot].astype(bf16)
        wd_bf[...] = wd_buf[slot].astype(bf16)

    @pl.when(i < n_act)
    def _compute():
        x_copy(i).wait()
        x = x_buf[i % X_SLOTS].astype(bf16)
        g = _dot(x, wg_bf[...])
        u = _dot(x, wu_bf[...])
        a = (_silu(g) * u).astype(bf16)
        y = _dot(a, wd_bf[...])

        @pl.when(i >= Y_SLOTS)
        def _():
            y_copy(i - Y_SLOTS).wait()

        _store_rows(y_buf, y, lead=(i % Y_SLOTS,))
        y_copy(i).start()

    @pl.when(i == n_act - 1)
    def _drain():
        for d in range(Y_SLOTS):
            @pl.when(i - d >= 0)
            def _():
                y_copy(i - d).wait()


def _experts(xs, blk_e, n_act, row_start, cnt, w_gate, w_up, w_down):
    D = D_MODEL
    BM = EXPERT_ROWS
    F = EXPERT_DIM
    n_blocks = xs.shape[0] // BM
    blk_in_expert = jnp.arange(n_blocks, dtype=jnp.int32) - row_start[blk_e] // BM
    valid = jnp.clip(cnt[blk_e] - blk_in_expert * BM, 0, BM).astype(jnp.int32)
    ids = jnp.arange(n_blocks, dtype=jnp.int32)
    active = ids < n_act[0]
    first = active & ((ids == 0) | (blk_e != jnp.roll(blk_e, 1)))
    slot = ((jnp.cumsum(first.astype(jnp.int32)) - 1) % 2).astype(jnp.int32)
    first_pos = jnp.where(first, ids, n_blocks)
    later_first = lax.cummin(jnp.concatenate([first_pos[1:], jnp.full((1,), n_blocks, jnp.int32)]), reverse=True)
    next_e = jnp.where(later_first < n_blocks, blk_e[jnp.minimum(later_first, n_blocks - 1)], -1).astype(jnp.int32)

    grid_spec = pltpu.PrefetchScalarGridSpec(
        num_scalar_prefetch=6,
        grid=(n_blocks,),
        in_specs=[pl.BlockSpec(memory_space=pl.ANY)] * 4,
        out_specs=pl.BlockSpec(memory_space=pl.ANY),
        scratch_shapes=[
            pltpu.VMEM((X_SLOTS, BM, D), jnp.float32),
            pltpu.VMEM((Y_SLOTS, BM * ROW_TILE, LANES), jnp.float32),
            pltpu.VMEM((2, D, F), jnp.float32), pltpu.VMEM((2, D, F), jnp.float32),
            pltpu.VMEM((2, F, D), jnp.float32),
            pltpu.VMEM((D, F), jnp.bfloat16), pltpu.VMEM((D, F), jnp.bfloat16),
            pltpu.VMEM((F, D), jnp.bfloat16),
            pltpu.SemaphoreType.DMA((2,)), pltpu.SemaphoreType.DMA((X_SLOTS,)),
            pltpu.SemaphoreType.DMA((Y_SLOTS,)),
        ],
    )
    return pl.pallas_call(
        _experts_kernel,
        grid_spec=grid_spec,
        out_shape=jax.ShapeDtypeStruct((xs.shape[0] * ROW_TILE, LANES), jnp.float32),
        compiler_params=pltpu.CompilerParams(
            dimension_semantics=("arbitrary",), vmem_limit_bytes=VMEM_LIMIT_BYTES),
    )(blk_e, first.astype(jnp.int32), slot, next_e, valid, n_act, xs, w_gate, w_up, w_down)


def _finish_kernel(h_ref, part_ref, w_ref, *rest):
    slabs = rest[:SC_COMBINE_SLOTS]
    g_ref, b_ref, out_ref = rest[SC_COMBINE_SLOTS:]
    R = h_ref.shape[0]
    w = w_ref[...]
    ffn = part_ref[...]
    for k in range(SC_COMBINE_SLOTS):
        ffn = ffn + _load_rows(slabs[k], R) * w[:, k:k + 1]
    out_ref[...] = _layer_norm(DEEPNORM_ALPHA * h_ref[...] + ffn, g_ref[...], b_ref[...])


def _finish(h2, partial, top_w, gathered, ln_g, ln_b):
    T, D = h2.shape
    R = FINISH_ROWS
    rows = pl.BlockSpec((R, D), lambda i: (i, 0))
    vec = pl.BlockSpec((1, D), lambda i: (0, 0))
    slab = lambda k: pl.BlockSpec((R * ROW_TILE, LANES), lambda i: (k * (T // R) + i, 0))
    return pl.pallas_call(
        _finish_kernel,
        grid=(T // R,),
        in_specs=[rows, rows, pl.BlockSpec((R, TOP_K), lambda i: (i, 0))]
        + [slab(k) for k in range(SC_COMBINE_SLOTS)] + [vec, vec],
        out_specs=rows,
        out_shape=jax.ShapeDtypeStruct((T, D), jnp.float32),
        compiler_params=pltpu.CompilerParams(
            dimension_semantics=("arbitrary",), vmem_limit_bytes=VMEM_LIMIT_BYTES),
    )(h2, partial, top_w, *([gathered] * SC_COMBINE_SLOTS), ln_g.reshape(1, D), ln_b.reshape(1, D))


SC_COMBINE_SLOTS = 4


def _combine_kernel(pos_ref, pos_next_ref, h_ref, w_ref, ys_ref, wsg_ref, wsu_ref, wsd_ref,
                    out_ref, buf_ref, sems):
    R = h_ref.shape[0]
    i = pl.program_id(0)
    slot = i % 2

    def gather(p_ref, s):
        def issue(t, carry):
            for k in range(SC_COMBINE_SLOTS, TOP_K):
                src = p_ref[t * TOP_K + k]
                pltpu.make_async_copy(ys_ref.at[_row_tile(src)], buf_ref.at[s, k - SC_COMBINE_SLOTS, _row_tile(t)],
                                      sems.at[s]).start(priority=k % 2)
            return carry

        lax.fori_loop(0, R, issue, 0)

    @pl.when(i == 0)
    def _():
        gather(pos_ref, 0)

    @pl.when(i + 1 < pl.num_programs(0))
    def _():
        gather(pos_next_ref, 1 - slot)

    h = h_ref[...]
    hb = h.astype(jnp.bfloat16)
    act = (_silu(_dot(hb, wsg_ref[...])) * _dot(hb, wsu_ref[...])).astype(jnp.bfloat16)
    ffn = _dot(act, wsd_ref[...])
    pltpu.make_async_copy(buf_ref.at[slot], buf_ref.at[slot], sems.at[slot]).wait()
    w = w_ref[...]
    for k in range(SC_COMBINE_SLOTS, TOP_K):
        ffn = ffn + _load_rows(buf_ref, R, lead=(slot, k - SC_COMBINE_SLOTS)) * w[:, k:k + 1]
    out_ref[...] = ffn


def _combine(h2, pos, top_w, ys, ws_gate, ws_up, ws_down):
    T, D = h2.shape
    R = COMBINE_ROWS
    F = SHARED_DIM
    bf16 = jnp.bfloat16
    const = lambda shape: pl.BlockSpec(shape, lambda i: (0,) * len(shape))
    return pl.pallas_call(
        _combine_kernel,
        grid=(T // R,),
        in_specs=[
            pl.BlockSpec((R * TOP_K,), lambda i: (i,), memory_space=pltpu.SMEM),
            pl.BlockSpec((R * TOP_K,), lambda i: (jnp.minimum(i + 1, T // R - 1),), memory_space=pltpu.SMEM),
            pl.BlockSpec((R, D), lambda i: (i, 0)),
            pl.BlockSpec((R, TOP_K), lambda i: (i, 0)),
            pl.BlockSpec(memory_space=pl.ANY),
            const((D, F)), const((D, F)), const((F, D)),
        ],
        out_specs=pl.BlockSpec((R, D), lambda i: (i, 0)),
        scratch_shapes=[pltpu.VMEM((2, TOP_K - SC_COMBINE_SLOTS, R * ROW_TILE, LANES), jnp.float32),
                        pltpu.SemaphoreType.DMA((2,))],
        out_shape=jax.ShapeDtypeStruct((T, D), jnp.float32),
        compiler_params=pltpu.CompilerParams(
            dimension_semantics=("arbitrary",), vmem_limit_bytes=VMEM_LIMIT_BYTES),
    )(pos, pos, h2, top_w, ys, ws_gate.astype(bf16), ws_up.astype(bf16), ws_down.astype(bf16))


def _moe(h2, w_router, router_bias, w_gate, w_up, w_down, ws_gate, ws_up, ws_down, ln_g, ln_b):
    T = h2.shape[0]
    E = N_EXPERTS
    BM = EXPERT_ROWS
    e_idx, top_w, rank, counts = _router(h2, w_router, router_bias)
    cnt = counts[:, 0].astype(jnp.int32)
    nblk = (cnt + BM - 1) // BM
    blk_end = jnp.cumsum(nblk)
    row_start = ((blk_end - nblk) * BM).astype(jnp.int32)
    n_blocks = T * TOP_K // BM + E
    n_act = blk_end[-1:].astype(jnp.int32)
    blk_ids = jnp.minimum(jnp.arange(n_blocks, dtype=jnp.int32), n_act[0] - 1)
    blk_e = jnp.minimum(jnp.sum(blk_end[None, :] <= blk_ids[:, None], axis=1), E - 1).astype(jnp.int32)
    pos_kt = _positions(e_idx, rank, row_start).reshape(TOP_K, T)
    pos = pos_kt.T.reshape(T * TOP_K)
    xs = _dispatch_sc(h2, pos_kt, n_blocks * BM)
    ys = _experts(xs, blk_e, n_act, row_start, cnt, w_gate, w_up, w_down)
    gathered = _gather_rows_sc(ys, pos_kt[:SC_COMBINE_SLOTS])
    w_tk = top_w.T
    partial = _combine(h2, pos, w_tk, ys, ws_gate, ws_up, ws_down)
    return _finish(h2, partial, w_tk, gathered, ln_g, ln_b)


def kernel(x, ln_in_g, ln_in_b, w_in, w_out, rel_bias, attn_sinks, ln_mix_g, ln_mix_b, w_router,
           router_bias, w_gate, w_up, w_down, ws_gate, ws_up, ws_down, ln_ffn_g, ln_ffn_b):
    B, S, D = x.shape
    h = _mixer(x, ln_in_g, ln_in_b, w_in[0], w_out[0], rel_bias, attn_sinks[0], ln_mix_g[0], ln_mix_b[0])
    out = _moe(h, w_router[0], router_bias[0], w_gate[0], w_up[0], w_down[0],
               ws_gate[0], ws_up[0], ws_down[0], ln_ffn_g[0], ln_ffn_b[0])
    return out.reshape(B, S, D)
```

```python
import functools
import math

import jax
import jax.numpy as jnp
from jax import lax
from jax.experimental import pallas as pl
from jax.experimental.pallas import tpu as pltpu
from jax.experimental.pallas import tpu_sc as plsc

D_MODEL = 1024
DEPTH = 1
RET_HEADS = 4
RET_QK_DIM = 64
RET_V_DIM = 128
RET_CHUNK = 128
RET_WIDTH = RET_HEADS * RET_V_DIM
ROPE_BASE = 10000.0
SWA_HEADS = 8
SWA_KV_HEADS = 2
SWA_GROUP = SWA_HEADS // SWA_KV_HEADS
SWA_HEAD_DIM = 64
SWA_WINDOW = 128
SWA_WIDTH = SWA_HEADS * SWA_HEAD_DIM
MIX_WIDTH = RET_WIDTH + SWA_WIDTH
RQK = RET_HEADS * RET_QK_DIM
SKV = SWA_KV_HEADS * SWA_HEAD_DIM
IN_SIZES = (RQK, RQK, RET_WIDTH, RET_WIDTH, SWA_WIDTH, SKV, SKV)
IN_OFFS = tuple(sum(IN_SIZES[:i]) for i in range(len(IN_SIZES)))
IN_WIDTH = sum(IN_SIZES)
REL_BUCKETS = 32
REL_MAX_DIST = 128
N_EXPERTS = 256
TOP_K = 8
N_GROUPS = 8
GROUP_SIZE = N_EXPERTS // N_GROUPS
TOPK_GROUPS = 4
EXPERT_DIM = 256
SHARED_DIM = 256
ROUTED_SCALE = 2.5
LN_EPS = 1e-5
GN_EPS = 1e-6
DEEPNORM_ALPHA = (2 * DEPTH) ** 0.25
MASK_VALUE = -1e30

VMEM_LIMIT_BYTES = 56 * 1024 * 1024

MIX_ROWS = 256
ROUTE_ROWS = 256
DISPATCH_ROWS = 256
EXPERT_ROWS = 256
COMBINE_ROWS = 128
FINISH_ROWS = 512


def _layer_norm(x, g, b):
    mu = jnp.mean(x, axis=-1, keepdims=True)
    xc = x - mu
    var = jnp.mean(xc * xc, axis=-1, keepdims=True)
    return xc * lax.rsqrt(var + LN_EPS) * g + b


def _dot(a, b):
    return jnp.dot(a, b, preferred_element_type=jnp.float32)


def _dot_nt(a, b):
    return lax.dot_general(a, b, (((1,), (1,)), ((), ())), preferred_element_type=jnp.float32)


def _dot_tn(a, b):
    return lax.dot_general(a, b, (((0,), (0,)), ((), ())), preferred_element_type=jnp.float32)


def _silu(x):
    return x * (1.0 / (1.0 + jnp.exp(-x)))


LANES = 128
ROW_TILE = D_MODEL // LANES


def _load_rows(ref, n_rows, lead=()):
    return jnp.concatenate([ref[lead + (pl.ds(s, n_rows, stride=ROW_TILE), slice(None))]
                            for s in range(ROW_TILE)], axis=1)


def _store_rows(ref, val, lead=()):
    n_rows = val.shape[0]
    for s in range(ROW_TILE):
        ref[lead + (pl.ds(s, n_rows, stride=ROW_TILE), slice(None))] = val[:, s * LANES:(s + 1) * LANES]


def _row_tile(r):
    return pl.ds(pl.multiple_of(r * ROW_TILE, ROW_TILE), ROW_TILE)


def _swap_halves(x):
    n = x.shape[-1]
    half = RET_QK_DIM // 2
    lane = lax.broadcasted_iota(jnp.int32, x.shape, 1)
    from_right = pltpu.roll(x, n - half, axis=1)
    from_left = pltpu.roll(x, half, axis=1)
    return jnp.where((lane % RET_QK_DIM) < half, from_right, from_left)


def _mixer_kernel(rel_bias_ref, x_ref, g_in_ref, b_in_ref, w_in_ref, w_out_ref, rot_ref, decay_ref,
                  zeta_ref, xi_ref, cdecay_ref, bucket_ref, sink_ref, g_mix_ref, b_mix_ref,
                  h2_ref, state_ref, kprev_ref, vprev_ref, bias_ref):
    b_id = pl.program_id(0)
    c_id = pl.program_id(1)
    W = SWA_WINDOW

    @pl.when((b_id == 0) & (c_id == 0))
    def _build_bias():
        bucket = bucket_ref[...]
        for h in range(SWA_HEADS):
            acc = jnp.full((2 * W, W), MASK_VALUE, jnp.float32)
            for b in range(REL_BUCKETS):
                acc = jnp.where(bucket == b, rel_bias_ref[b, h], acc)
            kh, g = divmod(h, SWA_GROUP)
            bias_ref[kh, :, g * W:(g + 1) * W] = acc

    @pl.when(c_id == 0)
    def _reset():
        state_ref[...] = jnp.zeros_like(state_ref)
        kprev_ref[...] = jnp.zeros_like(kprev_ref)
        vprev_ref[...] = jnp.zeros_like(vprev_ref)

    h = _layer_norm(x_ref[...], g_in_ref[...], b_in_ref[...])
    proj = _dot(h.astype(jnp.bfloat16), w_in_ref[...])

    o_q, o_k, o_v, o_g, o_sq, o_sk, o_sv = IN_OFFS
    cos_t = rot_ref[:, :RQK]
    sin_t = rot_ref[:, RQK:]
    q_all = proj[:, o_q:o_q + RQK]
    k_all = proj[:, o_k:o_k + RQK]
    q_rot = q_all * cos_t + _swap_halves(q_all) * sin_t
    k_rot = (k_all * cos_t + _swap_halves(k_all) * sin_t) * (RET_QK_DIM ** -0.5)

    n_sub = x_ref.shape[0] // RET_CHUNK
    states = [state_ref[hh] for hh in range(RET_HEADS)]
    k_prev = kprev_ref[...]
    v_prev = vprev_ref[...]
    cat_rows = []
    for s in range(n_sub):
        r0 = s * RET_CHUNK
        rows = slice(r0, r0 + RET_CHUNK)
        pieces = []
        for hh in range(RET_HEADS):
            qk = slice(hh * RET_QK_DIM, (hh + 1) * RET_QK_DIM)
            vv = slice(o_v + hh * RET_V_DIM, o_v + (hh + 1) * RET_V_DIM)
            gg = slice(o_g + hh * RET_V_DIM, o_g + (hh + 1) * RET_V_DIM)
            q = q_rot[rows, qk].astype(jnp.bfloat16)
            k32 = k_rot[rows, qk]
            v = proj[rows, vv].astype(jnp.bfloat16)
            scores = _dot_nt(q, k32.astype(jnp.bfloat16)) * decay_ref[hh]
            intra = _dot(scores.astype(jnp.bfloat16), v)
            inter = _dot(q, states[hh].astype(jnp.bfloat16)) * xi_ref[hh]
            ret = intra + inter
            kz = (k32 * zeta_ref[hh]).astype(jnp.bfloat16)
            states[hh] = states[hh] * cdecay_ref[hh] + _dot_tn(kz, v)
            mu = jnp.mean(ret, axis=-1, keepdims=True)
            rc = ret - mu
            var = jnp.mean(rc * rc, axis=-1, keepdims=True)
            normed = rc * lax.rsqrt(var + GN_EPS)
            pieces.append((_silu(proj[rows, gg]) * normed).astype(jnp.bfloat16))
        k_cur = proj[rows, o_sk:o_sk + SKV].astype(jnp.bfloat16)
        v_cur = proj[rows, o_sv:o_sv + SKV].astype(jnp.bfloat16)
        for kh in range(SWA_KV_HEADS):
            kv = slice(kh * SWA_HEAD_DIM, (kh + 1) * SWA_HEAD_DIM)
            q4 = jnp.concatenate(
                [proj[rows, o_sq + (kh * SWA_GROUP + g) * SWA_HEAD_DIM:
                      o_sq + (kh * SWA_GROUP + g + 1) * SWA_HEAD_DIM] for g in range(SWA_GROUP)],
                axis=0) * (SWA_HEAD_DIM ** -0.5)
            kcat = jnp.concatenate([k_prev[:, kv], k_cur[:, kv]], axis=0)
            vcat = jnp.concatenate([v_prev[:, kv], v_cur[:, kv]], axis=0)
            logits = _dot_nt(kcat, q4.astype(jnp.bfloat16)) + bias_ref[kh]
            if s == 0:
                key = lax.broadcasted_iota(jnp.int32, logits.shape, 0)
                logits = logits + jnp.where((key < W) & (c_id == 0), MASK_VALUE, 0.0)
            sink = sink_ref[kh]
            m = jnp.maximum(jnp.max(logits, axis=0, keepdims=True), sink)
            p = jnp.exp(logits - m)
            den = jnp.sum(p, axis=0, keepdims=True) + jnp.exp(sink - m)
            probs = (p / den).astype(jnp.bfloat16)
            o4 = _dot_tn(vcat, probs)
            pieces.extend(o4[:, g * W:(g + 1) * W].T.astype(jnp.bfloat16) for g in range(SWA_GROUP))
        k_prev, v_prev = k_cur, v_cur
        cat_rows.append(jnp.concatenate(pieces, axis=1))
    for hh in range(RET_HEADS):
        state_ref[hh] = states[hh]
    kprev_ref[...] = k_prev
    vprev_ref[...] = v_prev

    mix = _dot(jnp.concatenate(cat_rows, axis=0), w_out_ref[...])
    h2_ref[...] = _layer_norm(DEEPNORM_ALPHA * h + mix, g_mix_ref[...], b_mix_ref[...])


def _t5_bucket(dist):
    n = jnp.maximum(dist, 0)
    max_exact = REL_BUCKETS // 2
    ratio = jnp.log(jnp.maximum(n, 1).astype(jnp.float32) / max_exact) / math.log(REL_MAX_DIST / max_exact)
    large = jnp.minimum(max_exact + (ratio * (REL_BUCKETS - max_exact)).astype(jnp.int32), REL_BUCKETS - 1)
    return jnp.where(n < max_exact, n, large)


def _mixer(x, ln_in_g, ln_in_b, w_in, w_out, rel_bias, sinks, ln_mix_g, ln_mix_b):
    B, S, D = x.shape
    R = MIX_ROWS
    C = RET_CHUNK
    W = SWA_WINDOW
    f32 = jnp.float32
    half = RET_QK_DIM // 2
    inv = ROPE_BASE ** (-jnp.arange(half, dtype=f32) / half)
    ang = jnp.arange(S, dtype=f32)[:, None] * inv[None, :]
    cos, sin = jnp.cos(ang), jnp.sin(ang)
    cos_t = jnp.tile(jnp.concatenate([cos, cos], axis=-1), (1, RET_HEADS))
    sin_t = jnp.tile(jnp.concatenate([-sin, sin], axis=-1), (1, RET_HEADS))
    rot = jnp.concatenate([cos_t, sin_t], axis=-1)
    log_gamma = jnp.log(1.0 - 2.0 ** (-5.0 - jnp.arange(RET_HEADS, dtype=f32)))
    idx = jnp.arange(C, dtype=f32)
    diff = idx[:, None] - idx[None, :]
    decay = jnp.where(diff[None] >= 0, jnp.exp(jnp.maximum(diff, 0.0)[None] * log_gamma[:, None, None]), 0.0)
    zeta = jnp.exp((C - 1.0 - idx)[None, :] * log_gamma[:, None])
    xi = jnp.exp((idx + 1.0)[None, :] * log_gamma[:, None])
    zeta_b = jnp.broadcast_to(zeta[:, :, None], (RET_HEADS, C, RET_QK_DIM))
    xi_b = jnp.broadcast_to(xi[:, :, None], (RET_HEADS, C, RET_V_DIM))
    cdecay = jnp.broadcast_to(jnp.exp(C * log_gamma)[:, None, None], (RET_HEADS, RET_QK_DIM, RET_V_DIM))
    i = jnp.arange(W)
    j = jnp.arange(2 * W)
    dist = i[:, None] + W - j[None, :]
    bucket = jnp.where((dist >= 0) & (dist < W), _t5_bucket(dist), -1).astype(jnp.int32).T
    sink_row = jnp.repeat(sinks.astype(f32), W).reshape(SWA_KV_HEADS, 1, SWA_GROUP * W)

    const = lambda shape: pl.BlockSpec(shape, lambda b, c, *_: (0,) * len(shape))
    grid_spec = pltpu.PrefetchScalarGridSpec(
        num_scalar_prefetch=1,
        grid=(B, S // R),
        in_specs=[
            pl.BlockSpec((None, R, D), lambda b, c, *_: (b, c, 0)),
            const((1, D)), const((1, D)),
            const((D, IN_WIDTH)), const((MIX_WIDTH, D)),
            pl.BlockSpec((R, 2 * RQK), lambda b, c, *_: (c, 0)),
            const((RET_HEADS, C, C)), const((RET_HEADS, C, RET_QK_DIM)), const((RET_HEADS, C, RET_V_DIM)),
            const((RET_HEADS, RET_QK_DIM, RET_V_DIM)),
            const((2 * W, W)), const((SWA_KV_HEADS, 1, SWA_GROUP * W)),
            const((1, D)), const((1, D)),
        ],
        out_specs=pl.BlockSpec((R, D), lambda b, c, *_: (b * (S // R) + c, 0)),
        scratch_shapes=[
            pltpu.VMEM((RET_HEADS, RET_QK_DIM, RET_V_DIM), f32),
            pltpu.VMEM((W, SKV), jnp.bfloat16),
            pltpu.VMEM((W, SKV), jnp.bfloat16),
            pltpu.VMEM((SWA_KV_HEADS, 2 * W, SWA_GROUP * W), f32),
        ],
    )
    return pl.pallas_call(
        _mixer_kernel,
        grid_spec=grid_spec,
        out_shape=jax.ShapeDtypeStruct((B * S, D), f32),
        compiler_params=pltpu.CompilerParams(
            dimension_semantics=("arbitrary", "arbitrary"), vmem_limit_bytes=VMEM_LIMIT_BYTES),
    )(rel_bias.astype(f32), x, ln_in_g.reshape(1, D), ln_in_b.reshape(1, D),
      w_in.astype(jnp.bfloat16), w_out.astype(jnp.bfloat16), rot, decay, zeta_b, xi_b, cdecay,
      bucket, sink_row, ln_mix_g.reshape(1, D), ln_mix_b.reshape(1, D))


def _router_kernel(h_ref, wr_ref, rb_ref, e_ref, w_ref, rk_ref, cnt_ref, run_ref):
    f32 = jnp.float32
    R = h_ref.shape[0]
    E = N_EXPERTS
    neg = -jnp.inf

    @pl.when(pl.program_id(0) == 0)
    def _init():
        run_ref[...] = jnp.zeros_like(run_ref)

    logits = lax.dot_general(wr_ref[...], h_ref[...], (((1,), (1,)), ((), ())),
                             precision=lax.Precision.HIGHEST, preferred_element_type=f32)
    scores = 1.0 / (1.0 + jnp.exp(-logits))
    choice = scores + rb_ref[...]
    eid = lax.broadcasted_iota(jnp.int32, (E, R), 0)

    def first_argmax(vals, ids, none):
        m = jnp.max(vals, axis=0, keepdims=True)
        idx = jnp.min(jnp.where(vals == m, ids, none), axis=0, keepdims=True)
        return m, idx

    gid = lax.broadcasted_iota(jnp.int32, (GROUP_SIZE, R), 0)
    groups, gscore = [], []
    for g in range(N_GROUPS):
        vals = choice[g * GROUP_SIZE:(g + 1) * GROUP_SIZE]
        m1, i1 = first_argmax(vals, gid, GROUP_SIZE)
        m2 = jnp.max(jnp.where(gid == i1, neg, vals), axis=0, keepdims=True)
        groups.append(vals)
        gscore.append(m1 + m2)
    kept = []
    for g in range(N_GROUPS):
        beaten = jnp.zeros((1, R), f32)
        for g2 in range(N_GROUPS):
            if g2 == g:
                continue
            ahead = (gscore[g2] > gscore[g]) | (gscore[g2] == gscore[g]) if g2 < g else gscore[g2] > gscore[g]
            beaten = beaten + jnp.where(ahead, 1.0, 0.0)
        kept.append(jnp.where(beaten < TOPK_GROUPS, groups[g], neg))
    masked = jnp.concatenate(kept, axis=0)

    idxs, wts = [], []
    picked = jnp.zeros((E, R), f32)
    for _ in range(TOP_K):
        _, idx = first_argmax(masked, eid, E)
        hit = eid == idx
        idxs.append(idx)
        wts.append(jnp.sum(jnp.where(hit, scores, 0.0), axis=0, keepdims=True))
        masked = jnp.where(hit, neg, masked)
        picked = jnp.where(hit, 1.0, picked)
    wsum = wts[0]
    for k in range(1, TOP_K):
        wsum = wsum + wts[k]

    row = lax.broadcasted_iota(jnp.int32, (R, R), 0)
    col = lax.broadcasted_iota(jnp.int32, (R, R), 1)
    earlier = jnp.where(row < col, 1.0, 0.0).astype(jnp.bfloat16)
    picked_bf = picked.astype(jnp.bfloat16)
    run = run_ref[...]
    before = _dot(picked_bf, earlier) + jnp.concatenate([run] * (R // LANES), axis=1)
    sub_k = lax.broadcasted_iota(jnp.int32, (TOP_K, R), 0)
    e_out = jnp.zeros((TOP_K, R), jnp.int32)
    w_out = jnp.zeros((TOP_K, R), f32)
    rk_out = jnp.zeros((TOP_K, R), jnp.int32)
    for k in range(TOP_K):
        rank_k = jnp.sum(jnp.where(eid == idxs[k], before, 0.0), axis=0, keepdims=True)
        e_out = jnp.where(sub_k == k, idxs[k], e_out)
        w_out = jnp.where(sub_k == k, wts[k] / wsum * ROUTED_SCALE, w_out)
        rk_out = jnp.where(sub_k == k, rank_k.astype(jnp.int32), rk_out)
    e_ref[...] = e_out
    w_ref[...] = w_out
    rk_ref[...] = rk_out
    run_ref[...] = run + _dot(picked_bf, jnp.ones((R, LANES), jnp.bfloat16))
    cnt_ref[...] = run_ref[...]


def _router(h2, w_router, router_bias):
    T, D = h2.shape
    R = ROUTE_ROWS
    E = N_EXPERTS
    return pl.pallas_call(
        _router_kernel,
        grid=(T // R,),
        in_specs=[
            pl.BlockSpec((R, D), lambda i: (i, 0)),
            pl.BlockSpec((E, D), lambda i: (0, 0)),
            pl.BlockSpec((E, R), lambda i: (0, 0)),
        ],
        out_specs=[
            pl.BlockSpec((TOP_K, R), lambda i: (0, i)),
            pl.BlockSpec((TOP_K, R), lambda i: (0, i)),
            pl.BlockSpec((TOP_K, R), lambda i: (0, i)),
            pl.BlockSpec((E, LANES), lambda i: (0, 0)),
        ],
        out_shape=[
            jax.ShapeDtypeStruct((TOP_K, T), jnp.int32),
            jax.ShapeDtypeStruct((TOP_K, T), jnp.float32),
            jax.ShapeDtypeStruct((TOP_K, T), jnp.int32),
            jax.ShapeDtypeStruct((E, LANES), jnp.float32),
        ],
        scratch_shapes=[pltpu.VMEM((E, LANES), jnp.float32)],
        compiler_params=pltpu.CompilerParams(
            dimension_semantics=("arbitrary",), vmem_limit_bytes=VMEM_LIMIT_BYTES),
    )(h2, w_router.T, jnp.broadcast_to(router_bias.astype(jnp.float32)[:, None], (E, R)))


def _positions_kernel(row_start_ref, e_ref, rk_ref, pos_ref):
    e = e_ref[...]
    rk = rk_ref[...]

    def per_expert(i, pos):
        return jnp.where(e == i, rk + row_start_ref[i], pos)

    pos_ref[...] = lax.fori_loop(0, N_EXPERTS, per_expert, jnp.zeros_like(rk))


def _positions(e_idx, rank, row_start):
    n = e_idx.size
    shape = (n // LANES, LANES)
    grid_spec = pltpu.PrefetchScalarGridSpec(
        num_scalar_prefetch=1,
        grid=(1,),
        in_specs=[pl.BlockSpec(shape, lambda i, *_: (0, 0)), pl.BlockSpec(shape, lambda i, *_: (0, 0))],
        out_specs=pl.BlockSpec(shape, lambda i, *_: (0, 0)),
    )
    pos = pl.pallas_call(
        _positions_kernel,
        grid_spec=grid_spec,
        out_shape=jax.ShapeDtypeStruct(shape, jnp.int32),
    )(row_start, e_idx.reshape(shape), rank.reshape(shape))
    return pos.reshape(n)


def _dispatch_kernel(row_start_ref, cnt_ref, n_act_ref, pos_ref, h_ref, xs_ref, zero_ref, sem, zsem,
                     *, n_tokens):
    R = h_ref.shape[0] // ROW_TILE
    BM = EXPERT_ROWS
    n_blocks = xs_ref.shape[0] // (BM * ROW_TILE)
    n_pad_units = n_blocks - n_tokens * TOP_K // BM

    @pl.when(pl.program_id(0) == 0)
    def _zero_padding():
        zero_ref[...] = jnp.zeros_like(zero_ref)

        def expert_tail(e, carry):
            n_tail = pl.multiple_of(((BM - cnt_ref[e] % BM) % BM) * ROW_TILE, ROW_TILE)

            @pl.when(n_tail > 0)
            def _():
                dst = pl.multiple_of((row_start_ref[e] + cnt_ref[e]) * ROW_TILE, ROW_TILE)
                pltpu.make_async_copy(zero_ref.at[pl.ds(0, n_tail)], xs_ref.at[pl.ds(dst, n_tail)], zsem).start()
            return carry

        lax.fori_loop(0, N_EXPERTS, expert_tail, 0)

        def idle_block(i, carry):
            dst = pl.multiple_of(i * (BM * ROW_TILE), BM * ROW_TILE)
            pltpu.make_async_copy(zero_ref, xs_ref.at[pl.ds(dst, BM * ROW_TILE)], zsem).start()
            return carry

        lax.fori_loop(n_act_ref[0], n_blocks, idle_block, 0)

        def drain(i, carry):
            pltpu.make_async_copy(zero_ref, xs_ref.at[pl.ds(0, BM * ROW_TILE)], zsem).wait()
            return carry

        lax.fori_loop(0, n_pad_units, drain, 0)

    def issue(t, carry):
        src = h_ref.at[_row_tile(t)]
        for k in range(TOP_K):
            dest = pos_ref[t * TOP_K + k]
            pltpu.make_async_copy(src, xs_ref.at[_row_tile(dest)], sem).start(priority=k % 2)
        return carry

    lax.fori_loop(0, R, issue, 0)
    n = R * TOP_K * ROW_TILE
    pltpu.make_async_copy(xs_ref.at[pl.ds(0, n)], xs_ref.at[pl.ds(0, n)], sem).wait()


def _dispatch(h2, pos, row_start, cnt, n_act, n_rows):
    T = h2.shape[0] // ROW_TILE
    R = DISPATCH_ROWS
    grid_spec = pltpu.PrefetchScalarGridSpec(
        num_scalar_prefetch=3,
        grid=(T // R,),
        in_specs=[
            pl.BlockSpec((R * TOP_K,), lambda i, *_: (i,), memory_space=pltpu.SMEM),
            pl.BlockSpec((R * ROW_TILE, LANES), lambda i, *_: (i, 0)),
        ],
        out_specs=pl.BlockSpec(memory_space=pl.ANY),
        scratch_shapes=[pltpu.VMEM((EXPERT_ROWS * ROW_TILE, LANES), jnp.float32),
                        pltpu.SemaphoreType.DMA(()), pltpu.SemaphoreType.DMA(())],
    )
    return pl.pallas_call(
        functools.partial(_dispatch_kernel, n_tokens=T),
        grid_spec=grid_spec,
        out_shape=jax.ShapeDtypeStruct((n_rows * ROW_TILE, LANES), jnp.float32),
        compiler_params=pltpu.CompilerParams(
            dimension_semantics=("arbitrary",), vmem_limit_bytes=VMEM_LIMIT_BYTES),
    )(row_start, cnt, n_act, pos, h2)


SC_WINDOW = 32


def _dispatch_sc(h2, pos_kt, n_rows):
    T, D = h2.shape
    W = SC_WINDOW
    idx = _window_indices(pos_kt, W)
    idx_rows = TOP_K * W // LANES
    mesh = plsc.VectorSubcoreMesh(core_axis_name="core", subcore_axis_name="subcore")

    @pl.kernel(out_type=jax.ShapeDtypeStruct((n_rows, D), h2.dtype), mesh=mesh, scratch_types=[])
    def scatter_rows(x_hbm, i_hbm, o_hbm):
        def body(x_vmem, i_vmem):
            for k in range(TOP_K):
                r, q = divmod(k * W, LANES)
                pltpu.sync_copy(x_vmem, o_hbm.at[i_vmem.at[r, pl.ds(q, W)]])

        pltpu.emit_pipeline(
            body,
            grid=(T // W,),
            in_specs=[pl.BlockSpec((W, D), lambda i: (i, 0)),
                      pl.BlockSpec((idx_rows, LANES), lambda i: (i, 0))],
            out_specs=[],
            core_axis_name=("core", "subcore"),
            dimension_semantics=(pltpu.PARALLEL,),
        )(x_hbm, i_hbm)

    return scatter_rows(h2, idx)


def _window_indices(pos_kt, window):
    K, T = pos_kt.shape
    return pos_kt.reshape(K, T // window, window).transpose(1, 0, 2).reshape(T * K // LANES, LANES)


def _gather_rows_sc(ys, pos_kt):
    n = pos_kt.size
    W = SC_WINDOW
    ys3 = ys.reshape(ys.shape[0] // ROW_TILE, ROW_TILE, LANES)
    idx = jnp.pad(pos_kt.reshape(n // W, W), ((0, 0), (0, LANES - W)))
    mesh = plsc.VectorSubcoreMesh(core_axis_name="core", subcore_axis_name="subcore")

    @pl.kernel(out_type=jax.ShapeDtypeStruct((n, ROW_TILE, LANES), ys.dtype), mesh=mesh, scratch_types=[])
    def gather_rows(y_hbm, i_hbm, o_hbm):
        def body(i_vmem, o_vmem):
            pltpu.sync_copy(y_hbm.at[i_vmem.at[0, pl.ds(0, W)]], o_vmem)

        pltpu.emit_pipeline(
            body,
            grid=(n // W,),
            in_specs=[pl.BlockSpec((1, LANES), lambda i: (i, 0))],
            out_specs=[pl.BlockSpec((W, ROW_TILE, LANES), lambda i: (i, 0, 0))],
            core_axis_name=("core", "subcore"),
            dimension_semantics=(pltpu.PARALLEL,),
        )(i_hbm, o_hbm)

    return gather_rows(ys3, idx).reshape(n * ROW_TILE, LANES)


X_SLOTS = 4
Y_SLOTS = 3


def _experts_kernel(blk_e_ref, first_ref, slot_ref, next_e_ref, valid_ref, n_act_ref, xs_hbm, wg_hbm,
                    wu_hbm, wd_hbm, ys_hbm, x_buf, y_buf, wg_buf, wu_buf, wd_buf, wg_bf, wu_bf, wd_bf,
                    sems, x_sems, y_sems):
    i = pl.program_id(0)
    n_act = n_act_ref[0]
    bf16 = jnp.bfloat16
    blk = EXPERT_ROWS * ROW_TILE

    def x_copy(j):
        n = pl.multiple_of((valid_ref[j] + 7) // 8 * 8, 8)
        src = xs_hbm.at[pl.ds(pl.multiple_of(j * EXPERT_ROWS, EXPERT_ROWS), n)]
        return pltpu.make_async_copy(src, x_buf.at[j % X_SLOTS, pl.ds(0, n)], x_sems.at[j % X_SLOTS])

    def y_copy(j):
        n = pl.multiple_of(valid_ref[j] * ROW_TILE, ROW_TILE)
        dst = ys_hbm.at[pl.ds(pl.multiple_of(j * blk, blk), n)]
        return pltpu.make_async_copy(y_buf.at[j % Y_SLOTS, pl.ds(0, n)], dst, y_sems.at[j % Y_SLOTS])

    @pl.when(i == 0)
    def _prime():
        x_buf[...] = jnp.zeros_like(x_buf)
        for j in range(X_SLOTS - 1):
            @pl.when(j < n_act)
            def _():
                x_copy(j).start()

    @pl.when(i + (X_SLOTS - 1) < n_act)
    def _prefetch():
        x_copy(i + (X_SLOTS - 1)).start()

    def weight_copies(e, slot):
        return (pltpu.make_async_copy(wg_hbm.at[e], wg_buf.at[slot], sems.at[slot]),
                pltpu.make_async_copy(wu_hbm.at[e], wu_buf.at[slot], sems.at[slot]),
                pltpu.make_async_copy(wd_hbm.at[e], wd_buf.at[slot], sems.at[slot]))

    @pl.when((i < n_act_ref[0]) & (first_ref[i] == 1))
    def _new_expert():
        slot = slot_ref[i]

        @pl.when(i == 0)
        def _():
            for c in weight_copies(blk_e_ref[0], 0):
                c.start()

        for c in weight_copies(blk_e_ref[i], slot):
            c.wait()

        @pl.when(next_e_ref[i] >= 0)
        def _():
            for c in weight_copies(next_e_ref[i], 1 - slot):
                c.start()

        wg_bf[...] = wg_buf[slot].astype(bf16)
        wu_bf[...] = wu_buf[slot].astype(bf16)
        wd_bf[...] = wd_buf[slot].astype(bf16)

    @pl.when(i < n_act)
    def _compute():
        x_copy(i).wait()
        x = x_buf[i % X_SLOTS].astype(bf16)
        g = _dot(x, wg_bf[...])
        u = _dot(x, wu_bf[...])
        a = (_silu(g) * u).astype(bf16)
        y = _dot(a, wd_bf[...])

        @pl.when(i >= Y_SLOTS)
        def _():
            y_copy(i - Y_SLOTS).wait()

        _store_rows(y_buf, y, lead=(i % Y_SLOTS,))
        y_copy(i).start()

    @pl.when(i == n_act - 1)
    def _drain():
        for d in range(Y_SLOTS):
            @pl.when(i - d >= 0)
            def _():
                y_copy(i - d).wait()


def _experts(xs, blk_e, n_act, row_start, cnt, w_gate, w_up, w_down):
    D = D_MODEL
    BM = EXPERT_ROWS
    F = EXPERT_DIM
    n_blocks = xs.shape[0] // BM
    blk_in_expert = jnp.arange(n_blocks, dtype=jnp.int32) - row_start[blk_e] // BM
    valid = jnp.clip(cnt[blk_e] - blk_in_expert * BM, 0, BM).astype(jnp.int32)
    ids = jnp.arange(n_blocks, dtype=jnp.int32)
    active = ids < n_act[0]
    first = active & ((ids == 0) | (blk_e != jnp.roll(blk_e, 1)))
    slot = ((jnp.cumsum(first.astype(jnp.int32)) - 1) % 2).astype(jnp.int32)
    first_pos = jnp.where(first, ids, n_blocks)
    later_first = lax.cummin(jnp.concatenate([first_pos[1:], jnp.full((1,), n_blocks, jnp.int32)]), reverse=True)
    next_e = jnp.where(later_first < n_blocks, blk_e[jnp.minimum(later_first, n_blocks - 1)], -1).astype(jnp.int32)

    grid_spec = pltpu.PrefetchScalarGridSpec(
        num_scalar_prefetch=6,
        grid=(n_blocks,),
        in_specs=[pl.BlockSpec(memory_space=pl.ANY)] * 4,
        out_specs=pl.BlockSpec(memory_space=pl.ANY),
        scratch_shapes=[
            pltpu.VMEM((X_SLOTS, BM, D), jnp.float32),
            pltpu.VMEM((Y_SLOTS, BM * ROW_TILE, LANES), jnp.float32),
            pltpu.VMEM((2, D, F), jnp.float32), pltpu.VMEM((2, D, F), jnp.float32),
            pltpu.VMEM((2, F, D), jnp.float32),
            pltpu.VMEM((D, F), jnp.bfloat16), pltpu.VMEM((D, F), jnp.bfloat16),
            pltpu.VMEM((F, D), jnp.bfloat16),
            pltpu.SemaphoreType.DMA((2,)), pltpu.SemaphoreType.DMA((X_SLOTS,)),
            pltpu.SemaphoreType.DMA((Y_SLOTS,)),
        ],
    )
    return pl.pallas_call(
        _experts_kernel,
        grid_spec=grid_spec,
        out_shape=jax.ShapeDtypeStruct((xs.shape[0] * ROW_TILE, LANES), jnp.float32),
        compiler_params=pltpu.CompilerParams(
            dimension_semantics=("arbitrary",), vmem_limit_bytes=VMEM_LIMIT_BYTES),
    )(blk_e, first.astype(jnp.int32), slot, next_e, valid, n_act, xs, w_gate, w_up, w_down)


def _finish_kernel(h_ref, part_ref, w_ref, *rest):
    slabs = rest[:SC_COMBINE_SLOTS]
    g_ref, b_ref, out_ref = rest[SC_COMBINE_SLOTS:]
    R = h_ref.shape[0]
    w = w_ref[...]
    ffn = part_ref[...]
    for k in range(SC_COMBINE_SLOTS):
        ffn = ffn + _load_rows(slabs[k], R) * w[:, k:k + 1]
    out_ref[...] = _layer_norm(DEEPNORM_ALPHA * h_ref[...] + ffn, g_ref[...], b_ref[...])


def _finish(h2, partial, top_w, gathered, ln_g, ln_b):
    T, D = h2.shape
    R = FINISH_ROWS
    rows = pl.BlockSpec((R, D), lambda i: (i, 0))
    vec = pl.BlockSpec((1, D), lambda i: (0, 0))
    slab = lambda k: pl.BlockSpec((R * ROW_TILE, LANES), lambda i: (k * (T // R) + i, 0))
    return pl.pallas_call(
        _finish_kernel,
        grid=(T // R,),
        in_specs=[rows, rows, pl.BlockSpec((R, TOP_K), lambda i: (i, 0))]
        + [slab(k) for k in range(SC_COMBINE_SLOTS)] + [vec, vec],
        out_specs=rows,
        out_shape=jax.ShapeDtypeStruct((T, D), jnp.float32),
        compiler_params=pltpu.CompilerParams(
            dimension_semantics=("arbitrary",), vmem_limit_bytes=VMEM_LIMIT_BYTES),
    )(h2, partial, top_w, *([gathered] * SC_COMBINE_SLOTS), ln_g.reshape(1, D), ln_b.reshape(1, D))


SC_COMBINE_SLOTS = 6


def _combine_kernel(pos_ref, pos_next_ref, h_ref, w_ref, ys_ref, wsg_ref, wsu_ref, wsd_ref,
                    out_ref, buf_ref, sems):
    R = h_ref.shape[0]
    i = pl.program_id(0)
    slot = i % 2

    def gather(p_ref, s):
        def issue(t, carry):
            for k in range(SC_COMBINE_SLOTS, TOP_K):
                src = p_ref[t * TOP_K + k]
                pltpu.make_async_copy(ys_ref.at[_row_tile(src)], buf_ref.at[s, k - SC_COMBINE_SLOTS, _row_tile(t)],
                                      sems.at[s]).start(priority=k % 2)
            return carry

        lax.fori_loop(0, R, issue, 0)

    @pl.when(i == 0)
    def _():
        gather(pos_ref, 0)

    @pl.when(i + 1 < pl.num_programs(0))
    def _():
        gather(pos_next_ref, 1 - slot)

    h = h_ref[...]
    hb = h.astype(jnp.bfloat16)
    act = (_silu(_dot(hb, wsg_ref[...])) * _dot(hb, wsu_ref[...])).astype(jnp.bfloat16)
    ffn = _dot(act, wsd_ref[...])
    pltpu.make_async_copy(buf_ref.at[slot], buf_ref.at[slot], sems.at[slot]).wait()
    w = w_ref[...]
    for k in range(SC_COMBINE_SLOTS, TOP_K):
        ffn = ffn + _load_rows(buf_ref, R, lead=(slot, k - SC_COMBINE_SLOTS)) * w[:, k:k + 1]
    out_ref[...] = ffn


def _combine(h2, pos, top_w, ys, ws_gate, ws_up, ws_down):
    T, D = h2.shape
    R = COMBINE_ROWS
    F = SHARED_DIM
    bf16 = jnp.bfloat16
    const = lambda shape: pl.BlockSpec(shape, lambda i: (0,) * len(shape))
    return pl.pallas_call(
        _combine_kernel,
        grid=(T // R,),
        in_specs=[
            pl.BlockSpec((R * TOP_K,), lambda i: (i,), memory_space=pltpu.SMEM),
            pl.BlockSpec((R * TOP_K,), lambda i: (jnp.minimum(i + 1, T // R - 1),), memory_space=pltpu.SMEM),
            pl.BlockSpec((R, D), lambda i: (i, 0)),
            pl.BlockSpec((R, TOP_K), lambda i: (i, 0)),
            pl.BlockSpec(memory_space=pl.ANY),
            const((D, F)), const((D, F)), const((F, D)),
        ],
        out_specs=pl.BlockSpec((R, D), lambda i: (i, 0)),
        scratch_shapes=[pltpu.VMEM((2, TOP_K - SC_COMBINE_SLOTS, R * ROW_TILE, LANES), jnp.float32),
                        pltpu.SemaphoreType.DMA((2,))],
        out_shape=jax.ShapeDtypeStruct((T, D), jnp.float32),
        compiler_params=pltpu.CompilerParams(
            dimension_semantics=("arbitrary",), vmem_limit_bytes=VMEM_LIMIT_BYTES),
    )(pos, pos, h2, top_w, ys, ws_gate.astype(bf16), ws_up.astype(bf16), ws_down.astype(bf16))


def _moe(h2, w_router, router_bias, w_gate, w_up, w_down, ws_gate, ws_up, ws_down, ln_g, ln_b):
    T = h2.shape[0]
    E = N_EXPERTS
    BM = EXPERT_ROWS
    e_idx, top_w, rank, counts = _router(h2, w_router, router_bias)
    cnt = counts[:, 0].astype(jnp.int32)
    nblk = (cnt + BM - 1) // BM
    blk_end = jnp.cumsum(nblk)
    row_start = ((blk_end - nblk) * BM).astype(jnp.int32)
    n_blocks = T * TOP_K // BM + E
    n_act = blk_end[-1:].astype(jnp.int32)
    blk_ids = jnp.minimum(jnp.arange(n_blocks, dtype=jnp.int32), n_act[0] - 1)
    blk_e = jnp.minimum(jnp.sum(blk_end[None, :] <= blk_ids[:, None], axis=1), E - 1).astype(jnp.int32)
    pos_kt = _positions(e_idx, rank, row_start).reshape(TOP_K, T)
    pos = pos_kt.T.reshape(T * TOP_K)
    xs = _dispatch_sc(h2, pos_kt, n_blocks * BM)
    ys = _experts(xs, blk_e, n_act, row_start, cnt, w_gate, w_up, w_down)
    gathered = _gather_rows_sc(ys, pos_kt[:SC_COMBINE_SLOTS])
    w_tk = top_w.T
    partial = _combine(h2, pos, w_tk, ys, ws_gate, ws_up, ws_down)
    return _finish(h2, partial, w_tk, gathered, ln_g, ln_b)


def kernel(x, ln_in_g, ln_in_b, w_in, w_out, rel_bias, attn_sinks, ln_mix_g, ln_mix_b, w_router,
           router_bias, w_gate, w_up, w_down, ws_gate, ws_up, ws_down, ln_ffn_g, ln_ffn_b):
    B, S, D = x.shape
    h = _mixer(x, ln_in_g, ln_in_b, w_in[0], w_out[0], rel_bias, attn_sinks[0], ln_mix_g[0], ln_mix_b[0])
    out = _moe(h, w_router[0], router_bias[0], w_gate[0], w_up[0], w_down[0],
               ws_gate[0], ws_up[0], ws_down[0], ln_ffn_g[0], ln_ffn_b[0])
    return out.reshape(B, S, D)
```

```python
import functools
import math

import jax
import jax.numpy as jnp
from jax import lax
from jax.experimental import pallas as pl
from jax.experimental.pallas import tpu as pltpu
from jax.experimental.pallas import tpu_sc as plsc

D_MODEL = 1024
DEPTH = 1
RET_HEADS = 4
RET_QK_DIM = 64
RET_V_DIM = 128
RET_CHUNK = 128
RET_WIDTH = RET_HEADS * RET_V_DIM
ROPE_BASE = 10000.0
SWA_HEADS = 8
SWA_KV_HEADS = 2
SWA_GROUP = SWA_HEADS // SWA_KV_HEADS
SWA_HEAD_DIM = 64
SWA_WINDOW = 128
SWA_WIDTH = SWA_HEADS * SWA_HEAD_DIM
MIX_WIDTH = RET_WIDTH + SWA_WIDTH
RQK = RET_HEADS * RET_QK_DIM
SKV = SWA_KV_HEADS * SWA_HEAD_DIM
IN_SIZES = (RQK, RQK, RET_WIDTH, RET_WIDTH, SWA_WIDTH, SKV, SKV)
IN_OFFS = tuple(sum(IN_SIZES[:i]) for i in range(len(IN_SIZES)))
IN_WIDTH = sum(IN_SIZES)
REL_BUCKETS = 32
REL_MAX_DIST = 128
N_EXPERTS = 256
TOP_K = 8
N_GROUPS = 8
GROUP_SIZE = N_EXPERTS // N_GROUPS
TOPK_GROUPS = 4
EXPERT_DIM = 256
SHARED_DIM = 256
ROUTED_SCALE = 2.5
LN_EPS = 1e-5
GN_EPS = 1e-6
DEEPNORM_ALPHA = (2 * DEPTH) ** 0.25
MASK_VALUE = -1e30

VMEM_LIMIT_BYTES = 56 * 1024 * 1024

MIX_ROWS = 256
ROUTE_ROWS = 256
DISPATCH_ROWS = 256
EXPERT_ROWS = 256
COMBINE_ROWS = 128
FINISH_ROWS = 512


def _layer_norm(x, g, b):
    mu = jnp.mean(x, axis=-1, keepdims=True)
    xc = x - mu
    var = jnp.mean(xc * xc, axis=-1, keepdims=True)
    return xc * lax.rsqrt(var + LN_EPS) * g + b


def _dot(a, b):
    return jnp.dot(a, b, preferred_element_type=jnp.float32)


def _dot_nt(a, b):
    return lax.dot_general(a, b, (((1,), (1,)), ((), ())), preferred_element_type=jnp.float32)


def _dot_tn(a, b):
    return lax.dot_general(a, b, (((0,), (0,)), ((), ())), preferred_element_type=jnp.float32)


def _silu(x):
    return x * (1.0 / (1.0 + jnp.exp(-x)))


LANES = 128
ROW_TILE = D_MODEL // LANES


def _load_rows(ref, n_rows, lead=()):
    return jnp.concatenate([ref[lead + (pl.ds(s, n_rows, stride=ROW_TILE), slice(None))]
                            for s in range(ROW_TILE)], axis=1)


def _store_rows(ref, val, lead=()):
    n_rows = val.shape[0]
    for s in range(ROW_TILE):
        ref[lead + (pl.ds(s, n_rows, stride=ROW_TILE), slice(None))] = val[:, s * LANES:(s + 1) * LANES]


def _row_tile(r):
    return pl.ds(pl.multiple_of(r * ROW_TILE, ROW_TILE), ROW_TILE)


def _swap_halves(x):
    n = x.shape[-1]
    half = RET_QK_DIM // 2
    lane = lax.broadcasted_iota(jnp.int32, x.shape, 1)
    from_right = pltpu.roll(x, n - half, axis=1)
    from_left = pltpu.roll(x, half, axis=1)
    return jnp.where((lane % RET_QK_DIM) < half, from_right, from_left)


def _mixer_kernel(rel_bias_ref, x_ref, g_in_ref, b_in_ref, w_in_ref, w_out_ref, rot_ref, decay_ref,
                  zeta_ref, xi_ref, cdecay_ref, bucket_ref, sink_ref, g_mix_ref, b_mix_ref,
                  h2_ref, state_ref, kprev_ref, vprev_ref, bias_ref):
    b_id = pl.program_id(0)
    c_id = pl.program_id(1)
    W = SWA_WINDOW

    @pl.when((b_id == 0) & (c_id == 0))
    def _build_bias():
        bucket = bucket_ref[...]
        for h in range(SWA_HEADS):
            acc = jnp.full((2 * W, W), MASK_VALUE, jnp.float32)
            for b in range(REL_BUCKETS):
                acc = jnp.where(bucket == b, rel_bias_ref[b, h], acc)
            kh, g = divmod(h, SWA_GROUP)
            bias_ref[kh, :, g * W:(g + 1) * W] = acc

    @pl.when(c_id == 0)
    def _reset():
        state_ref[...] = jnp.zeros_like(state_ref)
        kprev_ref[...] = jnp.zeros_like(kprev_ref)
        vprev_ref[...] = jnp.zeros_like(vprev_ref)

    h = _layer_norm(x_ref[...], g_in_ref[...], b_in_ref[...])
    proj = _dot(h.astype(jnp.bfloat16), w_in_ref[...])

    o_q, o_k, o_v, o_g, o_sq, o_sk, o_sv = IN_OFFS
    cos_t = rot_ref[:, :RQK]
    sin_t = rot_ref[:, RQK:]
    q_all = proj[:, o_q:o_q + RQK]
    k_all = proj[:, o_k:o_k + RQK]
    q_rot = q_all * cos_t + _swap_halves(q_all) * sin_t
    k_rot = (k_all * cos_t + _swap_halves(k_all) * sin_t) * (RET_QK_DIM ** -0.5)

    n_sub = x_ref.shape[0] // RET_CHUNK
    states = [state_ref[hh] for hh in range(RET_HEADS)]
    k_prev = kprev_ref[...]
    v_prev = vprev_ref[...]
    cat_rows = []
    for s in range(n_sub):
        r0 = s * RET_CHUNK
        rows = slice(r0, r0 + RET_CHUNK)
        pieces = []
        for hh in range(RET_HEADS):
            qk = slice(hh * RET_QK_DIM, (hh + 1) * RET_QK_DIM)
            vv = slice(o_v + hh * RET_V_DIM, o_v + (hh + 1) * RET_V_DIM)
            gg = slice(o_g + hh * RET_V_DIM, o_g + (hh + 1) * RET_V_DIM)
            q = q_rot[rows, qk].astype(jnp.bfloat16)
            k32 = k_rot[rows, qk]
            v = proj[rows, vv].astype(jnp.bfloat16)
            scores = _dot_nt(q, k32.astype(jnp.bfloat16)) * decay_ref[hh]
            intra = _dot(scores.astype(jnp.bfloat16), v)
            inter = _dot(q, states[hh].astype(jnp.bfloat16)) * xi_ref[hh]
            ret = intra + inter
            kz = (k32 * zeta_ref[hh]).astype(jnp.bfloat16)
            states[hh] = states[hh] * cdecay_ref[hh] + _dot_tn(kz, v)
            mu = jnp.mean(ret, axis=-1, keepdims=True)
            rc = ret - mu
            var = jnp.mean(rc * rc, axis=-1, keepdims=True)
            normed = rc * lax.rsqrt(var + GN_EPS)
            pieces.append((_silu(proj[rows, gg]) * normed).astype(jnp.bfloat16))
        k_cur = proj[rows, o_sk:o_sk + SKV].astype(jnp.bfloat16)
        v_cur = proj[rows, o_sv:o_sv + SKV].astype(jnp.bfloat16)
        for kh in range(SWA_KV_HEADS):
            kv = slice(kh * SWA_HEAD_DIM, (kh + 1) * SWA_HEAD_DIM)
            q4 = jnp.concatenate(
                [proj[rows, o_sq + (kh * SWA_GROUP + g) * SWA_HEAD_DIM:
                      o_sq + (kh * SWA_GROUP + g + 1) * SWA_HEAD_DIM] for g in range(SWA_GROUP)],
                axis=0) * (SWA_HEAD_DIM ** -0.5)
            kcat = jnp.concatenate([k_prev[:, kv], k_cur[:, kv]], axis=0)
            vcat = jnp.concatenate([v_prev[:, kv], v_cur[:, kv]], axis=0)
            logits = _dot_nt(kcat, q4.astype(jnp.bfloat16)) + bias_ref[kh]
            if s == 0:
                key = lax.broadcasted_iota(jnp.int32, logits.shape, 0)
                logits = logits + jnp.where((key < W) & (c_id == 0), MASK_VALUE, 0.0)
            sink = sink_ref[kh]
            m = jnp.maximum(jnp.max(logits, axis=0, keepdims=True), sink)
            p = jnp.exp(logits - m)
            den = jnp.sum(p, axis=0, keepdims=True) + jnp.exp(sink - m)
            probs = (p / den).astype(jnp.bfloat16)
            o4 = _dot_tn(vcat, probs)
            pieces.extend(o4[:, g * W:(g + 1) * W].T.astype(jnp.bfloat16) for g in range(SWA_GROUP))
        k_prev, v_prev = k_cur, v_cur
        cat_rows.append(jnp.concatenate(pieces, axis=1))
    for hh in range(RET_HEADS):
        state_ref[hh] = states[hh]
    kprev_ref[...] = k_prev
    vprev_ref[...] = v_prev

    mix = _dot(jnp.concatenate(cat_rows, axis=0), w_out_ref[...])
    h2_ref[...] = _layer_norm(DEEPNORM_ALPHA * h + mix, g_mix_ref[...], b_mix_ref[...])


def _t5_bucket(dist):
    n = jnp.maximum(dist, 0)
    max_exact = REL_BUCKETS // 2
    ratio = jnp.log(jnp.maximum(n, 1).astype(jnp.float32) / max_exact) / math.log(REL_MAX_DIST / max_exact)
    large = jnp.minimum(max_exact + (ratio * (REL_BUCKETS - max_exact)).astype(jnp.int32), REL_BUCKETS - 1)
    return jnp.where(n < max_exact, n, large)


def _mixer(x, ln_in_g, ln_in_b, w_in, w_out, rel_bias, sinks, ln_mix_g, ln_mix_b):
    B, S, D = x.shape
    R = MIX_ROWS
    C = RET_CHUNK
    W = SWA_WINDOW
    f32 = jnp.float32
    half = RET_QK_DIM // 2
    inv = ROPE_BASE ** (-jnp.arange(half, dtype=f32) / half)
    ang = jnp.arange(S, dtype=f32)[:, None] * inv[None, :]
    cos, sin = jnp.cos(ang), jnp.sin(ang)
    cos_t = jnp.tile(jnp.concatenate([cos, cos], axis=-1), (1, RET_HEADS))
    sin_t = jnp.tile(jnp.concatenate([-sin, sin], axis=-1), (1, RET_HEADS))
    rot = jnp.concatenate([cos_t, sin_t], axis=-1)
    log_gamma = jnp.log(1.0 - 2.0 ** (-5.0 - jnp.arange(RET_HEADS, dtype=f32)))
    idx = jnp.arange(C, dtype=f32)
    diff = idx[:, None] - idx[None, :]
    decay = jnp.where(diff[None] >= 0, jnp.exp(jnp.maximum(diff, 0.0)[None] * log_gamma[:, None, None]), 0.0)
    zeta = jnp.exp((C - 1.0 - idx)[None, :] * log_gamma[:, None])
    xi = jnp.exp((idx + 1.0)[None, :] * log_gamma[:, None])
    zeta_b = jnp.broadcast_to(zeta[:, :, None], (RET_HEADS, C, RET_QK_DIM))
    xi_b = jnp.broadcast_to(xi[:, :, None], (RET_HEADS, C, RET_V_DIM))
    cdecay = jnp.broadcast_to(jnp.exp(C * log_gamma)[:, None, None], (RET_HEADS, RET_QK_DIM, RET_V_DIM))
    i = jnp.arange(W)
    j = jnp.arange(2 * W)
    dist = i[:, None] + W - j[None, :]
    bucket = jnp.where((dist >= 0) & (dist < W), _t5_bucket(dist), -1).astype(jnp.int32).T
    sink_row = jnp.repeat(sinks.astype(f32), W).reshape(SWA_KV_HEADS, 1, SWA_GROUP * W)

    const = lambda shape: pl.BlockSpec(shape, lambda b, c, *_: (0,) * len(shape))
    grid_spec = pltpu.PrefetchScalarGridSpec(
        num_scalar_prefetch=1,
        grid=(B, S // R),
        in_specs=[
            pl.BlockSpec((None, R, D), lambda b, c, *_: (b, c, 0)),
            const((1, D)), const((1, D)),
            const((D, IN_WIDTH)), const((MIX_WIDTH, D)),
            pl.BlockSpec((R, 2 * RQK), lambda b, c, *_: (c, 0)),
            const((RET_HEADS, C, C)), const((RET_HEADS, C, RET_QK_DIM)), const((RET_HEADS, C, RET_V_DIM)),
            const((RET_HEADS, RET_QK_DIM, RET_V_DIM)),
            const((2 * W, W)), const((SWA_KV_HEADS, 1, SWA_GROUP * W)),
            const((1, D)), const((1, D)),
        ],
        out_specs=pl.BlockSpec((R, D), lambda b, c, *_: (b * (S // R) + c, 0)),
        scratch_shapes=[
            pltpu.VMEM((RET_HEADS, RET_QK_DIM, RET_V_DIM), f32),
            pltpu.VMEM((W, SKV), jnp.bfloat16),
            pltpu.VMEM((W, SKV), jnp.bfloat16),
            pltpu.VMEM((SWA_KV_HEADS, 2 * W, SWA_GROUP * W), f32),
        ],
    )
    return pl.pallas_call(
        _mixer_kernel,
        grid_spec=grid_spec,
        out_shape=jax.ShapeDtypeStruct((B * S, D), f32),
        compiler_params=pltpu.CompilerParams(
            dimension_semantics=("arbitrary", "arbitrary"), vmem_limit_bytes=VMEM_LIMIT_BYTES),
    )(rel_bias.astype(f32), x, ln_in_g.reshape(1, D), ln_in_b.reshape(1, D),
      w_in.astype(jnp.bfloat16), w_out.astype(jnp.bfloat16), rot, decay, zeta_b, xi_b, cdecay,
      bucket, sink_row, ln_mix_g.reshape(1, D), ln_mix_b.reshape(1, D))


def _router_kernel(h_ref, wr_ref, rb_ref, e_ref, w_ref, rk_ref, cnt_ref, run_ref):
    f32 = jnp.float32
    R = h_ref.shape[0]
    E = N_EXPERTS
    neg = -jnp.inf

    @pl.when(pl.program_id(0) == 0)
    def _init():
        run_ref[...] = jnp.zeros_like(run_ref)

    logits = lax.dot_general(wr_ref[...], h_ref[...], (((1,), (1,)), ((), ())),
                             precision=lax.Precision.HIGHEST, preferred_element_type=f32)
    scores = 1.0 / (1.0 + jnp.exp(-logits))
    choice = scores + rb_ref[...]
    eid = lax.broadcasted_iota(jnp.int32, (E, R), 0)

    def first_argmax(vals, ids, none):
        m = jnp.max(vals, axis=0, keepdims=True)
        idx = jnp.min(jnp.where(vals == m, ids, none), axis=0, keepdims=True)
        return m, idx

    gid = lax.broadcasted_iota(jnp.int32, (GROUP_SIZE, R), 0)
    groups, gscore = [], []
    for g in range(N_GROUPS):
        vals = choice[g * GROUP_SIZE:(g + 1) * GROUP_SIZE]
        m1, i1 = first_argmax(vals, gid, GROUP_SIZE)
        m2 = jnp.max(jnp.where(gid == i1, neg, vals), axis=0, keepdims=True)
        groups.append(vals)
        gscore.append(m1 + m2)
    kept = []
    for g in range(N_GROUPS):
        beaten = jnp.zeros((1, R), f32)
        for g2 in range(N_GROUPS):
            if g2 == g:
                continue
            ahead = (gscore[g2] > gscore[g]) | (gscore[g2] == gscore[g]) if g2 < g else gscore[g2] > gscore[g]
            beaten = beaten + jnp.where(ahead, 1.0, 0.0)
        kept.append(jnp.where(beaten < TOPK_GROUPS, groups[g], neg))
    masked = jnp.concatenate(kept, axis=0)

    idxs, wts = [], []
    picked = jnp.zeros((E, R), f32)
    for _ in range(TOP_K):
        _, idx = first_argmax(masked, eid, E)
        hit = eid == idx
        idxs.append(idx)
        wts.append(jnp.sum(jnp.where(hit, scores, 0.0), axis=0, keepdims=True))
        masked = jnp.where(hit, neg, masked)
        picked = jnp.where(hit, 1.0, picked)
    wsum = wts[0]
    for k in range(1, TOP_K):
        wsum = wsum + wts[k]

    row = lax.broadcasted_iota(jnp.int32, (R, R), 0)
    col = lax.broadcasted_iota(jnp.int32, (R, R), 1)
    earlier = jnp.where(row < col, 1.0, 0.0).astype(jnp.bfloat16)
    picked_bf = picked.astype(jnp.bfloat16)
    run = run_ref[...]
    before = _dot(picked_bf, earlier) + jnp.concatenate([run] * (R // LANES), axis=1)
    sub_k = lax.broadcasted_iota(jnp.int32, (TOP_K, R), 0)
    e_out = jnp.zeros((TOP_K, R), jnp.int32)
    w_out = jnp.zeros((TOP_K, R), f32)
    rk_out = jnp.zeros((TOP_K, R), jnp.int32)
    for k in range(TOP_K):
        rank_k = jnp.sum(jnp.where(eid == idxs[k], before, 0.0), axis=0, keepdims=True)
        e_out = jnp.where(sub_k == k, idxs[k], e_out)
        w_out = jnp.where(sub_k == k, wts[k] / wsum * ROUTED_SCALE, w_out)
        rk_out = jnp.where(sub_k == k, rank_k.astype(jnp.int32), rk_out)
    e_ref[...] = e_out
    w_ref[...] = w_out
    rk_ref[...] = rk_out
    run_ref[...] = run + _dot(picked_bf, jnp.ones((R, LANES), jnp.bfloat16))
    cnt_ref[...] = run_ref[...]


def _router(h2, w_router, router_bias):
    T, D = h2.shape
    R = ROUTE_ROWS
    E = N_EXPERTS
    return pl.pallas_call(
        _router_kernel,
        grid=(T // R,),
        in_specs=[
            pl.BlockSpec((R, D), lambda i: (i, 0)),
            pl.BlockSpec((E, D), lambda i: (0, 0)),
            pl.BlockSpec((E, R), lambda i: (0, 0)),
        ],
        out_specs=[
            pl.BlockSpec((TOP_K, R), lambda i: (0, i)),
            pl.BlockSpec((TOP_K, R), lambda i: (0, i)),
            pl.BlockSpec((TOP_K, R), lambda i: (0, i)),
            pl.BlockSpec((E, LANES), lambda i: (0, 0)),
        ],
        out_shape=[
            jax.ShapeDtypeStruct((TOP_K, T), jnp.int32),
            jax.ShapeDtypeStruct((TOP_K, T), jnp.float32),
            jax.ShapeDtypeStruct((TOP_K, T), jnp.int32),
            jax.ShapeDtypeStruct((E, LANES), jnp.float32),
        ],
        scratch_shapes=[pltpu.VMEM((E, LANES), jnp.float32)],
        compiler_params=pltpu.CompilerParams(
            dimension_semantics=("arbitrary",), vmem_limit_bytes=VMEM_LIMIT_BYTES),
    )(h2, w_router.T, jnp.broadcast_to(router_bias.astype(jnp.float32)[:, None], (E, R)))


def _positions_kernel(row_start_ref, e_ref, rk_ref, pos_ref):
    e = e_ref[...]
    rk = rk_ref[...]

    def per_expert(i, pos):
        return jnp.where(e == i, rk + row_start_ref[i], pos)

    pos_ref[...] = lax.fori_loop(0, N_EXPERTS, per_expert, jnp.zeros_like(rk))


def _positions(e_idx, rank, row_start):
    n = e_idx.size
    shape = (n // LANES, LANES)
    grid_spec = pltpu.PrefetchScalarGridSpec(
        num_scalar_prefetch=1,
        grid=(1,),
        in_specs=[pl.BlockSpec(shape, lambda i, *_: (0, 0)), pl.BlockSpec(shape, lambda i, *_: (0, 0))],
        out_specs=pl.BlockSpec(shape, lambda i, *_: (0, 0)),
    )
    pos = pl.pallas_call(
        _positions_kernel,
        grid_spec=grid_spec,
        out_shape=jax.ShapeDtypeStruct(shape, jnp.int32),
    )(row_start, e_idx.reshape(shape), rank.reshape(shape))
    return pos.reshape(n)


def _dispatch_kernel(row_start_ref, cnt_ref, n_act_ref, pos_ref, h_ref, xs_ref, zero_ref, sem, zsem,
                     *, n_tokens):
    R = h_ref.shape[0] // ROW_TILE
    BM = EXPERT_ROWS
    n_blocks = xs_ref.shape[0] // (BM * ROW_TILE)
    n_pad_units = n_blocks - n_tokens * TOP_K // BM

    @pl.when(pl.program_id(0) == 0)
    def _zero_padding():
        zero_ref[...] = jnp.zeros_like(zero_ref)

        def expert_tail(e, carry):
            n_tail = pl.multiple_of(((BM - cnt_ref[e] % BM) % BM) * ROW_TILE, ROW_TILE)

            @pl.when(n_tail > 0)
            def _():
                dst = pl.multiple_of((row_start_ref[e] + cnt_ref[e]) * ROW_TILE, ROW_TILE)
                pltpu.make_async_copy(zero_ref.at[pl.ds(0, n_tail)], xs_ref.at[pl.ds(dst, n_tail)], zsem).start()
            return carry

        lax.fori_loop(0, N_EXPERTS, expert_tail, 0)

        def idle_block(i, carry):
            dst = pl.multiple_of(i * (BM * ROW_TILE), BM * ROW_TILE)
            pltpu.make_async_copy(zero_ref, xs_ref.at[pl.ds(dst, BM * ROW_TILE)], zsem).start()
            return carry

        lax.fori_loop(n_act_ref[0], n_blocks, idle_block, 0)

        def drain(i, carry):
            pltpu.make_async_copy(zero_ref, xs_ref.at[pl.ds(0, BM * ROW_TILE)], zsem).wait()
            return carry

        lax.fori_loop(0, n_pad_units, drain, 0)

    def issue(t, carry):
        src = h_ref.at[_row_tile(t)]
        for k in range(TOP_K):
            dest = pos_ref[t * TOP_K + k]
            pltpu.make_async_copy(src, xs_ref.at[_row_tile(dest)], sem).start(priority=k % 2)
        return carry

    lax.fori_loop(0, R, issue, 0)
    n = R * TOP_K * ROW_TILE
    pltpu.make_async_copy(xs_ref.at[pl.ds(0, n)], xs_ref.at[pl.ds(0, n)], sem).wait()


def _dispatch(h2, pos, row_start, cnt, n_act, n_rows):
    T = h2.shape[0] // ROW_TILE
    R = DISPATCH_ROWS
    grid_spec = pltpu.PrefetchScalarGridSpec(
        num_scalar_prefetch=3,
        grid=(T // R,),
        in_specs=[
            pl.BlockSpec((R * TOP_K,), lambda i, *_: (i,), memory_space=pltpu.SMEM),
            pl.BlockSpec((R * ROW_TILE, LANES), lambda i, *_: (i, 0)),
        ],
        out_specs=pl.BlockSpec(memory_space=pl.ANY),
        scratch_shapes=[pltpu.VMEM((EXPERT_ROWS * ROW_TILE, LANES), jnp.float32),
                        pltpu.SemaphoreType.DMA(()), pltpu.SemaphoreType.DMA(())],
    )
    return pl.pallas_call(
        functools.partial(_dispatch_kernel, n_tokens=T),
        grid_spec=grid_spec,
        out_shape=jax.ShapeDtypeStruct((n_rows * ROW_TILE, LANES), jnp.float32),
        compiler_params=pltpu.CompilerParams(
            dimension_semantics=("arbitrary",), vmem_limit_bytes=VMEM_LIMIT_BYTES),
    )(row_start, cnt, n_act, pos, h2)


SC_WINDOW = 32


def _dispatch_sc(h2, pos_kt, n_rows):
    T, D = h2.shape
    W = SC_WINDOW
    idx = _window_indices(pos_kt, W)
    idx_rows = TOP_K * W // LANES
    mesh = plsc.VectorSubcoreMesh(core_axis_name="core", subcore_axis_name="subcore")

    @pl.kernel(out_type=jax.ShapeDtypeStruct((n_rows, D), h2.dtype), mesh=mesh, scratch_types=[])
    def scatter_rows(x_hbm, i_hbm, o_hbm):
        def body(x_vmem, i_vmem):
            for k in range(TOP_K):
                r, q = divmod(k * W, LANES)
                pltpu.sync_copy(x_vmem, o_hbm.at[i_vmem.at[r, pl.ds(q, W)]])

        pltpu.emit_pipeline(
            body,
            grid=(T // W,),
            in_specs=[pl.BlockSpec((W, D), lambda i: (i, 0)),
                      pl.BlockSpec((idx_rows, LANES), lambda i: (i, 0))],
            out_specs=[],
            core_axis_name=("core", "subcore"),
            dimension_semantics=(pltpu.PARALLEL,),
        )(x_hbm, i_hbm)

    return scatter_rows(h2, idx)


def _window_indices(pos_kt, window):
    K, T = pos_kt.shape
    return pos_kt.reshape(K, T // window, window).transpose(1, 0, 2).reshape(T * K // LANES, LANES)


def _gather_rows_sc(ys, pos_kt):
    n = pos_kt.size
    W = SC_WINDOW
    ys3 = ys.reshape(ys.shape[0] // ROW_TILE, ROW_TILE, LANES)
    idx = jnp.pad(pos_kt.reshape(n // W, W), ((0, 0), (0, LANES - W)))
    mesh = plsc.VectorSubcoreMesh(core_axis_name="core", subcore_axis_name="subcore")

    @pl.kernel(out_type=jax.ShapeDtypeStruct((n, ROW_TILE, LANES), ys.dtype), mesh=mesh, scratch_types=[])
    def gather_rows(y_hbm, i_hbm, o_hbm):
        def body(i_vmem, o_vmem):
            pltpu.sync_copy(y_hbm.at[i_vmem.at[0, pl.ds(0, W)]], o_vmem)

        pltpu.emit_pipeline(
            body,
            grid=(n // W,),
            in_specs=[pl.BlockSpec((1, LANES), lambda i: (i, 0))],
            out_specs=[pl.BlockSpec((W, ROW_TILE, LANES), lambda i: (i, 0, 0))],
            core_axis_name=("core", "subcore"),
            dimension_semantics=(pltpu.PARALLEL,),
        )(i_hbm, o_hbm)

    return gather_rows(ys3, idx).reshape(n * ROW_TILE, LANES)


X_SLOTS = 4
Y_SLOTS = 3


def _experts_kernel(blk_e_ref, first_ref, slot_ref, next_e_ref, valid_ref, n_act_ref, xs_hbm, wg_hbm,
                    wu_hbm, wd_hbm, ys_hbm, x_buf, y_buf, wg_buf, wu_buf, wd_buf, wg_bf, wu_bf, wd_bf,
                    sems, x_sems, y_sems):
    i = pl.program_id(0)
    n_act = n_act_ref[0]
    bf16 = jnp.bfloat16
    blk = EXPERT_ROWS * ROW_TILE

    def x_copy(j):
        n = pl.multiple_of((valid_ref[j] + 7) // 8 * 8, 8)
        src = xs_hbm.at[pl.ds(pl.multiple_of(j * EXPERT_ROWS, EXPERT_ROWS), n)]
        return pltpu.make_async_copy(src, x_buf.at[j % X_SLOTS, pl.ds(0, n)], x_sems.at[j % X_SLOTS])

    def y_copy(j):
        n = pl.multiple_of(valid_ref[j] * ROW_TILE, ROW_TILE)
        dst = ys_hbm.at[pl.ds(pl.multiple_of(j * blk, blk), n)]
        return pltpu.make_async_copy(y_buf.at[j % Y_SLOTS, pl.ds(0, n)], dst, y_sems.at[j % Y_SLOTS])

    @pl.when(i == 0)
    def _prime():
        x_buf[...] = jnp.zeros_like(x_buf)
        for j in range(X_SLOTS - 1):
            @pl.when(j < n_act)
            def _():
                x_copy(j).start()

    @pl.when(i + (X_SLOTS - 1) < n_act)
    def _prefetch():
        x_copy(i + (X_SLOTS - 1)).start()

    def weight_copies(e, slot):
        return (pltpu.make_async_copy(wg_hbm.at[e], wg_buf.at[slot], sems.at[slot]),
                pltpu.make_async_copy(wu_hbm.at[e], wu_buf.at[slot], sems.at[slot]),
                pltpu.make_async_copy(wd_hbm.at[e], wd_buf.at[slot], sems.at[slot]))

    @pl.when((i < n_act_ref[0]) & (first_ref[i] == 1))
    def _new_expert():
        slot = slot_ref[i]

        @pl.when(i == 0)
        def _():
            for c in weight_copies(blk_e_ref[0], 0):
                c.start()

        for c in weight_copies(blk_e_ref[i], slot):
            c.wait()

        @pl.when(next_e_ref[i] >= 0)
        def _():
            for c in weight_copies(next_e_ref[i], 1 - slot):
                c.start()

        wg_bf[...] = wg_buf[slot].astype(bf16)
        wu_bf[...] = wu_buf[slot].astype(bf16)
        wd_bf[...] = wd_buf[slot].astype(bf16)

    @pl.when(i < n_act)
    def _compute():
        x_copy(i).wait()
        x = x_buf[i % X_SLOTS].astype(bf16)
        g = _dot(x, wg_bf[...])
        u = _dot(x, wu_bf[...])
        a = (_silu(g) * u).astype(bf16)
        y = _dot(a, wd_bf[...])

        @pl.when(i >= Y_SLOTS)
        def _():
            y_copy(i - Y_SLOTS).wait()

        _store_rows(y_buf, y, lead=(i % Y_SLOTS,))
        y_copy(i).start()

    @pl.when(i == n_act - 1)
    def _drain():
        for d in range(Y_SLOTS):
            @pl.when(i - d >= 0)
            def _():
                y_copy(i - d).wait()


def _experts(xs, blk_e, n_act, row_start, cnt, w_gate, w_up, w_down):
    D = D_MODEL
    BM = EXPERT_ROWS
    F = EXPERT_DIM
    n_blocks = xs.shape[0] // BM
    blk_in_expert = jnp.arange(n_blocks, dtype=jnp.int32) - row_start[blk_e] // BM
    valid = jnp.clip(cnt[blk_e] - blk_in_expert * BM, 0, BM).astype(jnp.int32)
    ids = jnp.arange(n_blocks, dtype=jnp.int32)
    active = ids < n_act[0]
    first = active & ((ids == 0) | (blk_e != jnp.roll(blk_e, 1)))
    slot = ((jnp.cumsum(first.astype(jnp.int32)) - 1) % 2).astype(jnp.int32)
    first_pos = jnp.where(first, ids, n_blocks)
    later_first = lax.cummin(jnp.concatenate([first_pos[1:], jnp.full((1,), n_blocks, jnp.int32)]), reverse=True)
    next_e = jnp.where(later_first < n_blocks, blk_e[jnp.minimum(later_first, n_blocks - 1)], -1).astype(jnp.int32)

    grid_spec = pltpu.PrefetchScalarGridSpec(
        num_scalar_prefetch=6,
        grid=(n_blocks,),
        in_specs=[pl.BlockSpec(memory_space=pl.ANY)] * 4,
        out_specs=pl.BlockSpec(memory_space=pl.ANY),
        scratch_shapes=[
            pltpu.VMEM((X_SLOTS, BM, D), jnp.float32),
            pltpu.VMEM((Y_SLOTS, BM * ROW_TILE, LANES), jnp.float32),
            pltpu.VMEM((2, D, F), jnp.float32), pltpu.VMEM((2, D, F), jnp.float32),
            pltpu.VMEM((2, F, D), jnp.float32),
            pltpu.VMEM((D, F), jnp.bfloat16), pltpu.VMEM((D, F), jnp.bfloat16),
            pltpu.VMEM((F, D), jnp.bfloat16),
            pltpu.SemaphoreType.DMA((2,)), pltpu.SemaphoreType.DMA((X_SLOTS,)),
            pltpu.SemaphoreType.DMA((Y_SLOTS,)),
        ],
    )
    return pl.pallas_call(
        _experts_kernel,
        grid_spec=grid_spec,
        out_shape=jax.ShapeDtypeStruct((xs.shape[0] * ROW_TILE, LANES), jnp.float32),
        compiler_params=pltpu.CompilerParams(
            dimension_semantics=("arbitrary",), vmem_limit_bytes=VMEM_LIMIT_BYTES),
    )(blk_e, first.astype(jnp.int32), slot, next_e, valid, n_act, xs, w_gate, w_up, w_down)


def _finish_kernel(h_ref, part_ref, w_ref, *rest):
    slabs = rest[:SC_COMBINE_SLOTS]
    g_ref, b_ref, out_ref = rest[SC_COMBINE_SLOTS:]
    R = h_ref.shape[0]
    w = w_ref[...]
    ffn = part_ref[...]
    for k in range(SC_COMBINE_SLOTS):
        ffn = ffn + _load_rows(slabs[k], R) * w[:, k:k + 1]
    out_ref[...] = _layer_norm(DEEPNORM_ALPHA * h_ref[...] + ffn, g_ref[...], b_ref[...])


def _finish(h2, partial, top_w, gathered, ln_g, ln_b):
    T, D = h2.shape
    R = FINISH_ROWS
    rows = pl.BlockSpec((R, D), lambda i: (i, 0))
    vec = pl.BlockSpec((1, D), lambda i: (0, 0))
    slab = lambda k: pl.BlockSpec((R * ROW_TILE, LANES), lambda i: (k * (T // R) + i, 0))
    return pl.pallas_call(
        _finish_kernel,
        grid=(T // R,),
        in_specs=[rows, rows, pl.BlockSpec((R, TOP_K), lambda i: (i, 0))]
        + [slab(k) for k in range(SC_COMBINE_SLOTS)] + [vec, vec],
        out_specs=rows,
        out_shape=jax.ShapeDtypeStruct((T, D), jnp.float32),
        compiler_params=pltpu.CompilerParams(
            dimension_semantics=("arbitrary",), vmem_limit_bytes=VMEM_LIMIT_BYTES),
    )(h2, partial, top_w, *([gathered] * SC_COMBINE_SLOTS), ln_g.reshape(1, D), ln_b.reshape(1, D))


SC_COMBINE_SLOTS = 6


def _combine_kernel(pos_ref, pos_next_ref, shared_ref, w_ref, ys_ref, out_ref, buf_ref, sems):
    R = shared_ref.shape[0]
    i = pl.program_id(0)
    slot = i % 2

    def gather(p_ref, s):
        def issue(t, carry):
            for k in range(SC_COMBINE_SLOTS, TOP_K):
                src = p_ref[t * TOP_K + k]
                pltpu.make_async_copy(ys_ref.at[_row_tile(src)], buf_ref.at[s, k - SC_COMBINE_SLOTS, _row_tile(t)],
                                      sems.at[s]).start(priority=k % 2)
            return carry

        lax.fori_loop(0, R, issue, 0)

    @pl.when(i == 0)
    def _():
        gather(pos_ref, 0)

    @pl.when(i + 1 < pl.num_programs(0))
    def _():
        gather(pos_next_ref, 1 - slot)

    ffn = shared_ref[...]
    pltpu.make_async_copy(buf_ref.at[slot], buf_ref.at[slot], sems.at[slot]).wait()
    w = w_ref[...]
    for k in range(SC_COMBINE_SLOTS, TOP_K):
        ffn = ffn + _load_rows(buf_ref, R, lead=(slot, k - SC_COMBINE_SLOTS)) * w[:, k:k + 1]
    out_ref[...] = ffn


def _combine(shared, pos, top_w, ys):
    T, D = shared.shape
    R = COMBINE_ROWS
    return pl.pallas_call(
        _combine_kernel,
        grid=(T // R,),
        in_specs=[
            pl.BlockSpec((R * TOP_K,), lambda i: (i,), memory_space=pltpu.SMEM),
            pl.BlockSpec((R * TOP_K,), lambda i: (jnp.minimum(i + 1, T // R - 1),), memory_space=pltpu.SMEM),
            pl.BlockSpec((R, D), lambda i: (i, 0)),
            pl.BlockSpec((R, TOP_K), lambda i: (i, 0)),
            pl.BlockSpec(memory_space=pl.ANY),
        ],
        out_specs=pl.BlockSpec((R, D), lambda i: (i, 0)),
        scratch_shapes=[pltpu.VMEM((2, TOP_K - SC_COMBINE_SLOTS, R * ROW_TILE, LANES), jnp.float32),
                        pltpu.SemaphoreType.DMA((2,))],
        out_shape=jax.ShapeDtypeStruct((T, D), jnp.float32),
        compiler_params=pltpu.CompilerParams(
            dimension_semantics=("arbitrary",), vmem_limit_bytes=VMEM_LIMIT_BYTES),
    )(pos, pos, shared, top_w, ys)


def _shared_expert_kernel(h_ref, wsg_ref, wsu_ref, wsd_ref, out_ref):
    hb = h_ref[...].astype(jnp.bfloat16)
    act = (_silu(_dot(hb, wsg_ref[...])) * _dot(hb, wsu_ref[...])).astype(jnp.bfloat16)
    out_ref[...] = _dot(act, wsd_ref[...])


def _shared_expert(h2, ws_gate, ws_up, ws_down):
    T, D = h2.shape
    R = FINISH_ROWS
    F = SHARED_DIM
    bf16 = jnp.bfloat16
    const = lambda shape: pl.BlockSpec(shape, lambda i: (0,) * len(shape))
    rows = pl.BlockSpec((R, D), lambda i: (i, 0))
    return pl.pallas_call(
        _shared_expert_kernel,
        grid=(T // R,),
        in_specs=[rows, const((D, F)), const((D, F)), const((F, D))],
        out_specs=rows,
        out_shape=jax.ShapeDtypeStruct((T, D), jnp.float32),
        compiler_params=pltpu.CompilerParams(
            dimension_semantics=("arbitrary",), vmem_limit_bytes=VMEM_LIMIT_BYTES),
    )(h2, ws_gate.astype(bf16), ws_up.astype(bf16), ws_down.astype(bf16))


def _moe(h2, w_router, router_bias, w_gate, w_up, w_down, ws_gate, ws_up, ws_down, ln_g, ln_b):
    T = h2.shape[0]
    E = N_EXPERTS
    BM = EXPERT_ROWS
    e_idx, top_w, rank, counts = _router(h2, w_router, router_bias)
    cnt = counts[:, 0].astype(jnp.int32)
    nblk = (cnt + BM - 1) // BM
    blk_end = jnp.cumsum(nblk)
    row_start = ((blk_end - nblk) * BM).astype(jnp.int32)
    n_blocks = T * TOP_K // BM + E
    n_act = blk_end[-1:].astype(jnp.int32)
    blk_ids = jnp.minimum(jnp.arange(n_blocks, dtype=jnp.int32), n_act[0] - 1)
    blk_e = jnp.minimum(jnp.sum(blk_end[None, :] <= blk_ids[:, None], axis=1), E - 1).astype(jnp.int32)
    pos_kt = _positions(e_idx, rank, row_start).reshape(TOP_K, T)
    pos = pos_kt.T.reshape(T * TOP_K)
    xs = _dispatch_sc(h2, pos_kt, n_blocks * BM)
    shared = _shared_expert(h2, ws_gate, ws_up, ws_down)
    ys = _experts(xs, blk_e, n_act, row_start, cnt, w_gate, w_up, w_down)
    gathered = _gather_rows_sc(ys, pos_kt[:SC_COMBINE_SLOTS])
    w_tk = top_w.T
    partial = _combine(shared, pos, w_tk, ys)
    return _finish(h2, partial, w_tk, gathered, ln_g, ln_b)


def kernel(x, ln_in_g, ln_in_b, w_in, w_out, rel_bias, attn_sinks, ln_mix_g, ln_mix_b, w_router,
           router_bias, w_gate, w_up, w_down, ws_gate, ws_up, ws_down, ln_ffn_g, ln_ffn_b):
    B, S, D = x.shape
    h = _mixer(x, ln_in_g, ln_in_b, w_in[0], w_out[0], rel_bias, attn_sinks[0], ln_mix_g[0], ln_mix_b[0])
    out = _moe(h, w_router[0], router_bias[0], w_gate[0], w_up[0], w_down[0],
               ws_gate[0], ws_up[0], ws_down[0], ln_ffn_g[0], ln_ffn_b[0])
    return out.reshape(B, S, D)
```

```python
import functools
import math

import jax
import jax.numpy as jnp
from jax import lax
from jax.experimental import pallas as pl
from jax.experimental.pallas import tpu as pltpu
from jax.experimental.pallas import tpu_sc as plsc

D_MODEL = 1024
DEPTH = 1
RET_HEADS = 4
RET_QK_DIM = 64
RET_V_DIM = 128
RET_CHUNK = 128
RET_WIDTH = RET_HEADS * RET_V_DIM
ROPE_BASE = 10000.0
SWA_HEADS = 8
SWA_KV_HEADS = 2
SWA_GROUP = SWA_HEADS // SWA_KV_HEADS
SWA_HEAD_DIM = 64
SWA_WINDOW = 128
SWA_WIDTH = SWA_HEADS * SWA_HEAD_DIM
MIX_WIDTH = RET_WIDTH + SWA_WIDTH
RQK = RET_HEADS * RET_QK_DIM
SKV = SWA_KV_HEADS * SWA_HEAD_DIM
IN_SIZES = (RQK, RQK, RET_WIDTH, RET_WIDTH, SWA_WIDTH, SKV, SKV)
IN_OFFS = tuple(sum(IN_SIZES[:i]) for i in range(len(IN_SIZES)))
IN_WIDTH = sum(IN_SIZES)
REL_BUCKETS = 32
REL_MAX_DIST = 128
N_EXPERTS = 256
TOP_K = 8
N_GROUPS = 8
GROUP_SIZE = N_EXPERTS // N_GROUPS
TOPK_GROUPS = 4
EXPERT_DIM = 256
SHARED_DIM = 256
ROUTED_SCALE = 2.5
LN_EPS = 1e-5
GN_EPS = 1e-6
DEEPNORM_ALPHA = (2 * DEPTH) ** 0.25
MASK_VALUE = -1e30

VMEM_LIMIT_BYTES = 56 * 1024 * 1024

MIX_ROWS = 256
ROUTE_ROWS = 256
DISPATCH_ROWS = 256
EXPERT_ROWS = 256
COMBINE_ROWS = 256
POSITION_ROWS = 64
FINISH_ROWS = 512


def _layer_norm(x, g, b):
    mu = jnp.mean(x, axis=-1, keepdims=True)
    xc = x - mu
    var = jnp.mean(xc * xc, axis=-1, keepdims=True)
    return xc * lax.rsqrt(var + LN_EPS) * g + b


def _dot(a, b):
    return jnp.dot(a, b, preferred_element_type=jnp.float32)


def _dot_nt(a, b):
    return lax.dot_general(a, b, (((1,), (1,)), ((), ())), preferred_element_type=jnp.float32)


def _dot_tn(a, b):
    return lax.dot_general(a, b, (((0,), (0,)), ((), ())), preferred_element_type=jnp.float32)


def _silu(x):
    return x * (1.0 / (1.0 + jnp.exp(-x)))


LANES = 128
ROW_TILE = D_MODEL // LANES


def _load_rows(ref, n_rows, lead=()):
    return jnp.concatenate([ref[lead + (pl.ds(s, n_rows, stride=ROW_TILE), slice(None))]
                            for s in range(ROW_TILE)], axis=1)


def _store_rows(ref, val, lead=()):
    n_rows = val.shape[0]
    for s in range(ROW_TILE):
        ref[lead + (pl.ds(s, n_rows, stride=ROW_TILE), slice(None))] = val[:, s * LANES:(s + 1) * LANES]


def _row_tile(r):
    return pl.ds(pl.multiple_of(r * ROW_TILE, ROW_TILE), ROW_TILE)


def _swap_halves(x):
    n = x.shape[-1]
    half = RET_QK_DIM // 2
    lane = lax.broadcasted_iota(jnp.int32, x.shape, 1)
    from_right = pltpu.roll(x, n - half, axis=1)
    from_left = pltpu.roll(x, half, axis=1)
    return jnp.where((lane % RET_QK_DIM) < half, from_right, from_left)


def _mixer_kernel(rel_bias_ref, x_ref, g_in_ref, b_in_ref, w_in_ref, w_out_ref, rot_ref, decay_ref,
                  zeta_ref, xi_ref, cdecay_ref, bucket_ref, sink_ref, g_mix_ref, b_mix_ref,
                  h2_ref, state_ref, kprev_ref, vprev_ref, bias_ref):
    b_id = pl.program_id(0)
    c_id = pl.program_id(1)
    W = SWA_WINDOW

    @pl.when((b_id == 0) & (c_id == 0))
    def _build_bias():
        bucket = bucket_ref[...]
        for h in range(SWA_HEADS):
            acc = jnp.full((2 * W, W), MASK_VALUE, jnp.float32)
            for b in range(REL_BUCKETS):
                acc = jnp.where(bucket == b, rel_bias_ref[b, h], acc)
            kh, g = divmod(h, SWA_GROUP)
            bias_ref[kh, :, g * W:(g + 1) * W] = acc

    @pl.when(c_id == 0)
    def _reset():
        state_ref[...] = jnp.zeros_like(state_ref)
        kprev_ref[...] = jnp.zeros_like(kprev_ref)
        vprev_ref[...] = jnp.zeros_like(vprev_ref)

    h = _layer_norm(x_ref[...], g_in_ref[...], b_in_ref[...])
    proj = _dot(h.astype(jnp.bfloat16), w_in_ref[...])

    o_q, o_k, o_v, o_g, o_sq, o_sk, o_sv = IN_OFFS
    cos_t = rot_ref[:, :RQK]
    sin_t = rot_ref[:, RQK:]
    q_all = proj[:, o_q:o_q + RQK]
    k_all = proj[:, o_k:o_k + RQK]
    q_rot = q_all * cos_t + _swap_halves(q_all) * sin_t
    k_rot = (k_all * cos_t + _swap_halves(k_all) * sin_t) * (RET_QK_DIM ** -0.5)

    n_sub = x_ref.shape[0] // RET_CHUNK
    states = [state_ref[hh] for hh in range(RET_HEADS)]
    k_prev = kprev_ref[...]
    v_prev = vprev_ref[...]
    cat_rows = []
    for s in range(n_sub):
        r0 = s * RET_CHUNK
        rows = slice(r0, r0 + RET_CHUNK)
        pieces = []
        for hh in range(RET_HEADS):
            qk = slice(hh * RET_QK_DIM, (hh + 1) * RET_QK_DIM)
            vv = slice(o_v + hh * RET_V_DIM, o_v + (hh + 1) * RET_V_DIM)
            gg = slice(o_g + hh * RET_V_DIM, o_g + (hh + 1) * RET_V_DIM)
            q = q_rot[rows, qk].astype(jnp.bfloat16)
            k32 = k_rot[rows, qk]
            v = proj[rows, vv].astype(jnp.bfloat16)
            scores = _dot_nt(q, k32.astype(jnp.bfloat16)) * decay_ref[hh]
            intra = _dot(scores.astype(jnp.bfloat16), v)
            inter = _dot(q, states[hh].astype(jnp.bfloat16)) * xi_ref[hh]
            ret = intra + inter
            kz = (k32 * zeta_ref[hh]).astype(jnp.bfloat16)
            states[hh] = states[hh] * cdecay_ref[hh] + _dot_tn(kz, v)
            mu = jnp.mean(ret, axis=-1, keepdims=True)
            rc = ret - mu
            var = jnp.mean(rc * rc, axis=-1, keepdims=True)
            normed = rc * lax.rsqrt(var + GN_EPS)
            pieces.append((_silu(proj[rows, gg]) * normed).astype(jnp.bfloat16))
        k_cur = proj[rows, o_sk:o_sk + SKV].astype(jnp.bfloat16)
        v_cur = proj[rows, o_sv:o_sv + SKV].astype(jnp.bfloat16)
        for kh in range(SWA_KV_HEADS):
            kv = slice(kh * SWA_HEAD_DIM, (kh + 1) * SWA_HEAD_DIM)
            q4 = jnp.concatenate(
                [proj[rows, o_sq + (kh * SWA_GROUP + g) * SWA_HEAD_DIM:
                      o_sq + (kh * SWA_GROUP + g + 1) * SWA_HEAD_DIM] for g in range(SWA_GROUP)],
                axis=0) * (SWA_HEAD_DIM ** -0.5)
            kcat = jnp.concatenate([k_prev[:, kv], k_cur[:, kv]], axis=0)
            vcat = jnp.concatenate([v_prev[:, kv], v_cur[:, kv]], axis=0)
            logits = _dot_nt(kcat, q4.astype(jnp.bfloat16)) + bias_ref[kh]
            if s == 0:
                key = lax.broadcasted_iota(jnp.int32, logits.shape, 0)
                logits = logits + jnp.where((key < W) & (c_id == 0), MASK_VALUE, 0.0)
            sink = sink_ref[kh]
            m = jnp.maximum(jnp.max(logits, axis=0, keepdims=True), sink)
            p = jnp.exp(logits - m)
            den = jnp.sum(p, axis=0, keepdims=True) + jnp.exp(sink - m)
            probs = (p / den).astype(jnp.bfloat16)
            o4 = _dot_tn(vcat, probs)
            pieces.extend(o4[:, g * W:(g + 1) * W].T.astype(jnp.bfloat16) for g in range(SWA_GROUP))
        k_prev, v_prev = k_cur, v_cur
        cat_rows.append(jnp.concatenate(pieces, axis=1))
    for hh in range(RET_HEADS):
        state_ref[hh] = states[hh]
    kprev_ref[...] = k_prev
    vprev_ref[...] = v_prev

    mix = _dot(jnp.concatenate(cat_rows, axis=0), w_out_ref[...])
    h2_ref[...] = _layer_norm(DEEPNORM_ALPHA * h + mix, g_mix_ref[...], b_mix_ref[...])


def _t5_bucket(dist):
    n = jnp.maximum(dist, 0)
    max_exact = REL_BUCKETS // 2
    ratio = jnp.log(jnp.maximum(n, 1).astype(jnp.float32) / max_exact) / math.log(REL_MAX_DIST / max_exact)
    large = jnp.minimum(max_exact + (ratio * (REL_BUCKETS - max_exact)).astype(jnp.int32), REL_BUCKETS - 1)
    return jnp.where(n < max_exact, n, large)


def _mixer(x, ln_in_g, ln_in_b, w_in, w_out, rel_bias, sinks, ln_mix_g, ln_mix_b):
    B, S, D = x.shape
    R = MIX_ROWS
    C = RET_CHUNK
    W = SWA_WINDOW
    f32 = jnp.float32
    half = RET_QK_DIM // 2
    inv = ROPE_BASE ** (-jnp.arange(half, dtype=f32) / half)
    ang = jnp.arange(S, dtype=f32)[:, None] * inv[None, :]
    cos, sin = jnp.cos(ang), jnp.sin(ang)
    cos_t = jnp.tile(jnp.concatenate([cos, cos], axis=-1), (1, RET_HEADS))
    sin_t = jnp.tile(jnp.concatenate([-sin, sin], axis=-1), (1, RET_HEADS))
    rot = jnp.concatenate([cos_t, sin_t], axis=-1)
    log_gamma = jnp.log(1.0 - 2.0 ** (-5.0 - jnp.arange(RET_HEADS, dtype=f32)))
    idx = jnp.arange(C, dtype=f32)
    diff = idx[:, None] - idx[None, :]
    decay = jnp.where(diff[None] >= 0, jnp.exp(jnp.maximum(diff, 0.0)[None] * log_gamma[:, None, None]), 0.0)
    zeta = jnp.exp((C - 1.0 - idx)[None, :] * log_gamma[:, None])
    xi = jnp.exp((idx + 1.0)[None, :] * log_gamma[:, None])
    zeta_b = jnp.broadcast_to(zeta[:, :, None], (RET_HEADS, C, RET_QK_DIM))
    xi_b = jnp.broadcast_to(xi[:, :, None], (RET_HEADS, C, RET_V_DIM))
    cdecay = jnp.broadcast_to(jnp.exp(C * log_gamma)[:, None, None], (RET_HEADS, RET_QK_DIM, RET_V_DIM))
    i = jnp.arange(W)
    j = jnp.arange(2 * W)
    dist = i[:, None] + W - j[None, :]
    bucket = jnp.where((dist >= 0) & (dist < W), _t5_bucket(dist), -1).astype(jnp.int32).T
    sink_row = jnp.repeat(sinks.astype(f32), W).reshape(SWA_KV_HEADS, 1, SWA_GROUP * W)

    const = lambda shape: pl.BlockSpec(shape, lambda b, c, *_: (0,) * len(shape))
    grid_spec = pltpu.PrefetchScalarGridSpec(
        num_scalar_prefetch=1,
        grid=(B, S // R),
        in_specs=[
            pl.BlockSpec((None, R, D), lambda b, c, *_: (b, c, 0)),
            const((1, D)), const((1, D)),
            const((D, IN_WIDTH)), const((MIX_WIDTH, D)),
            pl.BlockSpec((R, 2 * RQK), lambda b, c, *_: (c, 0)),
            const((RET_HEADS, C, C)), const((RET_HEADS, C, RET_QK_DIM)), const((RET_HEADS, C, RET_V_DIM)),
            const((RET_HEADS, RET_QK_DIM, RET_V_DIM)),
            const((2 * W, W)), const((SWA_KV_HEADS, 1, SWA_GROUP * W)),
            const((1, D)), const((1, D)),
        ],
        out_specs=pl.BlockSpec((R, D), lambda b, c, *_: (b * (S // R) + c, 0)),
        scratch_shapes=[
            pltpu.VMEM((RET_HEADS, RET_QK_DIM, RET_V_DIM), f32),
            pltpu.VMEM((W, SKV), jnp.bfloat16),
            pltpu.VMEM((W, SKV), jnp.bfloat16),
            pltpu.VMEM((SWA_KV_HEADS, 2 * W, SWA_GROUP * W), f32),
        ],
    )
    return pl.pallas_call(
        _mixer_kernel,
        grid_spec=grid_spec,
        out_shape=jax.ShapeDtypeStruct((B * S, D), f32),
        compiler_params=pltpu.CompilerParams(
            dimension_semantics=("arbitrary", "arbitrary"), vmem_limit_bytes=VMEM_LIMIT_BYTES),
    )(rel_bias.astype(f32), x, ln_in_g.reshape(1, D), ln_in_b.reshape(1, D),
      w_in.astype(jnp.bfloat16), w_out.astype(jnp.bfloat16), rot, decay, zeta_b, xi_b, cdecay,
      bucket, sink_row, ln_mix_g.reshape(1, D), ln_mix_b.reshape(1, D))


def _router_kernel(h_ref, wr_ref, rb_ref, e_ref, w_ref, rk_ref, cnt_ref, run_ref):
    f32 = jnp.float32
    R = h_ref.shape[0]
    E = N_EXPERTS
    neg = -jnp.inf

    @pl.when(pl.program_id(0) == 0)
    def _init():
        run_ref[...] = jnp.zeros_like(run_ref)

    logits = _dot_nt(wr_ref[...], h_ref[...].astype(jnp.bfloat16))
    scores = 1.0 / (1.0 + jnp.exp(-logits))
    choice = scores + rb_ref[...]
    eid = lax.broadcasted_iota(jnp.int32, (E, R), 0)

    def first_argmax(vals, ids, none):
        m = jnp.max(vals, axis=0, keepdims=True)
        idx = jnp.min(jnp.where(vals == m, ids, none), axis=0, keepdims=True)
        return m, idx

    gid = lax.broadcasted_iota(jnp.int32, (GROUP_SIZE, R), 0)
    groups, gscore = [], []
    for g in range(N_GROUPS):
        vals = choice[g * GROUP_SIZE:(g + 1) * GROUP_SIZE]
        m1, i1 = first_argmax(vals, gid, GROUP_SIZE)
        m2 = jnp.max(jnp.where(gid == i1, neg, vals), axis=0, keepdims=True)
        groups.append(vals)
        gscore.append(m1 + m2)
    kept = []
    for g in range(N_GROUPS):
        beaten = jnp.zeros((1, R), f32)
        for g2 in range(N_GROUPS):
            if g2 == g:
                continue
            ahead = (gscore[g2] > gscore[g]) | (gscore[g2] == gscore[g]) if g2 < g else gscore[g2] > gscore[g]
            beaten = beaten + jnp.where(ahead, 1.0, 0.0)
        kept.append(jnp.where(beaten < TOPK_GROUPS, groups[g], neg))
    masked = jnp.concatenate(kept, axis=0)

    idxs, wts = [], []
    picked = jnp.zeros((E, R), f32)
    for _ in range(TOP_K):
        _, idx = first_argmax(masked, eid, E)
        hit = eid == idx
        idxs.append(idx)
        wts.append(jnp.sum(jnp.where(hit, scores, 0.0), axis=0, keepdims=True))
        masked = jnp.where(hit, neg, masked)
        picked = jnp.where(hit, 1.0, picked)
    wsum = wts[0]
    for k in range(1, TOP_K):
        wsum = wsum + wts[k]

    row = lax.broadcasted_iota(jnp.int32, (R, R), 0)
    col = lax.broadcasted_iota(jnp.int32, (R, R), 1)
    earlier = jnp.where(row < col, 1.0, 0.0).astype(jnp.bfloat16)
    picked_bf = picked.astype(jnp.bfloat16)
    run = run_ref[...]
    before = _dot(picked_bf, earlier) + jnp.concatenate([run] * (R // LANES), axis=1)
    sub_k = lax.broadcasted_iota(jnp.int32, (TOP_K, R), 0)
    e_out = jnp.zeros((TOP_K, R), jnp.int32)
    w_out = jnp.zeros((TOP_K, R), f32)
    rk_out = jnp.zeros((TOP_K, R), jnp.int32)
    for k in range(TOP_K):
        rank_k = jnp.sum(jnp.where(eid == idxs[k], before, 0.0), axis=0, keepdims=True)
        e_out = jnp.where(sub_k == k, idxs[k], e_out)
        w_out = jnp.where(sub_k == k, wts[k] / wsum * ROUTED_SCALE, w_out)
        rk_out = jnp.where(sub_k == k, rank_k.astype(jnp.int32), rk_out)
    e_ref[...] = e_out
    w_ref[...] = w_out
    rk_ref[...] = rk_out
    run_ref[...] = run + _dot(picked_bf, jnp.ones((R, LANES), jnp.bfloat16))
    cnt_ref[...] = run_ref[...]


def _router(h2, w_router, router_bias):
    T, D = h2.shape
    R = ROUTE_ROWS
    E = N_EXPERTS
    return pl.pallas_call(
        _router_kernel,
        grid=(T // R,),
        in_specs=[
            pl.BlockSpec((R, D), lambda i: (i, 0)),
            pl.BlockSpec((E, D), lambda i: (0, 0)),
            pl.BlockSpec((E, R), lambda i: (0, 0)),
        ],
        out_specs=[
            pl.BlockSpec((TOP_K, R), lambda i: (0, i)),
            pl.BlockSpec((TOP_K, R), lambda i: (0, i)),
            pl.BlockSpec((TOP_K, R), lambda i: (0, i)),
            pl.BlockSpec((E, LANES), lambda i: (0, 0)),
        ],
        out_shape=[
            jax.ShapeDtypeStruct((TOP_K, T), jnp.int32),
            jax.ShapeDtypeStruct((TOP_K, T), jnp.float32),
            jax.ShapeDtypeStruct((TOP_K, T), jnp.int32),
            jax.ShapeDtypeStruct((E, LANES), jnp.float32),
        ],
        scratch_shapes=[pltpu.VMEM((E, LANES), jnp.float32)],
        compiler_params=pltpu.CompilerParams(
            dimension_semantics=("arbitrary",), vmem_limit_bytes=VMEM_LIMIT_BYTES),
    )(h2, w_router.T.astype(jnp.bfloat16), jnp.broadcast_to(router_bias.astype(jnp.float32)[:, None], (E, R)))


def _positions_kernel(row_start_ref, e_ref, rk_ref, pos_ref):
    e = e_ref[...]
    rk = rk_ref[...]

    def per_expert(i, pos):
        return jnp.where(e == i, rk + row_start_ref[i], pos)

    pos_ref[...] = lax.fori_loop(0, N_EXPERTS, per_expert, jnp.zeros_like(rk))


def _positions(e_idx, rank, row_start):
    n = e_idx.size
    shape = (n // LANES, LANES)
    block = pl.BlockSpec((POSITION_ROWS, LANES), lambda i, *_: (i, 0))
    grid_spec = pltpu.PrefetchScalarGridSpec(
        num_scalar_prefetch=1,
        grid=(shape[0] // POSITION_ROWS,),
        in_specs=[block, block],
        out_specs=block,
    )
    pos = pl.pallas_call(
        _positions_kernel,
        grid_spec=grid_spec,
        out_shape=jax.ShapeDtypeStruct(shape, jnp.int32),
    )(row_start, e_idx.reshape(shape), rank.reshape(shape))
    return pos.reshape(n)


def _dispatch_kernel(row_start_ref, cnt_ref, n_act_ref, pos_ref, h_ref, xs_ref, zero_ref, sem, zsem,
                     *, n_tokens):
    R = h_ref.shape[0] // ROW_TILE
    BM = EXPERT_ROWS
    n_blocks = xs_ref.shape[0] // (BM * ROW_TILE)
    n_pad_units = n_blocks - n_tokens * TOP_K // BM

    @pl.when(pl.program_id(0) == 0)
    def _zero_padding():
        zero_ref[...] = jnp.zeros_like(zero_ref)

        def expert_tail(e, carry):
            n_tail = pl.multiple_of(((BM - cnt_ref[e] % BM) % BM) * ROW_TILE, ROW_TILE)

            @pl.when(n_tail > 0)
            def _():
                dst = pl.multiple_of((row_start_ref[e] + cnt_ref[e]) * ROW_TILE, ROW_TILE)
                pltpu.make_async_copy(zero_ref.at[pl.ds(0, n_tail)], xs_ref.at[pl.ds(dst, n_tail)], zsem).start()
            return carry

        lax.fori_loop(0, N_EXPERTS, expert_tail, 0)

        def idle_block(i, carry):
            dst = pl.multiple_of(i * (BM * ROW_TILE), BM * ROW_TILE)
            pltpu.make_async_copy(zero_ref, xs_ref.at[pl.ds(dst, BM * ROW_TILE)], zsem).start()
            return carry

        lax.fori_loop(n_act_ref[0], n_blocks, idle_block, 0)

        def drain(i, carry):
            pltpu.make_async_copy(zero_ref, xs_ref.at[pl.ds(0, BM * ROW_TILE)], zsem).wait()
            return carry

        lax.fori_loop(0, n_pad_units, drain, 0)

    def issue(t, carry):
        src = h_ref.at[_row_tile(t)]
        for k in range(TOP_K):
            dest = pos_ref[t * TOP_K + k]
            pltpu.make_async_copy(src, xs_ref.at[_row_tile(dest)], sem).start(priority=k % 2)
        return carry

    lax.fori_loop(0, R, issue, 0)
    n = R * TOP_K * ROW_TILE
    pltpu.make_async_copy(xs_ref.at[pl.ds(0, n)], xs_ref.at[pl.ds(0, n)], sem).wait()


def _dispatch(h2, pos, row_start, cnt, n_act, n_rows):
    T = h2.shape[0] // ROW_TILE
    R = DISPATCH_ROWS
    grid_spec = pltpu.PrefetchScalarGridSpec(
        num_scalar_prefetch=3,
        grid=(T // R,),
        in_specs=[
            pl.BlockSpec((R * TOP_K,), lambda i, *_: (i,), memory_space=pltpu.SMEM),
            pl.BlockSpec((R * ROW_TILE, LANES), lambda i, *_: (i, 0)),
        ],
        out_specs=pl.BlockSpec(memory_space=pl.ANY),
        scratch_shapes=[pltpu.VMEM((EXPERT_ROWS * ROW_TILE, LANES), jnp.float32),
                        pltpu.SemaphoreType.DMA(()), pltpu.SemaphoreType.DMA(())],
    )
    return pl.pallas_call(
        functools.partial(_dispatch_kernel, n_tokens=T),
        grid_spec=grid_spec,
        out_shape=jax.ShapeDtypeStruct((n_rows * ROW_TILE, LANES), jnp.float32),
        compiler_params=pltpu.CompilerParams(
            dimension_semantics=("arbitrary",), vmem_limit_bytes=VMEM_LIMIT_BYTES),
    )(row_start, cnt, n_act, pos, h2)


SC_WINDOW = 32


def _dispatch_sc(h2, pos_kt, n_rows):
    T, D = h2.shape
    W = SC_WINDOW
    idx = _window_indices(pos_kt, W)
    idx_rows = TOP_K * W // LANES
    mesh = plsc.VectorSubcoreMesh(core_axis_name="core", subcore_axis_name="subcore")

    @pl.kernel(out_type=jax.ShapeDtypeStruct((n_rows, D), h2.dtype), mesh=mesh, scratch_types=[])
    def scatter_rows(x_hbm, i_hbm, o_hbm):
        def body(x_vmem, i_vmem):
            for k in range(TOP_K):
                r, q = divmod(k * W, LANES)
                pltpu.sync_copy(x_vmem, o_hbm.at[i_vmem.at[r, pl.ds(q, W)]])

        pltpu.emit_pipeline(
            body,
            grid=(T // W,),
            in_specs=[pl.BlockSpec((W, D), lambda i: (i, 0)),
                      pl.BlockSpec((idx_rows, LANES), lambda i: (i, 0))],
            out_specs=[],
            core_axis_name=("core", "subcore"),
            dimension_semantics=(pltpu.PARALLEL,),
        )(x_hbm, i_hbm)

    return scatter_rows(h2, idx)


def _window_indices(pos_kt, window):
    K, T = pos_kt.shape
    return pos_kt.reshape(K, T // window, window).transpose(1, 0, 2).reshape(T * K // LANES, LANES)


def _gather_rows_sc(ys, pos_kt):
    n = pos_kt.size
    W = SC_WINDOW
    ys3 = ys.reshape(ys.shape[0] // ROW_TILE, ROW_TILE, LANES)
    idx = jnp.pad(pos_kt.reshape(n // W, W), ((0, 0), (0, LANES - W)))
    mesh = plsc.VectorSubcoreMesh(core_axis_name="core", subcore_axis_name="subcore")

    @pl.kernel(out_type=jax.ShapeDtypeStruct((n, ROW_TILE, LANES), ys.dtype), mesh=mesh, scratch_types=[])
    def gather_rows(y_hbm, i_hbm, o_hbm):
        def body(i_vmem, o_vmem):
            pltpu.sync_copy(y_hbm.at[i_vmem.at[0, pl.ds(0, W)]], o_vmem)

        pltpu.emit_pipeline(
            body,
            grid=(n // W,),
            in_specs=[pl.BlockSpec((1, LANES), lambda i: (i, 0))],
            out_specs=[pl.BlockSpec((W, ROW_TILE, LANES), lambda i: (i, 0, 0))],
            core_axis_name=("core", "subcore"),
            dimension_semantics=(pltpu.PARALLEL,),
        )(i_hbm, o_hbm)

    return gather_rows(ys3, idx).reshape(n * ROW_TILE, LANES)


X_SLOTS = 4
Y_SLOTS = 3


def _experts_kernel(blk_e_ref, first_ref, slot_ref, next_e_ref, valid_ref, n_act_ref, xs_hbm, wg_hbm,
                    wu_hbm, wd_hbm, ys_hbm, x_buf, y_buf, wg_buf, wu_buf, wd_buf, wg_bf, wu_bf, wd_bf,
                    sems, x_sems, y_sems):
    i = pl.program_id(0)
    n_act = n_act_ref[0]
    bf16 = jnp.bfloat16
    blk = EXPERT_ROWS * ROW_TILE

    def x_copy(j):
        n = pl.multiple_of((valid_ref[j] + 7) // 8 * 8, 8)
        src = xs_hbm.at[pl.ds(pl.multiple_of(j * EXPERT_ROWS, EXPERT_ROWS), n)]
        return pltpu.make_async_copy(src, x_buf.at[j % X_SLOTS, pl.ds(0, n)], x_sems.at[j % X_SLOTS])

    def y_copy(j):
        n = pl.multiple_of(valid_ref[j] * ROW_TILE, ROW_TILE)
        dst = ys_hbm.at[pl.ds(pl.multiple_of(j * blk, blk), n)]
        return pltpu.make_async_copy(y_buf.at[j % Y_SLOTS, pl.ds(0, n)], dst, y_sems.at[j % Y_SLOTS])

    @pl.when(i == 0)
    def _prime():
        x_buf[...] = jnp.zeros_like(x_buf)
        for j in range(X_SLOTS - 1):
            @pl.when(j < n_act)
            def _():
                x_copy(j).start()

    @pl.when(i + (X_SLOTS - 1) < n_act)
    def _prefetch():
        x_copy(i + (X_SLOTS - 1)).start()

    def weight_copies(e, slot):
        return (pltpu.make_async_copy(wg_hbm.at[e], wg_buf.at[slot], sems.at[slot]),
                pltpu.make_async_copy(wu_hbm.at[e], wu_buf.at[slot], sems.at[slot]),
                pltpu.make_async_copy(wd_hbm.at[e], wd_buf.at[slot], sems.at[slot]))

    @pl.when((i < n_act_ref[0]) & (first_ref[i] == 1))
    def _new_expert():
        slot = slot_ref[i]

        @pl.when(i == 0)
        def _():
            for c in weight_copies(blk_e_ref[0], 0):
                c.start()

        for c in weight_copies(blk_e_ref[i], slot):
            c.wait()

        @pl.when(next_e_ref[i] >= 0)
        def _():
            for c in weight_copies(next_e_ref[i], 1 - slot):
                c.start()

        wg_bf[...] = wg_buf[slot].astype(bf16)
        wu_bf[...] = wu_buf[slot].astype(bf16)
        wd_bf[...] = wd_buf[slot].astype(bf16)

    @pl.when(i < n_act)
    def _compute():
        x_copy(i).wait()
        x = x_buf[i % X_SLOTS].astype(bf16)
        g = _dot(x, wg_bf[...])
        u = _dot(x, wu_bf[...])
        a = (_silu(g) * u).astype(bf16)
        y = _dot(a, wd_bf[...])

        @pl.when(i >= Y_SLOTS)
        def _():
            y_copy(i - Y_SLOTS).wait()

        _store_rows(y_buf, y, lead=(i % Y_SLOTS,))
        y_copy(i).start()

    @pl.when(i == n_act - 1)
    def _drain():
        for d in range(Y_SLOTS):
            @pl.when(i - d >= 0)
            def _():
                y_copy(i - d).wait()


def _experts(xs, blk_e, n_act, row_start, cnt, w_gate, w_up, w_down):
    D = D_MODEL
    BM = EXPERT_ROWS
    F = EXPERT_DIM
    n_blocks = xs.shape[0] // BM
    blk_in_expert = jnp.arange(n_blocks, dtype=jnp.int32) - row_start[blk_e] // BM
    valid = jnp.clip(cnt[blk_e] - blk_in_expert * BM, 0, BM).astype(jnp.int32)
    ids = jnp.arange(n_blocks, dtype=jnp.int32)
    active = ids < n_act[0]
    first = active & ((ids == 0) | (blk_e != jnp.roll(blk_e, 1)))
    slot = ((jnp.cumsum(first.astype(jnp.int32)) - 1) % 2).astype(jnp.int32)
    first_pos = jnp.where(first, ids, n_blocks)
    later_first = lax.cummin(jnp.concatenate([first_pos[1:], jnp.full((1,), n_blocks, jnp.int32)]), reverse=True)
    next_e = jnp.where(later_first < n_blocks, blk_e[jnp.minimum(later_first, n_blocks - 1)], -1).astype(jnp.int32)

    grid_spec = pltpu.PrefetchScalarGridSpec(
        num_scalar_prefetch=6,
        grid=(n_blocks,),
        in_specs=[pl.BlockSpec(memory_space=pl.ANY)] * 4,
        out_specs=pl.BlockSpec(memory_space=pl.ANY),
        scratch_shapes=[
            pltpu.VMEM((X_SLOTS, BM, D), jnp.float32),
            pltpu.VMEM((Y_SLOTS, BM * ROW_TILE, LANES), jnp.float32),
            pltpu.VMEM((2, D, F), jnp.float32), pltpu.VMEM((2, D, F), jnp.float32),
            pltpu.VMEM((2, F, D), jnp.float32),
            pltpu.VMEM((D, F), jnp.bfloat16), pltpu.VMEM((D, F), jnp.bfloat16),
            pltpu.VMEM((F, D), jnp.bfloat16),
            pltpu.SemaphoreType.DMA((2,)), pltpu.SemaphoreType.DMA((X_SLOTS,)),
            pltpu.SemaphoreType.DMA((Y_SLOTS,)),
        ],
    )
    return pl.pallas_call(
        _experts_kernel,
        grid_spec=grid_spec,
        out_shape=jax.ShapeDtypeStruct((xs.shape[0] * ROW_TILE, LANES), jnp.float32),
        compiler_params=pltpu.CompilerParams(
            dimension_semantics=("arbitrary",), vmem_limit_bytes=VMEM_LIMIT_BYTES),
    )(blk_e, first.astype(jnp.int32), slot, next_e, valid, n_act, xs, w_gate, w_up, w_down)


def _finish_kernel(h_ref, part_ref, w_ref, *rest):
    slabs = rest[:SC_COMBINE_SLOTS]
    g_ref, b_ref, out_ref = rest[SC_COMBINE_SLOTS:]
    R = h_ref.shape[0]
    w = w_ref[...]
    ffn = part_ref[...]
    for k in range(SC_COMBINE_SLOTS):
        ffn = ffn + _load_rows(slabs[k], R) * w[:, k:k + 1]
    out_ref[...] = _layer_norm(DEEPNORM_ALPHA * h_ref[...] + ffn, g_ref[...], b_ref[...])


def _finish(h2, partial, top_w, gathered, ln_g, ln_b):
    T, D = h2.shape
    R = FINISH_ROWS
    rows = pl.BlockSpec((R, D), lambda i: (i, 0))
    vec = pl.BlockSpec((1, D), lambda i: (0, 0))
    slab = lambda k: pl.BlockSpec((R * ROW_TILE, LANES), lambda i: (k * (T // R) + i, 0))
    return pl.pallas_call(
        _finish_kernel,
        grid=(T // R,),
        in_specs=[rows, rows, pl.BlockSpec((R, TOP_K), lambda i: (i, 0))]
        + [slab(k) for k in range(SC_COMBINE_SLOTS)] + [vec, vec],
        out_specs=rows,
        out_shape=jax.ShapeDtypeStruct((T, D), jnp.float32),
        compiler_params=pltpu.CompilerParams(
            dimension_semantics=("arbitrary",), vmem_limit_bytes=VMEM_LIMIT_BYTES),
    )(h2, partial, top_w, *([gathered] * SC_COMBINE_SLOTS), ln_g.reshape(1, D), ln_b.reshape(1, D))


SC_COMBINE_SLOTS = 6


def _combine_kernel(pos_ref, pos_next_ref, h_ref, w_ref, ys_ref, wsg_ref, wsu_ref, wsd_ref,
                    out_ref, buf_ref, sems):
    R = h_ref.shape[0]
    i = pl.program_id(0)
    slot = i % 2

    def gather(p_ref, s):
        def issue(t, carry):
            for k in range(SC_COMBINE_SLOTS, TOP_K):
                src = p_ref[t * TOP_K + k]
                pltpu.make_async_copy(ys_ref.at[_row_tile(src)], buf_ref.at[s, k - SC_COMBINE_SLOTS, _row_tile(t)],
                                      sems.at[s]).start(priority=k % 2)
            return carry

        lax.fori_loop(0, R, issue, 0)

    @pl.when(i == 0)
    def _():
        gather(pos_ref, 0)

    @pl.when(i + 1 < pl.num_programs(0))
    def _():
        gather(pos_next_ref, 1 - slot)

    h = h_ref[...]
    hb = h.astype(jnp.bfloat16)
    act = (_silu(_dot(hb, wsg_ref[...])) * _dot(hb, wsu_ref[...])).astype(jnp.bfloat16)
    ffn = _dot(act, wsd_ref[...])
    pltpu.make_async_copy(buf_ref.at[slot], buf_ref.at[slot], sems.at[slot]).wait()
    w = w_ref[...]
    for k in range(SC_COMBINE_SLOTS, TOP_K):
        ffn = ffn + _load_rows(buf_ref, R, lead=(slot, k - SC_COMBINE_SLOTS)) * w[:, k:k + 1]
    out_ref[...] = ffn


def _combine(h2, pos, top_w, ys, ws_gate, ws_up, ws_down):
    T, D = h2.shape
    R = COMBINE_ROWS
    F = SHARED_DIM
    bf16 = jnp.bfloat16
    const = lambda shape: pl.BlockSpec(shape, lambda i: (0,) * len(shape))
    return pl.pallas_call(
        _combine_kernel,
        grid=(T // R,),
        in_specs=[
            pl.BlockSpec((R * TOP_K,), lambda i: (i,), memory_space=pltpu.SMEM),
            pl.BlockSpec((R * TOP_K,), lambda i: (jnp.minimum(i + 1, T // R - 1),), memory_space=pltpu.SMEM),
            pl.BlockSpec((R, D), lambda i: (i, 0)),
            pl.BlockSpec((R, TOP_K), lambda i: (i, 0)),
            pl.BlockSpec(memory_space=pl.ANY),
            const((D, F)), const((D, F)), const((F, D)),
        ],
        out_specs=pl.BlockSpec((R, D), lambda i: (i, 0)),
        scratch_shapes=[pltpu.VMEM((2, TOP_K - SC_COMBINE_SLOTS, R * ROW_TILE, LANES), jnp.float32),
                        pltpu.SemaphoreType.DMA((2,))],
        out_shape=jax.ShapeDtypeStruct((T, D), jnp.float32),
        compiler_params=pltpu.CompilerParams(
            dimension_semantics=("arbitrary",), vmem_limit_bytes=VMEM_LIMIT_BYTES),
    )(pos, pos, h2, top_w, ys, ws_gate.astype(bf16), ws_up.astype(bf16), ws_down.astype(bf16))


def _moe(h2, w_router, router_bias, w_gate, w_up, w_down, ws_gate, ws_up, ws_down, ln_g, ln_b):
    T = h2.shape[0]
    E = N_EXPERTS
    BM = EXPERT_ROWS
    e_idx, top_w, rank, counts = _router(h2, w_router, router_bias)
    cnt = counts[:, 0].astype(jnp.int32)
    nblk = (cnt + BM - 1) // BM
    blk_end = jnp.cumsum(nblk)
    row_start = ((blk_end - nblk) * BM).astype(jnp.int32)
    n_blocks = T * TOP_K // BM + E
    n_act = blk_end[-1:].astype(jnp.int32)
    blk_ids = jnp.minimum(jnp.arange(n_blocks, dtype=jnp.int32), n_act[0] - 1)
    blk_e = jnp.minimum(jnp.sum(blk_end[None, :] <= blk_ids[:, None], axis=1), E - 1).astype(jnp.int32)
    pos_kt = _positions(e_idx, rank, row_start).reshape(TOP_K, T)
    pos = pos_kt.T.reshape(T * TOP_K)
    xs = _dispatch_sc(h2, pos_kt, n_blocks * BM)
    ys = _experts(xs, blk_e, n_act, row_start, cnt, w_gate, w_up, w_down)
    gathered = _gather_rows_sc(ys, pos_kt[:SC_COMBINE_SLOTS])
    w_tk = top_w.T
    partial = _combine(h2, pos, w_tk, ys, ws_gate, ws_up, ws_down)
    return _finish(h2, partial, w_tk, gathered, ln_g, ln_b)


def kernel(x, ln_in_g, ln_in_b, w_in, w_out, rel_bias, attn_sinks, ln_mix_g, ln_mix_b, w_router,
           router_bias, w_gate, w_up, w_down, ws_gate, ws_up, ws_down, ln_ffn_g, ln_ffn_b):
    B, S, D = x.shape
    h = _mixer(x, ln_in_g, ln_in_b, w_in[0], w_out[0], rel_bias, attn_sinks[0], ln_mix_g[0], ln_mix_b[0])
    out = _moe(h, w_router[0], router_bias[0], w_gate[0], w_up[0], w_down[0],
               ws_gate[0], ws_up[0], ws_down[0], ln_ffn_g[0], ln_ffn_b[0])
    return out.reshape(B, S, D)
```

```python
import functools
import math

import jax
import jax.numpy as jnp
from jax import lax
from jax.experimental import pallas as pl
from jax.experimental.pallas import tpu as pltpu
from jax.experimental.pallas import tpu_sc as plsc

D_MODEL = 1024
DEPTH = 1
RET_HEADS = 4
RET_QK_DIM = 64
RET_V_DIM = 128
RET_CHUNK = 128
RET_WIDTH = RET_HEADS * RET_V_DIM
ROPE_BASE = 10000.0
SWA_HEADS = 8
SWA_KV_HEADS = 2
SWA_GROUP = SWA_HEADS // SWA_KV_HEADS
SWA_HEAD_DIM = 64
SWA_WINDOW = 128
SWA_WIDTH = SWA_HEADS * SWA_HEAD_DIM
MIX_WIDTH = RET_WIDTH + SWA_WIDTH
RQK = RET_HEADS * RET_QK_DIM
SKV = SWA_KV_HEADS * SWA_HEAD_DIM
IN_SIZES = (RQK, RQK, RET_WIDTH, RET_WIDTH, SWA_WIDTH, SKV, SKV)
IN_OFFS = tuple(sum(IN_SIZES[:i]) for i in range(len(IN_SIZES)))
IN_WIDTH = sum(IN_SIZES)
REL_BUCKETS = 32
REL_MAX_DIST = 128
N_EXPERTS = 256
TOP_K = 8
N_GROUPS = 8
GROUP_SIZE = N_EXPERTS // N_GROUPS
TOPK_GROUPS = 4
EXPERT_DIM = 256
SHARED_DIM = 256
ROUTED_SCALE = 2.5
LN_EPS = 1e-5
GN_EPS = 1e-6
DEEPNORM_ALPHA = (2 * DEPTH) ** 0.25
MASK_VALUE = -1e30

VMEM_LIMIT_BYTES = 56 * 1024 * 1024

MIX_ROWS = 256
ROUTE_ROWS = 256
DISPATCH_ROWS = 256
EXPERT_ROWS = 256
COMBINE_ROWS = 256
POSITION_ROWS = 64
FINISH_ROWS = 512


def _layer_norm(x, g, b):
    mu = jnp.mean(x, axis=-1, keepdims=True)
    xc = x - mu
    var = jnp.mean(xc * xc, axis=-1, keepdims=True)
    return xc * lax.rsqrt(var + LN_EPS) * g + b


def _dot(a, b):
    return jnp.dot(a, b, preferred_element_type=jnp.float32)


def _dot_nt(a, b):
    return lax.dot_general(a, b, (((1,), (1,)), ((), ())), preferred_element_type=jnp.float32)


def _dot_tn(a, b):
    return lax.dot_general(a, b, (((0,), (0,)), ((), ())), preferred_element_type=jnp.float32)


def _silu(x):
    return x * (1.0 / (1.0 + jnp.exp(-x)))


LANES = 128
ROW_TILE = D_MODEL // LANES


def _load_rows(ref, n_rows, lead=()):
    return jnp.concatenate([ref[lead + (pl.ds(s, n_rows, stride=ROW_TILE), slice(None))]
                            for s in range(ROW_TILE)], axis=1)


def _store_rows(ref, val, lead=()):
    n_rows = val.shape[0]
    for s in range(ROW_TILE):
        ref[lead + (pl.ds(s, n_rows, stride=ROW_TILE), slice(None))] = val[:, s * LANES:(s + 1) * LANES]


def _row_tile(r):
    return pl.ds(pl.multiple_of(r * ROW_TILE, ROW_TILE), ROW_TILE)


def _swap_halves(x):
    n = x.shape[-1]
    half = RET_QK_DIM // 2
    lane = lax.broadcasted_iota(jnp.int32, x.shape, 1)
    from_right = pltpu.roll(x, n - half, axis=1)
    from_left = pltpu.roll(x, half, axis=1)
    return jnp.where((lane % RET_QK_DIM) < half, from_right, from_left)


def _mixer_kernel(rel_bias_ref, x_ref, g_in_ref, b_in_ref, w_in_ref, w_out_ref, rot_ref, decay_ref,
                  zeta_ref, xi_ref, cdecay_ref, bucket_ref, sink_ref, g_mix_ref, b_mix_ref,
                  h2_ref, state_ref, kprev_ref, vprev_ref, bias_ref):
    b_id = pl.program_id(0)
    c_id = pl.program_id(1)
    W = SWA_WINDOW

    @pl.when((b_id == 0) & (c_id == 0))
    def _build_bias():
        bucket = bucket_ref[...]
        for h in range(SWA_HEADS):
            acc = jnp.full((2 * W, W), MASK_VALUE, jnp.float32)
            for b in range(REL_BUCKETS):
                acc = jnp.where(bucket == b, rel_bias_ref[b, h], acc)
            kh, g = divmod(h, SWA_GROUP)
            bias_ref[kh, :, g * W:(g + 1) * W] = acc

    @pl.when(c_id == 0)
    def _reset():
        state_ref[...] = jnp.zeros_like(state_ref)
        kprev_ref[...] = jnp.zeros_like(kprev_ref)
        vprev_ref[...] = jnp.zeros_like(vprev_ref)

    h = _layer_norm(x_ref[...], g_in_ref[...], b_in_ref[...])
    proj = _dot(h.astype(jnp.bfloat16), w_in_ref[...])

    o_q, o_k, o_v, o_g, o_sq, o_sk, o_sv = IN_OFFS
    cos_t = rot_ref[:, :RQK]
    sin_t = rot_ref[:, RQK:]
    q_all = proj[:, o_q:o_q + RQK]
    k_all = proj[:, o_k:o_k + RQK]
    q_rot = q_all * cos_t + _swap_halves(q_all) * sin_t
    k_rot = (k_all * cos_t + _swap_halves(k_all) * sin_t) * (RET_QK_DIM ** -0.5)

    n_sub = x_ref.shape[0] // RET_CHUNK
    states = [state_ref[hh] for hh in range(RET_HEADS)]
    k_prev = kprev_ref[...]
    v_prev = vprev_ref[...]
    cat_rows = []
    for s in range(n_sub):
        r0 = s * RET_CHUNK
        rows = slice(r0, r0 + RET_CHUNK)
        pieces = []
        for hh in range(RET_HEADS):
            qk = slice(hh * RET_QK_DIM, (hh + 1) * RET_QK_DIM)
            vv = slice(o_v + hh * RET_V_DIM, o_v + (hh + 1) * RET_V_DIM)
            gg = slice(o_g + hh * RET_V_DIM, o_g + (hh + 1) * RET_V_DIM)
            q = q_rot[rows, qk].astype(jnp.bfloat16)
            k32 = k_rot[rows, qk]
            v = proj[rows, vv].astype(jnp.bfloat16)
            scores = _dot_nt(q, k32.astype(jnp.bfloat16)) * decay_ref[hh]
            intra = _dot(scores.astype(jnp.bfloat16), v)
            inter = _dot(q, states[hh].astype(jnp.bfloat16)) * xi_ref[hh]
            ret = intra + inter
            kz = (k32 * zeta_ref[hh]).astype(jnp.bfloat16)
            states[hh] = states[hh] * cdecay_ref[hh] + _dot_tn(kz, v)
            mu = jnp.mean(ret, axis=-1, keepdims=True)
            rc = ret - mu
            var = jnp.mean(rc * rc, axis=-1, keepdims=True)
            normed = rc * lax.rsqrt(var + GN_EPS)
            pieces.append((_silu(proj[rows, gg]) * normed).astype(jnp.bfloat16))
        k_cur = proj[rows, o_sk:o_sk + SKV].astype(jnp.bfloat16)
        v_cur = proj[rows, o_sv:o_sv + SKV].astype(jnp.bfloat16)
        for kh in range(SWA_KV_HEADS):
            kv = slice(kh * SWA_HEAD_DIM, (kh + 1) * SWA_HEAD_DIM)
            q4 = jnp.concatenate(
                [proj[rows, o_sq + (kh * SWA_GROUP + g) * SWA_HEAD_DIM:
                      o_sq + (kh * SWA_GROUP + g + 1) * SWA_HEAD_DIM] for g in range(SWA_GROUP)],
                axis=0) * (SWA_HEAD_DIM ** -0.5)
            kcat = jnp.concatenate([k_prev[:, kv], k_cur[:, kv]], axis=0)
            vcat = jnp.concatenate([v_prev[:, kv], v_cur[:, kv]], axis=0)
            logits = _dot_nt(kcat, q4.astype(jnp.bfloat16)) + bias_ref[kh]
            if s == 0:
                key = lax.broadcasted_iota(jnp.int32, logits.shape, 0)
                logits = logits + jnp.where((key < W) & (c_id == 0), MASK_VALUE, 0.0)
            sink = sink_ref[kh]
            m = jnp.maximum(jnp.max(logits, axis=0, keepdims=True), sink)
            p = jnp.exp(logits - m)
            den = jnp.sum(p, axis=0, keepdims=True) + jnp.exp(sink - m)
            probs = (p / den).astype(jnp.bfloat16)
            o4 = _dot_tn(vcat, probs)
            pieces.extend(o4[:, g * W:(g + 1) * W].T.astype(jnp.bfloat16) for g in range(SWA_GROUP))
        k_prev, v_prev = k_cur, v_cur
        cat_rows.append(jnp.concatenate(pieces, axis=1))
    for hh in range(RET_HEADS):
        state_ref[hh] = states[hh]
    kprev_ref[...] = k_prev
    vprev_ref[...] = v_prev

    mix = _dot(jnp.concatenate(cat_rows, axis=0), w_out_ref[...])
    h2_ref[...] = _layer_norm(DEEPNORM_ALPHA * h + mix, g_mix_ref[...], b_mix_ref[...])


def _t5_bucket(dist):
    n = jnp.maximum(dist, 0)
    max_exact = REL_BUCKETS // 2
    ratio = jnp.log(jnp.maximum(n, 1).astype(jnp.float32) / max_exact) / math.log(REL_MAX_DIST / max_exact)
    large = jnp.minimum(max_exact + (ratio * (REL_BUCKETS - max_exact)).astype(jnp.int32), REL_BUCKETS - 1)
    return jnp.where(n < max_exact, n, large)


def _mixer(x, ln_in_g, ln_in_b, w_in, w_out, rel_bias, sinks, ln_mix_g, ln_mix_b):
    B, S, D = x.shape
    R = MIX_ROWS
    C = RET_CHUNK
    W = SWA_WINDOW
    f32 = jnp.float32
    half = RET_QK_DIM // 2
    inv = ROPE_BASE ** (-jnp.arange(half, dtype=f32) / half)
    ang = jnp.arange(S, dtype=f32)[:, None] * inv[None, :]
    cos, sin = jnp.cos(ang), jnp.sin(ang)
    cos_t = jnp.tile(jnp.concatenate([cos, cos], axis=-1), (1, RET_HEADS))
    sin_t = jnp.tile(jnp.concatenate([-sin, sin], axis=-1), (1, RET_HEADS))
    rot = jnp.concatenate([cos_t, sin_t], axis=-1)
    log_gamma = jnp.log(1.0 - 2.0 ** (-5.0 - jnp.arange(RET_HEADS, dtype=f32)))
    idx = jnp.arange(C, dtype=f32)
    diff = idx[:, None] - idx[None, :]
    decay = jnp.where(diff[None] >= 0, jnp.exp(jnp.maximum(diff, 0.0)[None] * log_gamma[:, None, None]), 0.0)
    zeta = jnp.exp((C - 1.0 - idx)[None, :] * log_gamma[:, None])
    xi = jnp.exp((idx + 1.0)[None, :] * log_gamma[:, None])
    zeta_b = jnp.broadcast_to(zeta[:, :, None], (RET_HEADS, C, RET_QK_DIM))
    xi_b = jnp.broadcast_to(xi[:, :, None], (RET_HEADS, C, RET_V_DIM))
    cdecay = jnp.broadcast_to(jnp.exp(C * log_gamma)[:, None, None], (RET_HEADS, RET_QK_DIM, RET_V_DIM))
    i = jnp.arange(W)
    j = jnp.arange(2 * W)
    dist = i[:, None] + W - j[None, :]
    bucket = jnp.where((dist >= 0) & (dist < W), _t5_bucket(dist), -1).astype(jnp.int32).T
    sink_row = jnp.repeat(sinks.astype(f32), W).reshape(SWA_KV_HEADS, 1, SWA_GROUP * W)

    const = lambda shape: pl.BlockSpec(shape, lambda b, c, *_: (0,) * len(shape))
    grid_spec = pltpu.PrefetchScalarGridSpec(
        num_scalar_prefetch=1,
        grid=(B, S // R),
        in_specs=[
            pl.BlockSpec((None, R, D), lambda b, c, *_: (b, c, 0)),
            const((1, D)), const((1, D)),
            const((D, IN_WIDTH)), const((MIX_WIDTH, D)),
            pl.BlockSpec((R, 2 * RQK), lambda b, c, *_: (c, 0)),
            const((RET_HEADS, C, C)), const((RET_HEADS, C, RET_QK_DIM)), const((RET_HEADS, C, RET_V_DIM)),
            const((RET_HEADS, RET_QK_DIM, RET_V_DIM)),
            const((2 * W, W)), const((SWA_KV_HEADS, 1, SWA_GROUP * W)),
            const((1, D)), const((1, D)),
        ],
        out_specs=pl.BlockSpec((R, D), lambda b, c, *_: (b * (S // R) + c, 0)),
        scratch_shapes=[
            pltpu.VMEM((RET_HEADS, RET_QK_DIM, RET_V_DIM), f32),
            pltpu.VMEM((W, SKV), jnp.bfloat16),
            pltpu.VMEM((W, SKV), jnp.bfloat16),
            pltpu.VMEM((SWA_KV_HEADS, 2 * W, SWA_GROUP * W), f32),
        ],
    )
    return pl.pallas_call(
        _mixer_kernel,
        grid_spec=grid_spec,
        out_shape=jax.ShapeDtypeStruct((B * S, D), f32),
        compiler_params=pltpu.CompilerParams(
            dimension_semantics=("arbitrary", "arbitrary"), vmem_limit_bytes=VMEM_LIMIT_BYTES),
    )(rel_bias.astype(f32), x, ln_in_g.reshape(1, D), ln_in_b.reshape(1, D),
      w_in.astype(jnp.bfloat16), w_out.astype(jnp.bfloat16), rot, decay, zeta_b, xi_b, cdecay,
      bucket, sink_row, ln_mix_g.reshape(1, D), ln_mix_b.reshape(1, D))


def _router_kernel(h_ref, wr_ref, rb_ref, e_ref, w_ref, rk_ref, cnt_ref, run_ref):
    f32 = jnp.float32
    R = h_ref.shape[0]
    E = N_EXPERTS
    neg = -jnp.inf

    @pl.when(pl.program_id(0) == 0)
    def _init():
        run_ref[...] = jnp.zeros_like(run_ref)

    logits = _dot_nt(wr_ref[...], h_ref[...].astype(jnp.bfloat16))
    scores = 1.0 / (1.0 + jnp.exp(-logits))
    choice = scores + rb_ref[...]
    eid = lax.broadcasted_iota(jnp.int32, (E, R), 0)

    def first_argmax(vals, ids, none):
        m = jnp.max(vals, axis=0, keepdims=True)
        idx = jnp.min(jnp.where(vals == m, ids, none), axis=0, keepdims=True)
        return m, idx

    gid = lax.broadcasted_iota(jnp.int32, (GROUP_SIZE, R), 0)
    groups, gscore = [], []
    for g in range(N_GROUPS):
        vals = choice[g * GROUP_SIZE:(g + 1) * GROUP_SIZE]
        m1, i1 = first_argmax(vals, gid, GROUP_SIZE)
        m2 = jnp.max(jnp.where(gid == i1, neg, vals), axis=0, keepdims=True)
        groups.append(vals)
        gscore.append(m1 + m2)
    kept = []
    for g in range(N_GROUPS):
        beaten = jnp.zeros((1, R), f32)
        for g2 in range(N_GROUPS):
            if g2 == g:
                continue
            ahead = (gscore[g2] > gscore[g]) | (gscore[g2] == gscore[g]) if g2 < g else gscore[g2] > gscore[g]
            beaten = beaten + jnp.where(ahead, 1.0, 0.0)
        kept.append(jnp.where(beaten < TOPK_GROUPS, groups[g], neg))
    masked = jnp.concatenate(kept, axis=0)

    idxs, wts = [], []
    picked = jnp.zeros((E, R), f32)
    for _ in range(TOP_K):
        _, idx = first_argmax(masked, eid, E)
        hit = eid == idx
        idxs.append(idx)
        wts.append(jnp.sum(jnp.where(hit, scores, 0.0), axis=0, keepdims=True))
        masked = jnp.where(hit, neg, masked)
        picked = jnp.where(hit, 1.0, picked)
    wsum = wts[0]
    for k in range(1, TOP_K):
        wsum = wsum + wts[k]

    row = lax.broadcasted_iota(jnp.int32, (R, R), 0)
    col = lax.broadcasted_iota(jnp.int32, (R, R), 1)
    earlier = jnp.where(row < col, 1.0, 0.0).astype(jnp.bfloat16)
    picked_bf = picked.astype(jnp.bfloat16)
    run = run_ref[...]
    before = _dot(picked_bf, earlier) + jnp.concatenate([run] * (R // LANES), axis=1)
    sub_k = lax.broadcasted_iota(jnp.int32, (TOP_K, R), 0)
    e_out = jnp.zeros((TOP_K, R), jnp.int32)
    w_out = jnp.zeros((TOP_K, R), f32)
    rk_out = jnp.zeros((TOP_K, R), jnp.int32)
    for k in range(TOP_K):
        rank_k = jnp.sum(jnp.where(eid == idxs[k], before, 0.0), axis=0, keepdims=True)
        e_out = jnp.where(sub_k == k, idxs[k], e_out)
        w_out = jnp.where(sub_k == k, wts[k] / wsum * ROUTED_SCALE, w_out)
        rk_out = jnp.where(sub_k == k, rank_k.astype(jnp.int32), rk_out)
    e_ref[...] = e_out
    w_ref[...] = w_out
    rk_ref[...] = rk_out
    run_ref[...] = run + _dot(picked_bf, jnp.ones((R, LANES), jnp.bfloat16))
    cnt_ref[...] = run_ref[...]


def _router(h2, w_router, router_bias):
    T, D = h2.shape
    R = ROUTE_ROWS
    E = N_EXPERTS
    return pl.pallas_call(
        _router_kernel,
        grid=(T // R,),
        in_specs=[
            pl.BlockSpec((R, D), lambda i: (i, 0)),
            pl.BlockSpec((E, D), lambda i: (0, 0)),
            pl.BlockSpec((E, R), lambda i: (0, 0)),
        ],
        out_specs=[
            pl.BlockSpec((TOP_K, R), lambda i: (0, i)),
            pl.BlockSpec((TOP_K, R), lambda i: (0, i)),
            pl.BlockSpec((TOP_K, R), lambda i: (0, i)),
            pl.BlockSpec((E, LANES), lambda i: (0, 0)),
        ],
        out_shape=[
            jax.ShapeDtypeStruct((TOP_K, T), jnp.int32),
            jax.ShapeDtypeStruct((TOP_K, T), jnp.float32),
            jax.ShapeDtypeStruct((TOP_K, T), jnp.int32),
            jax.ShapeDtypeStruct((E, LANES), jnp.float32),
        ],
        scratch_shapes=[pltpu.VMEM((E, LANES), jnp.float32)],
        compiler_params=pltpu.CompilerParams(
            dimension_semantics=("arbitrary",), vmem_limit_bytes=VMEM_LIMIT_BYTES),
    )(h2, w_router.T.astype(jnp.bfloat16), jnp.broadcast_to(router_bias.astype(jnp.float32)[:, None], (E, R)))


def _positions_kernel(row_start_ref, e_ref, rk_ref, pos_ref):
    e = e_ref[...]
    rk = rk_ref[...]

    def per_expert(i, pos):
        return jnp.where(e == i, rk + row_start_ref[i], pos)

    pos_ref[...] = lax.fori_loop(0, N_EXPERTS, per_expert, jnp.zeros_like(rk))


def _positions(e_idx, rank, row_start):
    n = e_idx.size
    shape = (n // LANES, LANES)
    block = pl.BlockSpec((POSITION_ROWS, LANES), lambda i, *_: (i, 0))
    grid_spec = pltpu.PrefetchScalarGridSpec(
        num_scalar_prefetch=1,
        grid=(shape[0] // POSITION_ROWS,),
        in_specs=[block, block],
        out_specs=block,
    )
    pos = pl.pallas_call(
        _positions_kernel,
        grid_spec=grid_spec,
        out_shape=jax.ShapeDtypeStruct(shape, jnp.int32),
    )(row_start, e_idx.reshape(shape), rank.reshape(shape))
    return pos.reshape(n)


def _dispatch_kernel(row_start_ref, cnt_ref, n_act_ref, pos_ref, h_ref, xs_ref, zero_ref, sem, zsem,
                     *, n_tokens):
    R = h_ref.shape[0] // ROW_TILE
    BM = EXPERT_ROWS
    n_blocks = xs_ref.shape[0] // (BM * ROW_TILE)
    n_pad_units = n_blocks - n_tokens * TOP_K // BM

    @pl.when(pl.program_id(0) == 0)
    def _zero_padding():
        zero_ref[...] = jnp.zeros_like(zero_ref)

        def expert_tail(e, carry):
            n_tail = pl.multiple_of(((BM - cnt_ref[e] % BM) % BM) * ROW_TILE, ROW_TILE)

            @pl.when(n_tail > 0)
            def _():
                dst = pl.multiple_of((row_start_ref[e] + cnt_ref[e]) * ROW_TILE, ROW_TILE)
                pltpu.make_async_copy(zero_ref.at[pl.ds(0, n_tail)], xs_ref.at[pl.ds(dst, n_tail)], zsem).start()
            return carry

        lax.fori_loop(0, N_EXPERTS, expert_tail, 0)

        def idle_block(i, carry):
            dst = pl.multiple_of(i * (BM * ROW_TILE), BM * ROW_TILE)
            pltpu.make_async_copy(zero_ref, xs_ref.at[pl.ds(dst, BM * ROW_TILE)], zsem).start()
            return carry

        lax.fori_loop(n_act_ref[0], n_blocks, idle_block, 0)

        def drain(i, carry):
            pltpu.make_async_copy(zero_ref, xs_ref.at[pl.ds(0, BM * ROW_TILE)], zsem).wait()
            return carry

        lax.fori_loop(0, n_pad_units, drain, 0)

    def issue(t, carry):
        src = h_ref.at[_row_tile(t)]
        for k in range(TOP_K):
            dest = pos_ref[t * TOP_K + k]
            pltpu.make_async_copy(src, xs_ref.at[_row_tile(dest)], sem).start(priority=k % 2)
        return carry

    lax.fori_loop(0, R, issue, 0)
    n = R * TOP_K * ROW_TILE
    pltpu.make_async_copy(xs_ref.at[pl.ds(0, n)], xs_ref.at[pl.ds(0, n)], sem).wait()


def _dispatch(h2, pos, row_start, cnt, n_act, n_rows):
    T = h2.shape[0] // ROW_TILE
    R = DISPATCH_ROWS
    grid_spec = pltpu.PrefetchScalarGridSpec(
        num_scalar_prefetch=3,
        grid=(T // R,),
        in_specs=[
            pl.BlockSpec((R * TOP_K,), lambda i, *_: (i,), memory_space=pltpu.SMEM),
            pl.BlockSpec((R * ROW_TILE, LANES), lambda i, *_: (i, 0)),
        ],
        out_specs=pl.BlockSpec(memory_space=pl.ANY),
        scratch_shapes=[pltpu.VMEM((EXPERT_ROWS * ROW_TILE, LANES), jnp.float32),
                        pltpu.SemaphoreType.DMA(()), pltpu.SemaphoreType.DMA(())],
    )
    return pl.pallas_call(
        functools.partial(_dispatch_kernel, n_tokens=T),
        grid_spec=grid_spec,
        out_shape=jax.ShapeDtypeStruct((n_rows * ROW_TILE, LANES), jnp.float32),
        compiler_params=pltpu.CompilerParams(
            dimension_semantics=("arbitrary",), vmem_limit_bytes=VMEM_LIMIT_BYTES),
    )(row_start, cnt, n_act, pos, h2)


SC_WINDOW = 32


def _dispatch_sc(h2, pos_kt, n_rows):
    T, D = h2.shape
    W = SC_WINDOW
    idx = _window_indices(pos_kt, W)
    idx_rows = TOP_K * W // LANES
    mesh = plsc.VectorSubcoreMesh(core_axis_name="core", subcore_axis_name="subcore")

    @pl.kernel(out_type=jax.ShapeDtypeStruct((n_rows, D), h2.dtype), mesh=mesh, scratch_types=[])
    def scatter_rows(x_hbm, i_hbm, o_hbm):
        def body(x_vmem, i_vmem):
            for k in range(TOP_K):
                r, q = divmod(k * W, LANES)
                pltpu.sync_copy(x_vmem, o_hbm.at[i_vmem.at[r, pl.ds(q, W)]])

        pltpu.emit_pipeline(
            body,
            grid=(T // W,),
            in_specs=[pl.BlockSpec((W, D), lambda i: (i, 0)),
                      pl.BlockSpec((idx_rows, LANES), lambda i: (i, 0))],
            out_specs=[],
            core_axis_name=("core", "subcore"),
            dimension_semantics=(pltpu.PARALLEL,),
        )(x_hbm, i_hbm)

    return scatter_rows(h2, idx)


def _window_indices(pos_kt, window):
    K, T = pos_kt.shape
    return pos_kt.reshape(K, T // window, window).transpose(1, 0, 2).reshape(T * K // LANES, LANES)


def _gather_rows_sc(ys, pos_kt):
    n = pos_kt.size
    W = SC_WINDOW
    ys3 = ys.reshape(ys.shape[0] // ROW_TILE, ROW_TILE, LANES)
    idx = jnp.pad(pos_kt.reshape(n // W, W), ((0, 0), (0, LANES - W)))
    mesh = plsc.VectorSubcoreMesh(core_axis_name="core", subcore_axis_name="subcore")

    @pl.kernel(out_type=jax.ShapeDtypeStruct((n, ROW_TILE, LANES), ys.dtype), mesh=mesh, scratch_types=[])
    def gather_rows(y_hbm, i_hbm, o_hbm):
        def body(i_vmem, o_vmem):
            pltpu.sync_copy(y_hbm.at[i_vmem.at[0, pl.ds(0, W)]], o_vmem)

        pltpu.emit_pipeline(
            body,
            grid=(n // W,),
            in_specs=[pl.BlockSpec((1, LANES), lambda i: (i, 0))],
            out_specs=[pl.BlockSpec((W, ROW_TILE, LANES), lambda i: (i, 0, 0))],
            core_axis_name=("core", "subcore"),
            dimension_semantics=(pltpu.PARALLEL,),
        )(i_hbm, o_hbm)

    return gather_rows(ys3, idx).reshape(n * ROW_TILE, LANES)


X_SLOTS = 4
Y_SLOTS = 3


def _experts_kernel(blk_e_ref, first_ref, slot_ref, next_e_ref, valid_ref, n_act_ref, xs_hbm, wg_hbm,
                    wu_hbm, wd_hbm, ys_hbm, x_buf, y_buf, wg_buf, wu_buf, wd_buf, wg_bf, wu_bf, wd_bf,
                    sems, x_sems, y_sems):
    i = pl.program_id(0)
    n_act = n_act_ref[0]
    bf16 = jnp.bfloat16
    blk = EXPERT_ROWS * ROW_TILE

    def x_copy(j):
        n = pl.multiple_of((valid_ref[j] + 7) // 8 * 8, 8)
        src = xs_hbm.at[pl.ds(pl.multiple_of(j * EXPERT_ROWS, EXPERT_ROWS), n)]
        return pltpu.make_async_copy(src, x_buf.at[j % X_SLOTS, pl.ds(0, n)], x_sems.at[j % X_SLOTS])

    def y_copy(j):
        n = pl.multiple_of(valid_ref[j] * ROW_TILE, ROW_TILE)
        dst = ys_hbm.at[pl.ds(pl.multiple_of(j * blk, blk), n)]
        return pltpu.make_async_copy(y_buf.at[j % Y_SLOTS, pl.ds(0, n)], dst, y_sems.at[j % Y_SLOTS])

    @pl.when(i == 0)
    def _prime():
        x_buf[...] = jnp.zeros_like(x_buf)
        for j in range(X_SLOTS - 1):
            @pl.when(j < n_act)
            def _():
                x_copy(j).start()

    @pl.when(i + (X_SLOTS - 1) < n_act)
    def _prefetch():
        x_copy(i + (X_SLOTS - 1)).start()

    def weight_copies(e, slot):
        return (pltpu.make_async_copy(wg_hbm.at[e], wg_buf.at[slot], sems.at[slot]),
                pltpu.make_async_copy(wu_hbm.at[e], wu_buf.at[slot], sems.at[slot]),
                pltpu.make_async_copy(wd_hbm.at[e], wd_buf.at[slot], sems.at[slot]))

    @pl.when((i < n_act_ref[0]) & (first_ref[i] == 1))
    def _new_expert():
        slot = slot_ref[i]

        @pl.when(i == 0)
        def _():
            for c in weight_copies(blk_e_ref[0], 0):
                c.start()

        for c in weight_copies(blk_e_ref[i], slot):
            c.wait()

        @pl.when(next_e_ref[i] >= 0)
        def _():
            for c in weight_copies(next_e_ref[i], 1 - slot):
                c.start()

        wg_bf[...] = wg_buf[slot].astype(bf16)
        wu_bf[...] = wu_buf[slot].astype(bf16)
        wd_bf[...] = wd_buf[slot].astype(bf16)

    @pl.when(i < n_act)
    def _compute():
        x_copy(i).wait()
        x = x_buf[i % X_SLOTS].astype(bf16)
        g = _dot(x, wg_bf[...])
        u = _dot(x, wu_bf[...])
        a = (_silu(g) * u).astype(bf16)
        y = _dot(a, wd_bf[...])

        @pl.when(i >= Y_SLOTS)
        def _():
            y_copy(i - Y_SLOTS).wait()

        _store_rows(y_buf, y, lead=(i % Y_SLOTS,))
        y_copy(i).start()

    @pl.when(i == n_act - 1)
    def _drain():
        for d in range(Y_SLOTS):
            @pl.when(i - d >= 0)
            def _():
                y_copy(i - d).wait()


def _experts(xs, blk_e, n_act, row_start, cnt, w_gate, w_up, w_down):
    D = D_MODEL
    BM = EXPERT_ROWS
    F = EXPERT_DIM
    n_blocks = xs.shape[0] // BM
    blk_in_expert = jnp.arange(n_blocks, dtype=jnp.int32) - row_start[blk_e] // BM
    valid = jnp.clip(cnt[blk_e] - blk_in_expert * BM, 0, BM).astype(jnp.int32)
    ids = jnp.arange(n_blocks, dtype=jnp.int32)
    active = ids < n_act[0]
    first = active & ((ids == 0) | (blk_e != jnp.roll(blk_e, 1)))
    slot = ((jnp.cumsum(first.astype(jnp.int32)) - 1) % 2).astype(jnp.int32)
    first_pos = jnp.where(first, ids, n_blocks)
    later_first = lax.cummin(jnp.concatenate([first_pos[1:], jnp.full((1,), n_blocks, jnp.int32)]), reverse=True)
    next_e = jnp.where(later_first < n_blocks, blk_e[jnp.minimum(later_first, n_blocks - 1)], -1).astype(jnp.int32)

    grid_spec = pltpu.PrefetchScalarGridSpec(
        num_scalar_prefetch=6,
        grid=(n_blocks,),
        in_specs=[pl.BlockSpec(memory_space=pl.ANY)] * 4,
        out_specs=pl.BlockSpec(memory_space=pl.ANY),
        scratch_shapes=[
            pltpu.VMEM((X_SLOTS, BM, D), jnp.float32),
            pltpu.VMEM((Y_SLOTS, BM * ROW_TILE, LANES), jnp.float32),
            pltpu.VMEM((2, D, F), jnp.float32), pltpu.VMEM((2, D, F), jnp.float32),
            pltpu.VMEM((2, F, D), jnp.float32),
            pltpu.VMEM((D, F), jnp.bfloat16), pltpu.VMEM((D, F), jnp.bfloat16),
            pltpu.VMEM((F, D), jnp.bfloat16),
            pltpu.SemaphoreType.DMA((2,)), pltpu.SemaphoreType.DMA((X_SLOTS,)),
            pltpu.SemaphoreType.DMA((Y_SLOTS,)),
        ],
    )
    return pl.pallas_call(
        _experts_kernel,
        grid_spec=grid_spec,
        out_shape=jax.ShapeDtypeStruct((xs.shape[0] * ROW_TILE, LANES), jnp.float32),
        compiler_params=pltpu.CompilerParams(
            dimension_semantics=("arbitrary",), vmem_limit_bytes=VMEM_LIMIT_BYTES),
    )(blk_e, first.astype(jnp.int32), slot, next_e, valid, n_act, xs, w_gate, w_up, w_down)


def _finish_kernel(h_ref, part_ref, w_ref, *rest):
    slabs = rest[:SC_COMBINE_SLOTS]
    g_ref, b_ref, out_ref = rest[SC_COMBINE_SLOTS:]
    R = h_ref.shape[0]
    w = w_ref[...]
    ffn = part_ref[...]
    for k in range(SC_COMBINE_SLOTS):
        ffn = ffn + _load_rows(slabs[k], R) * w[:, k:k + 1]
    out_ref[...] = _layer_norm(DEEPNORM_ALPHA * h_ref[...] + ffn, g_ref[...], b_ref[...])


def _finish(h2, partial, top_w, gathered, ln_g, ln_b):
    T, D = h2.shape
    R = FINISH_ROWS
    rows = pl.BlockSpec((R, D), lambda i: (i, 0))
    vec = pl.BlockSpec((1, D), lambda i: (0, 0))
    slab = lambda k: pl.BlockSpec((R * ROW_TILE, LANES), lambda i: (k * (T // R) + i, 0))
    return pl.pallas_call(
        _finish_kernel,
        grid=(T // R,),
        in_specs=[rows, rows, pl.BlockSpec((R, TOP_K), lambda i: (i, 0))]
        + [slab(k) for k in range(SC_COMBINE_SLOTS)] + [vec, vec],
        out_specs=rows,
        out_shape=jax.ShapeDtypeStruct((T, D), jnp.float32),
        compiler_params=pltpu.CompilerParams(
            dimension_semantics=("arbitrary",), vmem_limit_bytes=VMEM_LIMIT_BYTES),
    )(h2, partial, top_w, *([gathered] * SC_COMBINE_SLOTS), ln_g.reshape(1, D), ln_b.reshape(1, D))


SC_COMBINE_SLOTS = 5


def _combine_kernel(pos_ref, pos_next_ref, h_ref, w_ref, ys_ref, wsg_ref, wsu_ref, wsd_ref,
                    out_ref, buf_ref, sems):
    R = h_ref.shape[0]
    i = pl.program_id(0)
    slot = i % 2

    def gather(p_ref, s):
        def issue(t, carry):
            for k in range(SC_COMBINE_SLOTS, TOP_K):
                src = p_ref[t * TOP_K + k]
                pltpu.make_async_copy(ys_ref.at[_row_tile(src)], buf_ref.at[s, k - SC_COMBINE_SLOTS, _row_tile(t)],
                                      sems.at[s]).start(priority=k % 2)
            return carry

        lax.fori_loop(0, R, issue, 0)

    @pl.when(i == 0)
    def _():
        gather(pos_ref, 0)

    @pl.when(i + 1 < pl.num_programs(0))
    def _():
        gather(pos_next_ref, 1 - slot)

    h = h_ref[...]
    hb = h.astype(jnp.bfloat16)
    act = (_silu(_dot(hb, wsg_ref[...])) * _dot(hb, wsu_ref[...])).astype(jnp.bfloat16)
    ffn = _dot(act, wsd_ref[...])
    pltpu.make_async_copy(buf_ref.at[slot], buf_ref.at[slot], sems.at[slot]).wait()
    w = w_ref[...]
    for k in range(SC_COMBINE_SLOTS, TOP_K):
        ffn = ffn + _load_rows(buf_ref, R, lead=(slot, k - SC_COMBINE_SLOTS)) * w[:, k:k + 1]
    out_ref[...] = ffn


def _combine(h2, pos, top_w, ys, ws_gate, ws_up, ws_down):
    T, D = h2.shape
    R = COMBINE_ROWS
    F = SHARED_DIM
    bf16 = jnp.bfloat16
    const = lambda shape: pl.BlockSpec(shape, lambda i: (0,) * len(shape))
    return pl.pallas_call(
        _combine_kernel,
        grid=(T // R,),
        in_specs=[
            pl.BlockSpec((R * TOP_K,), lambda i: (i,), memory_space=pltpu.SMEM),
            pl.BlockSpec((R * TOP_K,), lambda i: (jnp.minimum(i + 1, T // R - 1),), memory_space=pltpu.SMEM),
            pl.BlockSpec((R, D), lambda i: (i, 0)),
            pl.BlockSpec((R, TOP_K), lambda i: (i, 0)),
            pl.BlockSpec(memory_space=pl.ANY),
            const((D, F)), const((D, F)), const((F, D)),
        ],
        out_specs=pl.BlockSpec((R, D), lambda i: (i, 0)),
        scratch_shapes=[pltpu.VMEM((2, TOP_K - SC_COMBINE_SLOTS, R * ROW_TILE, LANES), jnp.float32),
                        pltpu.SemaphoreType.DMA((2,))],
        out_shape=jax.ShapeDtypeStruct((T, D), jnp.float32),
        compiler_params=pltpu.CompilerParams(
            dimension_semantics=("arbitrary",), vmem_limit_bytes=VMEM_LIMIT_BYTES),
    )(pos, pos, h2, top_w, ys, ws_gate.astype(bf16), ws_up.astype(bf16), ws_down.astype(bf16))


def _moe(h2, w_router, router_bias, w_gate, w_up, w_down, ws_gate, ws_up, ws_down, ln_g, ln_b):
    T = h2.shape[0]
    E = N_EXPERTS
    BM = EXPERT_ROWS
    e_idx, top_w, rank, counts = _router(h2, w_router, router_bias)
    cnt = counts[:, 0].astype(jnp.int32)
    nblk = (cnt + BM - 1) // BM
    blk_end = jnp.cumsum(nblk)
    row_start = ((blk_end - nblk) * BM).astype(jnp.int32)
    n_blocks = T * TOP_K // BM + E
    n_act = blk_end[-1:].astype(jnp.int32)
    blk_ids = jnp.minimum(jnp.arange(n_blocks, dtype=jnp.int32), n_act[0] - 1)
    blk_e = jnp.minimum(jnp.sum(blk_end[None, :] <= blk_ids[:, None], axis=1), E - 1).astype(jnp.int32)
    pos_kt = _positions(e_idx, rank, row_start).reshape(TOP_K, T)
    pos = pos_kt.T.reshape(T * TOP_K)
    xs = _dispatch_sc(h2, pos_kt, n_blocks * BM)
    ys = _experts(xs, blk_e, n_act, row_start, cnt, w_gate, w_up, w_down)
    gathered = _gather_rows_sc(ys, pos_kt[:SC_COMBINE_SLOTS])
    w_tk = top_w.T
    partial = _combine(h2, pos, w_tk, ys, ws_gate, ws_up, ws_down)
    return _finish(h2, partial, w_tk, gathered, ln_g, ln_b)


def kernel(x, ln_in_g, ln_in_b, w_in, w_out, rel_bias, attn_sinks, ln_mix_g, ln_mix_b, w_router,
           router_bias, w_gate, w_up, w_down, ws_gate, ws_up, ws_down, ln_ffn_g, ln_ffn_b):
    B, S, D = x.shape
    h = _mixer(x, ln_in_g, ln_in_b, w_in[0], w_out[0], rel_bias, attn_sinks[0], ln_mix_g[0], ln_mix_b[0])
    out = _moe(h, w_router[0], router_bias[0], w_gate[0], w_up[0], w_down[0],
               ws_gate[0], ws_up[0], ws_down[0], ln_ffn_g[0], ln_ffn_b[0])
    return out.reshape(B, S, D)
```

```python
import functools
import math

import jax
import jax.numpy as jnp
from jax import lax
from jax.experimental import pallas as pl
from jax.experimental.pallas import tpu as pltpu
from jax.experimental.pallas import tpu_sc as plsc

D_MODEL = 1024
DEPTH = 1
RET_HEADS = 4
RET_QK_DIM = 64
RET_V_DIM = 128
RET_CHUNK = 128
RET_WIDTH = RET_HEADS * RET_V_DIM
ROPE_BASE = 10000.0
SWA_HEADS = 8
SWA_KV_HEADS = 2
SWA_GROUP = SWA_HEADS // SWA_KV_HEADS
SWA_HEAD_DIM = 64
SWA_WINDOW = 128
SWA_WIDTH = SWA_HEADS * SWA_HEAD_DIM
MIX_WIDTH = RET_WIDTH + SWA_WIDTH
RQK = RET_HEADS * RET_QK_DIM
SKV = SWA_KV_HEADS * SWA_HEAD_DIM
IN_SIZES = (RQK, RQK, RET_WIDTH, RET_WIDTH, SWA_WIDTH, SKV, SKV)
IN_OFFS = tuple(sum(IN_SIZES[:i]) for i in range(len(IN_SIZES)))
IN_WIDTH = sum(IN_SIZES)
REL_BUCKETS = 32
REL_MAX_DIST = 128
N_EXPERTS = 256
TOP_K = 8
N_GROUPS = 8
GROUP_SIZE = N_EXPERTS // N_GROUPS
TOPK_GROUPS = 4
EXPERT_DIM = 256
SHARED_DIM = 256
ROUTED_SCALE = 2.5
LN_EPS = 1e-5
GN_EPS = 1e-6
DEEPNORM_ALPHA = (2 * DEPTH) ** 0.25
MASK_VALUE = -1e30

VMEM_LIMIT_BYTES = 56 * 1024 * 1024

MIX_ROWS = 256
ROUTE_ROWS = 256
DISPATCH_ROWS = 256
EXPERT_ROWS = 256
COMBINE_ROWS = 256
POSITION_ROWS = 64
FINISH_ROWS = 512


def _layer_norm(x, g, b):
    mu = jnp.mean(x, axis=-1, keepdims=True)
    xc = x - mu
    var = jnp.mean(xc * xc, axis=-1, keepdims=True)
    return xc * lax.rsqrt(var + LN_EPS) * g + b


def _dot(a, b):
    return jnp.dot(a, b, preferred_element_type=jnp.float32)


def _dot_nt(a, b):
    return lax.dot_general(a, b, (((1,), (1,)), ((), ())), preferred_element_type=jnp.float32)


def _dot_tn(a, b):
    return lax.dot_general(a, b, (((0,), (0,)), ((), ())), preferred_element_type=jnp.float32)


def _silu(x):
    return x * (1.0 / (1.0 + jnp.exp(-x)))


LANES = 128
ROW_TILE = D_MODEL // LANES


def _load_rows(ref, n_rows, lead=()):
    return jnp.concatenate([ref[lead + (pl.ds(s, n_rows, stride=ROW_TILE), slice(None))]
                            for s in range(ROW_TILE)], axis=1)


def _store_rows(ref, val, lead=()):
    n_rows = val.shape[0]
    for s in range(ROW_TILE):
        ref[lead + (pl.ds(s, n_rows, stride=ROW_TILE), slice(None))] = val[:, s * LANES:(s + 1) * LANES]


def _row_tile(r):
    return pl.ds(pl.multiple_of(r * ROW_TILE, ROW_TILE), ROW_TILE)


def _pack_bf16_pairs(x):
    m = x.shape[1] // 2
    bits = lax.bitcast_convert_type(x.astype(jnp.bfloat16).astype(jnp.float32), jnp.uint32)
    return (bits[:, :m] >> 16) | (bits[:, m:] & jnp.uint32(0xFFFF0000))


def _unpack_bf16_pairs(p):
    lo = lax.bitcast_convert_type(p << 16, jnp.float32)
    hi = lax.bitcast_convert_type(p & jnp.uint32(0xFFFF0000), jnp.float32)
    return jnp.concatenate([lo, hi], axis=1).astype(jnp.bfloat16)


def _swap_halves(x):
    n = x.shape[-1]
    half = RET_QK_DIM // 2
    lane = lax.broadcasted_iota(jnp.int32, x.shape, 1)
    from_right = pltpu.roll(x, n - half, axis=1)
    from_left = pltpu.roll(x, half, axis=1)
    return jnp.where((lane % RET_QK_DIM) < half, from_right, from_left)


def _mixer_kernel(rel_bias_ref, x_ref, g_in_ref, b_in_ref, w_in_ref, w_out_ref, rot_ref, decay_ref,
                  zeta_ref, xi_ref, cdecay_ref, bucket_ref, sink_ref, g_mix_ref, b_mix_ref,
                  h2_ref, h2p_ref, state_ref, kprev_ref, vprev_ref, bias_ref):
    b_id = pl.program_id(0)
    c_id = pl.program_id(1)
    W = SWA_WINDOW

    @pl.when((b_id == 0) & (c_id == 0))
    def _build_bias():
        bucket = bucket_ref[...]
        for h in range(SWA_HEADS):
            acc = jnp.full((2 * W, W), MASK_VALUE, jnp.float32)
            for b in range(REL_BUCKETS):
                acc = jnp.where(bucket == b, rel_bias_ref[b, h], acc)
            kh, g = divmod(h, SWA_GROUP)
            bias_ref[kh, :, g * W:(g + 1) * W] = acc

    @pl.when(c_id == 0)
    def _reset():
        state_ref[...] = jnp.zeros_like(state_ref)
        kprev_ref[...] = jnp.zeros_like(kprev_ref)
        vprev_ref[...] = jnp.zeros_like(vprev_ref)

    h = _layer_norm(x_ref[...], g_in_ref[...], b_in_ref[...])
    proj = _dot(h.astype(jnp.bfloat16), w_in_ref[...])

    o_q, o_k, o_v, o_g, o_sq, o_sk, o_sv = IN_OFFS
    cos_t = rot_ref[:, :RQK]
    sin_t = rot_ref[:, RQK:]
    q_all = proj[:, o_q:o_q + RQK]
    k_all = proj[:, o_k:o_k + RQK]
    q_rot = q_all * cos_t + _swap_halves(q_all) * sin_t
    k_rot = (k_all * cos_t + _swap_halves(k_all) * sin_t) * (RET_QK_DIM ** -0.5)

    n_sub = x_ref.shape[0] // RET_CHUNK
    states = [state_ref[hh] for hh in range(RET_HEADS)]
    k_prev = kprev_ref[...]
    v_prev = vprev_ref[...]
    cat_rows = []
    for s in range(n_sub):
        r0 = s * RET_CHUNK
        rows = slice(r0, r0 + RET_CHUNK)
        pieces = []
        for hh in range(RET_HEADS):
            qk = slice(hh * RET_QK_DIM, (hh + 1) * RET_QK_DIM)
            vv = slice(o_v + hh * RET_V_DIM, o_v + (hh + 1) * RET_V_DIM)
            gg = slice(o_g + hh * RET_V_DIM, o_g + (hh + 1) * RET_V_DIM)
            q = q_rot[rows, qk].astype(jnp.bfloat16)
            k32 = k_rot[rows, qk]
            v = proj[rows, vv].astype(jnp.bfloat16)
            scores = _dot_nt(q, k32.astype(jnp.bfloat16)) * decay_ref[hh]
            intra = _dot(scores.astype(jnp.bfloat16), v)
            inter = _dot(q, states[hh].astype(jnp.bfloat16)) * xi_ref[hh]
            ret = intra + inter
            kz = (k32 * zeta_ref[hh]).astype(jnp.bfloat16)
            states[hh] = states[hh] * cdecay_ref[hh] + _dot_tn(kz, v)
            mu = jnp.mean(ret, axis=-1, keepdims=True)
            rc = ret - mu
            var = jnp.mean(rc * rc, axis=-1, keepdims=True)
            normed = rc * lax.rsqrt(var + GN_EPS)
            pieces.append((_silu(proj[rows, gg]) * normed).astype(jnp.bfloat16))
        k_cur = proj[rows, o_sk:o_sk + SKV].astype(jnp.bfloat16)
        v_cur = proj[rows, o_sv:o_sv + SKV].astype(jnp.bfloat16)
        for kh in range(SWA_KV_HEADS):
            kv = slice(kh * SWA_HEAD_DIM, (kh + 1) * SWA_HEAD_DIM)
            q4 = jnp.concatenate(
                [proj[rows, o_sq + (kh * SWA_GROUP + g) * SWA_HEAD_DIM:
                      o_sq + (kh * SWA_GROUP + g + 1) * SWA_HEAD_DIM] for g in range(SWA_GROUP)],
                axis=0) * (SWA_HEAD_DIM ** -0.5)
            kcat = jnp.concatenate([k_prev[:, kv], k_cur[:, kv]], axis=0)
            vcat = jnp.concatenate([v_prev[:, kv], v_cur[:, kv]], axis=0)
            logits = _dot_nt(kcat, q4.astype(jnp.bfloat16)) + bias_ref[kh]
            if s == 0:
                key = lax.broadcasted_iota(jnp.int32, logits.shape, 0)
                logits = logits + jnp.where((key < W) & (c_id == 0), MASK_VALUE, 0.0)
            sink = sink_ref[kh]
            m = jnp.maximum(jnp.max(logits, axis=0, keepdims=True), sink)
            p = jnp.exp(logits - m)
            den = jnp.sum(p, axis=0, keepdims=True) + jnp.exp(sink - m)
            probs = (p / den).astype(jnp.bfloat16)
            o4 = _dot_tn(vcat, probs)
            pieces.extend(o4[:, g * W:(g + 1) * W].T.astype(jnp.bfloat16) for g in range(SWA_GROUP))
        k_prev, v_prev = k_cur, v_cur
        cat_rows.append(jnp.concatenate(pieces, axis=1))
    for hh in range(RET_HEADS):
        state_ref[hh] = states[hh]
    kprev_ref[...] = k_prev
    vprev_ref[...] = v_prev

    mix = _dot(jnp.concatenate(cat_rows, axis=0), w_out_ref[...])
    h2 = _layer_norm(DEEPNORM_ALPHA * h + mix, g_mix_ref[...], b_mix_ref[...])
    h2_ref[...] = h2
    h2p_ref[...] = _pack_bf16_pairs(h2)


def _t5_bucket(dist):
    n = jnp.maximum(dist, 0)
    max_exact = REL_BUCKETS // 2
    ratio = jnp.log(jnp.maximum(n, 1).astype(jnp.float32) / max_exact) / math.log(REL_MAX_DIST / max_exact)
    large = jnp.minimum(max_exact + (ratio * (REL_BUCKETS - max_exact)).astype(jnp.int32), REL_BUCKETS - 1)
    return jnp.where(n < max_exact, n, large)


def _mixer(x, ln_in_g, ln_in_b, w_in, w_out, rel_bias, sinks, ln_mix_g, ln_mix_b):
    B, S, D = x.shape
    R = MIX_ROWS
    C = RET_CHUNK
    W = SWA_WINDOW
    f32 = jnp.float32
    half = RET_QK_DIM // 2
    inv = ROPE_BASE ** (-jnp.arange(half, dtype=f32) / half)
    ang = jnp.arange(S, dtype=f32)[:, None] * inv[None, :]
    cos, sin = jnp.cos(ang), jnp.sin(ang)
    cos_t = jnp.tile(jnp.concatenate([cos, cos], axis=-1), (1, RET_HEADS))
    sin_t = jnp.tile(jnp.concatenate([-sin, sin], axis=-1), (1, RET_HEADS))
    rot = jnp.concatenate([cos_t, sin_t], axis=-1)
    log_gamma = jnp.log(1.0 - 2.0 ** (-5.0 - jnp.arange(RET_HEADS, dtype=f32)))
    idx = jnp.arange(C, dtype=f32)
    diff = idx[:, None] - idx[None, :]
    decay = jnp.where(diff[None] >= 0, jnp.exp(jnp.maximum(diff, 0.0)[None] * log_gamma[:, None, None]), 0.0)
    zeta = jnp.exp((C - 1.0 - idx)[None, :] * log_gamma[:, None])
    xi = jnp.exp((idx + 1.0)[None, :] * log_gamma[:, None])
    zeta_b = jnp.broadcast_to(zeta[:, :, None], (RET_HEADS, C, RET_QK_DIM))
    xi_b = jnp.broadcast_to(xi[:, :, None], (RET_HEADS, C, RET_V_DIM))
    cdecay = jnp.broadcast_to(jnp.exp(C * log_gamma)[:, None, None], (RET_HEADS, RET_QK_DIM, RET_V_DIM))
    i = jnp.arange(W)
    j = jnp.arange(2 * W)
    dist = i[:, None] + W - j[None, :]
    bucket = jnp.where((dist >= 0) & (dist < W), _t5_bucket(dist), -1).astype(jnp.int32).T
    sink_row = jnp.repeat(sinks.astype(f32), W).reshape(SWA_KV_HEADS, 1, SWA_GROUP * W)

    const = lambda shape: pl.BlockSpec(shape, lambda b, c, *_: (0,) * len(shape))
    grid_spec = pltpu.PrefetchScalarGridSpec(
        num_scalar_prefetch=1,
        grid=(B, S // R),
        in_specs=[
            pl.BlockSpec((None, R, D), lambda b, c, *_: (b, c, 0)),
            const((1, D)), const((1, D)),
            const((D, IN_WIDTH)), const((MIX_WIDTH, D)),
            pl.BlockSpec((R, 2 * RQK), lambda b, c, *_: (c, 0)),
            const((RET_HEADS, C, C)), const((RET_HEADS, C, RET_QK_DIM)), const((RET_HEADS, C, RET_V_DIM)),
            const((RET_HEADS, RET_QK_DIM, RET_V_DIM)),
            const((2 * W, W)), const((SWA_KV_HEADS, 1, SWA_GROUP * W)),
            const((1, D)), const((1, D)),
        ],
        out_specs=[pl.BlockSpec((R, D), lambda b, c, *_: (b * (S // R) + c, 0)),
                   pl.BlockSpec((R, D // 2), lambda b, c, *_: (b * (S // R) + c, 0))],
        scratch_shapes=[
            pltpu.VMEM((RET_HEADS, RET_QK_DIM, RET_V_DIM), f32),
            pltpu.VMEM((W, SKV), jnp.bfloat16),
            pltpu.VMEM((W, SKV), jnp.bfloat16),
            pltpu.VMEM((SWA_KV_HEADS, 2 * W, SWA_GROUP * W), f32),
        ],
    )
    return pl.pallas_call(
        _mixer_kernel,
        grid_spec=grid_spec,
        out_shape=[jax.ShapeDtypeStruct((B * S, D), f32), jax.ShapeDtypeStruct((B * S, D // 2), jnp.uint32)],
        compiler_params=pltpu.CompilerParams(
            dimension_semantics=("arbitrary", "arbitrary"), vmem_limit_bytes=VMEM_LIMIT_BYTES),
    )(rel_bias.astype(f32), x, ln_in_g.reshape(1, D), ln_in_b.reshape(1, D),
      w_in.astype(jnp.bfloat16), w_out.astype(jnp.bfloat16), rot, decay, zeta_b, xi_b, cdecay,
      bucket, sink_row, ln_mix_g.reshape(1, D), ln_mix_b.reshape(1, D))


def _router_kernel(h_ref, wr_ref, rb_ref, e_ref, w_ref, rk_ref, cnt_ref, run_ref):
    f32 = jnp.float32
    R = h_ref.shape[0]
    E = N_EXPERTS
    neg = -jnp.inf

    @pl.when(pl.program_id(0) == 0)
    def _init():
        run_ref[...] = jnp.zeros_like(run_ref)

    logits = _dot_nt(wr_ref[...], h_ref[...].astype(jnp.bfloat16))
    scores = 1.0 / (1.0 + jnp.exp(-logits))
    choice = scores + rb_ref[...]
    eid = lax.broadcasted_iota(jnp.int32, (E, R), 0)

    def first_argmax(vals, ids, none):
        m = jnp.max(vals, axis=0, keepdims=True)
        idx = jnp.min(jnp.where(vals == m, ids, none), axis=0, keepdims=True)
        return m, idx

    gid = lax.broadcasted_iota(jnp.int32, (GROUP_SIZE, R), 0)
    groups, gscore = [], []
    for g in range(N_GROUPS):
        vals = choice[g * GROUP_SIZE:(g + 1) * GROUP_SIZE]
        m1, i1 = first_argmax(vals, gid, GROUP_SIZE)
        m2 = jnp.max(jnp.where(gid == i1, neg, vals), axis=0, keepdims=True)
        groups.append(vals)
        gscore.append(m1 + m2)
    kept = []
    for g in range(N_GROUPS):
        beaten = jnp.zeros((1, R), f32)
        for g2 in range(N_GROUPS):
            if g2 == g:
                continue
            ahead = (gscore[g2] > gscore[g]) | (gscore[g2] == gscore[g]) if g2 < g else gscore[g2] > gscore[g]
            beaten = beaten + jnp.where(ahead, 1.0, 0.0)
        kept.append(jnp.where(beaten < TOPK_GROUPS, groups[g], neg))
    masked = jnp.concatenate(kept, axis=0)

    idxs, wts = [], []
    picked = jnp.zeros((E, R), f32)
    for _ in range(TOP_K):
        _, idx = first_argmax(masked, eid, E)
        hit = eid == idx
        idxs.append(idx)
        wts.append(jnp.sum(jnp.where(hit, scores, 0.0), axis=0, keepdims=True))
        masked = jnp.where(hit, neg, masked)
        picked = jnp.where(hit, 1.0, picked)
    wsum = wts[0]
    for k in range(1, TOP_K):
        wsum = wsum + wts[k]

    row = lax.broadcasted_iota(jnp.int32, (R, R), 0)
    col = lax.broadcasted_iota(jnp.int32, (R, R), 1)
    earlier = jnp.where(row < col, 1.0, 0.0).astype(jnp.bfloat16)
    picked_bf = picked.astype(jnp.bfloat16)
    run = run_ref[...]
    before = _dot(picked_bf, earlier) + jnp.concatenate([run] * (R // LANES), axis=1)
    sub_k = lax.broadcasted_iota(jnp.int32, (TOP_K, R), 0)
    e_out = jnp.zeros((TOP_K, R), jnp.int32)
    w_out = jnp.zeros((TOP_K, R), f32)
    rk_out = jnp.zeros((TOP_K, R), jnp.int32)
    for k in range(TOP_K):
        rank_k = jnp.sum(jnp.where(eid == idxs[k], before, 0.0), axis=0, keepdims=True)
        e_out = jnp.where(sub_k == k, idxs[k], e_out)
        w_out = jnp.where(sub_k == k, wts[k] / wsum * ROUTED_SCALE, w_out)
        rk_out = jnp.where(sub_k == k, rank_k.astype(jnp.int32), rk_out)
    e_ref[...] = e_out
    w_ref[...] = w_out
    rk_ref[...] = rk_out
    run_ref[...] = run + _dot(picked_bf, jnp.ones((R, LANES), jnp.bfloat16))
    cnt_ref[...] = run_ref[...]


def _router(h2, w_router, router_bias):
    T, D = h2.shape
    R = ROUTE_ROWS
    E = N_EXPERTS
    return pl.pallas_call(
        _router_kernel,
        grid=(T // R,),
        in_specs=[
            pl.BlockSpec((R, D), lambda i: (i, 0)),
            pl.BlockSpec((E, D), lambda i: (0, 0)),
            pl.BlockSpec((E, R), lambda i: (0, 0)),
        ],
        out_specs=[
            pl.BlockSpec((TOP_K, R), lambda i: (0, i)),
            pl.BlockSpec((TOP_K, R), lambda i: (0, i)),
            pl.BlockSpec((TOP_K, R), lambda i: (0, i)),
            pl.BlockSpec((E, LANES), lambda i: (0, 0)),
        ],
        out_shape=[
            jax.ShapeDtypeStruct((TOP_K, T), jnp.int32),
            jax.ShapeDtypeStruct((TOP_K, T), jnp.float32),
            jax.ShapeDtypeStruct((TOP_K, T), jnp.int32),
            jax.ShapeDtypeStruct((E, LANES), jnp.float32),
        ],
        scratch_shapes=[pltpu.VMEM((E, LANES), jnp.float32)],
        compiler_params=pltpu.CompilerParams(
            dimension_semantics=("arbitrary",), vmem_limit_bytes=VMEM_LIMIT_BYTES),
    )(h2, w_router.T.astype(jnp.bfloat16), jnp.broadcast_to(router_bias.astype(jnp.float32)[:, None], (E, R)))


def _positions_kernel(row_start_ref, e_ref, rk_ref, pos_ref):
    e = e_ref[...]
    rk = rk_ref[...]

    def per_expert(i, pos):
        return jnp.where(e == i, rk + row_start_ref[i], pos)

    pos_ref[...] = lax.fori_loop(0, N_EXPERTS, per_expert, jnp.zeros_like(rk))


def _positions(e_idx, rank, row_start):
    n = e_idx.size
    shape = (n // LANES, LANES)
    block = pl.BlockSpec((POSITION_ROWS, LANES), lambda i, *_: (i, 0))
    grid_spec = pltpu.PrefetchScalarGridSpec(
        num_scalar_prefetch=1,
        grid=(shape[0] // POSITION_ROWS,),
        in_specs=[block, block],
        out_specs=block,
    )
    pos = pl.pallas_call(
        _positions_kernel,
        grid_spec=grid_spec,
        out_shape=jax.ShapeDtypeStruct(shape, jnp.int32),
    )(row_start, e_idx.reshape(shape), rank.reshape(shape))
    return pos.reshape(n)


def _dispatch_kernel(row_start_ref, cnt_ref, n_act_ref, pos_ref, h_ref, xs_ref, zero_ref, sem, zsem,
                     *, n_tokens):
    R = h_ref.shape[0] // ROW_TILE
    BM = EXPERT_ROWS
    n_blocks = xs_ref.shape[0] // (BM * ROW_TILE)
    n_pad_units = n_blocks - n_tokens * TOP_K // BM

    @pl.when(pl.program_id(0) == 0)
    def _zero_padding():
        zero_ref[...] = jnp.zeros_like(zero_ref)

        def expert_tail(e, carry):
            n_tail = pl.multiple_of(((BM - cnt_ref[e] % BM) % BM) * ROW_TILE, ROW_TILE)

            @pl.when(n_tail > 0)
            def _():
                dst = pl.multiple_of((row_start_ref[e] + cnt_ref[e]) * ROW_TILE, ROW_TILE)
                pltpu.make_async_copy(zero_ref.at[pl.ds(0, n_tail)], xs_ref.at[pl.ds(dst, n_tail)], zsem).start()
            return carry

        lax.fori_loop(0, N_EXPERTS, expert_tail, 0)

        def idle_block(i, carry):
            dst = pl.multiple_of(i * (BM * ROW_TILE), BM * ROW_TILE)
            pltpu.make_async_copy(zero_ref, xs_ref.at[pl.ds(dst, BM * ROW_TILE)], zsem).start()
            return carry

        lax.fori_loop(n_act_ref[0], n_blocks, idle_block, 0)

        def drain(i, carry):
            pltpu.make_async_copy(zero_ref, xs_ref.at[pl.ds(0, BM * ROW_TILE)], zsem).wait()
            return carry

        lax.fori_loop(0, n_pad_units, drain, 0)

    def issue(t, carry):
        src = h_ref.at[_row_tile(t)]
        for k in range(TOP_K):
            dest = pos_ref[t * TOP_K + k]
            pltpu.make_async_copy(src, xs_ref.at[_row_tile(dest)], sem).start(priority=k % 2)
        return carry

    lax.fori_loop(0, R, issue, 0)
    n = R * TOP_K * ROW_TILE
    pltpu.make_async_copy(xs_ref.at[pl.ds(0, n)], xs_ref.at[pl.ds(0, n)], sem).wait()


def _dispatch(h2, pos, row_start, cnt, n_act, n_rows):
    T = h2.shape[0] // ROW_TILE
    R = DISPATCH_ROWS
    grid_spec = pltpu.PrefetchScalarGridSpec(
        num_scalar_prefetch=3,
        grid=(T // R,),
        in_specs=[
            pl.BlockSpec((R * TOP_K,), lambda i, *_: (i,), memory_space=pltpu.SMEM),
            pl.BlockSpec((R * ROW_TILE, LANES), lambda i, *_: (i, 0)),
        ],
        out_specs=pl.BlockSpec(memory_space=pl.ANY),
        scratch_shapes=[pltpu.VMEM((EXPERT_ROWS * ROW_TILE, LANES), jnp.float32),
                        pltpu.SemaphoreType.DMA(()), pltpu.SemaphoreType.DMA(())],
    )
    return pl.pallas_call(
        functools.partial(_dispatch_kernel, n_tokens=T),
        grid_spec=grid_spec,
        out_shape=jax.ShapeDtypeStruct((n_rows * ROW_TILE, LANES), jnp.float32),
        compiler_params=pltpu.CompilerParams(
            dimension_semantics=("arbitrary",), vmem_limit_bytes=VMEM_LIMIT_BYTES),
    )(row_start, cnt, n_act, pos, h2)


SC_WINDOW = 32


def _dispatch_sc(h2, pos_kt, n_rows):
    T, D = h2.shape
    W = SC_WINDOW
    idx = _window_indices(pos_kt, W)
    idx_rows = TOP_K * W // LANES
    mesh = plsc.VectorSubcoreMesh(core_axis_name="core", subcore_axis_name="subcore")

    @pl.kernel(out_type=jax.ShapeDtypeStruct((n_rows, D), h2.dtype), mesh=mesh, scratch_types=[])
    def scatter_rows(x_hbm, i_hbm, o_hbm):
        def body(x_vmem, i_vmem):
            for k in range(TOP_K):
                r, q = divmod(k * W, LANES)
                pltpu.sync_copy(x_vmem, o_hbm.at[i_vmem.at[r, pl.ds(q, W)]])

        pltpu.emit_pipeline(
            body,
            grid=(T // W,),
            in_specs=[pl.BlockSpec((W, D), lambda i: (i, 0)),
                      pl.BlockSpec((idx_rows, LANES), lambda i: (i, 0))],
            out_specs=[],
            core_axis_name=("core", "subcore"),
            dimension_semantics=(pltpu.PARALLEL,),
        )(x_hbm, i_hbm)

    return scatter_rows(h2, idx)


def _window_indices(pos_kt, window):
    K, T = pos_kt.shape
    return pos_kt.reshape(K, T // window, window).transpose(1, 0, 2).reshape(T * K // LANES, LANES)


def _gather_rows_sc(ys, pos_kt):
    n = pos_kt.size
    W = SC_WINDOW
    ys3 = ys.reshape(ys.shape[0] // ROW_TILE, ROW_TILE, LANES)
    idx = jnp.pad(pos_kt.reshape(n // W, W), ((0, 0), (0, LANES - W)))
    mesh = plsc.VectorSubcoreMesh(core_axis_name="core", subcore_axis_name="subcore")

    @pl.kernel(out_type=jax.ShapeDtypeStruct((n, ROW_TILE, LANES), ys.dtype), mesh=mesh, scratch_types=[])
    def gather_rows(y_hbm, i_hbm, o_hbm):
        def body(i_vmem, o_vmem):
            pltpu.sync_copy(y_hbm.at[i_vmem.at[0, pl.ds(0, W)]], o_vmem)

        pltpu.emit_pipeline(
            body,
            grid=(n // W,),
            in_specs=[pl.BlockSpec((1, LANES), lambda i: (i, 0))],
            out_specs=[pl.BlockSpec((W, ROW_TILE, LANES), lambda i: (i, 0, 0))],
            core_axis_name=("core", "subcore"),
            dimension_semantics=(pltpu.PARALLEL,),
        )(i_hbm, o_hbm)

    return gather_rows(ys3, idx).reshape(n * ROW_TILE, LANES)


X_SLOTS = 4
Y_SLOTS = 3


def _experts_kernel(blk_e_ref, first_ref, slot_ref, next_e_ref, valid_ref, n_act_ref, xs_hbm, wg_hbm,
                    wu_hbm, wd_hbm, ys_hbm, x_buf, y_buf, wg_buf, wu_buf, wd_buf, wg_bf, wu_bf, wd_bf,
                    sems, x_sems, y_sems):
    i = pl.program_id(0)
    n_act = n_act_ref[0]
    bf16 = jnp.bfloat16
    blk = EXPERT_ROWS * ROW_TILE

    def x_copy(j):
        n = pl.multiple_of((valid_ref[j] + 7) // 8 * 8, 8)
        src = xs_hbm.at[pl.ds(pl.multiple_of(j * EXPERT_ROWS, EXPERT_ROWS), n)]
        return pltpu.make_async_copy(src, x_buf.at[j % X_SLOTS, pl.ds(0, n)], x_sems.at[j % X_SLOTS])

    def y_copy(j):
        n = pl.multiple_of(valid_ref[j] * ROW_TILE, ROW_TILE)
        dst = ys_hbm.at[pl.ds(pl.multiple_of(j * blk, blk), n)]
        return pltpu.make_async_copy(y_buf.at[j % Y_SLOTS, pl.ds(0, n)], dst, y_sems.at[j % Y_SLOTS])

    @pl.when(i == 0)
    def _prime():
        x_buf[...] = jnp.zeros_like(x_buf)
        for j in range(X_SLOTS - 1):
            @pl.when(j < n_act)
            def _():
                x_copy(j).start()

    @pl.when(i + (X_SLOTS - 1) < n_act)
    def _prefetch():
        x_copy(i + (X_SLOTS - 1)).start()

    def weight_copies(e, slot):
        return (pltpu.make_async_copy(wg_hbm.at[e], wg_buf.at[slot], sems.at[slot]),
                pltpu.make_async_copy(wu_hbm.at[e], wu_buf.at[slot], sems.at[slot]),
                pltpu.make_async_copy(wd_hbm.at[e], wd_buf.at[slot], sems.at[slot]))

    @pl.when((i < n_act_ref[0]) & (first_ref[i] == 1))
    def _new_expert():
        slot = slot_ref[i]

        @pl.when(i == 0)
        def _():
            for c in weight_copies(blk_e_ref[0], 0):
                c.start()

        for c in weight_copies(blk_e_ref[i], slot):
            c.wait()

        @pl.when(next_e_ref[i] >= 0)
        def _():
            for c in weight_copies(next_e_ref[i], 1 - slot):
                c.start()

        wg_bf[...] = wg_buf[slot].astype(bf16)
        wu_bf[...] = wu_buf[slot].astype(bf16)
        wd_bf[...] = wd_buf[slot].astype(bf16)

    @pl.when(i < n_act)
    def _compute():
        x_copy(i).wait()
        x = _unpack_bf16_pairs(x_buf[i % X_SLOTS])
        g = _dot(x, wg_bf[...])
        u = _dot(x, wu_bf[...])
        a = (_silu(g) * u).astype(bf16)
        y = _dot(a, wd_bf[...])

        @pl.when(i >= Y_SLOTS)
        def _():
            y_copy(i - Y_SLOTS).wait()

        _store_rows(y_buf, y, lead=(i % Y_SLOTS,))
        y_copy(i).start()

    @pl.when(i == n_act - 1)
    def _drain():
        for d in range(Y_SLOTS):
            @pl.when(i - d >= 0)
            def _():
                y_copy(i - d).wait()


def _experts(xs, blk_e, n_act, row_start, cnt, w_gate, w_up, w_down):
    D = D_MODEL
    BM = EXPERT_ROWS
    F = EXPERT_DIM
    n_blocks = xs.shape[0] // BM
    blk_in_expert = jnp.arange(n_blocks, dtype=jnp.int32) - row_start[blk_e] // BM
    valid = jnp.clip(cnt[blk_e] - blk_in_expert * BM, 0, BM).astype(jnp.int32)
    ids = jnp.arange(n_blocks, dtype=jnp.int32)
    active = ids < n_act[0]
    first = active & ((ids == 0) | (blk_e != jnp.roll(blk_e, 1)))
    slot = ((jnp.cumsum(first.astype(jnp.int32)) - 1) % 2).astype(jnp.int32)
    first_pos = jnp.where(first, ids, n_blocks)
    later_first = lax.cummin(jnp.concatenate([first_pos[1:], jnp.full((1,), n_blocks, jnp.int32)]), reverse=True)
    next_e = jnp.where(later_first < n_blocks, blk_e[jnp.minimum(later_first, n_blocks - 1)], -1).astype(jnp.int32)

    grid_spec = pltpu.PrefetchScalarGridSpec(
        num_scalar_prefetch=6,
        grid=(n_blocks,),
        in_specs=[pl.BlockSpec(memory_space=pl.ANY)] * 4,
        out_specs=pl.BlockSpec(memory_space=pl.ANY),
        scratch_shapes=[
            pltpu.VMEM((X_SLOTS, BM, D // 2), jnp.uint32),
            pltpu.VMEM((Y_SLOTS, BM * ROW_TILE, LANES), jnp.float32),
            pltpu.VMEM((2, D, F), jnp.float32), pltpu.VMEM((2, D, F), jnp.float32),
            pltpu.VMEM((2, F, D), jnp.float32),
            pltpu.VMEM((D, F), jnp.bfloat16), pltpu.VMEM((D, F), jnp.bfloat16),
            pltpu.VMEM((F, D), jnp.bfloat16),
            pltpu.SemaphoreType.DMA((2,)), pltpu.SemaphoreType.DMA((X_SLOTS,)),
            pltpu.SemaphoreType.DMA((Y_SLOTS,)),
        ],
    )
    return pl.pallas_call(
        _experts_kernel,
        grid_spec=grid_spec,
        out_shape=jax.ShapeDtypeStruct((xs.shape[0] * ROW_TILE, LANES), jnp.float32),
        compiler_params=pltpu.CompilerParams(
            dimension_semantics=("arbitrary",), vmem_limit_bytes=VMEM_LIMIT_BYTES),
    )(blk_e, first.astype(jnp.int32), slot, next_e, valid, n_act, xs, w_gate, w_up, w_down)


def _finish_kernel(h_ref, part_ref, w_ref, *rest):
    slabs = rest[:SC_COMBINE_SLOTS]
    g_ref, b_ref, out_ref = rest[SC_COMBINE_SLOTS:]
    R = h_ref.shape[0]
    w = w_ref[...]
    ffn = part_ref[...]
    for k in range(SC_COMBINE_SLOTS):
        ffn = ffn + _load_rows(slabs[k], R) * w[:, k:k + 1]
    out_ref[...] = _layer_norm(DEEPNORM_ALPHA * h_ref[...] + ffn, g_ref[...], b_ref[...])


def _finish(h2, partial, top_w, gathered, ln_g, ln_b):
    T, D = h2.shape
    R = FINISH_ROWS
    rows = pl.BlockSpec((R, D), lambda i: (i, 0))
    vec = pl.BlockSpec((1, D), lambda i: (0, 0))
    slab = lambda k: pl.BlockSpec((R * ROW_TILE, LANES), lambda i: (k * (T // R) + i, 0))
    return pl.pallas_call(
        _finish_kernel,
        grid=(T // R,),
        in_specs=[rows, rows, pl.BlockSpec((R, TOP_K), lambda i: (i, 0))]
        + [slab(k) for k in range(SC_COMBINE_SLOTS)] + [vec, vec],
        out_specs=rows,
        out_shape=jax.ShapeDtypeStruct((T, D), jnp.float32),
        compiler_params=pltpu.CompilerParams(
            dimension_semantics=("arbitrary",), vmem_limit_bytes=VMEM_LIMIT_BYTES),
    )(h2, partial, top_w, *([gathered] * SC_COMBINE_SLOTS), ln_g.reshape(1, D), ln_b.reshape(1, D))


SC_COMBINE_SLOTS = 5


def _combine_kernel(pos_ref, pos_next_ref, h_ref, w_ref, ys_ref, wsg_ref, wsu_ref, wsd_ref,
                    out_ref, buf_ref, sems):
    R = h_ref.shape[0]
    i = pl.program_id(0)
    slot = i % 2

    def gather(p_ref, s):
        def issue(t, carry):
            for k in range(SC_COMBINE_SLOTS, TOP_K):
                src = p_ref[t * TOP_K + k]
                pltpu.make_async_copy(ys_ref.at[_row_tile(src)], buf_ref.at[s, k - SC_COMBINE_SLOTS, _row_tile(t)],
                                      sems.at[s]).start(priority=k % 2)
            return carry

        lax.fori_loop(0, R, issue, 0)

    @pl.when(i == 0)
    def _():
        gather(pos_ref, 0)

    @pl.when(i + 1 < pl.num_programs(0))
    def _():
        gather(pos_next_ref, 1 - slot)

    h = h_ref[...]
    hb = h.astype(jnp.bfloat16)
    act = (_silu(_dot(hb, wsg_ref[...])) * _dot(hb, wsu_ref[...])).astype(jnp.bfloat16)
    ffn = _dot(act, wsd_ref[...])
    pltpu.make_async_copy(buf_ref.at[slot], buf_ref.at[slot], sems.at[slot]).wait()
    w = w_ref[...]
    for k in range(SC_COMBINE_SLOTS, TOP_K):
        ffn = ffn + _load_rows(buf_ref, R, lead=(slot, k - SC_COMBINE_SLOTS)) * w[:, k:k + 1]
    out_ref[...] = ffn


def _combine(h2, pos, top_w, ys, ws_gate, ws_up, ws_down):
    T, D = h2.shape
    R = COMBINE_ROWS
    F = SHARED_DIM
    bf16 = jnp.bfloat16
    const = lambda shape: pl.BlockSpec(shape, lambda i: (0,) * len(shape))
    return pl.pallas_call(
        _combine_kernel,
        grid=(T // R,),
        in_specs=[
            pl.BlockSpec((R * TOP_K,), lambda i: (i,), memory_space=pltpu.SMEM),
            pl.BlockSpec((R * TOP_K,), lambda i: (jnp.minimum(i + 1, T // R - 1),), memory_space=pltpu.SMEM),
            pl.BlockSpec((R, D), lambda i: (i, 0)),
            pl.BlockSpec((R, TOP_K), lambda i: (i, 0)),
            pl.BlockSpec(memory_space=pl.ANY),
            const((D, F)), const((D, F)), const((F, D)),
        ],
        out_specs=pl.BlockSpec((R, D), lambda i: (i, 0)),
        scratch_shapes=[pltpu.VMEM((2, TOP_K - SC_COMBINE_SLOTS, R * ROW_TILE, LANES), jnp.float32),
                        pltpu.SemaphoreType.DMA((2,))],
        out_shape=jax.ShapeDtypeStruct((T, D), jnp.float32),
        compiler_params=pltpu.CompilerParams(
            dimension_semantics=("arbitrary",), vmem_limit_bytes=VMEM_LIMIT_BYTES),
    )(pos, pos, h2, top_w, ys, ws_gate.astype(bf16), ws_up.astype(bf16), ws_down.astype(bf16))


def _moe(h2, h2_packed, w_router, router_bias, w_gate, w_up, w_down, ws_gate, ws_up, ws_down, ln_g, ln_b):
    T = h2.shape[0]
    E = N_EXPERTS
    BM = EXPERT_ROWS
    e_idx, top_w, rank, counts = _router(h2, w_router, router_bias)
    cnt = counts[:, 0].astype(jnp.int32)
    nblk = (cnt + BM - 1) // BM
    blk_end = jnp.cumsum(nblk)
    row_start = ((blk_end - nblk) * BM).astype(jnp.int32)
    n_blocks = T * TOP_K // BM + E
    n_act = blk_end[-1:].astype(jnp.int32)
    blk_ids = jnp.minimum(jnp.arange(n_blocks, dtype=jnp.int32), n_act[0] - 1)
    blk_e = jnp.minimum(jnp.sum(blk_end[None, :] <= blk_ids[:, None], axis=1), E - 1).astype(jnp.int32)
    pos_kt = _positions(e_idx, rank, row_start).reshape(TOP_K, T)
    pos = pos_kt.T.reshape(T * TOP_K)
    xs = _dispatch_sc(h2_packed, pos_kt, n_blocks * BM)
    ys = _experts(xs, blk_e, n_act, row_start, cnt, w_gate, w_up, w_down)
    gathered = _gather_rows_sc(ys, pos_kt[:SC_COMBINE_SLOTS])
    w_tk = top_w.T
    partial = _combine(h2, pos, w_tk, ys, ws_gate, ws_up, ws_down)
    return _finish(h2, partial, w_tk, gathered, ln_g, ln_b)


def kernel(x, ln_in_g, ln_in_b, w_in, w_out, rel_bias, attn_sinks, ln_mix_g, ln_mix_b, w_router,
           router_bias, w_gate, w_up, w_down, ws_gate, ws_up, ws_down, ln_ffn_g, ln_ffn_b):
    B, S, D = x.shape
    h, h_packed = _mixer(x, ln_in_g, ln_in_b, w_in[0], w_out[0], rel_bias, attn_sinks[0], ln_mix_g[0],
                         ln_mix_b[0])
    out = _moe(h, h_packed, w_router[0], router_bias[0], w_gate[0], w_up[0], w_down[0],
               ws_gate[0], ws_up[0], ws_down[0], ln_ffn_g[0], ln_ffn_b[0])
    return out.reshape(B, S, D)
```

```python
import functools
import math

import jax
import jax.numpy as jnp
from jax import lax
from jax.experimental import pallas as pl
from jax.experimental.pallas import tpu as pltpu
from jax.experimental.pallas import tpu_sc as plsc

D_MODEL = 1024
DEPTH = 1
RET_HEADS = 4
RET_QK_DIM = 64
RET_V_DIM = 128
RET_CHUNK = 128
RET_WIDTH = RET_HEADS * RET_V_DIM
ROPE_BASE = 10000.0
SWA_HEADS = 8
SWA_KV_HEADS = 2
SWA_GROUP = SWA_HEADS // SWA_KV_HEADS
SWA_HEAD_DIM = 64
SWA_WINDOW = 128
SWA_WIDTH = SWA_HEADS * SWA_HEAD_DIM
MIX_WIDTH = RET_WIDTH + SWA_WIDTH
RQK = RET_HEADS * RET_QK_DIM
SKV = SWA_KV_HEADS * SWA_HEAD_DIM
IN_SIZES = (RQK, RQK, RET_WIDTH, RET_WIDTH, SWA_WIDTH, SKV, SKV)
IN_OFFS = tuple(sum(IN_SIZES[:i]) for i in range(len(IN_SIZES)))
IN_WIDTH = sum(IN_SIZES)
REL_BUCKETS = 32
REL_MAX_DIST = 128
N_EXPERTS = 256
TOP_K = 8
N_GROUPS = 8
GROUP_SIZE = N_EXPERTS // N_GROUPS
TOPK_GROUPS = 4
EXPERT_DIM = 256
SHARED_DIM = 256
ROUTED_SCALE = 2.5
LN_EPS = 1e-5
GN_EPS = 1e-6
DEEPNORM_ALPHA = (2 * DEPTH) ** 0.25
MASK_VALUE = -1e30

VMEM_LIMIT_BYTES = 56 * 1024 * 1024

MIX_ROWS = 256
ROUTE_ROWS = 256
DISPATCH_ROWS = 256
EXPERT_ROWS = 256
COMBINE_ROWS = 256
POSITION_ROWS = 64
FINISH_ROWS = 512


def _layer_norm(x, g, b):
    mu = jnp.mean(x, axis=-1, keepdims=True)
    xc = x - mu
    var = jnp.mean(xc * xc, axis=-1, keepdims=True)
    return xc * lax.rsqrt(var + LN_EPS) * g + b


def _dot(a, b):
    return jnp.dot(a, b, preferred_element_type=jnp.float32)


def _dot_nt(a, b):
    return lax.dot_general(a, b, (((1,), (1,)), ((), ())), preferred_element_type=jnp.float32)


def _dot_tn(a, b):
    return lax.dot_general(a, b, (((0,), (0,)), ((), ())), preferred_element_type=jnp.float32)


def _silu(x):
    return x * (1.0 / (1.0 + jnp.exp(-x)))


LANES = 128
ROW_TILE = D_MODEL // LANES


def _load_rows(ref, n_rows, lead=()):
    return jnp.concatenate([ref[lead + (pl.ds(s, n_rows, stride=ROW_TILE), slice(None))]
                            for s in range(ROW_TILE)], axis=1)


def _store_rows(ref, val, lead=(), first_row=0):
    n_rows = val.shape[0]
    for s in range(ROW_TILE):
        dst = pl.ds(first_row * ROW_TILE + s, n_rows, stride=ROW_TILE)
        ref[lead + (dst, slice(None))] = val[:, s * LANES:(s + 1) * LANES]


def _row_tile(r):
    return pl.ds(pl.multiple_of(r * ROW_TILE, ROW_TILE), ROW_TILE)


def _pack_bf16_pairs(x):
    m = x.shape[1] // 2
    bits = lax.bitcast_convert_type(x.astype(jnp.bfloat16).astype(jnp.float32), jnp.uint32)
    return (bits[:, :m] >> 16) | (bits[:, m:] & jnp.uint32(0xFFFF0000))


def _unpack_bf16_pairs(p):
    lo = lax.bitcast_convert_type(p << 16, jnp.float32)
    hi = lax.bitcast_convert_type(p & jnp.uint32(0xFFFF0000), jnp.float32)
    return jnp.concatenate([lo, hi], axis=1).astype(jnp.bfloat16)


def _swap_halves(x):
    n = x.shape[-1]
    half = RET_QK_DIM // 2
    lane = lax.broadcasted_iota(jnp.int32, x.shape, 1)
    from_right = pltpu.roll(x, n - half, axis=1)
    from_left = pltpu.roll(x, half, axis=1)
    return jnp.where((lane % RET_QK_DIM) < half, from_right, from_left)


def _mixer_kernel(rel_bias_ref, x_ref, g_in_ref, b_in_ref, w_in_ref, w_out_ref, rot_ref, decay_ref,
                  zeta_ref, xi_ref, cdecay_ref, bucket_ref, sink_ref, g_mix_ref, b_mix_ref,
                  h2_ref, h2p_ref, state_ref, kprev_ref, vprev_ref, bias_ref):
    b_id = pl.program_id(0)
    c_id = pl.program_id(1)
    W = SWA_WINDOW

    @pl.when((b_id == 0) & (c_id == 0))
    def _build_bias():
        bucket = bucket_ref[...]
        for h in range(SWA_HEADS):
            acc = jnp.full((2 * W, W), MASK_VALUE, jnp.float32)
            for b in range(REL_BUCKETS):
                acc = jnp.where(bucket == b, rel_bias_ref[b, h], acc)
            kh, g = divmod(h, SWA_GROUP)
            bias_ref[kh, :, g * W:(g + 1) * W] = acc

    @pl.when(c_id == 0)
    def _reset():
        state_ref[...] = jnp.zeros_like(state_ref)
        kprev_ref[...] = jnp.zeros_like(kprev_ref)
        vprev_ref[...] = jnp.zeros_like(vprev_ref)

    h = _layer_norm(x_ref[...], g_in_ref[...], b_in_ref[...])
    proj = _dot(h.astype(jnp.bfloat16), w_in_ref[...])

    o_q, o_k, o_v, o_g, o_sq, o_sk, o_sv = IN_OFFS
    cos_t = rot_ref[:, :RQK]
    sin_t = rot_ref[:, RQK:]
    q_all = proj[:, o_q:o_q + RQK]
    k_all = proj[:, o_k:o_k + RQK]
    q_rot = q_all * cos_t + _swap_halves(q_all) * sin_t
    k_rot = (k_all * cos_t + _swap_halves(k_all) * sin_t) * (RET_QK_DIM ** -0.5)

    n_sub = x_ref.shape[0] // RET_CHUNK
    states = [state_ref[hh] for hh in range(RET_HEADS)]
    k_prev = kprev_ref[...]
    v_prev = vprev_ref[...]
    cat_rows = []
    for s in range(n_sub):
        r0 = s * RET_CHUNK
        rows = slice(r0, r0 + RET_CHUNK)
        pieces = []
        for hh in range(RET_HEADS):
            qk = slice(hh * RET_QK_DIM, (hh + 1) * RET_QK_DIM)
            vv = slice(o_v + hh * RET_V_DIM, o_v + (hh + 1) * RET_V_DIM)
            gg = slice(o_g + hh * RET_V_DIM, o_g + (hh + 1) * RET_V_DIM)
            q = q_rot[rows, qk].astype(jnp.bfloat16)
            k32 = k_rot[rows, qk]
            v = proj[rows, vv].astype(jnp.bfloat16)
            scores = _dot_nt(q, k32.astype(jnp.bfloat16)) * decay_ref[hh]
            intra = _dot(scores.astype(jnp.bfloat16), v)
            inter = _dot(q, states[hh].astype(jnp.bfloat16)) * xi_ref[hh]
            ret = intra + inter
            kz = (k32 * zeta_ref[hh]).astype(jnp.bfloat16)
            states[hh] = states[hh] * cdecay_ref[hh] + _dot_tn(kz, v)
            mu = jnp.mean(ret, axis=-1, keepdims=True)
            rc = ret - mu
            var = jnp.mean(rc * rc, axis=-1, keepdims=True)
            normed = rc * lax.rsqrt(var + GN_EPS)
            pieces.append((_silu(proj[rows, gg]) * normed).astype(jnp.bfloat16))
        k_cur = proj[rows, o_sk:o_sk + SKV].astype(jnp.bfloat16)
        v_cur = proj[rows, o_sv:o_sv + SKV].astype(jnp.bfloat16)
        for kh in range(SWA_KV_HEADS):
            kv = slice(kh * SWA_HEAD_DIM, (kh + 1) * SWA_HEAD_DIM)
            q4 = jnp.concatenate(
                [proj[rows, o_sq + (kh * SWA_GROUP + g) * SWA_HEAD_DIM:
                      o_sq + (kh * SWA_GROUP + g + 1) * SWA_HEAD_DIM] for g in range(SWA_GROUP)],
                axis=0) * (SWA_HEAD_DIM ** -0.5)
            kcat = jnp.concatenate([k_prev[:, kv], k_cur[:, kv]], axis=0)
            vcat = jnp.concatenate([v_prev[:, kv], v_cur[:, kv]], axis=0)
            logits = _dot_nt(kcat, q4.astype(jnp.bfloat16)) + bias_ref[kh]
            if s == 0:
                key = lax.broadcasted_iota(jnp.int32, logits.shape, 0)
                logits = logits + jnp.where((key < W) & (c_id == 0), MASK_VALUE, 0.0)
            sink = sink_ref[kh]
            m = jnp.maximum(jnp.max(logits, axis=0, keepdims=True), sink)
            p = jnp.exp(logits - m)
            den = jnp.sum(p, axis=0, keepdims=True) + jnp.exp(sink - m)
            probs = (p / den).astype(jnp.bfloat16)
            o4 = _dot_tn(vcat, probs)
            pieces.extend(o4[:, g * W:(g + 1) * W].T.astype(jnp.bfloat16) for g in range(SWA_GROUP))
        k_prev, v_prev = k_cur, v_cur
        cat_rows.append(jnp.concatenate(pieces, axis=1))
    for hh in range(RET_HEADS):
        state_ref[hh] = states[hh]
    kprev_ref[...] = k_prev
    vprev_ref[...] = v_prev

    mix = _dot(jnp.concatenate(cat_rows, axis=0), w_out_ref[...])
    h2 = _layer_norm(DEEPNORM_ALPHA * h + mix, g_mix_ref[...], b_mix_ref[...])
    h2_ref[...] = h2
    h2p_ref[...] = _pack_bf16_pairs(h2)


def _t5_bucket(dist):
    n = jnp.maximum(dist, 0)
    max_exact = REL_BUCKETS // 2
    ratio = jnp.log(jnp.maximum(n, 1).astype(jnp.float32) / max_exact) / math.log(REL_MAX_DIST / max_exact)
    large = jnp.minimum(max_exact + (ratio * (REL_BUCKETS - max_exact)).astype(jnp.int32), REL_BUCKETS - 1)
    return jnp.where(n < max_exact, n, large)


def _mixer(x, ln_in_g, ln_in_b, w_in, w_out, rel_bias, sinks, ln_mix_g, ln_mix_b):
    B, S, D = x.shape
    R = MIX_ROWS
    C = RET_CHUNK
    W = SWA_WINDOW
    f32 = jnp.float32
    half = RET_QK_DIM // 2
    inv = ROPE_BASE ** (-jnp.arange(half, dtype=f32) / half)
    ang = jnp.arange(S, dtype=f32)[:, None] * inv[None, :]
    cos, sin = jnp.cos(ang), jnp.sin(ang)
    cos_t = jnp.tile(jnp.concatenate([cos, cos], axis=-1), (1, RET_HEADS))
    sin_t = jnp.tile(jnp.concatenate([-sin, sin], axis=-1), (1, RET_HEADS))
    rot = jnp.concatenate([cos_t, sin_t], axis=-1)
    log_gamma = jnp.log(1.0 - 2.0 ** (-5.0 - jnp.arange(RET_HEADS, dtype=f32)))
    idx = jnp.arange(C, dtype=f32)
    diff = idx[:, None] - idx[None, :]
    decay = jnp.where(diff[None] >= 0, jnp.exp(jnp.maximum(diff, 0.0)[None] * log_gamma[:, None, None]), 0.0)
    zeta = jnp.exp((C - 1.0 - idx)[None, :] * log_gamma[:, None])
    xi = jnp.exp((idx + 1.0)[None, :] * log_gamma[:, None])
    zeta_b = jnp.broadcast_to(zeta[:, :, None], (RET_HEADS, C, RET_QK_DIM))
    xi_b = jnp.broadcast_to(xi[:, :, None], (RET_HEADS, C, RET_V_DIM))
    cdecay = jnp.broadcast_to(jnp.exp(C * log_gamma)[:, None, None], (RET_HEADS, RET_QK_DIM, RET_V_DIM))
    i = jnp.arange(W)
    j = jnp.arange(2 * W)
    dist = i[:, None] + W - j[None, :]
    bucket = jnp.where((dist >= 0) & (dist < W), _t5_bucket(dist), -1).astype(jnp.int32).T
    sink_row = jnp.repeat(sinks.astype(f32), W).reshape(SWA_KV_HEADS, 1, SWA_GROUP * W)

    const = lambda shape: pl.BlockSpec(shape, lambda b, c, *_: (0,) * len(shape))
    grid_spec = pltpu.PrefetchScalarGridSpec(
        num_scalar_prefetch=1,
        grid=(B, S // R),
        in_specs=[
            pl.BlockSpec((None, R, D), lambda b, c, *_: (b, c, 0)),
            const((1, D)), const((1, D)),
            const((D, IN_WIDTH)), const((MIX_WIDTH, D)),
            pl.BlockSpec((R, 2 * RQK), lambda b, c, *_: (c, 0)),
            const((RET_HEADS, C, C)), const((RET_HEADS, C, RET_QK_DIM)), const((RET_HEADS, C, RET_V_DIM)),
            const((RET_HEADS, RET_QK_DIM, RET_V_DIM)),
            const((2 * W, W)), const((SWA_KV_HEADS, 1, SWA_GROUP * W)),
            const((1, D)), const((1, D)),
        ],
        out_specs=[pl.BlockSpec((R, D), lambda b, c, *_: (b * (S // R) + c, 0)),
                   pl.BlockSpec((R, D // 2), lambda b, c, *_: (b * (S // R) + c, 0))],
        scratch_shapes=[
            pltpu.VMEM((RET_HEADS, RET_QK_DIM, RET_V_DIM), f32),
            pltpu.VMEM((W, SKV), jnp.bfloat16),
            pltpu.VMEM((W, SKV), jnp.bfloat16),
            pltpu.VMEM((SWA_KV_HEADS, 2 * W, SWA_GROUP * W), f32),
        ],
    )
    return pl.pallas_call(
        _mixer_kernel,
        grid_spec=grid_spec,
        out_shape=[jax.ShapeDtypeStruct((B * S, D), f32), jax.ShapeDtypeStruct((B * S, D // 2), jnp.uint32)],
        compiler_params=pltpu.CompilerParams(
            dimension_semantics=("arbitrary", "arbitrary"), vmem_limit_bytes=VMEM_LIMIT_BYTES),
    )(rel_bias.astype(f32), x, ln_in_g.reshape(1, D), ln_in_b.reshape(1, D),
      w_in.astype(jnp.bfloat16), w_out.astype(jnp.bfloat16), rot, decay, zeta_b, xi_b, cdecay,
      bucket, sink_row, ln_mix_g.reshape(1, D), ln_mix_b.reshape(1, D))


def _router_kernel(h_ref, wr_ref, rb_ref, e_ref, w_ref, rk_ref, cnt_ref, run_ref):
    f32 = jnp.float32
    R = h_ref.shape[0]
    E = N_EXPERTS
    neg = -jnp.inf

    @pl.when(pl.program_id(0) == 0)
    def _init():
        run_ref[...] = jnp.zeros_like(run_ref)

    logits = _dot_nt(wr_ref[...], h_ref[...].astype(jnp.bfloat16))
    scores = 1.0 / (1.0 + jnp.exp(-logits))
    choice = scores + rb_ref[...]
    eid = lax.broadcasted_iota(jnp.int32, (E, R), 0)

    def first_argmax(vals, ids, none):
        m = jnp.max(vals, axis=0, keepdims=True)
        idx = jnp.min(jnp.where(vals == m, ids, none), axis=0, keepdims=True)
        return m, idx

    gid = lax.broadcasted_iota(jnp.int32, (GROUP_SIZE, R), 0)
    groups, gscore = [], []
    for g in range(N_GROUPS):
        vals = choice[g * GROUP_SIZE:(g + 1) * GROUP_SIZE]
        m1, i1 = first_argmax(vals, gid, GROUP_SIZE)
        m2 = jnp.max(jnp.where(gid == i1, neg, vals), axis=0, keepdims=True)
        groups.append(vals)
        gscore.append(m1 + m2)
    kept = []
    for g in range(N_GROUPS):
        beaten = jnp.zeros((1, R), f32)
        for g2 in range(N_GROUPS):
            if g2 == g:
                continue
            ahead = (gscore[g2] > gscore[g]) | (gscore[g2] == gscore[g]) if g2 < g else gscore[g2] > gscore[g]
            beaten = beaten + jnp.where(ahead, 1.0, 0.0)
        kept.append(jnp.where(beaten < TOPK_GROUPS, groups[g], neg))
    masked = jnp.concatenate(kept, axis=0)

    idxs, wts = [], []
    picked = jnp.zeros((E, R), f32)
    for _ in range(TOP_K):
        _, idx = first_argmax(masked, eid, E)
        hit = eid == idx
        idxs.append(idx)
        wts.append(jnp.sum(jnp.where(hit, scores, 0.0), axis=0, keepdims=True))
        masked = jnp.where(hit, neg, masked)
        picked = jnp.where(hit, 1.0, picked)
    wsum = wts[0]
    for k in range(1, TOP_K):
        wsum = wsum + wts[k]

    row = lax.broadcasted_iota(jnp.int32, (R, R), 0)
    col = lax.broadcasted_iota(jnp.int32, (R, R), 1)
    earlier = jnp.where(row < col, 1.0, 0.0).astype(jnp.bfloat16)
    picked_bf = picked.astype(jnp.bfloat16)
    run = run_ref[...]
    before = _dot(picked_bf, earlier) + jnp.concatenate([run] * (R // LANES), axis=1)
    sub_k = lax.broadcasted_iota(jnp.int32, (TOP_K, R), 0)
    e_out = jnp.zeros((TOP_K, R), jnp.int32)
    w_out = jnp.zeros((TOP_K, R), f32)
    rk_out = jnp.zeros((TOP_K, R), jnp.int32)
    for k in range(TOP_K):
        rank_k = jnp.sum(jnp.where(eid == idxs[k], before, 0.0), axis=0, keepdims=True)
        e_out = jnp.where(sub_k == k, idxs[k], e_out)
        w_out = jnp.where(sub_k == k, wts[k] / wsum * ROUTED_SCALE, w_out)
        rk_out = jnp.where(sub_k == k, rank_k.astype(jnp.int32), rk_out)
    e_ref[...] = e_out
    w_ref[...] = w_out
    rk_ref[...] = rk_out
    run_ref[...] = run + _dot(picked_bf, jnp.ones((R, LANES), jnp.bfloat16))
    cnt_ref[...] = run_ref[...]


def _router(h2, w_router, router_bias):
    T, D = h2.shape
    R = ROUTE_ROWS
    E = N_EXPERTS
    return pl.pallas_call(
        _router_kernel,
        grid=(T // R,),
        in_specs=[
            pl.BlockSpec((R, D), lambda i: (i, 0)),
            pl.BlockSpec((E, D), lambda i: (0, 0)),
            pl.BlockSpec((E, R), lambda i: (0, 0)),
        ],
        out_specs=[
            pl.BlockSpec((TOP_K, R), lambda i: (0, i)),
            pl.BlockSpec((TOP_K, R), lambda i: (0, i)),
            pl.BlockSpec((TOP_K, R), lambda i: (0, i)),
            pl.BlockSpec((E, LANES), lambda i: (0, 0)),
        ],
        out_shape=[
            jax.ShapeDtypeStruct((TOP_K, T), jnp.int32),
            jax.ShapeDtypeStruct((TOP_K, T), jnp.float32),
            jax.ShapeDtypeStruct((TOP_K, T), jnp.int32),
            jax.ShapeDtypeStruct((E, LANES), jnp.float32),
        ],
        scratch_shapes=[pltpu.VMEM((E, LANES), jnp.float32)],
        compiler_params=pltpu.CompilerParams(
            dimension_semantics=("arbitrary",), vmem_limit_bytes=VMEM_LIMIT_BYTES),
    )(h2, w_router.T.astype(jnp.bfloat16), jnp.broadcast_to(router_bias.astype(jnp.float32)[:, None], (E, R)))


def _positions_kernel(row_start_ref, e_ref, rk_ref, pos_ref):
    e = e_ref[...]
    rk = rk_ref[...]

    def per_expert(i, pos):
        return jnp.where(e == i, rk + row_start_ref[i], pos)

    pos_ref[...] = lax.fori_loop(0, N_EXPERTS, per_expert, jnp.zeros_like(rk))


def _positions(e_idx, rank, row_start):
    n = e_idx.size
    shape = (n // LANES, LANES)
    block = pl.BlockSpec((POSITION_ROWS, LANES), lambda i, *_: (i, 0))
    grid_spec = pltpu.PrefetchScalarGridSpec(
        num_scalar_prefetch=1,
        grid=(shape[0] // POSITION_ROWS,),
        in_specs=[block, block],
        out_specs=block,
    )
    pos = pl.pallas_call(
        _positions_kernel,
        grid_spec=grid_spec,
        out_shape=jax.ShapeDtypeStruct(shape, jnp.int32),
    )(row_start, e_idx.reshape(shape), rank.reshape(shape))
    return pos.reshape(n)


def _dispatch_kernel(row_start_ref, cnt_ref, n_act_ref, pos_ref, h_ref, xs_ref, zero_ref, sem, zsem,
                     *, n_tokens):
    R = h_ref.shape[0] // ROW_TILE
    BM = EXPERT_ROWS
    n_blocks = xs_ref.shape[0] // (BM * ROW_TILE)
    n_pad_units = n_blocks - n_tokens * TOP_K // BM

    @pl.when(pl.program_id(0) == 0)
    def _zero_padding():
        zero_ref[...] = jnp.zeros_like(zero_ref)

        def expert_tail(e, carry):
            n_tail = pl.multiple_of(((BM - cnt_ref[e] % BM) % BM) * ROW_TILE, ROW_TILE)

            @pl.when(n_tail > 0)
            def _():
                dst = pl.multiple_of((row_start_ref[e] + cnt_ref[e]) * ROW_TILE, ROW_TILE)
                pltpu.make_async_copy(zero_ref.at[pl.ds(0, n_tail)], xs_ref.at[pl.ds(dst, n_tail)], zsem).start()
            return carry

        lax.fori_loop(0, N_EXPERTS, expert_tail, 0)

        def idle_block(i, carry):
            dst = pl.multiple_of(i * (BM * ROW_TILE), BM * ROW_TILE)
            pltpu.make_async_copy(zero_ref, xs_ref.at[pl.ds(dst, BM * ROW_TILE)], zsem).start()
            return carry

        lax.fori_loop(n_act_ref[0], n_blocks, idle_block, 0)

        def drain(i, carry):
            pltpu.make_async_copy(zero_ref, xs_ref.at[pl.ds(0, BM * ROW_TILE)], zsem).wait()
            return carry

        lax.fori_loop(0, n_pad_units, drain, 0)

    def issue(t, carry):
        src = h_ref.at[_row_tile(t)]
        for k in range(TOP_K):
            dest = pos_ref[t * TOP_K + k]
            pltpu.make_async_copy(src, xs_ref.at[_row_tile(dest)], sem).start(priority=k % 2)
        return carry

    lax.fori_loop(0, R, issue, 0)
    n = R * TOP_K * ROW_TILE
    pltpu.make_async_copy(xs_ref.at[pl.ds(0, n)], xs_ref.at[pl.ds(0, n)], sem).wait()


def _dispatch(h2, pos, row_start, cnt, n_act, n_rows):
    T = h2.shape[0] // ROW_TILE
    R = DISPATCH_ROWS
    grid_spec = pltpu.PrefetchScalarGridSpec(
        num_scalar_prefetch=3,
        grid=(T // R,),
        in_specs=[
            pl.BlockSpec((R * TOP_K,), lambda i, *_: (i,), memory_space=pltpu.SMEM),
            pl.BlockSpec((R * ROW_TILE, LANES), lambda i, *_: (i, 0)),
        ],
        out_specs=pl.BlockSpec(memory_space=pl.ANY),
        scratch_shapes=[pltpu.VMEM((EXPERT_ROWS * ROW_TILE, LANES), jnp.float32),
                        pltpu.SemaphoreType.DMA(()), pltpu.SemaphoreType.DMA(())],
    )
    return pl.pallas_call(
        functools.partial(_dispatch_kernel, n_tokens=T),
        grid_spec=grid_spec,
        out_shape=jax.ShapeDtypeStruct((n_rows * ROW_TILE, LANES), jnp.float32),
        compiler_params=pltpu.CompilerParams(
            dimension_semantics=("arbitrary",), vmem_limit_bytes=VMEM_LIMIT_BYTES),
    )(row_start, cnt, n_act, pos, h2)


SC_WINDOW = 32


def _dispatch_sc(h2, pos_kt, n_rows):
    T, D = h2.shape
    W = SC_WINDOW
    idx = _window_indices(pos_kt, W)
    idx_rows = TOP_K * W // LANES
    mesh = plsc.VectorSubcoreMesh(core_axis_name="core", subcore_axis_name="subcore")

    @pl.kernel(out_type=jax.ShapeDtypeStruct((n_rows, D), h2.dtype), mesh=mesh, scratch_types=[])
    def scatter_rows(x_hbm, i_hbm, o_hbm):
        def body(x_vmem, i_vmem):
            for k in range(TOP_K):
                r, q = divmod(k * W, LANES)
                pltpu.sync_copy(x_vmem, o_hbm.at[i_vmem.at[r, pl.ds(q, W)]])

        pltpu.emit_pipeline(
            body,
            grid=(T // W,),
            in_specs=[pl.BlockSpec((W, D), lambda i: (i, 0)),
                      pl.BlockSpec((idx_rows, LANES), lambda i: (i, 0))],
            out_specs=[],
            core_axis_name=("core", "subcore"),
            dimension_semantics=(pltpu.PARALLEL,),
        )(x_hbm, i_hbm)

    return scatter_rows(h2, idx)


def _window_indices(pos_kt, window):
    K, T = pos_kt.shape
    return pos_kt.reshape(K, T // window, window).transpose(1, 0, 2).reshape(T * K // LANES, LANES)


def _gather_rows_sc(ys, pos_kt):
    n = pos_kt.size
    W = SC_WINDOW
    ys3 = ys.reshape(ys.shape[0] // ROW_TILE, ROW_TILE, LANES)
    idx = jnp.pad(pos_kt.reshape(n // W, W), ((0, 0), (0, LANES - W)))
    mesh = plsc.VectorSubcoreMesh(core_axis_name="core", subcore_axis_name="subcore")

    @pl.kernel(out_type=jax.ShapeDtypeStruct((n, ROW_TILE, LANES), ys.dtype), mesh=mesh, scratch_types=[])
    def gather_rows(y_hbm, i_hbm, o_hbm):
        def body(i_vmem, o_vmem):
            pltpu.sync_copy(y_hbm.at[i_vmem.at[0, pl.ds(0, W)]], o_vmem)

        pltpu.emit_pipeline(
            body,
            grid=(n // W,),
            in_specs=[pl.BlockSpec((1, LANES), lambda i: (i, 0))],
            out_specs=[pl.BlockSpec((W, ROW_TILE, LANES), lambda i: (i, 0, 0))],
            core_axis_name=("core", "subcore"),
            dimension_semantics=(pltpu.PARALLEL,),
        )(i_hbm, o_hbm)

    return gather_rows(ys3, idx).reshape(n * ROW_TILE, LANES)


EXPERT_SUBBLOCKS = 2
X_SLOTS = 4
Y_SLOTS = 3


def _experts_kernel(blk_e_ref, first_ref, slot_ref, next_e_ref, valid_ref, n_act_ref, xs_hbm, wg_hbm,
                    wu_hbm, wd_hbm, ys_hbm, x_buf, y_buf, wg_buf, wu_buf, wd_buf, wg_bf, wu_bf, wd_bf,
                    sems, x_sems, y_sems):
    i = pl.program_id(0)
    n_act = n_act_ref[0]
    bf16 = jnp.bfloat16
    blk = EXPERT_ROWS * ROW_TILE

    def x_copy(j):
        n = pl.multiple_of((valid_ref[j] + 7) // 8 * 8, 8)
        src = xs_hbm.at[pl.ds(pl.multiple_of(j * EXPERT_ROWS, EXPERT_ROWS), n)]
        return pltpu.make_async_copy(src, x_buf.at[j % X_SLOTS, pl.ds(0, n)], x_sems.at[j % X_SLOTS])

    def y_copy(j):
        n = pl.multiple_of(valid_ref[j] * ROW_TILE, ROW_TILE)
        dst = ys_hbm.at[pl.ds(pl.multiple_of(j * blk, blk), n)]
        return pltpu.make_async_copy(y_buf.at[j % Y_SLOTS, pl.ds(0, n)], dst, y_sems.at[j % Y_SLOTS])

    @pl.when(i == 0)
    def _prime():
        x_buf[...] = jnp.zeros_like(x_buf)
        for j in range(X_SLOTS - 1):
            @pl.when(j < n_act)
            def _():
                x_copy(j).start()

    @pl.when(i + (X_SLOTS - 1) < n_act)
    def _prefetch():
        x_copy(i + (X_SLOTS - 1)).start()

    def weight_copies(e, slot):
        return (pltpu.make_async_copy(wg_hbm.at[e], wg_buf.at[slot], sems.at[slot]),
                pltpu.make_async_copy(wu_hbm.at[e], wu_buf.at[slot], sems.at[slot]),
                pltpu.make_async_copy(wd_hbm.at[e], wd_buf.at[slot], sems.at[slot]))

    @pl.when((i < n_act_ref[0]) & (first_ref[i] == 1))
    def _new_expert():
        slot = slot_ref[i]

        @pl.when(i == 0)
        def _():
            for c in weight_copies(blk_e_ref[0], 0):
                c.start()

        for c in weight_copies(blk_e_ref[i], slot):
            c.wait()

        @pl.when(next_e_ref[i] >= 0)
        def _():
            for c in weight_copies(next_e_ref[i], 1 - slot):
                c.start()

        wg_bf[...] = wg_buf[slot].astype(bf16)
        wu_bf[...] = wu_buf[slot].astype(bf16)
        wd_bf[...] = wd_buf[slot].astype(bf16)

    @pl.when(i < n_act)
    def _compute():
        x_copy(i).wait()

        @pl.when(i >= Y_SLOTS)
        def _():
            y_copy(i - Y_SLOTS).wait()

        sub = EXPERT_ROWS // EXPERT_SUBBLOCKS
        xs_ = [_unpack_bf16_pairs(x_buf[i % X_SLOTS, pl.ds(part * sub, sub), :]) for part in range(EXPERT_SUBBLOCKS)]
        gs = [_dot(x, wg_bf[...]) for x in xs_]
        us = [_dot(x, wu_bf[...]) for x in xs_]
        acts = [(_silu(g) * u).astype(bf16) for g, u in zip(gs, us)]
        for part in range(EXPERT_SUBBLOCKS):
            _store_rows(y_buf, _dot(acts[part], wd_bf[...]), lead=(i % Y_SLOTS,), first_row=part * sub)
        y_copy(i).start()

    @pl.when(i == n_act - 1)
    def _drain():
        for d in range(Y_SLOTS):
            @pl.when(i - d >= 0)
            def _():
                y_copy(i - d).wait()


def _experts(xs, blk_e, n_act, row_start, cnt, w_gate, w_up, w_down):
    D = D_MODEL
    BM = EXPERT_ROWS
    F = EXPERT_DIM
    n_blocks = xs.shape[0] // BM
    blk_in_expert = jnp.arange(n_blocks, dtype=jnp.int32) - row_start[blk_e] // BM
    valid = jnp.clip(cnt[blk_e] - blk_in_expert * BM, 0, BM).astype(jnp.int32)
    ids = jnp.arange(n_blocks, dtype=jnp.int32)
    active = ids < n_act[0]
    first = active & ((ids == 0) | (blk_e != jnp.roll(blk_e, 1)))
    slot = ((jnp.cumsum(first.astype(jnp.int32)) - 1) % 2).astype(jnp.int32)
    first_pos = jnp.where(first, ids, n_blocks)
    later_first = lax.cummin(jnp.concatenate([first_pos[1:], jnp.full((1,), n_blocks, jnp.int32)]), reverse=True)
    next_e = jnp.where(later_first < n_blocks, blk_e[jnp.minimum(later_first, n_blocks - 1)], -1).astype(jnp.int32)

    grid_spec = pltpu.PrefetchScalarGridSpec(
        num_scalar_prefetch=6,
        grid=(n_blocks,),
        in_specs=[pl.BlockSpec(memory_space=pl.ANY)] * 4,
        out_specs=pl.BlockSpec(memory_space=pl.ANY),
        scratch_shapes=[
            pltpu.VMEM((X_SLOTS, BM, D // 2), jnp.uint32),
            pltpu.VMEM((Y_SLOTS, BM * ROW_TILE, LANES), jnp.float32),
            pltpu.VMEM((2, D, F), jnp.float32), pltpu.VMEM((2, D, F), jnp.float32),
            pltpu.VMEM((2, F, D), jnp.float32),
            pltpu.VMEM((D, F), jnp.bfloat16), pltpu.VMEM((D, F), jnp.bfloat16),
            pltpu.VMEM((F, D), jnp.bfloat16),
            pltpu.SemaphoreType.DMA((2,)), pltpu.SemaphoreType.DMA((X_SLOTS,)),
            pltpu.SemaphoreType.DMA((Y_SLOTS,)),
        ],
    )
    return pl.pallas_call(
        _experts_kernel,
        grid_spec=grid_spec,
        out_shape=jax.ShapeDtypeStruct((xs.shape[0] * ROW_TILE, LANES), jnp.float32),
        compiler_params=pltpu.CompilerParams(
            dimension_semantics=("arbitrary",), vmem_limit_bytes=VMEM_LIMIT_BYTES),
    )(blk_e, first.astype(jnp.int32), slot, next_e, valid, n_act, xs, w_gate, w_up, w_down)


def _finish_kernel(h_ref, part_ref, w_ref, *rest):
    slabs = rest[:SC_COMBINE_SLOTS]
    g_ref, b_ref, out_ref = rest[SC_COMBINE_SLOTS:]
    R = h_ref.shape[0]
    w = w_ref[...]
    ffn = part_ref[...]
    for k in range(SC_COMBINE_SLOTS):
        ffn = ffn + _load_rows(slabs[k], R) * w[:, k:k + 1]
    out_ref[...] = _layer_norm(DEEPNORM_ALPHA * h_ref[...] + ffn, g_ref[...], b_ref[...])


def _finish(h2, partial, top_w, gathered, ln_g, ln_b):
    T, D = h2.shape
    R = FINISH_ROWS
    rows = pl.BlockSpec((R, D), lambda i: (i, 0))
    vec = pl.BlockSpec((1, D), lambda i: (0, 0))
    slab = lambda k: pl.BlockSpec((R * ROW_TILE, LANES), lambda i: (k * (T // R) + i, 0))
    return pl.pallas_call(
        _finish_kernel,
        grid=(T // R,),
        in_specs=[rows, rows, pl.BlockSpec((R, TOP_K), lambda i: (i, 0))]
        + [slab(k) for k in range(SC_COMBINE_SLOTS)] + [vec, vec],
        out_specs=rows,
        out_shape=jax.ShapeDtypeStruct((T, D), jnp.float32),
        compiler_params=pltpu.CompilerParams(
            dimension_semantics=("arbitrary",), vmem_limit_bytes=VMEM_LIMIT_BYTES),
    )(h2, partial, top_w, *([gathered] * SC_COMBINE_SLOTS), ln_g.reshape(1, D), ln_b.reshape(1, D))


SC_COMBINE_SLOTS = 5


def _combine_kernel(pos_ref, pos_next_ref, h_ref, w_ref, ys_ref, wsg_ref, wsu_ref, wsd_ref,
                    out_ref, buf_ref, sems):
    R = h_ref.shape[0]
    i = pl.program_id(0)
    slot = i % 2

    def gather(p_ref, s):
        def issue(t, carry):
            for k in range(SC_COMBINE_SLOTS, TOP_K):
                src = p_ref[t * TOP_K + k]
                pltpu.make_async_copy(ys_ref.at[_row_tile(src)], buf_ref.at[s, k - SC_COMBINE_SLOTS, _row_tile(t)],
                                      sems.at[s]).start(priority=k % 2)
            return carry

        lax.fori_loop(0, R, issue, 0)

    @pl.when(i == 0)
    def _():
        gather(pos_ref, 0)

    @pl.when(i + 1 < pl.num_programs(0))
    def _():
        gather(pos_next_ref, 1 - slot)

    h = h_ref[...]
    hb = h.astype(jnp.bfloat16)
    act = (_silu(_dot(hb, wsg_ref[...])) * _dot(hb, wsu_ref[...])).astype(jnp.bfloat16)
    ffn = _dot(act, wsd_ref[...])
    pltpu.make_async_copy(buf_ref.at[slot], buf_ref.at[slot], sems.at[slot]).wait()
    w = w_ref[...]
    for k in range(SC_COMBINE_SLOTS, TOP_K):
        ffn = ffn + _load_rows(buf_ref, R, lead=(slot, k - SC_COMBINE_SLOTS)) * w[:, k:k + 1]
    out_ref[...] = ffn


def _combine(h2, pos, top_w, ys, ws_gate, ws_up, ws_down):
    T, D = h2.shape
    R = COMBINE_ROWS
    F = SHARED_DIM
    bf16 = jnp.bfloat16
    const = lambda shape: pl.BlockSpec(shape, lambda i: (0,) * len(shape))
    return pl.pallas_call(
        _combine_kernel,
        grid=(T // R,),
        in_specs=[
            pl.BlockSpec((R * TOP_K,), lambda i: (i,), memory_space=pltpu.SMEM),
            pl.BlockSpec((R * TOP_K,), lambda i: (jnp.minimum(i + 1, T // R - 1),), memory_space=pltpu.SMEM),
            pl.BlockSpec((R, D), lambda i: (i, 0)),
            pl.BlockSpec((R, TOP_K), lambda i: (i, 0)),
            pl.BlockSpec(memory_space=pl.ANY),
            const((D, F)), const((D, F)), const((F, D)),
        ],
        out_specs=pl.BlockSpec((R, D), lambda i: (i, 0)),
        scratch_shapes=[pltpu.VMEM((2, TOP_K - SC_COMBINE_SLOTS, R * ROW_TILE, LANES), jnp.float32),
                        pltpu.SemaphoreType.DMA((2,))],
        out_shape=jax.ShapeDtypeStruct((T, D), jnp.float32),
        compiler_params=pltpu.CompilerParams(
            dimension_semantics=("arbitrary",), vmem_limit_bytes=VMEM_LIMIT_BYTES),
    )(pos, pos, h2, top_w, ys, ws_gate.astype(bf16), ws_up.astype(bf16), ws_down.astype(bf16))


def _moe(h2, h2_packed, w_router, router_bias, w_gate, w_up, w_down, ws_gate, ws_up, ws_down, ln_g, ln_b):
    T = h2.shape[0]
    E = N_EXPERTS
    BM = EXPERT_ROWS
    e_idx, top_w, rank, counts = _router(h2, w_router, router_bias)
    cnt = counts[:, 0].astype(jnp.int32)
    nblk = (cnt + BM - 1) // BM
    blk_end = jnp.cumsum(nblk)
    row_start = ((blk_end - nblk) * BM).astype(jnp.int32)
    n_blocks = T * TOP_K // BM + E
    n_act = blk_end[-1:].astype(jnp.int32)
    blk_ids = jnp.minimum(jnp.arange(n_blocks, dtype=jnp.int32), n_act[0] - 1)
    blk_e = jnp.minimum(jnp.sum(blk_end[None, :] <= blk_ids[:, None], axis=1), E - 1).astype(jnp.int32)
    pos_kt = _positions(e_idx, rank, row_start).reshape(TOP_K, T)
    pos = pos_kt.T.reshape(T * TOP_K)
    xs = _dispatch_sc(h2_packed, pos_kt, n_blocks * BM)
    ys = _experts(xs, blk_e, n_act, row_start, cnt, w_gate, w_up, w_down)
    gathered = _gather_rows_sc(ys, pos_kt[:SC_COMBINE_SLOTS])
    w_tk = top_w.T
    partial = _combine(h2, pos, w_tk, ys, ws_gate, ws_up, ws_down)
    return _finish(h2, partial, w_tk, gathered, ln_g, ln_b)


def kernel(x, ln_in_g, ln_in_b, w_in, w_out, rel_bias, attn_sinks, ln_mix_g, ln_mix_b, w_router,
           router_bias, w_gate, w_up, w_down, ws_gate, ws_up, ws_down, ln_ffn_g, ln_ffn_b):
    B, S, D = x.shape
    h, h_packed = _mixer(x, ln_in_g, ln_in_b, w_in[0], w_out[0], rel_bias, attn_sinks[0], ln_mix_g[0],
                         ln_mix_b[0])
    out = _moe(h, h_packed, w_router[0], router_bias[0], w_gate[0], w_up[0], w_down[0],
               ws_gate[0], ws_up[0], ws_down[0], ln_ffn_g[0], ln_ffn_b[0])
    return out.reshape(B, S, D)
```

```python
import functools
import math

import jax
import jax.numpy as jnp
from jax import lax
from jax.experimental import pallas as pl
from jax.experimental.pallas import tpu as pltpu
from jax.experimental.pallas import tpu_sc as plsc

D_MODEL = 1024
DEPTH = 1
RET_HEADS = 4
RET_QK_DIM = 64
RET_V_DIM = 128
RET_CHUNK = 128
RET_WIDTH = RET_HEADS * RET_V_DIM
ROPE_BASE = 10000.0
SWA_HEADS = 8
SWA_KV_HEADS = 2
SWA_GROUP = SWA_HEADS // SWA_KV_HEADS
SWA_HEAD_DIM = 64
SWA_WINDOW = 128
SWA_WIDTH = SWA_HEADS * SWA_HEAD_DIM
MIX_WIDTH = RET_WIDTH + SWA_WIDTH
RQK = RET_HEADS * RET_QK_DIM
SKV = SWA_KV_HEADS * SWA_HEAD_DIM
IN_SIZES = (RQK, RQK, RET_WIDTH, RET_WIDTH, SWA_WIDTH, SKV, SKV)
IN_OFFS = tuple(sum(IN_SIZES[:i]) for i in range(len(IN_SIZES)))
IN_WIDTH = sum(IN_SIZES)
REL_BUCKETS = 32
REL_MAX_DIST = 128
N_EXPERTS = 256
TOP_K = 8
N_GROUPS = 8
GROUP_SIZE = N_EXPERTS // N_GROUPS
TOPK_GROUPS = 4
EXPERT_DIM = 256
SHARED_DIM = 256
ROUTED_SCALE = 2.5
LN_EPS = 1e-5
GN_EPS = 1e-6
DEEPNORM_ALPHA = (2 * DEPTH) ** 0.25
MASK_VALUE = -1e30

VMEM_LIMIT_BYTES = 56 * 1024 * 1024

MIX_ROWS = 256
ROUTE_ROWS = 256
DISPATCH_ROWS = 256
EXPERT_ROWS = 256
COMBINE_ROWS = 256
POSITION_ROWS = 64
FINISH_ROWS = 512


def _layer_norm(x, g, b):
    mu = jnp.mean(x, axis=-1, keepdims=True)
    xc = x - mu
    var = jnp.mean(xc * xc, axis=-1, keepdims=True)
    return xc * lax.rsqrt(var + LN_EPS) * g + b


def _dot(a, b):
    return jnp.dot(a, b, preferred_element_type=jnp.float32)


def _dot_nt(a, b):
    return lax.dot_general(a, b, (((1,), (1,)), ((), ())), preferred_element_type=jnp.float32)


def _dot_tn(a, b):
    return lax.dot_general(a, b, (((0,), (0,)), ((), ())), preferred_element_type=jnp.float32)


def _silu(x):
    return x * (1.0 / (1.0 + jnp.exp(-x)))


LANES = 128
ROW_TILE = D_MODEL // LANES


def _load_rows(ref, n_rows, lead=()):
    return jnp.concatenate([ref[lead + (pl.ds(s, n_rows, stride=ROW_TILE), slice(None))]
                            for s in range(ROW_TILE)], axis=1)


def _store_rows(ref, val, lead=(), first_row=0):
    n_rows = val.shape[0]
    for s in range(ROW_TILE):
        dst = pl.ds(first_row * ROW_TILE + s, n_rows, stride=ROW_TILE)
        ref[lead + (dst, slice(None))] = val[:, s * LANES:(s + 1) * LANES]


def _row_tile(r):
    return pl.ds(pl.multiple_of(r * ROW_TILE, ROW_TILE), ROW_TILE)


def _pack_bf16_pairs(x):
    m = x.shape[1] // 2
    bits = lax.bitcast_convert_type(x.astype(jnp.bfloat16).astype(jnp.float32), jnp.uint32)
    return (bits[:, :m] >> 16) | (bits[:, m:] & jnp.uint32(0xFFFF0000))


def _unpack_bf16_pairs(p):
    lo = lax.bitcast_convert_type(p << 16, jnp.float32)
    hi = lax.bitcast_convert_type(p & jnp.uint32(0xFFFF0000), jnp.float32)
    return jnp.concatenate([lo, hi], axis=1).astype(jnp.bfloat16)


def _swap_halves(x):
    n = x.shape[-1]
    half = RET_QK_DIM // 2
    lane = lax.broadcasted_iota(jnp.int32, x.shape, 1)
    from_right = pltpu.roll(x, n - half, axis=1)
    from_left = pltpu.roll(x, half, axis=1)
    return jnp.where((lane % RET_QK_DIM) < half, from_right, from_left)


def _mixer_kernel(rel_bias_ref, x_ref, g_in_ref, b_in_ref, w_in_ref, w_out_ref, rot_ref, decay_ref,
                  zeta_ref, xi_ref, cdecay_ref, bucket_ref, sink_ref, g_mix_ref, b_mix_ref,
                  h2_ref, h2p_ref, state_ref, kprev_ref, vprev_ref, bias_ref):
    b_id = pl.program_id(0)
    c_id = pl.program_id(1)
    W = SWA_WINDOW

    @pl.when((b_id == 0) & (c_id == 0))
    def _build_bias():
        bucket = bucket_ref[...]
        for h in range(SWA_HEADS):
            acc = jnp.full((2 * W, W), MASK_VALUE, jnp.float32)
            for b in range(REL_BUCKETS):
                acc = jnp.where(bucket == b, rel_bias_ref[b, h], acc)
            kh, g = divmod(h, SWA_GROUP)
            bias_ref[kh, :, g * W:(g + 1) * W] = acc

    @pl.when(c_id == 0)
    def _reset():
        state_ref[...] = jnp.zeros_like(state_ref)
        kprev_ref[...] = jnp.zeros_like(kprev_ref)
        vprev_ref[...] = jnp.zeros_like(vprev_ref)

    h = _layer_norm(x_ref[...], g_in_ref[...], b_in_ref[...])
    proj = _dot(h.astype(jnp.bfloat16), w_in_ref[...])

    o_q, o_k, o_v, o_g, o_sq, o_sk, o_sv = IN_OFFS
    cos_t = rot_ref[:, :RQK]
    sin_t = rot_ref[:, RQK:]
    q_all = proj[:, o_q:o_q + RQK]
    k_all = proj[:, o_k:o_k + RQK]
    q_rot = q_all * cos_t + _swap_halves(q_all) * sin_t
    k_rot = (k_all * cos_t + _swap_halves(k_all) * sin_t) * (RET_QK_DIM ** -0.5)

    n_sub = x_ref.shape[0] // RET_CHUNK
    states = [state_ref[hh] for hh in range(RET_HEADS)]
    k_prev = kprev_ref[...]
    v_prev = vprev_ref[...]
    bf16 = jnp.bfloat16
    heads = range(RET_HEADS)
    ret_pieces, swa_pieces = [], []
    swa_jobs = []
    for s in range(n_sub):
        rows = slice(s * RET_CHUNK, (s + 1) * RET_CHUNK)
        k_cur = proj[rows, o_sk:o_sk + SKV].astype(bf16)
        v_cur = proj[rows, o_sv:o_sv + SKV].astype(bf16)
        for kh in range(SWA_KV_HEADS):
            kv = slice(kh * SWA_HEAD_DIM, (kh + 1) * SWA_HEAD_DIM)
            q4 = jnp.concatenate(
                [proj[rows, o_sq + (kh * SWA_GROUP + g) * SWA_HEAD_DIM:
                      o_sq + (kh * SWA_GROUP + g + 1) * SWA_HEAD_DIM] for g in range(SWA_GROUP)],
                axis=0) * (SWA_HEAD_DIM ** -0.5)
            kcat = jnp.concatenate([k_prev[:, kv], k_cur[:, kv]], axis=0)
            vcat = jnp.concatenate([v_prev[:, kv], v_cur[:, kv]], axis=0)
            swa_jobs.append((s, kh, q4.astype(bf16), kcat, vcat))
        k_prev, v_prev = k_cur, v_cur
    logits_all = [_dot_nt(kcat, q4) + bias_ref[kh] for (s, kh, q4, kcat, vcat) in swa_jobs]
    probs_all = []
    for (s, kh, q4, kcat, vcat), logits in zip(swa_jobs, logits_all):
        if s == 0:
            key = lax.broadcasted_iota(jnp.int32, logits.shape, 0)
            logits = logits + jnp.where((key < W) & (c_id == 0), MASK_VALUE, 0.0)
        sink = sink_ref[kh]
        m = jnp.maximum(jnp.max(logits, axis=0, keepdims=True), sink)
        p = jnp.exp(logits - m)
        den = jnp.sum(p, axis=0, keepdims=True) + jnp.exp(sink - m)
        probs_all.append((p / den).astype(bf16))
    o4_all = [_dot_tn(job[4], probs) for job, probs in zip(swa_jobs, probs_all)]
    for s in range(n_sub):
        swa_pieces.append([o4[:, g * W:(g + 1) * W].T.astype(bf16)
                           for job, o4 in zip(swa_jobs, o4_all) if job[0] == s for g in range(SWA_GROUP)])
    for s in range(n_sub):
        rows = slice(s * RET_CHUNK, (s + 1) * RET_CHUNK)
        qk = [slice(hh * RET_QK_DIM, (hh + 1) * RET_QK_DIM) for hh in heads]
        q = [q_rot[rows, qk[hh]].astype(bf16) for hh in heads]
        k32 = [k_rot[rows, qk[hh]] for hh in heads]
        v = [proj[rows, o_v + hh * RET_V_DIM:o_v + (hh + 1) * RET_V_DIM].astype(bf16) for hh in heads]
        scores = [_dot_nt(q[hh], k32[hh].astype(bf16)) * decay_ref[hh] for hh in heads]
        inter = [_dot(q[hh], states[hh].astype(bf16)) * xi_ref[hh] for hh in heads]
        kv_new = [_dot_tn((k32[hh] * zeta_ref[hh]).astype(bf16), v[hh]) for hh in heads]
        intra = [_dot(scores[hh].astype(bf16), v[hh]) for hh in heads]
        states = [states[hh] * cdecay_ref[hh] + kv_new[hh] for hh in heads]
        pieces = []
        for hh in heads:
            ret = intra[hh] + inter[hh]
            mu = jnp.mean(ret, axis=-1, keepdims=True)
            rc = ret - mu
            var = jnp.mean(rc * rc, axis=-1, keepdims=True)
            normed = rc * lax.rsqrt(var + GN_EPS)
            gate = proj[rows, o_g + hh * RET_V_DIM:o_g + (hh + 1) * RET_V_DIM]
            pieces.append((_silu(gate) * normed).astype(bf16))
        ret_pieces.append(pieces)
    cat_rows = [jnp.concatenate(ret_pieces[s] + swa_pieces[s], axis=1) for s in range(n_sub)]
    for hh in range(RET_HEADS):
        state_ref[hh] = states[hh]
    kprev_ref[...] = k_prev
    vprev_ref[...] = v_prev

    mix = _dot(jnp.concatenate(cat_rows, axis=0), w_out_ref[...])
    h2 = _layer_norm(DEEPNORM_ALPHA * h + mix, g_mix_ref[...], b_mix_ref[...])
    h2_ref[...] = h2
    h2p_ref[...] = _pack_bf16_pairs(h2)


def _t5_bucket(dist):
    n = jnp.maximum(dist, 0)
    max_exact = REL_BUCKETS // 2
    ratio = jnp.log(jnp.maximum(n, 1).astype(jnp.float32) / max_exact) / math.log(REL_MAX_DIST / max_exact)
    large = jnp.minimum(max_exact + (ratio * (REL_BUCKETS - max_exact)).astype(jnp.int32), REL_BUCKETS - 1)
    return jnp.where(n < max_exact, n, large)


def _mixer(x, ln_in_g, ln_in_b, w_in, w_out, rel_bias, sinks, ln_mix_g, ln_mix_b):
    B, S, D = x.shape
    R = MIX_ROWS
    C = RET_CHUNK
    W = SWA_WINDOW
    f32 = jnp.float32
    half = RET_QK_DIM // 2
    inv = ROPE_BASE ** (-jnp.arange(half, dtype=f32) / half)
    ang = jnp.arange(S, dtype=f32)[:, None] * inv[None, :]
    cos, sin = jnp.cos(ang), jnp.sin(ang)
    cos_t = jnp.tile(jnp.concatenate([cos, cos], axis=-1), (1, RET_HEADS))
    sin_t = jnp.tile(jnp.concatenate([-sin, sin], axis=-1), (1, RET_HEADS))
    rot = jnp.concatenate([cos_t, sin_t], axis=-1)
    log_gamma = jnp.log(1.0 - 2.0 ** (-5.0 - jnp.arange(RET_HEADS, dtype=f32)))
    idx = jnp.arange(C, dtype=f32)
    diff = idx[:, None] - idx[None, :]
    decay = jnp.where(diff[None] >= 0, jnp.exp(jnp.maximum(diff, 0.0)[None] * log_gamma[:, None, None]), 0.0)
    zeta = jnp.exp((C - 1.0 - idx)[None, :] * log_gamma[:, None])
    xi = jnp.exp((idx + 1.0)[None, :] * log_gamma[:, None])
    zeta_b = jnp.broadcast_to(zeta[:, :, None], (RET_HEADS, C, RET_QK_DIM))
    xi_b = jnp.broadcast_to(xi[:, :, None], (RET_HEADS, C, RET_V_DIM))
    cdecay = jnp.broadcast_to(jnp.exp(C * log_gamma)[:, None, None], (RET_HEADS, RET_QK_DIM, RET_V_DIM))
    i = jnp.arange(W)
    j = jnp.arange(2 * W)
    dist = i[:, None] + W - j[None, :]
    bucket = jnp.where((dist >= 0) & (dist < W), _t5_bucket(dist), -1).astype(jnp.int32).T
    sink_row = jnp.repeat(sinks.astype(f32), W).reshape(SWA_KV_HEADS, 1, SWA_GROUP * W)

    const = lambda shape: pl.BlockSpec(shape, lambda b, c, *_: (0,) * len(shape))
    grid_spec = pltpu.PrefetchScalarGridSpec(
        num_scalar_prefetch=1,
        grid=(B, S // R),
        in_specs=[
            pl.BlockSpec((None, R, D), lambda b, c, *_: (b, c, 0)),
            const((1, D)), const((1, D)),
            const((D, IN_WIDTH)), const((MIX_WIDTH, D)),
            pl.BlockSpec((R, 2 * RQK), lambda b, c, *_: (c, 0)),
            const((RET_HEADS, C, C)), const((RET_HEADS, C, RET_QK_DIM)), const((RET_HEADS, C, RET_V_DIM)),
            const((RET_HEADS, RET_QK_DIM, RET_V_DIM)),
            const((2 * W, W)), const((SWA_KV_HEADS, 1, SWA_GROUP * W)),
            const((1, D)), const((1, D)),
        ],
        out_specs=[pl.BlockSpec((R, D), lambda b, c, *_: (b * (S // R) + c, 0)),
                   pl.BlockSpec((R, D // 2), lambda b, c, *_: (b * (S // R) + c, 0))],
        scratch_shapes=[
            pltpu.VMEM((RET_HEADS, RET_QK_DIM, RET_V_DIM), f32),
            pltpu.VMEM((W, SKV), jnp.bfloat16),
            pltpu.VMEM((W, SKV), jnp.bfloat16),
            pltpu.VMEM((SWA_KV_HEADS, 2 * W, SWA_GROUP * W), f32),
        ],
    )
    return pl.pallas_call(
        _mixer_kernel,
        grid_spec=grid_spec,
        out_shape=[jax.ShapeDtypeStruct((B * S, D), f32), jax.ShapeDtypeStruct((B * S, D // 2), jnp.uint32)],
        compiler_params=pltpu.CompilerParams(
            dimension_semantics=("arbitrary", "arbitrary"), vmem_limit_bytes=VMEM_LIMIT_BYTES),
    )(rel_bias.astype(f32), x, ln_in_g.reshape(1, D), ln_in_b.reshape(1, D),
      w_in.astype(jnp.bfloat16), w_out.astype(jnp.bfloat16), rot, decay, zeta_b, xi_b, cdecay,
      bucket, sink_row, ln_mix_g.reshape(1, D), ln_mix_b.reshape(1, D))


def _router_kernel(h_ref, wr_ref, rb_ref, e_ref, w_ref, rk_ref, cnt_ref, run_ref):
    f32 = jnp.float32
    R = h_ref.shape[0]
    E = N_EXPERTS
    neg = -jnp.inf

    @pl.when(pl.program_id(0) == 0)
    def _init():
        run_ref[...] = jnp.zeros_like(run_ref)

    logits = _dot_nt(wr_ref[...], h_ref[...].astype(jnp.bfloat16))
    scores = 1.0 / (1.0 + jnp.exp(-logits))
    choice = scores + rb_ref[...]
    eid = lax.broadcasted_iota(jnp.int32, (E, R), 0)

    def first_argmax(vals, ids, none):
        m = jnp.max(vals, axis=0, keepdims=True)
        idx = jnp.min(jnp.where(vals == m, ids, none), axis=0, keepdims=True)
        return m, idx

    gid = lax.broadcasted_iota(jnp.int32, (GROUP_SIZE, R), 0)
    groups, gscore = [], []
    for g in range(N_GROUPS):
        vals = choice[g * GROUP_SIZE:(g + 1) * GROUP_SIZE]
        m1, i1 = first_argmax(vals, gid, GROUP_SIZE)
        m2 = jnp.max(jnp.where(gid == i1, neg, vals), axis=0, keepdims=True)
        groups.append(vals)
        gscore.append(m1 + m2)
    kept = []
    for g in range(N_GROUPS):
        beaten = jnp.zeros((1, R), f32)
        for g2 in range(N_GROUPS):
            if g2 == g:
                continue
            ahead = (gscore[g2] > gscore[g]) | (gscore[g2] == gscore[g]) if g2 < g else gscore[g2] > gscore[g]
            beaten = beaten + jnp.where(ahead, 1.0, 0.0)
        kept.append(jnp.where(beaten < TOPK_GROUPS, groups[g], neg))
    masked = jnp.concatenate(kept, axis=0)

    idxs, wts = [], []
    picked = jnp.zeros((E, R), f32)
    for _ in range(TOP_K):
        _, idx = first_argmax(masked, eid, E)
        hit = eid == idx
        idxs.append(idx)
        wts.append(jnp.sum(jnp.where(hit, scores, 0.0), axis=0, keepdims=True))
        masked = jnp.where(hit, neg, masked)
        picked = jnp.where(hit, 1.0, picked)
    wsum = wts[0]
    for k in range(1, TOP_K):
        wsum = wsum + wts[k]

    row = lax.broadcasted_iota(jnp.int32, (R, R), 0)
    col = lax.broadcasted_iota(jnp.int32, (R, R), 1)
    earlier = jnp.where(row < col, 1.0, 0.0).astype(jnp.bfloat16)
    picked_bf = picked.astype(jnp.bfloat16)
    run = run_ref[...]
    before = _dot(picked_bf, earlier) + jnp.concatenate([run] * (R // LANES), axis=1)
    sub_k = lax.broadcasted_iota(jnp.int32, (TOP_K, R), 0)
    e_out = jnp.zeros((TOP_K, R), jnp.int32)
    w_out = jnp.zeros((TOP_K, R), f32)
    rk_out = jnp.zeros((TOP_K, R), jnp.int32)
    for k in range(TOP_K):
        rank_k = jnp.sum(jnp.where(eid == idxs[k], before, 0.0), axis=0, keepdims=True)
        e_out = jnp.where(sub_k == k, idxs[k], e_out)
        w_out = jnp.where(sub_k == k, wts[k] / wsum * ROUTED_SCALE, w_out)
        rk_out = jnp.where(sub_k == k, rank_k.astype(jnp.int32), rk_out)
    e_ref[...] = e_out
    w_ref[...] = w_out
    rk_ref[...] = rk_out
    run_ref[...] = run + _dot(picked_bf, jnp.ones((R, LANES), jnp.bfloat16))
    cnt_ref[...] = run_ref[...]


def _router(h2, w_router, router_bias):
    T, D = h2.shape
    R = ROUTE_ROWS
    E = N_EXPERTS
    return pl.pallas_call(
        _router_kernel,
        grid=(T // R,),
        in_specs=[
            pl.BlockSpec((R, D), lambda i: (i, 0)),
            pl.BlockSpec((E, D), lambda i: (0, 0)),
            pl.BlockSpec((E, R), lambda i: (0, 0)),
        ],
        out_specs=[
            pl.BlockSpec((TOP_K, R), lambda i: (0, i)),
            pl.BlockSpec((TOP_K, R), lambda i: (0, i)),
            pl.BlockSpec((TOP_K, R), lambda i: (0, i)),
            pl.BlockSpec((E, LANES), lambda i: (0, 0)),
        ],
        out_shape=[
            jax.ShapeDtypeStruct((TOP_K, T), jnp.int32),
            jax.ShapeDtypeStruct((TOP_K, T), jnp.float32),
            jax.ShapeDtypeStruct((TOP_K, T), jnp.int32),
            jax.ShapeDtypeStruct((E, LANES), jnp.float32),
        ],
        scratch_shapes=[pltpu.VMEM((E, LANES), jnp.float32)],
        compiler_params=pltpu.CompilerParams(
            dimension_semantics=("arbitrary",), vmem_limit_bytes=VMEM_LIMIT_BYTES),
    )(h2, w_router.T.astype(jnp.bfloat16), jnp.broadcast_to(router_bias.astype(jnp.float32)[:, None], (E, R)))


def _positions_kernel(row_start_ref, e_ref, rk_ref, pos_ref):
    e = e_ref[...]
    rk = rk_ref[...]

    def per_expert(i, pos):
        return jnp.where(e == i, rk + row_start_ref[i], pos)

    pos_ref[...] = lax.fori_loop(0, N_EXPERTS, per_expert, jnp.zeros_like(rk))


def _positions(e_idx, rank, row_start):
    n = e_idx.size
    shape = (n // LANES, LANES)
    block = pl.BlockSpec((POSITION_ROWS, LANES), lambda i, *_: (i, 0))
    grid_spec = pltpu.PrefetchScalarGridSpec(
        num_scalar_prefetch=1,
        grid=(shape[0] // POSITION_ROWS,),
        in_specs=[block, block],
        out_specs=block,
    )
    pos = pl.pallas_call(
        _positions_kernel,
        grid_spec=grid_spec,
        out_shape=jax.ShapeDtypeStruct(shape, jnp.int32),
    )(row_start, e_idx.reshape(shape), rank.reshape(shape))
    return pos.reshape(n)


def _dispatch_kernel(row_start_ref, cnt_ref, n_act_ref, pos_ref, h_ref, xs_ref, zero_ref, sem, zsem,
                     *, n_tokens):
    R = h_ref.shape[0] // ROW_TILE
    BM = EXPERT_ROWS
    n_blocks = xs_ref.shape[0] // (BM * ROW_TILE)
    n_pad_units = n_blocks - n_tokens * TOP_K // BM

    @pl.when(pl.program_id(0) == 0)
    def _zero_padding():
        zero_ref[...] = jnp.zeros_like(zero_ref)

        def expert_tail(e, carry):
            n_tail = pl.multiple_of(((BM - cnt_ref[e] % BM) % BM) * ROW_TILE, ROW_TILE)

            @pl.when(n_tail > 0)
            def _():
                dst = pl.multiple_of((row_start_ref[e] + cnt_ref[e]) * ROW_TILE, ROW_TILE)
                pltpu.make_async_copy(zero_ref.at[pl.ds(0, n_tail)], xs_ref.at[pl.ds(dst, n_tail)], zsem).start()
            return carry

        lax.fori_loop(0, N_EXPERTS, expert_tail, 0)

        def idle_block(i, carry):
            dst = pl.multiple_of(i * (BM * ROW_TILE), BM * ROW_TILE)
            pltpu.make_async_copy(zero_ref, xs_ref.at[pl.ds(dst, BM * ROW_TILE)], zsem).start()
            return carry

        lax.fori_loop(n_act_ref[0], n_blocks, idle_block, 0)

        def drain(i, carry):
            pltpu.make_async_copy(zero_ref, xs_ref.at[pl.ds(0, BM * ROW_TILE)], zsem).wait()
            return carry

        lax.fori_loop(0, n_pad_units, drain, 0)

    def issue(t, carry):
        src = h_ref.at[_row_tile(t)]
        for k in range(TOP_K):
            dest = pos_ref[t * TOP_K + k]
            pltpu.make_async_copy(src, xs_ref.at[_row_tile(dest)], sem).start(priority=k % 2)
        return carry

    lax.fori_loop(0, R, issue, 0)
    n = R * TOP_K * ROW_TILE
    pltpu.make_async_copy(xs_ref.at[pl.ds(0, n)], xs_ref.at[pl.ds(0, n)], sem).wait()


def _dispatch(h2, pos, row_start, cnt, n_act, n_rows):
    T = h2.shape[0] // ROW_TILE
    R = DISPATCH_ROWS
    grid_spec = pltpu.PrefetchScalarGridSpec(
        num_scalar_prefetch=3,
        grid=(T // R,),
        in_specs=[
            pl.BlockSpec((R * TOP_K,), lambda i, *_: (i,), memory_space=pltpu.SMEM),
            pl.BlockSpec((R * ROW_TILE, LANES), lambda i, *_: (i, 0)),
        ],
        out_specs=pl.BlockSpec(memory_space=pl.ANY),
        scratch_shapes=[pltpu.VMEM((EXPERT_ROWS * ROW_TILE, LANES), jnp.float32),
                        pltpu.SemaphoreType.DMA(()), pltpu.SemaphoreType.DMA(())],
    )
    return pl.pallas_call(
        functools.partial(_dispatch_kernel, n_tokens=T),
        grid_spec=grid_spec,
        out_shape=jax.ShapeDtypeStruct((n_rows * ROW_TILE, LANES), jnp.float32),
        compiler_params=pltpu.CompilerParams(
            dimension_semantics=("arbitrary",), vmem_limit_bytes=VMEM_LIMIT_BYTES),
    )(row_start, cnt, n_act, pos, h2)


SC_WINDOW = 32


def _dispatch_sc(h2, pos_kt, n_rows):
    T, D = h2.shape
    W = SC_WINDOW
    idx = _window_indices(pos_kt, W)
    idx_rows = TOP_K * W // LANES
    mesh = plsc.VectorSubcoreMesh(core_axis_name="core", subcore_axis_name="subcore")

    @pl.kernel(out_type=jax.ShapeDtypeStruct((n_rows, D), h2.dtype), mesh=mesh, scratch_types=[])
    def scatter_rows(x_hbm, i_hbm, o_hbm):
        def body(x_vmem, i_vmem):
            for k in range(TOP_K):
                r, q = divmod(k * W, LANES)
                pltpu.sync_copy(x_vmem, o_hbm.at[i_vmem.at[r, pl.ds(q, W)]])

        pltpu.emit_pipeline(
            body,
            grid=(T // W,),
            in_specs=[pl.BlockSpec((W, D), lambda i: (i, 0)),
                      pl.BlockSpec((idx_rows, LANES), lambda i: (i, 0))],
            out_specs=[],
            core_axis_name=("core", "subcore"),
            dimension_semantics=(pltpu.PARALLEL,),
        )(x_hbm, i_hbm)

    return scatter_rows(h2, idx)


def _window_indices(pos_kt, window):
    K, T = pos_kt.shape
    return pos_kt.reshape(K, T // window, window).transpose(1, 0, 2).reshape(T * K // LANES, LANES)


def _gather_rows_sc(ys, pos_kt):
    n = pos_kt.size
    W = SC_WINDOW
    ys3 = ys.reshape(ys.shape[0] // ROW_TILE, ROW_TILE, LANES)
    idx = jnp.pad(pos_kt.reshape(n // W, W), ((0, 0), (0, LANES - W)))
    mesh = plsc.VectorSubcoreMesh(core_axis_name="core", subcore_axis_name="subcore")

    @pl.kernel(out_type=jax.ShapeDtypeStruct((n, ROW_TILE, LANES), ys.dtype), mesh=mesh, scratch_types=[])
    def gather_rows(y_hbm, i_hbm, o_hbm):
        def body(i_vmem, o_vmem):
            pltpu.sync_copy(y_hbm.at[i_vmem.at[0, pl.ds(0, W)]], o_vmem)

        pltpu.emit_pipeline(
            body,
            grid=(n // W,),
            in_specs=[pl.BlockSpec((1, LANES), lambda i: (i, 0))],
            out_specs=[pl.BlockSpec((W, ROW_TILE, LANES), lambda i: (i, 0, 0))],
            core_axis_name=("core", "subcore"),
            dimension_semantics=(pltpu.PARALLEL,),
        )(i_hbm, o_hbm)

    return gather_rows(ys3, idx).reshape(n * ROW_TILE, LANES)


EXPERT_SUBBLOCKS = 2
X_SLOTS = 4
Y_SLOTS = 3


def _experts_kernel(blk_e_ref, first_ref, slot_ref, next_e_ref, valid_ref, n_act_ref, xs_hbm, wg_hbm,
                    wu_hbm, wd_hbm, ys_hbm, x_buf, y_buf, wg_buf, wu_buf, wd_buf, wg_bf, wu_bf, wd_bf,
                    sems, x_sems, y_sems):
    i = pl.program_id(0)
    n_act = n_act_ref[0]
    bf16 = jnp.bfloat16
    blk = EXPERT_ROWS * ROW_TILE

    def x_copy(j):
        n = pl.multiple_of((valid_ref[j] + 7) // 8 * 8, 8)
        src = xs_hbm.at[pl.ds(pl.multiple_of(j * EXPERT_ROWS, EXPERT_ROWS), n)]
        return pltpu.make_async_copy(src, x_buf.at[j % X_SLOTS, pl.ds(0, n)], x_sems.at[j % X_SLOTS])

    def y_copy(j):
        n = pl.multiple_of(valid_ref[j] * ROW_TILE, ROW_TILE)
        dst = ys_hbm.at[pl.ds(pl.multiple_of(j * blk, blk), n)]
        return pltpu.make_async_copy(y_buf.at[j % Y_SLOTS, pl.ds(0, n)], dst, y_sems.at[j % Y_SLOTS])

    @pl.when(i == 0)
    def _prime():
        x_buf[...] = jnp.zeros_like(x_buf)
        for j in range(X_SLOTS - 1):
            @pl.when(j < n_act)
            def _():
                x_copy(j).start()

    @pl.when(i + (X_SLOTS - 1) < n_act)
    def _prefetch():
        x_copy(i + (X_SLOTS - 1)).start()

    def weight_copies(e, slot):
        return (pltpu.make_async_copy(wg_hbm.at[e], wg_buf.at[slot], sems.at[slot]),
                pltpu.make_async_copy(wu_hbm.at[e], wu_buf.at[slot], sems.at[slot]),
                pltpu.make_async_copy(wd_hbm.at[e], wd_buf.at[slot], sems.at[slot]))

    @pl.when((i < n_act_ref[0]) & (first_ref[i] == 1))
    def _new_expert():
        slot = slot_ref[i]

        @pl.when(i == 0)
        def _():
            for c in weight_copies(blk_e_ref[0], 0):
                c.start()

        for c in weight_copies(blk_e_ref[i], slot):
            c.wait()

        @pl.when(next_e_ref[i] >= 0)
        def _():
            for c in weight_copies(next_e_ref[i], 1 - slot):
                c.start()

        wg_bf[...] = wg_buf[slot].astype(bf16)
        wu_bf[...] = wu_buf[slot].astype(bf16)
        wd_bf[...] = wd_buf[slot].astype(bf16)

    @pl.when(i < n_act)
    def _compute():
        x_copy(i).wait()

        @pl.when(i >= Y_SLOTS)
        def _():
            y_copy(i - Y_SLOTS).wait()

        sub = EXPERT_ROWS // EXPERT_SUBBLOCKS
        xs_ = [_unpack_bf16_pairs(x_buf[i % X_SLOTS, pl.ds(part * sub, sub), :]) for part in range(EXPERT_SUBBLOCKS)]
        gs = [_dot(x, wg_bf[...]) for x in xs_]
        us = [_dot(x, wu_bf[...]) for x in xs_]
        acts = [(_silu(g) * u).astype(bf16) for g, u in zip(gs, us)]
        for part in range(EXPERT_SUBBLOCKS):
            _store_rows(y_buf, _dot(acts[part], wd_bf[...]), lead=(i % Y_SLOTS,), first_row=part * sub)
        y_copy(i).start()

    @pl.when(i == n_act - 1)
    def _drain():
        for d in range(Y_SLOTS):
            @pl.when(i - d >= 0)
            def _():
                y_copy(i - d).wait()


def _experts(xs, blk_e, n_act, row_start, cnt, w_gate, w_up, w_down):
    D = D_MODEL
    BM = EXPERT_ROWS
    F = EXPERT_DIM
    n_blocks = xs.shape[0] // BM
    blk_in_expert = jnp.arange(n_blocks, dtype=jnp.int32) - row_start[blk_e] // BM
    valid = jnp.clip(cnt[blk_e] - blk_in_expert * BM, 0, BM).astype(jnp.int32)
    ids = jnp.arange(n_blocks, dtype=jnp.int32)
    active = ids < n_act[0]
    first = active & ((ids == 0) | (blk_e != jnp.roll(blk_e, 1)))
    slot = ((jnp.cumsum(first.astype(jnp.int32)) - 1) % 2).astype(jnp.int32)
    first_pos = jnp.where(first, ids, n_blocks)
    later_first = lax.cummin(jnp.concatenate([first_pos[1:], jnp.full((1,), n_blocks, jnp.int32)]), reverse=True)
    next_e = jnp.where(later_first < n_blocks, blk_e[jnp.minimum(later_first, n_blocks - 1)], -1).astype(jnp.int32)

    grid_spec = pltpu.PrefetchScalarGridSpec(
        num_scalar_prefetch=6,
        grid=(n_blocks,),
        in_specs=[pl.BlockSpec(memory_space=pl.ANY)] * 4,
        out_specs=pl.BlockSpec(memory_space=pl.ANY),
        scratch_shapes=[
            pltpu.VMEM((X_SLOTS, BM, D // 2), jnp.uint32),
            pltpu.VMEM((Y_SLOTS, BM * ROW_TILE, LANES), jnp.float32),
            pltpu.VMEM((2, D, F), jnp.float32), pltpu.VMEM((2, D, F), jnp.float32),
            pltpu.VMEM((2, F, D), jnp.float32),
            pltpu.VMEM((D, F), jnp.bfloat16), pltpu.VMEM((D, F), jnp.bfloat16),
            pltpu.VMEM((F, D), jnp.bfloat16),
            pltpu.SemaphoreType.DMA((2,)), pltpu.SemaphoreType.DMA((X_SLOTS,)),
            pltpu.SemaphoreType.DMA((Y_SLOTS,)),
        ],
    )
    return pl.pallas_call(
        _experts_kernel,
        grid_spec=grid_spec,
        out_shape=jax.ShapeDtypeStruct((xs.shape[0] * ROW_TILE, LANES), jnp.float32),
        compiler_params=pltpu.CompilerParams(
            dimension_semantics=("arbitrary",), vmem_limit_bytes=VMEM_LIMIT_BYTES),
    )(blk_e, first.astype(jnp.int32), slot, next_e, valid, n_act, xs, w_gate, w_up, w_down)


def _finish_kernel(h_ref, part_ref, w_ref, *rest):
    slabs = rest[:SC_COMBINE_SLOTS]
    g_ref, b_ref, out_ref = rest[SC_COMBINE_SLOTS:]
    R = h_ref.shape[0]
    w = w_ref[...]
    ffn = part_ref[...]
    for k in range(SC_COMBINE_SLOTS):
        ffn = ffn + _load_rows(slabs[k], R) * w[:, k:k + 1]
    out_ref[...] = _layer_norm(DEEPNORM_ALPHA * h_ref[...] + ffn, g_ref[...], b_ref[...])


def _finish(h2, partial, top_w, gathered, ln_g, ln_b):
    T, D = h2.shape
    R = FINISH_ROWS
    rows = pl.BlockSpec((R, D), lambda i: (i, 0))
    vec = pl.BlockSpec((1, D), lambda i: (0, 0))
    slab = lambda k: pl.BlockSpec((R * ROW_TILE, LANES), lambda i: (k * (T // R) + i, 0))
    return pl.pallas_call(
        _finish_kernel,
        grid=(T // R,),
        in_specs=[rows, rows, pl.BlockSpec((R, TOP_K), lambda i: (i, 0))]
        + [slab(k) for k in range(SC_COMBINE_SLOTS)] + [vec, vec],
        out_specs=rows,
        out_shape=jax.ShapeDtypeStruct((T, D), jnp.float32),
        compiler_params=pltpu.CompilerParams(
            dimension_semantics=("arbitrary",), vmem_limit_bytes=VMEM_LIMIT_BYTES),
    )(h2, partial, top_w, *([gathered] * SC_COMBINE_SLOTS), ln_g.reshape(1, D), ln_b.reshape(1, D))


SC_COMBINE_SLOTS = 5


def _combine_kernel(pos_ref, pos_next_ref, h_ref, w_ref, ys_ref, wsg_ref, wsu_ref, wsd_ref,
                    out_ref, buf_ref, sems):
    R = h_ref.shape[0]
    i = pl.program_id(0)
    slot = i % 2

    def gather(p_ref, s):
        def issue(t, carry):
            for k in range(SC_COMBINE_SLOTS, TOP_K):
                src = p_ref[t * TOP_K + k]
                pltpu.make_async_copy(ys_ref.at[_row_tile(src)], buf_ref.at[s, k - SC_COMBINE_SLOTS, _row_tile(t)],
                                      sems.at[s]).start(priority=k % 2)
            return carry

        lax.fori_loop(0, R, issue, 0)

    @pl.when(i == 0)
    def _():
        gather(pos_ref, 0)

    @pl.when(i + 1 < pl.num_programs(0))
    def _():
        gather(pos_next_ref, 1 - slot)

    h = h_ref[...]
    hb = h.astype(jnp.bfloat16)
    act = (_silu(_dot(hb, wsg_ref[...])) * _dot(hb, wsu_ref[...])).astype(jnp.bfloat16)
    ffn = _dot(act, wsd_ref[...])
    pltpu.make_async_copy(buf_ref.at[slot], buf_ref.at[slot], sems.at[slot]).wait()
    w = w_ref[...]
    for k in range(SC_COMBINE_SLOTS, TOP_K):
        ffn = ffn + _load_rows(buf_ref, R, lead=(slot, k - SC_COMBINE_SLOTS)) * w[:, k:k + 1]
    out_ref[...] = ffn


def _combine(h2, pos, top_w, ys, ws_gate, ws_up, ws_down):
    T, D = h2.shape
    R = COMBINE_ROWS
    F = SHARED_DIM
    bf16 = jnp.bfloat16
    const = lambda shape: pl.BlockSpec(shape, lambda i: (0,) * len(shape))
    return pl.pallas_call(
        _combine_kernel,
        grid=(T // R,),
        in_specs=[
            pl.BlockSpec((R * TOP_K,), lambda i: (i,), memory_space=pltpu.SMEM),
            pl.BlockSpec((R * TOP_K,), lambda i: (jnp.minimum(i + 1, T // R - 1),), memory_space=pltpu.SMEM),
            pl.BlockSpec((R, D), lambda i: (i, 0)),
            pl.BlockSpec((R, TOP_K), lambda i: (i, 0)),
            pl.BlockSpec(memory_space=pl.ANY),
            const((D, F)), const((D, F)), const((F, D)),
        ],
        out_specs=pl.BlockSpec((R, D), lambda i: (i, 0)),
        scratch_shapes=[pltpu.VMEM((2, TOP_K - SC_COMBINE_SLOTS, R * ROW_TILE, LANES), jnp.float32),
                        pltpu.SemaphoreType.DMA((2,))],
        out_shape=jax.ShapeDtypeStruct((T, D), jnp.float32),
        compiler_params=pltpu.CompilerParams(
            dimension_semantics=("arbitrary",), vmem_limit_bytes=VMEM_LIMIT_BYTES),
    )(pos, pos, h2, top_w, ys, ws_gate.astype(bf16), ws_up.astype(bf16), ws_down.astype(bf16))


def _moe(h2, h2_packed, w_router, router_bias, w_gate, w_up, w_down, ws_gate, ws_up, ws_down, ln_g, ln_b):
    T = h2.shape[0]
    E = N_EXPERTS
    BM = EXPERT_ROWS
    e_idx, top_w, rank, counts = _router(h2, w_router, router_bias)
    cnt = counts[:, 0].astype(jnp.int32)
    nblk = (cnt + BM - 1) // BM
    blk_end = jnp.cumsum(nblk)
    row_start = ((blk_end - nblk) * BM).astype(jnp.int32)
    n_blocks = T * TOP_K // BM + E
    n_act = blk_end[-1:].astype(jnp.int32)
    blk_ids = jnp.minimum(jnp.arange(n_blocks, dtype=jnp.int32), n_act[0] - 1)
    blk_e = jnp.minimum(jnp.sum(blk_end[None, :] <= blk_ids[:, None], axis=1), E - 1).astype(jnp.int32)
    pos_kt = _positions(e_idx, rank, row_start).reshape(TOP_K, T)
    pos = pos_kt.T.reshape(T * TOP_K)
    xs = _dispatch_sc(h2_packed, pos_kt, n_blocks * BM)
    ys = _experts(xs, blk_e, n_act, row_start, cnt, w_gate, w_up, w_down)
    gathered = _gather_rows_sc(ys, pos_kt[:SC_COMBINE_SLOTS])
    w_tk = top_w.T
    partial = _combine(h2, pos, w_tk, ys, ws_gate, ws_up, ws_down)
    return _finish(h2, partial, w_tk, gathered, ln_g, ln_b)


def kernel(x, ln_in_g, ln_in_b, w_in, w_out, rel_bias, attn_sinks, ln_mix_g, ln_mix_b, w_router,
           router_bias, w_gate, w_up, w_down, ws_gate, ws_up, ws_down, ln_ffn_g, ln_ffn_b):
    B, S, D = x.shape
    h, h_packed = _mixer(x, ln_in_g, ln_in_b, w_in[0], w_out[0], rel_bias, attn_sinks[0], ln_mix_g[0],
                         ln_mix_b[0])
    out = _moe(h, h_packed, w_router[0], router_bias[0], w_gate[0], w_up[0], w_down[0],
               ws_gate[0], ws_up[0], ws_down[0], ln_ffn_g[0], ln_ffn_b[0])
    return out.reshape(B, S, D)
```

```python
import functools
import math

import jax
import jax.numpy as jnp
from jax import lax
from jax.experimental import pallas as pl
from jax.experimental.pallas import tpu as pltpu
from jax.experimental.pallas import tpu_sc as plsc

D_MODEL = 1024
DEPTH = 1
RET_HEADS = 4
RET_QK_DIM = 64
RET_V_DIM = 128
RET_CHUNK = 128
RET_WIDTH = RET_HEADS * RET_V_DIM
ROPE_BASE = 10000.0
SWA_HEADS = 8
SWA_KV_HEADS = 2
SWA_GROUP = SWA_HEADS // SWA_KV_HEADS
SWA_HEAD_DIM = 64
SWA_WINDOW = 128
SWA_WIDTH = SWA_HEADS * SWA_HEAD_DIM
MIX_WIDTH = RET_WIDTH + SWA_WIDTH
RQK = RET_HEADS * RET_QK_DIM
SKV = SWA_KV_HEADS * SWA_HEAD_DIM
IN_SIZES = (RQK, RQK, RET_WIDTH, RET_WIDTH, SWA_WIDTH, SKV, SKV)
IN_OFFS = tuple(sum(IN_SIZES[:i]) for i in range(len(IN_SIZES)))
IN_WIDTH = sum(IN_SIZES)
REL_BUCKETS = 32
REL_MAX_DIST = 128
N_EXPERTS = 256
TOP_K = 8
N_GROUPS = 8
GROUP_SIZE = N_EXPERTS // N_GROUPS
TOPK_GROUPS = 4
EXPERT_DIM = 256
SHARED_DIM = 256
ROUTED_SCALE = 2.5
LN_EPS = 1e-5
GN_EPS = 1e-6
DEEPNORM_ALPHA = (2 * DEPTH) ** 0.25
MASK_VALUE = -1e30

VMEM_LIMIT_BYTES = 56 * 1024 * 1024

MIX_ROWS = 256
ROUTE_ROWS = 256
DISPATCH_ROWS = 256
EXPERT_ROWS = 512
COMBINE_ROWS = 256
POSITION_ROWS = 64
FINISH_ROWS = 512


def _layer_norm(x, g, b):
    mu = jnp.mean(x, axis=-1, keepdims=True)
    xc = x - mu
    var = jnp.mean(xc * xc, axis=-1, keepdims=True)
    return xc * lax.rsqrt(var + LN_EPS) * g + b


def _dot(a, b):
    return jnp.dot(a, b, preferred_element_type=jnp.float32)


def _dot_nt(a, b):
    return lax.dot_general(a, b, (((1,), (1,)), ((), ())), preferred_element_type=jnp.float32)


def _dot_tn(a, b):
    return lax.dot_general(a, b, (((0,), (0,)), ((), ())), preferred_element_type=jnp.float32)


def _silu(x):
    return x * (1.0 / (1.0 + jnp.exp(-x)))


LANES = 128
ROW_TILE = D_MODEL // LANES


def _load_rows(ref, n_rows, lead=()):
    return jnp.concatenate([ref[lead + (pl.ds(s, n_rows, stride=ROW_TILE), slice(None))]
                            for s in range(ROW_TILE)], axis=1)


def _store_rows(ref, val, lead=(), first_row=0):
    n_rows = val.shape[0]
    for s in range(ROW_TILE):
        dst = pl.ds(first_row * ROW_TILE + s, n_rows, stride=ROW_TILE)
        ref[lead + (dst, slice(None))] = val[:, s * LANES:(s + 1) * LANES]


def _row_tile(r):
    return pl.ds(pl.multiple_of(r * ROW_TILE, ROW_TILE), ROW_TILE)


def _pack_bf16_pairs(x):
    m = x.shape[1] // 2
    bits = lax.bitcast_convert_type(x.astype(jnp.bfloat16).astype(jnp.float32), jnp.uint32)
    return (bits[:, :m] >> 16) | (bits[:, m:] & jnp.uint32(0xFFFF0000))


def _unpack_bf16_pairs(p):
    lo = lax.bitcast_convert_type(p << 16, jnp.float32)
    hi = lax.bitcast_convert_type(p & jnp.uint32(0xFFFF0000), jnp.float32)
    return jnp.concatenate([lo, hi], axis=1).astype(jnp.bfloat16)


def _swap_halves(x):
    n = x.shape[-1]
    half = RET_QK_DIM // 2
    lane = lax.broadcasted_iota(jnp.int32, x.shape, 1)
    from_right = pltpu.roll(x, n - half, axis=1)
    from_left = pltpu.roll(x, half, axis=1)
    return jnp.where((lane % RET_QK_DIM) < half, from_right, from_left)


def _mixer_kernel(rel_bias_ref, x_ref, g_in_ref, b_in_ref, w_in_ref, w_out_ref, rot_ref, decay_ref,
                  zeta_ref, xi_ref, cdecay_ref, bucket_ref, sink_ref, g_mix_ref, b_mix_ref,
                  h2_ref, h2p_ref, state_ref, kprev_ref, vprev_ref, bias_ref):
    b_id = pl.program_id(0)
    c_id = pl.program_id(1)
    W = SWA_WINDOW

    @pl.when((b_id == 0) & (c_id == 0))
    def _build_bias():
        bucket = bucket_ref[...]
        for h in range(SWA_HEADS):
            acc = jnp.full((2 * W, W), MASK_VALUE, jnp.float32)
            for b in range(REL_BUCKETS):
                acc = jnp.where(bucket == b, rel_bias_ref[b, h], acc)
            kh, g = divmod(h, SWA_GROUP)
            bias_ref[kh, :, g * W:(g + 1) * W] = acc

    @pl.when(c_id == 0)
    def _reset():
        state_ref[...] = jnp.zeros_like(state_ref)
        kprev_ref[...] = jnp.zeros_like(kprev_ref)
        vprev_ref[...] = jnp.zeros_like(vprev_ref)

    h = _layer_norm(x_ref[...], g_in_ref[...], b_in_ref[...])
    proj = _dot(h.astype(jnp.bfloat16), w_in_ref[...])

    o_q, o_k, o_v, o_g, o_sq, o_sk, o_sv = IN_OFFS
    cos_t = rot_ref[:, :RQK]
    sin_t = rot_ref[:, RQK:]
    q_all = proj[:, o_q:o_q + RQK]
    k_all = proj[:, o_k:o_k + RQK]
    q_rot = q_all * cos_t + _swap_halves(q_all) * sin_t
    k_rot = (k_all * cos_t + _swap_halves(k_all) * sin_t) * (RET_QK_DIM ** -0.5)

    n_sub = x_ref.shape[0] // RET_CHUNK
    states = [state_ref[hh] for hh in range(RET_HEADS)]
    k_prev = kprev_ref[...]
    v_prev = vprev_ref[...]
    bf16 = jnp.bfloat16
    heads = range(RET_HEADS)
    ret_pieces, swa_pieces = [], []
    swa_jobs = []
    for s in range(n_sub):
        rows = slice(s * RET_CHUNK, (s + 1) * RET_CHUNK)
        k_cur = proj[rows, o_sk:o_sk + SKV].astype(bf16)
        v_cur = proj[rows, o_sv:o_sv + SKV].astype(bf16)
        for kh in range(SWA_KV_HEADS):
            kv = slice(kh * SWA_HEAD_DIM, (kh + 1) * SWA_HEAD_DIM)
            q4 = jnp.concatenate(
                [proj[rows, o_sq + (kh * SWA_GROUP + g) * SWA_HEAD_DIM:
                      o_sq + (kh * SWA_GROUP + g + 1) * SWA_HEAD_DIM] for g in range(SWA_GROUP)],
                axis=0) * (SWA_HEAD_DIM ** -0.5)
            kcat = jnp.concatenate([k_prev[:, kv], k_cur[:, kv]], axis=0)
            vcat = jnp.concatenate([v_prev[:, kv], v_cur[:, kv]], axis=0)
            swa_jobs.append((s, kh, q4.astype(bf16), kcat, vcat))
        k_prev, v_prev = k_cur, v_cur
    logits_all = [_dot_nt(kcat, q4) + bias_ref[kh] for (s, kh, q4, kcat, vcat) in swa_jobs]
    probs_all = []
    for (s, kh, q4, kcat, vcat), logits in zip(swa_jobs, logits_all):
        if s == 0:
            key = lax.broadcasted_iota(jnp.int32, logits.shape, 0)
            logits = logits + jnp.where((key < W) & (c_id == 0), MASK_VALUE, 0.0)
        sink = sink_ref[kh]
        m = jnp.maximum(jnp.max(logits, axis=0, keepdims=True), sink)
        p = jnp.exp(logits - m)
        den = jnp.sum(p, axis=0, keepdims=True) + jnp.exp(sink - m)
        probs_all.append((p / den).astype(bf16))
    o4_all = [_dot_tn(job[4], probs) for job, probs in zip(swa_jobs, probs_all)]
    for s in range(n_sub):
        swa_pieces.append([o4[:, g * W:(g + 1) * W].T.astype(bf16)
                           for job, o4 in zip(swa_jobs, o4_all) if job[0] == s for g in range(SWA_GROUP)])
    for s in range(n_sub):
        rows = slice(s * RET_CHUNK, (s + 1) * RET_CHUNK)
        qk = [slice(hh * RET_QK_DIM, (hh + 1) * RET_QK_DIM) for hh in heads]
        q = [q_rot[rows, qk[hh]].astype(bf16) for hh in heads]
        k32 = [k_rot[rows, qk[hh]] for hh in heads]
        v = [proj[rows, o_v + hh * RET_V_DIM:o_v + (hh + 1) * RET_V_DIM].astype(bf16) for hh in heads]
        scores = [_dot_nt(q[hh], k32[hh].astype(bf16)) * decay_ref[hh] for hh in heads]
        inter = [_dot(q[hh], states[hh].astype(bf16)) * xi_ref[hh] for hh in heads]
        kv_new = [_dot_tn((k32[hh] * zeta_ref[hh]).astype(bf16), v[hh]) for hh in heads]
        intra = [_dot(scores[hh].astype(bf16), v[hh]) for hh in heads]
        states = [states[hh] * cdecay_ref[hh] + kv_new[hh] for hh in heads]
        pieces = []
        for hh in heads:
            ret = intra[hh] + inter[hh]
            mu = jnp.mean(ret, axis=-1, keepdims=True)
            rc = ret - mu
            var = jnp.mean(rc * rc, axis=-1, keepdims=True)
            normed = rc * lax.rsqrt(var + GN_EPS)
            gate = proj[rows, o_g + hh * RET_V_DIM:o_g + (hh + 1) * RET_V_DIM]
            pieces.append((_silu(gate) * normed).astype(bf16))
        ret_pieces.append(pieces)
    cat_rows = [jnp.concatenate(ret_pieces[s] + swa_pieces[s], axis=1) for s in range(n_sub)]
    for hh in range(RET_HEADS):
        state_ref[hh] = states[hh]
    kprev_ref[...] = k_prev
    vprev_ref[...] = v_prev

    mix = _dot(jnp.concatenate(cat_rows, axis=0), w_out_ref[...])
    h2 = _layer_norm(DEEPNORM_ALPHA * h + mix, g_mix_ref[...], b_mix_ref[...])
    h2_ref[...] = h2
    h2p_ref[...] = _pack_bf16_pairs(h2)


def _t5_bucket(dist):
    n = jnp.maximum(dist, 0)
    max_exact = REL_BUCKETS // 2
    ratio = jnp.log(jnp.maximum(n, 1).astype(jnp.float32) / max_exact) / math.log(REL_MAX_DIST / max_exact)
    large = jnp.minimum(max_exact + (ratio * (REL_BUCKETS - max_exact)).astype(jnp.int32), REL_BUCKETS - 1)
    return jnp.where(n < max_exact, n, large)


def _mixer(x, ln_in_g, ln_in_b, w_in, w_out, rel_bias, sinks, ln_mix_g, ln_mix_b):
    B, S, D = x.shape
    R = MIX_ROWS
    C = RET_CHUNK
    W = SWA_WINDOW
    f32 = jnp.float32
    half = RET_QK_DIM // 2
    inv = ROPE_BASE ** (-jnp.arange(half, dtype=f32) / half)
    ang = jnp.arange(S, dtype=f32)[:, None] * inv[None, :]
    cos, sin = jnp.cos(ang), jnp.sin(ang)
    cos_t = jnp.tile(jnp.concatenate([cos, cos], axis=-1), (1, RET_HEADS))
    sin_t = jnp.tile(jnp.concatenate([-sin, sin], axis=-1), (1, RET_HEADS))
    rot = jnp.concatenate([cos_t, sin_t], axis=-1)
    log_gamma = jnp.log(1.0 - 2.0 ** (-5.0 - jnp.arange(RET_HEADS, dtype=f32)))
    idx = jnp.arange(C, dtype=f32)
    diff = idx[:, None] - idx[None, :]
    decay = jnp.where(diff[None] >= 0, jnp.exp(jnp.maximum(diff, 0.0)[None] * log_gamma[:, None, None]), 0.0)
    zeta = jnp.exp((C - 1.0 - idx)[None, :] * log_gamma[:, None])
    xi = jnp.exp((idx + 1.0)[None, :] * log_gamma[:, None])
    zeta_b = jnp.broadcast_to(zeta[:, :, None], (RET_HEADS, C, RET_QK_DIM))
    xi_b = jnp.broadcast_to(xi[:, :, None], (RET_HEADS, C, RET_V_DIM))
    cdecay = jnp.broadcast_to(jnp.exp(C * log_gamma)[:, None, None], (RET_HEADS, RET_QK_DIM, RET_V_DIM))
    i = jnp.arange(W)
    j = jnp.arange(2 * W)
    dist = i[:, None] + W - j[None, :]
    bucket = jnp.where((dist >= 0) & (dist < W), _t5_bucket(dist), -1).astype(jnp.int32).T
    sink_row = jnp.repeat(sinks.astype(f32), W).reshape(SWA_KV_HEADS, 1, SWA_GROUP * W)

    const = lambda shape: pl.BlockSpec(shape, lambda b, c, *_: (0,) * len(shape))
    grid_spec = pltpu.PrefetchScalarGridSpec(
        num_scalar_prefetch=1,
        grid=(B, S // R),
        in_specs=[
            pl.BlockSpec((None, R, D), lambda b, c, *_: (b, c, 0)),
            const((1, D)), const((1, D)),
            const((D, IN_WIDTH)), const((MIX_WIDTH, D)),
            pl.BlockSpec((R, 2 * RQK), lambda b, c, *_: (c, 0)),
            const((RET_HEADS, C, C)), const((RET_HEADS, C, RET_QK_DIM)), const((RET_HEADS, C, RET_V_DIM)),
            const((RET_HEADS, RET_QK_DIM, RET_V_DIM)),
            const((2 * W, W)), const((SWA_KV_HEADS, 1, SWA_GROUP * W)),
            const((1, D)), const((1, D)),
        ],
        out_specs=[pl.BlockSpec((R, D), lambda b, c, *_: (b * (S // R) + c, 0)),
                   pl.BlockSpec((R, D // 2), lambda b, c, *_: (b * (S // R) + c, 0))],
        scratch_shapes=[
            pltpu.VMEM((RET_HEADS, RET_QK_DIM, RET_V_DIM), f32),
            pltpu.VMEM((W, SKV), jnp.bfloat16),
            pltpu.VMEM((W, SKV), jnp.bfloat16),
            pltpu.VMEM((SWA_KV_HEADS, 2 * W, SWA_GROUP * W), f32),
        ],
    )
    return pl.pallas_call(
        _mixer_kernel,
        grid_spec=grid_spec,
        out_shape=[jax.ShapeDtypeStruct((B * S, D), f32), jax.ShapeDtypeStruct((B * S, D // 2), jnp.uint32)],
        compiler_params=pltpu.CompilerParams(
            dimension_semantics=("arbitrary", "arbitrary"), vmem_limit_bytes=VMEM_LIMIT_BYTES),
    )(rel_bias.astype(f32), x, ln_in_g.reshape(1, D), ln_in_b.reshape(1, D),
      w_in.astype(jnp.bfloat16), w_out.astype(jnp.bfloat16), rot, decay, zeta_b, xi_b, cdecay,
      bucket, sink_row, ln_mix_g.reshape(1, D), ln_mix_b.reshape(1, D))


def _router_kernel(h_ref, wr_ref, rb_ref, e_ref, w_ref, rk_ref, cnt_ref, run_ref):
    f32 = jnp.float32
    R = h_ref.shape[0]
    E = N_EXPERTS
    neg = -jnp.inf

    @pl.when(pl.program_id(0) == 0)
    def _init():
        run_ref[...] = jnp.zeros_like(run_ref)

    logits = _dot_nt(wr_ref[...], h_ref[...].astype(jnp.bfloat16))
    scores = 1.0 / (1.0 + jnp.exp(-logits))
    choice = scores + rb_ref[...]
    eid = lax.broadcasted_iota(jnp.int32, (E, R), 0)

    def first_argmax(vals, ids, none):
        m = jnp.max(vals, axis=0, keepdims=True)
        idx = jnp.min(jnp.where(vals == m, ids, none), axis=0, keepdims=True)
        return m, idx

    gid = lax.broadcasted_iota(jnp.int32, (GROUP_SIZE, R), 0)
    groups, gscore = [], []
    for g in range(N_GROUPS):
        vals = choice[g * GROUP_SIZE:(g + 1) * GROUP_SIZE]
        m1, i1 = first_argmax(vals, gid, GROUP_SIZE)
        m2 = jnp.max(jnp.where(gid == i1, neg, vals), axis=0, keepdims=True)
        groups.append(vals)
        gscore.append(m1 + m2)
    kept = []
    for g in range(N_GROUPS):
        beaten = jnp.zeros((1, R), f32)
        for g2 in range(N_GROUPS):
            if g2 == g:
                continue
            ahead = (gscore[g2] > gscore[g]) | (gscore[g2] == gscore[g]) if g2 < g else gscore[g2] > gscore[g]
            beaten = beaten + jnp.where(ahead, 1.0, 0.0)
        kept.append(jnp.where(beaten < TOPK_GROUPS, groups[g], neg))
    masked = jnp.concatenate(kept, axis=0)

    idxs, wts = [], []
    picked = jnp.zeros((E, R), f32)
    for _ in range(TOP_K):
        _, idx = first_argmax(masked, eid, E)
        hit = eid == idx
        idxs.append(idx)
        wts.append(jnp.sum(jnp.where(hit, scores, 0.0), axis=0, keepdims=True))
        masked = jnp.where(hit, neg, masked)
        picked = jnp.where(hit, 1.0, picked)
    wsum = wts[0]
    for k in range(1, TOP_K):
        wsum = wsum + wts[k]

    row = lax.broadcasted_iota(jnp.int32, (R, R), 0)
    col = lax.broadcasted_iota(jnp.int32, (R, R), 1)
    earlier = jnp.where(row < col, 1.0, 0.0).astype(jnp.bfloat16)
    picked_bf = picked.astype(jnp.bfloat16)
    run = run_ref[...]
    before = _dot(picked_bf, earlier) + jnp.concatenate([run] * (R // LANES), axis=1)
    sub_k = lax.broadcasted_iota(jnp.int32, (TOP_K, R), 0)
    e_out = jnp.zeros((TOP_K, R), jnp.int32)
    w_out = jnp.zeros((TOP_K, R), f32)
    rk_out = jnp.zeros((TOP_K, R), jnp.int32)
    for k in range(TOP_K):
        rank_k = jnp.sum(jnp.where(eid == idxs[k], before, 0.0), axis=0, keepdims=True)
        e_out = jnp.where(sub_k == k, idxs[k], e_out)
        w_out = jnp.where(sub_k == k, wts[k] / wsum * ROUTED_SCALE, w_out)
        rk_out = jnp.where(sub_k == k, rank_k.astype(jnp.int32), rk_out)
    e_ref[...] = e_out
    w_ref[...] = w_out
    rk_ref[...] = rk_out
    run_ref[...] = run + _dot(picked_bf, jnp.ones((R, LANES), jnp.bfloat16))
    cnt_ref[...] = run_ref[...]


def _router(h2, w_router, router_bias):
    T, D = h2.shape
    R = ROUTE_ROWS
    E = N_EXPERTS
    return pl.pallas_call(
        _router_kernel,
        grid=(T // R,),
        in_specs=[
            pl.BlockSpec((R, D), lambda i: (i, 0)),
            pl.BlockSpec((E, D), lambda i: (0, 0)),
            pl.BlockSpec((E, R), lambda i: (0, 0)),
        ],
        out_specs=[
            pl.BlockSpec((TOP_K, R), lambda i: (0, i)),
            pl.BlockSpec((TOP_K, R), lambda i: (0, i)),
            pl.BlockSpec((TOP_K, R), lambda i: (0, i)),
            pl.BlockSpec((E, LANES), lambda i: (0, 0)),
        ],
        out_shape=[
            jax.ShapeDtypeStruct((TOP_K, T), jnp.int32),
            jax.ShapeDtypeStruct((TOP_K, T), jnp.float32),
            jax.ShapeDtypeStruct((TOP_K, T), jnp.int32),
            jax.ShapeDtypeStruct((E, LANES), jnp.float32),
        ],
        scratch_shapes=[pltpu.VMEM((E, LANES), jnp.float32)],
        compiler_params=pltpu.CompilerParams(
            dimension_semantics=("arbitrary",), vmem_limit_bytes=VMEM_LIMIT_BYTES),
    )(h2, w_router.T.astype(jnp.bfloat16), jnp.broadcast_to(router_bias.astype(jnp.float32)[:, None], (E, R)))


def _positions_kernel(row_start_ref, e_ref, rk_ref, pos_ref):
    e = e_ref[...]
    rk = rk_ref[...]

    def per_expert(i, pos):
        return jnp.where(e == i, rk + row_start_ref[i], pos)

    pos_ref[...] = lax.fori_loop(0, N_EXPERTS, per_expert, jnp.zeros_like(rk))


def _positions(e_idx, rank, row_start):
    n = e_idx.size
    shape = (n // LANES, LANES)
    block = pl.BlockSpec((POSITION_ROWS, LANES), lambda i, *_: (i, 0))
    grid_spec = pltpu.PrefetchScalarGridSpec(
        num_scalar_prefetch=1,
        grid=(shape[0] // POSITION_ROWS,),
        in_specs=[block, block],
        out_specs=block,
    )
    pos = pl.pallas_call(
        _positions_kernel,
        grid_spec=grid_spec,
        out_shape=jax.ShapeDtypeStruct(shape, jnp.int32),
    )(row_start, e_idx.reshape(shape), rank.reshape(shape))
    return pos.reshape(n)


def _dispatch_kernel(row_start_ref, cnt_ref, n_act_ref, pos_ref, h_ref, xs_ref, zero_ref, sem, zsem,
                     *, n_tokens):
    R = h_ref.shape[0] // ROW_TILE
    BM = EXPERT_ROWS
    n_blocks = xs_ref.shape[0] // (BM * ROW_TILE)
    n_pad_units = n_blocks - n_tokens * TOP_K // BM

    @pl.when(pl.program_id(0) == 0)
    def _zero_padding():
        zero_ref[...] = jnp.zeros_like(zero_ref)

        def expert_tail(e, carry):
            n_tail = pl.multiple_of(((BM - cnt_ref[e] % BM) % BM) * ROW_TILE, ROW_TILE)

            @pl.when(n_tail > 0)
            def _():
                dst = pl.multiple_of((row_start_ref[e] + cnt_ref[e]) * ROW_TILE, ROW_TILE)
                pltpu.make_async_copy(zero_ref.at[pl.ds(0, n_tail)], xs_ref.at[pl.ds(dst, n_tail)], zsem).start()
            return carry

        lax.fori_loop(0, N_EXPERTS, expert_tail, 0)

        def idle_block(i, carry):
            dst = pl.multiple_of(i * (BM * ROW_TILE), BM * ROW_TILE)
            pltpu.make_async_copy(zero_ref, xs_ref.at[pl.ds(dst, BM * ROW_TILE)], zsem).start()
            return carry

        lax.fori_loop(n_act_ref[0], n_blocks, idle_block, 0)

        def drain(i, carry):
            pltpu.make_async_copy(zero_ref, xs_ref.at[pl.ds(0, BM * ROW_TILE)], zsem).wait()
            return carry

        lax.fori_loop(0, n_pad_units, drain, 0)

    def issue(t, carry):
        src = h_ref.at[_row_tile(t)]
        for k in range(TOP_K):
            dest = pos_ref[t * TOP_K + k]
            pltpu.make_async_copy(src, xs_ref.at[_row_tile(dest)], sem).start(priority=k % 2)
        return carry

    lax.fori_loop(0, R, issue, 0)
    n = R * TOP_K * ROW_TILE
    pltpu.make_async_copy(xs_ref.at[pl.ds(0, n)], xs_ref.at[pl.ds(0, n)], sem).wait()


def _dispatch(h2, pos, row_start, cnt, n_act, n_rows):
    T = h2.shape[0] // ROW_TILE
    R = DISPATCH_ROWS
    grid_spec = pltpu.PrefetchScalarGridSpec(
        num_scalar_prefetch=3,
        grid=(T // R,),
        in_specs=[
            pl.BlockSpec((R * TOP_K,), lambda i, *_: (i,), memory_space=pltpu.SMEM),
            pl.BlockSpec((R * ROW_TILE, LANES), lambda i, *_: (i, 0)),
        ],
        out_specs=pl.BlockSpec(memory_space=pl.ANY),
        scratch_shapes=[pltpu.VMEM((EXPERT_ROWS * ROW_TILE, LANES), jnp.float32),
                        pltpu.SemaphoreType.DMA(()), pltpu.SemaphoreType.DMA(())],
    )
    return pl.pallas_call(
        functools.partial(_dispatch_kernel, n_tokens=T),
        grid_spec=grid_spec,
        out_shape=jax.ShapeDtypeStruct((n_rows * ROW_TILE, LANES), jnp.float32),
        compiler_params=pltpu.CompilerParams(
            dimension_semantics=("arbitrary",), vmem_limit_bytes=VMEM_LIMIT_BYTES),
    )(row_start, cnt, n_act, pos, h2)


SC_WINDOW = 32


def _dispatch_sc(h2, pos_kt, n_rows):
    T, D = h2.shape
    W = SC_WINDOW
    idx = _window_indices(pos_kt, W)
    idx_rows = TOP_K * W // LANES
    mesh = plsc.VectorSubcoreMesh(core_axis_name="core", subcore_axis_name="subcore")

    @pl.kernel(out_type=jax.ShapeDtypeStruct((n_rows, D), h2.dtype), mesh=mesh, scratch_types=[])
    def scatter_rows(x_hbm, i_hbm, o_hbm):
        def body(x_vmem, i_vmem):
            for k in range(TOP_K):
                r, q = divmod(k * W, LANES)
                pltpu.sync_copy(x_vmem, o_hbm.at[i_vmem.at[r, pl.ds(q, W)]])

        pltpu.emit_pipeline(
            body,
            grid=(T // W,),
            in_specs=[pl.BlockSpec((W, D), lambda i: (i, 0)),
                      pl.BlockSpec((idx_rows, LANES), lambda i: (i, 0))],
            out_specs=[],
            core_axis_name=("core", "subcore"),
            dimension_semantics=(pltpu.PARALLEL,),
        )(x_hbm, i_hbm)

    return scatter_rows(h2, idx)


def _window_indices(pos_kt, window):
    K, T = pos_kt.shape
    return pos_kt.reshape(K, T // window, window).transpose(1, 0, 2).reshape(T * K // LANES, LANES)


def _gather_rows_sc(ys, pos_kt):
    n = pos_kt.size
    W = SC_WINDOW
    ys3 = ys.reshape(ys.shape[0] // ROW_TILE, ROW_TILE, LANES)
    idx = jnp.pad(pos_kt.reshape(n // W, W), ((0, 0), (0, LANES - W)))
    mesh = plsc.VectorSubcoreMesh(core_axis_name="core", subcore_axis_name="subcore")

    @pl.kernel(out_type=jax.ShapeDtypeStruct((n, ROW_TILE, LANES), ys.dtype), mesh=mesh, scratch_types=[])
    def gather_rows(y_hbm, i_hbm, o_hbm):
        def body(i_vmem, o_vmem):
            pltpu.sync_copy(y_hbm.at[i_vmem.at[0, pl.ds(0, W)]], o_vmem)

        pltpu.emit_pipeline(
            body,
            grid=(n // W,),
            in_specs=[pl.BlockSpec((1, LANES), lambda i: (i, 0))],
            out_specs=[pl.BlockSpec((W, ROW_TILE, LANES), lambda i: (i, 0, 0))],
            core_axis_name=("core", "subcore"),
            dimension_semantics=(pltpu.PARALLEL,),
        )(i_hbm, o_hbm)

    return gather_rows(ys3, idx).reshape(n * ROW_TILE, LANES)


EXPERT_SUBROWS = 128
X_SLOTS = 4
Y_SLOTS = 3


def _experts_kernel(blk_e_ref, first_ref, slot_ref, next_e_ref, valid_ref, n_act_ref, xs_hbm, wg_hbm,
                    wu_hbm, wd_hbm, ys_hbm, x_buf, y_buf, wg_buf, wu_buf, wd_buf, wg_bf, wu_bf, wd_bf,
                    sems, x_sems, y_sems):
    i = pl.program_id(0)
    n_act = n_act_ref[0]
    bf16 = jnp.bfloat16
    blk = EXPERT_ROWS * ROW_TILE

    def x_copy(j):
        n = pl.multiple_of((valid_ref[j] + 7) // 8 * 8, 8)
        src = xs_hbm.at[pl.ds(pl.multiple_of(j * EXPERT_ROWS, EXPERT_ROWS), n)]
        return pltpu.make_async_copy(src, x_buf.at[j % X_SLOTS, pl.ds(0, n)], x_sems.at[j % X_SLOTS])

    def y_copy(j):
        n = pl.multiple_of(valid_ref[j] * ROW_TILE, ROW_TILE)
        dst = ys_hbm.at[pl.ds(pl.multiple_of(j * blk, blk), n)]
        return pltpu.make_async_copy(y_buf.at[j % Y_SLOTS, pl.ds(0, n)], dst, y_sems.at[j % Y_SLOTS])

    @pl.when(i == 0)
    def _prime():
        x_buf[...] = jnp.zeros_like(x_buf)
        for j in range(X_SLOTS - 1):
            @pl.when(j < n_act)
            def _():
                x_copy(j).start()

    @pl.when(i + (X_SLOTS - 1) < n_act)
    def _prefetch():
        x_copy(i + (X_SLOTS - 1)).start()

    def weight_copies(e, slot):
        return (pltpu.make_async_copy(wg_hbm.at[e], wg_buf.at[slot], sems.at[slot]),
                pltpu.make_async_copy(wu_hbm.at[e], wu_buf.at[slot], sems.at[slot]),
                pltpu.make_async_copy(wd_hbm.at[e], wd_buf.at[slot], sems.at[slot]))

    @pl.when((i < n_act_ref[0]) & (first_ref[i] == 1))
    def _new_expert():
        slot = slot_ref[i]

        @pl.when(i == 0)
        def _():
            for c in weight_copies(blk_e_ref[0], 0):
                c.start()

        for c in weight_copies(blk_e_ref[i], slot):
            c.wait()

        @pl.when(next_e_ref[i] >= 0)
        def _():
            for c in weight_copies(next_e_ref[i], 1 - slot):
                c.start()

        wg_bf[...] = wg_buf[slot].astype(bf16)
        wu_bf[...] = wu_buf[slot].astype(bf16)
        wd_bf[...] = wd_buf[slot].astype(bf16)

    @pl.when(i < n_act)
    def _compute():
        x_copy(i).wait()

        @pl.when(i >= Y_SLOTS)
        def _():
            y_copy(i - Y_SLOTS).wait()

        sub = EXPERT_SUBROWS
        n_parts = (valid_ref[i] + sub - 1) // sub

        def run(parts):
            xs_ = [_unpack_bf16_pairs(x_buf[i % X_SLOTS, pl.ds(part * sub, sub), :]) for part in range(parts)]
            gs = [_dot(x, wg_bf[...]) for x in xs_]
            us = [_dot(x, wu_bf[...]) for x in xs_]
            acts = [(_silu(g) * u).astype(bf16) for g, u in zip(gs, us)]
            for part in range(parts):
                _store_rows(y_buf, _dot(acts[part], wd_bf[...]), lead=(i % Y_SLOTS,), first_row=part * sub)

        for parts in range(1, EXPERT_ROWS // sub + 1):
            pl.when(n_parts == parts)(functools.partial(run, parts))
        y_copy(i).start()

    @pl.when(i == n_act - 1)
    def _drain():
        for d in range(Y_SLOTS):
            @pl.when(i - d >= 0)
            def _():
                y_copy(i - d).wait()


def _experts(xs, blk_e, n_act, row_start, cnt, w_gate, w_up, w_down):
    D = D_MODEL
    BM = EXPERT_ROWS
    F = EXPERT_DIM
    n_blocks = xs.shape[0] // BM
    blk_in_expert = jnp.arange(n_blocks, dtype=jnp.int32) - row_start[blk_e] // BM
    valid = jnp.clip(cnt[blk_e] - blk_in_expert * BM, 0, BM).astype(jnp.int32)
    ids = jnp.arange(n_blocks, dtype=jnp.int32)
    active = ids < n_act[0]
    first = active & ((ids == 0) | (blk_e != jnp.roll(blk_e, 1)))
    slot = ((jnp.cumsum(first.astype(jnp.int32)) - 1) % 2).astype(jnp.int32)
    first_pos = jnp.where(first, ids, n_blocks)
    later_first = lax.cummin(jnp.concatenate([first_pos[1:], jnp.full((1,), n_blocks, jnp.int32)]), reverse=True)
    next_e = jnp.where(later_first < n_blocks, blk_e[jnp.minimum(later_first, n_blocks - 1)], -1).astype(jnp.int32)

    grid_spec = pltpu.PrefetchScalarGridSpec(
        num_scalar_prefetch=6,
        grid=(n_blocks,),
        in_specs=[pl.BlockSpec(memory_space=pl.ANY)] * 4,
        out_specs=pl.BlockSpec(memory_space=pl.ANY),
        scratch_shapes=[
            pltpu.VMEM((X_SLOTS, BM, D // 2), jnp.uint32),
            pltpu.VMEM((Y_SLOTS, BM * ROW_TILE, LANES), jnp.float32),
            pltpu.VMEM((2, D, F), jnp.float32), pltpu.VMEM((2, D, F), jnp.float32),
            pltpu.VMEM((2, F, D), jnp.float32),
            pltpu.VMEM((D, F), jnp.bfloat16), pltpu.VMEM((D, F), jnp.bfloat16),
            pltpu.VMEM((F, D), jnp.bfloat16),
            pltpu.SemaphoreType.DMA((2,)), pltpu.SemaphoreType.DMA((X_SLOTS,)),
            pltpu.SemaphoreType.DMA((Y_SLOTS,)),
        ],
    )
    return pl.pallas_call(
        _experts_kernel,
        grid_spec=grid_spec,
        out_shape=jax.ShapeDtypeStruct((xs.shape[0] * ROW_TILE, LANES), jnp.float32),
        compiler_params=pltpu.CompilerParams(
            dimension_semantics=("arbitrary",), vmem_limit_bytes=VMEM_LIMIT_BYTES),
    )(blk_e, first.astype(jnp.int32), slot, next_e, valid, n_act, xs, w_gate, w_up, w_down)


def _finish_kernel(h_ref, part_ref, w_ref, *rest):
    slabs = rest[:SC_COMBINE_SLOTS]
    g_ref, b_ref, out_ref = rest[SC_COMBINE_SLOTS:]
    R = h_ref.shape[0]
    w = w_ref[...]
    ffn = part_ref[...]
    for k in range(SC_COMBINE_SLOTS):
        ffn = ffn + _load_rows(slabs[k], R) * w[:, k:k + 1]
    out_ref[...] = _layer_norm(DEEPNORM_ALPHA * h_ref[...] + ffn, g_ref[...], b_ref[...])


def _finish(h2, partial, top_w, gathered, ln_g, ln_b):
    T, D = h2.shape
    R = FINISH_ROWS
    rows = pl.BlockSpec((R, D), lambda i: (i, 0))
    vec = pl.BlockSpec((1, D), lambda i: (0, 0))
    slab = lambda k: pl.BlockSpec((R * ROW_TILE, LANES), lambda i: (k * (T // R) + i, 0))
    return pl.pallas_call(
        _finish_kernel,
        grid=(T // R,),
        in_specs=[rows, rows, pl.BlockSpec((R, TOP_K), lambda i: (i, 0))]
        + [slab(k) for k in range(SC_COMBINE_SLOTS)] + [vec, vec],
        out_specs=rows,
        out_shape=jax.ShapeDtypeStruct((T, D), jnp.float32),
        compiler_params=pltpu.CompilerParams(
            dimension_semantics=("arbitrary",), vmem_limit_bytes=VMEM_LIMIT_BYTES),
    )(h2, partial, top_w, *([gathered] * SC_COMBINE_SLOTS), ln_g.reshape(1, D), ln_b.reshape(1, D))


SC_COMBINE_SLOTS = 5


def _combine_kernel(pos_ref, pos_next_ref, h_ref, w_ref, ys_ref, wsg_ref, wsu_ref, wsd_ref,
                    out_ref, buf_ref, sems):
    R = h_ref.shape[0]
    i = pl.program_id(0)
    slot = i % 2

    def gather(p_ref, s):
        def issue(t, carry):
            for k in range(SC_COMBINE_SLOTS, TOP_K):
                src = p_ref[t * TOP_K + k]
                pltpu.make_async_copy(ys_ref.at[_row_tile(src)], buf_ref.at[s, k - SC_COMBINE_SLOTS, _row_tile(t)],
                                      sems.at[s]).start(priority=k % 2)
            return carry

        lax.fori_loop(0, R, issue, 0)

    @pl.when(i == 0)
    def _():
        gather(pos_ref, 0)

    @pl.when(i + 1 < pl.num_programs(0))
    def _():
        gather(pos_next_ref, 1 - slot)

    h = h_ref[...]
    hb = h.astype(jnp.bfloat16)
    act = (_silu(_dot(hb, wsg_ref[...])) * _dot(hb, wsu_ref[...])).astype(jnp.bfloat16)
    ffn = _dot(act, wsd_ref[...])
    pltpu.make_async_copy(buf_ref.at[slot], buf_ref.at[slot], sems.at[slot]).wait()
    w = w_ref[...]
    for k in range(SC_COMBINE_SLOTS, TOP_K):
        ffn = ffn + _load_rows(buf_ref, R, lead=(slot, k - SC_COMBINE_SLOTS)) * w[:, k:k + 1]
    out_ref[...] = ffn


def _combine(h2, pos, top_w, ys, ws_gate, ws_up, ws_down):
    T, D = h2.shape
    R = COMBINE_ROWS
    F = SHARED_DIM
    bf16 = jnp.bfloat16
    const = lambda shape: pl.BlockSpec(shape, lambda i: (0,) * len(shape))
    return pl.pallas_call(
        _combine_kernel,
        grid=(T // R,),
        in_specs=[
            pl.BlockSpec((R * TOP_K,), lambda i: (i,), memory_space=pltpu.SMEM),
            pl.BlockSpec((R * TOP_K,), lambda i: (jnp.minimum(i + 1, T // R - 1),), memory_space=pltpu.SMEM),
            pl.BlockSpec((R, D), lambda i: (i, 0)),
            pl.BlockSpec((R, TOP_K), lambda i: (i, 0)),
            pl.BlockSpec(memory_space=pl.ANY),
            const((D, F)), const((D, F)), const((F, D)),
        ],
        out_specs=pl.BlockSpec((R, D), lambda i: (i, 0)),
        scratch_shapes=[pltpu.VMEM((2, TOP_K - SC_COMBINE_SLOTS, R * ROW_TILE, LANES), jnp.float32),
                        pltpu.SemaphoreType.DMA((2,))],
        out_shape=jax.ShapeDtypeStruct((T, D), jnp.float32),
        compiler_params=pltpu.CompilerParams(
            dimension_semantics=("arbitrary",), vmem_limit_bytes=VMEM_LIMIT_BYTES),
    )(pos, pos, h2, top_w, ys, ws_gate.astype(bf16), ws_up.astype(bf16), ws_down.astype(bf16))


def _moe(h2, h2_packed, w_router, router_bias, w_gate, w_up, w_down, ws_gate, ws_up, ws_down, ln_g, ln_b):
    T = h2.shape[0]
    E = N_EXPERTS
    BM = EXPERT_ROWS
    e_idx, top_w, rank, counts = _router(h2, w_router, router_bias)
    cnt = counts[:, 0].astype(jnp.int32)
    nblk = (cnt + BM - 1) // BM
    blk_end = jnp.cumsum(nblk)
    row_start = ((blk_end - nblk) * BM).astype(jnp.int32)
    n_blocks = T * TOP_K // BM + E
    n_act = blk_end[-1:].astype(jnp.int32)
    blk_ids = jnp.minimum(jnp.arange(n_blocks, dtype=jnp.int32), n_act[0] - 1)
    blk_e = jnp.minimum(jnp.sum(blk_end[None, :] <= blk_ids[:, None], axis=1), E - 1).astype(jnp.int32)
    pos_kt = _positions(e_idx, rank, row_start).reshape(TOP_K, T)
    pos = pos_kt.T.reshape(T * TOP_K)
    xs = _dispatch_sc(h2_packed, pos_kt, n_blocks * BM)
    ys = _experts(xs, blk_e, n_act, row_start, cnt, w_gate, w_up, w_down)
    gathered = _gather_rows_sc(ys, pos_kt[:SC_COMBINE_SLOTS])
    w_tk = top_w.T
    partial = _combine(h2, pos, w_tk, ys, ws_gate, ws_up, ws_down)
    return _finish(h2, partial, w_tk, gathered, ln_g, ln_b)


def kernel(x, ln_in_g, ln_in_b, w_in, w_out, rel_bias, attn_sinks, ln_mix_g, ln_mix_b, w_router,
           router_bias, w_gate, w_up, w_down, ws_gate, ws_up, ws_down, ln_ffn_g, ln_ffn_b):
    B, S, D = x.shape
    h, h_packed = _mixer(x, ln_in_g, ln_in_b, w_in[0], w_out[0], rel_bias, attn_sinks[0], ln_mix_g[0],
                         ln_mix_b[0])
    out = _moe(h, h_packed, w_router[0], router_bias[0], w_gate[0], w_up[0], w_down[0],
               ws_gate[0], ws_up[0], ws_down[0], ln_ffn_g[0], ln_ffn_b[0])
    return out.reshape(B, S, D)
```

```python
import functools
import math

import jax
import jax.numpy as jnp
from jax import lax
from jax.experimental import pallas as pl
from jax.experimental.pallas import tpu as pltpu
from jax.experimental.pallas import tpu_sc as plsc

D_MODEL = 1024
DEPTH = 1
RET_HEADS = 4
RET_QK_DIM = 64
RET_V_DIM = 128
RET_CHUNK = 128
RET_WIDTH = RET_HEADS * RET_V_DIM
ROPE_BASE = 10000.0
SWA_HEADS = 8
SWA_KV_HEADS = 2
SWA_GROUP = SWA_HEADS // SWA_KV_HEADS
SWA_HEAD_DIM = 64
SWA_WINDOW = 128
SWA_WIDTH = SWA_HEADS * SWA_HEAD_DIM
MIX_WIDTH = RET_WIDTH + SWA_WIDTH
RQK = RET_HEADS * RET_QK_DIM
SKV = SWA_KV_HEADS * SWA_HEAD_DIM
IN_SIZES = (RQK, RQK, RET_WIDTH, RET_WIDTH, SWA_WIDTH, SKV, SKV)
IN_OFFS = tuple(sum(IN_SIZES[:i]) for i in range(len(IN_SIZES)))
IN_WIDTH = sum(IN_SIZES)
REL_BUCKETS = 32
REL_MAX_DIST = 128
N_EXPERTS = 256
TOP_K = 8
N_GROUPS = 8
GROUP_SIZE = N_EXPERTS // N_GROUPS
TOPK_GROUPS = 4
EXPERT_DIM = 256
SHARED_DIM = 256
ROUTED_SCALE = 2.5
LN_EPS = 1e-5
GN_EPS = 1e-6
DEEPNORM_ALPHA = (2 * DEPTH) ** 0.25
MASK_VALUE = -1e30

VMEM_LIMIT_BYTES = 56 * 1024 * 1024

MIX_ROWS = 256
ROUTE_ROWS = 256
DISPATCH_ROWS = 256
EXPERT_ROWS = 1024
COMBINE_ROWS = 256
POSITION_ROWS = 64
FINISH_ROWS = 512


def _layer_norm(x, g, b):
    mu = jnp.mean(x, axis=-1, keepdims=True)
    xc = x - mu
    var = jnp.mean(xc * xc, axis=-1, keepdims=True)
    return xc * lax.rsqrt(var + LN_EPS) * g + b


def _dot(a, b):
    return jnp.dot(a, b, preferred_element_type=jnp.float32)


def _dot_nt(a, b):
    return lax.dot_general(a, b, (((1,), (1,)), ((), ())), preferred_element_type=jnp.float32)


def _dot_tn(a, b):
    return lax.dot_general(a, b, (((0,), (0,)), ((), ())), preferred_element_type=jnp.float32)


def _silu(x):
    return x * (1.0 / (1.0 + jnp.exp(-x)))


LANES = 128
ROW_TILE = D_MODEL // LANES


def _load_rows(ref, n_rows, lead=()):
    return jnp.concatenate([ref[lead + (pl.ds(s, n_rows, stride=ROW_TILE), slice(None))]
                            for s in range(ROW_TILE)], axis=1)


def _store_rows(ref, val, lead=(), first_row=0):
    n_rows = val.shape[0]
    for s in range(ROW_TILE):
        dst = pl.ds(first_row * ROW_TILE + s, n_rows, stride=ROW_TILE)
        ref[lead + (dst, slice(None))] = val[:, s * LANES:(s + 1) * LANES]


def _row_tile(r):
    return pl.ds(pl.multiple_of(r * ROW_TILE, ROW_TILE), ROW_TILE)


def _pack_bf16_pairs(x):
    m = x.shape[1] // 2
    bits = lax.bitcast_convert_type(x.astype(jnp.bfloat16).astype(jnp.float32), jnp.uint32)
    return (bits[:, :m] >> 16) | (bits[:, m:] & jnp.uint32(0xFFFF0000))


def _unpack_bf16_pairs(p):
    lo = lax.bitcast_convert_type(p << 16, jnp.float32)
    hi = lax.bitcast_convert_type(p & jnp.uint32(0xFFFF0000), jnp.float32)
    return jnp.concatenate([lo, hi], axis=1).astype(jnp.bfloat16)


def _swap_halves(x):
    n = x.shape[-1]
    half = RET_QK_DIM // 2
    lane = lax.broadcasted_iota(jnp.int32, x.shape, 1)
    from_right = pltpu.roll(x, n - half, axis=1)
    from_left = pltpu.roll(x, half, axis=1)
    return jnp.where((lane % RET_QK_DIM) < half, from_right, from_left)


def _mixer_kernel(rel_bias_ref, x_ref, g_in_ref, b_in_ref, w_in_ref, w_out_ref, rot_ref, decay_ref,
                  zeta_ref, xi_ref, cdecay_ref, bucket_ref, sink_ref, g_mix_ref, b_mix_ref,
                  h2_ref, h2p_ref, state_ref, kprev_ref, vprev_ref, bias_ref):
    b_id = pl.program_id(0)
    c_id = pl.program_id(1)
    W = SWA_WINDOW

    @pl.when((b_id == 0) & (c_id == 0))
    def _build_bias():
        bucket = bucket_ref[...]
        for h in range(SWA_HEADS):
            acc = jnp.full((2 * W, W), MASK_VALUE, jnp.float32)
            for b in range(REL_BUCKETS):
                acc = jnp.where(bucket == b, rel_bias_ref[b, h], acc)
            kh, g = divmod(h, SWA_GROUP)
            bias_ref[kh, :, g * W:(g + 1) * W] = acc

    @pl.when(c_id == 0)
    def _reset():
        state_ref[...] = jnp.zeros_like(state_ref)
        kprev_ref[...] = jnp.zeros_like(kprev_ref)
        vprev_ref[...] = jnp.zeros_like(vprev_ref)

    h = _layer_norm(x_ref[...], g_in_ref[...], b_in_ref[...])
    proj = _dot(h.astype(jnp.bfloat16), w_in_ref[...])

    o_q, o_k, o_v, o_g, o_sq, o_sk, o_sv = IN_OFFS
    cos_t = rot_ref[:, :RQK]
    sin_t = rot_ref[:, RQK:]
    q_all = proj[:, o_q:o_q + RQK]
    k_all = proj[:, o_k:o_k + RQK]
    q_rot = q_all * cos_t + _swap_halves(q_all) * sin_t
    k_rot = (k_all * cos_t + _swap_halves(k_all) * sin_t) * (RET_QK_DIM ** -0.5)

    n_sub = x_ref.shape[0] // RET_CHUNK
    states = [state_ref[hh] for hh in range(RET_HEADS)]
    k_prev = kprev_ref[...]
    v_prev = vprev_ref[...]
    bf16 = jnp.bfloat16
    heads = range(RET_HEADS)
    ret_pieces, swa_pieces = [], []
    swa_jobs = []
    for s in range(n_sub):
        rows = slice(s * RET_CHUNK, (s + 1) * RET_CHUNK)
        k_cur = proj[rows, o_sk:o_sk + SKV].astype(bf16)
        v_cur = proj[rows, o_sv:o_sv + SKV].astype(bf16)
        for kh in range(SWA_KV_HEADS):
            kv = slice(kh * SWA_HEAD_DIM, (kh + 1) * SWA_HEAD_DIM)
            q4 = jnp.concatenate(
                [proj[rows, o_sq + (kh * SWA_GROUP + g) * SWA_HEAD_DIM:
                      o_sq + (kh * SWA_GROUP + g + 1) * SWA_HEAD_DIM] for g in range(SWA_GROUP)],
                axis=0) * (SWA_HEAD_DIM ** -0.5)
            kcat = jnp.concatenate([k_prev[:, kv], k_cur[:, kv]], axis=0)
            vcat = jnp.concatenate([v_prev[:, kv], v_cur[:, kv]], axis=0)
            swa_jobs.append((s, kh, q4.astype(bf16), kcat, vcat))
        k_prev, v_prev = k_cur, v_cur
    logits_all = [_dot_nt(kcat, q4) + bias_ref[kh] for (s, kh, q4, kcat, vcat) in swa_jobs]
    probs_all = []
    for (s, kh, q4, kcat, vcat), logits in zip(swa_jobs, logits_all):
        if s == 0:
            key = lax.broadcasted_iota(jnp.int32, logits.shape, 0)
            logits = logits + jnp.where((key < W) & (c_id == 0), MASK_VALUE, 0.0)
        sink = sink_ref[kh]
        m = jnp.maximum(jnp.max(logits, axis=0, keepdims=True), sink)
        p = jnp.exp(logits - m)
        den = jnp.sum(p, axis=0, keepdims=True) + jnp.exp(sink - m)
        probs_all.append((p / den).astype(bf16))
    o4_all = [_dot_tn(job[4], probs) for job, probs in zip(swa_jobs, probs_all)]
    for s in range(n_sub):
        swa_pieces.append([o4[:, g * W:(g + 1) * W].T.astype(bf16)
                           for job, o4 in zip(swa_jobs, o4_all) if job[0] == s for g in range(SWA_GROUP)])
    for s in range(n_sub):
        rows = slice(s * RET_CHUNK, (s + 1) * RET_CHUNK)
        qk = [slice(hh * RET_QK_DIM, (hh + 1) * RET_QK_DIM) for hh in heads]
        q = [q_rot[rows, qk[hh]].astype(bf16) for hh in heads]
        k32 = [k_rot[rows, qk[hh]] for hh in heads]
        v = [proj[rows, o_v + hh * RET_V_DIM:o_v + (hh + 1) * RET_V_DIM].astype(bf16) for hh in heads]
        scores = [_dot_nt(q[hh], k32[hh].astype(bf16)) * decay_ref[hh] for hh in heads]
        inter = [_dot(q[hh], states[hh].astype(bf16)) * xi_ref[hh] for hh in heads]
        kv_new = [_dot_tn((k32[hh] * zeta_ref[hh]).astype(bf16), v[hh]) for hh in heads]
        intra = [_dot(scores[hh].astype(bf16), v[hh]) for hh in heads]
        states = [states[hh] * cdecay_ref[hh] + kv_new[hh] for hh in heads]
        pieces = []
        for hh in heads:
            ret = intra[hh] + inter[hh]
            mu = jnp.mean(ret, axis=-1, keepdims=True)
            rc = ret - mu
            var = jnp.mean(rc * rc, axis=-1, keepdims=True)
            normed = rc * lax.rsqrt(var + GN_EPS)
            gate = proj[rows, o_g + hh * RET_V_DIM:o_g + (hh + 1) * RET_V_DIM]
            pieces.append((_silu(gate) * normed).astype(bf16))
        ret_pieces.append(pieces)
    cat_rows = [jnp.concatenate(ret_pieces[s] + swa_pieces[s], axis=1) for s in range(n_sub)]
    for hh in range(RET_HEADS):
        state_ref[hh] = states[hh]
    kprev_ref[...] = k_prev
    vprev_ref[...] = v_prev

    mix = _dot(jnp.concatenate(cat_rows, axis=0), w_out_ref[...])
    h2 = _layer_norm(DEEPNORM_ALPHA * h + mix, g_mix_ref[...], b_mix_ref[...])
    h2_ref[...] = h2
    h2p_ref[...] = _pack_bf16_pairs(h2)


def _t5_bucket(dist):
    n = jnp.maximum(dist, 0)
    max_exact = REL_BUCKETS // 2
    ratio = jnp.log(jnp.maximum(n, 1).astype(jnp.float32) / max_exact) / math.log(REL_MAX_DIST / max_exact)
    large = jnp.minimum(max_exact + (ratio * (REL_BUCKETS - max_exact)).astype(jnp.int32), REL_BUCKETS - 1)
    return jnp.where(n < max_exact, n, large)


def _mixer(x, ln_in_g, ln_in_b, w_in, w_out, rel_bias, sinks, ln_mix_g, ln_mix_b):
    B, S, D = x.shape
    R = MIX_ROWS
    C = RET_CHUNK
    W = SWA_WINDOW
    f32 = jnp.float32
    half = RET_QK_DIM // 2
    inv = ROPE_BASE ** (-jnp.arange(half, dtype=f32) / half)
    ang = jnp.arange(S, dtype=f32)[:, None] * inv[None, :]
    cos, sin = jnp.cos(ang), jnp.sin(ang)
    cos_t = jnp.tile(jnp.concatenate([cos, cos], axis=-1), (1, RET_HEADS))
    sin_t = jnp.tile(jnp.concatenate([-sin, sin], axis=-1), (1, RET_HEADS))
    rot = jnp.concatenate([cos_t, sin_t], axis=-1)
    log_gamma = jnp.log(1.0 - 2.0 ** (-5.0 - jnp.arange(RET_HEADS, dtype=f32)))
    idx = jnp.arange(C, dtype=f32)
    diff = idx[:, None] - idx[None, :]
    decay = jnp.where(diff[None] >= 0, jnp.exp(jnp.maximum(diff, 0.0)[None] * log_gamma[:, None, None]), 0.0)
    zeta = jnp.exp((C - 1.0 - idx)[None, :] * log_gamma[:, None])
    xi = jnp.exp((idx + 1.0)[None, :] * log_gamma[:, None])
    zeta_b = jnp.broadcast_to(zeta[:, :, None], (RET_HEADS, C, RET_QK_DIM))
    xi_b = jnp.broadcast_to(xi[:, :, None], (RET_HEADS, C, RET_V_DIM))
    cdecay = jnp.broadcast_to(jnp.exp(C * log_gamma)[:, None, None], (RET_HEADS, RET_QK_DIM, RET_V_DIM))
    i = jnp.arange(W)
    j = jnp.arange(2 * W)
    dist = i[:, None] + W - j[None, :]
    bucket = jnp.where((dist >= 0) & (dist < W), _t5_bucket(dist), -1).astype(jnp.int32).T
    sink_row = jnp.repeat(sinks.astype(f32), W).reshape(SWA_KV_HEADS, 1, SWA_GROUP * W)

    const = lambda shape: pl.BlockSpec(shape, lambda b, c, *_: (0,) * len(shape))
    grid_spec = pltpu.PrefetchScalarGridSpec(
        num_scalar_prefetch=1,
        grid=(B, S // R),
        in_specs=[
            pl.BlockSpec((None, R, D), lambda b, c, *_: (b, c, 0)),
            const((1, D)), const((1, D)),
            const((D, IN_WIDTH)), const((MIX_WIDTH, D)),
            pl.BlockSpec((R, 2 * RQK), lambda b, c, *_: (c, 0)),
            const((RET_HEADS, C, C)), const((RET_HEADS, C, RET_QK_DIM)), const((RET_HEADS, C, RET_V_DIM)),
            const((RET_HEADS, RET_QK_DIM, RET_V_DIM)),
            const((2 * W, W)), const((SWA_KV_HEADS, 1, SWA_GROUP * W)),
            const((1, D)), const((1, D)),
        ],
        out_specs=[pl.BlockSpec((R, D), lambda b, c, *_: (b * (S // R) + c, 0)),
                   pl.BlockSpec((R, D // 2), lambda b, c, *_: (b * (S // R) + c, 0))],
        scratch_shapes=[
            pltpu.VMEM((RET_HEADS, RET_QK_DIM, RET_V_DIM), f32),
            pltpu.VMEM((W, SKV), jnp.bfloat16),
            pltpu.VMEM((W, SKV), jnp.bfloat16),
            pltpu.VMEM((SWA_KV_HEADS, 2 * W, SWA_GROUP * W), f32),
        ],
    )
    return pl.pallas_call(
        _mixer_kernel,
        grid_spec=grid_spec,
        out_shape=[jax.ShapeDtypeStruct((B * S, D), f32), jax.ShapeDtypeStruct((B * S, D // 2), jnp.uint32)],
        compiler_params=pltpu.CompilerParams(
            dimension_semantics=("arbitrary", "arbitrary"), vmem_limit_bytes=VMEM_LIMIT_BYTES),
    )(rel_bias.astype(f32), x, ln_in_g.reshape(1, D), ln_in_b.reshape(1, D),
      w_in.astype(jnp.bfloat16), w_out.astype(jnp.bfloat16), rot, decay, zeta_b, xi_b, cdecay,
      bucket, sink_row, ln_mix_g.reshape(1, D), ln_mix_b.reshape(1, D))


def _router_kernel(h_ref, wr_ref, rb_ref, e_ref, w_ref, rk_ref, cnt_ref, run_ref):
    f32 = jnp.float32
    R = h_ref.shape[0]
    E = N_EXPERTS
    neg = -jnp.inf

    @pl.when(pl.program_id(0) == 0)
    def _init():
        run_ref[...] = jnp.zeros_like(run_ref)

    logits = _dot_nt(wr_ref[...], h_ref[...].astype(jnp.bfloat16))
    scores = 1.0 / (1.0 + jnp.exp(-logits))
    choice = scores + rb_ref[...]
    eid = lax.broadcasted_iota(jnp.int32, (E, R), 0)

    def first_argmax(vals, ids, none):
        m = jnp.max(vals, axis=0, keepdims=True)
        idx = jnp.min(jnp.where(vals == m, ids, none), axis=0, keepdims=True)
        return m, idx

    gid = lax.broadcasted_iota(jnp.int32, (GROUP_SIZE, R), 0)
    groups, gscore = [], []
    for g in range(N_GROUPS):
        vals = choice[g * GROUP_SIZE:(g + 1) * GROUP_SIZE]
        m1, i1 = first_argmax(vals, gid, GROUP_SIZE)
        m2 = jnp.max(jnp.where(gid == i1, neg, vals), axis=0, keepdims=True)
        groups.append(vals)
        gscore.append(m1 + m2)
    kept = []
    for g in range(N_GROUPS):
        beaten = jnp.zeros((1, R), f32)
        for g2 in range(N_GROUPS):
            if g2 == g:
                continue
            ahead = (gscore[g2] > gscore[g]) | (gscore[g2] == gscore[g]) if g2 < g else gscore[g2] > gscore[g]
            beaten = beaten + jnp.where(ahead, 1.0, 0.0)
        kept.append(jnp.where(beaten < TOPK_GROUPS, groups[g], neg))
    masked = jnp.concatenate(kept, axis=0)

    idxs, wts = [], []
    picked = jnp.zeros((E, R), f32)
    for _ in range(TOP_K):
        _, idx = first_argmax(masked, eid, E)
        hit = eid == idx
        idxs.append(idx)
        wts.append(jnp.sum(jnp.where(hit, scores, 0.0), axis=0, keepdims=True))
        masked = jnp.where(hit, neg, masked)
        picked = jnp.where(hit, 1.0, picked)
    wsum = wts[0]
    for k in range(1, TOP_K):
        wsum = wsum + wts[k]

    row = lax.broadcasted_iota(jnp.int32, (R, R), 0)
    col = lax.broadcasted_iota(jnp.int32, (R, R), 1)
    earlier = jnp.where(row < col, 1.0, 0.0).astype(jnp.bfloat16)
    picked_bf = picked.astype(jnp.bfloat16)
    run = run_ref[...]
    before = _dot(picked_bf, earlier) + jnp.concatenate([run] * (R // LANES), axis=1)
    sub_k = lax.broadcasted_iota(jnp.int32, (TOP_K, R), 0)
    e_out = jnp.zeros((TOP_K, R), jnp.int32)
    w_out = jnp.zeros((TOP_K, R), f32)
    rk_out = jnp.zeros((TOP_K, R), jnp.int32)
    for k in range(TOP_K):
        rank_k = jnp.sum(jnp.where(eid == idxs[k], before, 0.0), axis=0, keepdims=True)
        e_out = jnp.where(sub_k == k, idxs[k], e_out)
        w_out = jnp.where(sub_k == k, wts[k] / wsum * ROUTED_SCALE, w_out)
        rk_out = jnp.where(sub_k == k, rank_k.astype(jnp.int32), rk_out)
    e_ref[...] = e_out
    w_ref[...] = w_out
    rk_ref[...] = rk_out
    run_ref[...] = run + _dot(picked_bf, jnp.ones((R, LANES), jnp.bfloat16))
    cnt_ref[...] = run_ref[...]


def _router(h2, w_router, router_bias):
    T, D = h2.shape
    R = ROUTE_ROWS
    E = N_EXPERTS
    return pl.pallas_call(
        _router_kernel,
        grid=(T // R,),
        in_specs=[
            pl.BlockSpec((R, D), lambda i: (i, 0)),
            pl.BlockSpec((E, D), lambda i: (0, 0)),
            pl.BlockSpec((E, R), lambda i: (0, 0)),
        ],
        out_specs=[
            pl.BlockSpec((TOP_K, R), lambda i: (0, i)),
            pl.BlockSpec((TOP_K, R), lambda i: (0, i)),
            pl.BlockSpec((TOP_K, R), lambda i: (0, i)),
            pl.BlockSpec((E, LANES), lambda i: (0, 0)),
        ],
        out_shape=[
            jax.ShapeDtypeStruct((TOP_K, T), jnp.int32),
            jax.ShapeDtypeStruct((TOP_K, T), jnp.float32),
            jax.ShapeDtypeStruct((TOP_K, T), jnp.int32),
            jax.ShapeDtypeStruct((E, LANES), jnp.float32),
        ],
        scratch_shapes=[pltpu.VMEM((E, LANES), jnp.float32)],
        compiler_params=pltpu.CompilerParams(
            dimension_semantics=("arbitrary",), vmem_limit_bytes=VMEM_LIMIT_BYTES),
    )(h2, w_router.T.astype(jnp.bfloat16), jnp.broadcast_to(router_bias.astype(jnp.float32)[:, None], (E, R)))


def _positions_kernel(row_start_ref, e_ref, rk_ref, pos_ref):
    e = e_ref[...]
    rk = rk_ref[...]

    def per_expert(i, pos):
        return jnp.where(e == i, rk + row_start_ref[i], pos)

    pos_ref[...] = lax.fori_loop(0, N_EXPERTS, per_expert, jnp.zeros_like(rk))


def _positions(e_idx, rank, row_start):
    n = e_idx.size
    shape = (n // LANES, LANES)
    block = pl.BlockSpec((POSITION_ROWS, LANES), lambda i, *_: (i, 0))
    grid_spec = pltpu.PrefetchScalarGridSpec(
        num_scalar_prefetch=1,
        grid=(shape[0] // POSITION_ROWS,),
        in_specs=[block, block],
        out_specs=block,
    )
    pos = pl.pallas_call(
        _positions_kernel,
        grid_spec=grid_spec,
        out_shape=jax.ShapeDtypeStruct(shape, jnp.int32),
    )(row_start, e_idx.reshape(shape), rank.reshape(shape))
    return pos.reshape(n)


def _dispatch_kernel(row_start_ref, cnt_ref, n_act_ref, pos_ref, h_ref, xs_ref, zero_ref, sem, zsem,
                     *, n_tokens):
    R = h_ref.shape[0] // ROW_TILE
    BM = EXPERT_ROWS
    n_blocks = xs_ref.shape[0] // (BM * ROW_TILE)
    n_pad_units = n_blocks - n_tokens * TOP_K // BM

    @pl.when(pl.program_id(0) == 0)
    def _zero_padding():
        zero_ref[...] = jnp.zeros_like(zero_ref)

        def expert_tail(e, carry):
            n_tail = pl.multiple_of(((BM - cnt_ref[e] % BM) % BM) * ROW_TILE, ROW_TILE)

            @pl.when(n_tail > 0)
            def _():
                dst = pl.multiple_of((row_start_ref[e] + cnt_ref[e]) * ROW_TILE, ROW_TILE)
                pltpu.make_async_copy(zero_ref.at[pl.ds(0, n_tail)], xs_ref.at[pl.ds(dst, n_tail)], zsem).start()
            return carry

        lax.fori_loop(0, N_EXPERTS, expert_tail, 0)

        def idle_block(i, carry):
            dst = pl.multiple_of(i * (BM * ROW_TILE), BM * ROW_TILE)
            pltpu.make_async_copy(zero_ref, xs_ref.at[pl.ds(dst, BM * ROW_TILE)], zsem).start()
            return carry

        lax.fori_loop(n_act_ref[0], n_blocks, idle_block, 0)

        def drain(i, carry):
            pltpu.make_async_copy(zero_ref, xs_ref.at[pl.ds(0, BM * ROW_TILE)], zsem).wait()
            return carry

        lax.fori_loop(0, n_pad_units, drain, 0)

    def issue(t, carry):
        src = h_ref.at[_row_tile(t)]
        for k in range(TOP_K):
            dest = pos_ref[t * TOP_K + k]
            pltpu.make_async_copy(src, xs_ref.at[_row_tile(dest)], sem).start(priority=k % 2)
        return carry

    lax.fori_loop(0, R, issue, 0)
    n = R * TOP_K * ROW_TILE
    pltpu.make_async_copy(xs_ref.at[pl.ds(0, n)], xs_ref.at[pl.ds(0, n)], sem).wait()


def _dispatch(h2, pos, row_start, cnt, n_act, n_rows):
    T = h2.shape[0] // ROW_TILE
    R = DISPATCH_ROWS
    grid_spec = pltpu.PrefetchScalarGridSpec(
        num_scalar_prefetch=3,
        grid=(T // R,),
        in_specs=[
            pl.BlockSpec((R * TOP_K,), lambda i, *_: (i,), memory_space=pltpu.SMEM),
            pl.BlockSpec((R * ROW_TILE, LANES), lambda i, *_: (i, 0)),
        ],
        out_specs=pl.BlockSpec(memory_space=pl.ANY),
        scratch_shapes=[pltpu.VMEM((EXPERT_ROWS * ROW_TILE, LANES), jnp.float32),
                        pltpu.SemaphoreType.DMA(()), pltpu.SemaphoreType.DMA(())],
    )
    return pl.pallas_call(
        functools.partial(_dispatch_kernel, n_tokens=T),
        grid_spec=grid_spec,
        out_shape=jax.ShapeDtypeStruct((n_rows * ROW_TILE, LANES), jnp.float32),
        compiler_params=pltpu.CompilerParams(
            dimension_semantics=("arbitrary",), vmem_limit_bytes=VMEM_LIMIT_BYTES),
    )(row_start, cnt, n_act, pos, h2)


SC_WINDOW = 32


def _dispatch_sc(h2, pos_kt, n_rows):
    T, D = h2.shape
    W = SC_WINDOW
    idx = _window_indices(pos_kt, W)
    idx_rows = TOP_K * W // LANES
    mesh = plsc.VectorSubcoreMesh(core_axis_name="core", subcore_axis_name="subcore")

    @pl.kernel(out_type=jax.ShapeDtypeStruct((n_rows, D), h2.dtype), mesh=mesh, scratch_types=[])
    def scatter_rows(x_hbm, i_hbm, o_hbm):
        def body(x_vmem, i_vmem):
            for k in range(TOP_K):
                r, q = divmod(k * W, LANES)
                pltpu.sync_copy(x_vmem, o_hbm.at[i_vmem.at[r, pl.ds(q, W)]])

        pltpu.emit_pipeline(
            body,
            grid=(T // W,),
            in_specs=[pl.BlockSpec((W, D), lambda i: (i, 0)),
                      pl.BlockSpec((idx_rows, LANES), lambda i: (i, 0))],
            out_specs=[],
            core_axis_name=("core", "subcore"),
            dimension_semantics=(pltpu.PARALLEL,),
        )(x_hbm, i_hbm)

    return scatter_rows(h2, idx)


def _window_indices(pos_kt, window):
    K, T = pos_kt.shape
    return pos_kt.reshape(K, T // window, window).transpose(1, 0, 2).reshape(T * K // LANES, LANES)


def _gather_rows_sc(ys, pos_kt):
    n = pos_kt.size
    W = SC_WINDOW
    ys3 = ys.reshape(ys.shape[0] // ROW_TILE, ROW_TILE, LANES)
    idx = jnp.pad(pos_kt.reshape(n // W, W), ((0, 0), (0, LANES - W)))
    mesh = plsc.VectorSubcoreMesh(core_axis_name="core", subcore_axis_name="subcore")

    @pl.kernel(out_type=jax.ShapeDtypeStruct((n, ROW_TILE, LANES), ys.dtype), mesh=mesh, scratch_types=[])
    def gather_rows(y_hbm, i_hbm, o_hbm):
        def body(i_vmem, o_vmem):
            pltpu.sync_copy(y_hbm.at[i_vmem.at[0, pl.ds(0, W)]], o_vmem)

        pltpu.emit_pipeline(
            body,
            grid=(n // W,),
            in_specs=[pl.BlockSpec((1, LANES), lambda i: (i, 0))],
            out_specs=[pl.BlockSpec((W, ROW_TILE, LANES), lambda i: (i, 0, 0))],
            core_axis_name=("core", "subcore"),
            dimension_semantics=(pltpu.PARALLEL,),
        )(i_hbm, o_hbm)

    return gather_rows(ys3, idx).reshape(n * ROW_TILE, LANES)


EXPERT_SUBROWS = 128
X_SLOTS = 4
Y_SLOTS = 3


def _experts_kernel(blk_e_ref, first_ref, slot_ref, next_e_ref, valid_ref, n_act_ref, xs_hbm, wg_hbm,
                    wu_hbm, wd_hbm, ys_hbm, x_buf, y_buf, wg_buf, wu_buf, wd_buf, wg_bf, wu_bf, wd_bf,
                    sems, x_sems, y_sems):
    i = pl.program_id(0)
    n_act = n_act_ref[0]
    bf16 = jnp.bfloat16
    blk = EXPERT_ROWS * ROW_TILE

    def x_copy(j):
        n = pl.multiple_of((valid_ref[j] + 7) // 8 * 8, 8)
        src = xs_hbm.at[pl.ds(pl.multiple_of(j * EXPERT_ROWS, EXPERT_ROWS), n)]
        return pltpu.make_async_copy(src, x_buf.at[j % X_SLOTS, pl.ds(0, n)], x_sems.at[j % X_SLOTS])

    def y_copy(j):
        n = pl.multiple_of(valid_ref[j] * ROW_TILE, ROW_TILE)
        dst = ys_hbm.at[pl.ds(pl.multiple_of(j * blk, blk), n)]
        return pltpu.make_async_copy(y_buf.at[j % Y_SLOTS, pl.ds(0, n)], dst, y_sems.at[j % Y_SLOTS])

    @pl.when(i == 0)
    def _prime():
        x_buf[...] = jnp.zeros_like(x_buf)
        for j in range(X_SLOTS - 1):
            @pl.when(j < n_act)
            def _():
                x_copy(j).start()

    @pl.when(i + (X_SLOTS - 1) < n_act)
    def _prefetch():
        x_copy(i + (X_SLOTS - 1)).start()

    def weight_copies(e, slot):
        return (pltpu.make_async_copy(wg_hbm.at[e], wg_buf.at[slot], sems.at[slot]),
                pltpu.make_async_copy(wu_hbm.at[e], wu_buf.at[slot], sems.at[slot]),
                pltpu.make_async_copy(wd_hbm.at[e], wd_buf.at[slot], sems.at[slot]))

    @pl.when((i < n_act_ref[0]) & (first_ref[i] == 1))
    def _new_expert():
        slot = slot_ref[i]

        @pl.when(i == 0)
        def _():
            for c in weight_copies(blk_e_ref[0], 0):
                c.start()

        for c in weight_copies(blk_e_ref[i], slot):
            c.wait()

        @pl.when(next_e_ref[i] >= 0)
        def _():
            for c in weight_copies(next_e_ref[i], 1 - slot):
                c.start()

        wg_bf[...] = wg_buf[slot].astype(bf16)
        wu_bf[...] = wu_buf[slot].astype(bf16)
        wd_bf[...] = wd_buf[slot].astype(bf16)

    @pl.when(i < n_act)
    def _compute():
        x_copy(i).wait()

        @pl.when(i >= Y_SLOTS)
        def _():
            y_copy(i - Y_SLOTS).wait()

        sub = EXPERT_SUBROWS
        n_parts = (valid_ref[i] + sub - 1) // sub

        def run(parts):
            xs_ = [_unpack_bf16_pairs(x_buf[i % X_SLOTS, pl.ds(part * sub, sub), :]) for part in range(parts)]
            gs = [_dot(x, wg_bf[...]) for x in xs_]
            us = [_dot(x, wu_bf[...]) for x in xs_]
            acts = [(_silu(g) * u).astype(bf16) for g, u in zip(gs, us)]
            for part in range(parts):
                _store_rows(y_buf, _dot(acts[part], wd_bf[...]), lead=(i % Y_SLOTS,), first_row=part * sub)

        for parts in range(1, EXPERT_ROWS // sub + 1):
            pl.when(n_parts == parts)(functools.partial(run, parts))
        y_copy(i).start()

    @pl.when(i == n_act - 1)
    def _drain():
        for d in range(Y_SLOTS):
            @pl.when(i - d >= 0)
            def _():
                y_copy(i - d).wait()


def _experts(xs, blk_e, n_act, row_start, cnt, w_gate, w_up, w_down):
    D = D_MODEL
    BM = EXPERT_ROWS
    F = EXPERT_DIM
    n_blocks = xs.shape[0] // BM
    blk_in_expert = jnp.arange(n_blocks, dtype=jnp.int32) - row_start[blk_e] // BM
    valid = jnp.clip(cnt[blk_e] - blk_in_expert * BM, 0, BM).astype(jnp.int32)
    ids = jnp.arange(n_blocks, dtype=jnp.int32)
    active = ids < n_act[0]
    first = active & ((ids == 0) | (blk_e != jnp.roll(blk_e, 1)))
    slot = ((jnp.cumsum(first.astype(jnp.int32)) - 1) % 2).astype(jnp.int32)
    first_pos = jnp.where(first, ids, n_blocks)
    later_first = lax.cummin(jnp.concatenate([first_pos[1:], jnp.full((1,), n_blocks, jnp.int32)]), reverse=True)
    next_e = jnp.where(later_first < n_blocks, blk_e[jnp.minimum(later_first, n_blocks - 1)], -1).astype(jnp.int32)

    grid_spec = pltpu.PrefetchScalarGridSpec(
        num_scalar_prefetch=6,
        grid=(n_blocks,),
        in_specs=[pl.BlockSpec(memory_space=pl.ANY)] * 4,
        out_specs=pl.BlockSpec(memory_space=pl.ANY),
        scratch_shapes=[
            pltpu.VMEM((X_SLOTS, BM, D // 2), jnp.uint32),
            pltpu.VMEM((Y_SLOTS, BM * ROW_TILE, LANES), jnp.float32),
            pltpu.VMEM((2, D, F), jnp.float32), pltpu.VMEM((2, D, F), jnp.float32),
            pltpu.VMEM((2, F, D), jnp.float32),
            pltpu.VMEM((D, F), jnp.bfloat16), pltpu.VMEM((D, F), jnp.bfloat16),
            pltpu.VMEM((F, D), jnp.bfloat16),
            pltpu.SemaphoreType.DMA((2,)), pltpu.SemaphoreType.DMA((X_SLOTS,)),
            pltpu.SemaphoreType.DMA((Y_SLOTS,)),
        ],
    )
    return pl.pallas_call(
        _experts_kernel,
        grid_spec=grid_spec,
        out_shape=jax.ShapeDtypeStruct((xs.shape[0] * ROW_TILE, LANES), jnp.float32),
        compiler_params=pltpu.CompilerParams(
            dimension_semantics=("arbitrary",), vmem_limit_bytes=VMEM_LIMIT_BYTES),
    )(blk_e, first.astype(jnp.int32), slot, next_e, valid, n_act, xs, w_gate, w_up, w_down)


def _finish_kernel(h_ref, part_ref, w_ref, *rest):
    slabs = rest[:SC_COMBINE_SLOTS]
    g_ref, b_ref, out_ref = rest[SC_COMBINE_SLOTS:]
    R = h_ref.shape[0]
    w = w_ref[...]
    ffn = part_ref[...]
    for k in range(SC_COMBINE_SLOTS):
        ffn = ffn + _load_rows(slabs[k], R) * w[:, k:k + 1]
    out_ref[...] = _layer_norm(DEEPNORM_ALPHA * h_ref[...] + ffn, g_ref[...], b_ref[...])


def _finish(h2, partial, top_w, gathered, ln_g, ln_b):
    T, D = h2.shape
    R = FINISH_ROWS
    rows = pl.BlockSpec((R, D), lambda i: (i, 0))
    vec = pl.BlockSpec((1, D), lambda i: (0, 0))
    slab = lambda k: pl.BlockSpec((R * ROW_TILE, LANES), lambda i: (k * (T // R) + i, 0))
    return pl.pallas_call(
        _finish_kernel,
        grid=(T // R,),
        in_specs=[rows, rows, pl.BlockSpec((R, TOP_K), lambda i: (i, 0))]
        + [slab(k) for k in range(SC_COMBINE_SLOTS)] + [vec, vec],
        out_specs=rows,
        out_shape=jax.ShapeDtypeStruct((T, D), jnp.float32),
        compiler_params=pltpu.CompilerParams(
            dimension_semantics=("arbitrary",), vmem_limit_bytes=VMEM_LIMIT_BYTES),
    )(h2, partial, top_w, *([gathered] * SC_COMBINE_SLOTS), ln_g.reshape(1, D), ln_b.reshape(1, D))


SC_COMBINE_SLOTS = 5


def _combine_kernel(pos_ref, pos_next_ref, h_ref, w_ref, ys_ref, wsg_ref, wsu_ref, wsd_ref,
                    out_ref, buf_ref, sems):
    R = h_ref.shape[0]
    i = pl.program_id(0)
    slot = i % 2

    def gather(p_ref, s):
        def issue(t, carry):
            for k in range(SC_COMBINE_SLOTS, TOP_K):
                src = p_ref[t * TOP_K + k]
                pltpu.make_async_copy(ys_ref.at[_row_tile(src)], buf_ref.at[s, k - SC_COMBINE_SLOTS, _row_tile(t)],
                                      sems.at[s]).start(priority=k % 2)
            return carry

        lax.fori_loop(0, R, issue, 0)

    @pl.when(i == 0)
    def _():
        gather(pos_ref, 0)

    @pl.when(i + 1 < pl.num_programs(0))
    def _():
        gather(pos_next_ref, 1 - slot)

    h = h_ref[...]
    hb = h.astype(jnp.bfloat16)
    act = (_silu(_dot(hb, wsg_ref[...])) * _dot(hb, wsu_ref[...])).astype(jnp.bfloat16)
    ffn = _dot(act, wsd_ref[...])
    pltpu.make_async_copy(buf_ref.at[slot], buf_ref.at[slot], sems.at[slot]).wait()
    w = w_ref[...]
    for k in range(SC_COMBINE_SLOTS, TOP_K):
        ffn = ffn + _load_rows(buf_ref, R, lead=(slot, k - SC_COMBINE_SLOTS)) * w[:, k:k + 1]
    out_ref[...] = ffn


def _combine(h2, pos, top_w, ys, ws_gate, ws_up, ws_down):
    T, D = h2.shape
    R = COMBINE_ROWS
    F = SHARED_DIM
    bf16 = jnp.bfloat16
    const = lambda shape: pl.BlockSpec(shape, lambda i: (0,) * len(shape))
    return pl.pallas_call(
        _combine_kernel,
        grid=(T // R,),
        in_specs=[
            pl.BlockSpec((R * TOP_K,), lambda i: (i,), memory_space=pltpu.SMEM),
            pl.BlockSpec((R * TOP_K,), lambda i: (jnp.minimum(i + 1, T // R - 1),), memory_space=pltpu.SMEM),
            pl.BlockSpec((R, D), lambda i: (i, 0)),
            pl.BlockSpec((R, TOP_K), lambda i: (i, 0)),
            pl.BlockSpec(memory_space=pl.ANY),
            const((D, F)), const((D, F)), const((F, D)),
        ],
        out_specs=pl.BlockSpec((R, D), lambda i: (i, 0)),
        scratch_shapes=[pltpu.VMEM((2, TOP_K - SC_COMBINE_SLOTS, R * ROW_TILE, LANES), jnp.float32),
                        pltpu.SemaphoreType.DMA((2,))],
        out_shape=jax.ShapeDtypeStruct((T, D), jnp.float32),
        compiler_params=pltpu.CompilerParams(
            dimension_semantics=("arbitrary",), vmem_limit_bytes=VMEM_LIMIT_BYTES),
    )(pos, pos, h2, top_w, ys, ws_gate.astype(bf16), ws_up.astype(bf16), ws_down.astype(bf16))


def _moe(h2, h2_packed, w_router, router_bias, w_gate, w_up, w_down, ws_gate, ws_up, ws_down, ln_g, ln_b):
    T = h2.shape[0]
    E = N_EXPERTS
    BM = EXPERT_ROWS
    e_idx, top_w, rank, counts = _router(h2, w_router, router_bias)
    cnt = counts[:, 0].astype(jnp.int32)
    nblk = (cnt + BM - 1) // BM
    blk_end = jnp.cumsum(nblk)
    row_start = ((blk_end - nblk) * BM).astype(jnp.int32)
    n_blocks = T * TOP_K // BM + E
    n_act = blk_end[-1:].astype(jnp.int32)
    blk_ids = jnp.minimum(jnp.arange(n_blocks, dtype=jnp.int32), n_act[0] - 1)
    blk_e = jnp.minimum(jnp.sum(blk_end[None, :] <= blk_ids[:, None], axis=1), E - 1).astype(jnp.int32)
    pos_kt = _positions(e_idx, rank, row_start).reshape(TOP_K, T)
    pos = pos_kt.T.reshape(T * TOP_K)
    xs = _dispatch_sc(h2_packed, pos_kt, n_blocks * BM)
    ys = _experts(xs, blk_e, n_act, row_start, cnt, w_gate, w_up, w_down)
    gathered = _gather_rows_sc(ys, pos_kt[:SC_COMBINE_SLOTS])
    w_tk = top_w.T
    partial = _combine(h2, pos, w_tk, ys, ws_gate, ws_up, ws_down)
    return _finish(h2, partial, w_tk, gathered, ln_g, ln_b)


def kernel(x, ln_in_g, ln_in_b, w_in, w_out, rel_bias, attn_sinks, ln_mix_g, ln_mix_b, w_router,
           router_bias, w_gate, w_up, w_down, ws_gate, ws_up, ws_down, ln_ffn_g, ln_ffn_b):
    B, S, D = x.shape
    h, h_packed = _mixer(x, ln_in_g, ln_in_b, w_in[0], w_out[0], rel_bias, attn_sinks[0], ln_mix_g[0],
                         ln_mix_b[0])
    out = _moe(h, h_packed, w_router[0], router_bias[0], w_gate[0], w_up[0], w_down[0],
               ws_gate[0], ws_up[0], ws_down[0], ln_ffn_g[0], ln_ffn_b[0])
    return out.reshape(B, S, D)
```

```python
import functools
import math

import jax
import jax.numpy as jnp
from jax import lax
from jax.experimental import pallas as pl
from jax.experimental.pallas import tpu as pltpu
from jax.experimental.pallas import tpu_sc as plsc

D_MODEL = 1024
DEPTH = 1
RET_HEADS = 4
RET_QK_DIM = 64
RET_V_DIM = 128
RET_CHUNK = 128
RET_WIDTH = RET_HEADS * RET_V_DIM
ROPE_BASE = 10000.0
SWA_HEADS = 8
SWA_KV_HEADS = 2
SWA_GROUP = SWA_HEADS // SWA_KV_HEADS
SWA_HEAD_DIM = 64
SWA_WINDOW = 128
SWA_WIDTH = SWA_HEADS * SWA_HEAD_DIM
MIX_WIDTH = RET_WIDTH + SWA_WIDTH
RQK = RET_HEADS * RET_QK_DIM
SKV = SWA_KV_HEADS * SWA_HEAD_DIM
IN_SIZES = (RQK, RQK, RET_WIDTH, RET_WIDTH, SWA_WIDTH, SKV, SKV)
IN_OFFS = tuple(sum(IN_SIZES[:i]) for i in range(len(IN_SIZES)))
IN_WIDTH = sum(IN_SIZES)
REL_BUCKETS = 32
REL_MAX_DIST = 128
N_EXPERTS = 256
TOP_K = 8
N_GROUPS = 8
GROUP_SIZE = N_EXPERTS // N_GROUPS
TOPK_GROUPS = 4
EXPERT_DIM = 256
SHARED_DIM = 256
ROUTED_SCALE = 2.5
LN_EPS = 1e-5
GN_EPS = 1e-6
DEEPNORM_ALPHA = (2 * DEPTH) ** 0.25
MASK_VALUE = -1e30

VMEM_LIMIT_BYTES = 56 * 1024 * 1024

MIX_ROWS = 256
ROUTE_ROWS = 256
EXPERT_ROWS = 512
COMBINE_ROWS = 256
POSITION_ROWS = 64
FINISH_ROWS = 512


def _layer_norm(x, g, b):
    mu = jnp.mean(x, axis=-1, keepdims=True)
    xc = x - mu
    var = jnp.mean(xc * xc, axis=-1, keepdims=True)
    return xc * lax.rsqrt(var + LN_EPS) * g + b


def _dot(a, b):
    return jnp.dot(a, b, preferred_element_type=jnp.float32)


def _dot_nt(a, b):
    return lax.dot_general(a, b, (((1,), (1,)), ((), ())), preferred_element_type=jnp.float32)


def _dot_tn(a, b):
    return lax.dot_general(a, b, (((0,), (0,)), ((), ())), preferred_element_type=jnp.float32)


def _silu(x):
    return x * (1.0 / (1.0 + jnp.exp(-x)))


LANES = 128
ROW_TILE = D_MODEL // LANES


def _load_rows(ref, n_rows, lead=()):
    return jnp.concatenate([ref[lead + (pl.ds(s, n_rows, stride=ROW_TILE), slice(None))]
                            for s in range(ROW_TILE)], axis=1)


def _store_rows(ref, val, lead=(), first_row=0):
    n_rows = val.shape[0]
    for s in range(ROW_TILE):
        dst = pl.ds(first_row * ROW_TILE + s, n_rows, stride=ROW_TILE)
        ref[lead + (dst, slice(None))] = val[:, s * LANES:(s + 1) * LANES]


def _row_tile(r):
    return pl.ds(pl.multiple_of(r * ROW_TILE, ROW_TILE), ROW_TILE)


def _pack_bf16_pairs(x):
    m = x.shape[1] // 2
    bits = lax.bitcast_convert_type(x.astype(jnp.bfloat16).astype(jnp.float32), jnp.uint32)
    return (bits[:, :m] >> 16) | (bits[:, m:] & jnp.uint32(0xFFFF0000))


def _unpack_bf16_pairs(p):
    lo = lax.bitcast_convert_type(p << 16, jnp.float32)
    hi = lax.bitcast_convert_type(p & jnp.uint32(0xFFFF0000), jnp.float32)
    return jnp.concatenate([lo, hi], axis=1).astype(jnp.bfloat16)


def _swap_halves(x):
    n = x.shape[-1]
    half = RET_QK_DIM // 2
    lane = lax.broadcasted_iota(jnp.int32, x.shape, 1)
    from_right = pltpu.roll(x, n - half, axis=1)
    from_left = pltpu.roll(x, half, axis=1)
    return jnp.where((lane % RET_QK_DIM) < half, from_right, from_left)


def _mixer_kernel(rel_bias_ref, x_ref, g_in_ref, b_in_ref, w_in_ref, w_out_ref, rot_ref, decay_ref,
                  zeta_ref, xi_ref, cdecay_ref, bucket_ref, sink_ref, g_mix_ref, b_mix_ref,
                  h2_ref, h2p_ref, state_ref, kprev_ref, vprev_ref, bias_ref):
    b_id = pl.program_id(0)
    c_id = pl.program_id(1)
    W = SWA_WINDOW

    @pl.when((b_id == 0) & (c_id == 0))
    def _build_bias():
        bucket = bucket_ref[...]
        for h in range(SWA_HEADS):
            acc = jnp.full((2 * W, W), MASK_VALUE, jnp.float32)
            for b in range(REL_BUCKETS):
                acc = jnp.where(bucket == b, rel_bias_ref[b, h], acc)
            kh, g = divmod(h, SWA_GROUP)
            bias_ref[kh, :, g * W:(g + 1) * W] = acc

    @pl.when(c_id == 0)
    def _reset():
        state_ref[...] = jnp.zeros_like(state_ref)
        kprev_ref[...] = jnp.zeros_like(kprev_ref)
        vprev_ref[...] = jnp.zeros_like(vprev_ref)

    h = _layer_norm(x_ref[...], g_in_ref[...], b_in_ref[...])
    proj = _dot(h.astype(jnp.bfloat16), w_in_ref[...])

    o_q, o_k, o_v, o_g, o_sq, o_sk, o_sv = IN_OFFS
    cos_t = rot_ref[:, :RQK]
    sin_t = rot_ref[:, RQK:]
    q_all = proj[:, o_q:o_q + RQK]
    k_all = proj[:, o_k:o_k + RQK]
    q_rot = q_all * cos_t + _swap_halves(q_all) * sin_t
    k_rot = (k_all * cos_t + _swap_halves(k_all) * sin_t) * (RET_QK_DIM ** -0.5)

    n_sub = x_ref.shape[0] // RET_CHUNK
    states = [state_ref[hh] for hh in range(RET_HEADS)]
    k_prev = kprev_ref[...]
    v_prev = vprev_ref[...]
    bf16 = jnp.bfloat16
    heads = range(RET_HEADS)
    ret_pieces, swa_pieces = [], []
    swa_jobs = []
    for s in range(n_sub):
        rows = slice(s * RET_CHUNK, (s + 1) * RET_CHUNK)
        k_cur = proj[rows, o_sk:o_sk + SKV].astype(bf16)
        v_cur = proj[rows, o_sv:o_sv + SKV].astype(bf16)
        for kh in range(SWA_KV_HEADS):
            kv = slice(kh * SWA_HEAD_DIM, (kh + 1) * SWA_HEAD_DIM)
            q4 = jnp.concatenate(
                [proj[rows, o_sq + (kh * SWA_GROUP + g) * SWA_HEAD_DIM:
                      o_sq + (kh * SWA_GROUP + g + 1) * SWA_HEAD_DIM] for g in range(SWA_GROUP)],
                axis=0) * (SWA_HEAD_DIM ** -0.5)
            kcat = jnp.concatenate([k_prev[:, kv], k_cur[:, kv]], axis=0)
            vcat = jnp.concatenate([v_prev[:, kv], v_cur[:, kv]], axis=0)
            swa_jobs.append((s, kh, q4.astype(bf16), kcat, vcat))
        k_prev, v_prev = k_cur, v_cur
    logits_all = [_dot_nt(kcat, q4) + bias_ref[kh] for (s, kh, q4, kcat, vcat) in swa_jobs]
    probs_all = []
    for (s, kh, q4, kcat, vcat), logits in zip(swa_jobs, logits_all):
        if s == 0:
            key = lax.broadcasted_iota(jnp.int32, logits.shape, 0)
            logits = logits + jnp.where((key < W) & (c_id == 0), MASK_VALUE, 0.0)
        sink = sink_ref[kh]
        m = jnp.maximum(jnp.max(logits, axis=0, keepdims=True), sink)
        p = jnp.exp(logits - m)
        den = jnp.sum(p, axis=0, keepdims=True) + jnp.exp(sink - m)
        probs_all.append((p / den).astype(bf16))
    o4_all = [_dot_tn(job[4], probs) for job, probs in zip(swa_jobs, probs_all)]
    for s in range(n_sub):
        swa_pieces.append([o4[:, g * W:(g + 1) * W].T.astype(bf16)
                           for job, o4 in zip(swa_jobs, o4_all) if job[0] == s for g in range(SWA_GROUP)])
    for s in range(n_sub):
        rows = slice(s * RET_CHUNK, (s + 1) * RET_CHUNK)
        qk = [slice(hh * RET_QK_DIM, (hh + 1) * RET_QK_DIM) for hh in heads]
        q = [q_rot[rows, qk[hh]].astype(bf16) for hh in heads]
        k32 = [k_rot[rows, qk[hh]] for hh in heads]
        v = [proj[rows, o_v + hh * RET_V_DIM:o_v + (hh + 1) * RET_V_DIM].astype(bf16) for hh in heads]
        scores = [_dot_nt(q[hh], k32[hh].astype(bf16)) * decay_ref[hh] for hh in heads]
        inter = [_dot(q[hh], states[hh].astype(bf16)) * xi_ref[hh] for hh in heads]
        kv_new = [_dot_tn((k32[hh] * zeta_ref[hh]).astype(bf16), v[hh]) for hh in heads]
        intra = [_dot(scores[hh].astype(bf16), v[hh]) for hh in heads]
        states = [states[hh] * cdecay_ref[hh] + kv_new[hh] for hh in heads]
        pieces = []
        for hh in heads:
            ret = intra[hh] + inter[hh]
            mu = jnp.mean(ret, axis=-1, keepdims=True)
            rc = ret - mu
            var = jnp.mean(rc * rc, axis=-1, keepdims=True)
            normed = rc * lax.rsqrt(var + GN_EPS)
            gate = proj[rows, o_g + hh * RET_V_DIM:o_g + (hh + 1) * RET_V_DIM]
            pieces.append((_silu(gate) * normed).astype(bf16))
        ret_pieces.append(pieces)
    cat_rows = [jnp.concatenate(ret_pieces[s] + swa_pieces[s], axis=1) for s in range(n_sub)]
    for hh in range(RET_HEADS):
        state_ref[hh] = states[hh]
    kprev_ref[...] = k_prev
    vprev_ref[...] = v_prev

    mix = _dot(jnp.concatenate(cat_rows, axis=0), w_out_ref[...])
    h2 = _layer_norm(DEEPNORM_ALPHA * h + mix, g_mix_ref[...], b_mix_ref[...])
    h2_ref[...] = h2
    h2p_ref[...] = _pack_bf16_pairs(h2)


def _t5_bucket(dist):
    n = jnp.maximum(dist, 0)
    max_exact = REL_BUCKETS // 2
    ratio = jnp.log(jnp.maximum(n, 1).astype(jnp.float32) / max_exact) / math.log(REL_MAX_DIST / max_exact)
    large = jnp.minimum(max_exact + (ratio * (REL_BUCKETS - max_exact)).astype(jnp.int32), REL_BUCKETS - 1)
    return jnp.where(n < max_exact, n, large)


def _mixer(x, ln_in_g, ln_in_b, w_in, w_out, rel_bias, sinks, ln_mix_g, ln_mix_b):
    B, S, D = x.shape
    R = MIX_ROWS
    C = RET_CHUNK
    W = SWA_WINDOW
    f32 = jnp.float32
    half = RET_QK_DIM // 2
    inv = ROPE_BASE ** (-jnp.arange(half, dtype=f32) / half)
    ang = jnp.arange(S, dtype=f32)[:, None] * inv[None, :]
    cos, sin = jnp.cos(ang), jnp.sin(ang)
    cos_t = jnp.tile(jnp.concatenate([cos, cos], axis=-1), (1, RET_HEADS))
    sin_t = jnp.tile(jnp.concatenate([-sin, sin], axis=-1), (1, RET_HEADS))
    rot = jnp.concatenate([cos_t, sin_t], axis=-1)
    log_gamma = jnp.log(1.0 - 2.0 ** (-5.0 - jnp.arange(RET_HEADS, dtype=f32)))
    idx = jnp.arange(C, dtype=f32)
    diff = idx[:, None] - idx[None, :]
    decay = jnp.where(diff[None] >= 0, jnp.exp(jnp.maximum(diff, 0.0)[None] * log_gamma[:, None, None]), 0.0)
    zeta = jnp.exp((C - 1.0 - idx)[None, :] * log_gamma[:, None])
    xi = jnp.exp((idx + 1.0)[None, :] * log_gamma[:, None])
    zeta_b = jnp.broadcast_to(zeta[:, :, None], (RET_HEADS, C, RET_QK_DIM))
    xi_b = jnp.broadcast_to(xi[:, :, None], (RET_HEADS, C, RET_V_DIM))
    cdecay = jnp.broadcast_to(jnp.exp(C * log_gamma)[:, None, None], (RET_HEADS, RET_QK_DIM, RET_V_DIM))
    i = jnp.arange(W)
    j = jnp.arange(2 * W)
    dist = i[:, None] + W - j[None, :]
    bucket = jnp.where((dist >= 0) & (dist < W), _t5_bucket(dist), -1).astype(jnp.int32).T
    sink_row = jnp.repeat(sinks.astype(f32), W).reshape(SWA_KV_HEADS, 1, SWA_GROUP * W)

    const = lambda shape: pl.BlockSpec(shape, lambda b, c, *_: (0,) * len(shape))
    grid_spec = pltpu.PrefetchScalarGridSpec(
        num_scalar_prefetch=1,
        grid=(B, S // R),
        in_specs=[
            pl.BlockSpec((None, R, D), lambda b, c, *_: (b, c, 0)),
            const((1, D)), const((1, D)),
            const((D, IN_WIDTH)), const((MIX_WIDTH, D)),
            pl.BlockSpec((R, 2 * RQK), lambda b, c, *_: (c, 0)),
            const((RET_HEADS, C, C)), const((RET_HEADS, C, RET_QK_DIM)), const((RET_HEADS, C, RET_V_DIM)),
            const((RET_HEADS, RET_QK_DIM, RET_V_DIM)),
            const((2 * W, W)), const((SWA_KV_HEADS, 1, SWA_GROUP * W)),
            const((1, D)), const((1, D)),
        ],
        out_specs=[pl.BlockSpec((R, D), lambda b, c, *_: (b * (S // R) + c, 0)),
                   pl.BlockSpec((R, D // 2), lambda b, c, *_: (b * (S // R) + c, 0))],
        scratch_shapes=[
            pltpu.VMEM((RET_HEADS, RET_QK_DIM, RET_V_DIM), f32),
            pltpu.VMEM((W, SKV), jnp.bfloat16),
            pltpu.VMEM((W, SKV), jnp.bfloat16),
            pltpu.VMEM((SWA_KV_HEADS, 2 * W, SWA_GROUP * W), f32),
        ],
    )
    return pl.pallas_call(
        _mixer_kernel,
        grid_spec=grid_spec,
        out_shape=[jax.ShapeDtypeStruct((B * S, D), f32), jax.ShapeDtypeStruct((B * S, D // 2), jnp.uint32)],
        compiler_params=pltpu.CompilerParams(
            dimension_semantics=("arbitrary", "arbitrary"), vmem_limit_bytes=VMEM_LIMIT_BYTES),
    )(rel_bias.astype(f32), x, ln_in_g.reshape(1, D), ln_in_b.reshape(1, D),
      w_in.astype(jnp.bfloat16), w_out.astype(jnp.bfloat16), rot, decay, zeta_b, xi_b, cdecay,
      bucket, sink_row, ln_mix_g.reshape(1, D), ln_mix_b.reshape(1, D))


def _router_kernel(h_ref, wr_ref, rb_ref, e_ref, w_ref, rk_ref, cnt_ref, run_ref):
    f32 = jnp.float32
    R = h_ref.shape[0]
    E = N_EXPERTS
    neg = -jnp.inf

    @pl.when(pl.program_id(0) == 0)
    def _init():
        run_ref[...] = jnp.zeros_like(run_ref)

    logits = _dot_nt(wr_ref[...], h_ref[...].astype(jnp.bfloat16))
    scores = 1.0 / (1.0 + jnp.exp(-logits))
    choice = scores + rb_ref[...]
    eid = lax.broadcasted_iota(jnp.int32, (E, R), 0)

    def first_argmax(vals, ids, none):
        m = jnp.max(vals, axis=0, keepdims=True)
        idx = jnp.min(jnp.where(vals == m, ids, none), axis=0, keepdims=True)
        return m, idx

    gid = lax.broadcasted_iota(jnp.int32, (GROUP_SIZE, R), 0)
    groups, gscore = [], []
    for g in range(N_GROUPS):
        vals = choice[g * GROUP_SIZE:(g + 1) * GROUP_SIZE]
        m1, i1 = first_argmax(vals, gid, GROUP_SIZE)
        m2 = jnp.max(jnp.where(gid == i1, neg, vals), axis=0, keepdims=True)
        groups.append(vals)
        gscore.append(m1 + m2)
    kept = []
    for g in range(N_GROUPS):
        beaten = jnp.zeros((1, R), f32)
        for g2 in range(N_GROUPS):
            if g2 == g:
                continue
            ahead = (gscore[g2] > gscore[g]) | (gscore[g2] == gscore[g]) if g2 < g else gscore[g2] > gscore[g]
            beaten = beaten + jnp.where(ahead, 1.0, 0.0)
        kept.append(jnp.where(beaten < TOPK_GROUPS, groups[g], neg))
    masked = jnp.concatenate(kept, axis=0)

    idxs, wts = [], []
    picked = jnp.zeros((E, R), f32)
    for _ in range(TOP_K):
        _, idx = first_argmax(masked, eid, E)
        hit = eid == idx
        idxs.append(idx)
        wts.append(jnp.sum(jnp.where(hit, scores, 0.0), axis=0, keepdims=True))
        masked = jnp.where(hit, neg, masked)
        picked = jnp.where(hit, 1.0, picked)
    wsum = wts[0]
    for k in range(1, TOP_K):
        wsum = wsum + wts[k]

    row = lax.broadcasted_iota(jnp.int32, (R, R), 0)
    col = lax.broadcasted_iota(jnp.int32, (R, R), 1)
    earlier = jnp.where(row < col, 1.0, 0.0).astype(jnp.bfloat16)
    picked_bf = picked.astype(jnp.bfloat16)
    run = run_ref[...]
    before = _dot(picked_bf, earlier) + jnp.concatenate([run] * (R // LANES), axis=1)
    sub_k = lax.broadcasted_iota(jnp.int32, (TOP_K, R), 0)
    e_out = jnp.zeros((TOP_K, R), jnp.int32)
    w_out = jnp.zeros((TOP_K, R), f32)
    rk_out = jnp.zeros((TOP_K, R), jnp.int32)
    for k in range(TOP_K):
        rank_k = jnp.sum(jnp.where(eid == idxs[k], before, 0.0), axis=0, keepdims=True)
        e_out = jnp.where(sub_k == k, idxs[k], e_out)
        w_out = jnp.where(sub_k == k, wts[k] / wsum * ROUTED_SCALE, w_out)
        rk_out = jnp.where(sub_k == k, rank_k.astype(jnp.int32), rk_out)
    e_ref[...] = e_out
    w_ref[...] = w_out
    rk_ref[...] = rk_out
    run_ref[...] = run + _dot(picked_bf, jnp.ones((R, LANES), jnp.bfloat16))
    cnt_ref[...] = run_ref[...]


def _router(h2, w_router, router_bias):
    T, D = h2.shape
    R = ROUTE_ROWS
    E = N_EXPERTS
    return pl.pallas_call(
        _router_kernel,
        grid=(T // R,),
        in_specs=[
            pl.BlockSpec((R, D), lambda i: (i, 0)),
            pl.BlockSpec((E, D), lambda i: (0, 0)),
            pl.BlockSpec((E, R), lambda i: (0, 0)),
        ],
        out_specs=[
            pl.BlockSpec((TOP_K, R), lambda i: (0, i)),
            pl.BlockSpec((TOP_K, R), lambda i: (0, i)),
            pl.BlockSpec((TOP_K, R), lambda i: (0, i)),
            pl.BlockSpec((E, LANES), lambda i: (0, 0)),
        ],
        out_shape=[
            jax.ShapeDtypeStruct((TOP_K, T), jnp.int32),
            jax.ShapeDtypeStruct((TOP_K, T), jnp.float32),
            jax.ShapeDtypeStruct((TOP_K, T), jnp.int32),
            jax.ShapeDtypeStruct((E, LANES), jnp.float32),
        ],
        scratch_shapes=[pltpu.VMEM((E, LANES), jnp.float32)],
        compiler_params=pltpu.CompilerParams(
            dimension_semantics=("arbitrary",), vmem_limit_bytes=VMEM_LIMIT_BYTES),
    )(h2, w_router.T.astype(jnp.bfloat16), jnp.broadcast_to(router_bias.astype(jnp.float32)[:, None], (E, R)))


def _positions_kernel(row_start_ref, e_ref, rk_ref, pos_ref):
    e = e_ref[...]
    rk = rk_ref[...]

    def per_expert(i, pos):
        return jnp.where(e == i, rk + row_start_ref[i], pos)

    pos_ref[...] = lax.fori_loop(0, N_EXPERTS, per_expert, jnp.zeros_like(rk))


def _positions(e_idx, rank, row_start):
    n = e_idx.size
    shape = (n // LANES, LANES)
    block = pl.BlockSpec((POSITION_ROWS, LANES), lambda i, *_: (i, 0))
    grid_spec = pltpu.PrefetchScalarGridSpec(
        num_scalar_prefetch=1,
        grid=(shape[0] // POSITION_ROWS,),
        in_specs=[block, block],
        out_specs=block,
    )
    pos = pl.pallas_call(
        _positions_kernel,
        grid_spec=grid_spec,
        out_shape=jax.ShapeDtypeStruct(shape, jnp.int32),
    )(row_start, e_idx.reshape(shape), rank.reshape(shape))
    return pos.reshape(n)


SC_WINDOW = 32
SC_DISPATCH_WINDOW = 64


def _dispatch_sc(h2, pos_kt, n_rows):
    T, D = h2.shape
    W = SC_DISPATCH_WINDOW
    idx = _window_indices(pos_kt, W)
    idx_rows = TOP_K * W // LANES
    mesh = plsc.VectorSubcoreMesh(core_axis_name="core", subcore_axis_name="subcore")

    @pl.kernel(out_type=jax.ShapeDtypeStruct((n_rows, D), h2.dtype), mesh=mesh, scratch_types=[])
    def scatter_rows(x_hbm, i_hbm, o_hbm):
        def body(x_vmem, i_vmem):
            for k in range(TOP_K):
                r, q = divmod(k * W, LANES)
                pltpu.sync_copy(x_vmem, o_hbm.at[i_vmem.at[r, pl.ds(q, W)]])

        pltpu.emit_pipeline(
            body,
            grid=(T // W,),
            in_specs=[pl.BlockSpec((W, D), lambda i: (i, 0)),
                      pl.BlockSpec((idx_rows, LANES), lambda i: (i, 0))],
            out_specs=[],
            core_axis_name=("core", "subcore"),
            dimension_semantics=(pltpu.PARALLEL,),
        )(x_hbm, i_hbm)

    return scatter_rows(h2, idx)


def _window_indices(pos_kt, window):
    K, T = pos_kt.shape
    return pos_kt.reshape(K, T // window, window).transpose(1, 0, 2).reshape(T * K // LANES, LANES)


def _gather_rows_sc(ys, pos_kt):
    n = pos_kt.size
    W = SC_WINDOW
    ys3 = ys.reshape(ys.shape[0] // ROW_TILE, ROW_TILE, LANES)
    idx = jnp.pad(pos_kt.reshape(n // W, W), ((0, 0), (0, LANES - W)))
    mesh = plsc.VectorSubcoreMesh(core_axis_name="core", subcore_axis_name="subcore")

    @pl.kernel(out_type=jax.ShapeDtypeStruct((n, ROW_TILE, LANES), ys.dtype), mesh=mesh, scratch_types=[])
    def gather_rows(y_hbm, i_hbm, o_hbm):
        def body(i_vmem, o_vmem):
            pltpu.sync_copy(y_hbm.at[i_vmem.at[0, pl.ds(0, W)]], o_vmem)

        pltpu.emit_pipeline(
            body,
            grid=(n // W,),
            in_specs=[pl.BlockSpec((1, LANES), lambda i: (i, 0))],
            out_specs=[pl.BlockSpec((W, ROW_TILE, LANES), lambda i: (i, 0, 0))],
            core_axis_name=("core", "subcore"),
            dimension_semantics=(pltpu.PARALLEL,),
        )(i_hbm, o_hbm)

    return gather_rows(ys3, idx).reshape(n * ROW_TILE, LANES)


EXPERT_SUBROWS = 128
X_SLOTS = 4
Y_SLOTS = 3


def _experts_kernel(blk_e_ref, first_ref, slot_ref, next_e_ref, valid_ref, n_act_ref, xs_hbm, wg_hbm,
                    wu_hbm, wd_hbm, ys_hbm, x_buf, y_buf, wg_buf, wu_buf, wd_buf, wg_bf, wu_bf, wd_bf,
                    sems, x_sems, y_sems):
    i = pl.program_id(0)
    n_act = n_act_ref[0]
    bf16 = jnp.bfloat16
    blk = EXPERT_ROWS * ROW_TILE

    def x_copy(j):
        n = pl.multiple_of((valid_ref[j] + 7) // 8 * 8, 8)
        src = xs_hbm.at[pl.ds(pl.multiple_of(j * EXPERT_ROWS, EXPERT_ROWS), n)]
        return pltpu.make_async_copy(src, x_buf.at[j % X_SLOTS, pl.ds(0, n)], x_sems.at[j % X_SLOTS])

    def y_copy(j):
        n = pl.multiple_of(valid_ref[j] * ROW_TILE, ROW_TILE)
        dst = ys_hbm.at[pl.ds(pl.multiple_of(j * blk, blk), n)]
        return pltpu.make_async_copy(y_buf.at[j % Y_SLOTS, pl.ds(0, n)], dst, y_sems.at[j % Y_SLOTS])

    @pl.when(i == 0)
    def _prime():
        x_buf[...] = jnp.zeros_like(x_buf)
        for j in range(X_SLOTS - 1):
            @pl.when(j < n_act)
            def _():
                x_copy(j).start()

    @pl.when(i + (X_SLOTS - 1) < n_act)
    def _prefetch():
        x_copy(i + (X_SLOTS - 1)).start()

    def weight_copies(e, slot):
        return (pltpu.make_async_copy(wg_hbm.at[e], wg_buf.at[slot], sems.at[slot]),
                pltpu.make_async_copy(wu_hbm.at[e], wu_buf.at[slot], sems.at[slot]),
                pltpu.make_async_copy(wd_hbm.at[e], wd_buf.at[slot], sems.at[slot]))

    @pl.when((i < n_act_ref[0]) & (first_ref[i] == 1))
    def _new_expert():
        slot = slot_ref[i]

        @pl.when(i == 0)
        def _():
            for c in weight_copies(blk_e_ref[0], 0):
                c.start()

        for c in weight_copies(blk_e_ref[i], slot):
            c.wait()

        @pl.when(next_e_ref[i] >= 0)
        def _():
            for c in weight_copies(next_e_ref[i], 1 - slot):
                c.start()

        wg_bf[...] = wg_buf[slot].astype(bf16)
        wu_bf[...] = wu_buf[slot].astype(bf16)
        wd_bf[...] = wd_buf[slot].astype(bf16)

    @pl.when(i < n_act)
    def _compute():
        x_copy(i).wait()

        @pl.when(i >= Y_SLOTS)
        def _():
            y_copy(i - Y_SLOTS).wait()

        sub = EXPERT_SUBROWS
        n_parts = (valid_ref[i] + sub - 1) // sub

        def run(parts):
            xs_ = [_unpack_bf16_pairs(x_buf[i % X_SLOTS, pl.ds(part * sub, sub), :]) for part in range(parts)]
            gs = [_dot(x, wg_bf[...]) for x in xs_]
            us = [_dot(x, wu_bf[...]) for x in xs_]
            acts = [(_silu(g) * u).astype(bf16) for g, u in zip(gs, us)]
            for part in range(parts):
                _store_rows(y_buf, _dot(acts[part], wd_bf[...]), lead=(i % Y_SLOTS,), first_row=part * sub)

        for parts in range(1, EXPERT_ROWS // sub + 1):
            pl.when(n_parts == parts)(functools.partial(run, parts))
        y_copy(i).start()

    @pl.when(i == n_act - 1)
    def _drain():
        for d in range(Y_SLOTS):
            @pl.when(i - d >= 0)
            def _():
                y_copy(i - d).wait()


def _experts(xs, blk_e, n_act, row_start, cnt, w_gate, w_up, w_down):
    D = D_MODEL
    BM = EXPERT_ROWS
    F = EXPERT_DIM
    n_blocks = xs.shape[0] // BM
    blk_in_expert = jnp.arange(n_blocks, dtype=jnp.int32) - row_start[blk_e] // BM
    valid = jnp.clip(cnt[blk_e] - blk_in_expert * BM, 0, BM).astype(jnp.int32)
    ids = jnp.arange(n_blocks, dtype=jnp.int32)
    active = ids < n_act[0]
    first = active & ((ids == 0) | (blk_e != jnp.roll(blk_e, 1)))
    slot = ((jnp.cumsum(first.astype(jnp.int32)) - 1) % 2).astype(jnp.int32)
    first_pos = jnp.where(first, ids, n_blocks)
    later_first = lax.cummin(jnp.concatenate([first_pos[1:], jnp.full((1,), n_blocks, jnp.int32)]), reverse=True)
    next_e = jnp.where(later_first < n_blocks, blk_e[jnp.minimum(later_first, n_blocks - 1)], -1).astype(jnp.int32)

    grid_spec = pltpu.PrefetchScalarGridSpec(
        num_scalar_prefetch=6,
        grid=(n_blocks,),
        in_specs=[pl.BlockSpec(memory_space=pl.ANY)] * 4,
        out_specs=pl.BlockSpec(memory_space=pl.ANY),
        scratch_shapes=[
            pltpu.VMEM((X_SLOTS, BM, D // 2), jnp.uint32),
            pltpu.VMEM((Y_SLOTS, BM * ROW_TILE, LANES), jnp.float32),
            pltpu.VMEM((2, D, F), jnp.float32), pltpu.VMEM((2, D, F), jnp.float32),
            pltpu.VMEM((2, F, D), jnp.float32),
            pltpu.VMEM((D, F), jnp.bfloat16), pltpu.VMEM((D, F), jnp.bfloat16),
            pltpu.VMEM((F, D), jnp.bfloat16),
            pltpu.SemaphoreType.DMA((2,)), pltpu.SemaphoreType.DMA((X_SLOTS,)),
            pltpu.SemaphoreType.DMA((Y_SLOTS,)),
        ],
    )
    return pl.pallas_call(
        _experts_kernel,
        grid_spec=grid_spec,
        out_shape=jax.ShapeDtypeStruct((xs.shape[0] * ROW_TILE, LANES), jnp.float32),
        compiler_params=pltpu.CompilerParams(
            dimension_semantics=("arbitrary",), vmem_limit_bytes=VMEM_LIMIT_BYTES),
    )(blk_e, first.astype(jnp.int32), slot, next_e, valid, n_act, xs, w_gate, w_up, w_down)


def _finish_kernel(h_ref, part_ref, w_ref, *rest):
    slabs = rest[:SC_COMBINE_SLOTS]
    g_ref, b_ref, out_ref = rest[SC_COMBINE_SLOTS:]
    R = h_ref.shape[0]
    w = w_ref[...]
    ffn = part_ref[...]
    for k in range(SC_COMBINE_SLOTS):
        ffn = ffn + _load_rows(slabs[k], R) * w[:, k:k + 1]
    out_ref[...] = _layer_norm(DEEPNORM_ALPHA * h_ref[...] + ffn, g_ref[...], b_ref[...])


def _finish(h2, partial, top_w, gathered, ln_g, ln_b):
    T, D = h2.shape
    R = FINISH_ROWS
    rows = pl.BlockSpec((R, D), lambda i: (i, 0))
    vec = pl.BlockSpec((1, D), lambda i: (0, 0))
    slab = lambda k: pl.BlockSpec((R * ROW_TILE, LANES), lambda i: (k * (T // R) + i, 0))
    return pl.pallas_call(
        _finish_kernel,
        grid=(T // R,),
        in_specs=[rows, rows, pl.BlockSpec((R, TOP_K), lambda i: (i, 0))]
        + [slab(k) for k in range(SC_COMBINE_SLOTS)] + [vec, vec],
        out_specs=rows,
        out_shape=jax.ShapeDtypeStruct((T, D), jnp.float32),
        compiler_params=pltpu.CompilerParams(
            dimension_semantics=("arbitrary",), vmem_limit_bytes=VMEM_LIMIT_BYTES),
    )(h2, partial, top_w, *([gathered] * SC_COMBINE_SLOTS), ln_g.reshape(1, D), ln_b.reshape(1, D))


SC_COMBINE_SLOTS = 5


def _combine_kernel(pos_ref, pos_next_ref, h_ref, w_ref, ys_ref, wsg_ref, wsu_ref, wsd_ref,
                    out_ref, buf_ref, sems):
    R = h_ref.shape[0]
    i = pl.program_id(0)
    slot = i % 2

    def gather(p_ref, s):
        def issue(t, carry):
            for k in range(SC_COMBINE_SLOTS, TOP_K):
                src = p_ref[t * TOP_K + k]
                pltpu.make_async_copy(ys_ref.at[_row_tile(src)], buf_ref.at[s, k - SC_COMBINE_SLOTS, _row_tile(t)],
                                      sems.at[s]).start(priority=k % 2)
            return carry

        lax.fori_loop(0, R, issue, 0)

    @pl.when(i == 0)
    def _():
        gather(pos_ref, 0)

    @pl.when(i + 1 < pl.num_programs(0))
    def _():
        gather(pos_next_ref, 1 - slot)

    h = h_ref[...]
    hb = h.astype(jnp.bfloat16)
    act = (_silu(_dot(hb, wsg_ref[...])) * _dot(hb, wsu_ref[...])).astype(jnp.bfloat16)
    ffn = _dot(act, wsd_ref[...])
    pltpu.make_async_copy(buf_ref.at[slot], buf_ref.at[slot], sems.at[slot]).wait()
    w = w_ref[...]
    for k in range(SC_COMBINE_SLOTS, TOP_K):
        ffn = ffn + _load_rows(buf_ref, R, lead=(slot, k - SC_COMBINE_SLOTS)) * w[:, k:k + 1]
    out_ref[...] = ffn


def _combine(h2, pos, top_w, ys, ws_gate, ws_up, ws_down):
    T, D = h2.shape
    R = COMBINE_ROWS
    F = SHARED_DIM
    bf16 = jnp.bfloat16
    const = lambda shape: pl.BlockSpec(shape, lambda i: (0,) * len(shape))
    return pl.pallas_call(
        _combine_kernel,
        grid=(T // R,),
        in_specs=[
            pl.BlockSpec((R * TOP_K,), lambda i: (i,), memory_space=pltpu.SMEM),
            pl.BlockSpec((R * TOP_K,), lambda i: (jnp.minimum(i + 1, T // R - 1),), memory_space=pltpu.SMEM),
            pl.BlockSpec((R, D), lambda i: (i, 0)),
            pl.BlockSpec((R, TOP_K), lambda i: (i, 0)),
            pl.BlockSpec(memory_space=pl.ANY),
            const((D, F)), const((D, F)), const((F, D)),
        ],
        out_specs=pl.BlockSpec((R, D), lambda i: (i, 0)),
        scratch_shapes=[pltpu.VMEM((2, TOP_K - SC_COMBINE_SLOTS, R * ROW_TILE, LANES), jnp.float32),
                        pltpu.SemaphoreType.DMA((2,))],
        out_shape=jax.ShapeDtypeStruct((T, D), jnp.float32),
        compiler_params=pltpu.CompilerParams(
            dimension_semantics=("arbitrary",), vmem_limit_bytes=VMEM_LIMIT_BYTES),
    )(pos, pos, h2, top_w, ys, ws_gate.astype(bf16), ws_up.astype(bf16), ws_down.astype(bf16))


def _moe(h2, h2_packed, w_router, router_bias, w_gate, w_up, w_down, ws_gate, ws_up, ws_down, ln_g, ln_b):
    T = h2.shape[0]
    E = N_EXPERTS
    BM = EXPERT_ROWS
    e_idx, top_w, rank, counts = _router(h2, w_router, router_bias)
    cnt = counts[:, 0].astype(jnp.int32)
    nblk = (cnt + BM - 1) // BM
    blk_end = jnp.cumsum(nblk)
    row_start = ((blk_end - nblk) * BM).astype(jnp.int32)
    n_blocks = T * TOP_K // BM + E
    n_act = blk_end[-1:].astype(jnp.int32)
    blk_ids = jnp.minimum(jnp.arange(n_blocks, dtype=jnp.int32), n_act[0] - 1)
    blk_e = jnp.minimum(jnp.sum(blk_end[None, :] <= blk_ids[:, None], axis=1), E - 1).astype(jnp.int32)
    pos_kt = _positions(e_idx, rank, row_start).reshape(TOP_K, T)
    pos = pos_kt.T.reshape(T * TOP_K)
    xs = _dispatch_sc(h2_packed, pos_kt, n_blocks * BM)
    ys = _experts(xs, blk_e, n_act, row_start, cnt, w_gate, w_up, w_down)
    gathered = _gather_rows_sc(ys, pos_kt[:SC_COMBINE_SLOTS])
    w_tk = top_w.T
    partial = _combine(h2, pos, w_tk, ys, ws_gate, ws_up, ws_down)
    return _finish(h2, partial, w_tk, gathered, ln_g, ln_b)


def kernel(x, ln_in_g, ln_in_b, w_in, w_out, rel_bias, attn_sinks, ln_mix_g, ln_mix_b, w_router,
           router_bias, w_gate, w_up, w_down, ws_gate, ws_up, ws_down, ln_ffn_g, ln_ffn_b):
    B, S, D = x.shape
    h, h_packed = _mixer(x, ln_in_g, ln_in_b, w_in[0], w_out[0], rel_bias, attn_sinks[0], ln_mix_g[0],
                         ln_mix_b[0])
    out = _moe(h, h_packed, w_router[0], router_bias[0], w_gate[0], w_up[0], w_down[0],
               ws_gate[0], ws_up[0], ws_down[0], ln_ffn_g[0], ln_ffn_b[0])
    return out.reshape(B, S, D)
```

```python
import functools
import math

import jax
import jax.numpy as jnp
from jax import lax
from jax.experimental import pallas as pl
from jax.experimental.pallas import tpu as pltpu
from jax.experimental.pallas import tpu_sc as plsc

D_MODEL = 1024
DEPTH = 1
RET_HEADS = 4
RET_QK_DIM = 64
RET_V_DIM = 128
RET_CHUNK = 128
RET_WIDTH = RET_HEADS * RET_V_DIM
ROPE_BASE = 10000.0
SWA_HEADS = 8
SWA_KV_HEADS = 2
SWA_GROUP = SWA_HEADS // SWA_KV_HEADS
SWA_HEAD_DIM = 64
SWA_WINDOW = 128
SWA_WIDTH = SWA_HEADS * SWA_HEAD_DIM
MIX_WIDTH = RET_WIDTH + SWA_WIDTH
RQK = RET_HEADS * RET_QK_DIM
SKV = SWA_KV_HEADS * SWA_HEAD_DIM
IN_SIZES = (RQK, RQK, RET_WIDTH, RET_WIDTH, SWA_WIDTH, SKV, SKV)
IN_OFFS = tuple(sum(IN_SIZES[:i]) for i in range(len(IN_SIZES)))
IN_WIDTH = sum(IN_SIZES)
REL_BUCKETS = 32
REL_MAX_DIST = 128
N_EXPERTS = 256
TOP_K = 8
N_GROUPS = 8
GROUP_SIZE = N_EXPERTS // N_GROUPS
TOPK_GROUPS = 4
EXPERT_DIM = 256
SHARED_DIM = 256
ROUTED_SCALE = 2.5
LN_EPS = 1e-5
GN_EPS = 1e-6
DEEPNORM_ALPHA = (2 * DEPTH) ** 0.25
MASK_VALUE = -1e30

VMEM_LIMIT_BYTES = 56 * 1024 * 1024

MIX_ROWS = 256
ROUTE_ROWS = 256
EXPERT_ROWS = 512
COMBINE_ROWS = 256
POSITION_ROWS = 64
FINISH_ROWS = 512


def _layer_norm(x, g, b):
    mu = jnp.mean(x, axis=-1, keepdims=True)
    xc = x - mu
    var = jnp.mean(xc * xc, axis=-1, keepdims=True)
    return xc * lax.rsqrt(var + LN_EPS) * g + b


def _dot(a, b):
    return jnp.dot(a, b, preferred_element_type=jnp.float32)


def _dot_nt(a, b):
    return lax.dot_general(a, b, (((1,), (1,)), ((), ())), preferred_element_type=jnp.float32)


def _dot_tn(a, b):
    return lax.dot_general(a, b, (((0,), (0,)), ((), ())), preferred_element_type=jnp.float32)


def _silu(x):
    return x * (1.0 / (1.0 + jnp.exp(-x)))


LANES = 128
ROW_TILE = D_MODEL // LANES


def _load_rows(ref, n_rows, lead=()):
    return jnp.concatenate([ref[lead + (pl.ds(s, n_rows, stride=ROW_TILE), slice(None))]
                            for s in range(ROW_TILE)], axis=1)


def _store_rows(ref, val, lead=(), first_row=0):
    n_rows = val.shape[0]
    for s in range(ROW_TILE):
        dst = pl.ds(first_row * ROW_TILE + s, n_rows, stride=ROW_TILE)
        ref[lead + (dst, slice(None))] = val[:, s * LANES:(s + 1) * LANES]


def _row_tile(r):
    return pl.ds(pl.multiple_of(r * ROW_TILE, ROW_TILE), ROW_TILE)


def _pack_bf16_pairs(x):
    m = x.shape[1] // 2
    bits = lax.bitcast_convert_type(x.astype(jnp.bfloat16).astype(jnp.float32), jnp.uint32)
    return (bits[:, :m] >> 16) | (bits[:, m:] & jnp.uint32(0xFFFF0000))


def _unpack_bf16_pairs(p):
    lo = lax.bitcast_convert_type(p << 16, jnp.float32)
    hi = lax.bitcast_convert_type(p & jnp.uint32(0xFFFF0000), jnp.float32)
    return jnp.concatenate([lo, hi], axis=1).astype(jnp.bfloat16)


def _swap_halves(x):
    n = x.shape[-1]
    half = RET_QK_DIM // 2
    lane = lax.broadcasted_iota(jnp.int32, x.shape, 1)
    from_right = pltpu.roll(x, n - half, axis=1)
    from_left = pltpu.roll(x, half, axis=1)
    return jnp.where((lane % RET_QK_DIM) < half, from_right, from_left)


def _mixer_kernel(rel_bias_ref, x_ref, g_in_ref, b_in_ref, w_in_ref, w_out_ref, rot_ref, decay_ref,
                  zeta_ref, xi_ref, cdecay_ref, bucket_ref, sink_ref, g_mix_ref, b_mix_ref,
                  h2_ref, h2p_ref, state_ref, kprev_ref, vprev_ref, bias_ref):
    b_id = pl.program_id(0)
    c_id = pl.program_id(1)
    W = SWA_WINDOW

    @pl.when((b_id == 0) & (c_id == 0))
    def _build_bias():
        bucket = bucket_ref[...]
        for h in range(SWA_HEADS):
            acc = jnp.full((2 * W, W), MASK_VALUE, jnp.float32)
            for b in range(REL_BUCKETS):
                acc = jnp.where(bucket == b, rel_bias_ref[b, h], acc)
            kh, g = divmod(h, SWA_GROUP)
            bias_ref[kh, :, g * W:(g + 1) * W] = acc

    @pl.when(c_id == 0)
    def _reset():
        state_ref[...] = jnp.zeros_like(state_ref)
        kprev_ref[...] = jnp.zeros_like(kprev_ref)
        vprev_ref[...] = jnp.zeros_like(vprev_ref)

    h = _layer_norm(x_ref[...], g_in_ref[...], b_in_ref[...])
    proj = _dot(h.astype(jnp.bfloat16), w_in_ref[...])

    o_q, o_k, o_v, o_g, o_sq, o_sk, o_sv = IN_OFFS
    cos_t = rot_ref[:, :RQK]
    sin_t = rot_ref[:, RQK:]
    q_all = proj[:, o_q:o_q + RQK]
    k_all = proj[:, o_k:o_k + RQK]
    q_rot = q_all * cos_t + _swap_halves(q_all) * sin_t
    k_rot = (k_all * cos_t + _swap_halves(k_all) * sin_t) * (RET_QK_DIM ** -0.5)

    n_sub = x_ref.shape[0] // RET_CHUNK
    states = [state_ref[hh] for hh in range(RET_HEADS)]
    k_prev = kprev_ref[...]
    v_prev = vprev_ref[...]
    bf16 = jnp.bfloat16
    heads = range(RET_HEADS)
    ret_pieces, swa_pieces = [], []
    swa_jobs = []
    for s in range(n_sub):
        rows = slice(s * RET_CHUNK, (s + 1) * RET_CHUNK)
        k_cur = proj[rows, o_sk:o_sk + SKV].astype(bf16)
        v_cur = proj[rows, o_sv:o_sv + SKV].astype(bf16)
        for kh in range(SWA_KV_HEADS):
            kv = slice(kh * SWA_HEAD_DIM, (kh + 1) * SWA_HEAD_DIM)
            q4 = jnp.concatenate(
                [proj[rows, o_sq + (kh * SWA_GROUP + g) * SWA_HEAD_DIM:
                      o_sq + (kh * SWA_GROUP + g + 1) * SWA_HEAD_DIM] for g in range(SWA_GROUP)],
                axis=0) * (SWA_HEAD_DIM ** -0.5)
            kcat = jnp.concatenate([k_prev[:, kv], k_cur[:, kv]], axis=0)
            vcat = jnp.concatenate([v_prev[:, kv], v_cur[:, kv]], axis=0)
            swa_jobs.append((s, kh, q4.astype(bf16), kcat, vcat))
        k_prev, v_prev = k_cur, v_cur
    logits_all = [_dot_nt(kcat, q4) + bias_ref[kh] for (s, kh, q4, kcat, vcat) in swa_jobs]
    probs_all = []
    for (s, kh, q4, kcat, vcat), logits in zip(swa_jobs, logits_all):
        if s == 0:
            key = lax.broadcasted_iota(jnp.int32, logits.shape, 0)
            logits = logits + jnp.where((key < W) & (c_id == 0), MASK_VALUE, 0.0)
        sink = sink_ref[kh]
        m = jnp.maximum(jnp.max(logits, axis=0, keepdims=True), sink)
        p = jnp.exp(logits - m)
        den = jnp.sum(p, axis=0, keepdims=True) + jnp.exp(sink - m)
        probs_all.append((p / den).astype(bf16))
    o4_all = [_dot_tn(job[4], probs) for job, probs in zip(swa_jobs, probs_all)]
    for s in range(n_sub):
        swa_pieces.append([o4[:, g * W:(g + 1) * W].T.astype(bf16)
                           for job, o4 in zip(swa_jobs, o4_all) if job[0] == s for g in range(SWA_GROUP)])
    for s in range(n_sub):
        rows = slice(s * RET_CHUNK, (s + 1) * RET_CHUNK)
        qk = [slice(hh * RET_QK_DIM, (hh + 1) * RET_QK_DIM) for hh in heads]
        q = [q_rot[rows, qk[hh]].astype(bf16) for hh in heads]
        k32 = [k_rot[rows, qk[hh]] for hh in heads]
        v = [proj[rows, o_v + hh * RET_V_DIM:o_v + (hh + 1) * RET_V_DIM].astype(bf16) for hh in heads]
        scores = [_dot_nt(q[hh], k32[hh].astype(bf16)) * decay_ref[hh] for hh in heads]
        inter = [_dot(q[hh], states[hh].astype(bf16)) * xi_ref[hh] for hh in heads]
        kv_new = [_dot_tn((k32[hh] * zeta_ref[hh]).astype(bf16), v[hh]) for hh in heads]
        intra = [_dot(scores[hh].astype(bf16), v[hh]) for hh in heads]
        states = [states[hh] * cdecay_ref[hh] + kv_new[hh] for hh in heads]
        pieces = []
        for hh in heads:
            ret = intra[hh] + inter[hh]
            mu = jnp.mean(ret, axis=-1, keepdims=True)
            rc = ret - mu
            var = jnp.mean(rc * rc, axis=-1, keepdims=True)
            normed = rc * lax.rsqrt(var + GN_EPS)
            gate = proj[rows, o_g + hh * RET_V_DIM:o_g + (hh + 1) * RET_V_DIM]
            pieces.append((_silu(gate) * normed).astype(bf16))
        ret_pieces.append(pieces)
    cat_rows = [jnp.concatenate(ret_pieces[s] + swa_pieces[s], axis=1) for s in range(n_sub)]
    for hh in range(RET_HEADS):
        state_ref[hh] = states[hh]
    kprev_ref[...] = k_prev
    vprev_ref[...] = v_prev

    mix = _dot(jnp.concatenate(cat_rows, axis=0), w_out_ref[...])
    h2 = _layer_norm(DEEPNORM_ALPHA * h + mix, g_mix_ref[...], b_mix_ref[...])
    h2_ref[...] = h2
    h2p_ref[...] = _pack_bf16_pairs(h2)


def _t5_bucket(dist):
    n = jnp.maximum(dist, 0)
    max_exact = REL_BUCKETS // 2
    ratio = jnp.log(jnp.maximum(n, 1).astype(jnp.float32) / max_exact) / math.log(REL_MAX_DIST / max_exact)
    large = jnp.minimum(max_exact + (ratio * (REL_BUCKETS - max_exact)).astype(jnp.int32), REL_BUCKETS - 1)
    return jnp.where(n < max_exact, n, large)


def _mixer(x, ln_in_g, ln_in_b, w_in, w_out, rel_bias, sinks, ln_mix_g, ln_mix_b):
    B, S, D = x.shape
    R = MIX_ROWS
    C = RET_CHUNK
    W = SWA_WINDOW
    f32 = jnp.float32
    half = RET_QK_DIM // 2
    inv = ROPE_BASE ** (-jnp.arange(half, dtype=f32) / half)
    ang = jnp.arange(S, dtype=f32)[:, None] * inv[None, :]
    cos, sin = jnp.cos(ang), jnp.sin(ang)
    cos_t = jnp.tile(jnp.concatenate([cos, cos], axis=-1), (1, RET_HEADS))
    sin_t = jnp.tile(jnp.concatenate([-sin, sin], axis=-1), (1, RET_HEADS))
    rot = jnp.concatenate([cos_t, sin_t], axis=-1)
    log_gamma = jnp.log(1.0 - 2.0 ** (-5.0 - jnp.arange(RET_HEADS, dtype=f32)))
    idx = jnp.arange(C, dtype=f32)
    diff = idx[:, None] - idx[None, :]
    decay = jnp.where(diff[None] >= 0, jnp.exp(jnp.maximum(diff, 0.0)[None] * log_gamma[:, None, None]), 0.0)
    zeta = jnp.exp((C - 1.0 - idx)[None, :] * log_gamma[:, None])
    xi = jnp.exp((idx + 1.0)[None, :] * log_gamma[:, None])
    zeta_b = jnp.broadcast_to(zeta[:, :, None], (RET_HEADS, C, RET_QK_DIM))
    xi_b = jnp.broadcast_to(xi[:, :, None], (RET_HEADS, C, RET_V_DIM))
    cdecay = jnp.broadcast_to(jnp.exp(C * log_gamma)[:, None, None], (RET_HEADS, RET_QK_DIM, RET_V_DIM))
    i = jnp.arange(W)
    j = jnp.arange(2 * W)
    dist = i[:, None] + W - j[None, :]
    bucket = jnp.where((dist >= 0) & (dist < W), _t5_bucket(dist), -1).astype(jnp.int32).T
    sink_row = jnp.repeat(sinks.astype(f32), W).reshape(SWA_KV_HEADS, 1, SWA_GROUP * W)

    const = lambda shape: pl.BlockSpec(shape, lambda b, c, *_: (0,) * len(shape))
    grid_spec = pltpu.PrefetchScalarGridSpec(
        num_scalar_prefetch=1,
        grid=(B, S // R),
        in_specs=[
            pl.BlockSpec((None, R, D), lambda b, c, *_: (b, c, 0)),
            const((1, D)), const((1, D)),
            const((D, IN_WIDTH)), const((MIX_WIDTH, D)),
            pl.BlockSpec((R, 2 * RQK), lambda b, c, *_: (c, 0)),
            const((RET_HEADS, C, C)), const((RET_HEADS, C, RET_QK_DIM)), const((RET_HEADS, C, RET_V_DIM)),
            const((RET_HEADS, RET_QK_DIM, RET_V_DIM)),
            const((2 * W, W)), const((SWA_KV_HEADS, 1, SWA_GROUP * W)),
            const((1, D)), const((1, D)),
        ],
        out_specs=[pl.BlockSpec((R, D), lambda b, c, *_: (b * (S // R) + c, 0)),
                   pl.BlockSpec((R, D // 2), lambda b, c, *_: (b * (S // R) + c, 0))],
        scratch_shapes=[
            pltpu.VMEM((RET_HEADS, RET_QK_DIM, RET_V_DIM), f32),
            pltpu.VMEM((W, SKV), jnp.bfloat16),
            pltpu.VMEM((W, SKV), jnp.bfloat16),
            pltpu.VMEM((SWA_KV_HEADS, 2 * W, SWA_GROUP * W), f32),
        ],
    )
    return pl.pallas_call(
        _mixer_kernel,
        grid_spec=grid_spec,
        out_shape=[jax.ShapeDtypeStruct((B * S, D), f32), jax.ShapeDtypeStruct((B * S, D // 2), jnp.uint32)],
        compiler_params=pltpu.CompilerParams(
            dimension_semantics=("arbitrary", "arbitrary"), vmem_limit_bytes=VMEM_LIMIT_BYTES),
    )(rel_bias.astype(f32), x, ln_in_g.reshape(1, D), ln_in_b.reshape(1, D),
      w_in.astype(jnp.bfloat16), w_out.astype(jnp.bfloat16), rot, decay, zeta_b, xi_b, cdecay,
      bucket, sink_row, ln_mix_g.reshape(1, D), ln_mix_b.reshape(1, D))


def _router_kernel(h_ref, wr_ref, rb_ref, e_ref, w_ref, rk_ref, cnt_ref, run_ref):
    f32 = jnp.float32
    R = h_ref.shape[0]
    E = N_EXPERTS
    neg = -jnp.inf

    @pl.when(pl.program_id(0) == 0)
    def _init():
        run_ref[...] = jnp.zeros_like(run_ref)

    logits = _dot_nt(wr_ref[...], h_ref[...].astype(jnp.bfloat16))
    scores = 1.0 / (1.0 + jnp.exp(-logits))
    choice = scores + rb_ref[...]
    eid = lax.broadcasted_iota(jnp.int32, (E, R), 0)

    def first_argmax(vals, ids, none):
        m = jnp.max(vals, axis=0, keepdims=True)
        idx = jnp.min(jnp.where(vals == m, ids, none), axis=0, keepdims=True)
        return m, idx

    gid = lax.broadcasted_iota(jnp.int32, (GROUP_SIZE, R), 0)
    groups, gscore = [], []
    for g in range(N_GROUPS):
        vals = choice[g * GROUP_SIZE:(g + 1) * GROUP_SIZE]
        m1, i1 = first_argmax(vals, gid, GROUP_SIZE)
        m2 = jnp.max(jnp.where(gid == i1, neg, vals), axis=0, keepdims=True)
        groups.append(vals)
        gscore.append(m1 + m2)
    kept = []
    for g in range(N_GROUPS):
        beaten = jnp.zeros((1, R), f32)
        for g2 in range(N_GROUPS):
            if g2 == g:
                continue
            ahead = (gscore[g2] > gscore[g]) | (gscore[g2] == gscore[g]) if g2 < g else gscore[g2] > gscore[g]
            beaten = beaten + jnp.where(ahead, 1.0, 0.0)
        kept.append(jnp.where(beaten < TOPK_GROUPS, groups[g], neg))
    masked = jnp.concatenate(kept, axis=0)

    idxs, wts = [], []
    picked = jnp.zeros((E, R), f32)
    for _ in range(TOP_K):
        _, idx = first_argmax(masked, eid, E)
        hit = eid == idx
        idxs.append(idx)
        wts.append(jnp.sum(jnp.where(hit, scores, 0.0), axis=0, keepdims=True))
        masked = jnp.where(hit, neg, masked)
        picked = jnp.where(hit, 1.0, picked)
    wsum = wts[0]
    for k in range(1, TOP_K):
        wsum = wsum + wts[k]

    row = lax.broadcasted_iota(jnp.int32, (R, R), 0)
    col = lax.broadcasted_iota(jnp.int32, (R, R), 1)
    earlier = jnp.where(row < col, 1.0, 0.0).astype(jnp.bfloat16)
    picked_bf = picked.astype(jnp.bfloat16)
    run = run_ref[...]
    before = _dot(picked_bf, earlier) + jnp.concatenate([run] * (R // LANES), axis=1)
    sub_k = lax.broadcasted_iota(jnp.int32, (TOP_K, R), 0)
    e_out = jnp.zeros((TOP_K, R), jnp.int32)
    w_out = jnp.zeros((TOP_K, R), f32)
    rk_out = jnp.zeros((TOP_K, R), jnp.int32)
    for k in range(TOP_K):
        rank_k = jnp.sum(jnp.where(eid == idxs[k], before, 0.0), axis=0, keepdims=True)
        e_out = jnp.where(sub_k == k, idxs[k], e_out)
        w_out = jnp.where(sub_k == k, wts[k] / wsum * ROUTED_SCALE, w_out)
        rk_out = jnp.where(sub_k == k, rank_k.astype(jnp.int32), rk_out)
    e_ref[...] = e_out
    w_ref[...] = w_out
    rk_ref[...] = rk_out
    run_ref[...] = run + _dot(picked_bf, jnp.ones((R, LANES), jnp.bfloat16))
    cnt_ref[...] = run_ref[...]


def _router(h2, w_router, router_bias):
    T, D = h2.shape
    R = ROUTE_ROWS
    E = N_EXPERTS
    return pl.pallas_call(
        _router_kernel,
        grid=(T // R,),
        in_specs=[
            pl.BlockSpec((R, D), lambda i: (i, 0)),
            pl.BlockSpec((E, D), lambda i: (0, 0)),
            pl.BlockSpec((E, R), lambda i: (0, 0)),
        ],
        out_specs=[
            pl.BlockSpec((TOP_K, R), lambda i: (0, i)),
            pl.BlockSpec((TOP_K, R), lambda i: (0, i)),
            pl.BlockSpec((TOP_K, R), lambda i: (0, i)),
            pl.BlockSpec((E, LANES), lambda i: (0, 0)),
        ],
        out_shape=[
            jax.ShapeDtypeStruct((TOP_K, T), jnp.int32),
            jax.ShapeDtypeStruct((TOP_K, T), jnp.float32),
            jax.ShapeDtypeStruct((TOP_K, T), jnp.int32),
            jax.ShapeDtypeStruct((E, LANES), jnp.float32),
        ],
        scratch_shapes=[pltpu.VMEM((E, LANES), jnp.float32)],
        compiler_params=pltpu.CompilerParams(
            dimension_semantics=("arbitrary",), vmem_limit_bytes=VMEM_LIMIT_BYTES),
    )(h2, w_router.T.astype(jnp.bfloat16), jnp.broadcast_to(router_bias.astype(jnp.float32)[:, None], (E, R)))


def _positions_kernel(row_start_ref, e_ref, rk_ref, pos_ref):
    e = e_ref[...]
    rk = rk_ref[...]

    def per_expert(i, pos):
        return jnp.where(e == i, rk + row_start_ref[i], pos)

    pos_ref[...] = lax.fori_loop(0, N_EXPERTS, per_expert, jnp.zeros_like(rk))


def _positions(e_idx, rank, row_start):
    n = e_idx.size
    shape = (n // LANES, LANES)
    block = pl.BlockSpec((POSITION_ROWS, LANES), lambda i, *_: (i, 0))
    grid_spec = pltpu.PrefetchScalarGridSpec(
        num_scalar_prefetch=1,
        grid=(shape[0] // POSITION_ROWS,),
        in_specs=[block, block],
        out_specs=block,
    )
    pos = pl.pallas_call(
        _positions_kernel,
        grid_spec=grid_spec,
        out_shape=jax.ShapeDtypeStruct(shape, jnp.int32),
    )(row_start, e_idx.reshape(shape), rank.reshape(shape))
    return pos.reshape(n)


SC_WINDOW = 32
SC_DISPATCH_WINDOW = 64


def _dispatch_sc(h2, pos_kt, n_rows):
    T, D = h2.shape
    W = SC_DISPATCH_WINDOW
    idx = _window_indices(pos_kt, W)
    idx_rows = TOP_K * W // LANES
    mesh = plsc.VectorSubcoreMesh(core_axis_name="core", subcore_axis_name="subcore")

    @pl.kernel(out_type=jax.ShapeDtypeStruct((n_rows, D), h2.dtype), mesh=mesh, scratch_types=[])
    def scatter_rows(x_hbm, i_hbm, o_hbm):
        def body(x_vmem, i_vmem):
            for k in range(TOP_K):
                r, q = divmod(k * W, LANES)
                pltpu.sync_copy(x_vmem, o_hbm.at[i_vmem.at[r, pl.ds(q, W)]])

        pltpu.emit_pipeline(
            body,
            grid=(T // W,),
            in_specs=[pl.BlockSpec((W, D), lambda i: (i, 0)),
                      pl.BlockSpec((idx_rows, LANES), lambda i: (i, 0))],
            out_specs=[],
            core_axis_name=("core", "subcore"),
            dimension_semantics=(pltpu.PARALLEL,),
        )(x_hbm, i_hbm)

    return scatter_rows(h2, idx)


def _window_indices(pos_kt, window):
    K, T = pos_kt.shape
    return pos_kt.reshape(K, T // window, window).transpose(1, 0, 2).reshape(T * K // LANES, LANES)


def _gather_rows_sc(ys, pos_kt):
    n = pos_kt.size
    W = SC_WINDOW
    ys3 = ys.reshape(ys.shape[0] // ROW_TILE, ROW_TILE, LANES)
    idx = jnp.pad(pos_kt.reshape(n // W, W), ((0, 0), (0, LANES - W)))
    mesh = plsc.VectorSubcoreMesh(core_axis_name="core", subcore_axis_name="subcore")

    @pl.kernel(out_type=jax.ShapeDtypeStruct((n, ROW_TILE, LANES), ys.dtype), mesh=mesh, scratch_types=[])
    def gather_rows(y_hbm, i_hbm, o_hbm):
        def body(i_vmem, o_vmem):
            pltpu.sync_copy(y_hbm.at[i_vmem.at[0, pl.ds(0, W)]], o_vmem)

        pltpu.emit_pipeline(
            body,
            grid=(n // W,),
            in_specs=[pl.BlockSpec((1, LANES), lambda i: (i, 0))],
            out_specs=[pl.BlockSpec((W, ROW_TILE, LANES), lambda i: (i, 0, 0))],
            core_axis_name=("core", "subcore"),
            dimension_semantics=(pltpu.PARALLEL,),
        )(i_hbm, o_hbm)

    return gather_rows(ys3, idx).reshape(n * ROW_TILE, LANES)


EXPERT_SUBROWS = 128
X_SLOTS = 4
Y_SLOTS = 3


def _experts_kernel(blk_e_ref, first_ref, slot_ref, next_e_ref, valid_ref, n_act_ref, xs_hbm, wg_hbm,
                    wu_hbm, wd_hbm, ys_hbm, x_buf, y_buf, wg_buf, wu_buf, wd_buf, wg_bf, wu_bf, wd_bf,
                    sems, x_sems, y_sems):
    i = pl.program_id(0)
    n_act = n_act_ref[0]
    bf16 = jnp.bfloat16
    blk = EXPERT_ROWS * ROW_TILE

    def x_copy(j):
        n = pl.multiple_of((valid_ref[j] + 7) // 8 * 8, 8)
        src = xs_hbm.at[pl.ds(pl.multiple_of(j * EXPERT_ROWS, EXPERT_ROWS), n)]
        return pltpu.make_async_copy(src, x_buf.at[j % X_SLOTS, pl.ds(0, n)], x_sems.at[j % X_SLOTS])

    def y_copy(j):
        n = pl.multiple_of(valid_ref[j] * ROW_TILE, ROW_TILE)
        dst = ys_hbm.at[pl.ds(pl.multiple_of(j * blk, blk), n)]
        return pltpu.make_async_copy(y_buf.at[j % Y_SLOTS, pl.ds(0, n)], dst, y_sems.at[j % Y_SLOTS])

    @pl.when(i == 0)
    def _prime():
        x_buf[...] = jnp.zeros_like(x_buf)
        for j in range(X_SLOTS - 1):
            @pl.when(j < n_act)
            def _():
                x_copy(j).start()

    @pl.when(i + (X_SLOTS - 1) < n_act)
    def _prefetch():
        x_copy(i + (X_SLOTS - 1)).start()

    def weight_copies(e, slot):
        return (pltpu.make_async_copy(wg_hbm.at[e], wg_buf.at[slot], sems.at[slot]),
                pltpu.make_async_copy(wu_hbm.at[e], wu_buf.at[slot], sems.at[slot]),
                pltpu.make_async_copy(wd_hbm.at[e], wd_buf.at[slot], sems.at[slot]))

    @pl.when((i < n_act_ref[0]) & (first_ref[i] == 1))
    def _new_expert():
        slot = slot_ref[i]

        @pl.when(i == 0)
        def _():
            for c in weight_copies(blk_e_ref[0], 0):
                c.start()

        for c in weight_copies(blk_e_ref[i], slot):
            c.wait()

        @pl.when(next_e_ref[i] >= 0)
        def _():
            for c in weight_copies(next_e_ref[i], 1 - slot):
                c.start()

        wg_bf[...] = wg_buf[slot].astype(bf16)
        wu_bf[...] = wu_buf[slot].astype(bf16)
        wd_bf[...] = wd_buf[slot].astype(bf16)

    @pl.when(i < n_act)
    def _compute():
        x_copy(i).wait()

        @pl.when(i >= Y_SLOTS)
        def _():
            y_copy(i - Y_SLOTS).wait()

        sub = EXPERT_SUBROWS
        n_parts = (valid_ref[i] + sub - 1) // sub

        def run(parts):
            xs_ = [_unpack_bf16_pairs(x_buf[i % X_SLOTS, pl.ds(part * sub, sub), :]) for part in range(parts)]
            gs = [_dot(x, wg_bf[...]) for x in xs_]
            us = [_dot(x, wu_bf[...]) for x in xs_]
            acts = [(_silu(g) * u).astype(bf16) for g, u in zip(gs, us)]
            for part in range(parts):
                _store_rows(y_buf, _dot(acts[part], wd_bf[...]), lead=(i % Y_SLOTS,), first_row=part * sub)

        for parts in range(1, EXPERT_ROWS // sub + 1):
            pl.when(n_parts == parts)(functools.partial(run, parts))
        y_copy(i).start()

    @pl.when(i == n_act - 1)
    def _drain():
        for d in range(Y_SLOTS):
            @pl.when(i - d >= 0)
            def _():
                y_copy(i - d).wait()


def _experts(xs, blk_e, n_act, row_start, cnt, w_gate, w_up, w_down):
    D = D_MODEL
    BM = EXPERT_ROWS
    F = EXPERT_DIM
    n_blocks = xs.shape[0] // BM
    blk_in_expert = jnp.arange(n_blocks, dtype=jnp.int32) - row_start[blk_e] // BM
    valid = jnp.clip(cnt[blk_e] - blk_in_expert * BM, 0, BM).astype(jnp.int32)
    ids = jnp.arange(n_blocks, dtype=jnp.int32)
    active = ids < n_act[0]
    first = active & ((ids == 0) | (blk_e != jnp.roll(blk_e, 1)))
    slot = ((jnp.cumsum(first.astype(jnp.int32)) - 1) % 2).astype(jnp.int32)
    first_pos = jnp.where(first, ids, n_blocks)
    later_first = lax.cummin(jnp.concatenate([first_pos[1:], jnp.full((1,), n_blocks, jnp.int32)]), reverse=True)
    next_e = jnp.where(later_first < n_blocks, blk_e[jnp.minimum(later_first, n_blocks - 1)], -1).astype(jnp.int32)

    grid_spec = pltpu.PrefetchScalarGridSpec(
        num_scalar_prefetch=6,
        grid=(n_blocks,),
        in_specs=[pl.BlockSpec(memory_space=pl.ANY)] * 4,
        out_specs=pl.BlockSpec(memory_space=pl.ANY),
        scratch_shapes=[
            pltpu.VMEM((X_SLOTS, BM, D // 2), jnp.uint32),
            pltpu.VMEM((Y_SLOTS, BM * ROW_TILE, LANES), jnp.float32),
            pltpu.VMEM((2, D, F), jnp.float32), pltpu.VMEM((2, D, F), jnp.float32),
            pltpu.VMEM((2, F, D), jnp.float32),
            pltpu.VMEM((D, F), jnp.bfloat16), pltpu.VMEM((D, F), jnp.bfloat16),
            pltpu.VMEM((F, D), jnp.bfloat16),
            pltpu.SemaphoreType.DMA((2,)), pltpu.SemaphoreType.DMA((X_SLOTS,)),
            pltpu.SemaphoreType.DMA((Y_SLOTS,)),
        ],
    )
    return pl.pallas_call(
        _experts_kernel,
        grid_spec=grid_spec,
        out_shape=jax.ShapeDtypeStruct((xs.shape[0] * ROW_TILE, LANES), jnp.float32),
        compiler_params=pltpu.CompilerParams(
            dimension_semantics=("arbitrary",), vmem_limit_bytes=VMEM_LIMIT_BYTES),
    )(blk_e, first.astype(jnp.int32), slot, next_e, valid, n_act, xs, w_gate, w_up, w_down)


def _finish_kernel(h_ref, part_ref, w_ref, *rest):
    slabs = rest[:SC_COMBINE_SLOTS]
    g_ref, b_ref, out_ref = rest[SC_COMBINE_SLOTS:]
    R = h_ref.shape[0]
    w = w_ref[...]
    ffn = part_ref[...]
    for k in range(SC_COMBINE_SLOTS):
        ffn = ffn + _load_rows(slabs[k], R) * w[:, k:k + 1]
    out_ref[...] = _layer_norm(DEEPNORM_ALPHA * h_ref[...] + ffn, g_ref[...], b_ref[...])


def _finish(h2, partial, top_w, gathered, ln_g, ln_b):
    T, D = h2.shape
    R = FINISH_ROWS
    rows = pl.BlockSpec((R, D), lambda i: (i, 0))
    vec = pl.BlockSpec((1, D), lambda i: (0, 0))
    slab = lambda k: pl.BlockSpec((R * ROW_TILE, LANES), lambda i: (k * (T // R) + i, 0))
    return pl.pallas_call(
        _finish_kernel,
        grid=(T // R,),
        in_specs=[rows, rows, pl.BlockSpec((R, TOP_K), lambda i: (i, 0))]
        + [slab(k) for k in range(SC_COMBINE_SLOTS)] + [vec, vec],
        out_specs=rows,
        out_shape=jax.ShapeDtypeStruct((T, D), jnp.float32),
        compiler_params=pltpu.CompilerParams(
            dimension_semantics=("arbitrary",), vmem_limit_bytes=VMEM_LIMIT_BYTES),
    )(h2, partial, top_w, *([gathered] * SC_COMBINE_SLOTS), ln_g.reshape(1, D), ln_b.reshape(1, D))


SC_COMBINE_SLOTS = 5


def _combine_kernel(*refs):
    n_tc = TOP_K - SC_COMBINE_SLOTS
    pos_refs, pos_next_refs = refs[:n_tc], refs[n_tc:2 * n_tc]
    h_ref, w_ref, ys_ref, wsg_ref, wsu_ref, wsd_ref, out_ref, buf_ref, sems = refs[2 * n_tc:]
    R = h_ref.shape[0]
    i = pl.program_id(0)
    slot = i % 2

    def gather(p_refs, s):
        def issue(t, carry):
            for j in range(n_tc):
                pltpu.make_async_copy(ys_ref.at[_row_tile(p_refs[j][t])], buf_ref.at[s, j, _row_tile(t)],
                                      sems.at[s]).start(priority=j % 2)
            return carry

        lax.fori_loop(0, R, issue, 0)

    @pl.when(i == 0)
    def _():
        gather(pos_refs, 0)

    @pl.when(i + 1 < pl.num_programs(0))
    def _():
        gather(pos_next_refs, 1 - slot)

    h = h_ref[...]
    hb = h.astype(jnp.bfloat16)
    act = (_silu(_dot(hb, wsg_ref[...])) * _dot(hb, wsu_ref[...])).astype(jnp.bfloat16)
    ffn = _dot(act, wsd_ref[...])
    pltpu.make_async_copy(buf_ref.at[slot], buf_ref.at[slot], sems.at[slot]).wait()
    w = w_ref[...]
    for k in range(SC_COMBINE_SLOTS, TOP_K):
        ffn = ffn + _load_rows(buf_ref, R, lead=(slot, k - SC_COMBINE_SLOTS)) * w[:, k:k + 1]
    out_ref[...] = ffn


def _combine(h2, pos_kt, top_w, ys, ws_gate, ws_up, ws_down):
    T, D = h2.shape
    R = COMBINE_ROWS
    F = SHARED_DIM
    bf16 = jnp.bfloat16
    n_steps = T // R
    slots = range(SC_COMBINE_SLOTS, TOP_K)
    const = lambda shape: pl.BlockSpec(shape, lambda i: (0,) * len(shape))
    pos_now = [pl.BlockSpec((R,), lambda i, k=k: (k * n_steps + i,), memory_space=pltpu.SMEM) for k in slots]
    pos_next = [pl.BlockSpec((R,), lambda i, k=k: (k * n_steps + jnp.minimum(i + 1, n_steps - 1),),
                             memory_space=pltpu.SMEM) for k in slots]
    pos_flat = pos_kt.reshape(TOP_K * T)
    return pl.pallas_call(
        _combine_kernel,
        grid=(n_steps,),
        in_specs=pos_now + pos_next + [
            pl.BlockSpec((R, D), lambda i: (i, 0)),
            pl.BlockSpec((R, TOP_K), lambda i: (i, 0)),
            pl.BlockSpec(memory_space=pl.ANY),
            const((D, F)), const((D, F)), const((F, D)),
        ],
        out_specs=pl.BlockSpec((R, D), lambda i: (i, 0)),
        scratch_shapes=[pltpu.VMEM((2, TOP_K - SC_COMBINE_SLOTS, R * ROW_TILE, LANES), jnp.float32),
                        pltpu.SemaphoreType.DMA((2,))],
        out_shape=jax.ShapeDtypeStruct((T, D), jnp.float32),
        compiler_params=pltpu.CompilerParams(
            dimension_semantics=("arbitrary",), vmem_limit_bytes=VMEM_LIMIT_BYTES),
    )(*([pos_flat] * (2 * len(slots))), h2, top_w, ys, ws_gate.astype(bf16), ws_up.astype(bf16),
      ws_down.astype(bf16))


def _moe(h2, h2_packed, w_router, router_bias, w_gate, w_up, w_down, ws_gate, ws_up, ws_down, ln_g, ln_b):
    T = h2.shape[0]
    E = N_EXPERTS
    BM = EXPERT_ROWS
    e_idx, top_w, rank, counts = _router(h2, w_router, router_bias)
    cnt = counts[:, 0].astype(jnp.int32)
    nblk = (cnt + BM - 1) // BM
    blk_end = jnp.cumsum(nblk)
    row_start = ((blk_end - nblk) * BM).astype(jnp.int32)
    n_blocks = T * TOP_K // BM + E
    n_act = blk_end[-1:].astype(jnp.int32)
    blk_ids = jnp.minimum(jnp.arange(n_blocks, dtype=jnp.int32), n_act[0] - 1)
    blk_e = jnp.minimum(jnp.sum(blk_end[None, :] <= blk_ids[:, None], axis=1), E - 1).astype(jnp.int32)
    pos_kt = _positions(e_idx, rank, row_start).reshape(TOP_K, T)
    xs = _dispatch_sc(h2_packed, pos_kt, n_blocks * BM)
    ys = _experts(xs, blk_e, n_act, row_start, cnt, w_gate, w_up, w_down)
    gathered = _gather_rows_sc(ys, pos_kt[:SC_COMBINE_SLOTS])
    w_tk = top_w.T
    partial = _combine(h2, pos_kt, w_tk, ys, ws_gate, ws_up, ws_down)
    return _finish(h2, partial, w_tk, gathered, ln_g, ln_b)


def kernel(x, ln_in_g, ln_in_b, w_in, w_out, rel_bias, attn_sinks, ln_mix_g, ln_mix_b, w_router,
           router_bias, w_gate, w_up, w_down, ws_gate, ws_up, ws_down, ln_ffn_g, ln_ffn_b):
    B, S, D = x.shape
    h, h_packed = _mixer(x, ln_in_g, ln_in_b, w_in[0], w_out[0], rel_bias, attn_sinks[0], ln_mix_g[0],
                         ln_mix_b[0])
    out = _moe(h, h_packed, w_router[0], router_bias[0], w_gate[0], w_up[0], w_down[0],
               ws_gate[0], ws_up[0], ws_down[0], ln_ffn_g[0], ln_ffn_b[0])
    return out.reshape(B, S, D)
```

```python
import functools
import math

import jax
import jax.numpy as jnp
from jax import lax
from jax.experimental import pallas as pl
from jax.experimental.pallas import tpu as pltpu
from jax.experimental.pallas import tpu_sc as plsc

D_MODEL = 1024
DEPTH = 1
RET_HEADS = 4
RET_QK_DIM = 64
RET_V_DIM = 128
RET_CHUNK = 128
RET_WIDTH = RET_HEADS * RET_V_DIM
ROPE_BASE = 10000.0
SWA_HEADS = 8
SWA_KV_HEADS = 2
SWA_GROUP = SWA_HEADS // SWA_KV_HEADS
SWA_HEAD_DIM = 64
SWA_WINDOW = 128
SWA_WIDTH = SWA_HEADS * SWA_HEAD_DIM
MIX_WIDTH = RET_WIDTH + SWA_WIDTH
RQK = RET_HEADS * RET_QK_DIM
SKV = SWA_KV_HEADS * SWA_HEAD_DIM
IN_SIZES = (RQK, RQK, RET_WIDTH, RET_WIDTH, SWA_WIDTH, SKV, SKV)
IN_OFFS = tuple(sum(IN_SIZES[:i]) for i in range(len(IN_SIZES)))
IN_WIDTH = sum(IN_SIZES)
REL_BUCKETS = 32
REL_MAX_DIST = 128
N_EXPERTS = 256
TOP_K = 8
N_GROUPS = 8
GROUP_SIZE = N_EXPERTS // N_GROUPS
TOPK_GROUPS = 4
EXPERT_DIM = 256
SHARED_DIM = 256
ROUTED_SCALE = 2.5
LN_EPS = 1e-5
GN_EPS = 1e-6
DEEPNORM_ALPHA = (2 * DEPTH) ** 0.25
MASK_VALUE = -1e30

VMEM_LIMIT_BYTES = 56 * 1024 * 1024

MIX_ROWS = 512
ROUTE_ROWS = 256
EXPERT_ROWS = 512
COMBINE_ROWS = 256
POSITION_ROWS = 64
FINISH_ROWS = 512


def _layer_norm(x, g, b):
    mu = jnp.mean(x, axis=-1, keepdims=True)
    xc = x - mu
    var = jnp.mean(xc * xc, axis=-1, keepdims=True)
    return xc * lax.rsqrt(var + LN_EPS) * g + b


def _dot(a, b):
    return jnp.dot(a, b, preferred_element_type=jnp.float32)


def _dot_nt(a, b):
    return lax.dot_general(a, b, (((1,), (1,)), ((), ())), preferred_element_type=jnp.float32)


def _dot_tn(a, b):
    return lax.dot_general(a, b, (((0,), (0,)), ((), ())), preferred_element_type=jnp.float32)


def _silu(x):
    return x * (1.0 / (1.0 + jnp.exp(-x)))


LANES = 128
ROW_TILE = D_MODEL // LANES


def _load_rows(ref, n_rows, lead=()):
    return jnp.concatenate([ref[lead + (pl.ds(s, n_rows, stride=ROW_TILE), slice(None))]
                            for s in range(ROW_TILE)], axis=1)


def _store_rows(ref, val, lead=(), first_row=0):
    n_rows = val.shape[0]
    for s in range(ROW_TILE):
        dst = pl.ds(first_row * ROW_TILE + s, n_rows, stride=ROW_TILE)
        ref[lead + (dst, slice(None))] = val[:, s * LANES:(s + 1) * LANES]


def _row_tile(r):
    return pl.ds(pl.multiple_of(r * ROW_TILE, ROW_TILE), ROW_TILE)


def _pack_bf16_pairs(x):
    m = x.shape[1] // 2
    bits = lax.bitcast_convert_type(x.astype(jnp.bfloat16).astype(jnp.float32), jnp.uint32)
    return (bits[:, :m] >> 16) | (bits[:, m:] & jnp.uint32(0xFFFF0000))


def _unpack_bf16_pairs(p):
    lo = lax.bitcast_convert_type(p << 16, jnp.float32)
    hi = lax.bitcast_convert_type(p & jnp.uint32(0xFFFF0000), jnp.float32)
    return jnp.concatenate([lo, hi], axis=1).astype(jnp.bfloat16)


def _swap_halves(x):
    n = x.shape[-1]
    half = RET_QK_DIM // 2
    lane = lax.broadcasted_iota(jnp.int32, x.shape, 1)
    from_right = pltpu.roll(x, n - half, axis=1)
    from_left = pltpu.roll(x, half, axis=1)
    return jnp.where((lane % RET_QK_DIM) < half, from_right, from_left)


def _mixer_kernel(rel_bias_ref, x_ref, g_in_ref, b_in_ref, w_in_ref, w_out_ref, rot_ref, decay_ref,
                  zeta_ref, xi_ref, cdecay_ref, bucket_ref, sink_ref, g_mix_ref, b_mix_ref,
                  h2_ref, h2p_ref, state_ref, kprev_ref, vprev_ref, bias_ref):
    b_id = pl.program_id(0)
    c_id = pl.program_id(1)
    W = SWA_WINDOW

    @pl.when((b_id == 0) & (c_id == 0))
    def _build_bias():
        bucket = bucket_ref[...]
        for h in range(SWA_HEADS):
            acc = jnp.full((2 * W, W), MASK_VALUE, jnp.float32)
            for b in range(REL_BUCKETS):
                acc = jnp.where(bucket == b, rel_bias_ref[b, h], acc)
            kh, g = divmod(h, SWA_GROUP)
            bias_ref[kh, :, g * W:(g + 1) * W] = acc

    @pl.when(c_id == 0)
    def _reset():
        state_ref[...] = jnp.zeros_like(state_ref)
        kprev_ref[...] = jnp.zeros_like(kprev_ref)
        vprev_ref[...] = jnp.zeros_like(vprev_ref)

    h = _layer_norm(x_ref[...], g_in_ref[...], b_in_ref[...])
    proj = _dot(h.astype(jnp.bfloat16), w_in_ref[...])

    o_q, o_k, o_v, o_g, o_sq, o_sk, o_sv = IN_OFFS
    cos_t = rot_ref[:, :RQK]
    sin_t = rot_ref[:, RQK:]
    q_all = proj[:, o_q:o_q + RQK]
    k_all = proj[:, o_k:o_k + RQK]
    q_rot = q_all * cos_t + _swap_halves(q_all) * sin_t
    k_rot = (k_all * cos_t + _swap_halves(k_all) * sin_t) * (RET_QK_DIM ** -0.5)

    n_sub = x_ref.shape[0] // RET_CHUNK
    states = [state_ref[hh] for hh in range(RET_HEADS)]
    k_prev = kprev_ref[...]
    v_prev = vprev_ref[...]
    bf16 = jnp.bfloat16
    heads = range(RET_HEADS)
    ret_pieces, swa_pieces = [], []
    swa_jobs = []
    for s in range(n_sub):
        rows = slice(s * RET_CHUNK, (s + 1) * RET_CHUNK)
        k_cur = proj[rows, o_sk:o_sk + SKV].astype(bf16)
        v_cur = proj[rows, o_sv:o_sv + SKV].astype(bf16)
        for kh in range(SWA_KV_HEADS):
            kv = slice(kh * SWA_HEAD_DIM, (kh + 1) * SWA_HEAD_DIM)
            q4 = jnp.concatenate(
                [proj[rows, o_sq + (kh * SWA_GROUP + g) * SWA_HEAD_DIM:
                      o_sq + (kh * SWA_GROUP + g + 1) * SWA_HEAD_DIM] for g in range(SWA_GROUP)],
                axis=0) * (SWA_HEAD_DIM ** -0.5)
            kcat = jnp.concatenate([k_prev[:, kv], k_cur[:, kv]], axis=0)
            vcat = jnp.concatenate([v_prev[:, kv], v_cur[:, kv]], axis=0)
            swa_jobs.append((s, kh, q4.astype(bf16), kcat, vcat))
        k_prev, v_prev = k_cur, v_cur
    logits_all = [_dot_nt(kcat, q4) + bias_ref[kh] for (s, kh, q4, kcat, vcat) in swa_jobs]
    probs_all = []
    for (s, kh, q4, kcat, vcat), logits in zip(swa_jobs, logits_all):
        if s == 0:
            key = lax.broadcasted_iota(jnp.int32, logits.shape, 0)
            logits = logits + jnp.where((key < W) & (c_id == 0), MASK_VALUE, 0.0)
        sink = sink_ref[kh]
        m = jnp.maximum(jnp.max(logits, axis=0, keepdims=True), sink)
        p = jnp.exp(logits - m)
        den = jnp.sum(p, axis=0, keepdims=True) + jnp.exp(sink - m)
        probs_all.append((p / den).astype(bf16))
    o4_all = [_dot_tn(job[4], probs) for job, probs in zip(swa_jobs, probs_all)]
    for s in range(n_sub):
        swa_pieces.append([o4[:, g * W:(g + 1) * W].T.astype(bf16)
                           for job, o4 in zip(swa_jobs, o4_all) if job[0] == s for g in range(SWA_GROUP)])
    for s in range(n_sub):
        rows = slice(s * RET_CHUNK, (s + 1) * RET_CHUNK)
        qk = [slice(hh * RET_QK_DIM, (hh + 1) * RET_QK_DIM) for hh in heads]
        q = [q_rot[rows, qk[hh]].astype(bf16) for hh in heads]
        k32 = [k_rot[rows, qk[hh]] for hh in heads]
        v = [proj[rows, o_v + hh * RET_V_DIM:o_v + (hh + 1) * RET_V_DIM].astype(bf16) for hh in heads]
        scores = [_dot_nt(q[hh], k32[hh].astype(bf16)) * decay_ref[hh] for hh in heads]
        inter = [_dot(q[hh], states[hh].astype(bf16)) * xi_ref[hh] for hh in heads]
        kv_new = [_dot_tn((k32[hh] * zeta_ref[hh]).astype(bf16), v[hh]) for hh in heads]
        intra = [_dot(scores[hh].astype(bf16), v[hh]) for hh in heads]
        states = [states[hh] * cdecay_ref[hh] + kv_new[hh] for hh in heads]
        pieces = []
        for hh in heads:
            ret = intra[hh] + inter[hh]
            mu = jnp.mean(ret, axis=-1, keepdims=True)
            rc = ret - mu
            var = jnp.mean(rc * rc, axis=-1, keepdims=True)
            normed = rc * lax.rsqrt(var + GN_EPS)
            gate = proj[rows, o_g + hh * RET_V_DIM:o_g + (hh + 1) * RET_V_DIM]
            pieces.append((_silu(gate) * normed).astype(bf16))
        ret_pieces.append(pieces)
    cat_rows = [jnp.concatenate(ret_pieces[s] + swa_pieces[s], axis=1) for s in range(n_sub)]
    for hh in range(RET_HEADS):
        state_ref[hh] = states[hh]
    kprev_ref[...] = k_prev
    vprev_ref[...] = v_prev

    mix = _dot(jnp.concatenate(cat_rows, axis=0), w_out_ref[...])
    h2 = _layer_norm(DEEPNORM_ALPHA * h + mix, g_mix_ref[...], b_mix_ref[...])
    h2_ref[...] = h2
    h2p_ref[...] = _pack_bf16_pairs(h2)


def _t5_bucket(dist):
    n = jnp.maximum(dist, 0)
    max_exact = REL_BUCKETS // 2
    ratio = jnp.log(jnp.maximum(n, 1).astype(jnp.float32) / max_exact) / math.log(REL_MAX_DIST / max_exact)
    large = jnp.minimum(max_exact + (ratio * (REL_BUCKETS - max_exact)).astype(jnp.int32), REL_BUCKETS - 1)
    return jnp.where(n < max_exact, n, large)


def _mixer(x, ln_in_g, ln_in_b, w_in, w_out, rel_bias, sinks, ln_mix_g, ln_mix_b):
    B, S, D = x.shape
    R = MIX_ROWS
    C = RET_CHUNK
    W = SWA_WINDOW
    f32 = jnp.float32
    half = RET_QK_DIM // 2
    inv = ROPE_BASE ** (-jnp.arange(half, dtype=f32) / half)
    ang = jnp.arange(S, dtype=f32)[:, None] * inv[None, :]
    cos, sin = jnp.cos(ang), jnp.sin(ang)
    cos_t = jnp.tile(jnp.concatenate([cos, cos], axis=-1), (1, RET_HEADS))
    sin_t = jnp.tile(jnp.concatenate([-sin, sin], axis=-1), (1, RET_HEADS))
    rot = jnp.concatenate([cos_t, sin_t], axis=-1)
    log_gamma = jnp.log(1.0 - 2.0 ** (-5.0 - jnp.arange(RET_HEADS, dtype=f32)))
    idx = jnp.arange(C, dtype=f32)
    diff = idx[:, None] - idx[None, :]
    decay = jnp.where(diff[None] >= 0, jnp.exp(jnp.maximum(diff, 0.0)[None] * log_gamma[:, None, None]), 0.0)
    zeta = jnp.exp((C - 1.0 - idx)[None, :] * log_gamma[:, None])
    xi = jnp.exp((idx + 1.0)[None, :] * log_gamma[:, None])
    zeta_b = jnp.broadcast_to(zeta[:, :, None], (RET_HEADS, C, RET_QK_DIM))
    xi_b = jnp.broadcast_to(xi[:, :, None], (RET_HEADS, C, RET_V_DIM))
    cdecay = jnp.broadcast_to(jnp.exp(C * log_gamma)[:, None, None], (RET_HEADS, RET_QK_DIM, RET_V_DIM))
    i = jnp.arange(W)
    j = jnp.arange(2 * W)
    dist = i[:, None] + W - j[None, :]
    bucket = jnp.where((dist >= 0) & (dist < W), _t5_bucket(dist), -1).astype(jnp.int32).T
    sink_row = jnp.repeat(sinks.astype(f32), W).reshape(SWA_KV_HEADS, 1, SWA_GROUP * W)

    const = lambda shape: pl.BlockSpec(shape, lambda b, c, *_: (0,) * len(shape))
    grid_spec = pltpu.PrefetchScalarGridSpec(
        num_scalar_prefetch=1,
        grid=(B, S // R),
        in_specs=[
            pl.BlockSpec((None, R, D), lambda b, c, *_: (b, c, 0)),
            const((1, D)), const((1, D)),
            const((D, IN_WIDTH)), const((MIX_WIDTH, D)),
            pl.BlockSpec((R, 2 * RQK), lambda b, c, *_: (c, 0)),
            const((RET_HEADS, C, C)), const((RET_HEADS, C, RET_QK_DIM)), const((RET_HEADS, C, RET_V_DIM)),
            const((RET_HEADS, RET_QK_DIM, RET_V_DIM)),
            const((2 * W, W)), const((SWA_KV_HEADS, 1, SWA_GROUP * W)),
            const((1, D)), const((1, D)),
        ],
        out_specs=[pl.BlockSpec((R, D), lambda b, c, *_: (b * (S // R) + c, 0)),
                   pl.BlockSpec((R, D // 2), lambda b, c, *_: (b * (S // R) + c, 0))],
        scratch_shapes=[
            pltpu.VMEM((RET_HEADS, RET_QK_DIM, RET_V_DIM), f32),
            pltpu.VMEM((W, SKV), jnp.bfloat16),
            pltpu.VMEM((W, SKV), jnp.bfloat16),
            pltpu.VMEM((SWA_KV_HEADS, 2 * W, SWA_GROUP * W), f32),
        ],
    )
    return pl.pallas_call(
        _mixer_kernel,
        grid_spec=grid_spec,
        out_shape=[jax.ShapeDtypeStruct((B * S, D), f32), jax.ShapeDtypeStruct((B * S, D // 2), jnp.uint32)],
        compiler_params=pltpu.CompilerParams(
            dimension_semantics=("arbitrary", "arbitrary"), vmem_limit_bytes=VMEM_LIMIT_BYTES),
    )(rel_bias.astype(f32), x, ln_in_g.reshape(1, D), ln_in_b.reshape(1, D),
      w_in.astype(jnp.bfloat16), w_out.astype(jnp.bfloat16), rot, decay, zeta_b, xi_b, cdecay,
      bucket, sink_row, ln_mix_g.reshape(1, D), ln_mix_b.reshape(1, D))


def _router_kernel(h_ref, wr_ref, rb_ref, e_ref, w_ref, rk_ref, cnt_ref, run_ref):
    f32 = jnp.float32
    R = h_ref.shape[0]
    E = N_EXPERTS
    neg = -jnp.inf

    @pl.when(pl.program_id(0) == 0)
    def _init():
        run_ref[...] = jnp.zeros_like(run_ref)

    logits = _dot_nt(wr_ref[...], h_ref[...].astype(jnp.bfloat16))
    scores = 1.0 / (1.0 + jnp.exp(-logits))
    choice = scores + rb_ref[...]
    eid = lax.broadcasted_iota(jnp.int32, (E, R), 0)

    def first_argmax(vals, ids, none):
        m = jnp.max(vals, axis=0, keepdims=True)
        idx = jnp.min(jnp.where(vals == m, ids, none), axis=0, keepdims=True)
        return m, idx

    gid = lax.broadcasted_iota(jnp.int32, (GROUP_SIZE, R), 0)
    groups, gscore = [], []
    for g in range(N_GROUPS):
        vals = choice[g * GROUP_SIZE:(g + 1) * GROUP_SIZE]
        m1, i1 = first_argmax(vals, gid, GROUP_SIZE)
        m2 = jnp.max(jnp.where(gid == i1, neg, vals), axis=0, keepdims=True)
        groups.append(vals)
        gscore.append(m1 + m2)
    kept = []
    for g in range(N_GROUPS):
        beaten = jnp.zeros((1, R), f32)
        for g2 in range(N_GROUPS):
            if g2 == g:
                continue
            ahead = (gscore[g2] > gscore[g]) | (gscore[g2] == gscore[g]) if g2 < g else gscore[g2] > gscore[g]
            beaten = beaten + jnp.where(ahead, 1.0, 0.0)
        kept.append(jnp.where(beaten < TOPK_GROUPS, groups[g], neg))
    masked = jnp.concatenate(kept, axis=0)

    idxs, wts = [], []
    picked = jnp.zeros((E, R), f32)
    for _ in range(TOP_K):
        _, idx = first_argmax(masked, eid, E)
        hit = eid == idx
        idxs.append(idx)
        wts.append(jnp.sum(jnp.where(hit, scores, 0.0), axis=0, keepdims=True))
        masked = jnp.where(hit, neg, masked)
        picked = jnp.where(hit, 1.0, picked)
    wsum = wts[0]
    for k in range(1, TOP_K):
        wsum = wsum + wts[k]

    row = lax.broadcasted_iota(jnp.int32, (R, R), 0)
    col = lax.broadcasted_iota(jnp.int32, (R, R), 1)
    earlier = jnp.where(row < col, 1.0, 0.0).astype(jnp.bfloat16)
    picked_bf = picked.astype(jnp.bfloat16)
    run = run_ref[...]
    before = _dot(picked_bf, earlier) + jnp.concatenate([run] * (R // LANES), axis=1)
    sub_k = lax.broadcasted_iota(jnp.int32, (TOP_K, R), 0)
    e_out = jnp.zeros((TOP_K, R), jnp.int32)
    w_out = jnp.zeros((TOP_K, R), f32)
    rk_out = jnp.zeros((TOP_K, R), jnp.int32)
    for k in range(TOP_K):
        rank_k = jnp.sum(jnp.where(eid == idxs[k], before, 0.0), axis=0, keepdims=True)
        e_out = jnp.where(sub_k == k, idxs[k], e_out)
        w_out = jnp.where(sub_k == k, wts[k] / wsum * ROUTED_SCALE, w_out)
        rk_out = jnp.where(sub_k == k, rank_k.astype(jnp.int32), rk_out)
    e_ref[...] = e_out
    w_ref[...] = w_out
    rk_ref[...] = rk_out
    run_ref[...] = run + _dot(picked_bf, jnp.ones((R, LANES), jnp.bfloat16))
    cnt_ref[...] = run_ref[...]


def _router(h2, w_router, router_bias):
    T, D = h2.shape
    R = ROUTE_ROWS
    E = N_EXPERTS
    return pl.pallas_call(
        _router_kernel,
        grid=(T // R,),
        in_specs=[
            pl.BlockSpec((R, D), lambda i: (i, 0)),
            pl.BlockSpec((E, D), lambda i: (0, 0)),
            pl.BlockSpec((E, R), lambda i: (0, 0)),
        ],
        out_specs=[
            pl.BlockSpec((TOP_K, R), lambda i: (0, i)),
            pl.BlockSpec((TOP_K, R), lambda i: (0, i)),
            pl.BlockSpec((TOP_K, R), lambda i: (0, i)),
            pl.BlockSpec((E, LANES), lambda i: (0, 0)),
        ],
        out_shape=[
            jax.ShapeDtypeStruct((TOP_K, T), jnp.int32),
            jax.ShapeDtypeStruct((TOP_K, T), jnp.float32),
            jax.ShapeDtypeStruct((TOP_K, T), jnp.int32),
            jax.ShapeDtypeStruct((E, LANES), jnp.float32),
        ],
        scratch_shapes=[pltpu.VMEM((E, LANES), jnp.float32)],
        compiler_params=pltpu.CompilerParams(
            dimension_semantics=("arbitrary",), vmem_limit_bytes=VMEM_LIMIT_BYTES),
    )(h2, w_router.T.astype(jnp.bfloat16), jnp.broadcast_to(router_bias.astype(jnp.float32)[:, None], (E, R)))


def _positions_kernel(row_start_ref, e_ref, rk_ref, pos_ref):
    e = e_ref[...]
    rk = rk_ref[...]

    def per_expert(i, pos):
        return jnp.where(e == i, rk + row_start_ref[i], pos)

    pos_ref[...] = lax.fori_loop(0, N_EXPERTS, per_expert, jnp.zeros_like(rk))


def _positions(e_idx, rank, row_start):
    n = e_idx.size
    shape = (n // LANES, LANES)
    block = pl.BlockSpec((POSITION_ROWS, LANES), lambda i, *_: (i, 0))
    grid_spec = pltpu.PrefetchScalarGridSpec(
        num_scalar_prefetch=1,
        grid=(shape[0] // POSITION_ROWS,),
        in_specs=[block, block],
        out_specs=block,
    )
    pos = pl.pallas_call(
        _positions_kernel,
        grid_spec=grid_spec,
        out_shape=jax.ShapeDtypeStruct(shape, jnp.int32),
    )(row_start, e_idx.reshape(shape), rank.reshape(shape))
    return pos.reshape(n)


SC_WINDOW = 32
SC_DISPATCH_WINDOW = 64


def _dispatch_sc(h2, pos_kt, n_rows):
    T, D = h2.shape
    W = SC_DISPATCH_WINDOW
    idx = _window_indices(pos_kt, W)
    idx_rows = TOP_K * W // LANES
    mesh = plsc.VectorSubcoreMesh(core_axis_name="core", subcore_axis_name="subcore")

    @pl.kernel(out_type=jax.ShapeDtypeStruct((n_rows, D), h2.dtype), mesh=mesh, scratch_types=[])
    def scatter_rows(x_hbm, i_hbm, o_hbm):
        def body(x_vmem, i_vmem):
            for k in range(TOP_K):
                r, q = divmod(k * W, LANES)
                pltpu.sync_copy(x_vmem, o_hbm.at[i_vmem.at[r, pl.ds(q, W)]])

        pltpu.emit_pipeline(
            body,
            grid=(T // W,),
            in_specs=[pl.BlockSpec((W, D), lambda i: (i, 0)),
                      pl.BlockSpec((idx_rows, LANES), lambda i: (i, 0))],
            out_specs=[],
            core_axis_name=("core", "subcore"),
            dimension_semantics=(pltpu.PARALLEL,),
        )(x_hbm, i_hbm)

    return scatter_rows(h2, idx)


def _window_indices(pos_kt, window):
    K, T = pos_kt.shape
    return pos_kt.reshape(K, T // window, window).transpose(1, 0, 2).reshape(T * K // LANES, LANES)


def _gather_rows_sc(ys, pos_kt):
    n = pos_kt.size
    W = SC_WINDOW
    ys3 = ys.reshape(ys.shape[0] // ROW_TILE, ROW_TILE, LANES)
    idx = jnp.pad(pos_kt.reshape(n // W, W), ((0, 0), (0, LANES - W)))
    mesh = plsc.VectorSubcoreMesh(core_axis_name="core", subcore_axis_name="subcore")

    @pl.kernel(out_type=jax.ShapeDtypeStruct((n, ROW_TILE, LANES), ys.dtype), mesh=mesh, scratch_types=[])
    def gather_rows(y_hbm, i_hbm, o_hbm):
        def body(i_vmem, o_vmem):
            pltpu.sync_copy(y_hbm.at[i_vmem.at[0, pl.ds(0, W)]], o_vmem)

        pltpu.emit_pipeline(
            body,
            grid=(n // W,),
            in_specs=[pl.BlockSpec((1, LANES), lambda i: (i, 0))],
            out_specs=[pl.BlockSpec((W, ROW_TILE, LANES), lambda i: (i, 0, 0))],
            core_axis_name=("core", "subcore"),
            dimension_semantics=(pltpu.PARALLEL,),
        )(i_hbm, o_hbm)

    return gather_rows(ys3, idx).reshape(n * ROW_TILE, LANES)


EXPERT_SUBROWS = 128
X_SLOTS = 4
Y_SLOTS = 3


def _experts_kernel(blk_e_ref, first_ref, slot_ref, next_e_ref, valid_ref, n_act_ref, xs_hbm, wg_hbm,
                    wu_hbm, wd_hbm, ys_hbm, x_buf, y_buf, wg_buf, wu_buf, wd_buf, wg_bf, wu_bf, wd_bf,
                    sems, x_sems, y_sems):
    i = pl.program_id(0)
    n_act = n_act_ref[0]
    bf16 = jnp.bfloat16
    blk = EXPERT_ROWS * ROW_TILE

    def x_copy(j):
        n = pl.multiple_of((valid_ref[j] + 7) // 8 * 8, 8)
        src = xs_hbm.at[pl.ds(pl.multiple_of(j * EXPERT_ROWS, EXPERT_ROWS), n)]
        return pltpu.make_async_copy(src, x_buf.at[j % X_SLOTS, pl.ds(0, n)], x_sems.at[j % X_SLOTS])

    def y_copy(j):
        n = pl.multiple_of(valid_ref[j] * ROW_TILE, ROW_TILE)
        dst = ys_hbm.at[pl.ds(pl.multiple_of(j * blk, blk), n)]
        return pltpu.make_async_copy(y_buf.at[j % Y_SLOTS, pl.ds(0, n)], dst, y_sems.at[j % Y_SLOTS])

    @pl.when(i == 0)
    def _prime():
        x_buf[...] = jnp.zeros_like(x_buf)
        for j in range(X_SLOTS - 1):
            @pl.when(j < n_act)
            def _():
                x_copy(j).start()

    @pl.when(i + (X_SLOTS - 1) < n_act)
    def _prefetch():
        x_copy(i + (X_SLOTS - 1)).start()

    def weight_copies(e, slot):
        return (pltpu.make_async_copy(wg_hbm.at[e], wg_buf.at[slot], sems.at[slot]),
                pltpu.make_async_copy(wu_hbm.at[e], wu_buf.at[slot], sems.at[slot]),
                pltpu.make_async_copy(wd_hbm.at[e], wd_buf.at[slot], sems.at[slot]))

    @pl.when((i < n_act_ref[0]) & (first_ref[i] == 1))
    def _new_expert():
        slot = slot_ref[i]

        @pl.when(i == 0)
        def _():
            for c in weight_copies(blk_e_ref[0], 0):
                c.start()

        for c in weight_copies(blk_e_ref[i], slot):
            c.wait()

        @pl.when(next_e_ref[i] >= 0)
        def _():
            for c in weight_copies(next_e_ref[i], 1 - slot):
                c.start()

        wg_bf[...] = wg_buf[slot].astype(bf16)
        wu_bf[...] = wu_buf[slot].astype(bf16)
        wd_bf[...] = wd_buf[slot].astype(bf16)

    @pl.when(i < n_act)
    def _compute():
        x_copy(i).wait()

        @pl.when(i >= Y_SLOTS)
        def _():
            y_copy(i - Y_SLOTS).wait()

        sub = EXPERT_SUBROWS
        n_parts = (valid_ref[i] + sub - 1) // sub

        def run(parts):
            xs_ = [_unpack_bf16_pairs(x_buf[i % X_SLOTS, pl.ds(part * sub, sub), :]) for part in range(parts)]
            gs = [_dot(x, wg_bf[...]) for x in xs_]
            us = [_dot(x, wu_bf[...]) for x in xs_]
            acts = [(_silu(g) * u).astype(bf16) for g, u in zip(gs, us)]
            for part in range(parts):
                _store_rows(y_buf, _dot(acts[part], wd_bf[...]), lead=(i % Y_SLOTS,), first_row=part * sub)

        for parts in range(1, EXPERT_ROWS // sub + 1):
            pl.when(n_parts == parts)(functools.partial(run, parts))
        y_copy(i).start()

    @pl.when(i == n_act - 1)
    def _drain():
        for d in range(Y_SLOTS):
            @pl.when(i - d >= 0)
            def _():
                y_copy(i - d).wait()


def _experts(xs, blk_e, n_act, row_start, cnt, w_gate, w_up, w_down):
    D = D_MODEL
    BM = EXPERT_ROWS
    F = EXPERT_DIM
    n_blocks = xs.shape[0] // BM
    blk_in_expert = jnp.arange(n_blocks, dtype=jnp.int32) - row_start[blk_e] // BM
    valid = jnp.clip(cnt[blk_e] - blk_in_expert * BM, 0, BM).astype(jnp.int32)
    ids = jnp.arange(n_blocks, dtype=jnp.int32)
    active = ids < n_act[0]
    first = active & ((ids == 0) | (blk_e != jnp.roll(blk_e, 1)))
    slot = ((jnp.cumsum(first.astype(jnp.int32)) - 1) % 2).astype(jnp.int32)
    first_pos = jnp.where(first, ids, n_blocks)
    later_first = lax.cummin(jnp.concatenate([first_pos[1:], jnp.full((1,), n_blocks, jnp.int32)]), reverse=True)
    next_e = jnp.where(later_first < n_blocks, blk_e[jnp.minimum(later_first, n_blocks - 1)], -1).astype(jnp.int32)

    grid_spec = pltpu.PrefetchScalarGridSpec(
        num_scalar_prefetch=6,
        grid=(n_blocks,),
        in_specs=[pl.BlockSpec(memory_space=pl.ANY)] * 4,
        out_specs=pl.BlockSpec(memory_space=pl.ANY),
        scratch_shapes=[
            pltpu.VMEM((X_SLOTS, BM, D // 2), jnp.uint32),
            pltpu.VMEM((Y_SLOTS, BM * ROW_TILE, LANES), jnp.float32),
            pltpu.VMEM((2, D, F), jnp.float32), pltpu.VMEM((2, D, F), jnp.float32),
            pltpu.VMEM((2, F, D), jnp.float32),
            pltpu.VMEM((D, F), jnp.bfloat16), pltpu.VMEM((D, F), jnp.bfloat16),
            pltpu.VMEM((F, D), jnp.bfloat16),
            pltpu.SemaphoreType.DMA((2,)), pltpu.SemaphoreType.DMA((X_SLOTS,)),
            pltpu.SemaphoreType.DMA((Y_SLOTS,)),
        ],
    )
    return pl.pallas_call(
        _experts_kernel,
        grid_spec=grid_spec,
        out_shape=jax.ShapeDtypeStruct((xs.shape[0] * ROW_TILE, LANES), jnp.float32),
        compiler_params=pltpu.CompilerParams(
            dimension_semantics=("arbitrary",), vmem_limit_bytes=VMEM_LIMIT_BYTES),
    )(blk_e, first.astype(jnp.int32), slot, next_e, valid, n_act, xs, w_gate, w_up, w_down)


def _finish_kernel(h_ref, part_ref, w_ref, *rest):
    slabs = rest[:SC_COMBINE_SLOTS]
    g_ref, b_ref, out_ref = rest[SC_COMBINE_SLOTS:]
    R = h_ref.shape[0]
    w = w_ref[...]
    ffn = part_ref[...]
    for k in range(SC_COMBINE_SLOTS):
        ffn = ffn + _load_rows(slabs[k], R) * w[:, k:k + 1]
    out_ref[...] = _layer_norm(DEEPNORM_ALPHA * h_ref[...] + ffn, g_ref[...], b_ref[...])


def _finish(h2, partial, top_w, gathered, ln_g, ln_b):
    T, D = h2.shape
    R = FINISH_ROWS
    rows = pl.BlockSpec((R, D), lambda i: (i, 0))
    vec = pl.BlockSpec((1, D), lambda i: (0, 0))
    slab = lambda k: pl.BlockSpec((R * ROW_TILE, LANES), lambda i: (k * (T // R) + i, 0))
    return pl.pallas_call(
        _finish_kernel,
        grid=(T // R,),
        in_specs=[rows, rows, pl.BlockSpec((R, TOP_K), lambda i: (i, 0))]
        + [slab(k) for k in range(SC_COMBINE_SLOTS)] + [vec, vec],
        out_specs=rows,
        out_shape=jax.ShapeDtypeStruct((T, D), jnp.float32),
        compiler_params=pltpu.CompilerParams(
            dimension_semantics=("arbitrary",), vmem_limit_bytes=VMEM_LIMIT_BYTES),
    )(h2, partial, top_w, *([gathered] * SC_COMBINE_SLOTS), ln_g.reshape(1, D), ln_b.reshape(1, D))


SC_COMBINE_SLOTS = 5


def _combine_kernel(*refs):
    n_tc = TOP_K - SC_COMBINE_SLOTS
    pos_refs, pos_next_refs = refs[:n_tc], refs[n_tc:2 * n_tc]
    h_ref, w_ref, ys_ref, wsg_ref, wsu_ref, wsd_ref, out_ref, buf_ref, sems = refs[2 * n_tc:]
    R = h_ref.shape[0]
    i = pl.program_id(0)
    slot = i % 2

    def gather(p_refs, s):
        def issue(t, carry):
            for j in range(n_tc):
                pltpu.make_async_copy(ys_ref.at[_row_tile(p_refs[j][t])], buf_ref.at[s, j, _row_tile(t)],
                                      sems.at[s]).start(priority=j % 2)
            return carry

        lax.fori_loop(0, R, issue, 0)

    @pl.when(i == 0)
    def _():
        gather(pos_refs, 0)

    @pl.when(i + 1 < pl.num_programs(0))
    def _():
        gather(pos_next_refs, 1 - slot)

    h = h_ref[...]
    hb = h.astype(jnp.bfloat16)
    act = (_silu(_dot(hb, wsg_ref[...])) * _dot(hb, wsu_ref[...])).astype(jnp.bfloat16)
    ffn = _dot(act, wsd_ref[...])
    pltpu.make_async_copy(buf_ref.at[slot], buf_ref.at[slot], sems.at[slot]).wait()
    w = w_ref[...]
    for k in range(SC_COMBINE_SLOTS, TOP_K):
        ffn = ffn + _load_rows(buf_ref, R, lead=(slot, k - SC_COMBINE_SLOTS)) * w[:, k:k + 1]
    out_ref[...] = ffn


def _combine(h2, pos_kt, top_w, ys, ws_gate, ws_up, ws_down):
    T, D = h2.shape
    R = COMBINE_ROWS
    F = SHARED_DIM
    bf16 = jnp.bfloat16
    n_steps = T // R
    slots = range(SC_COMBINE_SLOTS, TOP_K)
    const = lambda shape: pl.BlockSpec(shape, lambda i: (0,) * len(shape))
    pos_now = [pl.BlockSpec((R,), lambda i, k=k: (k * n_steps + i,), memory_space=pltpu.SMEM) for k in slots]
    pos_next = [pl.BlockSpec((R,), lambda i, k=k: (k * n_steps + jnp.minimum(i + 1, n_steps - 1),),
                             memory_space=pltpu.SMEM) for k in slots]
    pos_flat = pos_kt.reshape(TOP_K * T)
    return pl.pallas_call(
        _combine_kernel,
        grid=(n_steps,),
        in_specs=pos_now + pos_next + [
            pl.BlockSpec((R, D), lambda i: (i, 0)),
            pl.BlockSpec((R, TOP_K), lambda i: (i, 0)),
            pl.BlockSpec(memory_space=pl.ANY),
            const((D, F)), const((D, F)), const((F, D)),
        ],
        out_specs=pl.BlockSpec((R, D), lambda i: (i, 0)),
        scratch_shapes=[pltpu.VMEM((2, TOP_K - SC_COMBINE_SLOTS, R * ROW_TILE, LANES), jnp.float32),
                        pltpu.SemaphoreType.DMA((2,))],
        out_shape=jax.ShapeDtypeStruct((T, D), jnp.float32),
        compiler_params=pltpu.CompilerParams(
            dimension_semantics=("arbitrary",), vmem_limit_bytes=VMEM_LIMIT_BYTES),
    )(*([pos_flat] * (2 * len(slots))), h2, top_w, ys, ws_gate.astype(bf16), ws_up.astype(bf16),
      ws_down.astype(bf16))


def _moe(h2, h2_packed, w_router, router_bias, w_gate, w_up, w_down, ws_gate, ws_up, ws_down, ln_g, ln_b):
    T = h2.shape[0]
    E = N_EXPERTS
    BM = EXPERT_ROWS
    e_idx, top_w, rank, counts = _router(h2, w_router, router_bias)
    cnt = counts[:, 0].astype(jnp.int32)
    nblk = (cnt + BM - 1) // BM
    blk_end = jnp.cumsum(nblk)
    row_start = ((blk_end - nblk) * BM).astype(jnp.int32)
    n_blocks = T * TOP_K // BM + E
    n_act = blk_end[-1:].astype(jnp.int32)
    blk_ids = jnp.minimum(jnp.arange(n_blocks, dtype=jnp.int32), n_act[0] - 1)
    blk_e = jnp.minimum(jnp.sum(blk_end[None, :] <= blk_ids[:, None], axis=1), E - 1).astype(jnp.int32)
    pos_kt = _positions(e_idx, rank, row_start).reshape(TOP_K, T)
    xs = _dispatch_sc(h2_packed, pos_kt, n_blocks * BM)
    ys = _experts(xs, blk_e, n_act, row_start, cnt, w_gate, w_up, w_down)
    gathered = _gather_rows_sc(ys, pos_kt[:SC_COMBINE_SLOTS])
    w_tk = top_w.T
    partial = _combine(h2, pos_kt, w_tk, ys, ws_gate, ws_up, ws_down)
    return _finish(h2, partial, w_tk, gathered, ln_g, ln_b)


def kernel(x, ln_in_g, ln_in_b, w_in, w_out, rel_bias, attn_sinks, ln_mix_g, ln_mix_b, w_router,
           router_bias, w_gate, w_up, w_down, ws_gate, ws_up, ws_down, ln_ffn_g, ln_ffn_b):
    B, S, D = x.shape
    h, h_packed = _mixer(x, ln_in_g, ln_in_b, w_in[0], w_out[0], rel_bias, attn_sinks[0], ln_mix_g[0],
                         ln_mix_b[0])
    out = _moe(h, h_packed, w_router[0], router_bias[0], w_gate[0], w_up[0], w_down[0],
               ws_gate[0], ws_up[0], ws_down[0], ln_ffn_g[0], ln_ffn_b[0])
    return out.reshape(B, S, D)
```

```python
import functools
import math

import jax
import jax.numpy as jnp
from jax import lax
from jax.experimental import pallas as pl
from jax.experimental.pallas import tpu as pltpu
from jax.experimental.pallas import tpu_sc as plsc

D_MODEL = 1024
DEPTH = 1
RET_HEADS = 4
RET_QK_DIM = 64
RET_V_DIM = 128
RET_CHUNK = 128
RET_WIDTH = RET_HEADS * RET_V_DIM
ROPE_BASE = 10000.0
SWA_HEADS = 8
SWA_KV_HEADS = 2
SWA_GROUP = SWA_HEADS // SWA_KV_HEADS
SWA_HEAD_DIM = 64
SWA_WINDOW = 128
SWA_WIDTH = SWA_HEADS * SWA_HEAD_DIM
MIX_WIDTH = RET_WIDTH + SWA_WIDTH
RQK = RET_HEADS * RET_QK_DIM
SKV = SWA_KV_HEADS * SWA_HEAD_DIM
IN_SIZES = (RQK, RQK, RET_WIDTH, RET_WIDTH, SWA_WIDTH, SKV, SKV)
IN_OFFS = tuple(sum(IN_SIZES[:i]) for i in range(len(IN_SIZES)))
IN_WIDTH = sum(IN_SIZES)
REL_BUCKETS = 32
REL_MAX_DIST = 128
N_EXPERTS = 256
TOP_K = 8
N_GROUPS = 8
GROUP_SIZE = N_EXPERTS // N_GROUPS
TOPK_GROUPS = 4
EXPERT_DIM = 256
SHARED_DIM = 256
ROUTED_SCALE = 2.5
LN_EPS = 1e-5
GN_EPS = 1e-6
DEEPNORM_ALPHA = (2 * DEPTH) ** 0.25
MASK_VALUE = -1e30

VMEM_LIMIT_BYTES = 56 * 1024 * 1024

MIX_ROWS = 512
ROUTE_ROWS = 256
EXPERT_ROWS = 512
COMBINE_ROWS = 256
POSITION_ROWS = 64
FINISH_ROWS = 512


def _layer_norm(x, g, b):
    mu = jnp.mean(x, axis=-1, keepdims=True)
    xc = x - mu
    var = jnp.mean(xc * xc, axis=-1, keepdims=True)
    return xc * lax.rsqrt(var + LN_EPS) * g + b


def _dot(a, b):
    return jnp.dot(a, b, preferred_element_type=jnp.float32)


def _dot_nt(a, b):
    return lax.dot_general(a, b, (((1,), (1,)), ((), ())), preferred_element_type=jnp.float32)


def _dot_tn(a, b):
    return lax.dot_general(a, b, (((0,), (0,)), ((), ())), preferred_element_type=jnp.float32)


def _silu(x):
    return x * (1.0 / (1.0 + jnp.exp(-x)))


LANES = 128
ROW_TILE = D_MODEL // LANES


def _load_rows(ref, n_rows, lead=()):
    return jnp.concatenate([ref[lead + (pl.ds(s, n_rows, stride=ROW_TILE), slice(None))]
                            for s in range(ROW_TILE)], axis=1)


def _store_rows(ref, val, lead=(), first_row=0):
    n_rows = val.shape[0]
    for s in range(ROW_TILE):
        dst = pl.ds(first_row * ROW_TILE + s, n_rows, stride=ROW_TILE)
        ref[lead + (dst, slice(None))] = val[:, s * LANES:(s + 1) * LANES]


def _row_tile(r):
    return pl.ds(pl.multiple_of(r * ROW_TILE, ROW_TILE), ROW_TILE)


def _pack_bf16_pairs(x):
    m = x.shape[1] // 2
    bits = lax.bitcast_convert_type(x.astype(jnp.bfloat16).astype(jnp.float32), jnp.uint32)
    return (bits[:, :m] >> 16) | (bits[:, m:] & jnp.uint32(0xFFFF0000))


def _unpack_bf16_pairs(p):
    lo = lax.bitcast_convert_type(p << 16, jnp.float32)
    hi = lax.bitcast_convert_type(p & jnp.uint32(0xFFFF0000), jnp.float32)
    return jnp.concatenate([lo, hi], axis=1).astype(jnp.bfloat16)


def _swap_halves(x):
    n = x.shape[-1]
    half = RET_QK_DIM // 2
    lane = lax.broadcasted_iota(jnp.int32, x.shape, 1)
    from_right = pltpu.roll(x, n - half, axis=1)
    from_left = pltpu.roll(x, half, axis=1)
    return jnp.where((lane % RET_QK_DIM) < half, from_right, from_left)


def _mixer_kernel(rel_bias_ref, x_ref, g_in_ref, b_in_ref, w_in_ref, w_out_ref, rot_ref, decay_ref,
                  zeta_ref, xi_ref, cdecay_ref, bucket_ref, sink_ref, g_mix_ref, b_mix_ref,
                  h2_ref, h2p_ref, state_ref, kprev_ref, vprev_ref, bias_ref):
    b_id = pl.program_id(0)
    c_id = pl.program_id(1)
    W = SWA_WINDOW

    @pl.when((b_id == 0) & (c_id == 0))
    def _build_bias():
        bucket = bucket_ref[...]
        for h in range(SWA_HEADS):
            acc = jnp.full((2 * W, W), MASK_VALUE, jnp.float32)
            for b in range(REL_BUCKETS):
                acc = jnp.where(bucket == b, rel_bias_ref[b, h], acc)
            kh, g = divmod(h, SWA_GROUP)
            bias_ref[kh, :, g * W:(g + 1) * W] = acc

    @pl.when(c_id == 0)
    def _reset():
        state_ref[...] = jnp.zeros_like(state_ref)
        kprev_ref[...] = jnp.zeros_like(kprev_ref)
        vprev_ref[...] = jnp.zeros_like(vprev_ref)

    h = _layer_norm(x_ref[...], g_in_ref[...], b_in_ref[...])
    proj = _dot(h.astype(jnp.bfloat16), w_in_ref[...])

    o_q, o_k, o_v, o_g, o_sq, o_sk, o_sv = IN_OFFS
    cos_t = rot_ref[:, :RQK]
    sin_t = rot_ref[:, RQK:]
    q_all = proj[:, o_q:o_q + RQK]
    k_all = proj[:, o_k:o_k + RQK]
    q_rot = q_all * cos_t + _swap_halves(q_all) * sin_t
    k_rot = (k_all * cos_t + _swap_halves(k_all) * sin_t) * (RET_QK_DIM ** -0.5)

    n_sub = x_ref.shape[0] // RET_CHUNK
    states = [state_ref[hh] for hh in range(RET_HEADS)]
    k_prev = kprev_ref[...]
    v_prev = vprev_ref[...]
    bf16 = jnp.bfloat16
    heads = range(RET_HEADS)
    ret_pieces, swa_pieces = [], []
    swa_jobs = []
    for s in range(n_sub):
        rows = slice(s * RET_CHUNK, (s + 1) * RET_CHUNK)
        k_cur = proj[rows, o_sk:o_sk + SKV].astype(bf16)
        v_cur = proj[rows, o_sv:o_sv + SKV].astype(bf16)
        for kh in range(SWA_KV_HEADS):
            kv = slice(kh * SWA_HEAD_DIM, (kh + 1) * SWA_HEAD_DIM)
            q4 = jnp.concatenate(
                [proj[rows, o_sq + (kh * SWA_GROUP + g) * SWA_HEAD_DIM:
                      o_sq + (kh * SWA_GROUP + g + 1) * SWA_HEAD_DIM] for g in range(SWA_GROUP)],
                axis=0) * (SWA_HEAD_DIM ** -0.5)
            kcat = jnp.concatenate([k_prev[:, kv], k_cur[:, kv]], axis=0)
            vcat = jnp.concatenate([v_prev[:, kv], v_cur[:, kv]], axis=0)
            swa_jobs.append((s, kh, q4.astype(bf16), kcat, vcat))
        k_prev, v_prev = k_cur, v_cur
    logits_all = [_dot_nt(kcat, q4) + bias_ref[kh] for (s, kh, q4, kcat, vcat) in swa_jobs]
    probs_all = []
    for (s, kh, q4, kcat, vcat), logits in zip(swa_jobs, logits_all):
        if s == 0:
            key = lax.broadcasted_iota(jnp.int32, logits.shape, 0)
            logits = logits + jnp.where((key < W) & (c_id == 0), MASK_VALUE, 0.0)
        sink = sink_ref[kh]
        m = jnp.maximum(jnp.max(logits, axis=0, keepdims=True), sink)
        p = jnp.exp(logits - m)
        den = jnp.sum(p, axis=0, keepdims=True) + jnp.exp(sink - m)
        probs_all.append((p / den).astype(bf16))
    o4_all = [_dot_tn(job[4], probs) for job, probs in zip(swa_jobs, probs_all)]
    for s in range(n_sub):
        swa_pieces.append([o4[:, g * W:(g + 1) * W].T.astype(bf16)
                           for job, o4 in zip(swa_jobs, o4_all) if job[0] == s for g in range(SWA_GROUP)])
    for s in range(n_sub):
        rows = slice(s * RET_CHUNK, (s + 1) * RET_CHUNK)
        qk = [slice(hh * RET_QK_DIM, (hh + 1) * RET_QK_DIM) for hh in heads]
        q = [q_rot[rows, qk[hh]].astype(bf16) for hh in heads]
        k32 = [k_rot[rows, qk[hh]] for hh in heads]
        v = [proj[rows, o_v + hh * RET_V_DIM:o_v + (hh + 1) * RET_V_DIM].astype(bf16) for hh in heads]
        scores = [_dot_nt(q[hh], k32[hh].astype(bf16)) * decay_ref[hh] for hh in heads]
        inter = [_dot(q[hh], states[hh].astype(bf16)) * xi_ref[hh] for hh in heads]
        kv_new = [_dot_tn((k32[hh] * zeta_ref[hh]).astype(bf16), v[hh]) for hh in heads]
        intra = [_dot(scores[hh].astype(bf16), v[hh]) for hh in heads]
        states = [states[hh] * cdecay_ref[hh] + kv_new[hh] for hh in heads]
        pieces = []
        for hh in heads:
            ret = intra[hh] + inter[hh]
            mu = jnp.mean(ret, axis=-1, keepdims=True)
            rc = ret - mu
            var = jnp.mean(rc * rc, axis=-1, keepdims=True)
            normed = rc * lax.rsqrt(var + GN_EPS)
            gate = proj[rows, o_g + hh * RET_V_DIM:o_g + (hh + 1) * RET_V_DIM]
            pieces.append((_silu(gate) * normed).astype(bf16))
        ret_pieces.append(pieces)
    cat_rows = [jnp.concatenate(ret_pieces[s] + swa_pieces[s], axis=1) for s in range(n_sub)]
    for hh in range(RET_HEADS):
        state_ref[hh] = states[hh]
    kprev_ref[...] = k_prev
    vprev_ref[...] = v_prev

    mix = _dot(jnp.concatenate(cat_rows, axis=0), w_out_ref[...])
    h2 = _layer_norm(DEEPNORM_ALPHA * h + mix, g_mix_ref[...], b_mix_ref[...])
    h2_ref[...] = h2
    h2p_ref[...] = _pack_bf16_pairs(h2)


def _t5_bucket(dist):
    n = jnp.maximum(dist, 0)
    max_exact = REL_BUCKETS // 2
    ratio = jnp.log(jnp.maximum(n, 1).astype(jnp.float32) / max_exact) / math.log(REL_MAX_DIST / max_exact)
    large = jnp.minimum(max_exact + (ratio * (REL_BUCKETS - max_exact)).astype(jnp.int32), REL_BUCKETS - 1)
    return jnp.where(n < max_exact, n, large)


def _mixer(x, ln_in_g, ln_in_b, w_in, w_out, rel_bias, sinks, ln_mix_g, ln_mix_b):
    B, S, D = x.shape
    R = MIX_ROWS
    C = RET_CHUNK
    W = SWA_WINDOW
    f32 = jnp.float32
    half = RET_QK_DIM // 2
    inv = ROPE_BASE ** (-jnp.arange(half, dtype=f32) / half)
    ang = jnp.arange(S, dtype=f32)[:, None] * inv[None, :]
    cos, sin = jnp.cos(ang), jnp.sin(ang)
    cos_t = jnp.tile(jnp.concatenate([cos, cos], axis=-1), (1, RET_HEADS))
    sin_t = jnp.tile(jnp.concatenate([-sin, sin], axis=-1), (1, RET_HEADS))
    rot = jnp.concatenate([cos_t, sin_t], axis=-1)
    log_gamma = jnp.log(1.0 - 2.0 ** (-5.0 - jnp.arange(RET_HEADS, dtype=f32)))
    idx = jnp.arange(C, dtype=f32)
    diff = idx[:, None] - idx[None, :]
    decay = jnp.where(diff[None] >= 0, jnp.exp(jnp.maximum(diff, 0.0)[None] * log_gamma[:, None, None]), 0.0)
    zeta = jnp.exp((C - 1.0 - idx)[None, :] * log_gamma[:, None])
    xi = jnp.exp((idx + 1.0)[None, :] * log_gamma[:, None])
    zeta_b = jnp.broadcast_to(zeta[:, :, None], (RET_HEADS, C, RET_QK_DIM))
    xi_b = jnp.broadcast_to(xi[:, :, None], (RET_HEADS, C, RET_V_DIM))
    cdecay = jnp.broadcast_to(jnp.exp(C * log_gamma)[:, None, None], (RET_HEADS, RET_QK_DIM, RET_V_DIM))
    i = jnp.arange(W)
    j = jnp.arange(2 * W)
    dist = i[:, None] + W - j[None, :]
    bucket = jnp.where((dist >= 0) & (dist < W), _t5_bucket(dist), -1).astype(jnp.int32).T
    sink_row = jnp.repeat(sinks.astype(f32), W).reshape(SWA_KV_HEADS, 1, SWA_GROUP * W)

    const = lambda shape: pl.BlockSpec(shape, lambda b, c, *_: (0,) * len(shape))
    grid_spec = pltpu.PrefetchScalarGridSpec(
        num_scalar_prefetch=1,
        grid=(B, S // R),
        in_specs=[
            pl.BlockSpec((None, R, D), lambda b, c, *_: (b, c, 0)),
            const((1, D)), const((1, D)),
            const((D, IN_WIDTH)), const((MIX_WIDTH, D)),
            pl.BlockSpec((R, 2 * RQK), lambda b, c, *_: (c, 0)),
            const((RET_HEADS, C, C)), const((RET_HEADS, C, RET_QK_DIM)), const((RET_HEADS, C, RET_V_DIM)),
            const((RET_HEADS, RET_QK_DIM, RET_V_DIM)),
            const((2 * W, W)), const((SWA_KV_HEADS, 1, SWA_GROUP * W)),
            const((1, D)), const((1, D)),
        ],
        out_specs=[pl.BlockSpec((R, D), lambda b, c, *_: (b * (S // R) + c, 0)),
                   pl.BlockSpec((R, D // 2), lambda b, c, *_: (b * (S // R) + c, 0))],
        scratch_shapes=[
            pltpu.VMEM((RET_HEADS, RET_QK_DIM, RET_V_DIM), f32),
            pltpu.VMEM((W, SKV), jnp.bfloat16),
            pltpu.VMEM((W, SKV), jnp.bfloat16),
            pltpu.VMEM((SWA_KV_HEADS, 2 * W, SWA_GROUP * W), f32),
        ],
    )
    return pl.pallas_call(
        _mixer_kernel,
        grid_spec=grid_spec,
        out_shape=[jax.ShapeDtypeStruct((B * S, D), f32), jax.ShapeDtypeStruct((B * S, D // 2), jnp.uint32)],
        compiler_params=pltpu.CompilerParams(
            dimension_semantics=("arbitrary", "arbitrary"), vmem_limit_bytes=VMEM_LIMIT_BYTES),
    )(rel_bias.astype(f32), x, ln_in_g.reshape(1, D), ln_in_b.reshape(1, D),
      w_in.astype(jnp.bfloat16), w_out.astype(jnp.bfloat16), rot, decay, zeta_b, xi_b, cdecay,
      bucket, sink_row, ln_mix_g.reshape(1, D), ln_mix_b.reshape(1, D))


def _router_kernel(h_ref, wr_ref, rb_ref, e_ref, w_ref, rk_ref, cnt_ref, run_ref):
    f32 = jnp.float32
    R = h_ref.shape[0]
    E = N_EXPERTS
    neg = -jnp.inf

    @pl.when(pl.program_id(0) == 0)
    def _init():
        run_ref[...] = jnp.zeros_like(run_ref)

    logits = _dot_nt(wr_ref[...], h_ref[...].astype(jnp.bfloat16))
    scores = 1.0 / (1.0 + jnp.exp(-logits))
    choice = scores + rb_ref[...]
    eid = lax.broadcasted_iota(jnp.int32, (E, R), 0)

    def first_argmax(vals, ids, none):
        m = jnp.max(vals, axis=0, keepdims=True)
        idx = jnp.min(jnp.where(vals == m, ids, none), axis=0, keepdims=True)
        return m, idx

    gid = lax.broadcasted_iota(jnp.int32, (GROUP_SIZE, R), 0)
    groups, gscore = [], []
    for g in range(N_GROUPS):
        vals = choice[g * GROUP_SIZE:(g + 1) * GROUP_SIZE]
        m1, i1 = first_argmax(vals, gid, GROUP_SIZE)
        m2 = jnp.max(jnp.where(gid == i1, neg, vals), axis=0, keepdims=True)
        groups.append(vals)
        gscore.append(m1 + m2)
    kept = []
    for g in range(N_GROUPS):
        beaten = jnp.zeros((1, R), f32)
        for g2 in range(N_GROUPS):
            if g2 == g:
                continue
            ahead = (gscore[g2] > gscore[g]) | (gscore[g2] == gscore[g]) if g2 < g else gscore[g2] > gscore[g]
            beaten = beaten + jnp.where(ahead, 1.0, 0.0)
        kept.append(jnp.where(beaten < TOPK_GROUPS, groups[g], neg))
    masked = jnp.concatenate(kept, axis=0)

    idxs, wts = [], []
    picked = jnp.zeros((E, R), f32)
    for _ in range(TOP_K):
        _, idx = first_argmax(masked, eid, E)
        hit = eid == idx
        idxs.append(idx)
        wts.append(jnp.sum(jnp.where(hit, scores, 0.0), axis=0, keepdims=True))
        masked = jnp.where(hit, neg, masked)
        picked = jnp.where(hit, 1.0, picked)
    wsum = wts[0]
    for k in range(1, TOP_K):
        wsum = wsum + wts[k]

    row = lax.broadcasted_iota(jnp.int32, (R, R), 0)
    col = lax.broadcasted_iota(jnp.int32, (R, R), 1)
    earlier = jnp.where(row < col, 1.0, 0.0).astype(jnp.bfloat16)
    picked_bf = picked.astype(jnp.bfloat16)
    run = run_ref[...]
    before = _dot(picked_bf, earlier) + jnp.concatenate([run] * (R // LANES), axis=1)
    sub_k = lax.broadcasted_iota(jnp.int32, (TOP_K, R), 0)
    e_out = jnp.zeros((TOP_K, R), jnp.int32)
    w_out = jnp.zeros((TOP_K, R), f32)
    rk_out = jnp.zeros((TOP_K, R), jnp.int32)
    for k in range(TOP_K):
        rank_k = jnp.sum(jnp.where(eid == idxs[k], before, 0.0), axis=0, keepdims=True)
        e_out = jnp.where(sub_k == k, idxs[k], e_out)
        w_out = jnp.where(sub_k == k, wts[k] / wsum * ROUTED_SCALE, w_out)
        rk_out = jnp.where(sub_k == k, rank_k.astype(jnp.int32), rk_out)
    e_ref[...] = e_out
    w_ref[...] = w_out
    rk_ref[...] = rk_out
    run_ref[...] = run + _dot(picked_bf, jnp.ones((R, LANES), jnp.bfloat16))
    cnt_ref[...] = run_ref[...]


def _router(h2, w_router, router_bias):
    T, D = h2.shape
    R = ROUTE_ROWS
    E = N_EXPERTS
    return pl.pallas_call(
        _router_kernel,
        grid=(T // R,),
        in_specs=[
            pl.BlockSpec((R, D), lambda i: (i, 0)),
            pl.BlockSpec((E, D), lambda i: (0, 0)),
            pl.BlockSpec((E, R), lambda i: (0, 0)),
        ],
        out_specs=[
            pl.BlockSpec((TOP_K, R), lambda i: (0, i)),
            pl.BlockSpec((TOP_K, R), lambda i: (0, i)),
            pl.BlockSpec((TOP_K, R), lambda i: (0, i)),
            pl.BlockSpec((E, LANES), lambda i: (0, 0)),
        ],
        out_shape=[
            jax.ShapeDtypeStruct((TOP_K, T), jnp.int32),
            jax.ShapeDtypeStruct((TOP_K, T), jnp.float32),
            jax.ShapeDtypeStruct((TOP_K, T), jnp.int32),
            jax.ShapeDtypeStruct((E, LANES), jnp.float32),
        ],
        scratch_shapes=[pltpu.VMEM((E, LANES), jnp.float32)],
        compiler_params=pltpu.CompilerParams(
            dimension_semantics=("arbitrary",), vmem_limit_bytes=VMEM_LIMIT_BYTES),
    )(h2, w_router.T.astype(jnp.bfloat16), jnp.broadcast_to(router_bias.astype(jnp.float32)[:, None], (E, R)))


def _positions_kernel(row_start_ref, e_ref, rk_ref, pos_ref):
    e = e_ref[...]
    rk = rk_ref[...]

    def per_expert(i, pos):
        return jnp.where(e == i, rk + row_start_ref[i], pos)

    pos_ref[...] = lax.fori_loop(0, N_EXPERTS, per_expert, jnp.zeros_like(rk), unroll=8)


def _positions(e_idx, rank, row_start):
    n = e_idx.size
    shape = (n // LANES, LANES)
    block = pl.BlockSpec((POSITION_ROWS, LANES), lambda i, *_: (i, 0))
    grid_spec = pltpu.PrefetchScalarGridSpec(
        num_scalar_prefetch=1,
        grid=(shape[0] // POSITION_ROWS,),
        in_specs=[block, block],
        out_specs=block,
    )
    pos = pl.pallas_call(
        _positions_kernel,
        grid_spec=grid_spec,
        out_shape=jax.ShapeDtypeStruct(shape, jnp.int32),
    )(row_start, e_idx.reshape(shape), rank.reshape(shape))
    return pos.reshape(n)


SC_WINDOW = 32
SC_DISPATCH_WINDOW = 64


def _dispatch_sc(h2, pos_kt, n_rows):
    T, D = h2.shape
    W = SC_DISPATCH_WINDOW
    idx = _window_indices(pos_kt, W)
    idx_rows = TOP_K * W // LANES
    mesh = plsc.VectorSubcoreMesh(core_axis_name="core", subcore_axis_name="subcore")

    @pl.kernel(out_type=jax.ShapeDtypeStruct((n_rows, D), h2.dtype), mesh=mesh, scratch_types=[])
    def scatter_rows(x_hbm, i_hbm, o_hbm):
        def body(x_vmem, i_vmem):
            for k in range(TOP_K):
                r, q = divmod(k * W, LANES)
                pltpu.sync_copy(x_vmem, o_hbm.at[i_vmem.at[r, pl.ds(q, W)]])

        pltpu.emit_pipeline(
            body,
            grid=(T // W,),
            in_specs=[pl.BlockSpec((W, D), lambda i: (i, 0)),
                      pl.BlockSpec((idx_rows, LANES), lambda i: (i, 0))],
            out_specs=[],
            core_axis_name=("core", "subcore"),
            dimension_semantics=(pltpu.PARALLEL,),
        )(x_hbm, i_hbm)

    return scatter_rows(h2, idx)


def _window_indices(pos_kt, window):
    K, T = pos_kt.shape
    return pos_kt.reshape(K, T // window, window).transpose(1, 0, 2).reshape(T * K // LANES, LANES)


def _gather_rows_sc(ys, pos_kt):
    n = pos_kt.size
    W = SC_WINDOW
    ys3 = ys.reshape(ys.shape[0] // ROW_TILE, ROW_TILE, LANES)
    idx = jnp.pad(pos_kt.reshape(n // W, W), ((0, 0), (0, LANES - W)))
    mesh = plsc.VectorSubcoreMesh(core_axis_name="core", subcore_axis_name="subcore")

    @pl.kernel(out_type=jax.ShapeDtypeStruct((n, ROW_TILE, LANES), ys.dtype), mesh=mesh, scratch_types=[])
    def gather_rows(y_hbm, i_hbm, o_hbm):
        def body(i_vmem, o_vmem):
            pltpu.sync_copy(y_hbm.at[i_vmem.at[0, pl.ds(0, W)]], o_vmem)

        pltpu.emit_pipeline(
            body,
            grid=(n // W,),
            in_specs=[pl.BlockSpec((1, LANES), lambda i: (i, 0))],
            out_specs=[pl.BlockSpec((W, ROW_TILE, LANES), lambda i: (i, 0, 0))],
            core_axis_name=("core", "subcore"),
            dimension_semantics=(pltpu.PARALLEL,),
        )(i_hbm, o_hbm)

    return gather_rows(ys3, idx).reshape(n * ROW_TILE, LANES)


EXPERT_SUBROWS = 128
X_SLOTS = 4
Y_SLOTS = 3


def _experts_kernel(blk_e_ref, first_ref, slot_ref, next_e_ref, valid_ref, n_act_ref, xs_hbm, wg_hbm,
                    wu_hbm, wd_hbm, ys_hbm, x_buf, y_buf, wg_buf, wu_buf, wd_buf, wg_bf, wu_bf, wd_bf,
                    sems, x_sems, y_sems):
    i = pl.program_id(0)
    n_act = n_act_ref[0]
    bf16 = jnp.bfloat16
    blk = EXPERT_ROWS * ROW_TILE

    def x_copy(j):
        n = pl.multiple_of((valid_ref[j] + 7) // 8 * 8, 8)
        src = xs_hbm.at[pl.ds(pl.multiple_of(j * EXPERT_ROWS, EXPERT_ROWS), n)]
        return pltpu.make_async_copy(src, x_buf.at[j % X_SLOTS, pl.ds(0, n)], x_sems.at[j % X_SLOTS])

    def y_copy(j):
        n = pl.multiple_of(valid_ref[j] * ROW_TILE, ROW_TILE)
        dst = ys_hbm.at[pl.ds(pl.multiple_of(j * blk, blk), n)]
        return pltpu.make_async_copy(y_buf.at[j % Y_SLOTS, pl.ds(0, n)], dst, y_sems.at[j % Y_SLOTS])

    @pl.when(i == 0)
    def _prime():
        x_buf[...] = jnp.zeros_like(x_buf)
        for j in range(X_SLOTS - 1):
            @pl.when(j < n_act)
            def _():
                x_copy(j).start()

    @pl.when(i + (X_SLOTS - 1) < n_act)
    def _prefetch():
        x_copy(i + (X_SLOTS - 1)).start()

    def weight_copies(e, slot):
        return (pltpu.make_async_copy(wg_hbm.at[e], wg_buf.at[slot], sems.at[slot]),
                pltpu.make_async_copy(wu_hbm.at[e], wu_buf.at[slot], sems.at[slot]),
                pltpu.make_async_copy(wd_hbm.at[e], wd_buf.at[slot], sems.at[slot]))

    @pl.when((i < n_act_ref[0]) & (first_ref[i] == 1))
    def _new_expert():
        slot = slot_ref[i]

        @pl.when(i == 0)
        def _():
            for c in weight_copies(blk_e_ref[0], 0):
                c.start()

        for c in weight_copies(blk_e_ref[i], slot):
            c.wait()

        @pl.when(next_e_ref[i] >= 0)
        def _():
            for c in weight_copies(next_e_ref[i], 1 - slot):
                c.start()

        wg_bf[...] = wg_buf[slot].astype(bf16)
        wu_bf[...] = wu_buf[slot].astype(bf16)
        wd_bf[...] = wd_buf[slot].astype(bf16)

    @pl.when(i < n_act)
    def _compute():
        x_copy(i).wait()

        @pl.when(i >= Y_SLOTS)
        def _():
            y_copy(i - Y_SLOTS).wait()

        sub = EXPERT_SUBROWS
        n_parts = (valid_ref[i] + sub - 1) // sub

        def run(parts):
            xs_ = [_unpack_bf16_pairs(x_buf[i % X_SLOTS, pl.ds(part * sub, sub), :]) for part in range(parts)]
            gs = [_dot(x, wg_bf[...]) for x in xs_]
            us = [_dot(x, wu_bf[...]) for x in xs_]
            acts = [(_silu(g) * u).astype(bf16) for g, u in zip(gs, us)]
            for part in range(parts):
                _store_rows(y_buf, _dot(acts[part], wd_bf[...]), lead=(i % Y_SLOTS,), first_row=part * sub)

        for parts in range(1, EXPERT_ROWS // sub + 1):
            pl.when(n_parts == parts)(functools.partial(run, parts))
        y_copy(i).start()

    @pl.when(i == n_act - 1)
    def _drain():
        for d in range(Y_SLOTS):
            @pl.when(i - d >= 0)
            def _():
                y_copy(i - d).wait()


def _experts(xs, blk_e, n_act, row_start, cnt, w_gate, w_up, w_down):
    D = D_MODEL
    BM = EXPERT_ROWS
    F = EXPERT_DIM
    n_blocks = xs.shape[0] // BM
    blk_in_expert = jnp.arange(n_blocks, dtype=jnp.int32) - row_start[blk_e] // BM
    valid = jnp.clip(cnt[blk_e] - blk_in_expert * BM, 0, BM).astype(jnp.int32)
    ids = jnp.arange(n_blocks, dtype=jnp.int32)
    active = ids < n_act[0]
    first = active & ((ids == 0) | (blk_e != jnp.roll(blk_e, 1)))
    slot = ((jnp.cumsum(first.astype(jnp.int32)) - 1) % 2).astype(jnp.int32)
    first_pos = jnp.where(first, ids, n_blocks)
    later_first = lax.cummin(jnp.concatenate([first_pos[1:], jnp.full((1,), n_blocks, jnp.int32)]), reverse=True)
    next_e = jnp.where(later_first < n_blocks, blk_e[jnp.minimum(later_first, n_blocks - 1)], -1).astype(jnp.int32)

    grid_spec = pltpu.PrefetchScalarGridSpec(
        num_scalar_prefetch=6,
        grid=(n_blocks,),
        in_specs=[pl.BlockSpec(memory_space=pl.ANY)] * 4,
        out_specs=pl.BlockSpec(memory_space=pl.ANY),
        scratch_shapes=[
            pltpu.VMEM((X_SLOTS, BM, D // 2), jnp.uint32),
            pltpu.VMEM((Y_SLOTS, BM * ROW_TILE, LANES), jnp.float32),
            pltpu.VMEM((2, D, F), jnp.float32), pltpu.VMEM((2, D, F), jnp.float32),
            pltpu.VMEM((2, F, D), jnp.float32),
            pltpu.VMEM((D, F), jnp.bfloat16), pltpu.VMEM((D, F), jnp.bfloat16),
            pltpu.VMEM((F, D), jnp.bfloat16),
            pltpu.SemaphoreType.DMA((2,)), pltpu.SemaphoreType.DMA((X_SLOTS,)),
            pltpu.SemaphoreType.DMA((Y_SLOTS,)),
        ],
    )
    return pl.pallas_call(
        _experts_kernel,
        grid_spec=grid_spec,
        out_shape=jax.ShapeDtypeStruct((xs.shape[0] * ROW_TILE, LANES), jnp.float32),
        compiler_params=pltpu.CompilerParams(
            dimension_semantics=("arbitrary",), vmem_limit_bytes=VMEM_LIMIT_BYTES),
    )(blk_e, first.astype(jnp.int32), slot, next_e, valid, n_act, xs, w_gate, w_up, w_down)


def _finish_kernel(h_ref, part_ref, w_ref, *rest):
    slabs = rest[:SC_COMBINE_SLOTS]
    g_ref, b_ref, out_ref = rest[SC_COMBINE_SLOTS:]
    R = h_ref.shape[0]
    w = w_ref[...]
    ffn = part_ref[...]
    for k in range(SC_COMBINE_SLOTS):
        ffn = ffn + _load_rows(slabs[k], R) * w[:, k:k + 1]
    out_ref[...] = _layer_norm(DEEPNORM_ALPHA * h_ref[...] + ffn, g_ref[...], b_ref[...])


def _finish(h2, partial, top_w, gathered, ln_g, ln_b):
    T, D = h2.shape
    R = FINISH_ROWS
    rows = pl.BlockSpec((R, D), lambda i: (i, 0))
    vec = pl.BlockSpec((1, D), lambda i: (0, 0))
    slab = lambda k: pl.BlockSpec((R * ROW_TILE, LANES), lambda i: (k * (T // R) + i, 0))
    return pl.pallas_call(
        _finish_kernel,
        grid=(T // R,),
        in_specs=[rows, rows, pl.BlockSpec((R, TOP_K), lambda i: (i, 0))]
        + [slab(k) for k in range(SC_COMBINE_SLOTS)] + [vec, vec],
        out_specs=rows,
        out_shape=jax.ShapeDtypeStruct((T, D), jnp.float32),
        compiler_params=pltpu.CompilerParams(
            dimension_semantics=("arbitrary",), vmem_limit_bytes=VMEM_LIMIT_BYTES),
    )(h2, partial, top_w, *([gathered] * SC_COMBINE_SLOTS), ln_g.reshape(1, D), ln_b.reshape(1, D))


SC_COMBINE_SLOTS = 5


def _combine_kernel(*refs):
    n_tc = TOP_K - SC_COMBINE_SLOTS
    pos_refs, pos_next_refs = refs[:n_tc], refs[n_tc:2 * n_tc]
    h_ref, w_ref, ys_ref, wsg_ref, wsu_ref, wsd_ref, out_ref, buf_ref, sems = refs[2 * n_tc:]
    R = h_ref.shape[0]
    i = pl.program_id(0)
    slot = i % 2

    def gather(p_refs, s):
        def issue(t, carry):
            for j in range(n_tc):
                pltpu.make_async_copy(ys_ref.at[_row_tile(p_refs[j][t])], buf_ref.at[s, j, _row_tile(t)],
                                      sems.at[s]).start(priority=j % 2)
            return carry

        lax.fori_loop(0, R, issue, 0)

    @pl.when(i == 0)
    def _():
        gather(pos_refs, 0)

    @pl.when(i + 1 < pl.num_programs(0))
    def _():
        gather(pos_next_refs, 1 - slot)

    h = h_ref[...]
    hb = h.astype(jnp.bfloat16)
    act = (_silu(_dot(hb, wsg_ref[...])) * _dot(hb, wsu_ref[...])).astype(jnp.bfloat16)
    ffn = _dot(act, wsd_ref[...])
    pltpu.make_async_copy(buf_ref.at[slot], buf_ref.at[slot], sems.at[slot]).wait()
    w = w_ref[...]
    for k in range(SC_COMBINE_SLOTS, TOP_K):
        ffn = ffn + _load_rows(buf_ref, R, lead=(slot, k - SC_COMBINE_SLOTS)) * w[:, k:k + 1]
    out_ref[...] = ffn


def _combine(h2, pos_kt, top_w, ys, ws_gate, ws_up, ws_down):
    T, D = h2.shape
    R = COMBINE_ROWS
    F = SHARED_DIM
    bf16 = jnp.bfloat16
    n_steps = T // R
    slots = range(SC_COMBINE_SLOTS, TOP_K)
    const = lambda shape: pl.BlockSpec(shape, lambda i: (0,) * len(shape))
    pos_now = [pl.BlockSpec((R,), lambda i, k=k: (k * n_steps + i,), memory_space=pltpu.SMEM) for k in slots]
    pos_next = [pl.BlockSpec((R,), lambda i, k=k: (k * n_steps + jnp.minimum(i + 1, n_steps - 1),),
                             memory_space=pltpu.SMEM) for k in slots]
    pos_flat = pos_kt.reshape(TOP_K * T)
    return pl.pallas_call(
        _combine_kernel,
        grid=(n_steps,),
        in_specs=pos_now + pos_next + [
            pl.BlockSpec((R, D), lambda i: (i, 0)),
            pl.BlockSpec((R, TOP_K), lambda i: (i, 0)),
            pl.BlockSpec(memory_space=pl.ANY),
            const((D, F)), const((D, F)), const((F, D)),
        ],
        out_specs=pl.BlockSpec((R, D), lambda i: (i, 0)),
        scratch_shapes=[pltpu.VMEM((2, TOP_K - SC_COMBINE_SLOTS, R * ROW_TILE, LANES), jnp.float32),
                        pltpu.SemaphoreType.DMA((2,))],
        out_shape=jax.ShapeDtypeStruct((T, D), jnp.float32),
        compiler_params=pltpu.CompilerParams(
            dimension_semantics=("arbitrary",), vmem_limit_bytes=VMEM_LIMIT_BYTES),
    )(*([pos_flat] * (2 * len(slots))), h2, top_w, ys, ws_gate.astype(bf16), ws_up.astype(bf16),
      ws_down.astype(bf16))


def _moe(h2, h2_packed, w_router, router_bias, w_gate, w_up, w_down, ws_gate, ws_up, ws_down, ln_g, ln_b):
    T = h2.shape[0]
    E = N_EXPERTS
    BM = EXPERT_ROWS
    e_idx, top_w, rank, counts = _router(h2, w_router, router_bias)
    cnt = counts[:, 0].astype(jnp.int32)
    nblk = (cnt + BM - 1) // BM
    blk_end = jnp.cumsum(nblk)
    row_start = ((blk_end - nblk) * BM).astype(jnp.int32)
    n_blocks = T * TOP_K // BM + E
    n_act = blk_end[-1:].astype(jnp.int32)
    blk_ids = jnp.minimum(jnp.arange(n_blocks, dtype=jnp.int32), n_act[0] - 1)
    blk_e = jnp.minimum(jnp.sum(blk_end[None, :] <= blk_ids[:, None], axis=1), E - 1).astype(jnp.int32)
    pos_kt = _positions(e_idx, rank, row_start).reshape(TOP_K, T)
    xs = _dispatch_sc(h2_packed, pos_kt, n_blocks * BM)
    ys = _experts(xs, blk_e, n_act, row_start, cnt, w_gate, w_up, w_down)
    gathered = _gather_rows_sc(ys, pos_kt[:SC_COMBINE_SLOTS])
    w_tk = top_w.T
    partial = _combine(h2, pos_kt, w_tk, ys, ws_gate, ws_up, ws_down)
    return _finish(h2, partial, w_tk, gathered, ln_g, ln_b)


def kernel(x, ln_in_g, ln_in_b, w_in, w_out, rel_bias, attn_sinks, ln_mix_g, ln_mix_b, w_router,
           router_bias, w_gate, w_up, w_down, ws_gate, ws_up, ws_down, ln_ffn_g, ln_ffn_b):
    B, S, D = x.shape
    h, h_packed = _mixer(x, ln_in_g, ln_in_b, w_in[0], w_out[0], rel_bias, attn_sinks[0], ln_mix_g[0],
                         ln_mix_b[0])
    out = _moe(h, h_packed, w_router[0], router_bias[0], w_gate[0], w_up[0], w_down[0],
               ws_gate[0], ws_up[0], ws_down[0], ln_ffn_g[0], ln_ffn_b[0])
    return out.reshape(B, S, D)
```

```python
import functools
import math

import jax
import jax.numpy as jnp
from jax import lax
from jax.experimental import pallas as pl
from jax.experimental.pallas import tpu as pltpu
from jax.experimental.pallas import tpu_sc as plsc

D_MODEL = 1024
DEPTH = 1
RET_HEADS = 4
RET_QK_DIM = 64
RET_V_DIM = 128
RET_CHUNK = 128
RET_WIDTH = RET_HEADS * RET_V_DIM
ROPE_BASE = 10000.0
SWA_HEADS = 8
SWA_KV_HEADS = 2
SWA_GROUP = SWA_HEADS // SWA_KV_HEADS
SWA_HEAD_DIM = 64
SWA_WINDOW = 128
SWA_WIDTH = SWA_HEADS * SWA_HEAD_DIM
MIX_WIDTH = RET_WIDTH + SWA_WIDTH
RQK = RET_HEADS * RET_QK_DIM
SKV = SWA_KV_HEADS * SWA_HEAD_DIM
IN_SIZES = (RQK, RQK, RET_WIDTH, RET_WIDTH, SWA_WIDTH, SKV, SKV)
IN_OFFS = tuple(sum(IN_SIZES[:i]) for i in range(len(IN_SIZES)))
IN_WIDTH = sum(IN_SIZES)
REL_BUCKETS = 32
REL_MAX_DIST = 128
N_EXPERTS = 256
TOP_K = 8
N_GROUPS = 8
GROUP_SIZE = N_EXPERTS // N_GROUPS
TOPK_GROUPS = 4
EXPERT_DIM = 256
SHARED_DIM = 256
ROUTED_SCALE = 2.5
LN_EPS = 1e-5
GN_EPS = 1e-6
DEEPNORM_ALPHA = (2 * DEPTH) ** 0.25
MASK_VALUE = -1e30

VMEM_LIMIT_BYTES = 56 * 1024 * 1024

MIX_ROWS = 512
ROUTE_ROWS = 256
EXPERT_ROWS = 512
COMBINE_ROWS = 512
POSITION_ROWS = 64
FINISH_ROWS = 512


def _layer_norm(x, g, b):
    mu = jnp.mean(x, axis=-1, keepdims=True)
    xc = x - mu
    var = jnp.mean(xc * xc, axis=-1, keepdims=True)
    return xc * lax.rsqrt(var + LN_EPS) * g + b


def _dot(a, b):
    return jnp.dot(a, b, preferred_element_type=jnp.float32)


def _dot_nt(a, b):
    return lax.dot_general(a, b, (((1,), (1,)), ((), ())), preferred_element_type=jnp.float32)


def _dot_tn(a, b):
    return lax.dot_general(a, b, (((0,), (0,)), ((), ())), preferred_element_type=jnp.float32)


def _silu(x):
    return x * (1.0 / (1.0 + jnp.exp(-x)))


LANES = 128
ROW_TILE = D_MODEL // LANES


def _load_rows(ref, n_rows, lead=()):
    return jnp.concatenate([ref[lead + (pl.ds(s, n_rows, stride=ROW_TILE), slice(None))]
                            for s in range(ROW_TILE)], axis=1)


def _store_rows(ref, val, lead=(), first_row=0):
    n_rows = val.shape[0]
    for s in range(ROW_TILE):
        dst = pl.ds(first_row * ROW_TILE + s, n_rows, stride=ROW_TILE)
        ref[lead + (dst, slice(None))] = val[:, s * LANES:(s + 1) * LANES]


def _row_tile(r):
    return pl.ds(pl.multiple_of(r * ROW_TILE, ROW_TILE), ROW_TILE)


def _pack_bf16_pairs(x):
    m = x.shape[1] // 2
    bits = lax.bitcast_convert_type(x.astype(jnp.bfloat16).astype(jnp.float32), jnp.uint32)
    return (bits[:, :m] >> 16) | (bits[:, m:] & jnp.uint32(0xFFFF0000))


def _unpack_bf16_pairs(p):
    lo = lax.bitcast_convert_type(p << 16, jnp.float32)
    hi = lax.bitcast_convert_type(p & jnp.uint32(0xFFFF0000), jnp.float32)
    return jnp.concatenate([lo, hi], axis=1).astype(jnp.bfloat16)


def _swap_halves(x):
    n = x.shape[-1]
    half = RET_QK_DIM // 2
    lane = lax.broadcasted_iota(jnp.int32, x.shape, 1)
    from_right = pltpu.roll(x, n - half, axis=1)
    from_left = pltpu.roll(x, half, axis=1)
    return jnp.where((lane % RET_QK_DIM) < half, from_right, from_left)


def _mixer_kernel(rel_bias_ref, x_ref, g_in_ref, b_in_ref, w_in_ref, w_out_ref, rot_ref, decay_ref,
                  zeta_ref, xi_ref, cdecay_ref, bucket_ref, sink_ref, g_mix_ref, b_mix_ref,
                  h2_ref, h2p_ref, state_ref, kprev_ref, vprev_ref, bias_ref):
    b_id = pl.program_id(0)
    c_id = pl.program_id(1)
    W = SWA_WINDOW

    @pl.when((b_id == 0) & (c_id == 0))
    def _build_bias():
        bucket = bucket_ref[...]
        for h in range(SWA_HEADS):
            acc = jnp.full((2 * W, W), MASK_VALUE, jnp.float32)
            for b in range(REL_BUCKETS):
                acc = jnp.where(bucket == b, rel_bias_ref[b, h], acc)
            kh, g = divmod(h, SWA_GROUP)
            bias_ref[kh, :, g * W:(g + 1) * W] = acc

    @pl.when(c_id == 0)
    def _reset():
        state_ref[...] = jnp.zeros_like(state_ref)
        kprev_ref[...] = jnp.zeros_like(kprev_ref)
        vprev_ref[...] = jnp.zeros_like(vprev_ref)

    h = _layer_norm(x_ref[...], g_in_ref[...], b_in_ref[...])
    proj = _dot(h.astype(jnp.bfloat16), w_in_ref[...])

    o_q, o_k, o_v, o_g, o_sq, o_sk, o_sv = IN_OFFS
    cos_t = rot_ref[:, :RQK]
    sin_t = rot_ref[:, RQK:]
    q_all = proj[:, o_q:o_q + RQK]
    k_all = proj[:, o_k:o_k + RQK]
    q_rot = q_all * cos_t + _swap_halves(q_all) * sin_t
    k_rot = (k_all * cos_t + _swap_halves(k_all) * sin_t) * (RET_QK_DIM ** -0.5)

    n_sub = x_ref.shape[0] // RET_CHUNK
    states = [state_ref[hh] for hh in range(RET_HEADS)]
    k_prev = kprev_ref[...]
    v_prev = vprev_ref[...]
    bf16 = jnp.bfloat16
    heads = range(RET_HEADS)
    ret_pieces, swa_pieces = [], []
    swa_jobs = []
    for s in range(n_sub):
        rows = slice(s * RET_CHUNK, (s + 1) * RET_CHUNK)
        k_cur = proj[rows, o_sk:o_sk + SKV].astype(bf16)
        v_cur = proj[rows, o_sv:o_sv + SKV].astype(bf16)
        for kh in range(SWA_KV_HEADS):
            kv = slice(kh * SWA_HEAD_DIM, (kh + 1) * SWA_HEAD_DIM)
            q4 = jnp.concatenate(
                [proj[rows, o_sq + (kh * SWA_GROUP + g) * SWA_HEAD_DIM:
                      o_sq + (kh * SWA_GROUP + g + 1) * SWA_HEAD_DIM] for g in range(SWA_GROUP)],
                axis=0) * (SWA_HEAD_DIM ** -0.5)
            kcat = jnp.concatenate([k_prev[:, kv], k_cur[:, kv]], axis=0)
            vcat = jnp.concatenate([v_prev[:, kv], v_cur[:, kv]], axis=0)
            swa_jobs.append((s, kh, q4.astype(bf16), kcat, vcat))
        k_prev, v_prev = k_cur, v_cur
    logits_all = [_dot_nt(kcat, q4) + bias_ref[kh] for (s, kh, q4, kcat, vcat) in swa_jobs]
    probs_all = []
    for (s, kh, q4, kcat, vcat), logits in zip(swa_jobs, logits_all):
        if s == 0:
            key = lax.broadcasted_iota(jnp.int32, logits.shape, 0)
            logits = logits + jnp.where((key < W) & (c_id == 0), MASK_VALUE, 0.0)
        sink = sink_ref[kh]
        m = jnp.maximum(jnp.max(logits, axis=0, keepdims=True), sink)
        p = jnp.exp(logits - m)
        den = jnp.sum(p, axis=0, keepdims=True) + jnp.exp(sink - m)
        probs_all.append((p / den).astype(bf16))
    o4_all = [_dot_tn(job[4], probs) for job, probs in zip(swa_jobs, probs_all)]
    for s in range(n_sub):
        swa_pieces.append([o4[:, g * W:(g + 1) * W].T.astype(bf16)
                           for job, o4 in zip(swa_jobs, o4_all) if job[0] == s for g in range(SWA_GROUP)])
    for s in range(n_sub):
        rows = slice(s * RET_CHUNK, (s + 1) * RET_CHUNK)
        qk = [slice(hh * RET_QK_DIM, (hh + 1) * RET_QK_DIM) for hh in heads]
        q = [q_rot[rows, qk[hh]].astype(bf16) for hh in heads]
        k32 = [k_rot[rows, qk[hh]] for hh in heads]
        v = [proj[rows, o_v + hh * RET_V_DIM:o_v + (hh + 1) * RET_V_DIM].astype(bf16) for hh in heads]
        scores = [_dot_nt(q[hh], k32[hh].astype(bf16)) * decay_ref[hh] for hh in heads]
        inter = [_dot(q[hh], states[hh].astype(bf16)) * xi_ref[hh] for hh in heads]
        kv_new = [_dot_tn((k32[hh] * zeta_ref[hh]).astype(bf16), v[hh]) for hh in heads]
        intra = [_dot(scores[hh].astype(bf16), v[hh]) for hh in heads]
        states = [states[hh] * cdecay_ref[hh] + kv_new[hh] for hh in heads]
        pieces = []
        for hh in heads:
            ret = intra[hh] + inter[hh]
            mu = jnp.mean(ret, axis=-1, keepdims=True)
            rc = ret - mu
            var = jnp.mean(rc * rc, axis=-1, keepdims=True)
            normed = rc * lax.rsqrt(var + GN_EPS)
            gate = proj[rows, o_g + hh * RET_V_DIM:o_g + (hh + 1) * RET_V_DIM]
            pieces.append((_silu(gate) * normed).astype(bf16))
        ret_pieces.append(pieces)
    cat_rows = [jnp.concatenate(ret_pieces[s] + swa_pieces[s], axis=1) for s in range(n_sub)]
    for hh in range(RET_HEADS):
        state_ref[hh] = states[hh]
    kprev_ref[...] = k_prev
    vprev_ref[...] = v_prev

    mix = _dot(jnp.concatenate(cat_rows, axis=0), w_out_ref[...])
    h2 = _layer_norm(DEEPNORM_ALPHA * h + mix, g_mix_ref[...], b_mix_ref[...])
    h2_ref[...] = h2
    h2p_ref[...] = _pack_bf16_pairs(h2)


def _t5_bucket(dist):
    n = jnp.maximum(dist, 0)
    max_exact = REL_BUCKETS // 2
    ratio = jnp.log(jnp.maximum(n, 1).astype(jnp.float32) / max_exact) / math.log(REL_MAX_DIST / max_exact)
    large = jnp.minimum(max_exact + (ratio * (REL_BUCKETS - max_exact)).astype(jnp.int32), REL_BUCKETS - 1)
    return jnp.where(n < max_exact, n, large)


def _mixer(x, ln_in_g, ln_in_b, w_in, w_out, rel_bias, sinks, ln_mix_g, ln_mix_b):
    B, S, D = x.shape
    R = MIX_ROWS
    C = RET_CHUNK
    W = SWA_WINDOW
    f32 = jnp.float32
    half = RET_QK_DIM // 2
    inv = ROPE_BASE ** (-jnp.arange(half, dtype=f32) / half)
    ang = jnp.arange(S, dtype=f32)[:, None] * inv[None, :]
    cos, sin = jnp.cos(ang), jnp.sin(ang)
    cos_t = jnp.tile(jnp.concatenate([cos, cos], axis=-1), (1, RET_HEADS))
    sin_t = jnp.tile(jnp.concatenate([-sin, sin], axis=-1), (1, RET_HEADS))
    rot = jnp.concatenate([cos_t, sin_t], axis=-1)
    log_gamma = jnp.log(1.0 - 2.0 ** (-5.0 - jnp.arange(RET_HEADS, dtype=f32)))
    idx = jnp.arange(C, dtype=f32)
    diff = idx[:, None] - idx[None, :]
    decay = jnp.where(diff[None] >= 0, jnp.exp(jnp.maximum(diff, 0.0)[None] * log_gamma[:, None, None]), 0.0)
    zeta = jnp.exp((C - 1.0 - idx)[None, :] * log_gamma[:, None])
    xi = jnp.exp((idx + 1.0)[None, :] * log_gamma[:, None])
    zeta_b = jnp.broadcast_to(zeta[:, :, None], (RET_HEADS, C, RET_QK_DIM))
    xi_b = jnp.broadcast_to(xi[:, :, None], (RET_HEADS, C, RET_V_DIM))
    cdecay = jnp.broadcast_to(jnp.exp(C * log_gamma)[:, None, None], (RET_HEADS, RET_QK_DIM, RET_V_DIM))
    i = jnp.arange(W)
    j = jnp.arange(2 * W)
    dist = i[:, None] + W - j[None, :]
    bucket = jnp.where((dist >= 0) & (dist < W), _t5_bucket(dist), -1).astype(jnp.int32).T
    sink_row = jnp.repeat(sinks.astype(f32), W).reshape(SWA_KV_HEADS, 1, SWA_GROUP * W)

    const = lambda shape: pl.BlockSpec(shape, lambda b, c, *_: (0,) * len(shape))
    grid_spec = pltpu.PrefetchScalarGridSpec(
        num_scalar_prefetch=1,
        grid=(B, S // R),
        in_specs=[
            pl.BlockSpec((None, R, D), lambda b, c, *_: (b, c, 0)),
            const((1, D)), const((1, D)),
            const((D, IN_WIDTH)), const((MIX_WIDTH, D)),
            pl.BlockSpec((R, 2 * RQK), lambda b, c, *_: (c, 0)),
            const((RET_HEADS, C, C)), const((RET_HEADS, C, RET_QK_DIM)), const((RET_HEADS, C, RET_V_DIM)),
            const((RET_HEADS, RET_QK_DIM, RET_V_DIM)),
            const((2 * W, W)), const((SWA_KV_HEADS, 1, SWA_GROUP * W)),
            const((1, D)), const((1, D)),
        ],
        out_specs=[pl.BlockSpec((R, D), lambda b, c, *_: (b * (S // R) + c, 0)),
                   pl.BlockSpec((R, D // 2), lambda b, c, *_: (b * (S // R) + c, 0))],
        scratch_shapes=[
            pltpu.VMEM((RET_HEADS, RET_QK_DIM, RET_V_DIM), f32),
            pltpu.VMEM((W, SKV), jnp.bfloat16),
            pltpu.VMEM((W, SKV), jnp.bfloat16),
            pltpu.VMEM((SWA_KV_HEADS, 2 * W, SWA_GROUP * W), f32),
        ],
    )
    return pl.pallas_call(
        _mixer_kernel,
        grid_spec=grid_spec,
        out_shape=[jax.ShapeDtypeStruct((B * S, D), f32), jax.ShapeDtypeStruct((B * S, D // 2), jnp.uint32)],
        compiler_params=pltpu.CompilerParams(
            dimension_semantics=("arbitrary", "arbitrary"), vmem_limit_bytes=VMEM_LIMIT_BYTES),
    )(rel_bias.astype(f32), x, ln_in_g.reshape(1, D), ln_in_b.reshape(1, D),
      w_in.astype(jnp.bfloat16), w_out.astype(jnp.bfloat16), rot, decay, zeta_b, xi_b, cdecay,
      bucket, sink_row, ln_mix_g.reshape(1, D), ln_mix_b.reshape(1, D))


def _router_kernel(h_ref, wr_ref, rb_ref, e_ref, w_ref, rk_ref, cnt_ref, run_ref):
    f32 = jnp.float32
    R = h_ref.shape[0]
    E = N_EXPERTS
    neg = -jnp.inf

    @pl.when(pl.program_id(0) == 0)
    def _init():
        run_ref[...] = jnp.zeros_like(run_ref)

    logits = _dot_nt(wr_ref[...], h_ref[...].astype(jnp.bfloat16))
    scores = 1.0 / (1.0 + jnp.exp(-logits))
    choice = scores + rb_ref[...]
    eid = lax.broadcasted_iota(jnp.int32, (E, R), 0)

    def first_argmax(vals, ids, none):
        m = jnp.max(vals, axis=0, keepdims=True)
        idx = jnp.min(jnp.where(vals == m, ids, none), axis=0, keepdims=True)
        return m, idx

    gid = lax.broadcasted_iota(jnp.int32, (GROUP_SIZE, R), 0)
    groups, gscore = [], []
    for g in range(N_GROUPS):
        vals = choice[g * GROUP_SIZE:(g + 1) * GROUP_SIZE]
        m1, i1 = first_argmax(vals, gid, GROUP_SIZE)
        m2 = jnp.max(jnp.where(gid == i1, neg, vals), axis=0, keepdims=True)
        groups.append(vals)
        gscore.append(m1 + m2)
    kept = []
    for g in range(N_GROUPS):
        beaten = jnp.zeros((1, R), f32)
        for g2 in range(N_GROUPS):
            if g2 == g:
                continue
            ahead = (gscore[g2] > gscore[g]) | (gscore[g2] == gscore[g]) if g2 < g else gscore[g2] > gscore[g]
            beaten = beaten + jnp.where(ahead, 1.0, 0.0)
        kept.append(jnp.where(beaten < TOPK_GROUPS, groups[g], neg))
    masked = jnp.concatenate(kept, axis=0)

    idxs, wts = [], []
    picked = jnp.zeros((E, R), f32)
    for _ in range(TOP_K):
        _, idx = first_argmax(masked, eid, E)
        hit = eid == idx
        idxs.append(idx)
        wts.append(jnp.sum(jnp.where(hit, scores, 0.0), axis=0, keepdims=True))
        masked = jnp.where(hit, neg, masked)
        picked = jnp.where(hit, 1.0, picked)
    wsum = wts[0]
    for k in range(1, TOP_K):
        wsum = wsum + wts[k]

    row = lax.broadcasted_iota(jnp.int32, (R, R), 0)
    col = lax.broadcasted_iota(jnp.int32, (R, R), 1)
    earlier = jnp.where(row < col, 1.0, 0.0).astype(jnp.bfloat16)
    picked_bf = picked.astype(jnp.bfloat16)
    run = run_ref[...]
    before = _dot(picked_bf, earlier) + jnp.concatenate([run] * (R // LANES), axis=1)
    sub_k = lax.broadcasted_iota(jnp.int32, (TOP_K, R), 0)
    e_out = jnp.zeros((TOP_K, R), jnp.int32)
    w_out = jnp.zeros((TOP_K, R), f32)
    rk_out = jnp.zeros((TOP_K, R), jnp.int32)
    for k in range(TOP_K):
        rank_k = jnp.sum(jnp.where(eid == idxs[k], before, 0.0), axis=0, keepdims=True)
        e_out = jnp.where(sub_k == k, idxs[k], e_out)
        w_out = jnp.where(sub_k == k, wts[k] / wsum * ROUTED_SCALE, w_out)
        rk_out = jnp.where(sub_k == k, rank_k.astype(jnp.int32), rk_out)
    e_ref[...] = e_out
    w_ref[...] = w_out
    rk_ref[...] = rk_out
    run_ref[...] = run + _dot(picked_bf, jnp.ones((R, LANES), jnp.bfloat16))
    cnt_ref[...] = run_ref[...]


def _router(h2, w_router, router_bias):
    T, D = h2.shape
    R = ROUTE_ROWS
    E = N_EXPERTS
    return pl.pallas_call(
        _router_kernel,
        grid=(T // R,),
        in_specs=[
            pl.BlockSpec((R, D), lambda i: (i, 0)),
            pl.BlockSpec((E, D), lambda i: (0, 0)),
            pl.BlockSpec((E, R), lambda i: (0, 0)),
        ],
        out_specs=[
            pl.BlockSpec((TOP_K, R), lambda i: (0, i)),
            pl.BlockSpec((TOP_K, R), lambda i: (0, i)),
            pl.BlockSpec((TOP_K, R), lambda i: (0, i)),
            pl.BlockSpec((E, LANES), lambda i: (0, 0)),
        ],
        out_shape=[
            jax.ShapeDtypeStruct((TOP_K, T), jnp.int32),
            jax.ShapeDtypeStruct((TOP_K, T), jnp.float32),
            jax.ShapeDtypeStruct((TOP_K, T), jnp.int32),
            jax.ShapeDtypeStruct((E, LANES), jnp.float32),
        ],
        scratch_shapes=[pltpu.VMEM((E, LANES), jnp.float32)],
        compiler_params=pltpu.CompilerParams(
            dimension_semantics=("arbitrary",), vmem_limit_bytes=VMEM_LIMIT_BYTES),
    )(h2, w_router.T.astype(jnp.bfloat16), jnp.broadcast_to(router_bias.astype(jnp.float32)[:, None], (E, R)))


def _positions_kernel(row_start_ref, e_ref, rk_ref, pos_ref):
    e = e_ref[...]
    rk = rk_ref[...]

    def per_expert(i, pos):
        return jnp.where(e == i, rk + row_start_ref[i], pos)

    pos_ref[...] = lax.fori_loop(0, N_EXPERTS, per_expert, jnp.zeros_like(rk), unroll=8)


def _positions(e_idx, rank, row_start):
    n = e_idx.size
    shape = (n // LANES, LANES)
    block = pl.BlockSpec((POSITION_ROWS, LANES), lambda i, *_: (i, 0))
    grid_spec = pltpu.PrefetchScalarGridSpec(
        num_scalar_prefetch=1,
        grid=(shape[0] // POSITION_ROWS,),
        in_specs=[block, block],
        out_specs=block,
    )
    pos = pl.pallas_call(
        _positions_kernel,
        grid_spec=grid_spec,
        out_shape=jax.ShapeDtypeStruct(shape, jnp.int32),
    )(row_start, e_idx.reshape(shape), rank.reshape(shape))
    return pos.reshape(n)


SC_WINDOW = 32
SC_DISPATCH_WINDOW = 64


def _dispatch_sc(h2, pos_kt, n_rows):
    T, D = h2.shape
    W = SC_DISPATCH_WINDOW
    idx = _window_indices(pos_kt, W)
    idx_rows = TOP_K * W // LANES
    mesh = plsc.VectorSubcoreMesh(core_axis_name="core", subcore_axis_name="subcore")

    @pl.kernel(out_type=jax.ShapeDtypeStruct((n_rows, D), h2.dtype), mesh=mesh, scratch_types=[])
    def scatter_rows(x_hbm, i_hbm, o_hbm):
        def body(x_vmem, i_vmem):
            for k in range(TOP_K):
                r, q = divmod(k * W, LANES)
                pltpu.sync_copy(x_vmem, o_hbm.at[i_vmem.at[r, pl.ds(q, W)]])

        pltpu.emit_pipeline(
            body,
            grid=(T // W,),
            in_specs=[pl.BlockSpec((W, D), lambda i: (i, 0)),
                      pl.BlockSpec((idx_rows, LANES), lambda i: (i, 0))],
            out_specs=[],
            core_axis_name=("core", "subcore"),
            dimension_semantics=(pltpu.PARALLEL,),
        )(x_hbm, i_hbm)

    return scatter_rows(h2, idx)


def _window_indices(pos_kt, window):
    K, T = pos_kt.shape
    return pos_kt.reshape(K, T // window, window).transpose(1, 0, 2).reshape(T * K // LANES, LANES)


def _gather_rows_sc(ys, pos_kt):
    n = pos_kt.size
    W = SC_WINDOW
    ys3 = ys.reshape(ys.shape[0] // ROW_TILE, ROW_TILE, LANES)
    idx = jnp.pad(pos_kt.reshape(n // W, W), ((0, 0), (0, LANES - W)))
    mesh = plsc.VectorSubcoreMesh(core_axis_name="core", subcore_axis_name="subcore")

    @pl.kernel(out_type=jax.ShapeDtypeStruct((n, ROW_TILE, LANES), ys.dtype), mesh=mesh, scratch_types=[])
    def gather_rows(y_hbm, i_hbm, o_hbm):
        def body(i_vmem, o_vmem):
            pltpu.sync_copy(y_hbm.at[i_vmem.at[0, pl.ds(0, W)]], o_vmem)

        pltpu.emit_pipeline(
            body,
            grid=(n // W,),
            in_specs=[pl.BlockSpec((1, LANES), lambda i: (i, 0))],
            out_specs=[pl.BlockSpec((W, ROW_TILE, LANES), lambda i: (i, 0, 0))],
            core_axis_name=("core", "subcore"),
            dimension_semantics=(pltpu.PARALLEL,),
        )(i_hbm, o_hbm)

    return gather_rows(ys3, idx).reshape(n * ROW_TILE, LANES)


EXPERT_SUBROWS = 128
X_SLOTS = 4
Y_SLOTS = 3


def _experts_kernel(blk_e_ref, first_ref, slot_ref, next_e_ref, valid_ref, n_act_ref, xs_hbm, wg_hbm,
                    wu_hbm, wd_hbm, ys_hbm, x_buf, y_buf, wg_buf, wu_buf, wd_buf, wg_bf, wu_bf, wd_bf,
                    sems, x_sems, y_sems):
    i = pl.program_id(0)
    n_act = n_act_ref[0]
    bf16 = jnp.bfloat16
    blk = EXPERT_ROWS * ROW_TILE

    def x_copy(j):
        n = pl.multiple_of((valid_ref[j] + 7) // 8 * 8, 8)
        src = xs_hbm.at[pl.ds(pl.multiple_of(j * EXPERT_ROWS, EXPERT_ROWS), n)]
        return pltpu.make_async_copy(src, x_buf.at[j % X_SLOTS, pl.ds(0, n)], x_sems.at[j % X_SLOTS])

    def y_copy(j):
        n = pl.multiple_of(valid_ref[j] * ROW_TILE, ROW_TILE)
        dst = ys_hbm.at[pl.ds(pl.multiple_of(j * blk, blk), n)]
        return pltpu.make_async_copy(y_buf.at[j % Y_SLOTS, pl.ds(0, n)], dst, y_sems.at[j % Y_SLOTS])

    @pl.when(i == 0)
    def _prime():
        x_buf[...] = jnp.zeros_like(x_buf)
        for j in range(X_SLOTS - 1):
            @pl.when(j < n_act)
            def _():
                x_copy(j).start()

    @pl.when(i + (X_SLOTS - 1) < n_act)
    def _prefetch():
        x_copy(i + (X_SLOTS - 1)).start()

    def weight_copies(e, slot):
        return (pltpu.make_async_copy(wg_hbm.at[e], wg_buf.at[slot], sems.at[slot]),
                pltpu.make_async_copy(wu_hbm.at[e], wu_buf.at[slot], sems.at[slot]),
                pltpu.make_async_copy(wd_hbm.at[e], wd_buf.at[slot], sems.at[slot]))

    @pl.when((i < n_act_ref[0]) & (first_ref[i] == 1))
    def _new_expert():
        slot = slot_ref[i]

        @pl.when(i == 0)
        def _():
            for c in weight_copies(blk_e_ref[0], 0):
                c.start()

        for c in weight_copies(blk_e_ref[i], slot):
            c.wait()

        @pl.when(next_e_ref[i] >= 0)
        def _():
            for c in weight_copies(next_e_ref[i], 1 - slot):
                c.start()

        wg_bf[...] = wg_buf[slot].astype(bf16)
        wu_bf[...] = wu_buf[slot].astype(bf16)
        wd_bf[...] = wd_buf[slot].astype(bf16)

    @pl.when(i < n_act)
    def _compute():
        x_copy(i).wait()

        @pl.when(i >= Y_SLOTS)
        def _():
            y_copy(i - Y_SLOTS).wait()

        sub = EXPERT_SUBROWS
        n_parts = (valid_ref[i] + sub - 1) // sub

        def run(parts):
            xs_ = [_unpack_bf16_pairs(x_buf[i % X_SLOTS, pl.ds(part * sub, sub), :]) for part in range(parts)]
            gs = [_dot(x, wg_bf[...]) for x in xs_]
            us = [_dot(x, wu_bf[...]) for x in xs_]
            acts = [(_silu(g) * u).astype(bf16) for g, u in zip(gs, us)]
            for part in range(parts):
                _store_rows(y_buf, _dot(acts[part], wd_bf[...]), lead=(i % Y_SLOTS,), first_row=part * sub)

        for parts in range(1, EXPERT_ROWS // sub + 1):
            pl.when(n_parts == parts)(functools.partial(run, parts))
        y_copy(i).start()

    @pl.when(i == n_act - 1)
    def _drain():
        for d in range(Y_SLOTS):
            @pl.when(i - d >= 0)
            def _():
                y_copy(i - d).wait()


def _experts(xs, blk_e, n_act, row_start, cnt, w_gate, w_up, w_down):
    D = D_MODEL
    BM = EXPERT_ROWS
    F = EXPERT_DIM
    n_blocks = xs.shape[0] // BM
    blk_in_expert = jnp.arange(n_blocks, dtype=jnp.int32) - row_start[blk_e] // BM
    valid = jnp.clip(cnt[blk_e] - blk_in_expert * BM, 0, BM).astype(jnp.int32)
    ids = jnp.arange(n_blocks, dtype=jnp.int32)
    active = ids < n_act[0]
    first = active & ((ids == 0) | (blk_e != jnp.roll(blk_e, 1)))
    slot = ((jnp.cumsum(first.astype(jnp.int32)) - 1) % 2).astype(jnp.int32)
    first_pos = jnp.where(first, ids, n_blocks)
    later_first = lax.cummin(jnp.concatenate([first_pos[1:], jnp.full((1,), n_blocks, jnp.int32)]), reverse=True)
    next_e = jnp.where(later_first < n_blocks, blk_e[jnp.minimum(later_first, n_blocks - 1)], -1).astype(jnp.int32)

    grid_spec = pltpu.PrefetchScalarGridSpec(
        num_scalar_prefetch=6,
        grid=(n_blocks,),
        in_specs=[pl.BlockSpec(memory_space=pl.ANY)] * 4,
        out_specs=pl.BlockSpec(memory_space=pl.ANY),
        scratch_shapes=[
            pltpu.VMEM((X_SLOTS, BM, D // 2), jnp.uint32),
            pltpu.VMEM((Y_SLOTS, BM * ROW_TILE, LANES), jnp.float32),
            pltpu.VMEM((2, D, F), jnp.float32), pltpu.VMEM((2, D, F), jnp.float32),
            pltpu.VMEM((2, F, D), jnp.float32),
            pltpu.VMEM((D, F), jnp.bfloat16), pltpu.VMEM((D, F), jnp.bfloat16),
            pltpu.VMEM((F, D), jnp.bfloat16),
            pltpu.SemaphoreType.DMA((2,)), pltpu.SemaphoreType.DMA((X_SLOTS,)),
            pltpu.SemaphoreType.DMA((Y_SLOTS,)),
        ],
    )
    return pl.pallas_call(
        _experts_kernel,
        grid_spec=grid_spec,
        out_shape=jax.ShapeDtypeStruct((xs.shape[0] * ROW_TILE, LANES), jnp.float32),
        compiler_params=pltpu.CompilerParams(
            dimension_semantics=("arbitrary",), vmem_limit_bytes=VMEM_LIMIT_BYTES),
    )(blk_e, first.astype(jnp.int32), slot, next_e, valid, n_act, xs, w_gate, w_up, w_down)


def _finish_kernel(h_ref, part_ref, w_ref, *rest):
    slabs = rest[:SC_COMBINE_SLOTS]
    g_ref, b_ref, out_ref = rest[SC_COMBINE_SLOTS:]
    R = h_ref.shape[0]
    w = w_ref[...]
    ffn = part_ref[...]
    for k in range(SC_COMBINE_SLOTS):
        ffn = ffn + _load_rows(slabs[k], R) * w[:, k:k + 1]
    out_ref[...] = _layer_norm(DEEPNORM_ALPHA * h_ref[...] + ffn, g_ref[...], b_ref[...])


def _finish(h2, partial, top_w, gathered, ln_g, ln_b):
    T, D = h2.shape
    R = FINISH_ROWS
    rows = pl.BlockSpec((R, D), lambda i: (i, 0))
    vec = pl.BlockSpec((1, D), lambda i: (0, 0))
    slab = lambda k: pl.BlockSpec((R * ROW_TILE, LANES), lambda i: (k * (T // R) + i, 0))
    return pl.pallas_call(
        _finish_kernel,
        grid=(T // R,),
        in_specs=[rows, rows, pl.BlockSpec((R, TOP_K), lambda i: (i, 0))]
        + [slab(k) for k in range(SC_COMBINE_SLOTS)] + [vec, vec],
        out_specs=rows,
        out_shape=jax.ShapeDtypeStruct((T, D), jnp.float32),
        compiler_params=pltpu.CompilerParams(
            dimension_semantics=("arbitrary",), vmem_limit_bytes=VMEM_LIMIT_BYTES),
    )(h2, partial, top_w, *([gathered] * SC_COMBINE_SLOTS), ln_g.reshape(1, D), ln_b.reshape(1, D))


SC_COMBINE_SLOTS = 5


def _combine_kernel(*refs):
    n_tc = TOP_K - SC_COMBINE_SLOTS
    pos_refs, pos_next_refs = refs[:n_tc], refs[n_tc:2 * n_tc]
    h_ref, w_ref, ys_ref, wsg_ref, wsu_ref, wsd_ref, out_ref, buf_ref, sems = refs[2 * n_tc:]
    R = h_ref.shape[0]
    i = pl.program_id(0)
    slot = i % 2

    def gather(p_refs, s):
        def issue(t, carry):
            for j in range(n_tc):
                pltpu.make_async_copy(ys_ref.at[_row_tile(p_refs[j][t])], buf_ref.at[s, j, _row_tile(t)],
                                      sems.at[s]).start(priority=j % 2)
            return carry

        lax.fori_loop(0, R, issue, 0)

    @pl.when(i == 0)
    def _():
        gather(pos_refs, 0)

    @pl.when(i + 1 < pl.num_programs(0))
    def _():
        gather(pos_next_refs, 1 - slot)

    h = h_ref[...]
    hb = h.astype(jnp.bfloat16)
    act = (_silu(_dot(hb, wsg_ref[...])) * _dot(hb, wsu_ref[...])).astype(jnp.bfloat16)
    ffn = _dot(act, wsd_ref[...])
    pltpu.make_async_copy(buf_ref.at[slot], buf_ref.at[slot], sems.at[slot]).wait()
    w = w_ref[...]
    for k in range(SC_COMBINE_SLOTS, TOP_K):
        ffn = ffn + _load_rows(buf_ref, R, lead=(slot, k - SC_COMBINE_SLOTS)) * w[:, k:k + 1]
    out_ref[...] = ffn


def _combine(h2, pos_kt, top_w, ys, ws_gate, ws_up, ws_down):
    T, D = h2.shape
    R = COMBINE_ROWS
    F = SHARED_DIM
    bf16 = jnp.bfloat16
    n_steps = T // R
    slots = range(SC_COMBINE_SLOTS, TOP_K)
    const = lambda shape: pl.BlockSpec(shape, lambda i: (0,) * len(shape))
    pos_now = [pl.BlockSpec((R,), lambda i, k=k: (k * n_steps + i,), memory_space=pltpu.SMEM) for k in slots]
    pos_next = [pl.BlockSpec((R,), lambda i, k=k: (k * n_steps + jnp.minimum(i + 1, n_steps - 1),),
                             memory_space=pltpu.SMEM) for k in slots]
    pos_flat = pos_kt.reshape(TOP_K * T)
    return pl.pallas_call(
        _combine_kernel,
        grid=(n_steps,),
        in_specs=pos_now + pos_next + [
            pl.BlockSpec((R, D), lambda i: (i, 0)),
            pl.BlockSpec((R, TOP_K), lambda i: (i, 0)),
            pl.BlockSpec(memory_space=pl.ANY),
            const((D, F)), const((D, F)), const((F, D)),
        ],
        out_specs=pl.BlockSpec((R, D), lambda i: (i, 0)),
        scratch_shapes=[pltpu.VMEM((2, TOP_K - SC_COMBINE_SLOTS, R * ROW_TILE, LANES), jnp.float32),
                        pltpu.SemaphoreType.DMA((2,))],
        out_shape=jax.ShapeDtypeStruct((T, D), jnp.float32),
        compiler_params=pltpu.CompilerParams(
            dimension_semantics=("arbitrary",), vmem_limit_bytes=VMEM_LIMIT_BYTES),
    )(*([pos_flat] * (2 * len(slots))), h2, top_w, ys, ws_gate.astype(bf16), ws_up.astype(bf16),
      ws_down.astype(bf16))


def _moe(h2, h2_packed, w_router, router_bias, w_gate, w_up, w_down, ws_gate, ws_up, ws_down, ln_g, ln_b):
    T = h2.shape[0]
    E = N_EXPERTS
    BM = EXPERT_ROWS
    e_idx, top_w, rank, counts = _router(h2, w_router, router_bias)
    cnt = counts[:, 0].astype(jnp.int32)
    nblk = (cnt + BM - 1) // BM
    blk_end = jnp.cumsum(nblk)
    row_start = ((blk_end - nblk) * BM).astype(jnp.int32)
    n_blocks = T * TOP_K // BM + E
    n_act = blk_end[-1:].astype(jnp.int32)
    blk_ids = jnp.minimum(jnp.arange(n_blocks, dtype=jnp.int32), n_act[0] - 1)
    blk_e = jnp.minimum(jnp.sum(blk_end[None, :] <= blk_ids[:, None], axis=1), E - 1).astype(jnp.int32)
    pos_kt = _positions(e_idx, rank, row_start).reshape(TOP_K, T)
    xs = _dispatch_sc(h2_packed, pos_kt, n_blocks * BM)
    ys = _experts(xs, blk_e, n_act, row_start, cnt, w_gate, w_up, w_down)
    gathered = _gather_rows_sc(ys, pos_kt[:SC_COMBINE_SLOTS])
    w_tk = top_w.T
    partial = _combine(h2, pos_kt, w_tk, ys, ws_gate, ws_up, ws_down)
    return _finish(h2, partial, w_tk, gathered, ln_g, ln_b)


def kernel(x, ln_in_g, ln_in_b, w_in, w_out, rel_bias, attn_sinks, ln_mix_g, ln_mix_b, w_router,
           router_bias, w_gate, w_up, w_down, ws_gate, ws_up, ws_down, ln_ffn_g, ln_ffn_b):
    B, S, D = x.shape
    h, h_packed = _mixer(x, ln_in_g, ln_in_b, w_in[0], w_out[0], rel_bias, attn_sinks[0], ln_mix_g[0],
                         ln_mix_b[0])
    out = _moe(h, h_packed, w_router[0], router_bias[0], w_gate[0], w_up[0], w_down[0],
               ws_gate[0], ws_up[0], ws_down[0], ln_ffn_g[0], ln_ffn_b[0])
    return out.reshape(B, S, D)
```

```python
import functools
import math

import jax
import jax.numpy as jnp
from jax import lax
from jax.experimental import pallas as pl
from jax.experimental.pallas import tpu as pltpu
from jax.experimental.pallas import tpu_sc as plsc

D_MODEL = 1024
DEPTH = 1
RET_HEADS = 4
RET_QK_DIM = 64
RET_V_DIM = 128
RET_CHUNK = 128
RET_WIDTH = RET_HEADS * RET_V_DIM
ROPE_BASE = 10000.0
SWA_HEADS = 8
SWA_KV_HEADS = 2
SWA_GROUP = SWA_HEADS // SWA_KV_HEADS
SWA_HEAD_DIM = 64
SWA_WINDOW = 128
SWA_WIDTH = SWA_HEADS * SWA_HEAD_DIM
MIX_WIDTH = RET_WIDTH + SWA_WIDTH
RQK = RET_HEADS * RET_QK_DIM
SKV = SWA_KV_HEADS * SWA_HEAD_DIM
IN_SIZES = (RQK, RQK, RET_WIDTH, RET_WIDTH, SWA_WIDTH, SKV, SKV)
IN_OFFS = tuple(sum(IN_SIZES[:i]) for i in range(len(IN_SIZES)))
IN_WIDTH = sum(IN_SIZES)
REL_BUCKETS = 32
REL_MAX_DIST = 128
N_EXPERTS = 256
TOP_K = 8
N_GROUPS = 8
GROUP_SIZE = N_EXPERTS // N_GROUPS
TOPK_GROUPS = 4
EXPERT_DIM = 256
SHARED_DIM = 256
ROUTED_SCALE = 2.5
LN_EPS = 1e-5
GN_EPS = 1e-6
DEEPNORM_ALPHA = (2 * DEPTH) ** 0.25
MASK_VALUE = -1e30

VMEM_LIMIT_BYTES = 56 * 1024 * 1024

MIX_ROWS = 512
ROUTE_ROWS = 256
EXPERT_ROWS = 512
COMBINE_ROWS = 512
POSITION_ROWS = 64
FINISH_ROWS = 512


def _layer_norm(x, g, b):
    mu = jnp.mean(x, axis=-1, keepdims=True)
    xc = x - mu
    var = jnp.mean(xc * xc, axis=-1, keepdims=True)
    return xc * lax.rsqrt(var + LN_EPS) * g + b


def _dot(a, b):
    return jnp.dot(a, b, preferred_element_type=jnp.float32)


def _dot_nt(a, b):
    return lax.dot_general(a, b, (((1,), (1,)), ((), ())), preferred_element_type=jnp.float32)


def _dot_tn(a, b):
    return lax.dot_general(a, b, (((0,), (0,)), ((), ())), preferred_element_type=jnp.float32)


def _silu(x):
    return x * (1.0 / (1.0 + jnp.exp(-x)))


LANES = 128
ROW_TILE = D_MODEL // LANES


def _load_rows(ref, n_rows, lead=()):
    return jnp.concatenate([ref[lead + (pl.ds(s, n_rows, stride=ROW_TILE), slice(None))]
                            for s in range(ROW_TILE)], axis=1)


def _store_rows(ref, val, lead=(), first_row=0):
    n_rows = val.shape[0]
    for s in range(ROW_TILE):
        dst = pl.ds(first_row * ROW_TILE + s, n_rows, stride=ROW_TILE)
        ref[lead + (dst, slice(None))] = val[:, s * LANES:(s + 1) * LANES]


def _row_tile(r):
    return pl.ds(pl.multiple_of(r * ROW_TILE, ROW_TILE), ROW_TILE)


def _pack_bf16_pairs(x):
    m = x.shape[1] // 2
    bits = lax.bitcast_convert_type(x.astype(jnp.bfloat16).astype(jnp.float32), jnp.uint32)
    return (bits[:, :m] >> 16) | (bits[:, m:] & jnp.uint32(0xFFFF0000))


def _unpack_bf16_pairs(p):
    lo = lax.bitcast_convert_type(p << 16, jnp.float32)
    hi = lax.bitcast_convert_type(p & jnp.uint32(0xFFFF0000), jnp.float32)
    return jnp.concatenate([lo, hi], axis=1).astype(jnp.bfloat16)


def _swap_halves(x):
    n = x.shape[-1]
    half = RET_QK_DIM // 2
    lane = lax.broadcasted_iota(jnp.int32, x.shape, 1)
    from_right = pltpu.roll(x, n - half, axis=1)
    from_left = pltpu.roll(x, half, axis=1)
    return jnp.where((lane % RET_QK_DIM) < half, from_right, from_left)


def _mixer_kernel(rel_bias_ref, x_ref, g_in_ref, b_in_ref, w_in_ref, w_out_ref, rot_ref, decay_ref,
                  zeta_ref, xi_ref, cdecay_ref, bucket_ref, sink_ref, g_mix_ref, b_mix_ref,
                  h2_ref, h2p_ref, state_ref, kprev_ref, vprev_ref, bias_ref):
    b_id = pl.program_id(0)
    c_id = pl.program_id(1)
    W = SWA_WINDOW

    @pl.when((b_id == 0) & (c_id == 0))
    def _build_bias():
        bucket = bucket_ref[...]
        for h in range(SWA_HEADS):
            acc = jnp.full((2 * W, W), MASK_VALUE, jnp.float32)
            for b in range(REL_BUCKETS):
                acc = jnp.where(bucket == b, rel_bias_ref[b, h], acc)
            kh, g = divmod(h, SWA_GROUP)
            bias_ref[kh, :, g * W:(g + 1) * W] = acc

    @pl.when(c_id == 0)
    def _reset():
        state_ref[...] = jnp.zeros_like(state_ref)
        kprev_ref[...] = jnp.zeros_like(kprev_ref)
        vprev_ref[...] = jnp.zeros_like(vprev_ref)

    h = _layer_norm(x_ref[...], g_in_ref[...], b_in_ref[...])
    proj = _dot(h.astype(jnp.bfloat16), w_in_ref[...])

    o_q, o_k, o_v, o_g, o_sq, o_sk, o_sv = IN_OFFS
    cos_t = rot_ref[:, :RQK]
    sin_t = rot_ref[:, RQK:]
    q_all = proj[:, o_q:o_q + RQK]
    k_all = proj[:, o_k:o_k + RQK]
    q_rot = q_all * cos_t + _swap_halves(q_all) * sin_t
    k_rot = (k_all * cos_t + _swap_halves(k_all) * sin_t) * (RET_QK_DIM ** -0.5)

    n_sub = x_ref.shape[0] // RET_CHUNK
    states = [state_ref[hh] for hh in range(RET_HEADS)]
    k_prev = kprev_ref[...]
    v_prev = vprev_ref[...]
    bf16 = jnp.bfloat16
    heads = range(RET_HEADS)
    ret_pieces, swa_pieces = [], []
    swa_jobs = []
    for s in range(n_sub):
        rows = slice(s * RET_CHUNK, (s + 1) * RET_CHUNK)
        k_cur = proj[rows, o_sk:o_sk + SKV].astype(bf16)
        v_cur = proj[rows, o_sv:o_sv + SKV].astype(bf16)
        for kh in range(SWA_KV_HEADS):
            kv = slice(kh * SWA_HEAD_DIM, (kh + 1) * SWA_HEAD_DIM)
            q4 = jnp.concatenate(
                [proj[rows, o_sq + (kh * SWA_GROUP + g) * SWA_HEAD_DIM:
                      o_sq + (kh * SWA_GROUP + g + 1) * SWA_HEAD_DIM] for g in range(SWA_GROUP)],
                axis=0) * (SWA_HEAD_DIM ** -0.5)
            kcat = jnp.concatenate([k_prev[:, kv], k_cur[:, kv]], axis=0)
            vcat = jnp.concatenate([v_prev[:, kv], v_cur[:, kv]], axis=0)
            swa_jobs.append((s, kh, q4.astype(bf16), kcat, vcat))
        k_prev, v_prev = k_cur, v_cur
    logits_all = [_dot_nt(kcat, q4) + bias_ref[kh] for (s, kh, q4, kcat, vcat) in swa_jobs]
    probs_all = []
    for (s, kh, q4, kcat, vcat), logits in zip(swa_jobs, logits_all):
        if s == 0:
            key = lax.broadcasted_iota(jnp.int32, logits.shape, 0)
            logits = logits + jnp.where((key < W) & (c_id == 0), MASK_VALUE, 0.0)
        sink = sink_ref[kh]
        m = jnp.maximum(jnp.max(logits, axis=0, keepdims=True), sink)
        p = jnp.exp(logits - m)
        den = jnp.sum(p, axis=0, keepdims=True) + jnp.exp(sink - m)
        probs_all.append((p / den).astype(bf16))
    o4_all = [_dot_tn(job[4], probs) for job, probs in zip(swa_jobs, probs_all)]
    for s in range(n_sub):
        swa_pieces.append([o4[:, g * W:(g + 1) * W].T.astype(bf16)
                           for job, o4 in zip(swa_jobs, o4_all) if job[0] == s for g in range(SWA_GROUP)])
    for s in range(n_sub):
        rows = slice(s * RET_CHUNK, (s + 1) * RET_CHUNK)
        qk = [slice(hh * RET_QK_DIM, (hh + 1) * RET_QK_DIM) for hh in heads]
        q = [q_rot[rows, qk[hh]].astype(bf16) for hh in heads]
        k32 = [k_rot[rows, qk[hh]] for hh in heads]
        v = [proj[rows, o_v + hh * RET_V_DIM:o_v + (hh + 1) * RET_V_DIM].astype(bf16) for hh in heads]
        scores = [_dot_nt(q[hh], k32[hh].astype(bf16)) * decay_ref[hh] for hh in heads]
        inter = [_dot(q[hh], states[hh].astype(bf16)) * xi_ref[hh] for hh in heads]
        kv_new = [_dot_tn((k32[hh] * zeta_ref[hh]).astype(bf16), v[hh]) for hh in heads]
        intra = [_dot(scores[hh].astype(bf16), v[hh]) for hh in heads]
        states = [states[hh] * cdecay_ref[hh] + kv_new[hh] for hh in heads]
        pieces = []
        for hh in heads:
            ret = intra[hh] + inter[hh]
            mu = jnp.mean(ret, axis=-1, keepdims=True)
            rc = ret - mu
            var = jnp.mean(rc * rc, axis=-1, keepdims=True)
            normed = rc * lax.rsqrt(var + GN_EPS)
            gate = proj[rows, o_g + hh * RET_V_DIM:o_g + (hh + 1) * RET_V_DIM]
            pieces.append((_silu(gate) * normed).astype(bf16))
        ret_pieces.append(pieces)
    cat_rows = [jnp.concatenate(ret_pieces[s] + swa_pieces[s], axis=1) for s in range(n_sub)]
    for hh in range(RET_HEADS):
        state_ref[hh] = states[hh]
    kprev_ref[...] = k_prev
    vprev_ref[...] = v_prev

    mix = _dot(jnp.concatenate(cat_rows, axis=0), w_out_ref[...])
    h2 = _layer_norm(DEEPNORM_ALPHA * h + mix, g_mix_ref[...], b_mix_ref[...])
    h2_ref[...] = h2
    h2p_ref[...] = _pack_bf16_pairs(h2)


def _t5_bucket(dist):
    n = jnp.maximum(dist, 0)
    max_exact = REL_BUCKETS // 2
    ratio = jnp.log(jnp.maximum(n, 1).astype(jnp.float32) / max_exact) / math.log(REL_MAX_DIST / max_exact)
    large = jnp.minimum(max_exact + (ratio * (REL_BUCKETS - max_exact)).astype(jnp.int32), REL_BUCKETS - 1)
    return jnp.where(n < max_exact, n, large)


def _mixer(x, ln_in_g, ln_in_b, w_in, w_out, rel_bias, sinks, ln_mix_g, ln_mix_b):
    B, S, D = x.shape
    R = MIX_ROWS
    C = RET_CHUNK
    W = SWA_WINDOW
    f32 = jnp.float32
    half = RET_QK_DIM // 2
    inv = ROPE_BASE ** (-jnp.arange(half, dtype=f32) / half)
    ang = jnp.arange(S, dtype=f32)[:, None] * inv[None, :]
    cos, sin = jnp.cos(ang), jnp.sin(ang)
    cos_t = jnp.tile(jnp.concatenate([cos, cos], axis=-1), (1, RET_HEADS))
    sin_t = jnp.tile(jnp.concatenate([-sin, sin], axis=-1), (1, RET_HEADS))
    rot = jnp.concatenate([cos_t, sin_t], axis=-1)
    log_gamma = jnp.log(1.0 - 2.0 ** (-5.0 - jnp.arange(RET_HEADS, dtype=f32)))
    idx = jnp.arange(C, dtype=f32)
    diff = idx[:, None] - idx[None, :]
    decay = jnp.where(diff[None] >= 0, jnp.exp(jnp.maximum(diff, 0.0)[None] * log_gamma[:, None, None]), 0.0)
    zeta = jnp.exp((C - 1.0 - idx)[None, :] * log_gamma[:, None])
    xi = jnp.exp((idx + 1.0)[None, :] * log_gamma[:, None])
    zeta_b = jnp.broadcast_to(zeta[:, :, None], (RET_HEADS, C, RET_QK_DIM))
    xi_b = jnp.broadcast_to(xi[:, :, None], (RET_HEADS, C, RET_V_DIM))
    cdecay = jnp.broadcast_to(jnp.exp(C * log_gamma)[:, None, None], (RET_HEADS, RET_QK_DIM, RET_V_DIM))
    i = jnp.arange(W)
    j = jnp.arange(2 * W)
    dist = i[:, None] + W - j[None, :]
    bucket = jnp.where((dist >= 0) & (dist < W), _t5_bucket(dist), -1).astype(jnp.int32).T
    sink_row = jnp.repeat(sinks.astype(f32), W).reshape(SWA_KV_HEADS, 1, SWA_GROUP * W)

    const = lambda shape: pl.BlockSpec(shape, lambda b, c, *_: (0,) * len(shape))
    grid_spec = pltpu.PrefetchScalarGridSpec(
        num_scalar_prefetch=1,
        grid=(B, S // R),
        in_specs=[
            pl.BlockSpec((None, R, D), lambda b, c, *_: (b, c, 0)),
            const((1, D)), const((1, D)),
            const((D, IN_WIDTH)), const((MIX_WIDTH, D)),
            pl.BlockSpec((R, 2 * RQK), lambda b, c, *_: (c, 0)),
            const((RET_HEADS, C, C)), const((RET_HEADS, C, RET_QK_DIM)), const((RET_HEADS, C, RET_V_DIM)),
            const((RET_HEADS, RET_QK_DIM, RET_V_DIM)),
            const((2 * W, W)), const((SWA_KV_HEADS, 1, SWA_GROUP * W)),
            const((1, D)), const((1, D)),
        ],
        out_specs=[pl.BlockSpec((R, D), lambda b, c, *_: (b * (S // R) + c, 0)),
                   pl.BlockSpec((R, D // 2), lambda b, c, *_: (b * (S // R) + c, 0))],
        scratch_shapes=[
            pltpu.VMEM((RET_HEADS, RET_QK_DIM, RET_V_DIM), f32),
            pltpu.VMEM((W, SKV), jnp.bfloat16),
            pltpu.VMEM((W, SKV), jnp.bfloat16),
            pltpu.VMEM((SWA_KV_HEADS, 2 * W, SWA_GROUP * W), f32),
        ],
    )
    return pl.pallas_call(
        _mixer_kernel,
        grid_spec=grid_spec,
        out_shape=[jax.ShapeDtypeStruct((B * S, D), f32), jax.ShapeDtypeStruct((B * S, D // 2), jnp.uint32)],
        compiler_params=pltpu.CompilerParams(
            dimension_semantics=("arbitrary", "arbitrary"), vmem_limit_bytes=VMEM_LIMIT_BYTES),
    )(rel_bias.astype(f32), x, ln_in_g.reshape(1, D), ln_in_b.reshape(1, D),
      w_in.astype(jnp.bfloat16), w_out.astype(jnp.bfloat16), rot, decay, zeta_b, xi_b, cdecay,
      bucket, sink_row, ln_mix_g.reshape(1, D), ln_mix_b.reshape(1, D))


def _router_kernel(h_ref, wr_ref, rb_ref, e_ref, w_ref, rk_ref, cnt_ref, run_ref):
    f32 = jnp.float32
    R = h_ref.shape[0]
    E = N_EXPERTS
    neg = -jnp.inf

    @pl.when(pl.program_id(0) == 0)
    def _init():
        run_ref[...] = jnp.zeros_like(run_ref)

    logits = _dot_nt(wr_ref[...], h_ref[...].astype(jnp.bfloat16))
    scores = 1.0 / (1.0 + jnp.exp(-logits))
    choice = scores + rb_ref[...]
    eid = lax.broadcasted_iota(jnp.int32, (E, R), 0)

    def first_argmax(vals, ids, none):
        m = jnp.max(vals, axis=0, keepdims=True)
        idx = jnp.min(jnp.where(vals == m, ids, none), axis=0, keepdims=True)
        return m, idx

    gid = lax.broadcasted_iota(jnp.int32, (GROUP_SIZE, R), 0)
    groups, gscore = [], []
    for g in range(N_GROUPS):
        vals = choice[g * GROUP_SIZE:(g + 1) * GROUP_SIZE]
        m1, i1 = first_argmax(vals, gid, GROUP_SIZE)
        m2 = jnp.max(jnp.where(gid == i1, neg, vals), axis=0, keepdims=True)
        groups.append(vals)
        gscore.append(m1 + m2)
    kept = []
    for g in range(N_GROUPS):
        beaten = jnp.zeros((1, R), f32)
        for g2 in range(N_GROUPS):
            if g2 == g:
                continue
            ahead = (gscore[g2] > gscore[g]) | (gscore[g2] == gscore[g]) if g2 < g else gscore[g2] > gscore[g]
            beaten = beaten + jnp.where(ahead, 1.0, 0.0)
        kept.append(jnp.where(beaten < TOPK_GROUPS, groups[g], neg))
    masked = jnp.concatenate(kept, axis=0)

    idxs, wts = [], []
    picked = jnp.zeros((E, R), f32)
    for _ in range(TOP_K):
        _, idx = first_argmax(masked, eid, E)
        hit = eid == idx
        idxs.append(idx)
        wts.append(jnp.sum(jnp.where(hit, scores, 0.0), axis=0, keepdims=True))
        masked = jnp.where(hit, neg, masked)
        picked = jnp.where(hit, 1.0, picked)
    wsum = wts[0]
    for k in range(1, TOP_K):
        wsum = wsum + wts[k]

    row = lax.broadcasted_iota(jnp.int32, (R, R), 0)
    col = lax.broadcasted_iota(jnp.int32, (R, R), 1)
    earlier = jnp.where(row < col, 1.0, 0.0).astype(jnp.bfloat16)
    picked_bf = picked.astype(jnp.bfloat16)
    run = run_ref[...]
    before = _dot(picked_bf, earlier) + jnp.concatenate([run] * (R // LANES), axis=1)
    sub_k = lax.broadcasted_iota(jnp.int32, (TOP_K, R), 0)
    e_out = jnp.zeros((TOP_K, R), jnp.int32)
    w_out = jnp.zeros((TOP_K, R), f32)
    rk_out = jnp.zeros((TOP_K, R), jnp.int32)
    for k in range(TOP_K):
        rank_k = jnp.sum(jnp.where(eid == idxs[k], before, 0.0), axis=0, keepdims=True)
        e_out = jnp.where(sub_k == k, idxs[k], e_out)
        w_out = jnp.where(sub_k == k, wts[k] / wsum * ROUTED_SCALE, w_out)
        rk_out = jnp.where(sub_k == k, rank_k.astype(jnp.int32), rk_out)
    e_ref[...] = e_out
    w_ref[...] = w_out
    rk_ref[...] = rk_out
    run_ref[...] = run + _dot(picked_bf, jnp.ones((R, LANES), jnp.bfloat16))
    cnt_ref[...] = run_ref[...]


def _router(h2, w_router, router_bias):
    T, D = h2.shape
    R = ROUTE_ROWS
    E = N_EXPERTS
    return pl.pallas_call(
        _router_kernel,
        grid=(T // R,),
        in_specs=[
            pl.BlockSpec((R, D), lambda i: (i, 0)),
            pl.BlockSpec((E, D), lambda i: (0, 0)),
            pl.BlockSpec((E, R), lambda i: (0, 0)),
        ],
        out_specs=[
            pl.BlockSpec((TOP_K, R), lambda i: (0, i)),
            pl.BlockSpec((TOP_K, R), lambda i: (0, i)),
            pl.BlockSpec((TOP_K, R), lambda i: (0, i)),
            pl.BlockSpec((E, LANES), lambda i: (0, 0)),
        ],
        out_shape=[
            jax.ShapeDtypeStruct((TOP_K, T), jnp.int32),
            jax.ShapeDtypeStruct((TOP_K, T), jnp.float32),
            jax.ShapeDtypeStruct((TOP_K, T), jnp.int32),
            jax.ShapeDtypeStruct((E, LANES), jnp.float32),
        ],
        scratch_shapes=[pltpu.VMEM((E, LANES), jnp.float32)],
        compiler_params=pltpu.CompilerParams(
            dimension_semantics=("arbitrary",), vmem_limit_bytes=VMEM_LIMIT_BYTES),
    )(h2, w_router.T.astype(jnp.bfloat16), jnp.broadcast_to(router_bias.astype(jnp.float32)[:, None], (E, R)))


def _positions_kernel(row_start_ref, e_ref, rk_ref, pos_ref):
    e = e_ref[...]
    rk = rk_ref[...]

    def per_expert(i, pos):
        return jnp.where(e == i, rk + row_start_ref[i], pos)

    pos_ref[...] = lax.fori_loop(0, N_EXPERTS, per_expert, jnp.zeros_like(rk), unroll=8)


def _positions(e_idx, rank, row_start):
    n = e_idx.size
    shape = (n // LANES, LANES)
    block = pl.BlockSpec((POSITION_ROWS, LANES), lambda i, *_: (i, 0))
    grid_spec = pltpu.PrefetchScalarGridSpec(
        num_scalar_prefetch=1,
        grid=(shape[0] // POSITION_ROWS,),
        in_specs=[block, block],
        out_specs=block,
    )
    pos = pl.pallas_call(
        _positions_kernel,
        grid_spec=grid_spec,
        out_shape=jax.ShapeDtypeStruct(shape, jnp.int32),
    )(row_start, e_idx.reshape(shape), rank.reshape(shape))
    return pos.reshape(n)


SC_WINDOW = 32
SC_DISPATCH_WINDOW = 64


def _dispatch_sc(h2, pos_kt, n_rows):
    T, D = h2.shape
    W = SC_DISPATCH_WINDOW
    idx = _window_indices(pos_kt, W)
    idx_rows = TOP_K * W // LANES
    mesh = plsc.VectorSubcoreMesh(core_axis_name="core", subcore_axis_name="subcore")

    @pl.kernel(out_type=jax.ShapeDtypeStruct((n_rows, D), h2.dtype), mesh=mesh, scratch_types=[])
    def scatter_rows(x_hbm, i_hbm, o_hbm):
        def body(x_vmem, i_vmem):
            for k in range(TOP_K):
                r, q = divmod(k * W, LANES)
                pltpu.sync_copy(x_vmem, o_hbm.at[i_vmem.at[r, pl.ds(q, W)]])

        pltpu.emit_pipeline(
            body,
            grid=(T // W,),
            in_specs=[pl.BlockSpec((W, D), lambda i: (i, 0)),
                      pl.BlockSpec((idx_rows, LANES), lambda i: (i, 0))],
            out_specs=[],
            core_axis_name=("core", "subcore"),
            dimension_semantics=(pltpu.PARALLEL,),
        )(x_hbm, i_hbm)

    return scatter_rows(h2, idx)


def _window_indices(pos_kt, window):
    K, T = pos_kt.shape
    return pos_kt.reshape(K, T // window, window).transpose(1, 0, 2).reshape(T * K // LANES, LANES)


def _gather_rows_sc(ys, pos_kt):
    n = pos_kt.size
    W = SC_WINDOW
    ys3 = ys.reshape(ys.shape[0] // ROW_TILE, ROW_TILE, LANES)
    idx = jnp.pad(pos_kt.reshape(n // W, W), ((0, 0), (0, LANES - W)))
    mesh = plsc.VectorSubcoreMesh(core_axis_name="core", subcore_axis_name="subcore")

    @pl.kernel(out_type=jax.ShapeDtypeStruct((n, ROW_TILE, LANES), ys.dtype), mesh=mesh, scratch_types=[])
    def gather_rows(y_hbm, i_hbm, o_hbm):
        def body(i_vmem, o_vmem):
            pltpu.sync_copy(y_hbm.at[i_vmem.at[0, pl.ds(0, W)]], o_vmem)

        pltpu.emit_pipeline(
            body,
            grid=(n // W,),
            in_specs=[pl.BlockSpec((1, LANES), lambda i: (i, 0))],
            out_specs=[pl.BlockSpec((W, ROW_TILE, LANES), lambda i: (i, 0, 0))],
            core_axis_name=("core", "subcore"),
            dimension_semantics=(pltpu.PARALLEL,),
        )(i_hbm, o_hbm)

    return gather_rows(ys3, idx).reshape(n * ROW_TILE, LANES)


EXPERT_SUBROWS = 128
X_SLOTS = 4
Y_SLOTS = 3


def _experts_kernel(blk_e_ref, first_ref, slot_ref, next_e_ref, valid_ref, n_act_ref, xs_hbm, wg_hbm,
                    wu_hbm, wd_hbm, ys_hbm, x_buf, y_buf, wg_buf, wu_buf, wd_buf, wg_bf, wu_bf, wd_bf,
                    sems, x_sems, y_sems):
    i = pl.program_id(0)
    n_act = n_act_ref[0]
    bf16 = jnp.bfloat16
    blk = EXPERT_ROWS * ROW_TILE

    def x_copy(j):
        n = pl.multiple_of((valid_ref[j] + 7) // 8 * 8, 8)
        src = xs_hbm.at[pl.ds(pl.multiple_of(j * EXPERT_ROWS, EXPERT_ROWS), n)]
        return pltpu.make_async_copy(src, x_buf.at[j % X_SLOTS, pl.ds(0, n)], x_sems.at[j % X_SLOTS])

    def y_copy(j):
        n = pl.multiple_of(valid_ref[j] * ROW_TILE, ROW_TILE)
        dst = ys_hbm.at[pl.ds(pl.multiple_of(j * blk, blk), n)]
        return pltpu.make_async_copy(y_buf.at[j % Y_SLOTS, pl.ds(0, n)], dst, y_sems.at[j % Y_SLOTS])

    @pl.when(i == 0)
    def _prime():
        x_buf[...] = jnp.zeros_like(x_buf)
        for j in range(X_SLOTS - 1):
            @pl.when(j < n_act)
            def _():
                x_copy(j).start()

    @pl.when(i + (X_SLOTS - 1) < n_act)
    def _prefetch():
        x_copy(i + (X_SLOTS - 1)).start()

    def weight_copies(e, slot):
        return (pltpu.make_async_copy(wg_hbm.at[e], wg_buf.at[slot], sems.at[slot]),
                pltpu.make_async_copy(wu_hbm.at[e], wu_buf.at[slot], sems.at[slot]),
                pltpu.make_async_copy(wd_hbm.at[e], wd_buf.at[slot], sems.at[slot]))

    @pl.when((i < n_act_ref[0]) & (first_ref[i] == 1))
    def _new_expert():
        slot = slot_ref[i]

        @pl.when(i == 0)
        def _():
            for c in weight_copies(blk_e_ref[0], 0):
                c.start()

        for c in weight_copies(blk_e_ref[i], slot):
            c.wait()

        @pl.when(next_e_ref[i] >= 0)
        def _():
            for c in weight_copies(next_e_ref[i], 1 - slot):
                c.start()

        wg_bf[...] = wg_buf[slot].astype(bf16)
        wu_bf[...] = wu_buf[slot].astype(bf16)
        wd_bf[...] = wd_buf[slot].astype(bf16)

    @pl.when(i < n_act)
    def _compute():
        x_copy(i).wait()

        @pl.when(i >= Y_SLOTS)
        def _():
            y_copy(i - Y_SLOTS).wait()

        sub = EXPERT_SUBROWS
        n_parts = (valid_ref[i] + sub - 1) // sub

        def run(parts):
            xs_ = [_unpack_bf16_pairs(x_buf[i % X_SLOTS, pl.ds(part * sub, sub), :]) for part in range(parts)]
            gs = [_dot(x, wg_bf[...]) for x in xs_]
            us = [_dot(x, wu_bf[...]) for x in xs_]
            acts = [(_silu(g) * u).astype(bf16) for g, u in zip(gs, us)]
            for part in range(parts):
                _store_rows(y_buf, _dot(acts[part], wd_bf[...]), lead=(i % Y_SLOTS,), first_row=part * sub)

        for parts in range(1, EXPERT_ROWS // sub + 1):
            pl.when(n_parts == parts)(functools.partial(run, parts))
        y_copy(i).start()

    @pl.when(i == n_act - 1)
    def _drain():
        for d in range(Y_SLOTS):
            @pl.when(i - d >= 0)
            def _():
                y_copy(i - d).wait()


def _experts(xs, blk_e, n_act, row_start, cnt, w_gate, w_up, w_down):
    D = D_MODEL
    BM = EXPERT_ROWS
    F = EXPERT_DIM
    n_blocks = xs.shape[0] // BM
    blk_in_expert = jnp.arange(n_blocks, dtype=jnp.int32) - row_start[blk_e] // BM
    valid = jnp.clip(cnt[blk_e] - blk_in_expert * BM, 0, BM).astype(jnp.int32)
    ids = jnp.arange(n_blocks, dtype=jnp.int32)
    active = ids < n_act[0]
    first = active & ((ids == 0) | (blk_e != jnp.roll(blk_e, 1)))
    slot = ((jnp.cumsum(first.astype(jnp.int32)) - 1) % 2).astype(jnp.int32)
    first_pos = jnp.where(first, ids, n_blocks)
    later_first = lax.cummin(jnp.concatenate([first_pos[1:], jnp.full((1,), n_blocks, jnp.int32)]), reverse=True)
    next_e = jnp.where(later_first < n_blocks, blk_e[jnp.minimum(later_first, n_blocks - 1)], -1).astype(jnp.int32)

    grid_spec = pltpu.PrefetchScalarGridSpec(
        num_scalar_prefetch=6,
        grid=(n_blocks,),
        in_specs=[pl.BlockSpec(memory_space=pl.ANY)] * 4,
        out_specs=pl.BlockSpec(memory_space=pl.ANY),
        scratch_shapes=[
            pltpu.VMEM((X_SLOTS, BM, D // 2), jnp.uint32),
            pltpu.VMEM((Y_SLOTS, BM * ROW_TILE, LANES), jnp.float32),
            pltpu.VMEM((2, D, F), jnp.float32), pltpu.VMEM((2, D, F), jnp.float32),
            pltpu.VMEM((2, F, D), jnp.float32),
            pltpu.VMEM((D, F), jnp.bfloat16), pltpu.VMEM((D, F), jnp.bfloat16),
            pltpu.VMEM((F, D), jnp.bfloat16),
            pltpu.SemaphoreType.DMA((2,)), pltpu.SemaphoreType.DMA((X_SLOTS,)),
            pltpu.SemaphoreType.DMA((Y_SLOTS,)),
        ],
    )
    return pl.pallas_call(
        _experts_kernel,
        grid_spec=grid_spec,
        out_shape=jax.ShapeDtypeStruct((xs.shape[0] * ROW_TILE, LANES), jnp.float32),
        compiler_params=pltpu.CompilerParams(
            dimension_semantics=("arbitrary",), vmem_limit_bytes=VMEM_LIMIT_BYTES),
    )(blk_e, first.astype(jnp.int32), slot, next_e, valid, n_act, xs, w_gate, w_up, w_down)


def _finish_kernel(part_ref, w_ref, *rest):
    slabs = rest[:SC_COMBINE_SLOTS]
    g_ref, b_ref, out_ref = rest[SC_COMBINE_SLOTS:]
    R = part_ref.shape[0]
    w = w_ref[...]
    acc = part_ref[...]
    for k in range(SC_COMBINE_SLOTS):
        acc = acc + _load_rows(slabs[k], R) * w[:, k:k + 1]
    out_ref[...] = _layer_norm(acc, g_ref[...], b_ref[...])


def _finish(partial, top_w, gathered, ln_g, ln_b):
    T, D = partial.shape
    R = FINISH_ROWS
    rows = pl.BlockSpec((R, D), lambda i: (i, 0))
    vec = pl.BlockSpec((1, D), lambda i: (0, 0))
    slab = lambda k: pl.BlockSpec((R * ROW_TILE, LANES), lambda i: (k * (T // R) + i, 0))
    return pl.pallas_call(
        _finish_kernel,
        grid=(T // R,),
        in_specs=[rows, pl.BlockSpec((R, TOP_K), lambda i: (i, 0))]
        + [slab(k) for k in range(SC_COMBINE_SLOTS)] + [vec, vec],
        out_specs=rows,
        out_shape=jax.ShapeDtypeStruct((T, D), jnp.float32),
        compiler_params=pltpu.CompilerParams(
            dimension_semantics=("arbitrary",), vmem_limit_bytes=VMEM_LIMIT_BYTES),
    )(partial, top_w, *([gathered] * SC_COMBINE_SLOTS), ln_g.reshape(1, D), ln_b.reshape(1, D))


SC_COMBINE_SLOTS = 5


def _combine_kernel(*refs):
    n_tc = TOP_K - SC_COMBINE_SLOTS
    pos_refs, pos_next_refs = refs[:n_tc], refs[n_tc:2 * n_tc]
    h_ref, w_ref, ys_ref, wsg_ref, wsu_ref, wsd_ref, out_ref, buf_ref, sems = refs[2 * n_tc:]
    R = h_ref.shape[0]
    i = pl.program_id(0)
    slot = i % 2

    def gather(p_refs, s):
        def issue(t, carry):
            for j in range(n_tc):
                pltpu.make_async_copy(ys_ref.at[_row_tile(p_refs[j][t])], buf_ref.at[s, j, _row_tile(t)],
                                      sems.at[s]).start(priority=j % 2)
            return carry

        lax.fori_loop(0, R, issue, 0)

    @pl.when(i == 0)
    def _():
        gather(pos_refs, 0)

    @pl.when(i + 1 < pl.num_programs(0))
    def _():
        gather(pos_next_refs, 1 - slot)

    h = h_ref[...]
    hb = h.astype(jnp.bfloat16)
    act = (_silu(_dot(hb, wsg_ref[...])) * _dot(hb, wsu_ref[...])).astype(jnp.bfloat16)
    ffn = _dot(act, wsd_ref[...])
    pltpu.make_async_copy(buf_ref.at[slot], buf_ref.at[slot], sems.at[slot]).wait()
    w = w_ref[...]
    for k in range(SC_COMBINE_SLOTS, TOP_K):
        ffn = ffn + _load_rows(buf_ref, R, lead=(slot, k - SC_COMBINE_SLOTS)) * w[:, k:k + 1]
    out_ref[...] = DEEPNORM_ALPHA * h + ffn


def _combine(h2, pos_kt, top_w, ys, ws_gate, ws_up, ws_down):
    T, D = h2.shape
    R = COMBINE_ROWS
    F = SHARED_DIM
    bf16 = jnp.bfloat16
    n_steps = T // R
    slots = range(SC_COMBINE_SLOTS, TOP_K)
    const = lambda shape: pl.BlockSpec(shape, lambda i: (0,) * len(shape))
    pos_now = [pl.BlockSpec((R,), lambda i, k=k: (k * n_steps + i,), memory_space=pltpu.SMEM) for k in slots]
    pos_next = [pl.BlockSpec((R,), lambda i, k=k: (k * n_steps + jnp.minimum(i + 1, n_steps - 1),),
                             memory_space=pltpu.SMEM) for k in slots]
    pos_flat = pos_kt.reshape(TOP_K * T)
    return pl.pallas_call(
        _combine_kernel,
        grid=(n_steps,),
        in_specs=pos_now + pos_next + [
            pl.BlockSpec((R, D), lambda i: (i, 0)),
            pl.BlockSpec((R, TOP_K), lambda i: (i, 0)),
            pl.BlockSpec(memory_space=pl.ANY),
            const((D, F)), const((D, F)), const((F, D)),
        ],
        out_specs=pl.BlockSpec((R, D), lambda i: (i, 0)),
        scratch_shapes=[pltpu.VMEM((2, TOP_K - SC_COMBINE_SLOTS, R * ROW_TILE, LANES), jnp.float32),
                        pltpu.SemaphoreType.DMA((2,))],
        out_shape=jax.ShapeDtypeStruct((T, D), jnp.float32),
        compiler_params=pltpu.CompilerParams(
            dimension_semantics=("arbitrary",), vmem_limit_bytes=VMEM_LIMIT_BYTES),
    )(*([pos_flat] * (2 * len(slots))), h2, top_w, ys, ws_gate.astype(bf16), ws_up.astype(bf16),
      ws_down.astype(bf16))


def _moe(h2, h2_packed, w_router, router_bias, w_gate, w_up, w_down, ws_gate, ws_up, ws_down, ln_g, ln_b):
    T = h2.shape[0]
    E = N_EXPERTS
    BM = EXPERT_ROWS
    e_idx, top_w, rank, counts = _router(h2, w_router, router_bias)
    cnt = counts[:, 0].astype(jnp.int32)
    nblk = (cnt + BM - 1) // BM
    blk_end = jnp.cumsum(nblk)
    row_start = ((blk_end - nblk) * BM).astype(jnp.int32)
    n_blocks = T * TOP_K // BM + E
    n_act = blk_end[-1:].astype(jnp.int32)
    blk_ids = jnp.minimum(jnp.arange(n_blocks, dtype=jnp.int32), n_act[0] - 1)
    blk_e = jnp.minimum(jnp.sum(blk_end[None, :] <= blk_ids[:, None], axis=1), E - 1).astype(jnp.int32)
    pos_kt = _positions(e_idx, rank, row_start).reshape(TOP_K, T)
    xs = _dispatch_sc(h2_packed, pos_kt, n_blocks * BM)
    ys = _experts(xs, blk_e, n_act, row_start, cnt, w_gate, w_up, w_down)
    gathered = _gather_rows_sc(ys, pos_kt[:SC_COMBINE_SLOTS])
    w_tk = top_w.T
    partial = _combine(h2, pos_kt, w_tk, ys, ws_gate, ws_up, ws_down)
    return _finish(partial, w_tk, gathered, ln_g, ln_b)


def kernel(x, ln_in_g, ln_in_b, w_in, w_out, rel_bias, attn_sinks, ln_mix_g, ln_mix_b, w_router,
           router_bias, w_gate, w_up, w_down, ws_gate, ws_up, ws_down, ln_ffn_g, ln_ffn_b):
    B, S, D = x.shape
    h, h_packed = _mixer(x, ln_in_g, ln_in_b, w_in[0], w_out[0], rel_bias, attn_sinks[0], ln_mix_g[0],
                         ln_mix_b[0])
    out = _moe(h, h_packed, w_router[0], router_bias[0], w_gate[0], w_up[0], w_down[0],
               ws_gate[0], ws_up[0], ws_down[0], ln_ffn_g[0], ln_ffn_b[0])
    return out.reshape(B, S, D)
```

```python
import functools
import math

import jax
import jax.numpy as jnp
from jax import lax
from jax.experimental import pallas as pl
from jax.experimental.pallas import tpu as pltpu
from jax.experimental.pallas import tpu_sc as plsc

D_MODEL = 1024
DEPTH = 1
RET_HEADS = 4
RET_QK_DIM = 64
RET_V_DIM = 128
RET_CHUNK = 128
RET_WIDTH = RET_HEADS * RET_V_DIM
ROPE_BASE = 10000.0
SWA_HEADS = 8
SWA_KV_HEADS = 2
SWA_GROUP = SWA_HEADS // SWA_KV_HEADS
SWA_HEAD_DIM = 64
SWA_WINDOW = 128
SWA_WIDTH = SWA_HEADS * SWA_HEAD_DIM
MIX_WIDTH = RET_WIDTH + SWA_WIDTH
RQK = RET_HEADS * RET_QK_DIM
SKV = SWA_KV_HEADS * SWA_HEAD_DIM
IN_SIZES = (RQK, RQK, RET_WIDTH, RET_WIDTH, SWA_WIDTH, SKV, SKV)
IN_OFFS = tuple(sum(IN_SIZES[:i]) for i in range(len(IN_SIZES)))
IN_WIDTH = sum(IN_SIZES)
REL_BUCKETS = 32
REL_MAX_DIST = 128
N_EXPERTS = 256
TOP_K = 8
N_GROUPS = 8
GROUP_SIZE = N_EXPERTS // N_GROUPS
TOPK_GROUPS = 4
EXPERT_DIM = 256
SHARED_DIM = 256
ROUTED_SCALE = 2.5
LN_EPS = 1e-5
GN_EPS = 1e-6
DEEPNORM_ALPHA = (2 * DEPTH) ** 0.25
MASK_VALUE = -1e30

VMEM_LIMIT_BYTES = 56 * 1024 * 1024

MIX_ROWS = 512
ROUTE_ROWS = 256
EXPERT_ROWS = 512
COMBINE_ROWS = 512
POSITION_ROWS = 64
FINISH_ROWS = 512


def _layer_norm(x, g, b):
    mu = jnp.mean(x, axis=-1, keepdims=True)
    xc = x - mu
    var = jnp.mean(xc * xc, axis=-1, keepdims=True)
    return xc * lax.rsqrt(var + LN_EPS) * g + b


def _dot(a, b):
    return jnp.dot(a, b, preferred_element_type=jnp.float32)


def _dot_nt(a, b):
    return lax.dot_general(a, b, (((1,), (1,)), ((), ())), preferred_element_type=jnp.float32)


def _dot_tn(a, b):
    return lax.dot_general(a, b, (((0,), (0,)), ((), ())), preferred_element_type=jnp.float32)


def _silu(x):
    return x * (1.0 / (1.0 + jnp.exp(-x)))


LANES = 128
ROW_TILE = D_MODEL // LANES


def _load_rows(ref, n_rows, lead=()):
    return jnp.concatenate([ref[lead + (pl.ds(s, n_rows, stride=ROW_TILE), slice(None))]
                            for s in range(ROW_TILE)], axis=1)


def _store_rows(ref, val, lead=(), first_row=0):
    n_rows = val.shape[0]
    for s in range(ROW_TILE):
        dst = pl.ds(first_row * ROW_TILE + s, n_rows, stride=ROW_TILE)
        ref[lead + (dst, slice(None))] = val[:, s * LANES:(s + 1) * LANES]


def _row_tile(r):
    return pl.ds(pl.multiple_of(r * ROW_TILE, ROW_TILE), ROW_TILE)


def _pack_bf16_pairs(x):
    m = x.shape[1] // 2
    bits = lax.bitcast_convert_type(x.astype(jnp.bfloat16).astype(jnp.float32), jnp.uint32)
    return (bits[:, :m] >> 16) | (bits[:, m:] & jnp.uint32(0xFFFF0000))


def _unpack_bf16_pairs(p):
    lo = lax.bitcast_convert_type(p << 16, jnp.float32)
    hi = lax.bitcast_convert_type(p & jnp.uint32(0xFFFF0000), jnp.float32)
    return jnp.concatenate([lo, hi], axis=1).astype(jnp.bfloat16)


def _swap_halves(x):
    n = x.shape[-1]
    half = RET_QK_DIM // 2
    lane = lax.broadcasted_iota(jnp.int32, x.shape, 1)
    from_right = pltpu.roll(x, n - half, axis=1)
    from_left = pltpu.roll(x, half, axis=1)
    return jnp.where((lane % RET_QK_DIM) < half, from_right, from_left)


def _mixer_kernel(rel_bias_ref, x_ref, g_in_ref, b_in_ref, w_in_ref, w_out_ref, rot_ref, decay_ref,
                  zeta_ref, xi_ref, cdecay_ref, bucket_ref, sink_ref, g_mix_ref, b_mix_ref,
                  h2_ref, h2p_ref, state_ref, kprev_ref, vprev_ref, bias_ref):
    b_id = pl.program_id(0)
    c_id = pl.program_id(1)
    W = SWA_WINDOW

    @pl.when((b_id == 0) & (c_id == 0))
    def _build_bias():
        bucket = bucket_ref[...]
        for h in range(SWA_HEADS):
            acc = jnp.full((2 * W, W), MASK_VALUE, jnp.float32)
            for b in range(REL_BUCKETS):
                acc = jnp.where(bucket == b, rel_bias_ref[b, h], acc)
            kh, g = divmod(h, SWA_GROUP)
            bias_ref[kh, :, g * W:(g + 1) * W] = acc

    @pl.when(c_id == 0)
    def _reset():
        state_ref[...] = jnp.zeros_like(state_ref)
        kprev_ref[...] = jnp.zeros_like(kprev_ref)
        vprev_ref[...] = jnp.zeros_like(vprev_ref)

    h = _layer_norm(x_ref[...], g_in_ref[...], b_in_ref[...])
    proj = _dot(h.astype(jnp.bfloat16), w_in_ref[...])

    o_q, o_k, o_v, o_g, o_sq, o_sk, o_sv = IN_OFFS
    cos_t = rot_ref[:, :RQK]
    sin_t = rot_ref[:, RQK:]
    q_all = proj[:, o_q:o_q + RQK]
    k_all = proj[:, o_k:o_k + RQK]
    q_rot = q_all * cos_t + _swap_halves(q_all) * sin_t
    k_rot = (k_all * cos_t + _swap_halves(k_all) * sin_t) * (RET_QK_DIM ** -0.5)

    n_sub = x_ref.shape[0] // RET_CHUNK
    states = [state_ref[hh] for hh in range(RET_HEADS)]
    k_prev = kprev_ref[...]
    v_prev = vprev_ref[...]
    bf16 = jnp.bfloat16
    heads = range(RET_HEADS)
    ret_pieces, swa_pieces = [], []
    swa_jobs = []
    for s in range(n_sub):
        rows = slice(s * RET_CHUNK, (s + 1) * RET_CHUNK)
        k_cur = proj[rows, o_sk:o_sk + SKV].astype(bf16)
        v_cur = proj[rows, o_sv:o_sv + SKV].astype(bf16)
        for kh in range(SWA_KV_HEADS):
            kv = slice(kh * SWA_HEAD_DIM, (kh + 1) * SWA_HEAD_DIM)
            q4 = jnp.concatenate(
                [proj[rows, o_sq + (kh * SWA_GROUP + g) * SWA_HEAD_DIM:
                      o_sq + (kh * SWA_GROUP + g + 1) * SWA_HEAD_DIM] for g in range(SWA_GROUP)],
                axis=0) * (SWA_HEAD_DIM ** -0.5)
            kcat = jnp.concatenate([k_prev[:, kv], k_cur[:, kv]], axis=0)
            vcat = jnp.concatenate([v_prev[:, kv], v_cur[:, kv]], axis=0)
            swa_jobs.append((s, kh, q4.astype(bf16), kcat, vcat))
        k_prev, v_prev = k_cur, v_cur
    logits_all = [_dot_nt(kcat, q4) + bias_ref[kh] for (s, kh, q4, kcat, vcat) in swa_jobs]
    probs_all = []
    for (s, kh, q4, kcat, vcat), logits in zip(swa_jobs, logits_all):
        if s == 0:
            key = lax.broadcasted_iota(jnp.int32, logits.shape, 0)
            logits = logits + jnp.where((key < W) & (c_id == 0), MASK_VALUE, 0.0)
        sink = sink_ref[kh]
        m = jnp.maximum(jnp.max(logits, axis=0, keepdims=True), sink)
        p = jnp.exp(logits - m)
        den = jnp.sum(p, axis=0, keepdims=True) + jnp.exp(sink - m)
        probs_all.append((p / den).astype(bf16))
    o4_all = [_dot_tn(job[4], probs) for job, probs in zip(swa_jobs, probs_all)]
    for s in range(n_sub):
        swa_pieces.append([o4[:, g * W:(g + 1) * W].T.astype(bf16)
                           for job, o4 in zip(swa_jobs, o4_all) if job[0] == s for g in range(SWA_GROUP)])
    for s in range(n_sub):
        rows = slice(s * RET_CHUNK, (s + 1) * RET_CHUNK)
        qk = [slice(hh * RET_QK_DIM, (hh + 1) * RET_QK_DIM) for hh in heads]
        q = [q_rot[rows, qk[hh]].astype(bf16) for hh in heads]
        k32 = [k_rot[rows, qk[hh]] for hh in heads]
        v = [proj[rows, o_v + hh * RET_V_DIM:o_v + (hh + 1) * RET_V_DIM].astype(bf16) for hh in heads]
        scores = [_dot_nt(q[hh], k32[hh].astype(bf16)) * decay_ref[hh] for hh in heads]
        inter = [_dot(q[hh], states[hh].astype(bf16)) * xi_ref[hh] for hh in heads]
        kv_new = [_dot_tn((k32[hh] * zeta_ref[hh]).astype(bf16), v[hh]) for hh in heads]
        intra = [_dot(scores[hh].astype(bf16), v[hh]) for hh in heads]
        states = [states[hh] * cdecay_ref[hh] + kv_new[hh] for hh in heads]
        pieces = []
        for hh in heads:
            ret = intra[hh] + inter[hh]
            mu = jnp.mean(ret, axis=-1, keepdims=True)
            rc = ret - mu
            var = jnp.mean(rc * rc, axis=-1, keepdims=True)
            normed = rc * lax.rsqrt(var + GN_EPS)
            gate = proj[rows, o_g + hh * RET_V_DIM:o_g + (hh + 1) * RET_V_DIM]
            pieces.append((_silu(gate) * normed).astype(bf16))
        ret_pieces.append(pieces)
    cat_rows = [jnp.concatenate(ret_pieces[s] + swa_pieces[s], axis=1) for s in range(n_sub)]
    for hh in range(RET_HEADS):
        state_ref[hh] = states[hh]
    kprev_ref[...] = k_prev
    vprev_ref[...] = v_prev

    mix = _dot(jnp.concatenate(cat_rows, axis=0), w_out_ref[...])
    h2 = _layer_norm(DEEPNORM_ALPHA * h + mix, g_mix_ref[...], b_mix_ref[...])
    h2_ref[...] = h2
    h2p_ref[...] = _pack_bf16_pairs(h2)


def _t5_bucket(dist):
    n = jnp.maximum(dist, 0)
    max_exact = REL_BUCKETS // 2
    ratio = jnp.log(jnp.maximum(n, 1).astype(jnp.float32) / max_exact) / math.log(REL_MAX_DIST / max_exact)
    large = jnp.minimum(max_exact + (ratio * (REL_BUCKETS - max_exact)).astype(jnp.int32), REL_BUCKETS - 1)
    return jnp.where(n < max_exact, n, large)


def _mixer(x, ln_in_g, ln_in_b, w_in, w_out, rel_bias, sinks, ln_mix_g, ln_mix_b):
    B, S, D = x.shape
    R = MIX_ROWS
    C = RET_CHUNK
    W = SWA_WINDOW
    f32 = jnp.float32
    half = RET_QK_DIM // 2
    inv = ROPE_BASE ** (-jnp.arange(half, dtype=f32) / half)
    ang = jnp.arange(S, dtype=f32)[:, None] * inv[None, :]
    cos, sin = jnp.cos(ang), jnp.sin(ang)
    cos_t = jnp.tile(jnp.concatenate([cos, cos], axis=-1), (1, RET_HEADS))
    sin_t = jnp.tile(jnp.concatenate([-sin, sin], axis=-1), (1, RET_HEADS))
    rot = jnp.concatenate([cos_t, sin_t], axis=-1)
    log_gamma = jnp.log(1.0 - 2.0 ** (-5.0 - jnp.arange(RET_HEADS, dtype=f32)))
    idx = jnp.arange(C, dtype=f32)
    diff = idx[:, None] - idx[None, :]
    decay = jnp.where(diff[None] >= 0, jnp.exp(jnp.maximum(diff, 0.0)[None] * log_gamma[:, None, None]), 0.0)
    zeta = jnp.exp((C - 1.0 - idx)[None, :] * log_gamma[:, None])
    xi = jnp.exp((idx + 1.0)[None, :] * log_gamma[:, None])
    zeta_b = jnp.broadcast_to(zeta[:, :, None], (RET_HEADS, C, RET_QK_DIM))
    xi_b = jnp.broadcast_to(xi[:, :, None], (RET_HEADS, C, RET_V_DIM))
    cdecay = jnp.broadcast_to(jnp.exp(C * log_gamma)[:, None, None], (RET_HEADS, RET_QK_DIM, RET_V_DIM))
    i = jnp.arange(W)
    j = jnp.arange(2 * W)
    dist = i[:, None] + W - j[None, :]
    bucket = jnp.where((dist >= 0) & (dist < W), _t5_bucket(dist), -1).astype(jnp.int32).T
    sink_row = jnp.repeat(sinks.astype(f32), W).reshape(SWA_KV_HEADS, 1, SWA_GROUP * W)

    const = lambda shape: pl.BlockSpec(shape, lambda b, c, *_: (0,) * len(shape))
    grid_spec = pltpu.PrefetchScalarGridSpec(
        num_scalar_prefetch=1,
        grid=(B, S // R),
        in_specs=[
            pl.BlockSpec((None, R, D), lambda b, c, *_: (b, c, 0)),
            const((1, D)), const((1, D)),
            const((D, IN_WIDTH)), const((MIX_WIDTH, D)),
            pl.BlockSpec((R, 2 * RQK), lambda b, c, *_: (c, 0)),
            const((RET_HEADS, C, C)), const((RET_HEADS, C, RET_QK_DIM)), const((RET_HEADS, C, RET_V_DIM)),
            const((RET_HEADS, RET_QK_DIM, RET_V_DIM)),
            const((2 * W, W)), const((SWA_KV_HEADS, 1, SWA_GROUP * W)),
            const((1, D)), const((1, D)),
        ],
        out_specs=[pl.BlockSpec((R, D), lambda b, c, *_: (b * (S // R) + c, 0)),
                   pl.BlockSpec((R, D // 2), lambda b, c, *_: (b * (S // R) + c, 0))],
        scratch_shapes=[
            pltpu.VMEM((RET_HEADS, RET_QK_DIM, RET_V_DIM), f32),
            pltpu.VMEM((W, SKV), jnp.bfloat16),
            pltpu.VMEM((W, SKV), jnp.bfloat16),
            pltpu.VMEM((SWA_KV_HEADS, 2 * W, SWA_GROUP * W), f32),
        ],
    )
    return pl.pallas_call(
        _mixer_kernel,
        grid_spec=grid_spec,
        out_shape=[jax.ShapeDtypeStruct((B * S, D), f32), jax.ShapeDtypeStruct((B * S, D // 2), jnp.uint32)],
        compiler_params=pltpu.CompilerParams(
            dimension_semantics=("arbitrary", "arbitrary"), vmem_limit_bytes=VMEM_LIMIT_BYTES),
    )(rel_bias.astype(f32), x, ln_in_g.reshape(1, D), ln_in_b.reshape(1, D),
      w_in.astype(jnp.bfloat16), w_out.astype(jnp.bfloat16), rot, decay, zeta_b, xi_b, cdecay,
      bucket, sink_row, ln_mix_g.reshape(1, D), ln_mix_b.reshape(1, D))


def _router_kernel(h_ref, wr_ref, rb_ref, e_ref, w_ref, rk_ref, cnt_ref, run_ref):
    f32 = jnp.float32
    R = h_ref.shape[0]
    E = N_EXPERTS
    neg = -jnp.inf

    @pl.when(pl.program_id(0) == 0)
    def _init():
        run_ref[...] = jnp.zeros_like(run_ref)

    logits = _dot_nt(wr_ref[...], h_ref[...].astype(jnp.bfloat16))
    scores = 1.0 / (1.0 + jnp.exp(-logits))
    choice = scores + rb_ref[...]
    eid = lax.broadcasted_iota(jnp.int32, (E, R), 0)

    def first_argmax(vals, ids, none):
        m = jnp.max(vals, axis=0, keepdims=True)
        idx = jnp.min(jnp.where(vals == m, ids, none), axis=0, keepdims=True)
        return m, idx

    gid = lax.broadcasted_iota(jnp.int32, (GROUP_SIZE, R), 0)
    groups, gscore = [], []
    for g in range(N_GROUPS):
        vals = choice[g * GROUP_SIZE:(g + 1) * GROUP_SIZE]
        m1, i1 = first_argmax(vals, gid, GROUP_SIZE)
        m2 = jnp.max(jnp.where(gid == i1, neg, vals), axis=0, keepdims=True)
        groups.append(vals)
        gscore.append(m1 + m2)
    kept = []
    for g in range(N_GROUPS):
        beaten = jnp.zeros((1, R), f32)
        for g2 in range(N_GROUPS):
            if g2 == g:
                continue
            ahead = (gscore[g2] > gscore[g]) | (gscore[g2] == gscore[g]) if g2 < g else gscore[g2] > gscore[g]
            beaten = beaten + jnp.where(ahead, 1.0, 0.0)
        kept.append(jnp.where(beaten < TOPK_GROUPS, groups[g], neg))
    masked = jnp.concatenate(kept, axis=0)

    idxs, wts = [], []
    picked = jnp.zeros((E, R), f32)
    for _ in range(TOP_K):
        _, idx = first_argmax(masked, eid, E)
        hit = eid == idx
        idxs.append(idx)
        wts.append(jnp.sum(jnp.where(hit, scores, 0.0), axis=0, keepdims=True))
        masked = jnp.where(hit, neg, masked)
        picked = jnp.where(hit, 1.0, picked)
    wsum = wts[0]
    for k in range(1, TOP_K):
        wsum = wsum + wts[k]

    row = lax.broadcasted_iota(jnp.int32, (R, R), 0)
    col = lax.broadcasted_iota(jnp.int32, (R, R), 1)
    earlier = jnp.where(row < col, 1.0, 0.0).astype(jnp.bfloat16)
    picked_bf = picked.astype(jnp.bfloat16)
    run = run_ref[...]
    before = _dot(picked_bf, earlier) + jnp.concatenate([run] * (R // LANES), axis=1)
    sub_k = lax.broadcasted_iota(jnp.int32, (TOP_K, R), 0)
    e_out = jnp.zeros((TOP_K, R), jnp.int32)
    w_out = jnp.zeros((TOP_K, R), f32)
    rk_out = jnp.zeros((TOP_K, R), jnp.int32)
    for k in range(TOP_K):
        rank_k = jnp.sum(jnp.where(eid == idxs[k], before, 0.0), axis=0, keepdims=True)
        e_out = jnp.where(sub_k == k, idxs[k], e_out)
        w_out = jnp.where(sub_k == k, wts[k] / wsum * ROUTED_SCALE, w_out)
        rk_out = jnp.where(sub_k == k, rank_k.astype(jnp.int32), rk_out)
    e_ref[...] = e_out
    w_ref[...] = w_out
    rk_ref[...] = rk_out
    run_ref[...] = run + _dot(picked_bf, jnp.ones((R, LANES), jnp.bfloat16))
    cnt_ref[...] = run_ref[...]


def _router(h2, w_router, router_bias):
    T, D = h2.shape
    R = ROUTE_ROWS
    E = N_EXPERTS
    return pl.pallas_call(
        _router_kernel,
        grid=(T // R,),
        in_specs=[
            pl.BlockSpec((R, D), lambda i: (i, 0)),
            pl.BlockSpec((E, D), lambda i: (0, 0)),
            pl.BlockSpec((E, R), lambda i: (0, 0)),
        ],
        out_specs=[
            pl.BlockSpec((TOP_K, R), lambda i: (0, i)),
            pl.BlockSpec((TOP_K, R), lambda i: (0, i)),
            pl.BlockSpec((TOP_K, R), lambda i: (0, i)),
            pl.BlockSpec((E, LANES), lambda i: (0, 0)),
        ],
        out_shape=[
            jax.ShapeDtypeStruct((TOP_K, T), jnp.int32),
            jax.ShapeDtypeStruct((TOP_K, T), jnp.float32),
            jax.ShapeDtypeStruct((TOP_K, T), jnp.int32),
            jax.ShapeDtypeStruct((E, LANES), jnp.float32),
        ],
        scratch_shapes=[pltpu.VMEM((E, LANES), jnp.float32)],
        compiler_params=pltpu.CompilerParams(
            dimension_semantics=("arbitrary",), vmem_limit_bytes=VMEM_LIMIT_BYTES),
    )(h2, w_router.T.astype(jnp.bfloat16), jnp.broadcast_to(router_bias.astype(jnp.float32)[:, None], (E, R)))


def _positions_kernel(row_start_ref, e_ref, rk_ref, pos_ref):
    e = e_ref[...]
    rk = rk_ref[...]

    def per_expert(i, pos):
        return jnp.where(e == i, rk + row_start_ref[i], pos)

    pos_ref[...] = lax.fori_loop(0, N_EXPERTS, per_expert, jnp.zeros_like(rk), unroll=8)


def _positions(e_idx, rank, row_start):
    n = e_idx.size
    shape = (n // LANES, LANES)
    block = pl.BlockSpec((POSITION_ROWS, LANES), lambda i, *_: (i, 0))
    grid_spec = pltpu.PrefetchScalarGridSpec(
        num_scalar_prefetch=1,
        grid=(shape[0] // POSITION_ROWS,),
        in_specs=[block, block],
        out_specs=block,
    )
    pos = pl.pallas_call(
        _positions_kernel,
        grid_spec=grid_spec,
        out_shape=jax.ShapeDtypeStruct(shape, jnp.int32),
    )(row_start, e_idx.reshape(shape), rank.reshape(shape))
    return pos.reshape(n)


SC_WINDOW = 32
SC_DISPATCH_WINDOW = 64


def _dispatch_sc(h2, pos_kt, n_rows):
    T, D = h2.shape
    W = SC_DISPATCH_WINDOW
    idx = _window_indices(pos_kt, W)
    idx_rows = TOP_K * W // LANES
    mesh = plsc.VectorSubcoreMesh(core_axis_name="core", subcore_axis_name="subcore")

    @pl.kernel(out_type=jax.ShapeDtypeStruct((n_rows, D), h2.dtype), mesh=mesh, scratch_types=[])
    def scatter_rows(x_hbm, i_hbm, o_hbm):
        def body(x_vmem, i_vmem):
            for k in range(TOP_K):
                r, q = divmod(k * W, LANES)
                pltpu.sync_copy(x_vmem, o_hbm.at[i_vmem.at[r, pl.ds(q, W)]])

        pltpu.emit_pipeline(
            body,
            grid=(T // W,),
            in_specs=[pl.BlockSpec((W, D), lambda i: (i, 0)),
                      pl.BlockSpec((idx_rows, LANES), lambda i: (i, 0))],
            out_specs=[],
            core_axis_name=("core", "subcore"),
            dimension_semantics=(pltpu.PARALLEL,),
        )(x_hbm, i_hbm)

    return scatter_rows(h2, idx)


def _window_indices(pos_kt, window):
    K, T = pos_kt.shape
    return pos_kt.reshape(K, T // window, window).transpose(1, 0, 2).reshape(T * K // LANES, LANES)


def _gather_rows_sc(ys, pos_kt):
    n = pos_kt.size
    W = SC_WINDOW
    ys3 = ys.reshape(ys.shape[0] // ROW_TILE, ROW_TILE, LANES)
    idx = jnp.pad(pos_kt.reshape(n // W, W), ((0, 0), (0, LANES - W)))
    mesh = plsc.VectorSubcoreMesh(core_axis_name="core", subcore_axis_name="subcore")

    @pl.kernel(out_type=jax.ShapeDtypeStruct((n, ROW_TILE, LANES), ys.dtype), mesh=mesh, scratch_types=[])
    def gather_rows(y_hbm, i_hbm, o_hbm):
        def body(i_vmem, o_vmem):
            pltpu.sync_copy(y_hbm.at[i_vmem.at[0, pl.ds(0, W)]], o_vmem)

        pltpu.emit_pipeline(
            body,
            grid=(n // W,),
            in_specs=[pl.BlockSpec((1, LANES), lambda i: (i, 0))],
            out_specs=[pl.BlockSpec((W, ROW_TILE, LANES), lambda i: (i, 0, 0))],
            core_axis_name=("core", "subcore"),
            dimension_semantics=(pltpu.PARALLEL,),
        )(i_hbm, o_hbm)

    return gather_rows(ys3, idx).reshape(n * ROW_TILE, LANES)


EXPERT_SUBROWS = 128
W_SLOTS = 3
X_SLOTS = 4
Y_SLOTS = 3


def _experts_kernel(blk_e_ref, first_ref, slot_ref, next_e_ref, next2_e_ref, valid_ref, n_act_ref, xs_hbm, wg_hbm,
                    wu_hbm, wd_hbm, ys_hbm, x_buf, y_buf, wg_buf, wu_buf, wd_buf, wg_bf, wu_bf, wd_bf,
                    sems, x_sems, y_sems):
    i = pl.program_id(0)
    n_act = n_act_ref[0]
    bf16 = jnp.bfloat16
    blk = EXPERT_ROWS * ROW_TILE

    def x_copy(j):
        n = pl.multiple_of((valid_ref[j] + 7) // 8 * 8, 8)
        src = xs_hbm.at[pl.ds(pl.multiple_of(j * EXPERT_ROWS, EXPERT_ROWS), n)]
        return pltpu.make_async_copy(src, x_buf.at[j % X_SLOTS, pl.ds(0, n)], x_sems.at[j % X_SLOTS])

    def y_copy(j):
        n = pl.multiple_of(valid_ref[j] * ROW_TILE, ROW_TILE)
        dst = ys_hbm.at[pl.ds(pl.multiple_of(j * blk, blk), n)]
        return pltpu.make_async_copy(y_buf.at[j % Y_SLOTS, pl.ds(0, n)], dst, y_sems.at[j % Y_SLOTS])

    @pl.when(i == 0)
    def _prime():
        x_buf[...] = jnp.zeros_like(x_buf)
        for j in range(X_SLOTS - 1):
            @pl.when(j < n_act)
            def _():
                x_copy(j).start()

    @pl.when(i + (X_SLOTS - 1) < n_act)
    def _prefetch():
        x_copy(i + (X_SLOTS - 1)).start()

    def weight_copies(e, slot):
        return (pltpu.make_async_copy(wg_hbm.at[e], wg_buf.at[slot], sems.at[slot]),
                pltpu.make_async_copy(wu_hbm.at[e], wu_buf.at[slot], sems.at[slot]),
                pltpu.make_async_copy(wd_hbm.at[e], wd_buf.at[slot], sems.at[slot]))

    @pl.when((i < n_act_ref[0]) & (first_ref[i] == 1))
    def _new_expert():
        slot = slot_ref[i]

        @pl.when(i == 0)
        def _():
            for c in weight_copies(blk_e_ref[0], 0):
                c.start()

            @pl.when(next_e_ref[0] >= 0)
            def _():
                for c in weight_copies(next_e_ref[0], 1):
                    c.start()

        for c in weight_copies(blk_e_ref[i], slot):
            c.wait()

        @pl.when(next2_e_ref[i] >= 0)
        def _():
            for c in weight_copies(next2_e_ref[i], (slot + 2) % W_SLOTS):
                c.start()

        wg_bf[...] = wg_buf[slot].astype(bf16)
        wu_bf[...] = wu_buf[slot].astype(bf16)
        wd_bf[...] = wd_buf[slot].astype(bf16)

    @pl.when(i < n_act)
    def _compute():
        x_copy(i).wait()

        @pl.when(i >= Y_SLOTS)
        def _():
            y_copy(i - Y_SLOTS).wait()

        sub = EXPERT_SUBROWS
        n_parts = (valid_ref[i] + sub - 1) // sub

        def run(parts):
            xs_ = [_unpack_bf16_pairs(x_buf[i % X_SLOTS, pl.ds(part * sub, sub), :]) for part in range(parts)]
            gs = [_dot(x, wg_bf[...]) for x in xs_]
            us = [_dot(x, wu_bf[...]) for x in xs_]
            acts = [(_silu(g) * u).astype(bf16) for g, u in zip(gs, us)]
            for part in range(parts):
                _store_rows(y_buf, _dot(acts[part], wd_bf[...]), lead=(i % Y_SLOTS,), first_row=part * sub)

        for parts in range(1, EXPERT_ROWS // sub + 1):
            pl.when(n_parts == parts)(functools.partial(run, parts))
        y_copy(i).start()

    @pl.when(i == n_act - 1)
    def _drain():
        for d in range(Y_SLOTS):
            @pl.when(i - d >= 0)
            def _():
                y_copy(i - d).wait()


def _experts(xs, blk_e, n_act, row_start, cnt, w_gate, w_up, w_down):
    D = D_MODEL
    BM = EXPERT_ROWS
    F = EXPERT_DIM
    n_blocks = xs.shape[0] // BM
    blk_in_expert = jnp.arange(n_blocks, dtype=jnp.int32) - row_start[blk_e] // BM
    valid = jnp.clip(cnt[blk_e] - blk_in_expert * BM, 0, BM).astype(jnp.int32)
    ids = jnp.arange(n_blocks, dtype=jnp.int32)
    active = ids < n_act[0]
    first = active & ((ids == 0) | (blk_e != jnp.roll(blk_e, 1)))
    slot = ((jnp.cumsum(first.astype(jnp.int32)) - 1) % W_SLOTS).astype(jnp.int32)
    first_pos = jnp.where(first, ids, n_blocks)
    later_first = lax.cummin(jnp.concatenate([first_pos[1:], jnp.full((1,), n_blocks, jnp.int32)]), reverse=True)
    later_first2 = jnp.where(later_first < n_blocks, later_first[jnp.minimum(later_first, n_blocks - 1)], n_blocks)
    expert_at = lambda b: jnp.where(b < n_blocks, blk_e[jnp.minimum(b, n_blocks - 1)], -1).astype(jnp.int32)
    next_e, next2_e = expert_at(later_first), expert_at(later_first2)

    grid_spec = pltpu.PrefetchScalarGridSpec(
        num_scalar_prefetch=7,
        grid=(n_blocks,),
        in_specs=[pl.BlockSpec(memory_space=pl.ANY)] * 4,
        out_specs=pl.BlockSpec(memory_space=pl.ANY),
        scratch_shapes=[
            pltpu.VMEM((X_SLOTS, BM, D // 2), jnp.uint32),
            pltpu.VMEM((Y_SLOTS, BM * ROW_TILE, LANES), jnp.float32),
            pltpu.VMEM((W_SLOTS, D, F), jnp.float32), pltpu.VMEM((W_SLOTS, D, F), jnp.float32),
            pltpu.VMEM((W_SLOTS, F, D), jnp.float32),
            pltpu.VMEM((D, F), jnp.bfloat16), pltpu.VMEM((D, F), jnp.bfloat16),
            pltpu.VMEM((F, D), jnp.bfloat16),
            pltpu.SemaphoreType.DMA((W_SLOTS,)), pltpu.SemaphoreType.DMA((X_SLOTS,)),
            pltpu.SemaphoreType.DMA((Y_SLOTS,)),
        ],
    )
    return pl.pallas_call(
        _experts_kernel,
        grid_spec=grid_spec,
        out_shape=jax.ShapeDtypeStruct((xs.shape[0] * ROW_TILE, LANES), jnp.float32),
        compiler_params=pltpu.CompilerParams(
            dimension_semantics=("arbitrary",), vmem_limit_bytes=VMEM_LIMIT_BYTES),
    )(blk_e, first.astype(jnp.int32), slot, next_e, next2_e, valid, n_act, xs, w_gate, w_up, w_down)


def _finish_kernel(part_ref, w_ref, *rest):
    slabs = rest[:SC_COMBINE_SLOTS]
    g_ref, b_ref, out_ref = rest[SC_COMBINE_SLOTS:]
    R = part_ref.shape[0]
    w = w_ref[...]
    acc = part_ref[...]
    for k in range(SC_COMBINE_SLOTS):
        acc = acc + _load_rows(slabs[k], R) * w[:, k:k + 1]
    out_ref[...] = _layer_norm(acc, g_ref[...], b_ref[...])


def _finish(partial, top_w, gathered, ln_g, ln_b):
    T, D = partial.shape
    R = FINISH_ROWS
    rows = pl.BlockSpec((R, D), lambda i: (i, 0))
    vec = pl.BlockSpec((1, D), lambda i: (0, 0))
    slab = lambda k: pl.BlockSpec((R * ROW_TILE, LANES), lambda i: (k * (T // R) + i, 0))
    return pl.pallas_call(
        _finish_kernel,
        grid=(T // R,),
        in_specs=[rows, pl.BlockSpec((R, TOP_K), lambda i: (i, 0))]
        + [slab(k) for k in range(SC_COMBINE_SLOTS)] + [vec, vec],
        out_specs=rows,
        out_shape=jax.ShapeDtypeStruct((T, D), jnp.float32),
        compiler_params=pltpu.CompilerParams(
            dimension_semantics=("arbitrary",), vmem_limit_bytes=VMEM_LIMIT_BYTES),
    )(partial, top_w, *([gathered] * SC_COMBINE_SLOTS), ln_g.reshape(1, D), ln_b.reshape(1, D))


SC_COMBINE_SLOTS = 5


def _combine_kernel(*refs):
    n_tc = TOP_K - SC_COMBINE_SLOTS
    pos_refs, pos_next_refs = refs[:n_tc], refs[n_tc:2 * n_tc]
    h_ref, w_ref, ys_ref, wsg_ref, wsu_ref, wsd_ref, out_ref, buf_ref, sems = refs[2 * n_tc:]
    R = h_ref.shape[0]
    i = pl.program_id(0)
    slot = i % 2

    def gather(p_refs, s):
        def issue(t, carry):
            for j in range(n_tc):
                pltpu.make_async_copy(ys_ref.at[_row_tile(p_refs[j][t])], buf_ref.at[s, j, _row_tile(t)],
                                      sems.at[s]).start(priority=j % 2)
            return carry

        lax.fori_loop(0, R, issue, 0)

    @pl.when(i == 0)
    def _():
        gather(pos_refs, 0)

    @pl.when(i + 1 < pl.num_programs(0))
    def _():
        gather(pos_next_refs, 1 - slot)

    h = h_ref[...]
    hb = h.astype(jnp.bfloat16)
    act = (_silu(_dot(hb, wsg_ref[...])) * _dot(hb, wsu_ref[...])).astype(jnp.bfloat16)
    ffn = _dot(act, wsd_ref[...])
    pltpu.make_async_copy(buf_ref.at[slot], buf_ref.at[slot], sems.at[slot]).wait()
    w = w_ref[...]
    for k in range(SC_COMBINE_SLOTS, TOP_K):
        ffn = ffn + _load_rows(buf_ref, R, lead=(slot, k - SC_COMBINE_SLOTS)) * w[:, k:k + 1]
    out_ref[...] = DEEPNORM_ALPHA * h + ffn


def _combine(h2, pos_kt, top_w, ys, ws_gate, ws_up, ws_down):
    T, D = h2.shape
    R = COMBINE_ROWS
    F = SHARED_DIM
    bf16 = jnp.bfloat16
    n_steps = T // R
    slots = range(SC_COMBINE_SLOTS, TOP_K)
    const = lambda shape: pl.BlockSpec(shape, lambda i: (0,) * len(shape))
    pos_now = [pl.BlockSpec((R,), lambda i, k=k: (k * n_steps + i,), memory_space=pltpu.SMEM) for k in slots]
    pos_next = [pl.BlockSpec((R,), lambda i, k=k: (k * n_steps + jnp.minimum(i + 1, n_steps - 1),),
                             memory_space=pltpu.SMEM) for k in slots]
    pos_flat = pos_kt.reshape(TOP_K * T)
    return pl.pallas_call(
        _combine_kernel,
        grid=(n_steps,),
        in_specs=pos_now + pos_next + [
            pl.BlockSpec((R, D), lambda i: (i, 0)),
            pl.BlockSpec((R, TOP_K), lambda i: (i, 0)),
            pl.BlockSpec(memory_space=pl.ANY),
            const((D, F)), const((D, F)), const((F, D)),
        ],
        out_specs=pl.BlockSpec((R, D), lambda i: (i, 0)),
        scratch_shapes=[pltpu.VMEM((2, TOP_K - SC_COMBINE_SLOTS, R * ROW_TILE, LANES), jnp.float32),
                        pltpu.SemaphoreType.DMA((2,))],
        out_shape=jax.ShapeDtypeStruct((T, D), jnp.float32),
        compiler_params=pltpu.CompilerParams(
            dimension_semantics=("arbitrary",), vmem_limit_bytes=VMEM_LIMIT_BYTES),
    )(*([pos_flat] * (2 * len(slots))), h2, top_w, ys, ws_gate.astype(bf16), ws_up.astype(bf16),
      ws_down.astype(bf16))


def _moe(h2, h2_packed, w_router, router_bias, w_gate, w_up, w_down, ws_gate, ws_up, ws_down, ln_g, ln_b):
    T = h2.shape[0]
    E = N_EXPERTS
    BM = EXPERT_ROWS
    e_idx, top_w, rank, counts = _router(h2, w_router, router_bias)
    cnt = counts[:, 0].astype(jnp.int32)
    nblk = (cnt + BM - 1) // BM
    blk_end = jnp.cumsum(nblk)
    row_start = ((blk_end - nblk) * BM).astype(jnp.int32)
    n_blocks = T * TOP_K // BM + E
    n_act = blk_end[-1:].astype(jnp.int32)
    blk_ids = jnp.minimum(jnp.arange(n_blocks, dtype=jnp.int32), n_act[0] - 1)
    blk_e = jnp.minimum(jnp.sum(blk_end[None, :] <= blk_ids[:, None], axis=1), E - 1).astype(jnp.int32)
    pos_kt = _positions(e_idx, rank, row_start).reshape(TOP_K, T)
    xs = _dispatch_sc(h2_packed, pos_kt, n_blocks * BM)
    ys = _experts(xs, blk_e, n_act, row_start, cnt, w_gate, w_up, w_down)
    gathered = _gather_rows_sc(ys, pos_kt[:SC_COMBINE_SLOTS])
    w_tk = top_w.T
    partial = _combine(h2, pos_kt, w_tk, ys, ws_gate, ws_up, ws_down)
    return _finish(partial, w_tk, gathered, ln_g, ln_b)


def kernel(x, ln_in_g, ln_in_b, w_in, w_out, rel_bias, attn_sinks, ln_mix_g, ln_mix_b, w_router,
           router_bias, w_gate, w_up, w_down, ws_gate, ws_up, ws_down, ln_ffn_g, ln_ffn_b):
    B, S, D = x.shape
    h, h_packed = _mixer(x, ln_in_g, ln_in_b, w_in[0], w_out[0], rel_bias, attn_sinks[0], ln_mix_g[0],
                         ln_mix_b[0])
    out = _moe(h, h_packed, w_router[0], router_bias[0], w_gate[0], w_up[0], w_down[0],
               ws_gate[0], ws_up[0], ws_down[0], ln_ffn_g[0], ln_ffn_b[0])
    return out.reshape(B, S, D)
```

```python
import functools
import math

import jax
import jax.numpy as jnp
from jax import lax
from jax.experimental import pallas as pl
from jax.experimental.pallas import tpu as pltpu
from jax.experimental.pallas import tpu_sc as plsc

D_MODEL = 1024
DEPTH = 1
RET_HEADS = 4
RET_QK_DIM = 64
RET_V_DIM = 128
RET_CHUNK = 128
RET_WIDTH = RET_HEADS * RET_V_DIM
ROPE_BASE = 10000.0
SWA_HEADS = 8
SWA_KV_HEADS = 2
SWA_GROUP = SWA_HEADS // SWA_KV_HEADS
SWA_HEAD_DIM = 64
SWA_WINDOW = 128
SWA_WIDTH = SWA_HEADS * SWA_HEAD_DIM
MIX_WIDTH = RET_WIDTH + SWA_WIDTH
RQK = RET_HEADS * RET_QK_DIM
SKV = SWA_KV_HEADS * SWA_HEAD_DIM
IN_SIZES = (RQK, RQK, RET_WIDTH, RET_WIDTH, SWA_WIDTH, SKV, SKV)
IN_OFFS = tuple(sum(IN_SIZES[:i]) for i in range(len(IN_SIZES)))
IN_WIDTH = sum(IN_SIZES)
REL_BUCKETS = 32
REL_MAX_DIST = 128
N_EXPERTS = 256
TOP_K = 8
N_GROUPS = 8
GROUP_SIZE = N_EXPERTS // N_GROUPS
TOPK_GROUPS = 4
EXPERT_DIM = 256
SHARED_DIM = 256
ROUTED_SCALE = 2.5
LN_EPS = 1e-5
GN_EPS = 1e-6
DEEPNORM_ALPHA = (2 * DEPTH) ** 0.25
MASK_VALUE = -1e30

VMEM_LIMIT_BYTES = 56 * 1024 * 1024

MIX_ROWS = 512
ROUTE_ROWS = 256
EXPERT_ROWS = 512
COMBINE_ROWS = 512
POSITION_ROWS = 64
FINISH_ROWS = 512


def _layer_norm(x, g, b):
    mu = jnp.mean(x, axis=-1, keepdims=True)
    xc = x - mu
    var = jnp.mean(xc * xc, axis=-1, keepdims=True)
    return xc * lax.rsqrt(var + LN_EPS) * g + b


def _dot(a, b):
    return jnp.dot(a, b, preferred_element_type=jnp.float32)


def _dot_nt(a, b):
    return lax.dot_general(a, b, (((1,), (1,)), ((), ())), preferred_element_type=jnp.float32)


def _dot_tn(a, b):
    return lax.dot_general(a, b, (((0,), (0,)), ((), ())), preferred_element_type=jnp.float32)


def _silu(x):
    return x * (1.0 / (1.0 + jnp.exp(-x)))


LANES = 128
ROW_TILE = D_MODEL // LANES


def _load_rows(ref, n_rows, lead=()):
    return jnp.concatenate([ref[lead + (pl.ds(s, n_rows, stride=ROW_TILE), slice(None))]
                            for s in range(ROW_TILE)], axis=1)


def _store_rows(ref, val, lead=(), first_row=0):
    n_rows = val.shape[0]
    for s in range(ROW_TILE):
        dst = pl.ds(first_row * ROW_TILE + s, n_rows, stride=ROW_TILE)
        ref[lead + (dst, slice(None))] = val[:, s * LANES:(s + 1) * LANES]


def _row_tile(r):
    return pl.ds(pl.multiple_of(r * ROW_TILE, ROW_TILE), ROW_TILE)


def _pack_bf16_pairs(x):
    m = x.shape[1] // 2
    bits = lax.bitcast_convert_type(x.astype(jnp.bfloat16).astype(jnp.float32), jnp.uint32)
    return (bits[:, :m] >> 16) | (bits[:, m:] & jnp.uint32(0xFFFF0000))


def _unpack_bf16_pairs(p):
    lo = lax.bitcast_convert_type(p << 16, jnp.float32)
    hi = lax.bitcast_convert_type(p & jnp.uint32(0xFFFF0000), jnp.float32)
    return jnp.concatenate([lo, hi], axis=1).astype(jnp.bfloat16)


def _swap_halves(x):
    n = x.shape[-1]
    half = RET_QK_DIM // 2
    lane = lax.broadcasted_iota(jnp.int32, x.shape, 1)
    from_right = pltpu.roll(x, n - half, axis=1)
    from_left = pltpu.roll(x, half, axis=1)
    return jnp.where((lane % RET_QK_DIM) < half, from_right, from_left)


def _mixer_kernel(rel_bias_ref, x_ref, g_in_ref, b_in_ref, w_in_ref, w_out_ref, rot_ref, decay_ref,
                  zeta_ref, xi_ref, cdecay_ref, bucket_ref, sink_ref, g_mix_ref, b_mix_ref,
                  h2_ref, h2p_ref, state_ref, kprev_ref, vprev_ref, bias_ref):
    b_id = pl.program_id(0)
    c_id = pl.program_id(1)
    W = SWA_WINDOW

    @pl.when((b_id == 0) & (c_id == 0))
    def _build_bias():
        bucket = bucket_ref[...]
        for h in range(SWA_HEADS):
            acc = jnp.full((2 * W, W), MASK_VALUE, jnp.float32)
            for b in range(REL_BUCKETS):
                acc = jnp.where(bucket == b, rel_bias_ref[b, h], acc)
            kh, g = divmod(h, SWA_GROUP)
            bias_ref[kh, :, g * W:(g + 1) * W] = acc

    @pl.when(c_id == 0)
    def _reset():
        state_ref[...] = jnp.zeros_like(state_ref)
        kprev_ref[...] = jnp.zeros_like(kprev_ref)
        vprev_ref[...] = jnp.zeros_like(vprev_ref)

    h = _layer_norm(x_ref[...], g_in_ref[...], b_in_ref[...])
    proj = _dot(h.astype(jnp.bfloat16), w_in_ref[...])

    o_q, o_k, o_v, o_g, o_sq, o_sk, o_sv = IN_OFFS
    cos_t = rot_ref[:, :RQK]
    sin_t = rot_ref[:, RQK:]
    q_all = proj[:, o_q:o_q + RQK]
    k_all = proj[:, o_k:o_k + RQK]
    q_rot = q_all * cos_t + _swap_halves(q_all) * sin_t
    k_rot = (k_all * cos_t + _swap_halves(k_all) * sin_t) * (RET_QK_DIM ** -0.5)

    n_sub = x_ref.shape[0] // RET_CHUNK
    states = [state_ref[hh] for hh in range(RET_HEADS)]
    k_prev = kprev_ref[...]
    v_prev = vprev_ref[...]
    bf16 = jnp.bfloat16
    heads = range(RET_HEADS)
    ret_pieces, swa_pieces = [], []
    swa_jobs = []
    for s in range(n_sub):
        rows = slice(s * RET_CHUNK, (s + 1) * RET_CHUNK)
        k_cur = proj[rows, o_sk:o_sk + SKV].astype(bf16)
        v_cur = proj[rows, o_sv:o_sv + SKV].astype(bf16)
        for kh in range(SWA_KV_HEADS):
            kv = slice(kh * SWA_HEAD_DIM, (kh + 1) * SWA_HEAD_DIM)
            q4 = jnp.concatenate(
                [proj[rows, o_sq + (kh * SWA_GROUP + g) * SWA_HEAD_DIM:
                      o_sq + (kh * SWA_GROUP + g + 1) * SWA_HEAD_DIM] for g in range(SWA_GROUP)],
                axis=0) * (SWA_HEAD_DIM ** -0.5)
            kcat = jnp.concatenate([k_prev[:, kv], k_cur[:, kv]], axis=0)
            vcat = jnp.concatenate([v_prev[:, kv], v_cur[:, kv]], axis=0)
            swa_jobs.append((s, kh, q4.astype(bf16), kcat, vcat))
        k_prev, v_prev = k_cur, v_cur
    logits_all = [_dot_nt(kcat, q4) + bias_ref[kh] for (s, kh, q4, kcat, vcat) in swa_jobs]
    probs_all = []
    for (s, kh, q4, kcat, vcat), logits in zip(swa_jobs, logits_all):
        if s == 0:
            key = lax.broadcasted_iota(jnp.int32, logits.shape, 0)
            logits = logits + jnp.where((key < W) & (c_id == 0), MASK_VALUE, 0.0)
        sink = sink_ref[kh]
        m = jnp.maximum(jnp.max(logits, axis=0, keepdims=True), sink)
        p = jnp.exp(logits - m)
        den = jnp.sum(p, axis=0, keepdims=True) + jnp.exp(sink - m)
        probs_all.append((p / den).astype(bf16))
    o4_all = [_dot_tn(job[4], probs) for job, probs in zip(swa_jobs, probs_all)]
    for s in range(n_sub):
        swa_pieces.append([o4[:, g * W:(g + 1) * W].T.astype(bf16)
                           for job, o4 in zip(swa_jobs, o4_all) if job[0] == s for g in range(SWA_GROUP)])
    for s in range(n_sub):
        rows = slice(s * RET_CHUNK, (s + 1) * RET_CHUNK)
        qk = [slice(hh * RET_QK_DIM, (hh + 1) * RET_QK_DIM) for hh in heads]
        q = [q_rot[rows, qk[hh]].astype(bf16) for hh in heads]
        k32 = [k_rot[rows, qk[hh]] for hh in heads]
        v = [proj[rows, o_v + hh * RET_V_DIM:o_v + (hh + 1) * RET_V_DIM].astype(bf16) for hh in heads]
        scores = [_dot_nt(q[hh], k32[hh].astype(bf16)) * decay_ref[hh] for hh in heads]
        inter = [_dot(q[hh], states[hh].astype(bf16)) * xi_ref[hh] for hh in heads]
        kv_new = [_dot_tn((k32[hh] * zeta_ref[hh]).astype(bf16), v[hh]) for hh in heads]
        intra = [_dot(scores[hh].astype(bf16), v[hh]) for hh in heads]
        states = [states[hh] * cdecay_ref[hh] + kv_new[hh] for hh in heads]
        pieces = []
        for hh in heads:
            ret = intra[hh] + inter[hh]
            mu = jnp.mean(ret, axis=-1, keepdims=True)
            rc = ret - mu
            var = jnp.mean(rc * rc, axis=-1, keepdims=True)
            normed = rc * lax.rsqrt(var + GN_EPS)
            gate = proj[rows, o_g + hh * RET_V_DIM:o_g + (hh + 1) * RET_V_DIM]
            pieces.append((_silu(gate) * normed).astype(bf16))
        ret_pieces.append(pieces)
    cat_rows = [jnp.concatenate(ret_pieces[s] + swa_pieces[s], axis=1) for s in range(n_sub)]
    for hh in range(RET_HEADS):
        state_ref[hh] = states[hh]
    kprev_ref[...] = k_prev
    vprev_ref[...] = v_prev

    mix = _dot(jnp.concatenate(cat_rows, axis=0), w_out_ref[...])
    h2 = _layer_norm(DEEPNORM_ALPHA * h + mix, g_mix_ref[...], b_mix_ref[...])
    h2_ref[...] = h2
    h2p_ref[...] = _pack_bf16_pairs(h2)


def _t5_bucket(dist):
    n = jnp.maximum(dist, 0)
    max_exact = REL_BUCKETS // 2
    ratio = jnp.log(jnp.maximum(n, 1).astype(jnp.float32) / max_exact) / math.log(REL_MAX_DIST / max_exact)
    large = jnp.minimum(max_exact + (ratio * (REL_BUCKETS - max_exact)).astype(jnp.int32), REL_BUCKETS - 1)
    return jnp.where(n < max_exact, n, large)


def _mixer(x, ln_in_g, ln_in_b, w_in, w_out, rel_bias, sinks, ln_mix_g, ln_mix_b):
    B, S, D = x.shape
    R = MIX_ROWS
    C = RET_CHUNK
    W = SWA_WINDOW
    f32 = jnp.float32
    half = RET_QK_DIM // 2
    inv = ROPE_BASE ** (-jnp.arange(half, dtype=f32) / half)
    ang = jnp.arange(S, dtype=f32)[:, None] * inv[None, :]
    cos, sin = jnp.cos(ang), jnp.sin(ang)
    cos_t = jnp.tile(jnp.concatenate([cos, cos], axis=-1), (1, RET_HEADS))
    sin_t = jnp.tile(jnp.concatenate([-sin, sin], axis=-1), (1, RET_HEADS))
    rot = jnp.concatenate([cos_t, sin_t], axis=-1)
    log_gamma = jnp.log(1.0 - 2.0 ** (-5.0 - jnp.arange(RET_HEADS, dtype=f32)))
    idx = jnp.arange(C, dtype=f32)
    diff = idx[:, None] - idx[None, :]
    decay = jnp.where(diff[None] >= 0, jnp.exp(jnp.maximum(diff, 0.0)[None] * log_gamma[:, None, None]), 0.0)
    zeta = jnp.exp((C - 1.0 - idx)[None, :] * log_gamma[:, None])
    xi = jnp.exp((idx + 1.0)[None, :] * log_gamma[:, None])
    zeta_b = jnp.broadcast_to(zeta[:, :, None], (RET_HEADS, C, RET_QK_DIM))
    xi_b = jnp.broadcast_to(xi[:, :, None], (RET_HEADS, C, RET_V_DIM))
    cdecay = jnp.broadcast_to(jnp.exp(C * log_gamma)[:, None, None], (RET_HEADS, RET_QK_DIM, RET_V_DIM))
    i = jnp.arange(W)
    j = jnp.arange(2 * W)
    dist = i[:, None] + W - j[None, :]
    bucket = jnp.where((dist >= 0) & (dist < W), _t5_bucket(dist), -1).astype(jnp.int32).T
    sink_row = jnp.repeat(sinks.astype(f32), W).reshape(SWA_KV_HEADS, 1, SWA_GROUP * W)

    const = lambda shape: pl.BlockSpec(shape, lambda b, c, *_: (0,) * len(shape))
    grid_spec = pltpu.PrefetchScalarGridSpec(
        num_scalar_prefetch=1,
        grid=(B, S // R),
        in_specs=[
            pl.BlockSpec((None, R, D), lambda b, c, *_: (b, c, 0)),
            const((1, D)), const((1, D)),
            const((D, IN_WIDTH)), const((MIX_WIDTH, D)),
            pl.BlockSpec((R, 2 * RQK), lambda b, c, *_: (c, 0)),
            const((RET_HEADS, C, C)), const((RET_HEADS, C, RET_QK_DIM)), const((RET_HEADS, C, RET_V_DIM)),
            const((RET_HEADS, RET_QK_DIM, RET_V_DIM)),
            const((2 * W, W)), const((SWA_KV_HEADS, 1, SWA_GROUP * W)),
            const((1, D)), const((1, D)),
        ],
        out_specs=[pl.BlockSpec((R, D), lambda b, c, *_: (b * (S // R) + c, 0)),
                   pl.BlockSpec((R, D // 2), lambda b, c, *_: (b * (S // R) + c, 0))],
        scratch_shapes=[
            pltpu.VMEM((RET_HEADS, RET_QK_DIM, RET_V_DIM), f32),
            pltpu.VMEM((W, SKV), jnp.bfloat16),
            pltpu.VMEM((W, SKV), jnp.bfloat16),
            pltpu.VMEM((SWA_KV_HEADS, 2 * W, SWA_GROUP * W), f32),
        ],
    )
    return pl.pallas_call(
        _mixer_kernel,
        grid_spec=grid_spec,
        out_shape=[jax.ShapeDtypeStruct((B * S, D), f32), jax.ShapeDtypeStruct((B * S, D // 2), jnp.uint32)],
        compiler_params=pltpu.CompilerParams(
            dimension_semantics=("arbitrary", "arbitrary"), vmem_limit_bytes=VMEM_LIMIT_BYTES),
    )(rel_bias.astype(f32), x, ln_in_g.reshape(1, D), ln_in_b.reshape(1, D),
      w_in.astype(jnp.bfloat16), w_out.astype(jnp.bfloat16), rot, decay, zeta_b, xi_b, cdecay,
      bucket, sink_row, ln_mix_g.reshape(1, D), ln_mix_b.reshape(1, D))


def _router_kernel(h_ref, wr_ref, rb_ref, e_ref, w_ref, rk_ref, cnt_ref, run_ref):
    f32 = jnp.float32
    R = h_ref.shape[0]
    E = N_EXPERTS
    neg = -jnp.inf

    @pl.when(pl.program_id(0) == 0)
    def _init():
        run_ref[...] = jnp.zeros_like(run_ref)

    logits = _dot_nt(wr_ref[...], h_ref[...].astype(jnp.bfloat16))
    scores = 1.0 / (1.0 + jnp.exp(-logits))
    choice = scores + rb_ref[...]
    eid = lax.broadcasted_iota(jnp.int32, (E, R), 0)

    def first_argmax(vals, ids, none):
        m = jnp.max(vals, axis=0, keepdims=True)
        idx = jnp.min(jnp.where(vals == m, ids, none), axis=0, keepdims=True)
        return m, idx

    gid = lax.broadcasted_iota(jnp.int32, (GROUP_SIZE, R), 0)
    groups, gscore = [], []
    for g in range(N_GROUPS):
        vals = choice[g * GROUP_SIZE:(g + 1) * GROUP_SIZE]
        m1, i1 = first_argmax(vals, gid, GROUP_SIZE)
        m2 = jnp.max(jnp.where(gid == i1, neg, vals), axis=0, keepdims=True)
        groups.append(vals)
        gscore.append(m1 + m2)
    kept = []
    for g in range(N_GROUPS):
        beaten = jnp.zeros((1, R), f32)
        for g2 in range(N_GROUPS):
            if g2 == g:
                continue
            ahead = (gscore[g2] > gscore[g]) | (gscore[g2] == gscore[g]) if g2 < g else gscore[g2] > gscore[g]
            beaten = beaten + jnp.where(ahead, 1.0, 0.0)
        kept.append(jnp.where(beaten < TOPK_GROUPS, groups[g], neg))
    masked = jnp.concatenate(kept, axis=0)

    idxs, wts = [], []
    picked = jnp.zeros((E, R), f32)
    for _ in range(TOP_K):
        _, idx = first_argmax(masked, eid, E)
        hit = eid == idx
        idxs.append(idx)
        wts.append(jnp.sum(jnp.where(hit, scores, 0.0), axis=0, keepdims=True))
        masked = jnp.where(hit, neg, masked)
        picked = jnp.where(hit, 1.0, picked)
    wsum = wts[0]
    for k in range(1, TOP_K):
        wsum = wsum + wts[k]

    row = lax.broadcasted_iota(jnp.int32, (R, R), 0)
    col = lax.broadcasted_iota(jnp.int32, (R, R), 1)
    earlier = jnp.where(row < col, 1.0, 0.0).astype(jnp.bfloat16)
    picked_bf = picked.astype(jnp.bfloat16)
    run = run_ref[...]
    before = _dot(picked_bf, earlier) + jnp.concatenate([run] * (R // LANES), axis=1)
    sub_k = lax.broadcasted_iota(jnp.int32, (TOP_K, R), 0)
    e_out = jnp.zeros((TOP_K, R), jnp.int32)
    w_out = jnp.zeros((TOP_K, R), f32)
    rk_out = jnp.zeros((TOP_K, R), jnp.int32)
    for k in range(TOP_K):
        rank_k = jnp.sum(jnp.where(eid == idxs[k], before, 0.0), axis=0, keepdims=True)
        e_out = jnp.where(sub_k == k, idxs[k], e_out)
        w_out = jnp.where(sub_k == k, wts[k] / wsum * ROUTED_SCALE, w_out)
        rk_out = jnp.where(sub_k == k, rank_k.astype(jnp.int32), rk_out)
    e_ref[...] = e_out
    w_ref[...] = w_out
    rk_ref[...] = rk_out
    run_ref[...] = run + _dot(picked_bf, jnp.ones((R, LANES), jnp.bfloat16))
    cnt_ref[...] = run_ref[...]


def _router(h2, w_router, router_bias):
    T, D = h2.shape
    R = ROUTE_ROWS
    E = N_EXPERTS
    return pl.pallas_call(
        _router_kernel,
        grid=(T // R,),
        in_specs=[
            pl.BlockSpec((R, D), lambda i: (i, 0)),
            pl.BlockSpec((E, D), lambda i: (0, 0)),
            pl.BlockSpec((E, R), lambda i: (0, 0)),
        ],
        out_specs=[
            pl.BlockSpec((TOP_K, R), lambda i: (0, i)),
            pl.BlockSpec((TOP_K, R), lambda i: (0, i)),
            pl.BlockSpec((TOP_K, R), lambda i: (0, i)),
            pl.BlockSpec((E, LANES), lambda i: (0, 0)),
        ],
        out_shape=[
            jax.ShapeDtypeStruct((TOP_K, T), jnp.int32),
            jax.ShapeDtypeStruct((TOP_K, T), jnp.float32),
            jax.ShapeDtypeStruct((TOP_K, T), jnp.int32),
            jax.ShapeDtypeStruct((E, LANES), jnp.float32),
        ],
        scratch_shapes=[pltpu.VMEM((E, LANES), jnp.float32)],
        compiler_params=pltpu.CompilerParams(
            dimension_semantics=("arbitrary",), vmem_limit_bytes=VMEM_LIMIT_BYTES),
    )(h2, w_router.T.astype(jnp.bfloat16), jnp.broadcast_to(router_bias.astype(jnp.float32)[:, None], (E, R)))


def _positions_kernel(row_start_ref, e_ref, rk_ref, pos_ref):
    e = e_ref[...]
    rk = rk_ref[...]

    def per_expert(i, pos):
        return jnp.where(e == i, rk + row_start_ref[i], pos)

    pos_ref[...] = lax.fori_loop(0, N_EXPERTS, per_expert, jnp.zeros_like(rk), unroll=8)


def _positions(e_idx, rank, row_start):
    n = e_idx.size
    shape = (n // LANES, LANES)
    block = pl.BlockSpec((POSITION_ROWS, LANES), lambda i, *_: (i, 0))
    grid_spec = pltpu.PrefetchScalarGridSpec(
        num_scalar_prefetch=1,
        grid=(shape[0] // POSITION_ROWS,),
        in_specs=[block, block],
        out_specs=block,
    )
    pos = pl.pallas_call(
        _positions_kernel,
        grid_spec=grid_spec,
        out_shape=jax.ShapeDtypeStruct(shape, jnp.int32),
    )(row_start, e_idx.reshape(shape), rank.reshape(shape))
    return pos.reshape(n)


SC_WINDOW = 32
SC_DISPATCH_WINDOW = 64


def _dispatch_sc(h2, pos_kt, n_rows):
    T, D = h2.shape
    W = SC_DISPATCH_WINDOW
    idx = _window_indices(pos_kt, W)
    idx_rows = TOP_K * W // LANES
    mesh = plsc.VectorSubcoreMesh(core_axis_name="core", subcore_axis_name="subcore")

    @pl.kernel(out_type=jax.ShapeDtypeStruct((n_rows, D), h2.dtype), mesh=mesh, scratch_types=[])
    def scatter_rows(x_hbm, i_hbm, o_hbm):
        def body(x_vmem, i_vmem):
            for k in range(TOP_K):
                r, q = divmod(k * W, LANES)
                pltpu.sync_copy(x_vmem, o_hbm.at[i_vmem.at[r, pl.ds(q, W)]])

        pltpu.emit_pipeline(
            body,
            grid=(T // W,),
            in_specs=[pl.BlockSpec((W, D), lambda i: (i, 0)),
                      pl.BlockSpec((idx_rows, LANES), lambda i: (i, 0))],
            out_specs=[],
            core_axis_name=("core", "subcore"),
            dimension_semantics=(pltpu.PARALLEL,),
        )(x_hbm, i_hbm)

    return scatter_rows(h2, idx)


def _window_indices(pos_kt, window):
    K, T = pos_kt.shape
    return pos_kt.reshape(K, T // window, window).transpose(1, 0, 2).reshape(T * K // LANES, LANES)


def _gather_rows_sc(ys, pos_kt):
    n = pos_kt.size
    W = SC_WINDOW
    ys3 = ys.reshape(ys.shape[0] // ROW_TILE, ROW_TILE, LANES)
    idx = jnp.pad(pos_kt.reshape(n // W, W), ((0, 0), (0, LANES - W)))
    mesh = plsc.VectorSubcoreMesh(core_axis_name="core", subcore_axis_name="subcore")

    @pl.kernel(out_type=jax.ShapeDtypeStruct((n, ROW_TILE, LANES), ys.dtype), mesh=mesh, scratch_types=[])
    def gather_rows(y_hbm, i_hbm, o_hbm):
        def body(i_vmem, o_vmem):
            pltpu.sync_copy(y_hbm.at[i_vmem.at[0, pl.ds(0, W)]], o_vmem)

        pltpu.emit_pipeline(
            body,
            grid=(n // W,),
            in_specs=[pl.BlockSpec((1, LANES), lambda i: (i, 0))],
            out_specs=[pl.BlockSpec((W, ROW_TILE, LANES), lambda i: (i, 0, 0))],
            core_axis_name=("core", "subcore"),
            dimension_semantics=(pltpu.PARALLEL,),
        )(i_hbm, o_hbm)

    return gather_rows(ys3, idx).reshape(n * ROW_TILE, LANES)


EXPERT_SUBROWS = 128
W_SLOTS = 4
X_SLOTS = 4
Y_SLOTS = 3


def _experts_kernel(blk_e_ref, first_ref, slot_ref, *refs):
    ahead_refs = refs[:W_SLOTS - 1]
    (valid_ref, n_act_ref, xs_hbm, wg_hbm, wu_hbm, wd_hbm, ys_hbm, x_buf, y_buf, wg_buf, wu_buf, wd_buf,
     wg_bf, wu_bf, wd_bf, sems, x_sems, y_sems) = refs[W_SLOTS - 1:]
    i = pl.program_id(0)
    n_act = n_act_ref[0]
    bf16 = jnp.bfloat16
    blk = EXPERT_ROWS * ROW_TILE

    def x_copy(j):
        n = pl.multiple_of((valid_ref[j] + 7) // 8 * 8, 8)
        src = xs_hbm.at[pl.ds(pl.multiple_of(j * EXPERT_ROWS, EXPERT_ROWS), n)]
        return pltpu.make_async_copy(src, x_buf.at[j % X_SLOTS, pl.ds(0, n)], x_sems.at[j % X_SLOTS])

    def y_copy(j):
        n = pl.multiple_of(valid_ref[j] * ROW_TILE, ROW_TILE)
        dst = ys_hbm.at[pl.ds(pl.multiple_of(j * blk, blk), n)]
        return pltpu.make_async_copy(y_buf.at[j % Y_SLOTS, pl.ds(0, n)], dst, y_sems.at[j % Y_SLOTS])

    @pl.when(i == 0)
    def _prime():
        x_buf[...] = jnp.zeros_like(x_buf)
        for j in range(X_SLOTS - 1):
            @pl.when(j < n_act)
            def _():
                x_copy(j).start()

    @pl.when(i + (X_SLOTS - 1) < n_act)
    def _prefetch():
        x_copy(i + (X_SLOTS - 1)).start()

    def weight_copies(e, slot):
        return (pltpu.make_async_copy(wg_hbm.at[e], wg_buf.at[slot], sems.at[slot]),
                pltpu.make_async_copy(wu_hbm.at[e], wu_buf.at[slot], sems.at[slot]),
                pltpu.make_async_copy(wd_hbm.at[e], wd_buf.at[slot], sems.at[slot]))

    @pl.when((i < n_act_ref[0]) & (first_ref[i] == 1))
    def _new_expert():
        slot = slot_ref[i]

        @pl.when(i == 0)
        def _():
            for c in weight_copies(blk_e_ref[0], 0):
                c.start()
            for j in range(1, W_SLOTS - 1):
                @pl.when(ahead_refs[j - 1][0] >= 0)
                def _():
                    for c in weight_copies(ahead_refs[j - 1][0], j):
                        c.start()

        for c in weight_copies(blk_e_ref[i], slot):
            c.wait()

        @pl.when(ahead_refs[-1][i] >= 0)
        def _():
            for c in weight_copies(ahead_refs[-1][i], (slot + W_SLOTS - 1) % W_SLOTS):
                c.start()

        wg_bf[...] = wg_buf[slot].astype(bf16)
        wu_bf[...] = wu_buf[slot].astype(bf16)
        wd_bf[...] = wd_buf[slot].astype(bf16)

    @pl.when(i < n_act)
    def _compute():
        x_copy(i).wait()

        @pl.when(i >= Y_SLOTS)
        def _():
            y_copy(i - Y_SLOTS).wait()

        sub = EXPERT_SUBROWS
        n_parts = (valid_ref[i] + sub - 1) // sub

        def run(parts):
            xs_ = [_unpack_bf16_pairs(x_buf[i % X_SLOTS, pl.ds(part * sub, sub), :]) for part in range(parts)]
            gs = [_dot(x, wg_bf[...]) for x in xs_]
            us = [_dot(x, wu_bf[...]) for x in xs_]
            acts = [(_silu(g) * u).astype(bf16) for g, u in zip(gs, us)]
            for part in range(parts):
                _store_rows(y_buf, _dot(acts[part], wd_bf[...]), lead=(i % Y_SLOTS,), first_row=part * sub)

        for parts in range(1, EXPERT_ROWS // sub + 1):
            pl.when(n_parts == parts)(functools.partial(run, parts))
        y_copy(i).start()

    @pl.when(i == n_act - 1)
    def _drain():
        for d in range(Y_SLOTS):
            @pl.when(i - d >= 0)
            def _():
                y_copy(i - d).wait()


def _experts(xs, blk_e, n_act, row_start, cnt, w_gate, w_up, w_down):
    D = D_MODEL
    BM = EXPERT_ROWS
    F = EXPERT_DIM
    n_blocks = xs.shape[0] // BM
    blk_in_expert = jnp.arange(n_blocks, dtype=jnp.int32) - row_start[blk_e] // BM
    valid = jnp.clip(cnt[blk_e] - blk_in_expert * BM, 0, BM).astype(jnp.int32)
    ids = jnp.arange(n_blocks, dtype=jnp.int32)
    active = ids < n_act[0]
    first = active & ((ids == 0) | (blk_e != jnp.roll(blk_e, 1)))
    slot = ((jnp.cumsum(first.astype(jnp.int32)) - 1) % W_SLOTS).astype(jnp.int32)
    first_pos = jnp.where(first, ids, n_blocks)
    later_first = lax.cummin(jnp.concatenate([first_pos[1:], jnp.full((1,), n_blocks, jnp.int32)]), reverse=True)
    expert_at = lambda b: jnp.where(b < n_blocks, blk_e[jnp.minimum(b, n_blocks - 1)], -1).astype(jnp.int32)
    ahead, later = [], later_first
    for _ in range(W_SLOTS - 1):
        ahead.append(expert_at(later))
        later = jnp.where(later < n_blocks, later_first[jnp.minimum(later, n_blocks - 1)], n_blocks)

    grid_spec = pltpu.PrefetchScalarGridSpec(
        num_scalar_prefetch=5 + len(ahead),
        grid=(n_blocks,),
        in_specs=[pl.BlockSpec(memory_space=pl.ANY)] * 4,
        out_specs=pl.BlockSpec(memory_space=pl.ANY),
        scratch_shapes=[
            pltpu.VMEM((X_SLOTS, BM, D // 2), jnp.uint32),
            pltpu.VMEM((Y_SLOTS, BM * ROW_TILE, LANES), jnp.float32),
            pltpu.VMEM((W_SLOTS, D, F), jnp.float32), pltpu.VMEM((W_SLOTS, D, F), jnp.float32),
            pltpu.VMEM((W_SLOTS, F, D), jnp.float32),
            pltpu.VMEM((D, F), jnp.bfloat16), pltpu.VMEM((D, F), jnp.bfloat16),
            pltpu.VMEM((F, D), jnp.bfloat16),
            pltpu.SemaphoreType.DMA((W_SLOTS,)), pltpu.SemaphoreType.DMA((X_SLOTS,)),
            pltpu.SemaphoreType.DMA((Y_SLOTS,)),
        ],
    )
    return pl.pallas_call(
        _experts_kernel,
        grid_spec=grid_spec,
        out_shape=jax.ShapeDtypeStruct((xs.shape[0] * ROW_TILE, LANES), jnp.float32),
        compiler_params=pltpu.CompilerParams(
            dimension_semantics=("arbitrary",), vmem_limit_bytes=VMEM_LIMIT_BYTES),
    )(blk_e, first.astype(jnp.int32), slot, *ahead, valid, n_act, xs, w_gate, w_up, w_down)


def _finish_kernel(part_ref, w_ref, *rest):
    slabs = rest[:SC_COMBINE_SLOTS]
    g_ref, b_ref, out_ref = rest[SC_COMBINE_SLOTS:]
    R = part_ref.shape[0]
    w = w_ref[...]
    acc = part_ref[...]
    for k in range(SC_COMBINE_SLOTS):
        acc = acc + _load_rows(slabs[k], R) * w[:, k:k + 1]
    out_ref[...] = _layer_norm(acc, g_ref[...], b_ref[...])


def _finish(partial, top_w, gathered, ln_g, ln_b):
    T, D = partial.shape
    R = FINISH_ROWS
    rows = pl.BlockSpec((R, D), lambda i: (i, 0))
    vec = pl.BlockSpec((1, D), lambda i: (0, 0))
    slab = lambda k: pl.BlockSpec((R * ROW_TILE, LANES), lambda i: (k * (T // R) + i, 0))
    return pl.pallas_call(
        _finish_kernel,
        grid=(T // R,),
        in_specs=[rows, pl.BlockSpec((R, TOP_K), lambda i: (i, 0))]
        + [slab(k) for k in range(SC_COMBINE_SLOTS)] + [vec, vec],
        out_specs=rows,
        out_shape=jax.ShapeDtypeStruct((T, D), jnp.float32),
        compiler_params=pltpu.CompilerParams(
            dimension_semantics=("arbitrary",), vmem_limit_bytes=VMEM_LIMIT_BYTES),
    )(partial, top_w, *([gathered] * SC_COMBINE_SLOTS), ln_g.reshape(1, D), ln_b.reshape(1, D))


SC_COMBINE_SLOTS = 5


def _combine_kernel(*refs):
    n_tc = TOP_K - SC_COMBINE_SLOTS
    pos_refs, pos_next_refs = refs[:n_tc], refs[n_tc:2 * n_tc]
    h_ref, w_ref, ys_ref, wsg_ref, wsu_ref, wsd_ref, out_ref, buf_ref, sems = refs[2 * n_tc:]
    R = h_ref.shape[0]
    i = pl.program_id(0)
    slot = i % 2

    def gather(p_refs, s):
        def issue(t, carry):
            for j in range(n_tc):
                pltpu.make_async_copy(ys_ref.at[_row_tile(p_refs[j][t])], buf_ref.at[s, j, _row_tile(t)],
                                      sems.at[s]).start(priority=j % 2)
            return carry

        lax.fori_loop(0, R, issue, 0)

    @pl.when(i == 0)
    def _():
        gather(pos_refs, 0)

    @pl.when(i + 1 < pl.num_programs(0))
    def _():
        gather(pos_next_refs, 1 - slot)

    h = h_ref[...]
    hb = h.astype(jnp.bfloat16)
    act = (_silu(_dot(hb, wsg_ref[...])) * _dot(hb, wsu_ref[...])).astype(jnp.bfloat16)
    ffn = _dot(act, wsd_ref[...])
    pltpu.make_async_copy(buf_ref.at[slot], buf_ref.at[slot], sems.at[slot]).wait()
    w = w_ref[...]
    for k in range(SC_COMBINE_SLOTS, TOP_K):
        ffn = ffn + _load_rows(buf_ref, R, lead=(slot, k - SC_COMBINE_SLOTS)) * w[:, k:k + 1]
    out_ref[...] = DEEPNORM_ALPHA * h + ffn


def _combine(h2, pos_kt, top_w, ys, ws_gate, ws_up, ws_down):
    T, D = h2.shape
    R = COMBINE_ROWS
    F = SHARED_DIM
    bf16 = jnp.bfloat16
    n_steps = T // R
    slots = range(SC_COMBINE_SLOTS, TOP_K)
    const = lambda shape: pl.BlockSpec(shape, lambda i: (0,) * len(shape))
    pos_now = [pl.BlockSpec((R,), lambda i, k=k: (k * n_steps + i,), memory_space=pltpu.SMEM) for k in slots]
    pos_next = [pl.BlockSpec((R,), lambda i, k=k: (k * n_steps + jnp.minimum(i + 1, n_steps - 1),),
                             memory_space=pltpu.SMEM) for k in slots]
    pos_flat = pos_kt.reshape(TOP_K * T)
    return pl.pallas_call(
        _combine_kernel,
        grid=(n_steps,),
        in_specs=pos_now + pos_next + [
            pl.BlockSpec((R, D), lambda i: (i, 0)),
            pl.BlockSpec((R, TOP_K), lambda i: (i, 0)),
            pl.BlockSpec(memory_space=pl.ANY),
            const((D, F)), const((D, F)), const((F, D)),
        ],
        out_specs=pl.BlockSpec((R, D), lambda i: (i, 0)),
        scratch_shapes=[pltpu.VMEM((2, TOP_K - SC_COMBINE_SLOTS, R * ROW_TILE, LANES), jnp.float32),
                        pltpu.SemaphoreType.DMA((2,))],
        out_shape=jax.ShapeDtypeStruct((T, D), jnp.float32),
        compiler_params=pltpu.CompilerParams(
            dimension_semantics=("arbitrary",), vmem_limit_bytes=VMEM_LIMIT_BYTES),
    )(*([pos_flat] * (2 * len(slots))), h2, top_w, ys, ws_gate.astype(bf16), ws_up.astype(bf16),
      ws_down.astype(bf16))


def _moe(h2, h2_packed, w_router, router_bias, w_gate, w_up, w_down, ws_gate, ws_up, ws_down, ln_g, ln_b):
    T = h2.shape[0]
    E = N_EXPERTS
    BM = EXPERT_ROWS
    e_idx, top_w, rank, counts = _router(h2, w_router, router_bias)
    cnt = counts[:, 0].astype(jnp.int32)
    nblk = (cnt + BM - 1) // BM
    blk_end = jnp.cumsum(nblk)
    row_start = ((blk_end - nblk) * BM).astype(jnp.int32)
    n_blocks = T * TOP_K // BM + E
    n_act = blk_end[-1:].astype(jnp.int32)
    blk_ids = jnp.minimum(jnp.arange(n_blocks, dtype=jnp.int32), n_act[0] - 1)
    blk_e = jnp.minimum(jnp.sum(blk_end[None, :] <= blk_ids[:, None], axis=1), E - 1).astype(jnp.int32)
    pos_kt = _positions(e_idx, rank, row_start).reshape(TOP_K, T)
    xs = _dispatch_sc(h2_packed, pos_kt, n_blocks * BM)
    ys = _experts(xs, blk_e, n_act, row_start, cnt, w_gate, w_up, w_down)
    gathered = _gather_rows_sc(ys, pos_kt[:SC_COMBINE_SLOTS])
    w_tk = top_w.T
    partial = _combine(h2, pos_kt, w_tk, ys, ws_gate, ws_up, ws_down)
    return _finish(partial, w_tk, gathered, ln_g, ln_b)


def kernel(x, ln_in_g, ln_in_b, w_in, w_out, rel_bias, attn_sinks, ln_mix_g, ln_mix_b, w_router,
           router_bias, w_gate, w_up, w_down, ws_gate, ws_up, ws_down, ln_ffn_g, ln_ffn_b):
    B, S, D = x.shape
    h, h_packed = _mixer(x, ln_in_g, ln_in_b, w_in[0], w_out[0], rel_bias, attn_sinks[0], ln_mix_g[0],
                         ln_mix_b[0])
    out = _moe(h, h_packed, w_router[0], router_bias[0], w_gate[0], w_up[0], w_down[0],
               ws_gate[0], ws_up[0], ws_down[0], ln_ffn_g[0], ln_ffn_b[0])
    return out.reshape(B, S, D)
```
